```python
import math
import jax, jax.numpy as jnp
from jax import lax
import numpy as np


D_MODEL = 1024
BATCH = 8
SEQ = 2048
DEPTH = 1
DEC_BATCH = 128
DEC_SEQ = 1
PAST_LEN = 16384
PAGE_SIZE = 128

N_META = 16
CHUNK = 128
CONV_W = 4
EPS = 1e-6
D_A = D_MODEL
H_A = 8
DH_A = D_A // H_A
D_B = D_MODEL
HD_B = 64
H_B = D_B // HD_B
N_STATE = 128
G_B = 2
CONV_B = D_B + 2 * G_B * N_STATE
D_MIX = D_A + D_B
D_IN = 2 * D_A + 2 * H_A + D_B + CONV_B + H_B
N_EGROUPS = 4
N_EPG = 4
N_EXPERTS = N_EGROUPS * N_EPG
TOP_K = 2
D_FF = 1024

kernel_name = 'hybrid_mlstm_ssd_hmoe_step'


def rmsnorm(x, g):
    xf = x.astype(jnp.float32)
    y = xf * lax.rsqrt(jnp.mean(xf * xf, -1, keepdims=True) + EPS)
    return (y * g.astype(jnp.float32)).astype(x.dtype)


def causal_conv(x, prev, w, b):
    xp = jnp.concatenate([prev.astype(x.dtype), x], axis=1)
    L = x.shape[1]
    y = sum(w[j] * xp[:, j:j + L] for j in range(CONV_W)) + b
    return y, xp[:, -(CONV_W - 1):]


def chunk_for(L):
    return CHUNK if L % CHUNK == 0 else L


def to_chunks(a, chunk):
    Bn, L = a.shape[0], a.shape[1]
    a = a.reshape((Bn, L // chunk, chunk) + a.shape[2:])
    return jnp.swapaxes(jnp.moveaxis(a, 1, 0), 2, 3)


def from_chunks(a):
    nc, Bn, H, c, D = a.shape
    return jnp.swapaxes(jnp.moveaxis(a, 0, 1), 2, 3).reshape(Bn, nc * c, H, D)


def mlstm_chunk_scan(q, k, v, ig, fg, state, chunk):
    xs = tuple(to_chunks(a, chunk) for a in (q, k, v, ig, fg))
    causal = jnp.tril(jnp.ones((chunk, chunk), bool))

    def step(carry, inp):
        C0, n0, m0 = carry
        qc, kc, vc, ic, fc = inp
        bcum = jnp.cumsum(jax.nn.log_sigmoid(fc), -1)
        dmat = jnp.where(causal, bcum[..., :, None] - bcum[..., None, :] + ic[..., None, :], -jnp.inf)
        m_inter = bcum + m0[..., None]
        m = jnp.maximum(m_inter, dmat.max(-1))
        w_inter = jnp.exp(m_inter - m)
        s = jnp.einsum('bhtd,bhsd->bhts', qc, kc) * jnp.exp(dmat - m[..., None])
        num = jnp.einsum('bhts,bhse->bhte', s, vc) + w_inter[..., None] * jnp.einsum('bhtd,bhde->bhte', qc, C0)
        den = s.sum(-1) + w_inter * jnp.einsum('bhtd,bhd->bht', qc, n0)
        h = num / jnp.maximum(jnp.abs(den), jnp.exp(-m))[..., None]
        mL = m[..., -1]
        dec = jnp.exp(bcum[..., -1] + m0 - mL)
        ws = jnp.exp(bcum[..., -1:] - bcum + ic - mL[..., None])
        kw = kc * ws[..., None]
        C1 = dec[..., None, None] * C0 + jnp.einsum('bhsd,bhse->bhde', kw, vc)
        n1 = dec[..., None] * n0 + kw.sum(2)
        return (C1, n1, mL), h

    state, hs = lax.scan(step, state, xs)
    return from_chunks(hs), state


def ssd_chunk_scan(x, dt, A, bm, cm, S, chunk):
    xs = tuple(to_chunks(a, chunk) for a in (x, dt, bm, cm))
    causal = jnp.tril(jnp.ones((chunk, chunk), bool))

    def step(S0, inp):
        xc, dtc, bc, cc = inp
        a = jnp.cumsum(dtc * A[:, None], -1)
        decay = jnp.exp(jnp.where(causal, a[..., :, None] - a[..., None, :], -jnp.inf))
        scores = jnp.einsum('bhtn,bhsn->bhts', cc, bc) * decay * dtc[..., None, :]
        y = jnp.einsum('bhts,bhsp->bhtp', scores, xc) + jnp.exp(a)[..., None] * jnp.einsum('bhtn,bhpn->bhtp', cc, S0)
        aL = a[..., -1]
        w = jnp.exp(aL[..., None] - a) * dtc
        S1 = jnp.exp(aL)[..., None, None] * S0 + jnp.einsum('bhsp,bhsn->bhpn', xc * w[..., None], bc)
        return S1, y

    S, ys = lax.scan(step, S, xs)
    return from_chunks(ys), S


def run_segments(scan_fn, arrays, state, segments):
    outs, start = [], 0
    for length, chunk in segments:
        y, state = scan_fn(*[a[:, start:start + length] for a in arrays], state, chunk)
        outs.append(y)
        start += length
    return jnp.concatenate(outs, axis=1), state


def mixer_block(hn, conv_a_prev, conv_b_prev, m_state, s_state, segments, p):
    f32 = jnp.float32
    Bn, L, _ = hn.shape
    u = hn @ p['w_in']
    offs = list(np.cumsum([D_A, D_A, H_A, H_A, D_B, CONV_B]))
    xa, za, ia, fa, zb, xbc, dtr = jnp.split(u, offs, axis=-1)
    xa_c, conv_a_new = causal_conv(xa, conv_a_prev, p['conv_a_w'], p['conv_a_b'])
    xc = jax.nn.silu(xa_c).reshape(Bn, L, H_A, DH_A)
    xa_h = xa.reshape(Bn, L, H_A, DH_A)
    q = jnp.einsum('blhd,hde->blhe', xc, p['w_q']).astype(f32)
    k = (jnp.einsum('blhd,hde->blhe', xc, p['w_k']) * DH_A ** -0.5).astype(f32)
    v = jnp.einsum('blhd,hde->blhe', xa_h, p['w_v']).astype(f32)
    ig = (ia + p['b_i']).astype(f32)
    fg = (fa + p['b_f']).astype(f32)
    h_a, m_state = run_segments(mlstm_chunk_scan, (q, k, v, ig, fg), m_state, segments)
    h_a = h_a * lax.rsqrt(jnp.mean(h_a * h_a, -1, keepdims=True) + EPS) * p['norm_a'].astype(f32)
    h_a = h_a.reshape(Bn, L, D_A) * jax.nn.sigmoid(za.astype(f32))
    xbc_c, conv_b_new = causal_conv(xbc, conv_b_prev, p['conv_b_w'], p['conv_b_b'])
    xbc_c = jax.nn.silu(xbc_c)
    xs, bm, cm = jnp.split(xbc_c, [D_B, D_B + G_B * N_STATE], axis=-1)
    xs = xs.reshape(Bn, L, H_B, HD_B).astype(f32)
    rep = H_B // G_B
    bm = jnp.repeat(bm.reshape(Bn, L, G_B, N_STATE), rep, axis=2).astype(f32)
    cm = jnp.repeat(cm.reshape(Bn, L, G_B, N_STATE), rep, axis=2).astype(f32)
    dt = jax.nn.softplus((dtr + p['dt_bias']).astype(f32))
    A = -jnp.exp(p['a_log'].astype(f32))
    y, s_state = run_segments(lambda x_, d_, b_, c_, s_, ch: ssd_chunk_scan(x_, d_, A, b_, c_, s_, ch),
                              (xs, dt, bm, cm), s_state, segments)
    y = y + p['d_skip'].astype(f32)[:, None] * xs
    y = y.reshape(Bn, L, D_B) * jax.nn.silu(zb.astype(f32))
    yg = y.reshape(Bn, L, G_B, D_B // G_B)
    yg = yg * lax.rsqrt(jnp.mean(yg * yg, -1, keepdims=True) + EPS)
    y = yg.reshape(Bn, L, D_B) * p['norm_b'].astype(f32)
    out = jnp.concatenate([h_a, y], axis=-1).astype(hn.dtype) @ p['w_out']
    C1, n1, m1 = m_state
    return out, (C1, n1, m1, conv_a_new, s_state, conv_b_new)


def hier_moe(x, p):
    xf = x.astype(jnp.float32)
    p1 = jax.nn.softmax(xf @ p['w_r1'].astype(jnp.float32) + p['b_r1'], axis=-1)
    gp, gidx = lax.top_k(p1, 1)
    l2 = jnp.einsum('bsd,dge->bsge', xf, p['w_r2'].astype(jnp.float32)) + p['b_r2']
    l2 = jnp.take_along_axis(l2, gidx[..., None], axis=2)[:, :, 0]
    v2, i2 = lax.top_k(l2, TOP_K)
    w2 = jax.nn.softmax(v2, axis=-1)
    eid = gidx * N_EPG + i2
    gate = gp * jnp.sum(w2[..., None] * jax.nn.one_hot(eid, N_EXPERTS, dtype=jnp.float32), axis=-2)
    gate = gate.astype(x.dtype)

    def per_row(args):
        xr, gr = args
        hg = jnp.einsum('sd,edf->sef', xr, p['w_gate'])
        hu = jnp.einsum('sd,edf->sef', xr, p['w_up'])
        return jnp.einsum('sef,efd->sd', jax.nn.silu(hg) * hu * gr[..., None], p['w_down'])

    return lax.map(per_row, (x, gate))


def trunk(h, init_states, segments, drop_meta, layers, norm_final):
    outs = []
    for l in range(DEPTH):
        p = layers[l]
        conv_a, conv_b, mst, sst = init_states[l]
        mix, new = mixer_block(rmsnorm(h, p['norm_mix']), conv_a, conv_b, mst, sst, segments, p)
        h = h + mix
        if drop_meta and l == DEPTH - 1:
            h = h[:, N_META:]
        h = h + hier_moe(rmsnorm(h, p['norm_ffn']), p)
        outs.append(new)
    y = rmsnorm(h, norm_final)
    return y, [jnp.stack(s) for s in zip(*outs)]


def setup_inputs(seed: int = 0) -> dict:
    key = jax.random.key(seed)
    ks = iter(jax.random.split(key, 48))
    f32 = jnp.float32

    def nrm(shape, scale):
        return jax.random.normal(next(ks), shape, f32) * scale

    x_prompt = nrm((BATCH, SEQ, D_MODEL), 1.0)
    x_sample = nrm((DEC_BATCH, DEC_SEQ, D_MODEL), 1.0)
    state_mlstm_C = nrm((DEPTH, DEC_BATCH, H_A, DH_A, DH_A), 0.1)
    state_mlstm_n = nrm((DEPTH, DEC_BATCH, H_A, DH_A), 0.5)
    state_mlstm_m = nrm((DEPTH, DEC_BATCH, H_A), 1.0)
    state_mlstm_conv = nrm((DEPTH, DEC_BATCH, CONV_W - 1, D_A), 1.0)
    state_ssm = nrm((DEPTH, DEC_BATCH, H_B, HD_B, N_STATE), 0.3)
    state_ssm_conv = nrm((DEPTH, DEC_BATCH, CONV_W - 1, CONV_B), 1.0)
    meta_tokens = nrm((N_META, D_MODEL), 1.0)
    norm_mix = 1.0 + nrm((DEPTH, D_MODEL), 0.02)
    w_in = nrm((DEPTH, D_MODEL, D_IN), D_MODEL ** -0.5)
    conv_a_w = nrm((DEPTH, CONV_W, D_A), 0.5)
    conv_a_b = nrm((DEPTH, D_A), 0.01)
    w_q = nrm((DEPTH, H_A, DH_A, DH_A), DH_A ** -0.5)
    w_k = nrm((DEPTH, H_A, DH_A, DH_A), DH_A ** -0.5)
    w_v = nrm((DEPTH, H_A, DH_A, DH_A), DH_A ** -0.5)
    b_i = nrm((DEPTH, H_A), 0.1)
    b_f = jnp.linspace(3.0, 6.0, H_A, dtype=f32)[None] + nrm((DEPTH, H_A), 0.1)
    norm_a = 1.0 + nrm((DEPTH, H_A, DH_A), 0.02)
    conv_b_w = nrm((DEPTH, CONV_W, CONV_B), 0.5)
    conv_b_b = nrm((DEPTH, CONV_B), 0.01)
    dt0 = jnp.exp(jax.random.uniform(next(ks), (DEPTH, H_B), f32, math.log(1e-3), math.log(1e-1)))
    dt_bias = dt0 + jnp.log(-jnp.expm1(-dt0))
    a_log = jnp.log(jax.random.uniform(next(ks), (DEPTH, H_B), f32, 1.0, 16.0))
    d_skip = 1.0 + nrm((DEPTH, H_B), 0.02)
    norm_b = 1.0 + nrm((DEPTH, D_B), 0.02)
    w_out = nrm((DEPTH, D_MIX, D_MODEL), D_MIX ** -0.5)
    norm_ffn = 1.0 + nrm((DEPTH, D_MODEL), 0.02)
    w_r1 = nrm((DEPTH, D_MODEL, N_EGROUPS), D_MODEL ** -0.5)
    b_r1 = nrm((DEPTH, N_EGROUPS), 0.01)
    w_r2 = nrm((DEPTH, D_MODEL, N_EGROUPS, N_EPG), D_MODEL ** -0.5)
    b_r2 = nrm((DEPTH, N_EGROUPS, N_EPG), 0.01)
    w_gate = nrm((DEPTH, N_EXPERTS, D_MODEL, D_FF), D_MODEL ** -0.5)
    w_up = nrm((DEPTH, N_EXPERTS, D_MODEL, D_FF), D_MODEL ** -0.5)
    w_down = nrm((DEPTH, N_EXPERTS, D_FF, D_MODEL), D_FF ** -0.5)
    norm_final = 1.0 + nrm((D_MODEL,), 0.02)
    return {'x_prompt': x_prompt, 'x_sample': x_sample,
            'state_mlstm_C': state_mlstm_C, 'state_mlstm_n': state_mlstm_n, 'state_mlstm_m': state_mlstm_m,
            'state_mlstm_conv': state_mlstm_conv, 'state_ssm': state_ssm, 'state_ssm_conv': state_ssm_conv,
            'meta_tokens': meta_tokens, 'norm_mix': norm_mix, 'w_in': w_in,
            'conv_a_w': conv_a_w, 'conv_a_b': conv_a_b, 'w_q': w_q, 'w_k': w_k, 'w_v': w_v,
            'b_i': b_i, 'b_f': b_f, 'norm_a': norm_a, 'conv_b_w': conv_b_w, 'conv_b_b': conv_b_b,
            'dt_bias': dt_bias, 'a_log': a_log, 'd_skip': d_skip, 'norm_b': norm_b, 'w_out': w_out,
            'norm_ffn': norm_ffn, 'w_r1': w_r1, 'b_r1': b_r1, 'w_r2': w_r2, 'b_r2': b_r2,
            'w_gate': w_gate, 'w_up': w_up, 'w_down': w_down, 'norm_final': norm_final}


def reference(x_prompt, x_sample, state_mlstm_C, state_mlstm_n, state_mlstm_m, state_mlstm_conv,
              state_ssm, state_ssm_conv, meta_tokens, norm_mix, w_in, conv_a_w, conv_a_b, w_q, w_k, w_v,
              b_i, b_f, norm_a, conv_b_w, conv_b_b, dt_bias, a_log, d_skip, norm_b, w_out, norm_ffn,
              w_r1, b_r1, w_r2, b_r2, w_gate, w_up, w_down, norm_final):
    f32 = jnp.float32
    layers = [dict(norm_mix=norm_mix[l], w_in=w_in[l], conv_a_w=conv_a_w[l], conv_a_b=conv_a_b[l],
                   w_q=w_q[l], w_k=w_k[l], w_v=w_v[l], b_i=b_i[l], b_f=b_f[l], norm_a=norm_a[l],
                   conv_b_w=conv_b_w[l], conv_b_b=conv_b_b[l], dt_bias=dt_bias[l], a_log=a_log[l],
                   d_skip=d_skip[l], norm_b=norm_b[l], w_out=w_out[l], norm_ffn=norm_ffn[l],
                   w_r1=w_r1[l], b_r1=b_r1[l], w_r2=w_r2[l], b_r2=b_r2[l],
                   w_gate=w_gate[l], w_up=w_up[l], w_down=w_down[l]) for l in range(DEPTH)]
    bp, sp = x_prompt.shape[0], x_prompt.shape[1]
    dt_x = x_prompt.dtype
    h_p = jnp.concatenate([jnp.broadcast_to(meta_tokens.astype(dt_x)[None], (bp, N_META, D_MODEL)), x_prompt], axis=1)
    init_p = [(jnp.zeros((bp, CONV_W - 1, D_A), dt_x), jnp.zeros((bp, CONV_W - 1, CONV_B), dt_x),
               (jnp.zeros((bp, H_A, DH_A, DH_A), f32), jnp.zeros((bp, H_A, DH_A), f32), jnp.zeros((bp, H_A), f32)),
               jnp.zeros((bp, H_B, HD_B, N_STATE), f32)) for _ in range(DEPTH)]
    seg_p = ((N_META, N_META), (sp, chunk_for(sp)))
    y_prompt, st_p = trunk(h_p, init_p, seg_p, True, layers, norm_final)
    ds = x_sample.shape[1]
    init_s = [(state_mlstm_conv[l], state_ssm_conv[l],
               (state_mlstm_C[l].astype(f32), state_mlstm_n[l].astype(f32), state_mlstm_m[l].astype(f32)),
               state_ssm[l].astype(f32)) for l in range(DEPTH)]
    seg_s = ((ds, chunk_for(ds)),)
    y_sample, st_s = trunk(x_sample, init_s, seg_s, False, layers, norm_final)
    p_C, p_n, p_m, p_conv_a, p_ssm, p_conv_b = st_p
    s_C, s_n, s_m, s_conv_a, s_ssm, s_conv_b = st_s
    return (y_prompt, y_sample, p_C, p_n, p_m, p_conv_a, p_ssm, p_conv_b,
            s_C, s_n, s_m, s_conv_a, s_ssm, s_conv_b)
```

```python
import functools
import math

import jax
import jax.numpy as jnp
from jax import lax
from jax.experimental import pallas as pl
from jax.experimental.pallas import tpu as pltpu

F32 = jnp.float32
BF16 = jnp.bfloat16

EPS = 1e-6
N_META = 16
CONV_W = 4
CHUNK = 128
H_A = 8
DH_A = 128
H_B = 16
HD_B = 64
N_STATE = 128
G_B = 2
N_EGROUPS = 4
N_EPG = 4
N_EXPERTS = 16
LANES = 128
SUBLANES = 8
CONV_HDR = SUBLANES
VMEM_LIMIT = 56 * 1024 * 1024

L_F = 0
L_DTA = 8
L_I = 24
L_DT = 32

NEG_INF = float("-inf")


def _dot(a, b):
    return jnp.dot(a, b, preferred_element_type=F32)


def _dot_nt(a, b):
    return lax.dot_general(a, b, (((1,), (1,)), ((), ())), preferred_element_type=F32)


def _dot_tn(a, b):
    return lax.dot_general(a, b, (((0,), (0,)), ((), ())), preferred_element_type=F32)


def _split3(x):
    hi = x.astype(BF16)
    r = x - hi.astype(F32)
    mid = r.astype(BF16)
    lo = (r - mid.astype(F32)).astype(BF16)
    return hi, mid, lo


def _silu(x):
    return x * jax.nn.sigmoid(x)


def _softplus_parts(x):
    t = jnp.log1p(jnp.exp(-jnp.abs(x)))
    return jnp.maximum(x, 0.0) + t, jnp.minimum(x, 0.0) - t


def _rms_scale(x):
    return lax.rsqrt(jnp.mean(x * x, axis=-1, keepdims=True) + EPS)


def _prompt_mixer_kernel(xmeta_ref, xp_ref, nmix_ref, wcat_ref, bsm_ref, alog_ref,
                         cwa_ref, cba_ref, cwb_ref, cbb_ref, wq_ref, wk_ref, wv_ref,
                         na_ref, nb_ref, dsk_ref, wout_ref,
                         hmid_ref, c_ref, n_ref, m_ref, conva_ref, s_ref, convb_ref,
                         xa_buf, xbc_buf, y_buf, merged):
    c = pl.program_id(1)
    T = CHUNK
    d_a = H_A * DH_A
    d_b = H_B * HD_B

    @pl.when(c == 0)
    def _init():
        c_ref[...] = jnp.zeros_like(c_ref)
        n_ref[...] = jnp.zeros_like(n_ref)
        m_ref[...] = jnp.zeros_like(m_ref)
        s_ref[...] = jnp.zeros_like(s_ref)
        xa_buf[0:CONV_HDR, :] = jnp.zeros((CONV_HDR, xa_buf.shape[1]), F32)
        xbc_buf[0:CONV_HDR, :] = jnp.zeros((CONV_HDR, xbc_buf.shape[1]), F32)

    x = jnp.where(c == 0, xmeta_ref[...], xp_ref[...])
    row = lax.broadcasted_iota(jnp.int32, (T, 1), 0)
    valid = jnp.logical_or(c > 0, row >= T - N_META)

    hn = (x * _rms_scale(x) * nmix_ref[...]).astype(BF16)

    lane = lax.broadcasted_iota(jnp.int32, (1, LANES), 1)
    lane_f = lane < L_DTA
    lane_dta = jnp.logical_and(lane >= L_DTA, lane < L_I)
    lane_i = jnp.logical_and(lane >= L_I, lane < L_DT)
    lane_dt = jnp.logical_and(lane >= L_DT, lane < L_DT + H_B)
    pre = _dot(hn, wcat_ref[:, 2 * d_a + d_b + (d_b + 2 * G_B * N_STATE):]) + bsm_ref[...]
    sp, lsig = _softplus_parts(pre)
    a_neg = jnp.where(lane_dta, -jnp.exp(alog_ref[...]), 0.0)
    to_cum = jnp.where(lane_f, lsig, jnp.where(lane_dta, sp * a_neg, 0.0))
    to_cum = jnp.where(valid, to_cum, 0.0)
    ri = lax.broadcasted_iota(jnp.int32, (T, T), 0)
    ci = lax.broadcasted_iota(jnp.int32, (T, T), 1)
    causal = ri >= ci
    tri = jnp.where(causal, 1.0, 0.0).astype(BF16)
    hi, mid, lo = _split3(to_cum)
    cum = _dot(tri, hi) + _dot(tri, mid) + _dot(tri, lo)
    extra = jnp.where(lane_i, jnp.where(valid, pre, NEG_INF),
                      jnp.where(lane_dt, jnp.where(valid, sp, 0.0), 0.0))
    gcol = cum + extra
    grow = gcol.T

    xa = _dot(hn, wcat_ref[:, 0:d_a])
    xa_buf[CONV_HDR:CONV_HDR + T, :] = xa
    xc = (cwa_ref[3:4, :] * xa + cwa_ref[2:3, :] * xa_buf[CONV_HDR - 1:CONV_HDR - 1 + T, :]
          + cwa_ref[1:2, :] * xa_buf[CONV_HDR - 2:CONV_HDR - 2 + T, :]
          + cwa_ref[0:1, :] * xa_buf[CONV_HDR - 3:CONV_HDR - 3 + T, :] + cba_ref[...])
    xa_buf[CONV_HDR - 3:CONV_HDR, :] = xa[T - 3:T, :]
    xc = _silu(xc).astype(BF16)
    xab = xa.astype(BF16)
    za = _dot(hn, wcat_ref[:, d_a:2 * d_a])
    m_all = m_ref[...]
    m_new = m_all
    for h in range(H_A):
        sl = slice(h * DH_A, (h + 1) * DH_A)
        q = _dot(xc[:, sl], wq_ref[h]).astype(BF16)
        k = _dot(xc[:, sl], wk_ref[h]) * (DH_A ** -0.5)
        v = _dot(xab[:, sl], wv_ref[h]).astype(BF16)
        b_col = gcol[:, L_F + h:L_F + h + 1]
        i_col = gcol[:, L_I + h:L_I + h + 1]
        b_row = grow[L_F + h:L_F + h + 1, :]
        i_row = grow[L_I + h:L_I + h + 1, :]
        m0 = m_all[:, h:h + 1]
        dmat = jnp.where(causal, b_col - (b_row - i_row), NEG_INF)
        m_inter = b_col + m0
        m = jnp.maximum(m_inter, jnp.max(dmat, axis=-1, keepdims=True))
        w_inter = jnp.exp(m_inter - m)
        s = _dot_nt(q, k.astype(BF16)) * jnp.exp(dmat - m)
        c0 = c_ref[h]
        n0 = n_ref[h:h + 1, :]
        num = _dot(s.astype(BF16), v) + w_inter * _dot(q, c0.astype(BF16))
        qf = q.astype(F32)
        den = jnp.sum(s, axis=-1, keepdims=True) + w_inter * jnp.sum(qf * n0, axis=-1, keepdims=True)
        hh = num / jnp.maximum(jnp.abs(den), jnp.exp(-m))
        m_last = m[T - 1:T, :]
        b_last = b_col[T - 1:T, :]
        dec = jnp.exp(b_last + m0 - m_last)
        ws = jnp.exp(b_last - b_col + i_col - m_last)
        kw = k * ws
        c_ref[h] = dec * c0 + _dot_tn(kw.astype(BF16), v)
        n_ref[h:h + 1, :] = dec * n0 + jnp.sum(kw, axis=0, keepdims=True)
        m_new = jnp.where(lane == h, m_last, m_new)
        hh = hh * _rms_scale(hh) * na_ref[:, sl]
        merged[:, sl] = (hh * jax.nn.sigmoid(za[:, sl])).astype(BF16)
    m_ref[...] = m_new

    off_xbc = 3 * d_a
    xbc = _dot(hn, wcat_ref[:, off_xbc:off_xbc + d_b + 2 * G_B * N_STATE])
    xbc_buf[CONV_HDR:CONV_HDR + T, :] = xbc
    xbc_c = (cwb_ref[3:4, :] * xbc + cwb_ref[2:3, :] * xbc_buf[CONV_HDR - 1:CONV_HDR - 1 + T, :]
             + cwb_ref[1:2, :] * xbc_buf[CONV_HDR - 2:CONV_HDR - 2 + T, :]
             + cwb_ref[0:1, :] * xbc_buf[CONV_HDR - 3:CONV_HDR - 3 + T, :] + cbb_ref[...])
    xbc_buf[CONV_HDR - 3:CONV_HDR, :] = xbc[T - 3:T, :]
    xbc_c = _silu(xbc_c)
    zb = _dot(hn, wcat_ref[:, 2 * d_a:2 * d_a + d_b])
    left = lane < HD_B
    top = lax.broadcasted_iota(jnp.int32, (LANES, 1), 0) < HD_B
    pairs_per_group = H_B // G_B // 2
    for g in range(G_B):
        bg = xbc_c[:, d_b + g * N_STATE:d_b + (g + 1) * N_STATE].astype(BF16)
        cg = xbc_c[:, d_b + (G_B + g) * N_STATE:d_b + (G_B + g + 1) * N_STATE].astype(BF16)
        cb = _dot_nt(cg, bg)
        for p in range(pairs_per_group):
            pi = g * pairs_per_group + p
            sl = slice(pi * LANES, (pi + 1) * LANES)
            xpair = xbc_c[:, sl]
            xpb = xpair.astype(BF16)
            ys, a_cols, w_cols, a_lasts = [], [], [], []
            for j in (2 * pi, 2 * pi + 1):
                a_col = gcol[:, L_DTA + j:L_DTA + j + 1]
                a_row = grow[L_DTA + j:L_DTA + j + 1, :]
                dt_col = gcol[:, L_DT + j:L_DT + j + 1]
                dt_row = grow[L_DT + j:L_DT + j + 1, :]
                decay = jnp.exp(jnp.where(causal, a_col - a_row, NEG_INF))
                scores = cb * decay * dt_row
                ys.append(_dot(scores.astype(BF16), xpb))
                a_last = a_col[T - 1:T, :]
                a_cols.append(a_col)
                a_lasts.append(a_last)
                w_cols.append(jnp.exp(a_last - a_col) * dt_col)
            s0 = s_ref[pi]
            y = jnp.where(left, ys[0], ys[1])
            ea = jnp.exp(jnp.where(left, a_cols[0], a_cols[1]))
            y = y + ea * _dot_nt(cg, s0.astype(BF16))
            xw = (xpair * jnp.where(left, w_cols[0], w_cols[1])).astype(BF16)
            ea_last = jnp.exp(jnp.where(top, a_lasts[0], a_lasts[1]))
            s_ref[pi] = ea_last * s0 + _dot_tn(xw, bg)
            y = y + dsk_ref[:, sl] * xpair
            y_buf[:, sl] = y * _silu(zb[:, sl])
    gw = d_b // G_B
    for g in range(G_B):
        yg = y_buf[:, g * gw:(g + 1) * gw]
        merged[:, d_a + g * gw:d_a + (g + 1) * gw] = (
            yg * _rms_scale(yg) * nb_ref[:, g * gw:(g + 1) * gw]).astype(BF16)

    @pl.when(c > 0)
    def _out():
        hmid_ref[...] = x + _dot(merged[...], wout_ref[...])

    @pl.when(c == 0)
    def _no_out():
        hmid_ref[...] = jnp.zeros_like(hmid_ref)

    conva_ref[...] = xa_buf[CONV_HDR + T - 3:CONV_HDR + T, :]
    convb_ref[...] = xbc_buf[CONV_HDR + T - 3:CONV_HDR + T, :]


def _const_spec(shape):
    nd = len(shape)
    return pl.BlockSpec(shape, lambda b, c, _nd=nd: (0,) * _nd)


def _prompt_mixer(x_prompt, xmeta, p, n_extra_rows):
    bsz, seq, d = x_prompt.shape
    assert n_extra_rows == CHUNK and seq % CHUNK == 0
    n_chunks = seq // CHUNK + 1
    cps = seq // CHUNK
    d_a = H_A * DH_A
    conv_b = H_B * HD_B + 2 * G_B * N_STATE
    consts = [p["nmix"], p["wcat"], p["bsm"], p["alog"], p["cwa"], p["cba"], p["cwb"], p["cbb"],
              p["wq"], p["wk"], p["wv"], p["na"], p["nb"], p["dsk"], p["wout"]]
    in_specs = [_const_spec(xmeta.shape),
                pl.BlockSpec((None, CHUNK, d), lambda b, c: (b, jnp.maximum(c - 1, 0), 0))]
    in_specs += [_const_spec(a.shape) for a in consts]
    out_shape = (
        jax.ShapeDtypeStruct((bsz * seq + n_extra_rows, d), F32),
        jax.ShapeDtypeStruct((bsz, H_A, DH_A, DH_A), F32),
        jax.ShapeDtypeStruct((bsz, H_A, DH_A), F32),
        jax.ShapeDtypeStruct((bsz, 1, LANES), F32),
        jax.ShapeDtypeStruct((bsz, CONV_W - 1, d_a), F32),
        jax.ShapeDtypeStruct((bsz, H_B // 2, 2 * HD_B, N_STATE), F32),
        jax.ShapeDtypeStruct((bsz, CONV_W - 1, conv_b), F32),
    )
    out_specs = (
        pl.BlockSpec((CHUNK, d), lambda b, c: (
            jnp.where(jnp.logical_and(b == 0, c == 0), bsz * cps, b * cps + jnp.maximum(c - 1, 0)), 0)),
        pl.BlockSpec((None, H_A, DH_A, DH_A), lambda b, c: (b, 0, 0, 0)),
        pl.BlockSpec((None, H_A, DH_A), lambda b, c: (b, 0, 0)),
        pl.BlockSpec((None, 1, LANES), lambda b, c: (b, 0, 0)),
        pl.BlockSpec((None, CONV_W - 1, d_a), lambda b, c: (b, 0, 0)),
        pl.BlockSpec((None, H_B // 2, 2 * HD_B, N_STATE), lambda b, c: (b, 0, 0, 0)),
        pl.BlockSpec((None, CONV_W - 1, conv_b), lambda b, c: (b, 0, 0)),
    )
    return pl.pallas_call(
        _prompt_mixer_kernel,
        out_shape=out_shape,
        grid=(bsz, n_chunks),
        in_specs=in_specs,
        out_specs=out_specs,
        scratch_shapes=[
            pltpu.VMEM((CONV_HDR + CHUNK, d_a), F32),
            pltpu.VMEM((CONV_HDR + CHUNK, conv_b), F32),
            pltpu.VMEM((CHUNK, H_B * HD_B), F32),
            pltpu.VMEM((CHUNK, d_a + H_B * HD_B), BF16),
        ],
        compiler_params=pltpu.CompilerParams(
            dimension_semantics=("arbitrary", "arbitrary"), vmem_limit_bytes=VMEM_LIMIT),
        name="prompt_mixer",
    )(xmeta, x_prompt, *consts)


def _prep_mixer_params(norm_mix, w_in, conv_a_w, conv_a_b, w_q, w_k, w_v, b_i, b_f, norm_a,
                       conv_b_w, conv_b_b, dt_bias, a_log, d_skip, norm_b, w_out):
    d_a = H_A * DH_A
    d_b = H_B * HD_B
    conv_b = d_b + 2 * G_B * N_STATE
    w = w_in[0]
    o = 0
    w_xa = w[:, o:o + d_a]; o += d_a
    w_za = w[:, o:o + d_a]; o += d_a
    w_i = w[:, o:o + H_A]; o += H_A
    w_f = w[:, o:o + H_A]; o += H_A
    w_zb = w[:, o:o + d_b]; o += d_b
    w_xbc = w[:, o:o + conv_b]; o += conv_b
    w_dt = w[:, o:o + H_B]
    pad = jnp.zeros((w.shape[0], LANES - (L_DT + H_B)), w.dtype)
    wcat = jnp.concatenate([w_xa, w_za, w_zb, w_xbc, w_f, w_dt, w_i, w_dt, pad], axis=1).astype(BF16)

    def lanes(parts):
        v = jnp.zeros((1, LANES), F32)
        for off, a in parts:
            v = v.at[0, off:off + a.shape[0]].set(a.astype(F32))
        return v

    return dict(
        nmix=norm_mix[0][None, :].astype(F32),
        wcat=wcat,
        bsm=lanes([(L_F, b_f[0]), (L_DTA, dt_bias[0]), (L_I, b_i[0]), (L_DT, dt_bias[0])]),
        alog=lanes([(L_DTA, a_log[0])]),
        cwa=conv_a_w[0].astype(F32), cba=conv_a_b[0][None, :].astype(F32),
        cwb=conv_b_w[0].astype(F32), cbb=conv_b_b[0][None, :].astype(F32),
        wq=w_q[0].astype(BF16), wk=w_k[0].astype(BF16), wv=w_v[0].astype(BF16),
        na=norm_a[0].reshape(1, d_a).astype(F32), nb=norm_b[0][None, :].astype(F32),
        dsk=jnp.repeat(d_skip[0].astype(F32), HD_B)[None, :],
        wout=w_out[0].astype(BF16),
    )


SAMPLE_BLOCK = 8


def _expand_lanes(vals, first_lane, n_heads, width):
    r = lax.broadcasted_iota(jnp.int32, (LANES, n_heads * width), 0) - first_lane
    c = lax.broadcasted_iota(jnp.int32, (LANES, n_heads * width), 1)
    sel = jnp.logical_and(c >= r * width, c < (r + 1) * width)
    e = jnp.where(sel, 1.0, 0.0).astype(BF16)
    hi, mid, lo = _split3(vals)
    return (_dot(hi, e) + _dot(mid, e)) + _dot(lo, e)


def _sample_pre_kernel(x_ref, nmix_ref, wcat_ref, bsm_ref, alog_ref, cwa_ref, cba_ref, cwb_ref, cbb_ref,
                       wq_ref, wk_ref, wv_ref, dsk_ref, conva_ref, convb_ref, n0_ref, m0_ref,
                       conva_out, convb_out, n1_out, m1_out, g_out, qt_out, kwt_out, xwt_out,
                       v_out, bc_out, a1_out, w1_out, den_out, y1_out, ea_out, zbs_out, zas_out):
    d_a = H_A * DH_A
    d_b = H_B * HD_B
    conv_b = d_b + 2 * G_B * N_STATE
    shift_i = LANES - (L_I - L_F)
    x = x_ref[...]
    hn = (x * _rms_scale(x) * nmix_ref[...]).astype(BF16)
    lane = lax.broadcasted_iota(jnp.int32, (1, LANES), 1)
    lane_f = lane < L_DTA
    lane_dta = jnp.logical_and(lane >= L_DTA, lane < L_I)
    pre = _dot(hn, wcat_ref[:, 2 * d_a + d_b + conv_b:]) + bsm_ref[...]
    sp, lsig = _softplus_parts(pre)
    a_neg = jnp.where(lane_dta, -jnp.exp(alog_ref[...]), 0.0)
    pre_al = pltpu.roll(pre, shift_i, axis=1)
    sp_al = pltpu.roll(sp, shift_i, axis=1)
    m_inter = lsig + m0_ref[...]
    m = jnp.maximum(m_inter, pre_al)
    w_inter = jnp.exp(m_inter - m)
    sfac = jnp.exp(pre_al - m)
    ea = jnp.exp(sp * a_neg)
    dt = sp_al

    xa = _dot(hn, wcat_ref[:, 0:d_a])
    xc = (cwa_ref[0:1, :] * conva_ref[:, 0:d_a] + cwa_ref[1:2, :] * conva_ref[:, d_a:2 * d_a]
          + cwa_ref[2:3, :] * conva_ref[:, 2 * d_a:3 * d_a] + cwa_ref[3:4, :] * xa + cba_ref[...])
    conva_out[:, 0:2 * d_a] = conva_ref[:, d_a:3 * d_a]
    conva_out[:, 2 * d_a:3 * d_a] = xa
    xc = _silu(xc).astype(BF16)
    xab = xa.astype(BF16)
    sf_e = _expand_lanes(sfac, L_F, H_A, DH_A)
    w_e = _expand_lanes(w_inter, L_F, H_A, DH_A)
    qk8 = jnp.zeros((x.shape[0], LANES), F32)
    qn8 = jnp.zeros((x.shape[0], LANES), F32)
    for h in range(H_A):
        sl = slice(h * DH_A, (h + 1) * DH_A)
        q = _dot(xc[:, sl], wq_ref[h])
        k = _dot(xc[:, sl], wk_ref[h]) * (DH_A ** -0.5)
        v = _dot(xab[:, sl], wv_ref[h])
        kw = k * sf_e[:, sl]
        qk8 = jnp.where(lane == h, jnp.sum(q * k, axis=-1, keepdims=True), qk8)
        qn8 = jnp.where(lane == h, jnp.sum(q * n0_ref[:, sl], axis=-1, keepdims=True), qn8)
        n1_out[:, sl] = w_e[:, sl] * n0_ref[:, sl] + kw
        v_out[:, sl] = v
        qt_out[h] = q.T
        kwt_out[h] = kw.T
    s8 = qk8 * sfac
    a1_out[...] = _expand_lanes(s8, L_F, H_A, DH_A) * v_out[...]
    w1_out[...] = w_e
    den_out[...] = jnp.maximum(jnp.abs(_expand_lanes(s8 + w_inter * qn8, L_F, H_A, DH_A)),
                               jnp.exp(-_expand_lanes(m, L_F, H_A, DH_A)))
    m1_out[...] = m
    g_out[...] = jnp.where(lane_f, w_inter, jnp.where(lane_dta, ea, 0.0))
    zas_out[...] = jax.nn.sigmoid(_dot(hn, wcat_ref[:, d_a:2 * d_a]))

    off_xbc = 2 * d_a + d_b
    xbc = _dot(hn, wcat_ref[:, off_xbc:off_xbc + conv_b])
    xbc_c = (cwb_ref[0:1, :] * convb_ref[:, 0:conv_b] + cwb_ref[1:2, :] * convb_ref[:, conv_b:2 * conv_b]
             + cwb_ref[2:3, :] * convb_ref[:, 2 * conv_b:3 * conv_b] + cwb_ref[3:4, :] * xbc + cbb_ref[...])
    convb_out[:, 0:2 * conv_b] = convb_ref[:, conv_b:3 * conv_b]
    convb_out[:, 2 * conv_b:3 * conv_b] = xbc
    xbc_c = _silu(xbc_c)
    xs = xbc_c[:, 0:d_b]
    bc = xbc_c[:, d_b:conv_b]
    bc_out[...] = bc
    heads_per_group = H_B // G_B
    cbl = jnp.zeros((x.shape[0], LANES), F32)
    for g in range(G_B):
        cb_g = jnp.sum(bc[:, g * N_STATE:(g + 1) * N_STATE]
                       * bc[:, (G_B + g) * N_STATE:(G_B + g + 1) * N_STATE], axis=-1, keepdims=True)
        in_g = jnp.logical_and(lane >= L_DTA + g * heads_per_group,
                               lane < L_DTA + (g + 1) * heads_per_group)
        cbl = jnp.where(in_g, cb_g, cbl)
    dt_e = _expand_lanes(dt, L_DTA, H_B, HD_B)
    y1_out[...] = _expand_lanes(cbl * dt, L_DTA, H_B, HD_B) * xs + dsk_ref[...] * xs
    ea_out[...] = _expand_lanes(ea, L_DTA, H_B, HD_B)
    zbs_out[...] = _silu(_dot(hn, wcat_ref[:, 2 * d_a:2 * d_a + d_b]))
    xw = xs * dt_e
    for pi in range(H_B // 2):
        xwt_out[pi] = xw[:, pi * LANES:(pi + 1) * LANES].T


def _sample_state_kernel(dec_ref, ea_ref, c0_ref, s0_ref, qt_ref, kwt_ref, xwt_ref, v_ref, bc_ref,
                         c1_ref, s1_ref, qc_ref, ysi_ref):
    i = pl.program_id(0)
    bb = c0_ref.shape[0]
    shift = lax.rem(LANES - lax.rem(i * bb, LANES), LANES)
    lane = lax.broadcasted_iota(jnp.int32, (1, LANES), 1)
    top = lax.broadcasted_iota(jnp.int32, (LANES, 1), 0) < HD_B
    heads_per_group = H_B // G_B
    for h in range(H_A):
        sl = slice(h * DH_A, (h + 1) * DH_A)
        qt = pltpu.roll(qt_ref[h], shift, axis=1)
        kwt = pltpu.roll(kwt_ref[h], shift, axis=1)
        for r in range(bb):
            b = i * bb + r
            c0 = c0_ref[r, h]
            dec = dec_ref[b * H_A + h]
            v_row = v_ref[r:r + 1, sl]
            qc_ref[r:r + 1, sl] = jnp.sum(c0 * qt[:, r:r + 1], axis=0, keepdims=True)
            c1_ref[r, h] = dec * c0 + kwt[:, r:r + 1] * v_row
    for pi in range(H_B // 2):
        g = (2 * pi) // heads_per_group
        sl = slice(pi * LANES, (pi + 1) * LANES)
        xwt = pltpu.roll(xwt_ref[pi], shift, axis=1)
        acc = jnp.zeros((LANES, LANES), F32)
        for r in range(bb):
            b = i * bb + r
            s0 = s0_ref[r, pi]
            b_row = bc_ref[r:r + 1, g * N_STATE:(g + 1) * N_STATE]
            c_row = bc_ref[r:r + 1, (G_B + g) * N_STATE:(G_B + g + 1) * N_STATE]
            col = jnp.sum(s0 * c_row, axis=-1, keepdims=True)
            acc = jnp.where(lane == r, col, acc)
            ea_rows = jnp.where(top, ea_ref[b * H_B + 2 * pi], ea_ref[b * H_B + 2 * pi + 1])
            s1_ref[r, pi] = ea_rows * s0 + xwt[:, r:r + 1] * b_row
        ysi_ref[:, sl] = acc.T[0:bb, :]


def _sample_post_kernel(a1_ref, w1_ref, den_ref, y1_ref, ea_ref, zbs_ref, zas_ref, x_ref, qc_ref, ysi_ref,
                        na_ref, nb_ref, wout_ref, hall_ref, hmid_ref, merged):
    del hall_ref
    d_a = H_A * DH_A
    d_b = H_B * HD_B
    hh = (a1_ref[...] + w1_ref[...] * qc_ref[...]) / den_ref[...]
    for h in range(H_A):
        sl = slice(h * DH_A, (h + 1) * DH_A)
        hs = hh[:, sl]
        merged[:, sl] = (hs * _rms_scale(hs) * na_ref[:, sl] * zas_ref[:, sl]).astype(BF16)
    y = (y1_ref[...] + ea_ref[...] * ysi_ref[...]) * zbs_ref[...]
    gw = d_b // G_B
    for g in range(G_B):
        yg = y[:, g * gw:(g + 1) * gw]
        merged[:, d_a + g * gw:d_a + (g + 1) * gw] = (
            yg * _rms_scale(yg) * nb_ref[:, g * gw:(g + 1) * gw]).astype(BF16)
    hmid_ref[...] = x_ref[...] + _dot(merged[...], wout_ref[...])


def _vmem_specs(arrays):
    return [pl.BlockSpec(a.shape, lambda *_, _nd=a.ndim: (0,) * _nd) for a in arrays]


def _sample_mixer(x, c0, n0, m0, conva, s0, convb, p, hmid_all, row_offset):
    nb, d = x.shape
    d_a = H_A * DH_A
    d_b = H_B * HD_B
    conv_b = d_b + 2 * G_B * N_STATE
    row = lambda w: jax.ShapeDtypeStruct((nb, w), F32)
    tile = lambda k: jax.ShapeDtypeStruct((k, LANES, nb), F32)
    pre_in = [x, p["nmix"], p["wcat"], p["bsm"], p["alog"], p["cwa"], p["cba"], p["cwb"], p["cbb"],
              p["wq"], p["wk"], p["wv"], p["dsk"], conva, convb, n0, m0]
    pre_out_shape = (row(3 * d_a), row(3 * conv_b), row(d_a), row(LANES), row(LANES),
                     tile(H_A), tile(H_A), tile(H_B // 2), row(d_a), row(2 * G_B * N_STATE),
                     row(d_a), row(d_a), row(d_a), row(d_b), row(d_b), row(d_b), row(d_a))
    (conva1, convb1, n1, m1, g8, qt, kwt, xwt, v, bc, a1, w1, den, y1, ea_e, zbs, zas) = pl.pallas_call(
        _sample_pre_kernel,
        out_shape=pre_out_shape,
        grid=(1,),
        in_specs=_vmem_specs(pre_in),
        out_specs=tuple(pl.BlockSpec(s.shape, lambda i, _nd=len(s.shape): (0,) * _nd) for s in pre_out_shape),
        compiler_params=pltpu.CompilerParams(
            dimension_semantics=("arbitrary",), vmem_limit_bytes=VMEM_LIMIT),
        name="sample_pre",
    )(*pre_in)

    dec_flat = g8[:, L_F:L_F + H_A].reshape(nb * H_A)
    ea_flat = g8[:, L_DTA:L_DTA + H_B].reshape(nb * H_B)
    bb = SAMPLE_BLOCK
    const3 = lambda k: pl.BlockSpec((k, LANES, nb), lambda i, de, ea: (0, 0, 0))
    state_grid = pltpu.PrefetchScalarGridSpec(
        num_scalar_prefetch=2,
        grid=(nb // bb,),
        in_specs=[pl.BlockSpec((bb, H_A, DH_A, DH_A), lambda i, de, ea: (i, 0, 0, 0)),
                  pl.BlockSpec((bb, H_B // 2, 2 * HD_B, N_STATE), lambda i, de, ea: (i, 0, 0, 0)),
                  const3(H_A), const3(H_A), const3(H_B // 2),
                  pl.BlockSpec((bb, d_a), lambda i, de, ea: (i, 0)),
                  pl.BlockSpec((bb, 2 * G_B * N_STATE), lambda i, de, ea: (i, 0))],
        out_specs=(pl.BlockSpec((bb, H_A, DH_A, DH_A), lambda i, de, ea: (i, 0, 0, 0)),
                   pl.BlockSpec((bb, H_B // 2, 2 * HD_B, N_STATE), lambda i, de, ea: (i, 0, 0, 0)),
                   pl.BlockSpec((bb, d_a), lambda i, de, ea: (i, 0)),
                   pl.BlockSpec((bb, d_b), lambda i, de, ea: (i, 0))),
    )
    c1, s1, qc, ysi = pl.pallas_call(
        _sample_state_kernel,
        out_shape=(jax.ShapeDtypeStruct(c0.shape, F32), jax.ShapeDtypeStruct(s0.shape, F32),
                   row(d_a), row(d_b)),
        grid_spec=state_grid,
        compiler_params=pltpu.CompilerParams(
            dimension_semantics=("arbitrary",), vmem_limit_bytes=VMEM_LIMIT),
        name="sample_state",
    )(dec_flat, ea_flat, c0, s0, qt, kwt, xwt, v, bc)

    post_in = [a1, w1, den, y1, ea_e, zbs, zas, x, qc, ysi, p["na"], p["nb"], p["wout"]]
    hmid_all = pl.pallas_call(
        _sample_post_kernel,
        out_shape=jax.ShapeDtypeStruct(hmid_all.shape, F32),
        grid=(1,),
        in_specs=_vmem_specs(post_in) + [pl.BlockSpec(memory_space=pl.ANY)],
        out_specs=pl.BlockSpec((nb, d), lambda i: (row_offset // nb, 0)),
        scratch_shapes=[pltpu.VMEM((nb, d_a + d_b), BF16)],
        input_output_aliases={len(post_in): 0},
        compiler_params=pltpu.CompilerParams(
            dimension_semantics=("arbitrary",), vmem_limit_bytes=VMEM_LIMIT),
        name="sample_post",
    )(*post_in, hmid_all)
    return hmid_all, c1, n1, m1, conva1, s1, convb1


R_EA, R_EB, R_RA, R_RB, R_GA, R_GB = 0, 1, 2, 3, 4, 5
RL_E = N_EGROUPS


def _router_kernel(h_ref, nf_ref, whi_ref, wmid_ref, br_ref, xn_ref, info_ref, cnt_ref, carry):
    i = pl.program_id(0)
    tr = h_ref.shape[0]

    @pl.when(i == 0)
    def _init():
        carry[...] = jnp.zeros_like(carry)

    h = h_ref[...]
    xn = h * _rms_scale(h) * nf_ref[...]
    xn_ref[...] = xn
    x_hi, x_mid, _ = _split3(xn)
    logits = (_dot(x_hi, whi_ref[...]) + _dot(x_hi, wmid_ref[...]) + _dot(x_mid, whi_ref[...])
              + br_ref[...])
    lane_i = lax.broadcasted_iota(jnp.int32, (1, LANES), 1)
    lane = lane_i.astype(F32)
    big = float(LANES)

    def first_lane_of(cond):
        return jnp.min(jnp.where(cond, lane, big), axis=-1, keepdims=True)

    l1 = jnp.where(lane_i < N_EGROUPS, logits, NEG_INF)
    e1 = jnp.exp(l1 - jnp.max(l1, axis=-1, keepdims=True))
    p1 = e1 / jnp.sum(e1, axis=-1, keepdims=True)
    gp = jnp.max(p1, axis=-1, keepdims=True)
    gidx = first_lane_of(p1 == gp)
    lo = RL_E + N_EPG * gidx
    l2 = jnp.where(jnp.logical_and(lane >= lo, lane < lo + N_EPG), logits, NEG_INF)
    va = jnp.max(l2, axis=-1, keepdims=True)
    ia = first_lane_of(l2 == va)
    l2b = jnp.where(lane == ia, NEG_INF, l2)
    vb = jnp.max(l2b, axis=-1, keepdims=True)
    ib = first_lane_of(l2b == vb)
    eb = jnp.exp(vb - va)
    wa = 1.0 / (1.0 + eb)
    wb = eb / (1.0 + eb)

    is_a = lane == ia
    is_b = lane == ib
    onehot = jnp.where(jnp.logical_or(is_a, is_b), 1.0, 0.0)
    ri = lax.broadcasted_iota(jnp.int32, (tr, tr), 0)
    ci = lax.broadcasted_iota(jnp.int32, (tr, tr), 1)
    tri = jnp.where(ri >= ci, 1.0, 0.0).astype(BF16)
    incl = _dot(tri, onehot.astype(BF16))
    excl = incl - onehot + carry[...]
    rank_a = jnp.sum(jnp.where(is_a, excl, 0.0), axis=-1, keepdims=True)
    rank_b = jnp.sum(jnp.where(is_b, excl, 0.0), axis=-1, keepdims=True)
    carry[...] = carry[...] + incl[tr - 1:tr, :]
    cnt_ref[...] = carry[...]

    info = jnp.where(lane_i == R_EA, ia - RL_E, 0.0)
    info = jnp.where(lane_i == R_EB, ib - RL_E, info)
    info = jnp.where(lane_i == R_RA, rank_a, info)
    info = jnp.where(lane_i == R_RB, rank_b, info)
    info = jnp.where(lane_i == R_GA, gp * wa, info)
    info = jnp.where(lane_i == R_GB, gp * wb, info)
    info_ref[...] = info


def _row_tile(n, candidates):
    for t in candidates:
        if n % t == 0:
            return t
    raise ValueError(f"no row tile for {n} rows among {candidates}")


def _router(hmid, rp):
    n, d = hmid.shape
    tr = _row_tile(n, (512, 384, 256, 128))
    return pl.pallas_call(
        _router_kernel,
        out_shape=(jax.ShapeDtypeStruct((n, d), F32),
                   jax.ShapeDtypeStruct((n, LANES), F32),
                   jax.ShapeDtypeStruct((1, LANES), F32)),
        grid=(n // tr,),
        in_specs=[pl.BlockSpec((tr, d), lambda i: (i, 0)),
                  pl.BlockSpec((1, d), lambda i: (0, 0)),
                  pl.BlockSpec((d, LANES), lambda i: (0, 0)),
                  pl.BlockSpec((d, LANES), lambda i: (0, 0)),
                  pl.BlockSpec((1, LANES), lambda i: (0, 0))],
        out_specs=(pl.BlockSpec((tr, d), lambda i: (i, 0)),
                   pl.BlockSpec((tr, LANES), lambda i: (i, 0)),
                   pl.BlockSpec((1, LANES), lambda i: (0, 0))),
        scratch_shapes=[pltpu.VMEM((1, LANES), F32)],
        compiler_params=pltpu.CompilerParams(
            dimension_semantics=("arbitrary",), vmem_limit_bytes=VMEM_LIMIT),
        name="router",
    )(hmid, rp["nf"], rp["whi"], rp["wmid"], rp["br"])


def _prep_router_params(norm_ffn, w_r1, b_r1, w_r2, b_r2):
    d = w_r1.shape[1]
    w = jnp.concatenate([w_r1[0].astype(F32), w_r2[0].reshape(d, N_EXPERTS).astype(F32),
                         jnp.zeros((d, LANES - RL_E - N_EXPERTS), F32)], axis=1)
    whi = w.astype(BF16)
    wmid = (w - whi.astype(F32)).astype(BF16)
    br = jnp.concatenate([b_r1[0].astype(F32), b_r2[0].reshape(N_EXPERTS).astype(F32),
                          jnp.zeros((LANES - RL_E - N_EXPERTS,), F32)])[None, :]
    return dict(nf=norm_ffn[0][None, :].astype(F32), whi=whi, wmid=wmid, br=br)


FFN_TM = 256


def _start_row_gather(idx_ref, base, n_rows, src_hbm, dst, sem):
    def body(r, carry):
        pltpu.make_async_copy(src_hbm.at[pl.ds(idx_ref[base + r], 1), :],
                              dst.at[pl.ds(r, 1), :], sem).start()
        return carry
    lax.fori_loop(0, n_rows, body, 0, unroll=8)


def _wait_row_gather(n_rows, src_hbm, dst, sem):
    pltpu.make_async_copy(src_hbm.at[pl.ds(0, n_rows), :], dst, sem).wait()


def _ffn_kernel(te_ref, src_ref, nv_ref, xn_hbm, wg_ref, wu_ref, wd_ref, ys_ref, xbuf, sem):
    i = pl.program_id(0)
    n_valid = nv_ref[0]
    tm = xbuf.shape[1]
    slot = lax.rem(i, 2)

    @pl.when(i == 0)
    def _first():
        _start_row_gather(src_ref, 0, tm, xn_hbm, xbuf.at[0], sem.at[0])

    @pl.when(i + 1 < n_valid)
    def _next():
        _start_row_gather(src_ref, (i + 1) * tm, tm, xn_hbm, xbuf.at[1 - slot], sem.at[1 - slot])

    @pl.when(i < n_valid)
    def _compute():
        _wait_row_gather(tm, xn_hbm, xbuf.at[slot], sem.at[slot])
        x = xbuf[slot].astype(BF16)
        hg = _dot(x, wg_ref[...])
        hu = _dot(x, wu_ref[...])
        ys_ref[...] = _dot((_silu(hg) * hu).astype(BF16), wd_ref[...])

    @pl.when(i >= n_valid)
    def _pad():
        ys_ref[...] = jnp.zeros_like(ys_ref)


def _expert_ffn(xn, tile_expert, src_row, n_valid, wg, wu, wd):
    n, d = xn.shape
    n_tiles = tile_expert.shape[0]
    tm = FFN_TM
    dff = wg.shape[-1]
    grid_spec = pltpu.PrefetchScalarGridSpec(
        num_scalar_prefetch=3,
        grid=(n_tiles,),
        in_specs=[pl.BlockSpec(memory_space=pl.ANY),
                  pl.BlockSpec((None, d, dff), lambda i, te, src, nv: (te[i], 0, 0)),
                  pl.BlockSpec((None, d, dff), lambda i, te, src, nv: (te[i], 0, 0)),
                  pl.BlockSpec((None, dff, d), lambda i, te, src, nv: (te[i], 0, 0))],
        out_specs=pl.BlockSpec((tm, d), lambda i, te, src, nv: (i, 0)),
        scratch_shapes=[pltpu.VMEM((2, tm, d), F32), pltpu.SemaphoreType.DMA((2,))],
    )
    return pl.pallas_call(
        _ffn_kernel,
        out_shape=jax.ShapeDtypeStruct((n_tiles * tm, d), F32),
        grid_spec=grid_spec,
        compiler_params=pltpu.CompilerParams(
            dimension_semantics=("arbitrary",), vmem_limit_bytes=VMEM_LIMIT),
        name="expert_ffn",
    )(tile_expert, src_row, n_valid, xn, wg, wu, wd)


def _routing_tables(info, counts, n_tiles, tm):
    n = info.shape[0]
    eid_a = info[:, R_EA].astype(jnp.int32)
    eid_b = info[:, R_EB].astype(jnp.int32)
    rank_a = info[:, R_RA].astype(jnp.int32)
    rank_b = info[:, R_RB].astype(jnp.int32)
    cnt = counts[0, RL_E:RL_E + N_EXPERTS].astype(jnp.int32)
    padded = ((cnt + tm - 1) // tm) * tm
    ends = jnp.cumsum(padded)
    off = ends - padded
    pos_a = off[eid_a] + rank_a
    pos_b = off[eid_b] + rank_b
    tok = jnp.arange(n, dtype=jnp.int32)
    src_row = jnp.zeros((n_tiles * tm,), jnp.int32).at[pos_a].set(tok).at[pos_b].set(tok)
    n_valid = (ends[-1] // tm).astype(jnp.int32)
    tile_start = jnp.arange(n_tiles, dtype=jnp.int32) * tm
    te = jnp.sum((tile_start[:, None] >= ends[None, :]).astype(jnp.int32), axis=1)
    te_last = jnp.take(te, jnp.maximum(n_valid - 1, 0))
    te = jnp.where(tile_start < ends[-1], te, te_last)
    te = jnp.minimum(te, N_EXPERTS - 1)
    return te, src_row, n_valid.reshape(1), pos_a, pos_b


def _combine_kernel(pa_ref, pb_ref, h_ref, info_ref, ys_hbm, nfin_ref, yp_ref, ysm_ref,
                    buf_a, buf_b, sem, *, n_prompt_tiles):
    i = pl.program_id(0)
    n_steps = pl.num_programs(0)
    tt = h_ref.shape[0]
    slot = lax.rem(i, 2)

    def start(tile, s):
        _start_row_gather(pa_ref, tile * tt, tt, ys_hbm, buf_a.at[s], sem.at[s])
        _start_row_gather(pb_ref, tile * tt, tt, ys_hbm, buf_b.at[s], sem.at[s])

    @pl.when(i == 0)
    def _first():
        start(0, 0)

    @pl.when(i + 1 < n_steps)
    def _next():
        start(i + 1, 1 - slot)

    _wait_row_gather(tt, ys_hbm, buf_a.at[slot], sem.at[slot])
    _wait_row_gather(tt, ys_hbm, buf_b.at[slot], sem.at[slot])
    info = info_ref[...]
    h = h_ref[...] + info[:, R_GA:R_GA + 1] * buf_a[slot] + info[:, R_GB:R_GB + 1] * buf_b[slot]
    y = h * _rms_scale(h) * nfin_ref[...]

    @pl.when(i < n_prompt_tiles)
    def _prompt():
        yp_ref[...] = y

    @pl.when(i >= n_prompt_tiles)
    def _sample():
        ysm_ref[...] = y


def _combine(hmid, info, ys, pos_a, pos_b, nfin, n_prompt):
    n, d = hmid.shape
    tt = CHUNK
    n_prompt_tiles = n_prompt // tt
    n_sample = n - n_prompt
    grid_spec = pltpu.PrefetchScalarGridSpec(
        num_scalar_prefetch=2,
        grid=(n // tt,),
        in_specs=[pl.BlockSpec((tt, d), lambda i, pa, pb: (i, 0)),
                  pl.BlockSpec((tt, LANES), lambda i, pa, pb: (i, 0)),
                  pl.BlockSpec(memory_space=pl.ANY),
                  pl.BlockSpec((1, d), lambda i, pa, pb: (0, 0))],
        out_specs=(pl.BlockSpec((tt, d), lambda i, pa, pb: (jnp.minimum(i, n_prompt_tiles - 1), 0)),
                   pl.BlockSpec((tt, d), lambda i, pa, pb: (jnp.maximum(i - n_prompt_tiles, 0), 0))),
        scratch_shapes=[pltpu.VMEM((2, tt, d), F32), pltpu.VMEM((2, tt, d), F32),
                        pltpu.SemaphoreType.DMA((2,))],
    )
    return pl.pallas_call(
        functools.partial(_combine_kernel, n_prompt_tiles=n_prompt_tiles),
        out_shape=(jax.ShapeDtypeStruct((n_prompt, d), F32),
                   jax.ShapeDtypeStruct((n_sample, d), F32)),
        grid_spec=grid_spec,
        compiler_params=pltpu.CompilerParams(
            dimension_semantics=("arbitrary",), vmem_limit_bytes=VMEM_LIMIT),
        name="combine",
    )(pos_a, pos_b, hmid, info, ys, nfin)


def _moe_and_final_norm(hmid, n_prompt, rp, wg, wu, wd, nfin):
    n = hmid.shape[0]
    tm = FFN_TM
    n_tiles = (2 * n + N_EXPERTS * (tm - 1)) // tm
    xn, info, counts = _router(hmid, rp)
    te, src_row, n_valid, pos_a, pos_b = _routing_tables(info, counts, n_tiles, tm)
    ys = _expert_ffn(xn, te, src_row, n_valid, wg, wu, wd)
    return _combine(hmid, info, ys, pos_a, pos_b, nfin, n_prompt)


def kernel(x_prompt, x_sample, state_mlstm_C, state_mlstm_n, state_mlstm_m, state_mlstm_conv, state_ssm, state_ssm_conv, meta_tokens, norm_mix, w_in, conv_a_w, conv_a_b, w_q, w_k, w_v, b_i, b_f, norm_a, conv_b_w, conv_b_b, dt_bias, a_log, d_skip, norm_b, w_out, norm_ffn, w_r1, b_r1, w_r2, b_r2, w_gate, w_up, w_down, norm_final):
    bsz, seq, d = x_prompt.shape
    nb = x_sample.shape[0]
    d_a = H_A * DH_A
    conv_b = H_B * HD_B + 2 * G_B * N_STATE
    assert w_in.shape[0] == 1 and x_sample.shape[1] == 1 and seq % CHUNK == 0 and nb == CHUNK
    mp = _prep_mixer_params(norm_mix, w_in, conv_a_w, conv_a_b, w_q, w_k, w_v, b_i, b_f, norm_a,
                            conv_b_w, conv_b_b, dt_bias, a_log, d_skip, norm_b, w_out)
    rp = _prep_router_params(norm_ffn, w_r1, b_r1, w_r2, b_r2)
    xmeta = jnp.concatenate([jnp.zeros((CHUNK - N_META, d), F32), meta_tokens.astype(F32)], 0)

    hmid, p_c, p_n, p_m, p_ca, p_s, p_cb = _prompt_mixer(x_prompt.astype(F32), xmeta, mp, nb)
    m0 = jnp.pad(state_mlstm_m[0].astype(F32), ((0, 0), (0, LANES - H_A)))
    hmid, s_c, s_n, s_m, s_ca, s_s, s_cb = _sample_mixer(
        x_sample.reshape(nb, d).astype(F32),
        state_mlstm_C[0].astype(F32),
        state_mlstm_n[0].astype(F32).reshape(nb, d_a),
        m0,
        state_mlstm_conv[0].astype(F32).reshape(nb, (CONV_W - 1) * d_a),
        state_ssm[0].astype(F32).reshape(nb, H_B // 2, 2 * HD_B, N_STATE),
        state_ssm_conv[0].astype(F32).reshape(nb, (CONV_W - 1) * conv_b),
        mp, hmid, bsz * seq)

    y_p, y_s = _moe_and_final_norm(
        hmid, bsz * seq, rp, w_gate[0].astype(BF16), w_up[0].astype(BF16), w_down[0].astype(BF16),
        norm_final[None, :].astype(F32))

    return (y_p.reshape(bsz, seq, d), y_s.reshape(nb, 1, d),
            p_c[None], p_n[None], p_m[:, 0, :H_A][None], p_ca[None],
            p_s.reshape(bsz, H_B, HD_B, N_STATE)[None], p_cb[None],
            s_c[None], s_n.reshape(nb, H_A, DH_A)[None], s_m[:, :H_A][None],
            s_ca.reshape(nb, CONV_W - 1, d_a)[None],
            s_s.reshape(nb, H_B, HD_B, N_STATE)[None],
            s_cb.reshape(nb, CONV_W - 1, conv_b)[None])
```

```python
import functools
import math

import jax
import jax.numpy as jnp
from jax import lax
from jax.experimental import pallas as pl
from jax.experimental.pallas import tpu as pltpu

F32 = jnp.float32
BF16 = jnp.bfloat16

EPS = 1e-6
N_META = 16
CONV_W = 4
CHUNK = 128
H_A = 8
DH_A = 128
H_B = 16
HD_B = 64
N_STATE = 128
G_B = 2
N_EGROUPS = 4
N_EPG = 4
N_EXPERTS = 16
LANES = 128
SUBLANES = 8
CONV_HDR = SUBLANES
VMEM_LIMIT = 56 * 1024 * 1024

L_F = 0
L_DTA = 8
L_I = 24
L_DT = 32

NEG_INF = float("-inf")


def _dot(a, b):
    return jnp.dot(a, b, preferred_element_type=F32)


def _dot_nt(a, b):
    return lax.dot_general(a, b, (((1,), (1,)), ((), ())), preferred_element_type=F32)


def _dot_tn(a, b):
    return lax.dot_general(a, b, (((0,), (0,)), ((), ())), preferred_element_type=F32)


def _split3(x):
    hi = x.astype(BF16)
    r = x - hi.astype(F32)
    mid = r.astype(BF16)
    lo = (r - mid.astype(F32)).astype(BF16)
    return hi, mid, lo


def _silu(x):
    return x * jax.nn.sigmoid(x)


def _softplus_parts(x):
    t = jnp.log1p(jnp.exp(-jnp.abs(x)))
    return jnp.maximum(x, 0.0) + t, jnp.minimum(x, 0.0) - t


def _rms_scale(x):
    return lax.rsqrt(jnp.mean(x * x, axis=-1, keepdims=True) + EPS)


TOK_TILE_ROWS = SUBLANES


def _store_token_tiles(ref, x):
    n = x.shape[0]
    for j in range(TOK_TILE_ROWS):
        ref[pl.ds(j, n, stride=TOK_TILE_ROWS), :] = x[:, j * LANES:(j + 1) * LANES]


def _load_token_tiles(ref, n):
    return jnp.concatenate(
        [ref[pl.ds(j, n, stride=TOK_TILE_ROWS), :] for j in range(TOK_TILE_ROWS)], axis=1)


def _prompt_mixer_kernel(xmeta_ref, xp_ref, nmix_ref, wcat_ref, bsm_ref, alog_ref,
                         cwa_ref, cba_ref, cwb_ref, cbb_ref, wq_ref, wk_ref, wv_ref,
                         na_ref, nb_ref, dsk_ref, wout_ref,
                         hmid_ref, c_ref, n_ref, m_ref, conva_ref, s_ref, convb_ref,
                         xa_buf, xbc_buf, y_buf, merged):
    c = pl.program_id(1)
    T = CHUNK
    d_a = H_A * DH_A
    d_b = H_B * HD_B

    @pl.when(c == 0)
    def _init():
        c_ref[...] = jnp.zeros_like(c_ref)
        n_ref[...] = jnp.zeros_like(n_ref)
        m_ref[...] = jnp.zeros_like(m_ref)
        s_ref[...] = jnp.zeros_like(s_ref)
        xa_buf[0:CONV_HDR, :] = jnp.zeros((CONV_HDR, xa_buf.shape[1]), F32)
        xbc_buf[0:CONV_HDR, :] = jnp.zeros((CONV_HDR, xbc_buf.shape[1]), F32)

    x = jnp.where(c == 0, xmeta_ref[...], xp_ref[...])
    row = lax.broadcasted_iota(jnp.int32, (T, 1), 0)
    valid = jnp.logical_or(c > 0, row >= T - N_META)

    hn = (x * _rms_scale(x) * nmix_ref[...]).astype(BF16)

    lane = lax.broadcasted_iota(jnp.int32, (1, LANES), 1)
    lane_f = lane < L_DTA
    lane_dta = jnp.logical_and(lane >= L_DTA, lane < L_I)
    lane_i = jnp.logical_and(lane >= L_I, lane < L_DT)
    lane_dt = jnp.logical_and(lane >= L_DT, lane < L_DT + H_B)
    pre = _dot(hn, wcat_ref[:, 2 * d_a + d_b + (d_b + 2 * G_B * N_STATE):]) + bsm_ref[...]
    sp, lsig = _softplus_parts(pre)
    a_neg = jnp.where(lane_dta, -jnp.exp(alog_ref[...]), 0.0)
    to_cum = jnp.where(lane_f, lsig, jnp.where(lane_dta, sp * a_neg, 0.0))
    to_cum = jnp.where(valid, to_cum, 0.0)
    ri = lax.broadcasted_iota(jnp.int32, (T, T), 0)
    ci = lax.broadcasted_iota(jnp.int32, (T, T), 1)
    causal = ri >= ci
    tri = jnp.where(causal, 1.0, 0.0).astype(BF16)
    hi, mid, lo = _split3(to_cum)
    cum = _dot(tri, hi) + _dot(tri, mid) + _dot(tri, lo)
    extra = jnp.where(lane_i, jnp.where(valid, pre, NEG_INF),
                      jnp.where(lane_dt, jnp.where(valid, sp, 0.0), 0.0))
    gcol = cum + extra
    grow = gcol.T

    xa = _dot(hn, wcat_ref[:, 0:d_a])
    xa_buf[CONV_HDR:CONV_HDR + T, :] = xa
    xc = (cwa_ref[3:4, :] * xa + cwa_ref[2:3, :] * xa_buf[CONV_HDR - 1:CONV_HDR - 1 + T, :]
          + cwa_ref[1:2, :] * xa_buf[CONV_HDR - 2:CONV_HDR - 2 + T, :]
          + cwa_ref[0:1, :] * xa_buf[CONV_HDR - 3:CONV_HDR - 3 + T, :] + cba_ref[...])
    xa_buf[CONV_HDR - 3:CONV_HDR, :] = xa[T - 3:T, :]
    xc = _silu(xc).astype(BF16)
    xab = xa.astype(BF16)
    za = _dot(hn, wcat_ref[:, d_a:2 * d_a])
    m_all = m_ref[...]
    m_new = m_all
    for h in range(H_A):
        sl = slice(h * DH_A, (h + 1) * DH_A)
        q = _dot(xc[:, sl], wq_ref[h]).astype(BF16)
        k = _dot(xc[:, sl], wk_ref[h]) * (DH_A ** -0.5)
        v = _dot(xab[:, sl], wv_ref[h]).astype(BF16)
        b_col = gcol[:, L_F + h:L_F + h + 1]
        i_col = gcol[:, L_I + h:L_I + h + 1]
        b_row = grow[L_F + h:L_F + h + 1, :]
        i_row = grow[L_I + h:L_I + h + 1, :]
        m0 = m_all[:, h:h + 1]
        dmat = jnp.where(causal, b_col - (b_row - i_row), NEG_INF)
        m_inter = b_col + m0
        m = jnp.maximum(m_inter, jnp.max(dmat, axis=-1, keepdims=True))
        w_inter = jnp.exp(m_inter - m)
        s = _dot_nt(q, k.astype(BF16)) * jnp.exp(dmat - m)
        c0 = c_ref[h]
        n0 = n_ref[h:h + 1, :]
        num = _dot(s.astype(BF16), v) + w_inter * _dot(q, c0.astype(BF16))
        qf = q.astype(F32)
        den = jnp.sum(s, axis=-1, keepdims=True) + w_inter * jnp.sum(qf * n0, axis=-1, keepdims=True)
        hh = num / jnp.maximum(jnp.abs(den), jnp.exp(-m))
        m_last = m[T - 1:T, :]
        b_last = b_col[T - 1:T, :]
        dec = jnp.exp(b_last + m0 - m_last)
        ws = jnp.exp(b_last - b_col + i_col - m_last)
        kw = k * ws
        c_ref[h] = dec * c0 + _dot_tn(kw.astype(BF16), v)
        n_ref[h:h + 1, :] = dec * n0 + jnp.sum(kw, axis=0, keepdims=True)
        m_new = jnp.where(lane == h, m_last, m_new)
        hh = hh * _rms_scale(hh) * na_ref[:, sl]
        merged[:, sl] = (hh * jax.nn.sigmoid(za[:, sl])).astype(BF16)
    m_ref[...] = m_new

    off_xbc = 3 * d_a
    xbc = _dot(hn, wcat_ref[:, off_xbc:off_xbc + d_b + 2 * G_B * N_STATE])
    xbc_buf[CONV_HDR:CONV_HDR + T, :] = xbc
    xbc_c = (cwb_ref[3:4, :] * xbc + cwb_ref[2:3, :] * xbc_buf[CONV_HDR - 1:CONV_HDR - 1 + T, :]
             + cwb_ref[1:2, :] * xbc_buf[CONV_HDR - 2:CONV_HDR - 2 + T, :]
             + cwb_ref[0:1, :] * xbc_buf[CONV_HDR - 3:CONV_HDR - 3 + T, :] + cbb_ref[...])
    xbc_buf[CONV_HDR - 3:CONV_HDR, :] = xbc[T - 3:T, :]
    xbc_c = _silu(xbc_c)
    zb = _dot(hn, wcat_ref[:, 2 * d_a:2 * d_a + d_b])
    left = lane < HD_B
    top = lax.broadcasted_iota(jnp.int32, (LANES, 1), 0) < HD_B
    pairs_per_group = H_B // G_B // 2
    for g in range(G_B):
        bg = xbc_c[:, d_b + g * N_STATE:d_b + (g + 1) * N_STATE].astype(BF16)
        cg = xbc_c[:, d_b + (G_B + g) * N_STATE:d_b + (G_B + g + 1) * N_STATE].astype(BF16)
        cb = _dot_nt(cg, bg)
        for p in range(pairs_per_group):
            pi = g * pairs_per_group + p
            sl = slice(pi * LANES, (pi + 1) * LANES)
            xpair = xbc_c[:, sl]
            xpb = xpair.astype(BF16)
            ys, a_cols, w_cols, a_lasts = [], [], [], []
            for j in (2 * pi, 2 * pi + 1):
                a_col = gcol[:, L_DTA + j:L_DTA + j + 1]
                a_row = grow[L_DTA + j:L_DTA + j + 1, :]
                dt_col = gcol[:, L_DT + j:L_DT + j + 1]
                dt_row = grow[L_DT + j:L_DT + j + 1, :]
                decay = jnp.exp(jnp.where(causal, a_col - a_row, NEG_INF))
                scores = cb * decay * dt_row
                ys.append(_dot(scores.astype(BF16), xpb))
                a_last = a_col[T - 1:T, :]
                a_cols.append(a_col)
                a_lasts.append(a_last)
                w_cols.append(jnp.exp(a_last - a_col) * dt_col)
            s0 = s_ref[pi]
            y = jnp.where(left, ys[0], ys[1])
            ea = jnp.exp(jnp.where(left, a_cols[0], a_cols[1]))
            y = y + ea * _dot_nt(cg, s0.astype(BF16))
            xw = (xpair * jnp.where(left, w_cols[0], w_cols[1])).astype(BF16)
            ea_last = jnp.exp(jnp.where(top, a_lasts[0], a_lasts[1]))
            s_ref[pi] = ea_last * s0 + _dot_tn(xw, bg)
            y = y + dsk_ref[:, sl] * xpair
            y_buf[:, sl] = y * _silu(zb[:, sl])
    gw = d_b // G_B
    for g in range(G_B):
        yg = y_buf[:, g * gw:(g + 1) * gw]
        merged[:, d_a + g * gw:d_a + (g + 1) * gw] = (
            yg * _rms_scale(yg) * nb_ref[:, g * gw:(g + 1) * gw]).astype(BF16)

    @pl.when(c > 0)
    def _out():
        hmid_ref[...] = x + _dot(merged[...], wout_ref[...])

    @pl.when(c == 0)
    def _no_out():
        hmid_ref[...] = jnp.zeros_like(hmid_ref)

    conva_ref[...] = xa_buf[CONV_HDR + T - 3:CONV_HDR + T, :]
    convb_ref[...] = xbc_buf[CONV_HDR + T - 3:CONV_HDR + T, :]


def _const_spec(shape):
    nd = len(shape)
    return pl.BlockSpec(shape, lambda b, c, _nd=nd: (0,) * _nd)


def _prompt_mixer(x_prompt, xmeta, p, n_extra_rows):
    bsz, seq, d = x_prompt.shape
    assert n_extra_rows == CHUNK and seq % CHUNK == 0
    n_chunks = seq // CHUNK + 1
    cps = seq // CHUNK
    d_a = H_A * DH_A
    conv_b = H_B * HD_B + 2 * G_B * N_STATE
    consts = [p["nmix"], p["wcat"], p["bsm"], p["alog"], p["cwa"], p["cba"], p["cwb"], p["cbb"],
              p["wq"], p["wk"], p["wv"], p["na"], p["nb"], p["dsk"], p["wout"]]
    in_specs = [_const_spec(xmeta.shape),
                pl.BlockSpec((None, CHUNK, d), lambda b, c: (b, jnp.maximum(c - 1, 0), 0))]
    in_specs += [_const_spec(a.shape) for a in consts]
    out_shape = (
        jax.ShapeDtypeStruct((bsz * seq + n_extra_rows, d), F32),
        jax.ShapeDtypeStruct((bsz, H_A, DH_A, DH_A), F32),
        jax.ShapeDtypeStruct((bsz, H_A, DH_A), F32),
        jax.ShapeDtypeStruct((bsz, 1, LANES), F32),
        jax.ShapeDtypeStruct((bsz, CONV_W - 1, d_a), F32),
        jax.ShapeDtypeStruct((bsz, H_B // 2, 2 * HD_B, N_STATE), F32),
        jax.ShapeDtypeStruct((bsz, CONV_W - 1, conv_b), F32),
    )
    out_specs = (
        pl.BlockSpec((CHUNK, d), lambda b, c: (
            jnp.where(jnp.logical_and(b == 0, c == 0), bsz * cps, b * cps + jnp.maximum(c - 1, 0)), 0)),
        pl.BlockSpec((None, H_A, DH_A, DH_A), lambda b, c: (b, 0, 0, 0)),
        pl.BlockSpec((None, H_A, DH_A), lambda b, c: (b, 0, 0)),
        pl.BlockSpec((None, 1, LANES), lambda b, c: (b, 0, 0)),
        pl.BlockSpec((None, CONV_W - 1, d_a), lambda b, c: (b, 0, 0)),
        pl.BlockSpec((None, H_B // 2, 2 * HD_B, N_STATE), lambda b, c: (b, 0, 0, 0)),
        pl.BlockSpec((None, CONV_W - 1, conv_b), lambda b, c: (b, 0, 0)),
    )
    return pl.pallas_call(
        _prompt_mixer_kernel,
        out_shape=out_shape,
        grid=(bsz, n_chunks),
        in_specs=in_specs,
        out_specs=out_specs,
        scratch_shapes=[
            pltpu.VMEM((CONV_HDR + CHUNK, d_a), F32),
            pltpu.VMEM((CONV_HDR + CHUNK, conv_b), F32),
            pltpu.VMEM((CHUNK, H_B * HD_B), F32),
            pltpu.VMEM((CHUNK, d_a + H_B * HD_B), BF16),
        ],
        compiler_params=pltpu.CompilerParams(
            dimension_semantics=("arbitrary", "arbitrary"), vmem_limit_bytes=VMEM_LIMIT),
        name="prompt_mixer",
    )(xmeta, x_prompt, *consts)


def _prep_mixer_params(norm_mix, w_in, conv_a_w, conv_a_b, w_q, w_k, w_v, b_i, b_f, norm_a,
                       conv_b_w, conv_b_b, dt_bias, a_log, d_skip, norm_b, w_out):
    d_a = H_A * DH_A
    d_b = H_B * HD_B
    conv_b = d_b + 2 * G_B * N_STATE
    w = w_in[0]
    o = 0
    w_xa = w[:, o:o + d_a]; o += d_a
    w_za = w[:, o:o + d_a]; o += d_a
    w_i = w[:, o:o + H_A]; o += H_A
    w_f = w[:, o:o + H_A]; o += H_A
    w_zb = w[:, o:o + d_b]; o += d_b
    w_xbc = w[:, o:o + conv_b]; o += conv_b
    w_dt = w[:, o:o + H_B]
    pad = jnp.zeros((w.shape[0], LANES - (L_DT + H_B)), w.dtype)
    wcat = jnp.concatenate([w_xa, w_za, w_zb, w_xbc, w_f, w_dt, w_i, w_dt, pad], axis=1).astype(BF16)

    def lanes(parts):
        v = jnp.zeros((1, LANES), F32)
        for off, a in parts:
            v = v.at[0, off:off + a.shape[0]].set(a.astype(F32))
        return v

    return dict(
        nmix=norm_mix[0][None, :].astype(F32),
        wcat=wcat,
        bsm=lanes([(L_F, b_f[0]), (L_DTA, dt_bias[0]), (L_I, b_i[0]), (L_DT, dt_bias[0])]),
        alog=lanes([(L_DTA, a_log[0])]),
        cwa=conv_a_w[0].astype(F32), cba=conv_a_b[0][None, :].astype(F32),
        cwb=conv_b_w[0].astype(F32), cbb=conv_b_b[0][None, :].astype(F32),
        wq=w_q[0].astype(BF16), wk=w_k[0].astype(BF16), wv=w_v[0].astype(BF16),
        na=norm_a[0].reshape(1, d_a).astype(F32), nb=norm_b[0][None, :].astype(F32),
        dsk=jnp.repeat(d_skip[0].astype(F32), HD_B)[None, :],
        wout=w_out[0].astype(BF16),
    )


SAMPLE_BLOCK = 8


def _expand_lanes(vals, first_lane, n_heads, width):
    r = lax.broadcasted_iota(jnp.int32, (LANES, n_heads * width), 0) - first_lane
    c = lax.broadcasted_iota(jnp.int32, (LANES, n_heads * width), 1)
    sel = jnp.logical_and(c >= r * width, c < (r + 1) * width)
    e = jnp.where(sel, 1.0, 0.0).astype(BF16)
    hi, mid, lo = _split3(vals)
    return (_dot(hi, e) + _dot(mid, e)) + _dot(lo, e)


def _sample_pre_kernel(x_ref, nmix_ref, wcat_ref, bsm_ref, alog_ref, cwa_ref, cba_ref, cwb_ref, cbb_ref,
                       wq_ref, wk_ref, wv_ref, dsk_ref, conva_ref, convb_ref, n0_ref, m0_ref,
                       conva_out, convb_out, n1_out, m1_out, g_out, qt_out, kwt_out, xwt_out,
                       v_out, bc_out, a1_out, w1_out, den_out, y1_out, ea_out, zbs_out, zas_out):
    d_a = H_A * DH_A
    d_b = H_B * HD_B
    conv_b = d_b + 2 * G_B * N_STATE
    shift_i = LANES - (L_I - L_F)
    x = x_ref[...]
    hn = (x * _rms_scale(x) * nmix_ref[...]).astype(BF16)
    lane = lax.broadcasted_iota(jnp.int32, (1, LANES), 1)
    lane_f = lane < L_DTA
    lane_dta = jnp.logical_and(lane >= L_DTA, lane < L_I)
    pre = _dot(hn, wcat_ref[:, 2 * d_a + d_b + conv_b:]) + bsm_ref[...]
    sp, lsig = _softplus_parts(pre)
    a_neg = jnp.where(lane_dta, -jnp.exp(alog_ref[...]), 0.0)
    pre_al = pltpu.roll(pre, shift_i, axis=1)
    sp_al = pltpu.roll(sp, shift_i, axis=1)
    m_inter = lsig + m0_ref[...]
    m = jnp.maximum(m_inter, pre_al)
    w_inter = jnp.exp(m_inter - m)
    sfac = jnp.exp(pre_al - m)
    ea = jnp.exp(sp * a_neg)
    dt = sp_al

    xa = _dot(hn, wcat_ref[:, 0:d_a])
    xc = (cwa_ref[0:1, :] * conva_ref[:, 0:d_a] + cwa_ref[1:2, :] * conva_ref[:, d_a:2 * d_a]
          + cwa_ref[2:3, :] * conva_ref[:, 2 * d_a:3 * d_a] + cwa_ref[3:4, :] * xa + cba_ref[...])
    conva_out[:, 0:2 * d_a] = conva_ref[:, d_a:3 * d_a]
    conva_out[:, 2 * d_a:3 * d_a] = xa
    xc = _silu(xc).astype(BF16)
    xab = xa.astype(BF16)
    sf_e = _expand_lanes(sfac, L_F, H_A, DH_A)
    w_e = _expand_lanes(w_inter, L_F, H_A, DH_A)
    qk8 = jnp.zeros((x.shape[0], LANES), F32)
    qn8 = jnp.zeros((x.shape[0], LANES), F32)
    for h in range(H_A):
        sl = slice(h * DH_A, (h + 1) * DH_A)
        q = _dot(xc[:, sl], wq_ref[h])
        k = _dot(xc[:, sl], wk_ref[h]) * (DH_A ** -0.5)
        v = _dot(xab[:, sl], wv_ref[h])
        kw = k * sf_e[:, sl]
        qk8 = jnp.where(lane == h, jnp.sum(q * k, axis=-1, keepdims=True), qk8)
        qn8 = jnp.where(lane == h, jnp.sum(q * n0_ref[:, sl], axis=-1, keepdims=True), qn8)
        n1_out[:, sl] = w_e[:, sl] * n0_ref[:, sl] + kw
        v_out[:, sl] = v
        qt_out[h] = q.T
        kwt_out[h] = kw.T
    s8 = qk8 * sfac
    a1_out[...] = _expand_lanes(s8, L_F, H_A, DH_A) * v_out[...]
    w1_out[...] = w_e
    den_out[...] = jnp.maximum(jnp.abs(_expand_lanes(s8 + w_inter * qn8, L_F, H_A, DH_A)),
                               jnp.exp(-_expand_lanes(m, L_F, H_A, DH_A)))
    m1_out[...] = m
    g_out[...] = jnp.where(lane_f, w_inter, jnp.where(lane_dta, ea, 0.0))
    zas_out[...] = jax.nn.sigmoid(_dot(hn, wcat_ref[:, d_a:2 * d_a]))

    off_xbc = 2 * d_a + d_b
    xbc = _dot(hn, wcat_ref[:, off_xbc:off_xbc + conv_b])
    xbc_c = (cwb_ref[0:1, :] * convb_ref[:, 0:conv_b] + cwb_ref[1:2, :] * convb_ref[:, conv_b:2 * conv_b]
             + cwb_ref[2:3, :] * convb_ref[:, 2 * conv_b:3 * conv_b] + cwb_ref[3:4, :] * xbc + cbb_ref[...])
    convb_out[:, 0:2 * conv_b] = convb_ref[:, conv_b:3 * conv_b]
    convb_out[:, 2 * conv_b:3 * conv_b] = xbc
    xbc_c = _silu(xbc_c)
    xs = xbc_c[:, 0:d_b]
    bc = xbc_c[:, d_b:conv_b]
    bc_out[...] = bc
    heads_per_group = H_B // G_B
    cbl = jnp.zeros((x.shape[0], LANES), F32)
    for g in range(G_B):
        cb_g = jnp.sum(bc[:, g * N_STATE:(g + 1) * N_STATE]
                       * bc[:, (G_B + g) * N_STATE:(G_B + g + 1) * N_STATE], axis=-1, keepdims=True)
        in_g = jnp.logical_and(lane >= L_DTA + g * heads_per_group,
                               lane < L_DTA + (g + 1) * heads_per_group)
        cbl = jnp.where(in_g, cb_g, cbl)
    dt_e = _expand_lanes(dt, L_DTA, H_B, HD_B)
    y1_out[...] = _expand_lanes(cbl * dt, L_DTA, H_B, HD_B) * xs + dsk_ref[...] * xs
    ea_out[...] = _expand_lanes(ea, L_DTA, H_B, HD_B)
    zbs_out[...] = _silu(_dot(hn, wcat_ref[:, 2 * d_a:2 * d_a + d_b]))
    xw = xs * dt_e
    for pi in range(H_B // 2):
        xwt_out[pi] = xw[:, pi * LANES:(pi + 1) * LANES].T


def _sample_state_kernel(dec_ref, ea_ref, c0_ref, s0_ref, qt_ref, kwt_ref, xwt_ref, v_ref, bc_ref,
                         c1_ref, s1_ref, qc_ref, ysi_ref):
    i = pl.program_id(0)
    bb = c0_ref.shape[0]
    shift = lax.rem(LANES - lax.rem(i * bb, LANES), LANES)
    lane = lax.broadcasted_iota(jnp.int32, (1, LANES), 1)
    top = lax.broadcasted_iota(jnp.int32, (LANES, 1), 0) < HD_B
    heads_per_group = H_B // G_B
    for h in range(H_A):
        sl = slice(h * DH_A, (h + 1) * DH_A)
        qt = pltpu.roll(qt_ref[h], shift, axis=1)
        kwt = pltpu.roll(kwt_ref[h], shift, axis=1)
        for r in range(bb):
            b = i * bb + r
            c0 = c0_ref[r, h]
            dec = dec_ref[b * H_A + h]
            v_row = v_ref[r:r + 1, sl]
            qc_ref[r:r + 1, sl] = jnp.sum(c0 * qt[:, r:r + 1], axis=0, keepdims=True)
            c1_ref[r, h] = dec * c0 + kwt[:, r:r + 1] * v_row
    for pi in range(H_B // 2):
        g = (2 * pi) // heads_per_group
        sl = slice(pi * LANES, (pi + 1) * LANES)
        xwt = pltpu.roll(xwt_ref[pi], shift, axis=1)
        acc = jnp.zeros((LANES, LANES), F32)
        for r in range(bb):
            b = i * bb + r
            s0 = s0_ref[r, pi]
            b_row = bc_ref[r:r + 1, g * N_STATE:(g + 1) * N_STATE]
            c_row = bc_ref[r:r + 1, (G_B + g) * N_STATE:(G_B + g + 1) * N_STATE]
            col = jnp.sum(s0 * c_row, axis=-1, keepdims=True)
            acc = jnp.where(lane == r, col, acc)
            ea_rows = jnp.where(top, ea_ref[b * H_B + 2 * pi], ea_ref[b * H_B + 2 * pi + 1])
            s1_ref[r, pi] = ea_rows * s0 + xwt[:, r:r + 1] * b_row
        ysi_ref[:, sl] = acc.T[0:bb, :]


def _sample_post_kernel(a1_ref, w1_ref, den_ref, y1_ref, ea_ref, zbs_ref, zas_ref, x_ref, qc_ref, ysi_ref,
                        na_ref, nb_ref, wout_ref, hall_ref, hmid_ref, merged):
    del hall_ref
    d_a = H_A * DH_A
    d_b = H_B * HD_B
    hh = (a1_ref[...] + w1_ref[...] * qc_ref[...]) / den_ref[...]
    for h in range(H_A):
        sl = slice(h * DH_A, (h + 1) * DH_A)
        hs = hh[:, sl]
        merged[:, sl] = (hs * _rms_scale(hs) * na_ref[:, sl] * zas_ref[:, sl]).astype(BF16)
    y = (y1_ref[...] + ea_ref[...] * ysi_ref[...]) * zbs_ref[...]
    gw = d_b // G_B
    for g in range(G_B):
        yg = y[:, g * gw:(g + 1) * gw]
        merged[:, d_a + g * gw:d_a + (g + 1) * gw] = (
            yg * _rms_scale(yg) * nb_ref[:, g * gw:(g + 1) * gw]).astype(BF16)
    hmid_ref[...] = x_ref[...] + _dot(merged[...], wout_ref[...])


def _vmem_specs(arrays):
    return [pl.BlockSpec(a.shape, lambda *_, _nd=a.ndim: (0,) * _nd) for a in arrays]


def _sample_mixer(x, c0, n0, m0, conva, s0, convb, p, hmid_all, row_offset):
    nb, d = x.shape
    d_a = H_A * DH_A
    d_b = H_B * HD_B
    conv_b = d_b + 2 * G_B * N_STATE
    row = lambda w: jax.ShapeDtypeStruct((nb, w), F32)
    tile = lambda k: jax.ShapeDtypeStruct((k, LANES, nb), F32)
    pre_in = [x, p["nmix"], p["wcat"], p["bsm"], p["alog"], p["cwa"], p["cba"], p["cwb"], p["cbb"],
              p["wq"], p["wk"], p["wv"], p["dsk"], conva, convb, n0, m0]
    pre_out_shape = (row(3 * d_a), row(3 * conv_b), row(d_a), row(LANES), row(LANES),
                     tile(H_A), tile(H_A), tile(H_B // 2), row(d_a), row(2 * G_B * N_STATE),
                     row(d_a), row(d_a), row(d_a), row(d_b), row(d_b), row(d_b), row(d_a))
    (conva1, convb1, n1, m1, g8, qt, kwt, xwt, v, bc, a1, w1, den, y1, ea_e, zbs, zas) = pl.pallas_call(
        _sample_pre_kernel,
        out_shape=pre_out_shape,
        grid=(1,),
        in_specs=_vmem_specs(pre_in),
        out_specs=tuple(pl.BlockSpec(s.shape, lambda i, _nd=len(s.shape): (0,) * _nd) for s in pre_out_shape),
        compiler_params=pltpu.CompilerParams(
            dimension_semantics=("arbitrary",), vmem_limit_bytes=VMEM_LIMIT),
        name="sample_pre",
    )(*pre_in)

    dec_flat = g8[:, L_F:L_F + H_A].reshape(nb * H_A)
    ea_flat = g8[:, L_DTA:L_DTA + H_B].reshape(nb * H_B)
    bb = SAMPLE_BLOCK
    const3 = lambda k: pl.BlockSpec((k, LANES, nb), lambda i, de, ea: (0, 0, 0))
    state_grid = pltpu.PrefetchScalarGridSpec(
        num_scalar_prefetch=2,
        grid=(nb // bb,),
        in_specs=[pl.BlockSpec((bb, H_A, DH_A, DH_A), lambda i, de, ea: (i, 0, 0, 0)),
                  pl.BlockSpec((bb, H_B // 2, 2 * HD_B, N_STATE), lambda i, de, ea: (i, 0, 0, 0)),
                  const3(H_A), const3(H_A), const3(H_B // 2),
                  pl.BlockSpec((bb, d_a), lambda i, de, ea: (i, 0)),
                  pl.BlockSpec((bb, 2 * G_B * N_STATE), lambda i, de, ea: (i, 0))],
        out_specs=(pl.BlockSpec((bb, H_A, DH_A, DH_A), lambda i, de, ea: (i, 0, 0, 0)),
                   pl.BlockSpec((bb, H_B // 2, 2 * HD_B, N_STATE), lambda i, de, ea: (i, 0, 0, 0)),
                   pl.BlockSpec((bb, d_a), lambda i, de, ea: (i, 0)),
                   pl.BlockSpec((bb, d_b), lambda i, de, ea: (i, 0))),
    )
    c1, s1, qc, ysi = pl.pallas_call(
        _sample_state_kernel,
        out_shape=(jax.ShapeDtypeStruct(c0.shape, F32), jax.ShapeDtypeStruct(s0.shape, F32),
                   row(d_a), row(d_b)),
        grid_spec=state_grid,
        compiler_params=pltpu.CompilerParams(
            dimension_semantics=("arbitrary",), vmem_limit_bytes=VMEM_LIMIT),
        name="sample_state",
    )(dec_flat, ea_flat, c0, s0, qt, kwt, xwt, v, bc)

    post_in = [a1, w1, den, y1, ea_e, zbs, zas, x, qc, ysi, p["na"], p["nb"], p["wout"]]
    hmid_all = pl.pallas_call(
        _sample_post_kernel,
        out_shape=jax.ShapeDtypeStruct(hmid_all.shape, F32),
        grid=(1,),
        in_specs=_vmem_specs(post_in) + [pl.BlockSpec(memory_space=pl.ANY)],
        out_specs=pl.BlockSpec((nb, d), lambda i: (row_offset // nb, 0)),
        scratch_shapes=[pltpu.VMEM((nb, d_a + d_b), BF16)],
        input_output_aliases={len(post_in): 0},
        compiler_params=pltpu.CompilerParams(
            dimension_semantics=("arbitrary",), vmem_limit_bytes=VMEM_LIMIT),
        name="sample_post",
    )(*post_in, hmid_all)
    return hmid_all, c1, n1, m1, conva1, s1, convb1


R_EA, R_EB, R_RA, R_RB, R_GA, R_GB = 0, 1, 2, 3, 4, 5
RL_E = N_EGROUPS


def _router_kernel(h_ref, nf_ref, whi_ref, wmid_ref, br_ref, xn_ref, info_ref, cnt_ref, carry):
    i = pl.program_id(0)
    tr = h_ref.shape[0]

    @pl.when(i == 0)
    def _init():
        carry[...] = jnp.zeros_like(carry)

    h = h_ref[...]
    xn = h * _rms_scale(h) * nf_ref[...]
    _store_token_tiles(xn_ref, xn)
    x_hi, x_mid, _ = _split3(xn)
    logits = (_dot(x_hi, whi_ref[...]) + _dot(x_hi, wmid_ref[...]) + _dot(x_mid, whi_ref[...])
              + br_ref[...])
    lane_i = lax.broadcasted_iota(jnp.int32, (1, LANES), 1)
    lane = lane_i.astype(F32)
    big = float(LANES)

    def first_lane_of(cond):
        return jnp.min(jnp.where(cond, lane, big), axis=-1, keepdims=True)

    l1 = jnp.where(lane_i < N_EGROUPS, logits, NEG_INF)
    e1 = jnp.exp(l1 - jnp.max(l1, axis=-1, keepdims=True))
    p1 = e1 / jnp.sum(e1, axis=-1, keepdims=True)
    gp = jnp.max(p1, axis=-1, keepdims=True)
    gidx = first_lane_of(p1 == gp)
    lo = RL_E + N_EPG * gidx
    l2 = jnp.where(jnp.logical_and(lane >= lo, lane < lo + N_EPG), logits, NEG_INF)
    va = jnp.max(l2, axis=-1, keepdims=True)
    ia = first_lane_of(l2 == va)
    l2b = jnp.where(lane == ia, NEG_INF, l2)
    vb = jnp.max(l2b, axis=-1, keepdims=True)
    ib = first_lane_of(l2b == vb)
    eb = jnp.exp(vb - va)
    wa = 1.0 / (1.0 + eb)
    wb = eb / (1.0 + eb)

    is_a = lane == ia
    is_b = lane == ib
    onehot = jnp.where(jnp.logical_or(is_a, is_b), 1.0, 0.0)
    ri = lax.broadcasted_iota(jnp.int32, (tr, tr), 0)
    ci = lax.broadcasted_iota(jnp.int32, (tr, tr), 1)
    tri = jnp.where(ri >= ci, 1.0, 0.0).astype(BF16)
    incl = _dot(tri, onehot.astype(BF16))
    excl = incl - onehot + carry[...]
    rank_a = jnp.sum(jnp.where(is_a, excl, 0.0), axis=-1, keepdims=True)
    rank_b = jnp.sum(jnp.where(is_b, excl, 0.0), axis=-1, keepdims=True)
    carry[...] = carry[...] + incl[tr - 1:tr, :]
    cnt_ref[...] = carry[...]

    info = jnp.where(lane_i == R_EA, ia - RL_E, 0.0)
    info = jnp.where(lane_i == R_EB, ib - RL_E, info)
    info = jnp.where(lane_i == R_RA, rank_a, info)
    info = jnp.where(lane_i == R_RB, rank_b, info)
    info = jnp.where(lane_i == R_GA, gp * wa, info)
    info = jnp.where(lane_i == R_GB, gp * wb, info)
    info_ref[...] = info


def _row_tile(n, candidates):
    for t in candidates:
        if n % t == 0:
            return t
    raise ValueError(f"no row tile for {n} rows among {candidates}")


def _router(hmid, rp):
    n, d = hmid.shape
    assert d == TOK_TILE_ROWS * LANES
    tr = _row_tile(n, (512, 384, 256, 128))
    return pl.pallas_call(
        _router_kernel,
        out_shape=(jax.ShapeDtypeStruct((n * TOK_TILE_ROWS, LANES), F32),
                   jax.ShapeDtypeStruct((n, LANES), F32),
                   jax.ShapeDtypeStruct((1, LANES), F32)),
        grid=(n // tr,),
        in_specs=[pl.BlockSpec((tr, d), lambda i: (i, 0)),
                  pl.BlockSpec((1, d), lambda i: (0, 0)),
                  pl.BlockSpec((d, LANES), lambda i: (0, 0)),
                  pl.BlockSpec((d, LANES), lambda i: (0, 0)),
                  pl.BlockSpec((1, LANES), lambda i: (0, 0))],
        out_specs=(pl.BlockSpec((tr * TOK_TILE_ROWS, LANES), lambda i: (i, 0)),
                   pl.BlockSpec((tr, LANES), lambda i: (i, 0)),
                   pl.BlockSpec((1, LANES), lambda i: (0, 0))),
        scratch_shapes=[pltpu.VMEM((1, LANES), F32)],
        compiler_params=pltpu.CompilerParams(
            dimension_semantics=("arbitrary",), vmem_limit_bytes=VMEM_LIMIT),
        name="router",
    )(hmid, rp["nf"], rp["whi"], rp["wmid"], rp["br"])


def _prep_router_params(norm_ffn, w_r1, b_r1, w_r2, b_r2):
    d = w_r1.shape[1]
    w = jnp.concatenate([w_r1[0].astype(F32), w_r2[0].reshape(d, N_EXPERTS).astype(F32),
                         jnp.zeros((d, LANES - RL_E - N_EXPERTS), F32)], axis=1)
    whi = w.astype(BF16)
    wmid = (w - whi.astype(F32)).astype(BF16)
    br = jnp.concatenate([b_r1[0].astype(F32), b_r2[0].reshape(N_EXPERTS).astype(F32),
                          jnp.zeros((LANES - RL_E - N_EXPERTS,), F32)])[None, :]
    return dict(nf=norm_ffn[0][None, :].astype(F32), whi=whi, wmid=wmid, br=br)


FFN_TM = 256


def _start_tile_gather(idx_of_row, n_rows, src_hbm, dst, sem):
    for r in range(n_rows):
        start = pl.multiple_of(idx_of_row(r) * TOK_TILE_ROWS, TOK_TILE_ROWS)
        pltpu.make_async_copy(src_hbm.at[pl.ds(start, TOK_TILE_ROWS), :],
                              dst.at[pl.ds(r * TOK_TILE_ROWS, TOK_TILE_ROWS), :],
                              sem).start(priority=r % 2)


def _wait_tile_gather(n_rows, src_hbm, dst, sem):
    pltpu.make_async_copy(src_hbm.at[pl.ds(0, n_rows * TOK_TILE_ROWS), :], dst, sem).wait()


M_NVALID = 0
M_OFF = 1
M_CNT = M_OFF + N_EXPERTS
M_LEN = M_CNT + N_EXPERTS


def _ffn_kernel(te_ref, pa_ref, pb_ref, meta_ref, xn_hbm, wg_ref, wu_ref, wd_ref, ys_ref,
                src, xbuf, wbf, sem, *, n_tokens, tm):
    i = pl.program_id(0)
    n_valid = meta_ref[M_NVALID]
    slot = lax.rem(i, 2)

    @pl.when(i == 0)
    def _build_source_rows():
        for e in range(N_EXPERTS):
            first = meta_ref[M_OFF + e] + meta_ref[M_CNT + e]
            n_pad = lax.rem(tm - lax.rem(meta_ref[M_CNT + e], tm), tm)

            def pad_body(r, carry, first=first):
                src[first + r] = 0
                return carry
            lax.fori_loop(0, n_pad, pad_body, 0)

        def body(t, carry):
            src[pa_ref[t]] = t
            src[pb_ref[t]] = t
            return carry
        lax.fori_loop(0, n_tokens, body, 0, unroll=8)
        _start_tile_gather(lambda r: src[r], tm, xn_hbm, xbuf.at[0], sem.at[0])

    @pl.when(i + 1 < n_valid)
    def _next():
        base = (i + 1) * tm
        _start_tile_gather(lambda r: src[base + r], tm, xn_hbm, xbuf.at[1 - slot], sem.at[1 - slot])

    changed = jnp.logical_or(i == 0, te_ref[i] != te_ref[jnp.maximum(i - 1, 0)])

    @pl.when(jnp.logical_and(changed, i < n_valid))
    def _cast_weights():
        wbf[0] = wg_ref[...].astype(BF16)
        wbf[1] = wu_ref[...].astype(BF16)
        wbf[2] = wd_ref[...].astype(BF16)

    @pl.when(i < n_valid)
    def _compute():
        _wait_tile_gather(tm, xn_hbm, xbuf.at[slot], sem.at[slot])
        x = _load_token_tiles(xbuf.at[slot], tm).astype(BF16)
        hg = _dot(x, wbf[0])
        hu = _dot(x, wbf[1])
        _store_token_tiles(ys_ref, _dot((_silu(hg) * hu).astype(BF16), wbf[2]))

    @pl.when(i >= n_valid)
    def _pad():
        ys_ref[...] = jnp.zeros_like(ys_ref)


def _expert_ffn(xn_tiles, tile_expert, pos_a, pos_b, meta, wg, wu, wd):
    n = pos_a.shape[0]
    n_tiles = tile_expert.shape[0]
    tm = FFN_TM
    d, dff = wg.shape[1], wg.shape[2]
    rows = tm * TOK_TILE_ROWS
    idx = lambda i, te, pa, pb, meta: (te[i], 0, 0)
    grid_spec = pltpu.PrefetchScalarGridSpec(
        num_scalar_prefetch=4,
        grid=(n_tiles,),
        in_specs=[pl.BlockSpec(memory_space=pl.ANY),
                  pl.BlockSpec((None, d, dff), idx),
                  pl.BlockSpec((None, d, dff), idx),
                  pl.BlockSpec((None, dff, d), idx)],
        out_specs=pl.BlockSpec((rows, LANES), lambda i, te, pa, pb, meta: (i, 0)),
        scratch_shapes=[pltpu.SMEM((n_tiles * tm,), jnp.int32),
                        pltpu.VMEM((2, rows, LANES), F32),
                        pltpu.VMEM((3, d, dff), BF16),
                        pltpu.SemaphoreType.DMA((2,))],
    )
    return pl.pallas_call(
        functools.partial(_ffn_kernel, n_tokens=n, tm=tm),
        out_shape=jax.ShapeDtypeStruct((n_tiles * rows, LANES), F32),
        grid_spec=grid_spec,
        compiler_params=pltpu.CompilerParams(
            dimension_semantics=("arbitrary",), vmem_limit_bytes=VMEM_LIMIT),
        name="expert_ffn",
    )(tile_expert, pos_a, pos_b, meta, xn_tiles, wg, wu, wd)


def _positions_kernel(info_ref, cnt_ref, pa_ref, pb_ref, *, tm):
    lane_i = lax.broadcasted_iota(jnp.int32, (1, LANES), 1)
    lane = lane_i.astype(F32)
    cnt = cnt_ref[...]
    padded = jnp.floor((cnt + (tm - 1)) / tm) * tm
    ri = lax.broadcasted_iota(jnp.int32, (LANES, LANES), 0)
    ci = lax.broadcasted_iota(jnp.int32, (LANES, LANES), 1)
    before = jnp.where(ri < ci, 1.0, 0.0).astype(BF16)
    hi, mid, lo = _split3(jnp.broadcast_to(padded, (SUBLANES, LANES)))
    off = ((_dot(hi, before) + _dot(mid, before)) + _dot(lo, before))[0:1, :]
    for g in range(info_ref.shape[0] // LANES):
        blk = info_ref[g * LANES:(g + 1) * LANES, :]
        lane_a = blk[:, R_EA:R_EA + 1] + RL_E
        lane_b = blk[:, R_EB:R_EB + 1] + RL_E
        pos_a = blk[:, R_RA:R_RA + 1] + jnp.sum(jnp.where(lane == lane_a, off, 0.0), axis=-1, keepdims=True)
        pos_b = blk[:, R_RB:R_RB + 1] + jnp.sum(jnp.where(lane == lane_b, off, 0.0), axis=-1, keepdims=True)
        rows = jnp.where(lane_i == 0, pos_a, jnp.where(lane_i == 1, pos_b, 0.0)).T
        pa_ref[g:g + 1, :] = rows[0:1, :].astype(jnp.int32)
        pb_ref[g:g + 1, :] = rows[1:2, :].astype(jnp.int32)


def _routing_tables(info, counts, n_tiles, tm):
    n = info.shape[0]
    groups = n // LANES
    pa, pb = pl.pallas_call(
        functools.partial(_positions_kernel, tm=tm),
        out_shape=(jax.ShapeDtypeStruct((groups, LANES), jnp.int32),
                   jax.ShapeDtypeStruct((groups, LANES), jnp.int32)),
        grid=(1,),
        in_specs=[pl.BlockSpec((n, LANES), lambda i: (0, 0)), pl.BlockSpec((1, LANES), lambda i: (0, 0))],
        out_specs=(pl.BlockSpec((groups, LANES), lambda i: (0, 0)),
                   pl.BlockSpec((groups, LANES), lambda i: (0, 0))),
        compiler_params=pltpu.CompilerParams(
            dimension_semantics=("arbitrary",), vmem_limit_bytes=VMEM_LIMIT),
        name="positions",
    )(info, counts)
    cnt = counts[0, RL_E:RL_E + N_EXPERTS].astype(jnp.int32)
    padded = ((cnt + tm - 1) // tm) * tm
    ends = jnp.cumsum(padded)
    n_valid = ends[-1] // tm
    tile_start = jnp.arange(n_tiles, dtype=jnp.int32) * tm
    te = jnp.sum((jnp.minimum(tile_start, ends[-1] - 1)[:, None] >= ends[None, :]).astype(jnp.int32), axis=1)
    meta = jnp.concatenate([n_valid[None], ends - padded, cnt]).astype(jnp.int32)
    return te, meta, pa.reshape(n), pb.reshape(n)


def _combine_kernel(pa_ref, pb_ref, h_ref, info_ref, ys_hbm, nfin_ref, yp_ref, ysm_ref,
                    buf_a, buf_b, sem, *, n_prompt_tiles):
    i = pl.program_id(0)
    n_steps = pl.num_programs(0)
    tt = h_ref.shape[0]
    slot = lax.rem(i, 2)

    def start(tile, s):
        base = tile * tt
        _start_tile_gather(lambda r: pa_ref[base + r], tt, ys_hbm, buf_a.at[s], sem.at[s])
        _start_tile_gather(lambda r: pb_ref[base + r], tt, ys_hbm, buf_b.at[s], sem.at[s])

    @pl.when(i == 0)
    def _first():
        start(0, 0)

    @pl.when(i + 1 < n_steps)
    def _next():
        start(i + 1, 1 - slot)

    _wait_tile_gather(tt, ys_hbm, buf_a.at[slot], sem.at[slot])
    _wait_tile_gather(tt, ys_hbm, buf_b.at[slot], sem.at[slot])
    info = info_ref[...]
    h = (h_ref[...] + info[:, R_GA:R_GA + 1] * _load_token_tiles(buf_a.at[slot], tt)
         + info[:, R_GB:R_GB + 1] * _load_token_tiles(buf_b.at[slot], tt))
    y = h * _rms_scale(h) * nfin_ref[...]

    @pl.when(i < n_prompt_tiles)
    def _prompt():
        yp_ref[...] = y

    @pl.when(i >= n_prompt_tiles)
    def _sample():
        ysm_ref[...] = y


def _combine(hmid, info, ys, pos_a, pos_b, nfin, n_prompt):
    n, d = hmid.shape
    tt = CHUNK
    n_prompt_tiles = n_prompt // tt
    n_sample = n - n_prompt
    grid_spec = pltpu.PrefetchScalarGridSpec(
        num_scalar_prefetch=2,
        grid=(n // tt,),
        in_specs=[pl.BlockSpec((tt, d), lambda i, pa, pb: (i, 0)),
                  pl.BlockSpec((tt, LANES), lambda i, pa, pb: (i, 0)),
                  pl.BlockSpec(memory_space=pl.ANY),
                  pl.BlockSpec((1, d), lambda i, pa, pb: (0, 0))],
        out_specs=(pl.BlockSpec((tt, d), lambda i, pa, pb: (jnp.minimum(i, n_prompt_tiles - 1), 0)),
                   pl.BlockSpec((tt, d), lambda i, pa, pb: (jnp.maximum(i - n_prompt_tiles, 0), 0))),
        scratch_shapes=[pltpu.VMEM((2, tt * TOK_TILE_ROWS, LANES), F32),
                        pltpu.VMEM((2, tt * TOK_TILE_ROWS, LANES), F32),
                        pltpu.SemaphoreType.DMA((2,))],
    )
    return pl.pallas_call(
        functools.partial(_combine_kernel, n_prompt_tiles=n_prompt_tiles),
        out_shape=(jax.ShapeDtypeStruct((n_prompt, d), F32),
                   jax.ShapeDtypeStruct((n_sample, d), F32)),
        grid_spec=grid_spec,
        compiler_params=pltpu.CompilerParams(
            dimension_semantics=("arbitrary",), vmem_limit_bytes=VMEM_LIMIT),
        name="combine",
    )(pos_a, pos_b, hmid, info, ys, nfin)


def _moe_and_final_norm(hmid, n_prompt, rp, wg, wu, wd, nfin):
    n = hmid.shape[0]
    tm = FFN_TM
    n_tiles = (2 * n + N_EXPERTS * (tm - 1)) // tm
    xn, info, counts = _router(hmid, rp)
    te, meta, pos_a, pos_b = _routing_tables(info, counts, n_tiles, tm)
    ys = _expert_ffn(xn, te, pos_a, pos_b, meta, wg, wu, wd)
    return _combine(hmid, info, ys, pos_a, pos_b, nfin, n_prompt)


def kernel(x_prompt, x_sample, state_mlstm_C, state_mlstm_n, state_mlstm_m, state_mlstm_conv, state_ssm, state_ssm_conv, meta_tokens, norm_mix, w_in, conv_a_w, conv_a_b, w_q, w_k, w_v, b_i, b_f, norm_a, conv_b_w, conv_b_b, dt_bias, a_log, d_skip, norm_b, w_out, norm_ffn, w_r1, b_r1, w_r2, b_r2, w_gate, w_up, w_down, norm_final):
    bsz, seq, d = x_prompt.shape
    nb = x_sample.shape[0]
    d_a = H_A * DH_A
    conv_b = H_B * HD_B + 2 * G_B * N_STATE
    assert w_in.shape[0] == 1 and x_sample.shape[1] == 1 and seq % CHUNK == 0 and nb == CHUNK
    mp = _prep_mixer_params(norm_mix, w_in, conv_a_w, conv_a_b, w_q, w_k, w_v, b_i, b_f, norm_a,
                            conv_b_w, conv_b_b, dt_bias, a_log, d_skip, norm_b, w_out)
    rp = _prep_router_params(norm_ffn, w_r1, b_r1, w_r2, b_r2)
    xmeta = jnp.concatenate([jnp.zeros((CHUNK - N_META, d), F32), meta_tokens.astype(F32)], 0)

    hmid, p_c, p_n, p_m, p_ca, p_s, p_cb = _prompt_mixer(x_prompt.astype(F32), xmeta, mp, nb)
    m0 = jnp.pad(state_mlstm_m[0].astype(F32), ((0, 0), (0, LANES - H_A)))
    hmid, s_c, s_n, s_m, s_ca, s_s, s_cb = _sample_mixer(
        x_sample.reshape(nb, d).astype(F32),
        state_mlstm_C[0].astype(F32),
        state_mlstm_n[0].astype(F32).reshape(nb, d_a),
        m0,
        state_mlstm_conv[0].astype(F32).reshape(nb, (CONV_W - 1) * d_a),
        state_ssm[0].astype(F32).reshape(nb, H_B // 2, 2 * HD_B, N_STATE),
        state_ssm_conv[0].astype(F32).reshape(nb, (CONV_W - 1) * conv_b),
        mp, hmid, bsz * seq)

    y_p, y_s = _moe_and_final_norm(
        hmid, bsz * seq, rp, w_gate[0].astype(F32), w_up[0].astype(F32), w_down[0].astype(F32),
        norm_final[None, :].astype(F32))

    return (y_p.reshape(bsz, seq, d), y_s.reshape(nb, 1, d),
            p_c[None], p_n[None], p_m[:, 0, :H_A][None], p_ca[None],
            p_s.reshape(bsz, H_B, HD_B, N_STATE)[None], p_cb[None],
            s_c[None], s_n.reshape(nb, H_A, DH_A)[None], s_m[:, :H_A][None],
            s_ca.reshape(nb, CONV_W - 1, d_a)[None],
            s_s.reshape(nb, H_B, HD_B, N_STATE)[None],
            s_cb.reshape(nb, CONV_W - 1, conv_b)[None])
```

```python
import functools
import math

import jax
import jax.numpy as jnp
from jax import lax
from jax.experimental import pallas as pl
from jax.experimental.pallas import tpu as pltpu

F32 = jnp.float32
BF16 = jnp.bfloat16

EPS = 1e-6
N_META = 16
CONV_W = 4
CHUNK = 128
H_A = 8
DH_A = 128
H_B = 16
HD_B = 64
N_STATE = 128
G_B = 2
N_EGROUPS = 4
N_EPG = 4
N_EXPERTS = 16
LANES = 128
SUBLANES = 8
CONV_HDR = SUBLANES
VMEM_LIMIT = 56 * 1024 * 1024

L_F = 0
L_DTA = 8
L_I = 24
L_DT = 32

NEG_INF = float("-inf")


def _dot(a, b):
    return jnp.dot(a, b, preferred_element_type=F32)


def _dot_nt(a, b):
    return lax.dot_general(a, b, (((1,), (1,)), ((), ())), preferred_element_type=F32)


def _dot_tn(a, b):
    return lax.dot_general(a, b, (((0,), (0,)), ((), ())), preferred_element_type=F32)


def _split3(x):
    hi = x.astype(BF16)
    r = x - hi.astype(F32)
    mid = r.astype(BF16)
    lo = (r - mid.astype(F32)).astype(BF16)
    return hi, mid, lo


def _silu(x):
    return x * jax.nn.sigmoid(x)


def _softplus_parts(x):
    t = jnp.log1p(jnp.exp(-jnp.abs(x)))
    return jnp.maximum(x, 0.0) + t, jnp.minimum(x, 0.0) - t


def _rms_scale(x):
    return lax.rsqrt(jnp.mean(x * x, axis=-1, keepdims=True) + EPS)


TOK_TILE_ROWS = SUBLANES


def _store_token_tiles(ref, x):
    n = x.shape[0]
    for j in range(TOK_TILE_ROWS):
        ref[pl.ds(j, n, stride=TOK_TILE_ROWS), :] = x[:, j * LANES:(j + 1) * LANES]


def _load_token_tiles(ref, n):
    return jnp.concatenate(
        [ref[pl.ds(j, n, stride=TOK_TILE_ROWS), :] for j in range(TOK_TILE_ROWS)], axis=1)


def _prompt_mixer_kernel(xmeta_ref, xp_ref, nmix_ref, wcat_ref, bsm_ref, alog_ref,
                         cwa_ref, cba_ref, cwb_ref, cbb_ref, wq_ref, wk_ref, wv_ref,
                         na_ref, nb_ref, dsk_ref, wout_ref,
                         hmid_ref, c_ref, n_ref, m_ref, conva_ref, s_ref, convb_ref,
                         xa_buf, xbc_buf, y_buf, merged):
    c = pl.program_id(1)
    T = CHUNK
    d_a = H_A * DH_A
    d_b = H_B * HD_B

    @pl.when(c == 0)
    def _init():
        c_ref[...] = jnp.zeros_like(c_ref)
        n_ref[...] = jnp.zeros_like(n_ref)
        m_ref[...] = jnp.zeros_like(m_ref)
        s_ref[...] = jnp.zeros_like(s_ref)
        xa_buf[0:CONV_HDR, :] = jnp.zeros((CONV_HDR, xa_buf.shape[1]), F32)
        xbc_buf[0:CONV_HDR, :] = jnp.zeros((CONV_HDR, xbc_buf.shape[1]), F32)

    x = jnp.where(c == 0, xmeta_ref[...], xp_ref[...])
    row = lax.broadcasted_iota(jnp.int32, (T, 1), 0)
    valid = jnp.logical_or(c > 0, row >= T - N_META)

    hn = (x * _rms_scale(x) * nmix_ref[...]).astype(BF16)

    lane = lax.broadcasted_iota(jnp.int32, (1, LANES), 1)
    lane_f = lane < L_DTA
    lane_dta = jnp.logical_and(lane >= L_DTA, lane < L_I)
    lane_i = jnp.logical_and(lane >= L_I, lane < L_DT)
    lane_dt = jnp.logical_and(lane >= L_DT, lane < L_DT + H_B)
    pre = _dot(hn, wcat_ref[:, 2 * d_a + d_b + (d_b + 2 * G_B * N_STATE):]) + bsm_ref[...]
    sp, lsig = _softplus_parts(pre)
    a_neg = jnp.where(lane_dta, -jnp.exp(alog_ref[...]), 0.0)
    to_cum = jnp.where(lane_f, lsig, jnp.where(lane_dta, sp * a_neg, 0.0))
    to_cum = jnp.where(valid, to_cum, 0.0)
    ri = lax.broadcasted_iota(jnp.int32, (T, T), 0)
    ci = lax.broadcasted_iota(jnp.int32, (T, T), 1)
    causal = ri >= ci
    tri = jnp.where(causal, 1.0, 0.0).astype(BF16)
    hi, mid, lo = _split3(to_cum)
    cum = _dot(tri, hi) + _dot(tri, mid) + _dot(tri, lo)
    extra = jnp.where(lane_i, jnp.where(valid, pre, NEG_INF),
                      jnp.where(lane_dt, jnp.where(valid, sp, 0.0), 0.0))
    gcol = cum + extra
    grow = gcol.T

    xa = _dot(hn, wcat_ref[:, 0:d_a])
    xa_buf[CONV_HDR:CONV_HDR + T, :] = xa
    xc = (cwa_ref[3:4, :] * xa + cwa_ref[2:3, :] * xa_buf[CONV_HDR - 1:CONV_HDR - 1 + T, :]
          + cwa_ref[1:2, :] * xa_buf[CONV_HDR - 2:CONV_HDR - 2 + T, :]
          + cwa_ref[0:1, :] * xa_buf[CONV_HDR - 3:CONV_HDR - 3 + T, :] + cba_ref[...])
    xa_buf[CONV_HDR - 3:CONV_HDR, :] = xa[T - 3:T, :]
    xc = _silu(xc).astype(BF16)
    xab = xa.astype(BF16)
    za = _dot(hn, wcat_ref[:, d_a:2 * d_a])
    m_all = m_ref[...]
    m_new = m_all
    for h in range(H_A):
        sl = slice(h * DH_A, (h + 1) * DH_A)
        q = _dot(xc[:, sl], wq_ref[h]).astype(BF16)
        k = _dot(xc[:, sl], wk_ref[h]) * (DH_A ** -0.5)
        v = _dot(xab[:, sl], wv_ref[h]).astype(BF16)
        b_col = gcol[:, L_F + h:L_F + h + 1]
        i_col = gcol[:, L_I + h:L_I + h + 1]
        b_row = grow[L_F + h:L_F + h + 1, :]
        i_row = grow[L_I + h:L_I + h + 1, :]
        m0 = m_all[:, h:h + 1]
        dmat = jnp.where(causal, b_col - (b_row - i_row), NEG_INF)
        m_inter = b_col + m0
        m = jnp.maximum(m_inter, jnp.max(dmat, axis=-1, keepdims=True))
        w_inter = jnp.exp(m_inter - m)
        s = _dot_nt(q, k.astype(BF16)) * jnp.exp(dmat - m)
        c0 = c_ref[h]
        n0 = n_ref[h:h + 1, :]
        num = _dot(s.astype(BF16), v) + w_inter * _dot(q, c0.astype(BF16))
        qf = q.astype(F32)
        den = jnp.sum(s, axis=-1, keepdims=True) + w_inter * jnp.sum(qf * n0, axis=-1, keepdims=True)
        hh = num / jnp.maximum(jnp.abs(den), jnp.exp(-m))
        m_last = m[T - 1:T, :]
        b_last = b_col[T - 1:T, :]
        dec = jnp.exp(b_last + m0 - m_last)
        ws = jnp.exp(b_last - b_col + i_col - m_last)
        kw = k * ws
        c_ref[h] = dec * c0 + _dot_tn(kw.astype(BF16), v)
        n_ref[h:h + 1, :] = dec * n0 + jnp.sum(kw, axis=0, keepdims=True)
        m_new = jnp.where(lane == h, m_last, m_new)
        hh = hh * _rms_scale(hh) * na_ref[:, sl]
        merged[:, sl] = (hh * jax.nn.sigmoid(za[:, sl])).astype(BF16)
    m_ref[...] = m_new

    off_xbc = 3 * d_a
    xbc = _dot(hn, wcat_ref[:, off_xbc:off_xbc + d_b + 2 * G_B * N_STATE])
    xbc_buf[CONV_HDR:CONV_HDR + T, :] = xbc
    xbc_c = (cwb_ref[3:4, :] * xbc + cwb_ref[2:3, :] * xbc_buf[CONV_HDR - 1:CONV_HDR - 1 + T, :]
             + cwb_ref[1:2, :] * xbc_buf[CONV_HDR - 2:CONV_HDR - 2 + T, :]
             + cwb_ref[0:1, :] * xbc_buf[CONV_HDR - 3:CONV_HDR - 3 + T, :] + cbb_ref[...])
    xbc_buf[CONV_HDR - 3:CONV_HDR, :] = xbc[T - 3:T, :]
    xbc_c = _silu(xbc_c)
    zb = _dot(hn, wcat_ref[:, 2 * d_a:2 * d_a + d_b])
    left = lane < HD_B
    top = lax.broadcasted_iota(jnp.int32, (LANES, 1), 0) < HD_B
    pairs_per_group = H_B // G_B // 2
    for g in range(G_B):
        bg = xbc_c[:, d_b + g * N_STATE:d_b + (g + 1) * N_STATE].astype(BF16)
        cg = xbc_c[:, d_b + (G_B + g) * N_STATE:d_b + (G_B + g + 1) * N_STATE].astype(BF16)
        cb = _dot_nt(cg, bg)
        for p in range(pairs_per_group):
            pi = g * pairs_per_group + p
            sl = slice(pi * LANES, (pi + 1) * LANES)
            xpair = xbc_c[:, sl]
            xpb = xpair.astype(BF16)
            ys, a_cols, w_cols, a_lasts = [], [], [], []
            for j in (2 * pi, 2 * pi + 1):
                a_col = gcol[:, L_DTA + j:L_DTA + j + 1]
                a_row = grow[L_DTA + j:L_DTA + j + 1, :]
                dt_col = gcol[:, L_DT + j:L_DT + j + 1]
                dt_row = grow[L_DT + j:L_DT + j + 1, :]
                decay = jnp.exp(jnp.where(causal, a_col - a_row, NEG_INF))
                scores = cb * decay * dt_row
                ys.append(_dot(scores.astype(BF16), xpb))
                a_last = a_col[T - 1:T, :]
                a_cols.append(a_col)
                a_lasts.append(a_last)
                w_cols.append(jnp.exp(a_last - a_col) * dt_col)
            s0 = s_ref[pi]
            y = jnp.where(left, ys[0], ys[1])
            ea = jnp.exp(jnp.where(left, a_cols[0], a_cols[1]))
            y = y + ea * _dot_nt(cg, s0.astype(BF16))
            xw = (xpair * jnp.where(left, w_cols[0], w_cols[1])).astype(BF16)
            ea_last = jnp.exp(jnp.where(top, a_lasts[0], a_lasts[1]))
            s_ref[pi] = ea_last * s0 + _dot_tn(xw, bg)
            y = y + dsk_ref[:, sl] * xpair
            y_buf[:, sl] = y * _silu(zb[:, sl])
    gw = d_b // G_B
    for g in range(G_B):
        yg = y_buf[:, g * gw:(g + 1) * gw]
        merged[:, d_a + g * gw:d_a + (g + 1) * gw] = (
            yg * _rms_scale(yg) * nb_ref[:, g * gw:(g + 1) * gw]).astype(BF16)

    @pl.when(c > 0)
    def _out():
        hmid_ref[...] = x + _dot(merged[...], wout_ref[...])

    @pl.when(c == 0)
    def _no_out():
        hmid_ref[...] = jnp.zeros_like(hmid_ref)

    conva_ref[...] = xa_buf[CONV_HDR + T - 3:CONV_HDR + T, :]
    convb_ref[...] = xbc_buf[CONV_HDR + T - 3:CONV_HDR + T, :]


PROMPT_ROWS = 1


def _prompt_rows_kernel(xmeta_ref, xp_ref, nmix_ref, wcat_ref, bsm_ref, alog_ref,
                        cwa_ref, cba_ref, cwb_ref, cbb_ref, wq_ref, wk_ref, wv_ref,
                        na_ref, nb_ref, dsk_ref, wout_ref,
                        hmid_hbm, c_ref, n_ref, m_ref, conva_ref, s_ref, convb_ref,
                        xa_buf, xbc_buf, y_buf, merged, hout, sem, *, seq, n_prompt_rows):
    p = pl.program_id(0)
    c = pl.program_id(1)
    last_p = pl.num_programs(0) - 1
    last_c = pl.num_programs(1) - 1
    T = CHUNK
    RB = xp_ref.shape[0]
    d_a = H_A * DH_A
    d_b = H_B * HD_B
    conv_b = d_b + 2 * G_B * N_STATE

    def out_copy(r, row0):
        return pltpu.make_async_copy(hout.at[r], hmid_hbm.at[pl.ds(row0, T), :], sem.at[r])

    @pl.when(c == 0)
    def _init():
        c_ref[...] = jnp.zeros_like(c_ref)
        n_ref[...] = jnp.zeros_like(n_ref)
        m_ref[...] = jnp.zeros_like(m_ref)
        s_ref[...] = jnp.zeros_like(s_ref)
        xa_buf[:, 0:CONV_HDR, :] = jnp.zeros((RB, CONV_HDR, d_a), F32)
        xbc_buf[:, 0:CONV_HDR, :] = jnp.zeros((RB, CONV_HDR, conv_b), F32)

    @pl.when(jnp.logical_and(p == 0, c == 0))
    def _clear_sample_rows():
        hout[0] = jnp.zeros((T, hout.shape[2]), F32)
        cp = out_copy(0, n_prompt_rows)
        cp.start()
        cp.wait()

    row = lax.broadcasted_iota(jnp.int32, (T, 1), 0)
    valid = jnp.logical_or(c > 0, row >= T - N_META)
    xs_in = [jnp.where(c == 0, xmeta_ref[...], xp_ref[r]) for r in range(RB)]
    x2 = jnp.concatenate(xs_in, axis=0)
    hn = (x2 * _rms_scale(x2) * nmix_ref[...]).astype(BF16)

    lane = lax.broadcasted_iota(jnp.int32, (1, LANES), 1)
    lane_f = lane < L_DTA
    lane_dta = jnp.logical_and(lane >= L_DTA, lane < L_I)
    lane_i = jnp.logical_and(lane >= L_I, lane < L_DT)
    lane_dt = jnp.logical_and(lane >= L_DT, lane < L_DT + H_B)
    ri = lax.broadcasted_iota(jnp.int32, (T, T), 0)
    ci = lax.broadcasted_iota(jnp.int32, (T, T), 1)
    causal = ri >= ci
    tri = jnp.where(causal, 1.0, 0.0).astype(BF16)
    a_neg = jnp.where(lane_dta, -jnp.exp(alog_ref[...]), 0.0)
    left = lane < HD_B
    top = lax.broadcasted_iota(jnp.int32, (LANES, 1), 0) < HD_B

    off_small = 2 * d_a + d_b + conv_b
    pre2 = _dot(hn, wcat_ref[:, off_small:]) + bsm_ref[...]
    xa2 = _dot(hn, wcat_ref[:, 0:d_a])

    gcols, grows, xcs, xabs = [], [], [], []
    for r in range(RB):
        rs = slice(r * T, (r + 1) * T)
        pre = pre2[rs]
        sp, lsig = _softplus_parts(pre)
        to_cum = jnp.where(lane_f, lsig, jnp.where(lane_dta, sp * a_neg, 0.0))
        to_cum = jnp.where(valid, to_cum, 0.0)
        hi, mid, lo = _split3(to_cum)
        cum = _dot(tri, hi) + _dot(tri, mid) + _dot(tri, lo)
        extra = jnp.where(lane_i, jnp.where(valid, pre, NEG_INF),
                          jnp.where(lane_dt, jnp.where(valid, sp, 0.0), 0.0))
        gcol = cum + extra
        gcols.append(gcol)
        grows.append(gcol.T)
        xa = xa2[rs]
        xa_buf[r, CONV_HDR:CONV_HDR + T, :] = xa
        xc = (cwa_ref[3:4, :] * xa + cwa_ref[2:3, :] * xa_buf[r, CONV_HDR - 1:CONV_HDR - 1 + T, :]
              + cwa_ref[1:2, :] * xa_buf[r, CONV_HDR - 2:CONV_HDR - 2 + T, :]
              + cwa_ref[0:1, :] * xa_buf[r, CONV_HDR - 3:CONV_HDR - 3 + T, :] + cba_ref[...])
        xa_buf[r, CONV_HDR - 3:CONV_HDR, :] = xa[T - 3:T, :]
        conva_ref[r] = xa[T - 3:T, :]
        xcs.append(_silu(xc).astype(BF16))
        xabs.append(xa.astype(BF16))

    items = [(r, h) for h in range(H_A) for r in range(RB)]
    hsl = lambda h: slice(h * DH_A, (h + 1) * DH_A)
    m_alls = [m_ref[r] for r in range(RB)]
    m_news = list(m_alls)
    qs, ks, vs, qks, st, dd = {}, {}, {}, {}, {}, {}

    def stage_qkv(it):
        r, h = it
        qs[it] = _dot(xcs[r][:, hsl(h)], wq_ref[h]).astype(BF16)
        ks[it] = _dot(xcs[r][:, hsl(h)], wk_ref[h]) * (DH_A ** -0.5)
        vs[it] = _dot(xabs[r][:, hsl(h)], wv_ref[h]).astype(BF16)

    def stage_qk(it):
        qks[it] = _dot_nt(qs[it], ks[it].astype(BF16))

    xbc2 = _dot(hn, wcat_ref[:, 2 * d_a + d_b:off_small])
    xbcs = []
    for r in range(RB):
        rs = slice(r * T, (r + 1) * T)
        xbc = xbc2[rs]
        xbc_buf[r, CONV_HDR:CONV_HDR + T, :] = xbc
        xbc_c = (cwb_ref[3:4, :] * xbc + cwb_ref[2:3, :] * xbc_buf[r, CONV_HDR - 1:CONV_HDR - 1 + T, :]
                 + cwb_ref[1:2, :] * xbc_buf[r, CONV_HDR - 2:CONV_HDR - 2 + T, :]
                 + cwb_ref[0:1, :] * xbc_buf[r, CONV_HDR - 3:CONV_HDR - 3 + T, :] + cbb_ref[...])
        xbc_buf[r, CONV_HDR - 3:CONV_HDR, :] = xbc[T - 3:T, :]
        convb_ref[r] = xbc[T - 3:T, :]
        xbcs.append(_silu(xbc_c))
    pairs_per_group = H_B // G_B // 2
    groups = [(r, g) for g in range(G_B) for r in range(RB)]
    bgs = {(r, g): xbcs[r][:, d_b + g * N_STATE:d_b + (g + 1) * N_STATE].astype(BF16) for r, g in groups}
    cgs = {(r, g): xbcs[r][:, d_b + (G_B + g) * N_STATE:d_b + (G_B + g + 1) * N_STATE].astype(BF16)
           for r, g in groups}
    cbs = {rg: _dot_nt(cgs[rg], bgs[rg]) for rg in groups}
    def stage_weights(it):
        r, h = it
        gcol, grow = gcols[r], grows[r]
        b_col = gcol[:, L_F + h:L_F + h + 1]
        i_col = gcol[:, L_I + h:L_I + h + 1]
        b_row = grow[L_F + h:L_F + h + 1, :]
        i_row = grow[L_I + h:L_I + h + 1, :]
        m0 = m_alls[r][:, h:h + 1]
        dmat = jnp.where(causal, b_col - (b_row - i_row), NEG_INF)
        m_inter = b_col + m0
        m = jnp.maximum(m_inter, jnp.max(dmat, axis=-1, keepdims=True))
        w_inter = jnp.exp(m_inter - m)
        s = qks[it] * jnp.exp(dmat - m)
        n0 = n_ref[r, h:h + 1, :]
        den = (jnp.sum(s, axis=-1, keepdims=True)
               + w_inter * jnp.sum(qs[it].astype(F32) * n0, axis=-1, keepdims=True))
        m_last = m[T - 1:T, :]
        b_last = b_col[T - 1:T, :]
        dec = jnp.exp(b_last + m0 - m_last)
        kw = ks[it] * jnp.exp(b_last - b_col + i_col - m_last)
        n_ref[r, h:h + 1, :] = dec * n0 + jnp.sum(kw, axis=0, keepdims=True)
        m_news[r] = jnp.where(lane == h, m_last, m_news[r])
        st[it] = (s.astype(BF16), kw.astype(BF16), w_inter,
                  jnp.maximum(jnp.abs(den), jnp.exp(-m)), dec)

    def stage_readout(it):
        r, h = it
        s_b, kw_b, w_inter, den, dec = st[it]
        c0 = c_ref[r, h]
        num = _dot(s_b, vs[it]) + w_inter * _dot(qs[it], c0.astype(BF16))
        c_ref[r, h] = dec * c0 + _dot_tn(kw_b, vs[it])
        dd[it] = num / den

    def stage_head_out(it, za2):
        r, h = it
        hh = dd[it]
        hh = hh * _rms_scale(hh) * na_ref[:, hsl(h)]
        merged[r * T:(r + 1) * T, hsl(h)] = (
            hh * jax.nn.sigmoid(za2[r * T:(r + 1) * T, hsl(h)])).astype(BF16)

    for stage in (stage_qkv, stage_qk, stage_weights, stage_readout):
        for it in items:
            stage(it)
    for r in range(RB):
        m_ref[r] = m_news[r]
    za2 = _dot(hn, wcat_ref[:, d_a:2 * d_a])
    zb2 = _dot(hn, wcat_ref[:, 2 * d_a:2 * d_a + d_b])

    pairs = [(r, pi) for pi in range(H_B // 2) for r in range(RB)]
    psl = lambda pi: slice(pi * LANES, (pi + 1) * LANES)
    sw = {}

    def stage_decay(pr):
        r, pi = pr
        g = pi // pairs_per_group
        gcol, grow = gcols[r], grows[r]
        xpair = xbcs[r][:, psl(pi)]
        scs, a_cols, w_cols, a_lasts = [], [], [], []
        for j in (2 * pi, 2 * pi + 1):
            a_col = gcol[:, L_DTA + j:L_DTA + j + 1]
            a_row = grow[L_DTA + j:L_DTA + j + 1, :]
            dt_col = gcol[:, L_DT + j:L_DT + j + 1]
            dt_row = grow[L_DT + j:L_DT + j + 1, :]
            decay = jnp.exp(jnp.where(causal, a_col - a_row, NEG_INF))
            scs.append((cbs[(r, g)] * decay * dt_row).astype(BF16))
            a_last = a_col[T - 1:T, :]
            a_cols.append(a_col)
            a_lasts.append(a_last)
            w_cols.append(jnp.exp(a_last - a_col) * dt_col)
        sw[pr] = (scs, xpair.astype(BF16),
                  (xpair * jnp.where(left, w_cols[0], w_cols[1])).astype(BF16),
                  jnp.exp(jnp.where(left, a_cols[0], a_cols[1])),
                  jnp.exp(jnp.where(top, a_lasts[0], a_lasts[1])))
    def stage_pair_out(pr):
        r, pi = pr
        g = pi // pairs_per_group
        scs, xpb, xw, ea, ea_last = sw[pr]
        s0 = s_ref[r, pi]
        y = jnp.where(left, _dot(scs[0], xpb), _dot(scs[1], xpb))
        y = y + ea * _dot_nt(cgs[(r, g)], s0.astype(BF16))
        s_ref[r, pi] = ea_last * s0 + _dot_tn(xw, bgs[(r, g)])
        y = y + dsk_ref[:, psl(pi)] * xbcs[r][:, psl(pi)]
        y_buf[r, :, psl(pi)] = y * _silu(zb2[r * T:(r + 1) * T, psl(pi)])

    for it in items:
        stage_head_out(it, za2)
    for stage in (stage_decay, stage_pair_out):
        for pr in pairs:
            stage(pr)
    gw = d_b // G_B
    for r in range(RB):
        for g in range(G_B):
            yg = y_buf[r, :, g * gw:(g + 1) * gw]
            merged[r * T:(r + 1) * T, d_a + g * gw:d_a + (g + 1) * gw] = (
                yg * _rms_scale(yg) * nb_ref[:, g * gw:(g + 1) * gw]).astype(BF16)

    @pl.when(c > 0)
    def _out():
        out2 = x2 + _dot(merged[...], wout_ref[...])

        @pl.when(jnp.logical_or(c > 1, p > 0))
        def _wait_previous():
            for r in range(RB):
                out_copy(r, 0).wait()

        for r in range(RB):
            hout[r] = out2[r * T:(r + 1) * T]
            out_copy(r, (p * RB + r) * seq + (c - 1) * T).start()

        @pl.when(jnp.logical_and(p == last_p, c == last_c))
        def _drain():
            for r in range(RB):
                out_copy(r, 0).wait()


def _const_spec(shape):
    nd = len(shape)
    return pl.BlockSpec(shape, lambda b, c, _nd=nd: (0,) * _nd)


def _prompt_mixer(x_prompt, xmeta, p, n_extra_rows):
    bsz, seq, d = x_prompt.shape
    assert n_extra_rows == CHUNK and seq % CHUNK == 0
    n_chunks = seq // CHUNK + 1
    cps = seq // CHUNK
    d_a = H_A * DH_A
    conv_b = H_B * HD_B + 2 * G_B * N_STATE
    consts = [p["nmix"], p["wcat"], p["bsm"], p["alog"], p["cwa"], p["cba"], p["cwb"], p["cbb"],
              p["wq"], p["wk"], p["wv"], p["na"], p["nb"], p["dsk"], p["wout"]]
    rb = PROMPT_ROWS
    assert bsz % rb == 0
    in_specs = [_const_spec(xmeta.shape),
                pl.BlockSpec((rb, CHUNK, d), lambda b, c: (b, jnp.maximum(c - 1, 0), 0))]
    in_specs += [_const_spec(a.shape) for a in consts]
    out_shape = (
        jax.ShapeDtypeStruct((bsz * seq + n_extra_rows, d), F32),
        jax.ShapeDtypeStruct((bsz, H_A, DH_A, DH_A), F32),
        jax.ShapeDtypeStruct((bsz, H_A, DH_A), F32),
        jax.ShapeDtypeStruct((bsz, 1, LANES), F32),
        jax.ShapeDtypeStruct((bsz, CONV_W - 1, d_a), F32),
        jax.ShapeDtypeStruct((bsz, H_B // 2, 2 * HD_B, N_STATE), F32),
        jax.ShapeDtypeStruct((bsz, CONV_W - 1, conv_b), F32),
    )
    out_specs = (
        pl.BlockSpec(memory_space=pl.ANY),
        pl.BlockSpec((rb, H_A, DH_A, DH_A), lambda b, c: (b, 0, 0, 0)),
        pl.BlockSpec((rb, H_A, DH_A), lambda b, c: (b, 0, 0)),
        pl.BlockSpec((rb, 1, LANES), lambda b, c: (b, 0, 0)),
        pl.BlockSpec((rb, CONV_W - 1, d_a), lambda b, c: (b, 0, 0)),
        pl.BlockSpec((rb, H_B // 2, 2 * HD_B, N_STATE), lambda b, c: (b, 0, 0, 0)),
        pl.BlockSpec((rb, CONV_W - 1, conv_b), lambda b, c: (b, 0, 0)),
    )
    return pl.pallas_call(
        functools.partial(_prompt_rows_kernel, seq=seq, n_prompt_rows=bsz * seq),
        out_shape=out_shape,
        grid=(bsz // rb, n_chunks),
        in_specs=in_specs,
        out_specs=out_specs,
        scratch_shapes=[
            pltpu.VMEM((rb, CONV_HDR + CHUNK, d_a), F32),
            pltpu.VMEM((rb, CONV_HDR + CHUNK, conv_b), F32),
            pltpu.VMEM((rb, CHUNK, H_B * HD_B), F32),
            pltpu.VMEM((rb * CHUNK, d_a + H_B * HD_B), BF16),
            pltpu.VMEM((rb, CHUNK, d), F32),
            pltpu.SemaphoreType.DMA((rb,)),
        ],
        compiler_params=pltpu.CompilerParams(
            dimension_semantics=("arbitrary", "arbitrary"), vmem_limit_bytes=VMEM_LIMIT),
        name="prompt_mixer",
    )(xmeta, x_prompt, *consts)


def _prep_mixer_params(norm_mix, w_in, conv_a_w, conv_a_b, w_q, w_k, w_v, b_i, b_f, norm_a,
                       conv_b_w, conv_b_b, dt_bias, a_log, d_skip, norm_b, w_out):
    d_a = H_A * DH_A
    d_b = H_B * HD_B
    conv_b = d_b + 2 * G_B * N_STATE
    w = w_in[0]
    o = 0
    w_xa = w[:, o:o + d_a]; o += d_a
    w_za = w[:, o:o + d_a]; o += d_a
    w_i = w[:, o:o + H_A]; o += H_A
    w_f = w[:, o:o + H_A]; o += H_A
    w_zb = w[:, o:o + d_b]; o += d_b
    w_xbc = w[:, o:o + conv_b]; o += conv_b
    w_dt = w[:, o:o + H_B]
    pad = jnp.zeros((w.shape[0], LANES - (L_DT + H_B)), w.dtype)
    wcat = jnp.concatenate([w_xa, w_za, w_zb, w_xbc, w_f, w_dt, w_i, w_dt, pad], axis=1).astype(BF16)

    def lanes(parts):
        v = jnp.zeros((1, LANES), F32)
        for off, a in parts:
            v = v.at[0, off:off + a.shape[0]].set(a.astype(F32))
        return v

    return dict(
        nmix=norm_mix[0][None, :].astype(F32),
        wcat=wcat,
        bsm=lanes([(L_F, b_f[0]), (L_DTA, dt_bias[0]), (L_I, b_i[0]), (L_DT, dt_bias[0])]),
        alog=lanes([(L_DTA, a_log[0])]),
        cwa=conv_a_w[0].astype(F32), cba=conv_a_b[0][None, :].astype(F32),
        cwb=conv_b_w[0].astype(F32), cbb=conv_b_b[0][None, :].astype(F32),
        wq=w_q[0].astype(BF16), wk=w_k[0].astype(BF16), wv=w_v[0].astype(BF16),
        na=norm_a[0].reshape(1, d_a).astype(F32), nb=norm_b[0][None, :].astype(F32),
        dsk=jnp.repeat(d_skip[0].astype(F32), HD_B)[None, :],
        wout=w_out[0].astype(BF16),
    )


SAMPLE_BLOCK = 8


def _expand_lanes(vals, first_lane, n_heads, width):
    r = lax.broadcasted_iota(jnp.int32, (LANES, n_heads * width), 0) - first_lane
    c = lax.broadcasted_iota(jnp.int32, (LANES, n_heads * width), 1)
    sel = jnp.logical_and(c >= r * width, c < (r + 1) * width)
    e = jnp.where(sel, 1.0, 0.0).astype(BF16)
    hi, mid, lo = _split3(vals)
    return (_dot(hi, e) + _dot(mid, e)) + _dot(lo, e)


def _sample_pre_kernel(x_ref, nmix_ref, wcat_ref, bsm_ref, alog_ref, cwa_ref, cba_ref, cwb_ref, cbb_ref,
                       wq_ref, wk_ref, wv_ref, dsk_ref, conva_ref, convb_ref, n0_ref, m0_ref,
                       conva_out, convb_out, n1_out, m1_out, g_out, qt_out, kwt_out, xwt_out,
                       v_out, bc_out, a1_out, w1_out, den_out, y1_out, ea_out, zbs_out, zas_out):
    d_a = H_A * DH_A
    d_b = H_B * HD_B
    conv_b = d_b + 2 * G_B * N_STATE
    shift_i = LANES - (L_I - L_F)
    x = x_ref[...]
    hn = (x * _rms_scale(x) * nmix_ref[...]).astype(BF16)
    lane = lax.broadcasted_iota(jnp.int32, (1, LANES), 1)
    lane_f = lane < L_DTA
    lane_dta = jnp.logical_and(lane >= L_DTA, lane < L_I)
    pre = _dot(hn, wcat_ref[:, 2 * d_a + d_b + conv_b:]) + bsm_ref[...]
    sp, lsig = _softplus_parts(pre)
    a_neg = jnp.where(lane_dta, -jnp.exp(alog_ref[...]), 0.0)
    pre_al = pltpu.roll(pre, shift_i, axis=1)
    sp_al = pltpu.roll(sp, shift_i, axis=1)
    m_inter = lsig + m0_ref[...]
    m = jnp.maximum(m_inter, pre_al)
    w_inter = jnp.exp(m_inter - m)
    sfac = jnp.exp(pre_al - m)
    ea = jnp.exp(sp * a_neg)
    dt = sp_al

    xa = _dot(hn, wcat_ref[:, 0:d_a])
    xc = (cwa_ref[0:1, :] * conva_ref[:, 0:d_a] + cwa_ref[1:2, :] * conva_ref[:, d_a:2 * d_a]
          + cwa_ref[2:3, :] * conva_ref[:, 2 * d_a:3 * d_a] + cwa_ref[3:4, :] * xa + cba_ref[...])
    conva_out[:, 0:2 * d_a] = conva_ref[:, d_a:3 * d_a]
    conva_out[:, 2 * d_a:3 * d_a] = xa
    xc = _silu(xc).astype(BF16)
    xab = xa.astype(BF16)
    sf_e = _expand_lanes(sfac, L_F, H_A, DH_A)
    w_e = _expand_lanes(w_inter, L_F, H_A, DH_A)
    qk8 = jnp.zeros((x.shape[0], LANES), F32)
    qn8 = jnp.zeros((x.shape[0], LANES), F32)
    for h in range(H_A):
        sl = slice(h * DH_A, (h + 1) * DH_A)
        q = _dot(xc[:, sl], wq_ref[h])
        k = _dot(xc[:, sl], wk_ref[h]) * (DH_A ** -0.5)
        v = _dot(xab[:, sl], wv_ref[h])
        kw = k * sf_e[:, sl]
        qk8 = jnp.where(lane == h, jnp.sum(q * k, axis=-1, keepdims=True), qk8)
        qn8 = jnp.where(lane == h, jnp.sum(q * n0_ref[:, sl], axis=-1, keepdims=True), qn8)
        n1_out[:, sl] = w_e[:, sl] * n0_ref[:, sl] + kw
        v_out[:, sl] = v
        qt_out[h] = q.T
        kwt_out[h] = kw.T
    s8 = qk8 * sfac
    a1_out[...] = _expand_lanes(s8, L_F, H_A, DH_A) * v_out[...]
    w1_out[...] = w_e
    den_out[...] = jnp.maximum(jnp.abs(_expand_lanes(s8 + w_inter * qn8, L_F, H_A, DH_A)),
                               jnp.exp(-_expand_lanes(m, L_F, H_A, DH_A)))
    m1_out[...] = m
    g_out[...] = jnp.where(lane_f, w_inter, jnp.where(lane_dta, ea, 0.0))
    zas_out[...] = jax.nn.sigmoid(_dot(hn, wcat_ref[:, d_a:2 * d_a]))

    off_xbc = 2 * d_a + d_b
    xbc = _dot(hn, wcat_ref[:, off_xbc:off_xbc + conv_b])
    xbc_c = (cwb_ref[0:1, :] * convb_ref[:, 0:conv_b] + cwb_ref[1:2, :] * convb_ref[:, conv_b:2 * conv_b]
             + cwb_ref[2:3, :] * convb_ref[:, 2 * conv_b:3 * conv_b] + cwb_ref[3:4, :] * xbc + cbb_ref[...])
    convb_out[:, 0:2 * conv_b] = convb_ref[:, conv_b:3 * conv_b]
    convb_out[:, 2 * conv_b:3 * conv_b] = xbc
    xbc_c = _silu(xbc_c)
    xs = xbc_c[:, 0:d_b]
    bc = xbc_c[:, d_b:conv_b]
    bc_out[...] = bc
    heads_per_group = H_B // G_B
    cbl = jnp.zeros((x.shape[0], LANES), F32)
    for g in range(G_B):
        cb_g = jnp.sum(bc[:, g * N_STATE:(g + 1) * N_STATE]
                       * bc[:, (G_B + g) * N_STATE:(G_B + g + 1) * N_STATE], axis=-1, keepdims=True)
        in_g = jnp.logical_and(lane >= L_DTA + g * heads_per_group,
                               lane < L_DTA + (g + 1) * heads_per_group)
        cbl = jnp.where(in_g, cb_g, cbl)
    dt_e = _expand_lanes(dt, L_DTA, H_B, HD_B)
    y1_out[...] = _expand_lanes(cbl * dt, L_DTA, H_B, HD_B) * xs + dsk_ref[...] * xs
    ea_out[...] = _expand_lanes(ea, L_DTA, H_B, HD_B)
    zbs_out[...] = _silu(_dot(hn, wcat_ref[:, 2 * d_a:2 * d_a + d_b]))
    xw = xs * dt_e
    for pi in range(H_B // 2):
        xwt_out[pi] = xw[:, pi * LANES:(pi + 1) * LANES].T


def _sample_state_kernel(dec_ref, ea_ref, c0_ref, s0_ref, qt_ref, kwt_ref, xwt_ref, v_ref, bc_ref,
                         c1_ref, s1_ref, qc_ref, ysi_ref):
    i = pl.program_id(0)
    bb = c0_ref.shape[0]
    shift = lax.rem(LANES - lax.rem(i * bb, LANES), LANES)
    lane = lax.broadcasted_iota(jnp.int32, (1, LANES), 1)
    top = lax.broadcasted_iota(jnp.int32, (LANES, 1), 0) < HD_B
    heads_per_group = H_B // G_B
    for h in range(H_A):
        sl = slice(h * DH_A, (h + 1) * DH_A)
        qt = pltpu.roll(qt_ref[h], shift, axis=1)
        kwt = pltpu.roll(kwt_ref[h], shift, axis=1)
        for r in range(bb):
            b = i * bb + r
            c0 = c0_ref[r, h]
            dec = dec_ref[b * H_A + h]
            v_row = v_ref[r:r + 1, sl]
            qc_ref[r:r + 1, sl] = jnp.sum(c0 * qt[:, r:r + 1], axis=0, keepdims=True)
            c1_ref[r, h] = dec * c0 + kwt[:, r:r + 1] * v_row
    for pi in range(H_B // 2):
        g = (2 * pi) // heads_per_group
        sl = slice(pi * LANES, (pi + 1) * LANES)
        xwt = pltpu.roll(xwt_ref[pi], shift, axis=1)
        acc = jnp.zeros((LANES, LANES), F32)
        for r in range(bb):
            b = i * bb + r
            s0 = s0_ref[r, pi]
            b_row = bc_ref[r:r + 1, g * N_STATE:(g + 1) * N_STATE]
            c_row = bc_ref[r:r + 1, (G_B + g) * N_STATE:(G_B + g + 1) * N_STATE]
            col = jnp.sum(s0 * c_row, axis=-1, keepdims=True)
            acc = jnp.where(lane == r, col, acc)
            ea_rows = jnp.where(top, ea_ref[b * H_B + 2 * pi], ea_ref[b * H_B + 2 * pi + 1])
            s1_ref[r, pi] = ea_rows * s0 + xwt[:, r:r + 1] * b_row
        ysi_ref[:, sl] = acc.T[0:bb, :]


def _sample_post_kernel(a1_ref, w1_ref, den_ref, y1_ref, ea_ref, zbs_ref, zas_ref, x_ref, qc_ref, ysi_ref,
                        na_ref, nb_ref, wout_ref, hall_ref, hmid_ref, merged):
    del hall_ref
    d_a = H_A * DH_A
    d_b = H_B * HD_B
    hh = (a1_ref[...] + w1_ref[...] * qc_ref[...]) / den_ref[...]
    for h in range(H_A):
        sl = slice(h * DH_A, (h + 1) * DH_A)
        hs = hh[:, sl]
        merged[:, sl] = (hs * _rms_scale(hs) * na_ref[:, sl] * zas_ref[:, sl]).astype(BF16)
    y = (y1_ref[...] + ea_ref[...] * ysi_ref[...]) * zbs_ref[...]
    gw = d_b // G_B
    for g in range(G_B):
        yg = y[:, g * gw:(g + 1) * gw]
        merged[:, d_a + g * gw:d_a + (g + 1) * gw] = (
            yg * _rms_scale(yg) * nb_ref[:, g * gw:(g + 1) * gw]).astype(BF16)
    hmid_ref[...] = x_ref[...] + _dot(merged[...], wout_ref[...])


def _vmem_specs(arrays):
    return [pl.BlockSpec(a.shape, lambda *_, _nd=a.ndim: (0,) * _nd) for a in arrays]


def _sample_mixer(x, c0, n0, m0, conva, s0, convb, p, hmid_all, row_offset):
    nb, d = x.shape
    d_a = H_A * DH_A
    d_b = H_B * HD_B
    conv_b = d_b + 2 * G_B * N_STATE
    row = lambda w: jax.ShapeDtypeStruct((nb, w), F32)
    tile = lambda k: jax.ShapeDtypeStruct((k, LANES, nb), F32)
    pre_in = [x, p["nmix"], p["wcat"], p["bsm"], p["alog"], p["cwa"], p["cba"], p["cwb"], p["cbb"],
              p["wq"], p["wk"], p["wv"], p["dsk"], conva, convb, n0, m0]
    pre_out_shape = (row(3 * d_a), row(3 * conv_b), row(d_a), row(LANES), row(LANES),
                     tile(H_A), tile(H_A), tile(H_B // 2), row(d_a), row(2 * G_B * N_STATE),
                     row(d_a), row(d_a), row(d_a), row(d_b), row(d_b), row(d_b), row(d_a))
    (conva1, convb1, n1, m1, g8, qt, kwt, xwt, v, bc, a1, w1, den, y1, ea_e, zbs, zas) = pl.pallas_call(
        _sample_pre_kernel,
        out_shape=pre_out_shape,
        grid=(1,),
        in_specs=_vmem_specs(pre_in),
        out_specs=tuple(pl.BlockSpec(s.shape, lambda i, _nd=len(s.shape): (0,) * _nd) for s in pre_out_shape),
        compiler_params=pltpu.CompilerParams(
            dimension_semantics=("arbitrary",), vmem_limit_bytes=VMEM_LIMIT),
        name="sample_pre",
    )(*pre_in)

    dec_flat = g8[:, L_F:L_F + H_A].reshape(nb * H_A)
    ea_flat = g8[:, L_DTA:L_DTA + H_B].reshape(nb * H_B)
    bb = SAMPLE_BLOCK
    const3 = lambda k: pl.BlockSpec((k, LANES, nb), lambda i, de, ea: (0, 0, 0))
    state_grid = pltpu.PrefetchScalarGridSpec(
        num_scalar_prefetch=2,
        grid=(nb // bb,),
        in_specs=[pl.BlockSpec((bb, H_A, DH_A, DH_A), lambda i, de, ea: (i, 0, 0, 0)),
                  pl.BlockSpec((bb, H_B // 2, 2 * HD_B, N_STATE), lambda i, de, ea: (i, 0, 0, 0)),
                  const3(H_A), const3(H_A), const3(H_B // 2),
                  pl.BlockSpec((bb, d_a), lambda i, de, ea: (i, 0)),
                  pl.BlockSpec((bb, 2 * G_B * N_STATE), lambda i, de, ea: (i, 0))],
        out_specs=(pl.BlockSpec((bb, H_A, DH_A, DH_A), lambda i, de, ea: (i, 0, 0, 0)),
                   pl.BlockSpec((bb, H_B // 2, 2 * HD_B, N_STATE), lambda i, de, ea: (i, 0, 0, 0)),
                   pl.BlockSpec((bb, d_a), lambda i, de, ea: (i, 0)),
                   pl.BlockSpec((bb, d_b), lambda i, de, ea: (i, 0))),
    )
    c1, s1, qc, ysi = pl.pallas_call(
        _sample_state_kernel,
        out_shape=(jax.ShapeDtypeStruct(c0.shape, F32), jax.ShapeDtypeStruct(s0.shape, F32),
                   row(d_a), row(d_b)),
        grid_spec=state_grid,
        compiler_params=pltpu.CompilerParams(
            dimension_semantics=("arbitrary",), vmem_limit_bytes=VMEM_LIMIT),
        name="sample_state",
    )(dec_flat, ea_flat, c0, s0, qt, kwt, xwt, v, bc)

    post_in = [a1, w1, den, y1, ea_e, zbs, zas, x, qc, ysi, p["na"], p["nb"], p["wout"]]
    hmid_all = pl.pallas_call(
        _sample_post_kernel,
        out_shape=jax.ShapeDtypeStruct(hmid_all.shape, F32),
        grid=(1,),
        in_specs=_vmem_specs(post_in) + [pl.BlockSpec(memory_space=pl.ANY)],
        out_specs=pl.BlockSpec((nb, d), lambda i: (row_offset // nb, 0)),
        scratch_shapes=[pltpu.VMEM((nb, d_a + d_b), BF16)],
        input_output_aliases={len(post_in): 0},
        compiler_params=pltpu.CompilerParams(
            dimension_semantics=("arbitrary",), vmem_limit_bytes=VMEM_LIMIT),
        name="sample_post",
    )(*post_in, hmid_all)
    return hmid_all, c1, n1, m1, conva1, s1, convb1


R_EA, R_EB, R_RA, R_RB, R_GA, R_GB = 0, 1, 2, 3, 4, 5
RL_E = N_EGROUPS


def _router_kernel(h_ref, nf_ref, whi_ref, wmid_ref, br_ref, xn_ref, info_ref, cnt_ref, carry):
    i = pl.program_id(0)
    tr = h_ref.shape[0]

    @pl.when(i == 0)
    def _init():
        carry[...] = jnp.zeros_like(carry)

    h = h_ref[...]
    xn = h * _rms_scale(h) * nf_ref[...]
    _store_token_tiles(xn_ref, xn)
    x_hi, x_mid, _ = _split3(xn)
    logits = (_dot(x_hi, whi_ref[...]) + _dot(x_hi, wmid_ref[...]) + _dot(x_mid, whi_ref[...])
              + br_ref[...])
    lane_i = lax.broadcasted_iota(jnp.int32, (1, LANES), 1)
    lane = lane_i.astype(F32)
    big = float(LANES)

    def first_lane_of(cond):
        return jnp.min(jnp.where(cond, lane, big), axis=-1, keepdims=True)

    l1 = jnp.where(lane_i < N_EGROUPS, logits, NEG_INF)
    e1 = jnp.exp(l1 - jnp.max(l1, axis=-1, keepdims=True))
    p1 = e1 / jnp.sum(e1, axis=-1, keepdims=True)
    gp = jnp.max(p1, axis=-1, keepdims=True)
    gidx = first_lane_of(p1 == gp)
    lo = RL_E + N_EPG * gidx
    l2 = jnp.where(jnp.logical_and(lane >= lo, lane < lo + N_EPG), logits, NEG_INF)
    va = jnp.max(l2, axis=-1, keepdims=True)
    ia = first_lane_of(l2 == va)
    l2b = jnp.where(lane == ia, NEG_INF, l2)
    vb = jnp.max(l2b, axis=-1, keepdims=True)
    ib = first_lane_of(l2b == vb)
    eb = jnp.exp(vb - va)
    wa = 1.0 / (1.0 + eb)
    wb = eb / (1.0 + eb)

    is_a = lane == ia
    is_b = lane == ib
    onehot = jnp.where(jnp.logical_or(is_a, is_b), 1.0, 0.0)
    ri = lax.broadcasted_iota(jnp.int32, (tr, tr), 0)
    ci = lax.broadcasted_iota(jnp.int32, (tr, tr), 1)
    tri = jnp.where(ri >= ci, 1.0, 0.0).astype(BF16)
    incl = _dot(tri, onehot.astype(BF16))
    excl = incl - onehot + carry[...]
    rank_a = jnp.sum(jnp.where(is_a, excl, 0.0), axis=-1, keepdims=True)
    rank_b = jnp.sum(jnp.where(is_b, excl, 0.0), axis=-1, keepdims=True)
    carry[...] = carry[...] + incl[tr - 1:tr, :]
    cnt_ref[...] = carry[...]

    info = jnp.where(lane_i == R_EA, ia - RL_E, 0.0)
    info = jnp.where(lane_i == R_EB, ib - RL_E, info)
    info = jnp.where(lane_i == R_RA, rank_a, info)
    info = jnp.where(lane_i == R_RB, rank_b, info)
    info = jnp.where(lane_i == R_GA, gp * wa, info)
    info = jnp.where(lane_i == R_GB, gp * wb, info)
    info_ref[...] = info


def _row_tile(n, candidates):
    for t in candidates:
        if n % t == 0:
            return t
    raise ValueError(f"no row tile for {n} rows among {candidates}")


def _router(hmid, rp):
    n, d = hmid.shape
    assert d == TOK_TILE_ROWS * LANES
    tr = _row_tile(n, (512, 384, 256, 128))
    return pl.pallas_call(
        _router_kernel,
        out_shape=(jax.ShapeDtypeStruct((n * TOK_TILE_ROWS, LANES), F32),
                   jax.ShapeDtypeStruct((n, LANES), F32),
                   jax.ShapeDtypeStruct((1, LANES), F32)),
        grid=(n // tr,),
        in_specs=[pl.BlockSpec((tr, d), lambda i: (i, 0)),
                  pl.BlockSpec((1, d), lambda i: (0, 0)),
                  pl.BlockSpec((d, LANES), lambda i: (0, 0)),
                  pl.BlockSpec((d, LANES), lambda i: (0, 0)),
                  pl.BlockSpec((1, LANES), lambda i: (0, 0))],
        out_specs=(pl.BlockSpec((tr * TOK_TILE_ROWS, LANES), lambda i: (i, 0)),
                   pl.BlockSpec((tr, LANES), lambda i: (i, 0)),
                   pl.BlockSpec((1, LANES), lambda i: (0, 0))),
        scratch_shapes=[pltpu.VMEM((1, LANES), F32)],
        compiler_params=pltpu.CompilerParams(
            dimension_semantics=("arbitrary",), vmem_limit_bytes=VMEM_LIMIT),
        name="router",
    )(hmid, rp["nf"], rp["whi"], rp["wmid"], rp["br"])


def _prep_router_params(norm_ffn, w_r1, b_r1, w_r2, b_r2):
    d = w_r1.shape[1]
    w = jnp.concatenate([w_r1[0].astype(F32), w_r2[0].reshape(d, N_EXPERTS).astype(F32),
                         jnp.zeros((d, LANES - RL_E - N_EXPERTS), F32)], axis=1)
    whi = w.astype(BF16)
    wmid = (w - whi.astype(F32)).astype(BF16)
    br = jnp.concatenate([b_r1[0].astype(F32), b_r2[0].reshape(N_EXPERTS).astype(F32),
                          jnp.zeros((LANES - RL_E - N_EXPERTS,), F32)])[None, :]
    return dict(nf=norm_ffn[0][None, :].astype(F32), whi=whi, wmid=wmid, br=br)


FFN_TM = 256


def _start_tile_gather(idx_of_row, n_rows, src_hbm, dst, sem):
    for r in range(n_rows):
        start = pl.multiple_of(idx_of_row(r) * TOK_TILE_ROWS, TOK_TILE_ROWS)
        pltpu.make_async_copy(src_hbm.at[pl.ds(start, TOK_TILE_ROWS), :],
                              dst.at[pl.ds(r * TOK_TILE_ROWS, TOK_TILE_ROWS), :],
                              sem).start(priority=r % 2)


def _wait_tile_gather(n_rows, src_hbm, dst, sem):
    pltpu.make_async_copy(src_hbm.at[pl.ds(0, n_rows * TOK_TILE_ROWS), :], dst, sem).wait()


M_NVALID = 0
M_OFF = 1
M_CNT = M_OFF + N_EXPERTS
M_LEN = M_CNT + N_EXPERTS


def _ffn_kernel(te_ref, pa_ref, pb_ref, meta_ref, xn_hbm, wg_ref, wu_ref, wd_ref, ys_ref,
                src, xbuf, wbf, sem, *, n_tokens, tm):
    i = pl.program_id(0)
    n_valid = meta_ref[M_NVALID]
    slot = lax.rem(i, 2)

    @pl.when(i == 0)
    def _build_source_rows():
        for e in range(N_EXPERTS):
            first = meta_ref[M_OFF + e] + meta_ref[M_CNT + e]
            n_pad = lax.rem(tm - lax.rem(meta_ref[M_CNT + e], tm), tm)

            def pad_body(r, carry, first=first):
                src[first + r] = 0
                return carry
            lax.fori_loop(0, n_pad, pad_body, 0)

        def body(t, carry):
            src[pa_ref[t]] = t
            src[pb_ref[t]] = t
            return carry
        lax.fori_loop(0, n_tokens, body, 0, unroll=8)
        _start_tile_gather(lambda r: src[r], tm, xn_hbm, xbuf.at[0], sem.at[0])

    @pl.when(i + 1 < n_valid)
    def _next():
        base = (i + 1) * tm
        _start_tile_gather(lambda r: src[base + r], tm, xn_hbm, xbuf.at[1 - slot], sem.at[1 - slot])

    changed = jnp.logical_or(i == 0, te_ref[i] != te_ref[jnp.maximum(i - 1, 0)])

    @pl.when(jnp.logical_and(changed, i < n_valid))
    def _cast_weights():
        wbf[0] = wg_ref[...].astype(BF16)
        wbf[1] = wu_ref[...].astype(BF16)
        wbf[2] = wd_ref[...].astype(BF16)

    @pl.when(i < n_valid)
    def _compute():
        _wait_tile_gather(tm, xn_hbm, xbuf.at[slot], sem.at[slot])
        x = _load_token_tiles(xbuf.at[slot], tm).astype(BF16)
        hg = _dot(x, wbf[0])
        hu = _dot(x, wbf[1])
        _store_token_tiles(ys_ref, _dot((_silu(hg) * hu).astype(BF16), wbf[2]))

    @pl.when(i >= n_valid)
    def _pad():
        ys_ref[...] = jnp.zeros_like(ys_ref)


def _expert_ffn(xn_tiles, tile_expert, pos_a, pos_b, meta, wg, wu, wd):
    n = pos_a.shape[0]
    n_tiles = tile_expert.shape[0]
    tm = FFN_TM
    d, dff = wg.shape[1], wg.shape[2]
    rows = tm * TOK_TILE_ROWS
    idx = lambda i, te, pa, pb, meta: (te[i], 0, 0)
    grid_spec = pltpu.PrefetchScalarGridSpec(
        num_scalar_prefetch=4,
        grid=(n_tiles,),
        in_specs=[pl.BlockSpec(memory_space=pl.ANY),
                  pl.BlockSpec((None, d, dff), idx),
                  pl.BlockSpec((None, d, dff), idx),
                  pl.BlockSpec((None, dff, d), idx)],
        out_specs=pl.BlockSpec((rows, LANES), lambda i, te, pa, pb, meta: (i, 0)),
        scratch_shapes=[pltpu.SMEM((n_tiles * tm,), jnp.int32),
                        pltpu.VMEM((2, rows, LANES), F32),
                        pltpu.VMEM((3, d, dff), BF16),
                        pltpu.SemaphoreType.DMA((2,))],
    )
    return pl.pallas_call(
        functools.partial(_ffn_kernel, n_tokens=n, tm=tm),
        out_shape=jax.ShapeDtypeStruct((n_tiles * rows, LANES), F32),
        grid_spec=grid_spec,
        compiler_params=pltpu.CompilerParams(
            dimension_semantics=("arbitrary",), vmem_limit_bytes=VMEM_LIMIT),
        name="expert_ffn",
    )(tile_expert, pos_a, pos_b, meta, xn_tiles, wg, wu, wd)


def _positions_kernel(info_ref, cnt_ref, pa_ref, pb_ref, *, tm):
    lane_i = lax.broadcasted_iota(jnp.int32, (1, LANES), 1)
    lane = lane_i.astype(F32)
    cnt = cnt_ref[...]
    padded = jnp.floor((cnt + (tm - 1)) / tm) * tm
    ri = lax.broadcasted_iota(jnp.int32, (LANES, LANES), 0)
    ci = lax.broadcasted_iota(jnp.int32, (LANES, LANES), 1)
    before = jnp.where(ri < ci, 1.0, 0.0).astype(BF16)
    hi, mid, lo = _split3(jnp.broadcast_to(padded, (SUBLANES, LANES)))
    off = ((_dot(hi, before) + _dot(mid, before)) + _dot(lo, before))[0:1, :]
    for g in range(info_ref.shape[0] // LANES):
        blk = info_ref[g * LANES:(g + 1) * LANES, :]
        lane_a = blk[:, R_EA:R_EA + 1] + RL_E
        lane_b = blk[:, R_EB:R_EB + 1] + RL_E
        pos_a = blk[:, R_RA:R_RA + 1] + jnp.sum(jnp.where(lane == lane_a, off, 0.0), axis=-1, keepdims=True)
        pos_b = blk[:, R_RB:R_RB + 1] + jnp.sum(jnp.where(lane == lane_b, off, 0.0), axis=-1, keepdims=True)
        rows = jnp.where(lane_i == 0, pos_a, jnp.where(lane_i == 1, pos_b, 0.0)).T
        pa_ref[g:g + 1, :] = rows[0:1, :].astype(jnp.int32)
        pb_ref[g:g + 1, :] = rows[1:2, :].astype(jnp.int32)


def _routing_tables(info, counts, n_tiles, tm):
    n = info.shape[0]
    groups = n // LANES
    pa, pb = pl.pallas_call(
        functools.partial(_positions_kernel, tm=tm),
        out_shape=(jax.ShapeDtypeStruct((groups, LANES), jnp.int32),
                   jax.ShapeDtypeStruct((groups, LANES), jnp.int32)),
        grid=(1,),
        in_specs=[pl.BlockSpec((n, LANES), lambda i: (0, 0)), pl.BlockSpec((1, LANES), lambda i: (0, 0))],
        out_specs=(pl.BlockSpec((groups, LANES), lambda i: (0, 0)),
                   pl.BlockSpec((groups, LANES), lambda i: (0, 0))),
        compiler_params=pltpu.CompilerParams(
            dimension_semantics=("arbitrary",), vmem_limit_bytes=VMEM_LIMIT),
        name="positions",
    )(info, counts)
    cnt = counts[0, RL_E:RL_E + N_EXPERTS].astype(jnp.int32)
    padded = ((cnt + tm - 1) // tm) * tm
    ends = jnp.cumsum(padded)
    n_valid = ends[-1] // tm
    tile_start = jnp.arange(n_tiles, dtype=jnp.int32) * tm
    te = jnp.sum((jnp.minimum(tile_start, ends[-1] - 1)[:, None] >= ends[None, :]).astype(jnp.int32), axis=1)
    meta = jnp.concatenate([n_valid[None], ends - padded, cnt]).astype(jnp.int32)
    return te, meta, pa.reshape(n), pb.reshape(n)


def _combine_kernel(pa_ref, pb_ref, h_ref, info_ref, ys_hbm, nfin_ref, yp_ref, ysm_ref,
                    buf_a, buf_b, sem, *, n_prompt_tiles):
    i = pl.program_id(0)
    n_steps = pl.num_programs(0)
    tt = h_ref.shape[0]
    slot = lax.rem(i, 2)

    def start(tile, s):
        base = tile * tt
        _start_tile_gather(lambda r: pa_ref[base + r], tt, ys_hbm, buf_a.at[s], sem.at[s])
        _start_tile_gather(lambda r: pb_ref[base + r], tt, ys_hbm, buf_b.at[s], sem.at[s])

    @pl.when(i == 0)
    def _first():
        start(0, 0)

    @pl.when(i + 1 < n_steps)
    def _next():
        start(i + 1, 1 - slot)

    _wait_tile_gather(tt, ys_hbm, buf_a.at[slot], sem.at[slot])
    _wait_tile_gather(tt, ys_hbm, buf_b.at[slot], sem.at[slot])
    info = info_ref[...]
    h = (h_ref[...] + info[:, R_GA:R_GA + 1] * _load_token_tiles(buf_a.at[slot], tt)
         + info[:, R_GB:R_GB + 1] * _load_token_tiles(buf_b.at[slot], tt))
    y = h * _rms_scale(h) * nfin_ref[...]

    @pl.when(i < n_prompt_tiles)
    def _prompt():
        yp_ref[...] = y

    @pl.when(i >= n_prompt_tiles)
    def _sample():
        ysm_ref[...] = y


def _combine(hmid, info, ys, pos_a, pos_b, nfin, n_prompt):
    n, d = hmid.shape
    tt = CHUNK
    n_prompt_tiles = n_prompt // tt
    n_sample = n - n_prompt
    grid_spec = pltpu.PrefetchScalarGridSpec(
        num_scalar_prefetch=2,
        grid=(n // tt,),
        in_specs=[pl.BlockSpec((tt, d), lambda i, pa, pb: (i, 0)),
                  pl.BlockSpec((tt, LANES), lambda i, pa, pb: (i, 0)),
                  pl.BlockSpec(memory_space=pl.ANY),
                  pl.BlockSpec((1, d), lambda i, pa, pb: (0, 0))],
        out_specs=(pl.BlockSpec((tt, d), lambda i, pa, pb: (jnp.minimum(i, n_prompt_tiles - 1), 0)),
                   pl.BlockSpec((tt, d), lambda i, pa, pb: (jnp.maximum(i - n_prompt_tiles, 0), 0))),
        scratch_shapes=[pltpu.VMEM((2, tt * TOK_TILE_ROWS, LANES), F32),
                        pltpu.VMEM((2, tt * TOK_TILE_ROWS, LANES), F32),
                        pltpu.SemaphoreType.DMA((2,))],
    )
    return pl.pallas_call(
        functools.partial(_combine_kernel, n_prompt_tiles=n_prompt_tiles),
        out_shape=(jax.ShapeDtypeStruct((n_prompt, d), F32),
                   jax.ShapeDtypeStruct((n_sample, d), F32)),
        grid_spec=grid_spec,
        compiler_params=pltpu.CompilerParams(
            dimension_semantics=("arbitrary",), vmem_limit_bytes=VMEM_LIMIT),
        name="combine",
    )(pos_a, pos_b, hmid, info, ys, nfin)


def _moe_and_final_norm(hmid, n_prompt, rp, wg, wu, wd, nfin):
    n = hmid.shape[0]
    tm = FFN_TM
    n_tiles = (2 * n + N_EXPERTS * (tm - 1)) // tm
    xn, info, counts = _router(hmid, rp)
    te, meta, pos_a, pos_b = _routing_tables(info, counts, n_tiles, tm)
    ys = _expert_ffn(xn, te, pos_a, pos_b, meta, wg, wu, wd)
    return _combine(hmid, info, ys, pos_a, pos_b, nfin, n_prompt)


def kernel(x_prompt, x_sample, state_mlstm_C, state_mlstm_n, state_mlstm_m, state_mlstm_conv, state_ssm, state_ssm_conv, meta_tokens, norm_mix, w_in, conv_a_w, conv_a_b, w_q, w_k, w_v, b_i, b_f, norm_a, conv_b_w, conv_b_b, dt_bias, a_log, d_skip, norm_b, w_out, norm_ffn, w_r1, b_r1, w_r2, b_r2, w_gate, w_up, w_down, norm_final):
    bsz, seq, d = x_prompt.shape
    nb = x_sample.shape[0]
    d_a = H_A * DH_A
    conv_b = H_B * HD_B + 2 * G_B * N_STATE
    assert w_in.shape[0] == 1 and x_sample.shape[1] == 1 and seq % CHUNK == 0 and nb == CHUNK
    mp = _prep_mixer_params(norm_mix, w_in, conv_a_w, conv_a_b, w_q, w_k, w_v, b_i, b_f, norm_a,
                            conv_b_w, conv_b_b, dt_bias, a_log, d_skip, norm_b, w_out)
    rp = _prep_router_params(norm_ffn, w_r1, b_r1, w_r2, b_r2)
    xmeta = jnp.concatenate([jnp.zeros((CHUNK - N_META, d), F32), meta_tokens.astype(F32)], 0)

    hmid, p_c, p_n, p_m, p_ca, p_s, p_cb = _prompt_mixer(x_prompt.astype(F32), xmeta, mp, nb)
    m0 = jnp.pad(state_mlstm_m[0].astype(F32), ((0, 0), (0, LANES - H_A)))
    hmid, s_c, s_n, s_m, s_ca, s_s, s_cb = _sample_mixer(
        x_sample.reshape(nb, d).astype(F32),
        state_mlstm_C[0].astype(F32),
        state_mlstm_n[0].astype(F32).reshape(nb, d_a),
        m0,
        state_mlstm_conv[0].astype(F32).reshape(nb, (CONV_W - 1) * d_a),
        state_ssm[0].astype(F32).reshape(nb, H_B // 2, 2 * HD_B, N_STATE),
        state_ssm_conv[0].astype(F32).reshape(nb, (CONV_W - 1) * conv_b),
        mp, hmid, bsz * seq)

    y_p, y_s = _moe_and_final_norm(
        hmid, bsz * seq, rp, w_gate[0].astype(F32), w_up[0].astype(F32), w_down[0].astype(F32),
        norm_final[None, :].astype(F32))

    return (y_p.reshape(bsz, seq, d), y_s.reshape(nb, 1, d),
            p_c[None], p_n[None], p_m[:, 0, :H_A][None], p_ca[None],
            p_s.reshape(bsz, H_B, HD_B, N_STATE)[None], p_cb[None],
            s_c[None], s_n.reshape(nb, H_A, DH_A)[None], s_m[:, :H_A][None],
            s_ca.reshape(nb, CONV_W - 1, d_a)[None],
            s_s.reshape(nb, H_B, HD_B, N_STATE)[None],
            s_cb.reshape(nb, CONV_W - 1, conv_b)[None])
```

```python
import functools
import math

import jax
import jax.numpy as jnp
from jax import lax
from jax.experimental import pallas as pl
from jax.experimental.pallas import tpu as pltpu

F32 = jnp.float32
BF16 = jnp.bfloat16

EPS = 1e-6
N_META = 16
CONV_W = 4
CHUNK = 128
H_A = 8
DH_A = 128
H_B = 16
HD_B = 64
N_STATE = 128
G_B = 2
N_EGROUPS = 4
N_EPG = 4
N_EXPERTS = 16
LANES = 128
SUBLANES = 8
CONV_HDR = SUBLANES
VMEM_LIMIT = 56 * 1024 * 1024

L_F = 0
L_DTA = 8
L_I = 24
L_DT = 32

NEG_INF = float("-inf")


def _dot(a, b):
    return jnp.dot(a, b, preferred_element_type=F32)


def _dot_nt(a, b):
    return lax.dot_general(a, b, (((1,), (1,)), ((), ())), preferred_element_type=F32)


def _dot_tn(a, b):
    return lax.dot_general(a, b, (((0,), (0,)), ((), ())), preferred_element_type=F32)


def _split3(x):
    hi = x.astype(BF16)
    r = x - hi.astype(F32)
    mid = r.astype(BF16)
    lo = (r - mid.astype(F32)).astype(BF16)
    return hi, mid, lo


def _silu(x):
    return x * jax.nn.sigmoid(x)


def _softplus_parts(x):
    t = jnp.log1p(jnp.exp(-jnp.abs(x)))
    return jnp.maximum(x, 0.0) + t, jnp.minimum(x, 0.0) - t


def _rms_scale(x):
    return lax.rsqrt(jnp.mean(x * x, axis=-1, keepdims=True) + EPS)


TOK_TILE_ROWS = SUBLANES


def _store_token_tiles(ref, x):
    n = x.shape[0]
    for j in range(TOK_TILE_ROWS):
        ref[pl.ds(j, n, stride=TOK_TILE_ROWS), :] = x[:, j * LANES:(j + 1) * LANES]


def _causal_conv(x, tail, w_ref, b_ref):
    n_tail = tail.shape[0]
    row = lax.broadcasted_iota(jnp.int32, (n_tail, 1), 0)
    acc = w_ref[CONV_W - 1:CONV_W, :] * x + b_ref[...]
    for k in range(1, CONV_W):
        rolled = pltpu.roll(x, k, axis=0)
        head = jnp.where(row < k, pltpu.roll(tail, k, axis=0), rolled[0:n_tail])
        shifted = jnp.concatenate([head, rolled[n_tail:]], axis=0)
        acc = acc + w_ref[CONV_W - 1 - k:CONV_W - k, :] * shifted
    return acc


def _load_token_tiles(ref, n):
    return jnp.concatenate(
        [ref[pl.ds(j, n, stride=TOK_TILE_ROWS), :] for j in range(TOK_TILE_ROWS)], axis=1)


def _prompt_mixer_kernel(xmeta_ref, xp_ref, nmix_ref, wcat_ref, bsm_ref, alog_ref,
                         cwa_ref, cba_ref, cwb_ref, cbb_ref, wq_ref, wk_ref, wv_ref,
                         na_ref, nb_ref, dsk_ref, wout_ref,
                         hmid_ref, c_ref, n_ref, m_ref, conva_ref, s_ref, convb_ref,
                         xa_buf, xbc_buf, y_buf, merged):
    c = pl.program_id(1)
    T = CHUNK
    d_a = H_A * DH_A
    d_b = H_B * HD_B

    @pl.when(c == 0)
    def _init():
        c_ref[...] = jnp.zeros_like(c_ref)
        n_ref[...] = jnp.zeros_like(n_ref)
        m_ref[...] = jnp.zeros_like(m_ref)
        s_ref[...] = jnp.zeros_like(s_ref)
        xa_buf[0:CONV_HDR, :] = jnp.zeros((CONV_HDR, xa_buf.shape[1]), F32)
        xbc_buf[0:CONV_HDR, :] = jnp.zeros((CONV_HDR, xbc_buf.shape[1]), F32)

    x = jnp.where(c == 0, xmeta_ref[...], xp_ref[...])
    row = lax.broadcasted_iota(jnp.int32, (T, 1), 0)
    valid = jnp.logical_or(c > 0, row >= T - N_META)

    hn = (x * _rms_scale(x) * nmix_ref[...]).astype(BF16)

    lane = lax.broadcasted_iota(jnp.int32, (1, LANES), 1)
    lane_f = lane < L_DTA
    lane_dta = jnp.logical_and(lane >= L_DTA, lane < L_I)
    lane_i = jnp.logical_and(lane >= L_I, lane < L_DT)
    lane_dt = jnp.logical_and(lane >= L_DT, lane < L_DT + H_B)
    pre = _dot(hn, wcat_ref[:, 2 * d_a + d_b + (d_b + 2 * G_B * N_STATE):]) + bsm_ref[...]
    sp, lsig = _softplus_parts(pre)
    a_neg = jnp.where(lane_dta, -jnp.exp(alog_ref[...]), 0.0)
    to_cum = jnp.where(lane_f, lsig, jnp.where(lane_dta, sp * a_neg, 0.0))
    to_cum = jnp.where(valid, to_cum, 0.0)
    ri = lax.broadcasted_iota(jnp.int32, (T, T), 0)
    ci = lax.broadcasted_iota(jnp.int32, (T, T), 1)
    causal = ri >= ci
    tri = jnp.where(causal, 1.0, 0.0).astype(BF16)
    hi, mid, lo = _split3(to_cum)
    cum = _dot(tri, hi) + _dot(tri, mid) + _dot(tri, lo)
    extra = jnp.where(lane_i, jnp.where(valid, pre, NEG_INF),
                      jnp.where(lane_dt, jnp.where(valid, sp, 0.0), 0.0))
    gcol = cum + extra
    grow = gcol.T

    xa = _dot(hn, wcat_ref[:, 0:d_a])
    xa_buf[CONV_HDR:CONV_HDR + T, :] = xa
    xc = (cwa_ref[3:4, :] * xa + cwa_ref[2:3, :] * xa_buf[CONV_HDR - 1:CONV_HDR - 1 + T, :]
          + cwa_ref[1:2, :] * xa_buf[CONV_HDR - 2:CONV_HDR - 2 + T, :]
          + cwa_ref[0:1, :] * xa_buf[CONV_HDR - 3:CONV_HDR - 3 + T, :] + cba_ref[...])
    xa_buf[CONV_HDR - 3:CONV_HDR, :] = xa[T - 3:T, :]
    xc = _silu(xc).astype(BF16)
    xab = xa.astype(BF16)
    za = _dot(hn, wcat_ref[:, d_a:2 * d_a])
    m_all = m_ref[...]
    m_new = m_all
    for h in range(H_A):
        sl = slice(h * DH_A, (h + 1) * DH_A)
        q = _dot(xc[:, sl], wq_ref[h]).astype(BF16)
        k = _dot(xc[:, sl], wk_ref[h]) * (DH_A ** -0.5)
        v = _dot(xab[:, sl], wv_ref[h]).astype(BF16)
        b_col = gcol[:, L_F + h:L_F + h + 1]
        i_col = gcol[:, L_I + h:L_I + h + 1]
        b_row = grow[L_F + h:L_F + h + 1, :]
        i_row = grow[L_I + h:L_I + h + 1, :]
        m0 = m_all[:, h:h + 1]
        dmat = jnp.where(causal, b_col - (b_row - i_row), NEG_INF)
        m_inter = b_col + m0
        m = jnp.maximum(m_inter, jnp.max(dmat, axis=-1, keepdims=True))
        w_inter = jnp.exp(m_inter - m)
        s = _dot_nt(q, k.astype(BF16)) * jnp.exp(dmat - m)
        c0 = c_ref[h]
        n0 = n_ref[h:h + 1, :]
        num = _dot(s.astype(BF16), v) + w_inter * _dot(q, c0.astype(BF16))
        qf = q.astype(F32)
        den = jnp.sum(s, axis=-1, keepdims=True) + w_inter * jnp.sum(qf * n0, axis=-1, keepdims=True)
        hh = num / jnp.maximum(jnp.abs(den), jnp.exp(-m))
        m_last = m[T - 1:T, :]
        b_last = b_col[T - 1:T, :]
        dec = jnp.exp(b_last + m0 - m_last)
        ws = jnp.exp(b_last - b_col + i_col - m_last)
        kw = k * ws
        c_ref[h] = dec * c0 + _dot_tn(kw.astype(BF16), v)
        n_ref[h:h + 1, :] = dec * n0 + jnp.sum(kw, axis=0, keepdims=True)
        m_new = jnp.where(lane == h, m_last, m_new)
        hh = hh * _rms_scale(hh) * na_ref[:, sl]
        merged[:, sl] = (hh * jax.nn.sigmoid(za[:, sl])).astype(BF16)
    m_ref[...] = m_new

    off_xbc = 3 * d_a
    xbc = _dot(hn, wcat_ref[:, off_xbc:off_xbc + d_b + 2 * G_B * N_STATE])
    xbc_buf[CONV_HDR:CONV_HDR + T, :] = xbc
    xbc_c = (cwb_ref[3:4, :] * xbc + cwb_ref[2:3, :] * xbc_buf[CONV_HDR - 1:CONV_HDR - 1 + T, :]
             + cwb_ref[1:2, :] * xbc_buf[CONV_HDR - 2:CONV_HDR - 2 + T, :]
             + cwb_ref[0:1, :] * xbc_buf[CONV_HDR - 3:CONV_HDR - 3 + T, :] + cbb_ref[...])
    xbc_buf[CONV_HDR - 3:CONV_HDR, :] = xbc[T - 3:T, :]
    xbc_c = _silu(xbc_c)
    zb = _dot(hn, wcat_ref[:, 2 * d_a:2 * d_a + d_b])
    left = lane < HD_B
    top = lax.broadcasted_iota(jnp.int32, (LANES, 1), 0) < HD_B
    pairs_per_group = H_B // G_B // 2
    for g in range(G_B):
        bg = xbc_c[:, d_b + g * N_STATE:d_b + (g + 1) * N_STATE].astype(BF16)
        cg = xbc_c[:, d_b + (G_B + g) * N_STATE:d_b + (G_B + g + 1) * N_STATE].astype(BF16)
        cb = _dot_nt(cg, bg)
        for p in range(pairs_per_group):
            pi = g * pairs_per_group + p
            sl = slice(pi * LANES, (pi + 1) * LANES)
            xpair = xbc_c[:, sl]
            xpb = xpair.astype(BF16)
            ys, a_cols, w_cols, a_lasts = [], [], [], []
            for j in (2 * pi, 2 * pi + 1):
                a_col = gcol[:, L_DTA + j:L_DTA + j + 1]
                a_row = grow[L_DTA + j:L_DTA + j + 1, :]
                dt_col = gcol[:, L_DT + j:L_DT + j + 1]
                dt_row = grow[L_DT + j:L_DT + j + 1, :]
                decay = jnp.exp(jnp.where(causal, a_col - a_row, NEG_INF))
                scores = cb * decay * dt_row
                ys.append(_dot(scores.astype(BF16), xpb))
                a_last = a_col[T - 1:T, :]
                a_cols.append(a_col)
                a_lasts.append(a_last)
                w_cols.append(jnp.exp(a_last - a_col) * dt_col)
            s0 = s_ref[pi]
            y = jnp.where(left, ys[0], ys[1])
            ea = jnp.exp(jnp.where(left, a_cols[0], a_cols[1]))
            y = y + ea * _dot_nt(cg, s0.astype(BF16))
            xw = (xpair * jnp.where(left, w_cols[0], w_cols[1])).astype(BF16)
            ea_last = jnp.exp(jnp.where(top, a_lasts[0], a_lasts[1]))
            s_ref[pi] = ea_last * s0 + _dot_tn(xw, bg)
            y = y + dsk_ref[:, sl] * xpair
            y_buf[:, sl] = y * _silu(zb[:, sl])
    gw = d_b // G_B
    for g in range(G_B):
        yg = y_buf[:, g * gw:(g + 1) * gw]
        merged[:, d_a + g * gw:d_a + (g + 1) * gw] = (
            yg * _rms_scale(yg) * nb_ref[:, g * gw:(g + 1) * gw]).astype(BF16)

    @pl.when(c > 0)
    def _out():
        hmid_ref[...] = x + _dot(merged[...], wout_ref[...])

    @pl.when(c == 0)
    def _no_out():
        hmid_ref[...] = jnp.zeros_like(hmid_ref)

    conva_ref[...] = xa_buf[CONV_HDR + T - 3:CONV_HDR + T, :]
    convb_ref[...] = xbc_buf[CONV_HDR + T - 3:CONV_HDR + T, :]


PROMPT_ROWS = 1


def _prompt_rows_kernel(xmeta_ref, xp_ref, nmix_ref, wcat_ref, bsm_ref, alog_ref,
                        cwa_ref, cba_ref, cwb_ref, cbb_ref, wq_ref, wk_ref, wv_ref,
                        na_ref, nb_ref, dsk_ref, wout_ref,
                        hmid_hbm, c_ref, n_ref, m_ref, conva_ref, s_ref, convb_ref,
                        xa_buf, xbc_buf, y_buf, merged, hout, sem, *, seq, n_prompt_rows):
    p = pl.program_id(0)
    c = pl.program_id(1)
    last_p = pl.num_programs(0) - 1
    last_c = pl.num_programs(1) - 1
    T = CHUNK
    RB = xp_ref.shape[0]
    d_a = H_A * DH_A
    d_b = H_B * HD_B
    conv_b = d_b + 2 * G_B * N_STATE

    def out_copy(r, row0):
        return pltpu.make_async_copy(hout.at[r], hmid_hbm.at[pl.ds(row0, T), :], sem.at[r])

    @pl.when(c == 0)
    def _init():
        c_ref[...] = jnp.zeros_like(c_ref)
        n_ref[...] = jnp.zeros_like(n_ref)
        m_ref[...] = jnp.zeros_like(m_ref)
        s_ref[...] = jnp.zeros_like(s_ref)
        xa_buf[...] = jnp.zeros_like(xa_buf)
        xbc_buf[...] = jnp.zeros_like(xbc_buf)

    @pl.when(jnp.logical_and(p == 0, c == 0))
    def _clear_sample_rows():
        hout[0] = jnp.zeros((T, hout.shape[2]), F32)
        cp = out_copy(0, n_prompt_rows)
        cp.start()
        cp.wait()

    row = lax.broadcasted_iota(jnp.int32, (T, 1), 0)
    valid = jnp.logical_or(c > 0, row >= T - N_META)
    xs_in = [jnp.where(c == 0, xmeta_ref[...], xp_ref[r]) for r in range(RB)]
    x2 = jnp.concatenate(xs_in, axis=0)
    hn = (x2 * _rms_scale(x2) * nmix_ref[...]).astype(BF16)

    lane = lax.broadcasted_iota(jnp.int32, (1, LANES), 1)
    lane_f = lane < L_DTA
    lane_dta = jnp.logical_and(lane >= L_DTA, lane < L_I)
    lane_i = jnp.logical_and(lane >= L_I, lane < L_DT)
    lane_dt = jnp.logical_and(lane >= L_DT, lane < L_DT + H_B)
    ri = lax.broadcasted_iota(jnp.int32, (T, T), 0)
    ci = lax.broadcasted_iota(jnp.int32, (T, T), 1)
    causal = ri >= ci
    tri = jnp.where(causal, 1.0, 0.0).astype(BF16)
    a_neg = jnp.where(lane_dta, -jnp.exp(alog_ref[...]), 0.0)
    left = lane < HD_B
    top = lax.broadcasted_iota(jnp.int32, (LANES, 1), 0) < HD_B

    off_small = 2 * d_a + d_b + conv_b
    pre2 = _dot(hn, wcat_ref[:, off_small:]) + bsm_ref[...]
    xa2 = _dot(hn, wcat_ref[:, 0:d_a])

    gcols, grows, xcs, xabs = [], [], [], []
    for r in range(RB):
        rs = slice(r * T, (r + 1) * T)
        pre = pre2[rs]
        sp, lsig = _softplus_parts(pre)
        to_cum = jnp.where(lane_f, lsig, jnp.where(lane_dta, sp * a_neg, 0.0))
        to_cum = jnp.where(valid, to_cum, 0.0)
        hi, mid, lo = _split3(to_cum)
        cum = _dot(tri, hi) + _dot(tri, mid) + _dot(tri, lo)
        extra = jnp.where(lane_i, jnp.where(valid, pre, NEG_INF),
                          jnp.where(lane_dt, jnp.where(valid, sp, 0.0), 0.0))
        gcol = cum + extra
        gcols.append(gcol)
        grows.append(gcol.T)
        xa = xa2[rs]
        xc = _causal_conv(xa, xa_buf[r], cwa_ref, cba_ref)
        xa_buf[r] = xa[T - CONV_HDR:T, :]
        conva_ref[r] = xa[T - 3:T, :]
        xcs.append(_silu(xc).astype(BF16))
        xabs.append(xa.astype(BF16))

    items = [(r, h) for h in range(H_A) for r in range(RB)]
    hsl = lambda h: slice(h * DH_A, (h + 1) * DH_A)
    m_alls = [m_ref[r] for r in range(RB)]
    m_news = list(m_alls)
    qs, ks, vs, qks, st, dd = {}, {}, {}, {}, {}, {}

    def stage_qkv(it):
        r, h = it
        qs[it] = _dot(xcs[r][:, hsl(h)], wq_ref[h]).astype(BF16)
        ks[it] = _dot(xcs[r][:, hsl(h)], wk_ref[h]) * (DH_A ** -0.5)
        vs[it] = _dot(xabs[r][:, hsl(h)], wv_ref[h]).astype(BF16)

    def stage_qk(it):
        qks[it] = _dot_nt(qs[it], ks[it].astype(BF16))

    xbc2 = _dot(hn, wcat_ref[:, 2 * d_a + d_b:off_small])
    xbcs = []
    for r in range(RB):
        rs = slice(r * T, (r + 1) * T)
        xbc = xbc2[rs]
        xbc_c = _causal_conv(xbc, xbc_buf[r], cwb_ref, cbb_ref)
        xbc_buf[r] = xbc[T - CONV_HDR:T, :]
        convb_ref[r] = xbc[T - 3:T, :]
        xbcs.append(_silu(xbc_c))
    pairs_per_group = H_B // G_B // 2
    groups = [(r, g) for g in range(G_B) for r in range(RB)]
    bgs = {(r, g): xbcs[r][:, d_b + g * N_STATE:d_b + (g + 1) * N_STATE].astype(BF16) for r, g in groups}
    cgs = {(r, g): xbcs[r][:, d_b + (G_B + g) * N_STATE:d_b + (G_B + g + 1) * N_STATE].astype(BF16)
           for r, g in groups}
    cbs = {rg: _dot_nt(cgs[rg], bgs[rg]) for rg in groups}
    def stage_weights(it):
        r, h = it
        gcol, grow = gcols[r], grows[r]
        b_col = gcol[:, L_F + h:L_F + h + 1]
        i_col = gcol[:, L_I + h:L_I + h + 1]
        b_row = grow[L_F + h:L_F + h + 1, :]
        i_row = grow[L_I + h:L_I + h + 1, :]
        m0 = m_alls[r][:, h:h + 1]
        dmat = jnp.where(causal, b_col - (b_row - i_row), NEG_INF)
        m_inter = b_col + m0
        m = jnp.maximum(m_inter, jnp.max(dmat, axis=-1, keepdims=True))
        w_inter = jnp.exp(m_inter - m)
        s = qks[it] * jnp.exp(dmat - m)
        n0 = n_ref[r, h:h + 1, :]
        den = (jnp.sum(s, axis=-1, keepdims=True)
               + w_inter * jnp.sum(qs[it].astype(F32) * n0, axis=-1, keepdims=True))
        m_last = m[T - 1:T, :]
        b_last = b_col[T - 1:T, :]
        dec = jnp.exp(b_last + m0 - m_last)
        kw = ks[it] * jnp.exp(b_last - b_col + i_col - m_last)
        n_ref[r, h:h + 1, :] = dec * n0 + jnp.sum(kw, axis=0, keepdims=True)
        m_news[r] = jnp.where(lane == h, m_last, m_news[r])
        st[it] = (s.astype(BF16), kw.astype(BF16), w_inter,
                  jnp.maximum(jnp.abs(den), jnp.exp(-m)), dec)

    def stage_readout(it):
        r, h = it
        s_b, kw_b, w_inter, den, dec = st[it]
        c0 = c_ref[r, h]
        num = _dot(s_b, vs[it]) + w_inter * _dot(qs[it], c0.astype(BF16))
        c_ref[r, h] = dec * c0 + _dot_tn(kw_b, vs[it])
        dd[it] = num / den

    def stage_head_out(it, za2):
        r, h = it
        hh = dd[it]
        hh = hh * _rms_scale(hh) * na_ref[:, hsl(h)]
        merged[r * T:(r + 1) * T, hsl(h)] = (
            hh * jax.nn.sigmoid(za2[r * T:(r + 1) * T, hsl(h)])).astype(BF16)

    for stage in (stage_qkv, stage_qk, stage_weights, stage_readout):
        for it in items:
            stage(it)
    for r in range(RB):
        m_ref[r] = m_news[r]
    za2 = _dot(hn, wcat_ref[:, d_a:2 * d_a])
    zb2 = _dot(hn, wcat_ref[:, 2 * d_a:2 * d_a + d_b])

    pairs = [(r, pi) for pi in range(H_B // 2) for r in range(RB)]
    psl = lambda pi: slice(pi * LANES, (pi + 1) * LANES)
    sw = {}

    def stage_decay(pr):
        r, pi = pr
        g = pi // pairs_per_group
        gcol, grow = gcols[r], grows[r]
        xpair = xbcs[r][:, psl(pi)]
        scs, a_cols, w_cols, a_lasts = [], [], [], []
        for j in (2 * pi, 2 * pi + 1):
            a_col = gcol[:, L_DTA + j:L_DTA + j + 1]
            a_row = grow[L_DTA + j:L_DTA + j + 1, :]
            dt_col = gcol[:, L_DT + j:L_DT + j + 1]
            dt_row = grow[L_DT + j:L_DT + j + 1, :]
            decay = jnp.exp(jnp.where(causal, a_col - a_row, NEG_INF))
            scs.append((cbs[(r, g)] * decay * dt_row).astype(BF16))
            a_last = a_col[T - 1:T, :]
            a_cols.append(a_col)
            a_lasts.append(a_last)
            w_cols.append(jnp.exp(a_last - a_col) * dt_col)
        sw[pr] = (scs, xpair.astype(BF16),
                  (xpair * jnp.where(left, w_cols[0], w_cols[1])).astype(BF16),
                  jnp.exp(jnp.where(left, a_cols[0], a_cols[1])),
                  jnp.exp(jnp.where(top, a_lasts[0], a_lasts[1])))
    def stage_pair_out(pr):
        r, pi = pr
        g = pi // pairs_per_group
        scs, xpb, xw, ea, ea_last = sw[pr]
        s0 = s_ref[r, pi]
        y = jnp.where(left, _dot(scs[0], xpb), _dot(scs[1], xpb))
        y = y + ea * _dot_nt(cgs[(r, g)], s0.astype(BF16))
        s_ref[r, pi] = ea_last * s0 + _dot_tn(xw, bgs[(r, g)])
        y = y + dsk_ref[:, psl(pi)] * xbcs[r][:, psl(pi)]
        y_buf[r, :, psl(pi)] = y * _silu(zb2[r * T:(r + 1) * T, psl(pi)])

    for it in items:
        stage_head_out(it, za2)
    for stage in (stage_decay, stage_pair_out):
        for pr in pairs:
            stage(pr)
    gw = d_b // G_B
    for r in range(RB):
        for g in range(G_B):
            yg = y_buf[r, :, g * gw:(g + 1) * gw]
            merged[r * T:(r + 1) * T, d_a + g * gw:d_a + (g + 1) * gw] = (
                yg * _rms_scale(yg) * nb_ref[:, g * gw:(g + 1) * gw]).astype(BF16)

    @pl.when(c > 0)
    def _out():
        out2 = x2 + _dot(merged[...], wout_ref[...])

        @pl.when(jnp.logical_or(c > 1, p > 0))
        def _wait_previous():
            for r in range(RB):
                out_copy(r, 0).wait()

        for r in range(RB):
            hout[r] = out2[r * T:(r + 1) * T]
            out_copy(r, (p * RB + r) * seq + (c - 1) * T).start()

        @pl.when(jnp.logical_and(p == last_p, c == last_c))
        def _drain():
            for r in range(RB):
                out_copy(r, 0).wait()


def _const_spec(shape):
    nd = len(shape)
    return pl.BlockSpec(shape, lambda b, c, _nd=nd: (0,) * _nd)


def _prompt_mixer(x_prompt, xmeta, p, n_extra_rows):
    bsz, seq, d = x_prompt.shape
    assert n_extra_rows == CHUNK and seq % CHUNK == 0
    n_chunks = seq // CHUNK + 1
    cps = seq // CHUNK
    d_a = H_A * DH_A
    conv_b = H_B * HD_B + 2 * G_B * N_STATE
    consts = [p["nmix"], p["wcat"], p["bsm"], p["alog"], p["cwa"], p["cba"], p["cwb"], p["cbb"],
              p["wq"], p["wk"], p["wv"], p["na"], p["nb"], p["dsk"], p["wout"]]
    rb = PROMPT_ROWS
    assert bsz % rb == 0
    in_specs = [_const_spec(xmeta.shape),
                pl.BlockSpec((rb, CHUNK, d), lambda b, c: (b, jnp.maximum(c - 1, 0), 0))]
    in_specs += [_const_spec(a.shape) for a in consts]
    out_shape = (
        jax.ShapeDtypeStruct((bsz * seq + n_extra_rows, d), F32),
        jax.ShapeDtypeStruct((bsz, H_A, DH_A, DH_A), F32),
        jax.ShapeDtypeStruct((bsz, H_A, DH_A), F32),
        jax.ShapeDtypeStruct((bsz, 1, LANES), F32),
        jax.ShapeDtypeStruct((bsz, CONV_W - 1, d_a), F32),
        jax.ShapeDtypeStruct((bsz, H_B // 2, 2 * HD_B, N_STATE), F32),
        jax.ShapeDtypeStruct((bsz, CONV_W - 1, conv_b), F32),
    )
    out_specs = (
        pl.BlockSpec(memory_space=pl.ANY),
        pl.BlockSpec((rb, H_A, DH_A, DH_A), lambda b, c: (b, 0, 0, 0)),
        pl.BlockSpec((rb, H_A, DH_A), lambda b, c: (b, 0, 0)),
        pl.BlockSpec((rb, 1, LANES), lambda b, c: (b, 0, 0)),
        pl.BlockSpec((rb, CONV_W - 1, d_a), lambda b, c: (b, 0, 0)),
        pl.BlockSpec((rb, H_B // 2, 2 * HD_B, N_STATE), lambda b, c: (b, 0, 0, 0)),
        pl.BlockSpec((rb, CONV_W - 1, conv_b), lambda b, c: (b, 0, 0)),
    )
    return pl.pallas_call(
        functools.partial(_prompt_rows_kernel, seq=seq, n_prompt_rows=bsz * seq),
        out_shape=out_shape,
        grid=(bsz // rb, n_chunks),
        in_specs=in_specs,
        out_specs=out_specs,
        scratch_shapes=[
            pltpu.VMEM((rb, CONV_HDR, d_a), F32),
            pltpu.VMEM((rb, CONV_HDR, conv_b), F32),
            pltpu.VMEM((rb, CHUNK, H_B * HD_B), F32),
            pltpu.VMEM((rb * CHUNK, d_a + H_B * HD_B), BF16),
            pltpu.VMEM((rb, CHUNK, d), F32),
            pltpu.SemaphoreType.DMA((rb,)),
        ],
        compiler_params=pltpu.CompilerParams(
            dimension_semantics=("arbitrary", "arbitrary"), vmem_limit_bytes=VMEM_LIMIT),
        name="prompt_mixer",
    )(xmeta, x_prompt, *consts)


def _prep_mixer_params(norm_mix, w_in, conv_a_w, conv_a_b, w_q, w_k, w_v, b_i, b_f, norm_a,
                       conv_b_w, conv_b_b, dt_bias, a_log, d_skip, norm_b, w_out):
    d_a = H_A * DH_A
    d_b = H_B * HD_B
    conv_b = d_b + 2 * G_B * N_STATE
    w = w_in[0]
    o = 0
    w_xa = w[:, o:o + d_a]; o += d_a
    w_za = w[:, o:o + d_a]; o += d_a
    w_i = w[:, o:o + H_A]; o += H_A
    w_f = w[:, o:o + H_A]; o += H_A
    w_zb = w[:, o:o + d_b]; o += d_b
    w_xbc = w[:, o:o + conv_b]; o += conv_b
    w_dt = w[:, o:o + H_B]
    pad = jnp.zeros((w.shape[0], LANES - (L_DT + H_B)), w.dtype)
    wcat = jnp.concatenate([w_xa, w_za, w_zb, w_xbc, w_f, w_dt, w_i, w_dt, pad], axis=1).astype(BF16)

    def lanes(parts):
        v = jnp.zeros((1, LANES), F32)
        for off, a in parts:
            v = v.at[0, off:off + a.shape[0]].set(a.astype(F32))
        return v

    return dict(
        nmix=norm_mix[0][None, :].astype(F32),
        wcat=wcat,
        bsm=lanes([(L_F, b_f[0]), (L_DTA, dt_bias[0]), (L_I, b_i[0]), (L_DT, dt_bias[0])]),
        alog=lanes([(L_DTA, a_log[0])]),
        cwa=conv_a_w[0].astype(F32), cba=conv_a_b[0][None, :].astype(F32),
        cwb=conv_b_w[0].astype(F32), cbb=conv_b_b[0][None, :].astype(F32),
        wq=w_q[0].astype(BF16), wk=w_k[0].astype(BF16), wv=w_v[0].astype(BF16),
        na=norm_a[0].reshape(1, d_a).astype(F32), nb=norm_b[0][None, :].astype(F32),
        dsk=jnp.repeat(d_skip[0].astype(F32), HD_B)[None, :],
        wout=w_out[0].astype(BF16),
    )


SAMPLE_BLOCK = 8


def _expand_lanes(vals, first_lane, n_heads, width):
    r = lax.broadcasted_iota(jnp.int32, (LANES, n_heads * width), 0) - first_lane
    c = lax.broadcasted_iota(jnp.int32, (LANES, n_heads * width), 1)
    sel = jnp.logical_and(c >= r * width, c < (r + 1) * width)
    e = jnp.where(sel, 1.0, 0.0).astype(BF16)
    hi, mid, lo = _split3(vals)
    return (_dot(hi, e) + _dot(mid, e)) + _dot(lo, e)


def _sample_pre_kernel(x_ref, nmix_ref, wcat_ref, bsm_ref, alog_ref, cwa_ref, cba_ref, cwb_ref, cbb_ref,
                       wq_ref, wk_ref, wv_ref, dsk_ref, conva_ref, convb_ref, n0_ref, m0_ref,
                       conva_out, convb_out, n1_out, m1_out, g_out, qt_out, kwt_out, xwt_out,
                       v_out, bc_out, a1_out, w1_out, den_out, y1_out, ea_out, zbs_out, zas_out):
    d_a = H_A * DH_A
    d_b = H_B * HD_B
    conv_b = d_b + 2 * G_B * N_STATE
    shift_i = LANES - (L_I - L_F)
    x = x_ref[...]
    hn = (x * _rms_scale(x) * nmix_ref[...]).astype(BF16)
    lane = lax.broadcasted_iota(jnp.int32, (1, LANES), 1)
    lane_f = lane < L_DTA
    lane_dta = jnp.logical_and(lane >= L_DTA, lane < L_I)
    pre = _dot(hn, wcat_ref[:, 2 * d_a + d_b + conv_b:]) + bsm_ref[...]
    sp, lsig = _softplus_parts(pre)
    a_neg = jnp.where(lane_dta, -jnp.exp(alog_ref[...]), 0.0)
    pre_al = pltpu.roll(pre, shift_i, axis=1)
    sp_al = pltpu.roll(sp, shift_i, axis=1)
    m_inter = lsig + m0_ref[...]
    m = jnp.maximum(m_inter, pre_al)
    w_inter = jnp.exp(m_inter - m)
    sfac = jnp.exp(pre_al - m)
    ea = jnp.exp(sp * a_neg)
    dt = sp_al

    xa = _dot(hn, wcat_ref[:, 0:d_a])
    xc = (cwa_ref[0:1, :] * conva_ref[:, 0:d_a] + cwa_ref[1:2, :] * conva_ref[:, d_a:2 * d_a]
          + cwa_ref[2:3, :] * conva_ref[:, 2 * d_a:3 * d_a] + cwa_ref[3:4, :] * xa + cba_ref[...])
    conva_out[:, 0:2 * d_a] = conva_ref[:, d_a:3 * d_a]
    conva_out[:, 2 * d_a:3 * d_a] = xa
    xc = _silu(xc).astype(BF16)
    xab = xa.astype(BF16)
    sf_e = _expand_lanes(sfac, L_F, H_A, DH_A)
    w_e = _expand_lanes(w_inter, L_F, H_A, DH_A)
    qk8 = jnp.zeros((x.shape[0], LANES), F32)
    qn8 = jnp.zeros((x.shape[0], LANES), F32)
    for h in range(H_A):
        sl = slice(h * DH_A, (h + 1) * DH_A)
        q = _dot(xc[:, sl], wq_ref[h])
        k = _dot(xc[:, sl], wk_ref[h]) * (DH_A ** -0.5)
        v = _dot(xab[:, sl], wv_ref[h])
        kw = k * sf_e[:, sl]
        qk8 = jnp.where(lane == h, jnp.sum(q * k, axis=-1, keepdims=True), qk8)
        qn8 = jnp.where(lane == h, jnp.sum(q * n0_ref[:, sl], axis=-1, keepdims=True), qn8)
        n1_out[:, sl] = w_e[:, sl] * n0_ref[:, sl] + kw
        v_out[:, sl] = v
        qt_out[h] = q.T
        kwt_out[h] = kw.T
    s8 = qk8 * sfac
    a1_out[...] = _expand_lanes(s8, L_F, H_A, DH_A) * v_out[...]
    w1_out[...] = w_e
    den_out[...] = jnp.maximum(jnp.abs(_expand_lanes(s8 + w_inter * qn8, L_F, H_A, DH_A)),
                               jnp.exp(-_expand_lanes(m, L_F, H_A, DH_A)))
    m1_out[...] = m
    g_out[...] = jnp.where(lane_f, w_inter, jnp.where(lane_dta, ea, 0.0))
    zas_out[...] = jax.nn.sigmoid(_dot(hn, wcat_ref[:, d_a:2 * d_a]))

    off_xbc = 2 * d_a + d_b
    xbc = _dot(hn, wcat_ref[:, off_xbc:off_xbc + conv_b])
    xbc_c = (cwb_ref[0:1, :] * convb_ref[:, 0:conv_b] + cwb_ref[1:2, :] * convb_ref[:, conv_b:2 * conv_b]
             + cwb_ref[2:3, :] * convb_ref[:, 2 * conv_b:3 * conv_b] + cwb_ref[3:4, :] * xbc + cbb_ref[...])
    convb_out[:, 0:2 * conv_b] = convb_ref[:, conv_b:3 * conv_b]
    convb_out[:, 2 * conv_b:3 * conv_b] = xbc
    xbc_c = _silu(xbc_c)
    xs = xbc_c[:, 0:d_b]
    bc = xbc_c[:, d_b:conv_b]
    bc_out[...] = bc
    heads_per_group = H_B // G_B
    cbl = jnp.zeros((x.shape[0], LANES), F32)
    for g in range(G_B):
        cb_g = jnp.sum(bc[:, g * N_STATE:(g + 1) * N_STATE]
                       * bc[:, (G_B + g) * N_STATE:(G_B + g + 1) * N_STATE], axis=-1, keepdims=True)
        in_g = jnp.logical_and(lane >= L_DTA + g * heads_per_group,
                               lane < L_DTA + (g + 1) * heads_per_group)
        cbl = jnp.where(in_g, cb_g, cbl)
    dt_e = _expand_lanes(dt, L_DTA, H_B, HD_B)
    y1_out[...] = _expand_lanes(cbl * dt, L_DTA, H_B, HD_B) * xs + dsk_ref[...] * xs
    ea_out[...] = _expand_lanes(ea, L_DTA, H_B, HD_B)
    zbs_out[...] = _silu(_dot(hn, wcat_ref[:, 2 * d_a:2 * d_a + d_b]))
    xw = xs * dt_e
    for pi in range(H_B // 2):
        xwt_out[pi] = xw[:, pi * LANES:(pi + 1) * LANES].T


def _sample_state_kernel(dec_ref, ea_ref, c0_ref, s0_ref, qt_ref, kwt_ref, xwt_ref, v_ref, bc_ref,
                         c1_ref, s1_ref, qc_ref, ysi_ref):
    i = pl.program_id(0)
    bb = c0_ref.shape[0]
    shift = lax.rem(LANES - lax.rem(i * bb, LANES), LANES)
    lane = lax.broadcasted_iota(jnp.int32, (1, LANES), 1)
    top = lax.broadcasted_iota(jnp.int32, (LANES, 1), 0) < HD_B
    heads_per_group = H_B // G_B
    for h in range(H_A):
        sl = slice(h * DH_A, (h + 1) * DH_A)
        qt = pltpu.roll(qt_ref[h], shift, axis=1)
        kwt = pltpu.roll(kwt_ref[h], shift, axis=1)
        for r in range(bb):
            b = i * bb + r
            c0 = c0_ref[r, h]
            dec = dec_ref[b * H_A + h]
            v_row = v_ref[r:r + 1, sl]
            qc_ref[r:r + 1, sl] = jnp.sum(c0 * qt[:, r:r + 1], axis=0, keepdims=True)
            c1_ref[r, h] = dec * c0 + kwt[:, r:r + 1] * v_row
    for pi in range(H_B // 2):
        g = (2 * pi) // heads_per_group
        sl = slice(pi * LANES, (pi + 1) * LANES)
        xwt = pltpu.roll(xwt_ref[pi], shift, axis=1)
        acc = jnp.zeros((LANES, LANES), F32)
        for r in range(bb):
            b = i * bb + r
            s0 = s0_ref[r, pi]
            b_row = bc_ref[r:r + 1, g * N_STATE:(g + 1) * N_STATE]
            c_row = bc_ref[r:r + 1, (G_B + g) * N_STATE:(G_B + g + 1) * N_STATE]
            col = jnp.sum(s0 * c_row, axis=-1, keepdims=True)
            acc = jnp.where(lane == r, col, acc)
            ea_rows = jnp.where(top, ea_ref[b * H_B + 2 * pi], ea_ref[b * H_B + 2 * pi + 1])
            s1_ref[r, pi] = ea_rows * s0 + xwt[:, r:r + 1] * b_row
        ysi_ref[:, sl] = acc.T[0:bb, :]


def _sample_post_kernel(a1_ref, w1_ref, den_ref, y1_ref, ea_ref, zbs_ref, zas_ref, x_ref, qc_ref, ysi_ref,
                        na_ref, nb_ref, wout_ref, hall_ref, hmid_ref, merged):
    del hall_ref
    d_a = H_A * DH_A
    d_b = H_B * HD_B
    hh = (a1_ref[...] + w1_ref[...] * qc_ref[...]) / den_ref[...]
    for h in range(H_A):
        sl = slice(h * DH_A, (h + 1) * DH_A)
        hs = hh[:, sl]
        merged[:, sl] = (hs * _rms_scale(hs) * na_ref[:, sl] * zas_ref[:, sl]).astype(BF16)
    y = (y1_ref[...] + ea_ref[...] * ysi_ref[...]) * zbs_ref[...]
    gw = d_b // G_B
    for g in range(G_B):
        yg = y[:, g * gw:(g + 1) * gw]
        merged[:, d_a + g * gw:d_a + (g + 1) * gw] = (
            yg * _rms_scale(yg) * nb_ref[:, g * gw:(g + 1) * gw]).astype(BF16)
    hmid_ref[...] = x_ref[...] + _dot(merged[...], wout_ref[...])


def _vmem_specs(arrays):
    return [pl.BlockSpec(a.shape, lambda *_, _nd=a.ndim: (0,) * _nd) for a in arrays]


def _sample_mixer(x, c0, n0, m0, conva, s0, convb, p, hmid_all, row_offset):
    nb, d = x.shape
    d_a = H_A * DH_A
    d_b = H_B * HD_B
    conv_b = d_b + 2 * G_B * N_STATE
    row = lambda w: jax.ShapeDtypeStruct((nb, w), F32)
    tile = lambda k: jax.ShapeDtypeStruct((k, LANES, nb), F32)
    pre_in = [x, p["nmix"], p["wcat"], p["bsm"], p["alog"], p["cwa"], p["cba"], p["cwb"], p["cbb"],
              p["wq"], p["wk"], p["wv"], p["dsk"], conva, convb, n0, m0]
    pre_out_shape = (row(3 * d_a), row(3 * conv_b), row(d_a), row(LANES), row(LANES),
                     tile(H_A), tile(H_A), tile(H_B // 2), row(d_a), row(2 * G_B * N_STATE),
                     row(d_a), row(d_a), row(d_a), row(d_b), row(d_b), row(d_b), row(d_a))
    (conva1, convb1, n1, m1, g8, qt, kwt, xwt, v, bc, a1, w1, den, y1, ea_e, zbs, zas) = pl.pallas_call(
        _sample_pre_kernel,
        out_shape=pre_out_shape,
        grid=(1,),
        in_specs=_vmem_specs(pre_in),
        out_specs=tuple(pl.BlockSpec(s.shape, lambda i, _nd=len(s.shape): (0,) * _nd) for s in pre_out_shape),
        compiler_params=pltpu.CompilerParams(
            dimension_semantics=("arbitrary",), vmem_limit_bytes=VMEM_LIMIT),
        name="sample_pre",
    )(*pre_in)

    dec_flat = g8[:, L_F:L_F + H_A].reshape(nb * H_A)
    ea_flat = g8[:, L_DTA:L_DTA + H_B].reshape(nb * H_B)
    bb = SAMPLE_BLOCK
    const3 = lambda k: pl.BlockSpec((k, LANES, nb), lambda i, de, ea: (0, 0, 0))
    state_grid = pltpu.PrefetchScalarGridSpec(
        num_scalar_prefetch=2,
        grid=(nb // bb,),
        in_specs=[pl.BlockSpec((bb, H_A, DH_A, DH_A), lambda i, de, ea: (i, 0, 0, 0)),
                  pl.BlockSpec((bb, H_B // 2, 2 * HD_B, N_STATE), lambda i, de, ea: (i, 0, 0, 0)),
                  const3(H_A), const3(H_A), const3(H_B // 2),
                  pl.BlockSpec((bb, d_a), lambda i, de, ea: (i, 0)),
                  pl.BlockSpec((bb, 2 * G_B * N_STATE), lambda i, de, ea: (i, 0))],
        out_specs=(pl.BlockSpec((bb, H_A, DH_A, DH_A), lambda i, de, ea: (i, 0, 0, 0)),
                   pl.BlockSpec((bb, H_B // 2, 2 * HD_B, N_STATE), lambda i, de, ea: (i, 0, 0, 0)),
                   pl.BlockSpec((bb, d_a), lambda i, de, ea: (i, 0)),
                   pl.BlockSpec((bb, d_b), lambda i, de, ea: (i, 0))),
    )
    c1, s1, qc, ysi = pl.pallas_call(
        _sample_state_kernel,
        out_shape=(jax.ShapeDtypeStruct(c0.shape, F32), jax.ShapeDtypeStruct(s0.shape, F32),
                   row(d_a), row(d_b)),
        grid_spec=state_grid,
        compiler_params=pltpu.CompilerParams(
            dimension_semantics=("arbitrary",), vmem_limit_bytes=VMEM_LIMIT),
        name="sample_state",
    )(dec_flat, ea_flat, c0, s0, qt, kwt, xwt, v, bc)

    post_in = [a1, w1, den, y1, ea_e, zbs, zas, x, qc, ysi, p["na"], p["nb"], p["wout"]]
    hmid_all = pl.pallas_call(
        _sample_post_kernel,
        out_shape=jax.ShapeDtypeStruct(hmid_all.shape, F32),
        grid=(1,),
        in_specs=_vmem_specs(post_in) + [pl.BlockSpec(memory_space=pl.ANY)],
        out_specs=pl.BlockSpec((nb, d), lambda i: (row_offset // nb, 0)),
        scratch_shapes=[pltpu.VMEM((nb, d_a + d_b), BF16)],
        input_output_aliases={len(post_in): 0},
        compiler_params=pltpu.CompilerParams(
            dimension_semantics=("arbitrary",), vmem_limit_bytes=VMEM_LIMIT),
        name="sample_post",
    )(*post_in, hmid_all)
    return hmid_all, c1, n1, m1, conva1, s1, convb1


R_EA, R_EB, R_RA, R_RB, R_GA, R_GB = 0, 1, 2, 3, 4, 5
RL_E = N_EGROUPS


def _router_kernel(h_ref, nf_ref, whi_ref, wmid_ref, br_ref, xn_ref, info_ref, cnt_ref, carry):
    i = pl.program_id(0)
    tr = h_ref.shape[0]

    @pl.when(i == 0)
    def _init():
        carry[...] = jnp.zeros_like(carry)

    h = h_ref[...]
    xn = h * _rms_scale(h) * nf_ref[...]
    _store_token_tiles(xn_ref, xn)
    x_hi, x_mid, _ = _split3(xn)
    logits = (_dot(x_hi, whi_ref[...]) + _dot(x_hi, wmid_ref[...]) + _dot(x_mid, whi_ref[...])
              + br_ref[...])
    lane_i = lax.broadcasted_iota(jnp.int32, (1, LANES), 1)
    lane = lane_i.astype(F32)
    big = float(LANES)

    def first_lane_of(cond):
        return jnp.min(jnp.where(cond, lane, big), axis=-1, keepdims=True)

    l1 = jnp.where(lane_i < N_EGROUPS, logits, NEG_INF)
    e1 = jnp.exp(l1 - jnp.max(l1, axis=-1, keepdims=True))
    p1 = e1 / jnp.sum(e1, axis=-1, keepdims=True)
    gp = jnp.max(p1, axis=-1, keepdims=True)
    gidx = first_lane_of(p1 == gp)
    lo = RL_E + N_EPG * gidx
    l2 = jnp.where(jnp.logical_and(lane >= lo, lane < lo + N_EPG), logits, NEG_INF)
    va = jnp.max(l2, axis=-1, keepdims=True)
    ia = first_lane_of(l2 == va)
    l2b = jnp.where(lane == ia, NEG_INF, l2)
    vb = jnp.max(l2b, axis=-1, keepdims=True)
    ib = first_lane_of(l2b == vb)
    eb = jnp.exp(vb - va)
    wa = 1.0 / (1.0 + eb)
    wb = eb / (1.0 + eb)

    is_a = lane == ia
    is_b = lane == ib
    onehot = jnp.where(jnp.logical_or(is_a, is_b), 1.0, 0.0)
    ri = lax.broadcasted_iota(jnp.int32, (tr, tr), 0)
    ci = lax.broadcasted_iota(jnp.int32, (tr, tr), 1)
    tri = jnp.where(ri >= ci, 1.0, 0.0).astype(BF16)
    incl = _dot(tri, onehot.astype(BF16))
    excl = incl - onehot + carry[...]
    rank_a = jnp.sum(jnp.where(is_a, excl, 0.0), axis=-1, keepdims=True)
    rank_b = jnp.sum(jnp.where(is_b, excl, 0.0), axis=-1, keepdims=True)
    carry[...] = carry[...] + incl[tr - 1:tr, :]
    cnt_ref[...] = carry[...]

    info = jnp.where(lane_i == R_EA, ia - RL_E, 0.0)
    info = jnp.where(lane_i == R_EB, ib - RL_E, info)
    info = jnp.where(lane_i == R_RA, rank_a, info)
    info = jnp.where(lane_i == R_RB, rank_b, info)
    info = jnp.where(lane_i == R_GA, gp * wa, info)
    info = jnp.where(lane_i == R_GB, gp * wb, info)
    info_ref[...] = info


def _row_tile(n, candidates):
    for t in candidates:
        if n % t == 0:
            return t
    raise ValueError(f"no row tile for {n} rows among {candidates}")


def _router(hmid, rp):
    n, d = hmid.shape
    assert d == TOK_TILE_ROWS * LANES
    tr = _row_tile(n, (512, 384, 256, 128))
    return pl.pallas_call(
        _router_kernel,
        out_shape=(jax.ShapeDtypeStruct((n * TOK_TILE_ROWS, LANES), F32),
                   jax.ShapeDtypeStruct((n, LANES), F32),
                   jax.ShapeDtypeStruct((1, LANES), F32)),
        grid=(n // tr,),
        in_specs=[pl.BlockSpec((tr, d), lambda i: (i, 0)),
                  pl.BlockSpec((1, d), lambda i: (0, 0)),
                  pl.BlockSpec((d, LANES), lambda i: (0, 0)),
                  pl.BlockSpec((d, LANES), lambda i: (0, 0)),
                  pl.BlockSpec((1, LANES), lambda i: (0, 0))],
        out_specs=(pl.BlockSpec((tr * TOK_TILE_ROWS, LANES), lambda i: (i, 0)),
                   pl.BlockSpec((tr, LANES), lambda i: (i, 0)),
                   pl.BlockSpec((1, LANES), lambda i: (0, 0))),
        scratch_shapes=[pltpu.VMEM((1, LANES), F32)],
        compiler_params=pltpu.CompilerParams(
            dimension_semantics=("arbitrary",), vmem_limit_bytes=VMEM_LIMIT),
        name="router",
    )(hmid, rp["nf"], rp["whi"], rp["wmid"], rp["br"])


def _prep_router_params(norm_ffn, w_r1, b_r1, w_r2, b_r2):
    d = w_r1.shape[1]
    w = jnp.concatenate([w_r1[0].astype(F32), w_r2[0].reshape(d, N_EXPERTS).astype(F32),
                         jnp.zeros((d, LANES - RL_E - N_EXPERTS), F32)], axis=1)
    whi = w.astype(BF16)
    wmid = (w - whi.astype(F32)).astype(BF16)
    br = jnp.concatenate([b_r1[0].astype(F32), b_r2[0].reshape(N_EXPERTS).astype(F32),
                          jnp.zeros((LANES - RL_E - N_EXPERTS,), F32)])[None, :]
    return dict(nf=norm_ffn[0][None, :].astype(F32), whi=whi, wmid=wmid, br=br)


FFN_TM = 256


def _start_tile_gather(first_row_of, n_rows, src_hbm, dst, sem, priority_of):
    for r in range(n_rows):
        start = pl.multiple_of(first_row_of(r), TOK_TILE_ROWS)
        pltpu.make_async_copy(src_hbm.at[pl.ds(start, TOK_TILE_ROWS), :],
                              dst.at[pl.ds(r * TOK_TILE_ROWS, TOK_TILE_ROWS), :],
                              sem).start(priority=priority_of(r))


def _wait_tile_gather(n_rows, src_hbm, dst, sem):
    pltpu.make_async_copy(src_hbm.at[pl.ds(0, n_rows * TOK_TILE_ROWS), :], dst, sem).wait()


M_NVALID = 0
M_OFF = 1
M_CNT = M_OFF + N_EXPERTS
M_LEN = M_CNT + N_EXPERTS


def _ffn_kernel(te_ref, pa_ref, pb_ref, meta_ref, xn_hbm, wg_ref, wu_ref, wd_ref, ys_ref,
                src, xbuf, wbf, sem, *, n_tokens, tm):
    i = pl.program_id(0)
    n_valid = meta_ref[M_NVALID]
    slot = lax.rem(i, 2)
    gather_priority = lambda r: 1

    @pl.when(i == 0)
    def _build_source_rows():
        for e in range(N_EXPERTS):
            first = meta_ref[M_OFF + e] + meta_ref[M_CNT + e]
            n_pad = lax.rem(tm - lax.rem(meta_ref[M_CNT + e], tm), tm)

            def pad_body(r, carry, first=first):
                src[first + r] = 0
                return carry
            lax.fori_loop(0, n_pad, pad_body, 0)

        def body(t, carry):
            first_row = t * TOK_TILE_ROWS
            src[pa_ref[t]] = first_row
            src[pb_ref[t]] = first_row
            return carry
        lax.fori_loop(0, n_tokens, body, 0, unroll=8)
        _start_tile_gather(lambda r: src[r], tm, xn_hbm, xbuf.at[0], sem.at[0], gather_priority)

    @pl.when(i + 1 < n_valid)
    def _next():
        base = (i + 1) * tm
        _start_tile_gather(lambda r: src[base + r], tm, xn_hbm, xbuf.at[1 - slot], sem.at[1 - slot],
                           gather_priority)

    changed = jnp.logical_or(i == 0, te_ref[i] != te_ref[jnp.maximum(i - 1, 0)])

    @pl.when(jnp.logical_and(changed, i < n_valid))
    def _cast_weights():
        wbf[0] = wg_ref[...].astype(BF16)
        wbf[1] = wu_ref[...].astype(BF16)
        wbf[2] = wd_ref[...].astype(BF16)

    @pl.when(i < n_valid)
    def _compute():
        _wait_tile_gather(tm, xn_hbm, xbuf.at[slot], sem.at[slot])
        x = _load_token_tiles(xbuf.at[slot], tm).astype(BF16)
        hg = _dot(x, wbf[0])
        hu = _dot(x, wbf[1])
        _store_token_tiles(ys_ref, _dot((_silu(hg) * hu).astype(BF16), wbf[2]))

    @pl.when(i >= n_valid)
    def _pad():
        ys_ref[...] = jnp.zeros_like(ys_ref)


def _expert_ffn(xn_tiles, tile_expert, pos_a, pos_b, meta, wg, wu, wd):
    n = pos_a.shape[0]
    n_tiles = tile_expert.shape[0]
    tm = FFN_TM
    d, dff = wg.shape[1], wg.shape[2]
    rows = tm * TOK_TILE_ROWS
    idx = lambda i, te, pa, pb, meta: (te[i], 0, 0)
    grid_spec = pltpu.PrefetchScalarGridSpec(
        num_scalar_prefetch=4,
        grid=(n_tiles,),
        in_specs=[pl.BlockSpec(memory_space=pl.ANY),
                  pl.BlockSpec((None, d, dff), idx),
                  pl.BlockSpec((None, d, dff), idx),
                  pl.BlockSpec((None, dff, d), idx)],
        out_specs=pl.BlockSpec((rows, LANES), lambda i, te, pa, pb, meta: (i, 0)),
        scratch_shapes=[pltpu.SMEM((n_tiles * tm,), jnp.int32),
                        pltpu.VMEM((2, rows, LANES), F32),
                        pltpu.VMEM((3, d, dff), BF16),
                        pltpu.SemaphoreType.DMA((2,))],
    )
    return pl.pallas_call(
        functools.partial(_ffn_kernel, n_tokens=n, tm=tm),
        out_shape=jax.ShapeDtypeStruct((n_tiles * rows, LANES), F32),
        grid_spec=grid_spec,
        compiler_params=pltpu.CompilerParams(
            dimension_semantics=("arbitrary",), vmem_limit_bytes=VMEM_LIMIT),
        name="expert_ffn",
    )(tile_expert, pos_a, pos_b, meta, xn_tiles, wg, wu, wd)


def _positions_kernel(info_ref, cnt_ref, pa_ref, pb_ref, *, tm):
    lane_i = lax.broadcasted_iota(jnp.int32, (1, LANES), 1)
    lane = lane_i.astype(F32)
    cnt = cnt_ref[...]
    padded = jnp.floor((cnt + (tm - 1)) / tm) * tm
    ri = lax.broadcasted_iota(jnp.int32, (LANES, LANES), 0)
    ci = lax.broadcasted_iota(jnp.int32, (LANES, LANES), 1)
    before = jnp.where(ri < ci, 1.0, 0.0).astype(BF16)
    hi, mid, lo = _split3(jnp.broadcast_to(padded, (SUBLANES, LANES)))
    off = ((_dot(hi, before) + _dot(mid, before)) + _dot(lo, before))[0:1, :]
    for g in range(info_ref.shape[0] // LANES):
        blk = info_ref[g * LANES:(g + 1) * LANES, :]
        lane_a = blk[:, R_EA:R_EA + 1] + RL_E
        lane_b = blk[:, R_EB:R_EB + 1] + RL_E
        pos_a = blk[:, R_RA:R_RA + 1] + jnp.sum(jnp.where(lane == lane_a, off, 0.0), axis=-1, keepdims=True)
        pos_b = blk[:, R_RB:R_RB + 1] + jnp.sum(jnp.where(lane == lane_b, off, 0.0), axis=-1, keepdims=True)
        rows = jnp.where(lane_i == 0, pos_a, jnp.where(lane_i == 1, pos_b, 0.0)).T
        pa_ref[g:g + 1, :] = rows[0:1, :].astype(jnp.int32)
        pb_ref[g:g + 1, :] = rows[1:2, :].astype(jnp.int32)


def _routing_tables(info, counts, n_tiles, tm):
    n = info.shape[0]
    groups = n // LANES
    pa, pb = pl.pallas_call(
        functools.partial(_positions_kernel, tm=tm),
        out_shape=(jax.ShapeDtypeStruct((groups, LANES), jnp.int32),
                   jax.ShapeDtypeStruct((groups, LANES), jnp.int32)),
        grid=(1,),
        in_specs=[pl.BlockSpec((n, LANES), lambda i: (0, 0)), pl.BlockSpec((1, LANES), lambda i: (0, 0))],
        out_specs=(pl.BlockSpec((groups, LANES), lambda i: (0, 0)),
                   pl.BlockSpec((groups, LANES), lambda i: (0, 0))),
        compiler_params=pltpu.CompilerParams(
            dimension_semantics=("arbitrary",), vmem_limit_bytes=VMEM_LIMIT),
        name="positions",
    )(info, counts)
    cnt = counts[0, RL_E:RL_E + N_EXPERTS].astype(jnp.int32)
    padded = ((cnt + tm - 1) // tm) * tm
    ends = jnp.cumsum(padded)
    n_valid = ends[-1] // tm
    tile_start = jnp.arange(n_tiles, dtype=jnp.int32) * tm
    te = jnp.sum((jnp.minimum(tile_start, ends[-1] - 1)[:, None] >= ends[None, :]).astype(jnp.int32), axis=1)
    meta = jnp.concatenate([n_valid[None], ends - padded, cnt]).astype(jnp.int32)
    return te, meta, pa.reshape(n), pb.reshape(n)


def _combine_kernel(pa_ref, pb_ref, h_ref, info_ref, ys_hbm, nfin_ref, yp_ref, ysm_ref,
                    buf_a, buf_b, sem, *, n_prompt_tiles):
    i = pl.program_id(0)
    n_steps = pl.num_programs(0)
    tt = h_ref.shape[0]
    slot = lax.rem(i, 2)

    def start(tile, s):
        base = tile * tt
        _start_tile_gather(lambda r: pa_ref[base + r] * TOK_TILE_ROWS, tt, ys_hbm, buf_a.at[s],
                           sem.at[s], lambda r: 0)
        _start_tile_gather(lambda r: pb_ref[base + r] * TOK_TILE_ROWS, tt, ys_hbm, buf_b.at[s],
                           sem.at[s], lambda r: 1)

    @pl.when(i == 0)
    def _first():
        start(0, 0)

    @pl.when(i + 1 < n_steps)
    def _next():
        start(i + 1, 1 - slot)

    _wait_tile_gather(tt, ys_hbm, buf_a.at[slot], sem.at[slot])
    _wait_tile_gather(tt, ys_hbm, buf_b.at[slot], sem.at[slot])
    info = info_ref[...]
    h = (h_ref[...] + info[:, R_GA:R_GA + 1] * _load_token_tiles(buf_a.at[slot], tt)
         + info[:, R_GB:R_GB + 1] * _load_token_tiles(buf_b.at[slot], tt))
    y = h * _rms_scale(h) * nfin_ref[...]

    @pl.when(i < n_prompt_tiles)
    def _prompt():
        yp_ref[...] = y

    @pl.when(i >= n_prompt_tiles)
    def _sample():
        ysm_ref[...] = y


def _combine(hmid, info, ys, pos_a, pos_b, nfin, n_prompt):
    n, d = hmid.shape
    tt = CHUNK
    n_prompt_tiles = n_prompt // tt
    n_sample = n - n_prompt
    grid_spec = pltpu.PrefetchScalarGridSpec(
        num_scalar_prefetch=2,
        grid=(n // tt,),
        in_specs=[pl.BlockSpec((tt, d), lambda i, pa, pb: (i, 0)),
                  pl.BlockSpec((tt, LANES), lambda i, pa, pb: (i, 0)),
                  pl.BlockSpec(memory_space=pl.ANY),
                  pl.BlockSpec((1, d), lambda i, pa, pb: (0, 0))],
        out_specs=(pl.BlockSpec((tt, d), lambda i, pa, pb: (jnp.minimum(i, n_prompt_tiles - 1), 0)),
                   pl.BlockSpec((tt, d), lambda i, pa, pb: (jnp.maximum(i - n_prompt_tiles, 0), 0))),
        scratch_shapes=[pltpu.VMEM((2, tt * TOK_TILE_ROWS, LANES), F32),
                        pltpu.VMEM((2, tt * TOK_TILE_ROWS, LANES), F32),
                        pltpu.SemaphoreType.DMA((2,))],
    )
    return pl.pallas_call(
        functools.partial(_combine_kernel, n_prompt_tiles=n_prompt_tiles),
        out_shape=(jax.ShapeDtypeStruct((n_prompt, d), F32),
                   jax.ShapeDtypeStruct((n_sample, d), F32)),
        grid_spec=grid_spec,
        compiler_params=pltpu.CompilerParams(
            dimension_semantics=("arbitrary",), vmem_limit_bytes=VMEM_LIMIT),
        name="combine",
    )(pos_a, pos_b, hmid, info, ys, nfin)


def _moe_and_final_norm(hmid, n_prompt, rp, wg, wu, wd, nfin):
    n = hmid.shape[0]
    tm = FFN_TM
    n_tiles = (2 * n + N_EXPERTS * (tm - 1)) // tm
    xn, info, counts = _router(hmid, rp)
    te, meta, pos_a, pos_b = _routing_tables(info, counts, n_tiles, tm)
    ys = _expert_ffn(xn, te, pos_a, pos_b, meta, wg, wu, wd)
    return _combine(hmid, info, ys, pos_a, pos_b, nfin, n_prompt)


def kernel(x_prompt, x_sample, state_mlstm_C, state_mlstm_n, state_mlstm_m, state_mlstm_conv, state_ssm, state_ssm_conv, meta_tokens, norm_mix, w_in, conv_a_w, conv_a_b, w_q, w_k, w_v, b_i, b_f, norm_a, conv_b_w, conv_b_b, dt_bias, a_log, d_skip, norm_b, w_out, norm_ffn, w_r1, b_r1, w_r2, b_r2, w_gate, w_up, w_down, norm_final):
    bsz, seq, d = x_prompt.shape
    nb = x_sample.shape[0]
    d_a = H_A * DH_A
    conv_b = H_B * HD_B + 2 * G_B * N_STATE
    assert w_in.shape[0] == 1 and x_sample.shape[1] == 1 and seq % CHUNK == 0 and nb == CHUNK
    mp = _prep_mixer_params(norm_mix, w_in, conv_a_w, conv_a_b, w_q, w_k, w_v, b_i, b_f, norm_a,
                            conv_b_w, conv_b_b, dt_bias, a_log, d_skip, norm_b, w_out)
    rp = _prep_router_params(norm_ffn, w_r1, b_r1, w_r2, b_r2)
    xmeta = jnp.concatenate([jnp.zeros((CHUNK - N_META, d), F32), meta_tokens.astype(F32)], 0)

    hmid, p_c, p_n, p_m, p_ca, p_s, p_cb = _prompt_mixer(x_prompt.astype(F32), xmeta, mp, nb)
    m0 = jnp.pad(state_mlstm_m[0].astype(F32), ((0, 0), (0, LANES - H_A)))
    hmid, s_c, s_n, s_m, s_ca, s_s, s_cb = _sample_mixer(
        x_sample.reshape(nb, d).astype(F32),
        state_mlstm_C[0].astype(F32),
        state_mlstm_n[0].astype(F32).reshape(nb, d_a),
        m0,
        state_mlstm_conv[0].astype(F32).reshape(nb, (CONV_W - 1) * d_a),
        state_ssm[0].astype(F32).reshape(nb, H_B // 2, 2 * HD_B, N_STATE),
        state_ssm_conv[0].astype(F32).reshape(nb, (CONV_W - 1) * conv_b),
        mp, hmid, bsz * seq)

    y_p, y_s = _moe_and_final_norm(
        hmid, bsz * seq, rp, w_gate[0].astype(F32), w_up[0].astype(F32), w_down[0].astype(F32),
        norm_final[None, :].astype(F32))

    return (y_p.reshape(bsz, seq, d), y_s.reshape(nb, 1, d),
            p_c[None], p_n[None], p_m[:, 0, :H_A][None], p_ca[None],
            p_s.reshape(bsz, H_B, HD_B, N_STATE)[None], p_cb[None],
            s_c[None], s_n.reshape(nb, H_A, DH_A)[None], s_m[:, :H_A][None],
            s_ca.reshape(nb, CONV_W - 1, d_a)[None],
            s_s.reshape(nb, H_B, HD_B, N_STATE)[None],
            s_cb.reshape(nb, CONV_W - 1, conv_b)[None])
```

```python
import functools
import math

import jax
import jax.numpy as jnp
from jax import lax
from jax.experimental import pallas as pl
from jax.experimental.pallas import tpu as pltpu

F32 = jnp.float32
BF16 = jnp.bfloat16

EPS = 1e-6
N_META = 16
CONV_W = 4
CHUNK = 128
H_A = 8
DH_A = 128
H_B = 16
HD_B = 64
N_STATE = 128
G_B = 2
N_EGROUPS = 4
N_EPG = 4
N_EXPERTS = 16
LANES = 128
SUBLANES = 8
CONV_HDR = SUBLANES
VMEM_LIMIT = 56 * 1024 * 1024

L_F = 0
L_DTA = 8
L_I = 24
L_DT = 32

NEG_INF = float("-inf")


def _dot(a, b):
    return jnp.dot(a, b, preferred_element_type=F32)


def _dot_nt(a, b):
    return lax.dot_general(a, b, (((1,), (1,)), ((), ())), preferred_element_type=F32)


def _dot_tn(a, b):
    return lax.dot_general(a, b, (((0,), (0,)), ((), ())), preferred_element_type=F32)


def _split3(x):
    hi = x.astype(BF16)
    r = x - hi.astype(F32)
    mid = r.astype(BF16)
    lo = (r - mid.astype(F32)).astype(BF16)
    return hi, mid, lo


def _silu(x):
    return x * jax.nn.sigmoid(x)


def _softplus_parts(x):
    t = jnp.log1p(jnp.exp(-jnp.abs(x)))
    return jnp.maximum(x, 0.0) + t, jnp.minimum(x, 0.0) - t


def _rms_scale(x):
    return lax.rsqrt(jnp.mean(x * x, axis=-1, keepdims=True) + EPS)


TOK_TILE_ROWS = SUBLANES


def _store_token_tiles(ref, x):
    n = x.shape[0]
    for j in range(TOK_TILE_ROWS):
        ref[pl.ds(j, n, stride=TOK_TILE_ROWS), :] = x[:, j * LANES:(j + 1) * LANES]


def _causal_conv(x, tail, w_ref, b_ref):
    n_tail = tail.shape[0]
    row = lax.broadcasted_iota(jnp.int32, (n_tail, 1), 0)
    acc = w_ref[CONV_W - 1:CONV_W, :] * x + b_ref[...]
    for k in range(1, CONV_W):
        rolled = pltpu.roll(x, k, axis=0)
        head = jnp.where(row < k, pltpu.roll(tail, k, axis=0), rolled[0:n_tail])
        shifted = jnp.concatenate([head, rolled[n_tail:]], axis=0)
        acc = acc + w_ref[CONV_W - 1 - k:CONV_W - k, :] * shifted
    return acc


def _load_token_tiles(ref, n):
    return jnp.concatenate(
        [ref[pl.ds(j, n, stride=TOK_TILE_ROWS), :] for j in range(TOK_TILE_ROWS)], axis=1)


PROMPT_ROWS = 1


def _prompt_rows_kernel(xmeta_ref, xp_ref, nmix_ref, wcat_ref, bsm_ref, alog_ref,
                        cwa_ref, cba_ref, cwb_ref, cbb_ref, wq_ref, wk_ref, wv_ref,
                        na_ref, nb_ref, dsk_ref, wout_ref,
                        hmid_hbm, c_ref, n_ref, m_ref, conva_ref, s_ref, convb_ref,
                        xa_buf, xbc_buf, y_buf, merged, hout, sem, *, seq, n_prompt_rows):
    p = pl.program_id(0)
    c = pl.program_id(1)
    last_p = pl.num_programs(0) - 1
    last_c = pl.num_programs(1) - 1
    T = CHUNK
    RB = xp_ref.shape[0]
    d_a = H_A * DH_A
    d_b = H_B * HD_B
    conv_b = d_b + 2 * G_B * N_STATE

    def out_copy(r, row0):
        return pltpu.make_async_copy(hout.at[r], hmid_hbm.at[pl.ds(row0, T), :], sem.at[r])

    @pl.when(c == 0)
    def _init():
        c_ref[...] = jnp.zeros_like(c_ref)
        n_ref[...] = jnp.zeros_like(n_ref)
        m_ref[...] = jnp.zeros_like(m_ref)
        s_ref[...] = jnp.zeros_like(s_ref)
        xa_buf[...] = jnp.zeros_like(xa_buf)
        xbc_buf[...] = jnp.zeros_like(xbc_buf)

    @pl.when(jnp.logical_and(p == 0, c == 0))
    def _clear_sample_rows():
        hout[0] = jnp.zeros((T, hout.shape[2]), F32)
        cp = out_copy(0, n_prompt_rows)
        cp.start()
        cp.wait()

    row = lax.broadcasted_iota(jnp.int32, (T, 1), 0)
    valid = jnp.logical_or(c > 0, row >= T - N_META)
    xs_in = [jnp.where(c == 0, xmeta_ref[...], xp_ref[r]) for r in range(RB)]
    x2 = jnp.concatenate(xs_in, axis=0)
    hn = (x2 * _rms_scale(x2) * nmix_ref[...]).astype(BF16)

    lane = lax.broadcasted_iota(jnp.int32, (1, LANES), 1)
    lane_f = lane < L_DTA
    lane_dta = jnp.logical_and(lane >= L_DTA, lane < L_I)
    lane_i = jnp.logical_and(lane >= L_I, lane < L_DT)
    lane_dt = jnp.logical_and(lane >= L_DT, lane < L_DT + H_B)
    ri = lax.broadcasted_iota(jnp.int32, (T, T), 0)
    ci = lax.broadcasted_iota(jnp.int32, (T, T), 1)
    causal = ri >= ci
    tri = jnp.where(causal, 1.0, 0.0).astype(BF16)
    a_neg = jnp.where(lane_dta, -jnp.exp(alog_ref[...]), 0.0)
    left = lane < HD_B
    top = lax.broadcasted_iota(jnp.int32, (LANES, 1), 0) < HD_B

    off_small = 2 * d_a + d_b + conv_b
    pre2 = _dot(hn, wcat_ref[:, off_small:]) + bsm_ref[...]
    xa2 = _dot(hn, wcat_ref[:, 0:d_a])

    gcols, grows, xcs, xabs = [], [], [], []

    def gate_tables():
        for r in range(RB):
            pre = pre2[r * T:(r + 1) * T]
            sp, lsig = _softplus_parts(pre)
            to_cum = jnp.where(lane_f, lsig, jnp.where(lane_dta, sp * a_neg, 0.0))
            to_cum = jnp.where(valid, to_cum, 0.0)
            hi, mid, lo = _split3(to_cum)
            cum = _dot(tri, hi) + _dot(tri, mid) + _dot(tri, lo)
            extra = jnp.where(lane_i, jnp.where(valid, pre, NEG_INF),
                              jnp.where(lane_dt, jnp.where(valid, sp, 0.0), 0.0))
            gcol = cum + extra
            gcols.append(gcol)
            grows.append(gcol.T)

    gate_tables()
    for r in range(RB):
        rs = slice(r * T, (r + 1) * T)
        xa = xa2[rs]
        xc = _causal_conv(xa, xa_buf[r], cwa_ref, cba_ref)
        xa_buf[r] = xa[T - CONV_HDR:T, :]
        conva_ref[r] = xa[T - 3:T, :]
        xcs.append(_silu(xc).astype(BF16))
        xabs.append(xa.astype(BF16))

    items = [(r, h) for h in range(H_A) for r in range(RB)]
    hsl = lambda h: slice(h * DH_A, (h + 1) * DH_A)
    m_alls = [m_ref[r] for r in range(RB)]
    m_news = list(m_alls)
    qs, ks, vs, qks, st, dd = {}, {}, {}, {}, {}, {}

    def stage_qkv(it):
        r, h = it
        qs[it] = _dot(xcs[r][:, hsl(h)], wq_ref[h]).astype(BF16)
        ks[it] = _dot(xcs[r][:, hsl(h)], wk_ref[h]) * (DH_A ** -0.5)
        vs[it] = _dot(xabs[r][:, hsl(h)], wv_ref[h]).astype(BF16)

    def stage_qk(it):
        qks[it] = _dot_nt(qs[it], ks[it].astype(BF16))

    pairs_per_group = H_B // G_B // 2
    groups = [(r, g) for g in range(G_B) for r in range(RB)]
    proj, xbcs, bgs, cgs, cbs = {}, [], {}, {}, {}

    def project(name, lo, hi):
        proj[name] = _dot(hn, wcat_ref[:, lo:hi])

    def ssd_inputs():
        for r in range(RB):
            xbc = proj["xbc"][r * T:(r + 1) * T]
            xbc_c = _causal_conv(xbc, xbc_buf[r], cwb_ref, cbb_ref)
            xbc_buf[r] = xbc[T - CONV_HDR:T, :]
            convb_ref[r] = xbc[T - 3:T, :]
            xbcs.append(_silu(xbc_c))
        for r, g in groups:
            bgs[(r, g)] = xbcs[r][:, d_b + g * N_STATE:d_b + (g + 1) * N_STATE].astype(BF16)
            cgs[(r, g)] = xbcs[r][:, d_b + (G_B + g) * N_STATE:d_b + (G_B + g + 1) * N_STATE].astype(BF16)
            cbs[(r, g)] = _dot_nt(cgs[(r, g)], bgs[(r, g)])

    def stage_weights(it):
        r, h = it
        gcol, grow = gcols[r], grows[r]
        b_col = gcol[:, L_F + h:L_F + h + 1]
        i_col = gcol[:, L_I + h:L_I + h + 1]
        b_row = grow[L_F + h:L_F + h + 1, :]
        i_row = grow[L_I + h:L_I + h + 1, :]
        m0 = m_alls[r][:, h:h + 1]
        dmat = jnp.where(causal, b_col - (b_row - i_row), NEG_INF)
        m_inter = b_col + m0
        m = jnp.maximum(m_inter, jnp.max(dmat, axis=-1, keepdims=True))
        w_inter = jnp.exp(m_inter - m)
        s = qks[it] * jnp.exp(dmat - m)
        n0 = n_ref[r, h:h + 1, :]
        den = (jnp.sum(s, axis=-1, keepdims=True)
               + w_inter * jnp.sum(qs[it].astype(F32) * n0, axis=-1, keepdims=True))
        m_last = m[T - 1:T, :]
        b_last = b_col[T - 1:T, :]
        dec = jnp.exp(b_last + m0 - m_last)
        kw = ks[it] * jnp.exp(b_last - b_col + i_col - m_last)
        n_ref[r, h:h + 1, :] = dec * n0 + jnp.sum(kw, axis=0, keepdims=True)
        m_news[r] = jnp.where(lane == h, m_last, m_news[r])
        st[it] = (s.astype(BF16), kw.astype(BF16), w_inter,
                  jnp.maximum(jnp.abs(den), jnp.exp(-m)), dec)

    def stage_readout(it):
        r, h = it
        s_b, kw_b, w_inter, den, dec = st[it]
        c0 = c_ref[r, h]
        num = _dot(s_b, vs[it]) + w_inter * _dot(qs[it], c0.astype(BF16))
        c_ref[r, h] = dec * c0 + _dot_tn(kw_b, vs[it])
        dd[it] = num / den

    def stage_head_out(it):
        r, h = it
        hh = dd[it]
        hh = hh * _rms_scale(hh) * na_ref[:, hsl(h)]
        merged[r * T:(r + 1) * T, hsl(h)] = (
            hh * jax.nn.sigmoid(proj["za"][r * T:(r + 1) * T, hsl(h)])).astype(BF16)

    pairs = [(r, pi) for pi in range(H_B // 2) for r in range(RB)]
    psl = lambda pi: slice(pi * LANES, (pi + 1) * LANES)
    sw = {}

    def stage_decay(pr):
        r, pi = pr
        g = pi // pairs_per_group
        gcol, grow = gcols[r], grows[r]
        xpair = xbcs[r][:, psl(pi)]
        scs, a_cols, w_cols, a_lasts = [], [], [], []
        for j in (2 * pi, 2 * pi + 1):
            a_col = gcol[:, L_DTA + j:L_DTA + j + 1]
            a_row = grow[L_DTA + j:L_DTA + j + 1, :]
            dt_col = gcol[:, L_DT + j:L_DT + j + 1]
            dt_row = grow[L_DT + j:L_DT + j + 1, :]
            decay = jnp.exp(jnp.where(causal, a_col - a_row, NEG_INF))
            scs.append((cbs[(r, g)] * decay * dt_row).astype(BF16))
            a_last = a_col[T - 1:T, :]
            a_cols.append(a_col)
            a_lasts.append(a_last)
            w_cols.append(jnp.exp(a_last - a_col) * dt_col)
        sw[pr] = (scs, xpair.astype(BF16),
                  (xpair * jnp.where(left, w_cols[0], w_cols[1])).astype(BF16),
                  jnp.exp(jnp.where(left, a_cols[0], a_cols[1])),
                  jnp.exp(jnp.where(top, a_lasts[0], a_lasts[1])))
    def stage_pair_out(pr):
        r, pi = pr
        g = pi // pairs_per_group
        scs, xpb, xw, ea, ea_last = sw[pr]
        s0 = s_ref[r, pi]
        y = jnp.where(left, _dot(scs[0], xpb), _dot(scs[1], xpb))
        y = y + ea * _dot_nt(cgs[(r, g)], s0.astype(BF16))
        s_ref[r, pi] = ea_last * s0 + _dot_tn(xw, bgs[(r, g)])
        y = y + dsk_ref[:, psl(pi)] * xbcs[r][:, psl(pi)]
        y_buf[r, :, psl(pi)] = y * _silu(proj["zb"][r * T:(r + 1) * T, psl(pi)])

    def each(stage, seq):
        for e in seq:
            stage(e)

    each(stage_qkv, items)
    project("xbc", 2 * d_a + d_b, off_small)
    each(stage_qk, items)
    project("za", d_a, 2 * d_a)
    ssd_inputs()
    each(stage_weights, items)
    each(stage_readout, items)
    project("zb", 2 * d_a, 2 * d_a + d_b)
    for r in range(RB):
        m_ref[r] = m_news[r]
    each(stage_head_out, items)
    each(stage_decay, pairs)
    each(stage_pair_out, pairs)
    gw = d_b // G_B
    for r in range(RB):
        for g in range(G_B):
            yg = y_buf[r, :, g * gw:(g + 1) * gw]
            merged[r * T:(r + 1) * T, d_a + g * gw:d_a + (g + 1) * gw] = (
                yg * _rms_scale(yg) * nb_ref[:, g * gw:(g + 1) * gw]).astype(BF16)

    @pl.when(c > 0)
    def _out():
        out2 = x2 + _dot(merged[...], wout_ref[...])

        @pl.when(jnp.logical_or(c > 1, p > 0))
        def _wait_previous():
            for r in range(RB):
                out_copy(r, 0).wait()

        for r in range(RB):
            hout[r] = out2[r * T:(r + 1) * T]
            out_copy(r, (p * RB + r) * seq + (c - 1) * T).start()

        @pl.when(jnp.logical_and(p == last_p, c == last_c))
        def _drain():
            for r in range(RB):
                out_copy(r, 0).wait()


def _const_spec(shape):
    nd = len(shape)
    return pl.BlockSpec(shape, lambda b, c, _nd=nd: (0,) * _nd)


def _prompt_mixer(x_prompt, xmeta, p, n_extra_rows):
    bsz, seq, d = x_prompt.shape
    assert n_extra_rows == CHUNK and seq % CHUNK == 0
    n_chunks = seq // CHUNK + 1
    cps = seq // CHUNK
    d_a = H_A * DH_A
    conv_b = H_B * HD_B + 2 * G_B * N_STATE
    consts = [p["nmix"], p["wcat"], p["bsm"], p["alog"], p["cwa"], p["cba"], p["cwb"], p["cbb"],
              p["wq"], p["wk"], p["wv"], p["na"], p["nb"], p["dsk"], p["wout"]]
    rb = PROMPT_ROWS
    assert bsz % rb == 0
    in_specs = [_const_spec(xmeta.shape),
                pl.BlockSpec((rb, CHUNK, d), lambda b, c: (b, jnp.maximum(c - 1, 0), 0))]
    in_specs += [_const_spec(a.shape) for a in consts]
    out_shape = (
        jax.ShapeDtypeStruct((bsz * seq + n_extra_rows, d), F32),
        jax.ShapeDtypeStruct((bsz, H_A, DH_A, DH_A), F32),
        jax.ShapeDtypeStruct((bsz, H_A, DH_A), F32),
        jax.ShapeDtypeStruct((bsz, 1, LANES), F32),
        jax.ShapeDtypeStruct((bsz, CONV_W - 1, d_a), F32),
        jax.ShapeDtypeStruct((bsz, H_B // 2, 2 * HD_B, N_STATE), F32),
        jax.ShapeDtypeStruct((bsz, CONV_W - 1, conv_b), F32),
    )
    out_specs = (
        pl.BlockSpec(memory_space=pl.ANY),
        pl.BlockSpec((rb, H_A, DH_A, DH_A), lambda b, c: (b, 0, 0, 0)),
        pl.BlockSpec((rb, H_A, DH_A), lambda b, c: (b, 0, 0)),
        pl.BlockSpec((rb, 1, LANES), lambda b, c: (b, 0, 0)),
        pl.BlockSpec((rb, CONV_W - 1, d_a), lambda b, c: (b, 0, 0)),
        pl.BlockSpec((rb, H_B // 2, 2 * HD_B, N_STATE), lambda b, c: (b, 0, 0, 0)),
        pl.BlockSpec((rb, CONV_W - 1, conv_b), lambda b, c: (b, 0, 0)),
    )
    return pl.pallas_call(
        functools.partial(_prompt_rows_kernel, seq=seq, n_prompt_rows=bsz * seq),
        out_shape=out_shape,
        grid=(bsz // rb, n_chunks),
        in_specs=in_specs,
        out_specs=out_specs,
        scratch_shapes=[
            pltpu.VMEM((rb, CONV_HDR, d_a), F32),
            pltpu.VMEM((rb, CONV_HDR, conv_b), F32),
            pltpu.VMEM((rb, CHUNK, H_B * HD_B), F32),
            pltpu.VMEM((rb * CHUNK, d_a + H_B * HD_B), BF16),
            pltpu.VMEM((rb, CHUNK, d), F32),
            pltpu.SemaphoreType.DMA((rb,)),
        ],
        compiler_params=pltpu.CompilerParams(
            dimension_semantics=("arbitrary", "arbitrary"), vmem_limit_bytes=VMEM_LIMIT),
        name="prompt_mixer",
    )(xmeta, x_prompt, *consts)


def _regroup_w_in_kernel(w_ref, o_ref):
    d_a = H_A * DH_A
    d_b = H_B * HD_B
    conv_b = d_b + 2 * G_B * N_STATE
    o_i = 2 * d_a
    o_f = o_i + H_A
    o_zb = o_f + H_A
    o_xbc = o_zb + d_b
    o_dt = o_xbc + conv_b
    rows = w_ref.shape[0]
    o_ref[:, 0:2 * d_a] = w_ref[:, 0:2 * d_a].astype(BF16)
    o_ref[:, 2 * d_a:2 * d_a + d_b] = w_ref[:, o_zb:o_zb + d_b].astype(BF16)
    o_ref[:, 2 * d_a + d_b:2 * d_a + d_b + conv_b] = w_ref[:, o_xbc:o_xbc + conv_b].astype(BF16)
    small = jnp.concatenate(
        [w_ref[:, o_f:o_f + H_A], w_ref[:, o_dt:o_dt + H_B], w_ref[:, o_i:o_i + H_A],
         w_ref[:, o_dt:o_dt + H_B], jnp.zeros((rows, LANES - (L_DT + H_B)), F32)], axis=1)
    o_ref[:, 2 * d_a + d_b + conv_b:] = small.astype(BF16)


def _prep_mixer_params(norm_mix, w_in, conv_a_w, conv_a_b, w_q, w_k, w_v, b_i, b_f, norm_a,
                       conv_b_w, conv_b_b, dt_bias, a_log, d_skip, norm_b, w_out):
    d_a = H_A * DH_A
    d_b = H_B * HD_B
    conv_b = d_b + 2 * G_B * N_STATE
    w = w_in.reshape(w_in.shape[1], w_in.shape[2])
    n_cols = 2 * d_a + d_b + conv_b + LANES
    rows = 256
    assert w.shape[0] % rows == 0
    wcat = pl.pallas_call(
        _regroup_w_in_kernel,
        out_shape=jax.ShapeDtypeStruct((w.shape[0], n_cols), BF16),
        grid=(w.shape[0] // rows,),
        in_specs=[pl.BlockSpec((rows, w.shape[1]), lambda i: (i, 0))],
        out_specs=pl.BlockSpec((rows, n_cols), lambda i: (i, 0)),
        compiler_params=pltpu.CompilerParams(
            dimension_semantics=("arbitrary",), vmem_limit_bytes=VMEM_LIMIT),
        name="regroup_w_in",
    )(w.astype(F32))

    def lanes(parts):
        pieces, at = [], 0
        for off, a in parts:
            pieces += [jnp.zeros((1, off - at), F32), a.astype(F32)]
            at = off + a.shape[1]
        return jnp.concatenate(pieces + [jnp.zeros((1, LANES - at), F32)], axis=1)

    return dict(
        nmix=norm_mix.reshape(1, -1).astype(F32),
        wcat=wcat,
        bsm=lanes([(L_F, b_f), (L_DTA, dt_bias), (L_I, b_i), (L_DT, dt_bias)]),
        alog=lanes([(L_DTA, a_log)]),
        cwa=conv_a_w.reshape(CONV_W, d_a).astype(F32), cba=conv_a_b.reshape(1, d_a).astype(F32),
        cwb=conv_b_w.reshape(CONV_W, conv_b).astype(F32), cbb=conv_b_b.reshape(1, conv_b).astype(F32),
        wq=w_q.reshape(H_A, DH_A, DH_A).astype(BF16), wk=w_k.reshape(H_A, DH_A, DH_A).astype(BF16),
        wv=w_v.reshape(H_A, DH_A, DH_A).astype(BF16),
        na=norm_a.reshape(1, d_a).astype(F32), nb=norm_b.reshape(1, d_b).astype(F32),
        dsk=jnp.repeat(d_skip.reshape(H_B).astype(F32), HD_B)[None, :],
        wout=w_out.reshape(d_a + d_b, -1).astype(BF16),
    )


SAMPLE_BLOCK = 8


def _expand_lanes(vals, first_lane, n_heads, width):
    r = lax.broadcasted_iota(jnp.int32, (LANES, n_heads * width), 0) - first_lane
    c = lax.broadcasted_iota(jnp.int32, (LANES, n_heads * width), 1)
    sel = jnp.logical_and(c >= r * width, c < (r + 1) * width)
    e = jnp.where(sel, 1.0, 0.0).astype(BF16)
    hi, mid, lo = _split3(vals)
    return (_dot(hi, e) + _dot(mid, e)) + _dot(lo, e)


def _sample_pre_kernel(x_ref, nmix_ref, wcat_ref, bsm_ref, alog_ref, cwa_ref, cba_ref, cwb_ref, cbb_ref,
                       wq_ref, wk_ref, wv_ref, dsk_ref, conva_ref, convb_ref, n0_ref, m0_ref,
                       conva_out, convb_out, n1_out, m1_out, g_out, qt_out, kwt_out, xwt_out,
                       v_out, bc_out, a1_out, w1_out, den_out, y1_out, ea_out, zbs_out, zas_out):
    d_a = H_A * DH_A
    d_b = H_B * HD_B
    conv_b = d_b + 2 * G_B * N_STATE
    shift_i = LANES - (L_I - L_F)
    x = x_ref[...]
    hn = (x * _rms_scale(x) * nmix_ref[...]).astype(BF16)
    lane = lax.broadcasted_iota(jnp.int32, (1, LANES), 1)
    lane_f = lane < L_DTA
    lane_dta = jnp.logical_and(lane >= L_DTA, lane < L_I)
    pre = _dot(hn, wcat_ref[:, 2 * d_a + d_b + conv_b:]) + bsm_ref[...]
    sp, lsig = _softplus_parts(pre)
    a_neg = jnp.where(lane_dta, -jnp.exp(alog_ref[...]), 0.0)
    pre_al = pltpu.roll(pre, shift_i, axis=1)
    sp_al = pltpu.roll(sp, shift_i, axis=1)
    m_inter = lsig + m0_ref[...]
    m = jnp.maximum(m_inter, pre_al)
    w_inter = jnp.exp(m_inter - m)
    sfac = jnp.exp(pre_al - m)
    ea = jnp.exp(sp * a_neg)
    dt = sp_al

    xa = _dot(hn, wcat_ref[:, 0:d_a])
    xc = (cwa_ref[0:1, :] * conva_ref[:, 0:d_a] + cwa_ref[1:2, :] * conva_ref[:, d_a:2 * d_a]
          + cwa_ref[2:3, :] * conva_ref[:, 2 * d_a:3 * d_a] + cwa_ref[3:4, :] * xa + cba_ref[...])
    conva_out[:, 0:2 * d_a] = conva_ref[:, d_a:3 * d_a]
    conva_out[:, 2 * d_a:3 * d_a] = xa
    xc = _silu(xc).astype(BF16)
    xab = xa.astype(BF16)
    sf_e = _expand_lanes(sfac, L_F, H_A, DH_A)
    w_e = _expand_lanes(w_inter, L_F, H_A, DH_A)
    qk8 = jnp.zeros((x.shape[0], LANES), F32)
    qn8 = jnp.zeros((x.shape[0], LANES), F32)
    for h in range(H_A):
        sl = slice(h * DH_A, (h + 1) * DH_A)
        q = _dot(xc[:, sl], wq_ref[h])
        k = _dot(xc[:, sl], wk_ref[h]) * (DH_A ** -0.5)
        v = _dot(xab[:, sl], wv_ref[h])
        kw = k * sf_e[:, sl]
        qk8 = jnp.where(lane == h, jnp.sum(q * k, axis=-1, keepdims=True), qk8)
        qn8 = jnp.where(lane == h, jnp.sum(q * n0_ref[:, sl], axis=-1, keepdims=True), qn8)
        n1_out[:, sl] = w_e[:, sl] * n0_ref[:, sl] + kw
        v_out[:, sl] = v
        qt_out[h] = q.T
        kwt_out[h] = kw.T
    s8 = qk8 * sfac
    a1_out[...] = _expand_lanes(s8, L_F, H_A, DH_A) * v_out[...]
    w1_out[...] = w_e
    den_out[...] = jnp.maximum(jnp.abs(_expand_lanes(s8 + w_inter * qn8, L_F, H_A, DH_A)),
                               jnp.exp(-_expand_lanes(m, L_F, H_A, DH_A)))
    m1_out[...] = m
    g_out[...] = jnp.where(lane_f, w_inter, jnp.where(lane_dta, ea, 0.0))
    zas_out[...] = jax.nn.sigmoid(_dot(hn, wcat_ref[:, d_a:2 * d_a]))

    off_xbc = 2 * d_a + d_b
    xbc = _dot(hn, wcat_ref[:, off_xbc:off_xbc + conv_b])
    xbc_c = (cwb_ref[0:1, :] * convb_ref[:, 0:conv_b] + cwb_ref[1:2, :] * convb_ref[:, conv_b:2 * conv_b]
             + cwb_ref[2:3, :] * convb_ref[:, 2 * conv_b:3 * conv_b] + cwb_ref[3:4, :] * xbc + cbb_ref[...])
    convb_out[:, 0:2 * conv_b] = convb_ref[:, conv_b:3 * conv_b]
    convb_out[:, 2 * conv_b:3 * conv_b] = xbc
    xbc_c = _silu(xbc_c)
    xs = xbc_c[:, 0:d_b]
    bc = xbc_c[:, d_b:conv_b]
    bc_out[...] = bc
    heads_per_group = H_B // G_B
    cbl = jnp.zeros((x.shape[0], LANES), F32)
    for g in range(G_B):
        cb_g = jnp.sum(bc[:, g * N_STATE:(g + 1) * N_STATE]
                       * bc[:, (G_B + g) * N_STATE:(G_B + g + 1) * N_STATE], axis=-1, keepdims=True)
        in_g = jnp.logical_and(lane >= L_DTA + g * heads_per_group,
                               lane < L_DTA + (g + 1) * heads_per_group)
        cbl = jnp.where(in_g, cb_g, cbl)
    dt_e = _expand_lanes(dt, L_DTA, H_B, HD_B)
    y1_out[...] = _expand_lanes(cbl * dt, L_DTA, H_B, HD_B) * xs + dsk_ref[...] * xs
    ea_out[...] = _expand_lanes(ea, L_DTA, H_B, HD_B)
    zbs_out[...] = _silu(_dot(hn, wcat_ref[:, 2 * d_a:2 * d_a + d_b]))
    xw = xs * dt_e
    for pi in range(H_B // 2):
        xwt_out[pi] = xw[:, pi * LANES:(pi + 1) * LANES].T


def _sample_state_kernel(g_ref, c0_ref, s0_ref, qt_ref, kwt_ref, xwt_ref, v_ref, bc_ref,
                         c1_ref, s1_ref, qc_ref, ysi_ref):
    i = pl.program_id(0)
    bb = c0_ref.shape[0]
    shift = lax.rem(LANES - lax.rem(i * bb, LANES), LANES)
    lane = lax.broadcasted_iota(jnp.int32, (1, LANES), 1)
    top = lax.broadcasted_iota(jnp.int32, (LANES, 1), 0) < HD_B
    heads_per_group = H_B // G_B
    for h in range(H_A):
        sl = slice(h * DH_A, (h + 1) * DH_A)
        qt = pltpu.roll(qt_ref[h], shift, axis=1)
        kwt = pltpu.roll(kwt_ref[h], shift, axis=1)
        for r in range(bb):
            b = i * bb + r
            c0 = c0_ref[r, h]
            dec = g_ref[b, L_F + h]
            v_row = v_ref[r:r + 1, sl]
            qc_ref[r:r + 1, sl] = jnp.sum(c0 * qt[:, r:r + 1], axis=0, keepdims=True)
            c1_ref[r, h] = dec * c0 + kwt[:, r:r + 1] * v_row
    for pi in range(H_B // 2):
        g = (2 * pi) // heads_per_group
        sl = slice(pi * LANES, (pi + 1) * LANES)
        xwt = pltpu.roll(xwt_ref[pi], shift, axis=1)
        acc = jnp.zeros((LANES, LANES), F32)
        for r in range(bb):
            b = i * bb + r
            s0 = s0_ref[r, pi]
            b_row = bc_ref[r:r + 1, g * N_STATE:(g + 1) * N_STATE]
            c_row = bc_ref[r:r + 1, (G_B + g) * N_STATE:(G_B + g + 1) * N_STATE]
            col = jnp.sum(s0 * c_row, axis=-1, keepdims=True)
            acc = jnp.where(lane == r, col, acc)
            ea_rows = jnp.where(top, g_ref[b, L_DTA + 2 * pi], g_ref[b, L_DTA + 2 * pi + 1])
            s1_ref[r, pi] = ea_rows * s0 + xwt[:, r:r + 1] * b_row
        ysi_ref[:, sl] = acc.T[0:bb, :]


def _sample_post_kernel(a1_ref, w1_ref, den_ref, y1_ref, ea_ref, zbs_ref, zas_ref, x_ref, qc_ref, ysi_ref,
                        na_ref, nb_ref, wout_ref, hall_ref, hmid_ref, merged):
    del hall_ref
    d_a = H_A * DH_A
    d_b = H_B * HD_B
    hh = (a1_ref[...] + w1_ref[...] * qc_ref[...]) / den_ref[...]
    for h in range(H_A):
        sl = slice(h * DH_A, (h + 1) * DH_A)
        hs = hh[:, sl]
        merged[:, sl] = (hs * _rms_scale(hs) * na_ref[:, sl] * zas_ref[:, sl]).astype(BF16)
    y = (y1_ref[...] + ea_ref[...] * ysi_ref[...]) * zbs_ref[...]
    gw = d_b // G_B
    for g in range(G_B):
        yg = y[:, g * gw:(g + 1) * gw]
        merged[:, d_a + g * gw:d_a + (g + 1) * gw] = (
            yg * _rms_scale(yg) * nb_ref[:, g * gw:(g + 1) * gw]).astype(BF16)
    hmid_ref[...] = x_ref[...] + _dot(merged[...], wout_ref[...])


def _vmem_specs(arrays):
    return [pl.BlockSpec(a.shape, lambda *_, _nd=a.ndim: (0,) * _nd) for a in arrays]


def _sample_mixer(x, c0, n0, m0, conva, s0, convb, p, hmid_all, row_offset):
    nb, d = x.shape
    d_a = H_A * DH_A
    d_b = H_B * HD_B
    conv_b = d_b + 2 * G_B * N_STATE
    row = lambda w: jax.ShapeDtypeStruct((nb, w), F32)
    tile = lambda k: jax.ShapeDtypeStruct((k, LANES, nb), F32)
    pre_in = [x, p["nmix"], p["wcat"], p["bsm"], p["alog"], p["cwa"], p["cba"], p["cwb"], p["cbb"],
              p["wq"], p["wk"], p["wv"], p["dsk"], conva, convb, n0, m0]
    pre_out_shape = (row(3 * d_a), row(3 * conv_b), row(d_a), row(LANES), row(LANES),
                     tile(H_A), tile(H_A), tile(H_B // 2), row(d_a), row(2 * G_B * N_STATE),
                     row(d_a), row(d_a), row(d_a), row(d_b), row(d_b), row(d_b), row(d_a))
    (conva1, convb1, n1, m1, g8, qt, kwt, xwt, v, bc, a1, w1, den, y1, ea_e, zbs, zas) = pl.pallas_call(
        _sample_pre_kernel,
        out_shape=pre_out_shape,
        grid=(1,),
        in_specs=_vmem_specs(pre_in),
        out_specs=tuple(pl.BlockSpec(s.shape, lambda i, _nd=len(s.shape): (0,) * _nd) for s in pre_out_shape),
        compiler_params=pltpu.CompilerParams(
            dimension_semantics=("arbitrary",), vmem_limit_bytes=VMEM_LIMIT),
        name="sample_pre",
    )(*pre_in)

    bb = SAMPLE_BLOCK
    const3 = lambda k: pl.BlockSpec((k, LANES, nb), lambda i, g: (0, 0, 0))
    state_grid = pltpu.PrefetchScalarGridSpec(
        num_scalar_prefetch=1,
        grid=(nb // bb,),
        in_specs=[pl.BlockSpec((bb, H_A, DH_A, DH_A), lambda i, g: (i, 0, 0, 0)),
                  pl.BlockSpec((bb, H_B // 2, 2 * HD_B, N_STATE), lambda i, g: (i, 0, 0, 0)),
                  const3(H_A), const3(H_A), const3(H_B // 2),
                  pl.BlockSpec((bb, d_a), lambda i, g: (i, 0)),
                  pl.BlockSpec((bb, 2 * G_B * N_STATE), lambda i, g: (i, 0))],
        out_specs=(pl.BlockSpec((bb, H_A, DH_A, DH_A), lambda i, g: (i, 0, 0, 0)),
                   pl.BlockSpec((bb, H_B // 2, 2 * HD_B, N_STATE), lambda i, g: (i, 0, 0, 0)),
                   pl.BlockSpec((bb, d_a), lambda i, g: (i, 0)),
                   pl.BlockSpec((bb, d_b), lambda i, g: (i, 0))),
    )
    c1, s1, qc, ysi = pl.pallas_call(
        _sample_state_kernel,
        out_shape=(jax.ShapeDtypeStruct(c0.shape, F32), jax.ShapeDtypeStruct(s0.shape, F32),
                   row(d_a), row(d_b)),
        grid_spec=state_grid,
        compiler_params=pltpu.CompilerParams(
            dimension_semantics=("arbitrary",), vmem_limit_bytes=VMEM_LIMIT),
        name="sample_state",
    )(g8, c0, s0, qt, kwt, xwt, v, bc)

    post_in = [a1, w1, den, y1, ea_e, zbs, zas, x, qc, ysi, p["na"], p["nb"], p["wout"]]
    hmid_all = pl.pallas_call(
        _sample_post_kernel,
        out_shape=jax.ShapeDtypeStruct(hmid_all.shape, F32),
        grid=(1,),
        in_specs=_vmem_specs(post_in) + [pl.BlockSpec(memory_space=pl.ANY)],
        out_specs=pl.BlockSpec((nb, d), lambda i: (row_offset // nb, 0)),
        scratch_shapes=[pltpu.VMEM((nb, d_a + d_b), BF16)],
        input_output_aliases={len(post_in): 0},
        compiler_params=pltpu.CompilerParams(
            dimension_semantics=("arbitrary",), vmem_limit_bytes=VMEM_LIMIT),
        name="sample_post",
    )(*post_in, hmid_all)
    return hmid_all, c1, n1, m1, conva1, s1, convb1


R_EA, R_EB, R_RA, R_RB, R_GA, R_GB = 0, 1, 2, 3, 4, 5
RL_E = N_EGROUPS


def _router_kernel(h_ref, nf_ref, whi_ref, wmid_ref, br_ref, xn_ref, info_ref, cnt_ref, carry):
    i = pl.program_id(0)
    tr = h_ref.shape[0]

    @pl.when(i == 0)
    def _init():
        carry[...] = jnp.zeros_like(carry)

    h = h_ref[...]
    xn = h * _rms_scale(h) * nf_ref[...]
    _store_token_tiles(xn_ref, xn)
    x_hi, x_mid, _ = _split3(xn)
    logits = (_dot(x_hi, whi_ref[...]) + _dot(x_hi, wmid_ref[...]) + _dot(x_mid, whi_ref[...])
              + br_ref[...])
    lane_i = lax.broadcasted_iota(jnp.int32, (1, LANES), 1)
    lane = lane_i.astype(F32)
    big = float(LANES)

    def first_lane_of(cond):
        return jnp.min(jnp.where(cond, lane, big), axis=-1, keepdims=True)

    l1 = jnp.where(lane_i < N_EGROUPS, logits, NEG_INF)
    e1 = jnp.exp(l1 - jnp.max(l1, axis=-1, keepdims=True))
    p1 = e1 / jnp.sum(e1, axis=-1, keepdims=True)
    gp = jnp.max(p1, axis=-1, keepdims=True)
    gidx = first_lane_of(p1 == gp)
    lo = RL_E + N_EPG * gidx
    l2 = jnp.where(jnp.logical_and(lane >= lo, lane < lo + N_EPG), logits, NEG_INF)
    va = jnp.max(l2, axis=-1, keepdims=True)
    ia = first_lane_of(l2 == va)
    l2b = jnp.where(lane == ia, NEG_INF, l2)
    vb = jnp.max(l2b, axis=-1, keepdims=True)
    ib = first_lane_of(l2b == vb)
    eb = jnp.exp(vb - va)
    wa = 1.0 / (1.0 + eb)
    wb = eb / (1.0 + eb)

    is_a = lane == ia
    is_b = lane == ib
    onehot = jnp.where(jnp.logical_or(is_a, is_b), 1.0, 0.0)
    ri = lax.broadcasted_iota(jnp.int32, (tr, tr), 0)
    ci = lax.broadcasted_iota(jnp.int32, (tr, tr), 1)
    tri = jnp.where(ri >= ci, 1.0, 0.0).astype(BF16)
    incl = _dot(tri, onehot.astype(BF16))
    excl = incl - onehot + carry[...]
    rank_a = jnp.sum(jnp.where(is_a, excl, 0.0), axis=-1, keepdims=True)
    rank_b = jnp.sum(jnp.where(is_b, excl, 0.0), axis=-1, keepdims=True)
    carry[...] = carry[...] + incl[tr - 1:tr, :]
    cnt_ref[...] = carry[...]

    info = jnp.where(lane_i == R_EA, ia - RL_E, 0.0)
    info = jnp.where(lane_i == R_EB, ib - RL_E, info)
    info = jnp.where(lane_i == R_RA, rank_a, info)
    info = jnp.where(lane_i == R_RB, rank_b, info)
    info = jnp.where(lane_i == R_GA, gp * wa, info)
    info = jnp.where(lane_i == R_GB, gp * wb, info)
    info_ref[...] = info


def _row_tile(n, candidates):
    for t in candidates:
        if n % t == 0:
            return t
    raise ValueError(f"no row tile for {n} rows among {candidates}")


def _router(hmid, rp):
    n, d = hmid.shape
    assert d == TOK_TILE_ROWS * LANES
    tr = _row_tile(n, (512, 384, 256, 128))
    return pl.pallas_call(
        _router_kernel,
        out_shape=(jax.ShapeDtypeStruct((n * TOK_TILE_ROWS, LANES), F32),
                   jax.ShapeDtypeStruct((n, LANES), F32),
                   jax.ShapeDtypeStruct((1, LANES), F32)),
        grid=(n // tr,),
        in_specs=[pl.BlockSpec((tr, d), lambda i: (i, 0)),
                  pl.BlockSpec((1, d), lambda i: (0, 0)),
                  pl.BlockSpec((d, LANES), lambda i: (0, 0)),
                  pl.BlockSpec((d, LANES), lambda i: (0, 0)),
                  pl.BlockSpec((1, LANES), lambda i: (0, 0))],
        out_specs=(pl.BlockSpec((tr * TOK_TILE_ROWS, LANES), lambda i: (i, 0)),
                   pl.BlockSpec((tr, LANES), lambda i: (i, 0)),
                   pl.BlockSpec((1, LANES), lambda i: (0, 0))),
        scratch_shapes=[pltpu.VMEM((1, LANES), F32)],
        compiler_params=pltpu.CompilerParams(
            dimension_semantics=("arbitrary",), vmem_limit_bytes=VMEM_LIMIT),
        name="router",
    )(hmid, rp["nf"], rp["whi"], rp["wmid"], rp["br"])


def _prep_router_params(norm_ffn, w_r1, b_r1, w_r2, b_r2):
    d = w_r1.shape[1]
    w = jnp.concatenate([w_r1.reshape(d, N_EGROUPS).astype(F32), w_r2.reshape(d, N_EXPERTS).astype(F32),
                         jnp.zeros((d, LANES - RL_E - N_EXPERTS), F32)], axis=1)
    whi = w.astype(BF16)
    wmid = (w - whi.astype(F32)).astype(BF16)
    br = jnp.concatenate([b_r1.reshape(1, N_EGROUPS).astype(F32), b_r2.reshape(1, N_EXPERTS).astype(F32),
                          jnp.zeros((1, LANES - RL_E - N_EXPERTS), F32)], axis=1)
    return dict(nf=norm_ffn.reshape(1, d).astype(F32), whi=whi, wmid=wmid, br=br)


FFN_TM = 256


def _start_tile_gather(first_row_of, n_rows, src_hbm, dst, sem, priority_of):
    for r in range(n_rows):
        start = pl.multiple_of(first_row_of(r), TOK_TILE_ROWS)
        pltpu.make_async_copy(src_hbm.at[pl.ds(start, TOK_TILE_ROWS), :],
                              dst.at[pl.ds(r * TOK_TILE_ROWS, TOK_TILE_ROWS), :],
                              sem).start(priority=priority_of(r))


def _wait_tile_gather(n_rows, src_hbm, dst, sem):
    pltpu.make_async_copy(src_hbm.at[pl.ds(0, n_rows * TOK_TILE_ROWS), :], dst, sem).wait()


TAB_OFF, TAB_CNT, TAB_TILE_EXPERT, TAB_NVALID = 0, 1, 2, 3
TAB_LANES = 2 * LANES


def _ffn_kernel(tab_ref, pa_ref, pb_ref, xn_hbm, wg_ref, wu_ref, wd_ref, ys_ref,
                src, xbuf, wbf, sem, *, n_tokens, tm):
    i = pl.program_id(0)
    n_valid = tab_ref[TAB_NVALID, 0]
    slot = lax.rem(i, 2)
    gather_priority = lambda r: 1

    @pl.when(i == 0)
    def _build_source_rows():
        for e in range(N_EXPERTS):
            cnt_e = tab_ref[TAB_CNT, RL_E + e]
            first = tab_ref[TAB_OFF, RL_E + e] + cnt_e
            n_pad = lax.rem(tm - lax.rem(cnt_e, tm), tm)

            def pad_body(r, carry, first=first):
                src[first + r] = 0
                return carry
            lax.fori_loop(0, n_pad, pad_body, 0)

        def body(t, carry):
            first_row = t * TOK_TILE_ROWS
            src[pa_ref[t]] = first_row
            src[pb_ref[t]] = first_row
            return carry
        lax.fori_loop(0, n_tokens, body, 0, unroll=8)
        _start_tile_gather(lambda r: src[r], tm, xn_hbm, xbuf.at[0], sem.at[0], gather_priority)

    @pl.when(i + 1 < n_valid)
    def _next():
        base = (i + 1) * tm
        _start_tile_gather(lambda r: src[base + r], tm, xn_hbm, xbuf.at[1 - slot], sem.at[1 - slot],
                           gather_priority)

    changed = jnp.logical_or(i == 0, tab_ref[TAB_TILE_EXPERT, i]
                             != tab_ref[TAB_TILE_EXPERT, jnp.maximum(i - 1, 0)])

    @pl.when(jnp.logical_and(changed, i < n_valid))
    def _cast_weights():
        wbf[0] = wg_ref[...].astype(BF16)
        wbf[1] = wu_ref[...].astype(BF16)
        wbf[2] = wd_ref[...].astype(BF16)

    @pl.when(i < n_valid)
    def _compute():
        _wait_tile_gather(tm, xn_hbm, xbuf.at[slot], sem.at[slot])
        x = _load_token_tiles(xbuf.at[slot], tm).astype(BF16)
        hg = _dot(x, wbf[0])
        hu = _dot(x, wbf[1])
        _store_token_tiles(ys_ref, _dot((_silu(hg) * hu).astype(BF16), wbf[2]))

    @pl.when(i >= n_valid)
    def _pad():
        ys_ref[...] = jnp.zeros_like(ys_ref)


def _expert_ffn(xn_tiles, tab, pos_a, pos_b, n_tiles, wg, wu, wd):
    n = pos_a.shape[0]
    tm = FFN_TM
    d, dff = wg.shape[1], wg.shape[2]
    rows = tm * TOK_TILE_ROWS
    idx = lambda i, tab, pa, pb: (tab[TAB_TILE_EXPERT, i], 0, 0)
    grid_spec = pltpu.PrefetchScalarGridSpec(
        num_scalar_prefetch=3,
        grid=(n_tiles,),
        in_specs=[pl.BlockSpec(memory_space=pl.ANY),
                  pl.BlockSpec((None, d, dff), idx),
                  pl.BlockSpec((None, d, dff), idx),
                  pl.BlockSpec((None, dff, d), idx)],
        out_specs=pl.BlockSpec((rows, LANES), lambda i, tab, pa, pb: (i, 0)),
        scratch_shapes=[pltpu.SMEM((n_tiles * tm,), jnp.int32),
                        pltpu.VMEM((2, rows, LANES), F32),
                        pltpu.VMEM((3, d, dff), BF16),
                        pltpu.SemaphoreType.DMA((2,))],
    )
    return pl.pallas_call(
        functools.partial(_ffn_kernel, n_tokens=n, tm=tm),
        out_shape=jax.ShapeDtypeStruct((n_tiles * rows, LANES), F32),
        grid_spec=grid_spec,
        compiler_params=pltpu.CompilerParams(
            dimension_semantics=("arbitrary",), vmem_limit_bytes=VMEM_LIMIT),
        name="expert_ffn",
    )(tab, pos_a, pos_b, xn_tiles, wg, wu, wd)


def _positions_kernel(info_ref, cnt_ref, pos_ref, tab_ref, *, tm, chunk):
    lane_i = lax.broadcasted_iota(jnp.int32, (1, LANES), 1)
    lane = lane_i.astype(F32)
    is_expert = jnp.logical_and(lane_i >= RL_E, lane_i < RL_E + N_EXPERTS)
    cnt = jnp.where(is_expert, cnt_ref[...], 0.0)
    padded = jnp.floor((cnt + (tm - 1)) / tm) * tm
    ri = lax.broadcasted_iota(jnp.int32, (LANES, LANES), 0)
    ci = lax.broadcasted_iota(jnp.int32, (LANES, LANES), 1)
    before = jnp.where(ri < ci, 1.0, 0.0).astype(BF16)
    hi, mid, lo = _split3(jnp.broadcast_to(padded, (SUBLANES, LANES)))
    off = ((_dot(hi, before) + _dot(mid, before)) + _dot(lo, before))[0:1, :]
    pick = jnp.where(lax.broadcasted_iota(jnp.int32, (SUBLANES, LANES), 0) == lane_i, 1.0, 0.0).astype(BF16)

    total = jnp.sum(padded, axis=-1, keepdims=True)
    tile_row = lax.broadcasted_iota(jnp.int32, (TAB_LANES, 1), 0).astype(F32) * tm
    ends = off + padded
    done = jnp.logical_and(is_expert, ends <= jnp.minimum(tile_row, total - 1.0))
    te_col = jnp.sum(jnp.where(done, 1.0, 0.0), axis=-1, keepdims=True)
    te_rows = _dot_nt(pick, jnp.where(lane_i == 0, te_col, 0.0).astype(BF16))
    tab_ref[...] = jnp.zeros_like(tab_ref)
    tab_ref[TAB_OFF:TAB_OFF + 1, 0:LANES] = off.astype(jnp.int32)
    tab_ref[TAB_CNT:TAB_CNT + 1, 0:LANES] = cnt.astype(jnp.int32)
    tab_ref[TAB_TILE_EXPERT:TAB_TILE_EXPERT + 1, :] = te_rows[0:1, :].astype(jnp.int32)
    tab_ref[TAB_NVALID:TAB_NVALID + 1, 0:LANES] = jnp.broadcast_to(total / tm, (1, LANES)).astype(jnp.int32)

    n = info_ref.shape[0]
    for c0 in range(0, n, chunk):
        blk = info_ref[c0:c0 + chunk, :]
        lane_a = blk[:, R_EA:R_EA + 1] + RL_E
        lane_b = blk[:, R_EB:R_EB + 1] + RL_E
        pos_a = blk[:, R_RA:R_RA + 1] + jnp.sum(jnp.where(lane == lane_a, off, 0.0), axis=-1, keepdims=True)
        pos_b = blk[:, R_RB:R_RB + 1] + jnp.sum(jnp.where(lane == lane_b, off, 0.0), axis=-1, keepdims=True)
        z_hi, z_mid, z_lo = _split3(jnp.where(lane_i == 0, pos_a, jnp.where(lane_i == 1, pos_b, 0.0)))
        rows = (_dot_nt(pick, z_hi) + _dot_nt(pick, z_mid)) + _dot_nt(pick, z_lo)
        pos_ref[:, c0:c0 + chunk] = rows.astype(jnp.int32)


def _routing_tables(info, counts, n_tiles, tm):
    n = info.shape[0]
    groups = n // LANES
    assert n_tiles <= TAB_LANES
    chunk = LANES * max(g for g in range(1, 65) if groups % g == 0)
    pos, tab = pl.pallas_call(
        functools.partial(_positions_kernel, tm=tm, chunk=chunk),
        out_shape=(jax.ShapeDtypeStruct((SUBLANES, n), jnp.int32),
                   jax.ShapeDtypeStruct((SUBLANES, TAB_LANES), jnp.int32)),
        grid=(1,),
        in_specs=[pl.BlockSpec((n, LANES), lambda i: (0, 0)), pl.BlockSpec((1, LANES), lambda i: (0, 0))],
        out_specs=(pl.BlockSpec((SUBLANES, n), lambda i: (0, 0)),
                   pl.BlockSpec((SUBLANES, TAB_LANES), lambda i: (0, 0))),
        compiler_params=pltpu.CompilerParams(
            dimension_semantics=("arbitrary",), vmem_limit_bytes=VMEM_LIMIT),
        name="positions",
    )(info, counts)
    return tab, pos[0], pos[1]


def _combine_kernel(pa_ref, pb_ref, h_ref, info_ref, ys_hbm, nfin_ref, yp_ref, ysm_ref,
                    buf_a, buf_b, sem, *, n_prompt_tiles):
    i = pl.program_id(0)
    n_steps = pl.num_programs(0)
    tt = h_ref.shape[0]
    slot = lax.rem(i, 2)

    def start(tile, s):
        base = tile * tt
        _start_tile_gather(lambda r: pa_ref[base + r] * TOK_TILE_ROWS, tt, ys_hbm, buf_a.at[s],
                           sem.at[s], lambda r: 0)
        _start_tile_gather(lambda r: pb_ref[base + r] * TOK_TILE_ROWS, tt, ys_hbm, buf_b.at[s],
                           sem.at[s], lambda r: 1)

    @pl.when(i == 0)
    def _first():
        start(0, 0)

    @pl.when(i + 1 < n_steps)
    def _next():
        start(i + 1, 1 - slot)

    _wait_tile_gather(tt, ys_hbm, buf_a.at[slot], sem.at[slot])
    _wait_tile_gather(tt, ys_hbm, buf_b.at[slot], sem.at[slot])
    info = info_ref[...]
    h = (h_ref[...] + info[:, R_GA:R_GA + 1] * _load_token_tiles(buf_a.at[slot], tt)
         + info[:, R_GB:R_GB + 1] * _load_token_tiles(buf_b.at[slot], tt))
    y = h * _rms_scale(h) * nfin_ref[...]

    @pl.when(i < n_prompt_tiles)
    def _prompt():
        yp_ref[...] = y

    @pl.when(i >= n_prompt_tiles)
    def _sample():
        ysm_ref[...] = y


def _combine(hmid, info, ys, pos_a, pos_b, nfin, n_prompt):
    n, d = hmid.shape
    tt = CHUNK
    n_prompt_tiles = n_prompt // tt
    n_sample = n - n_prompt
    grid_spec = pltpu.PrefetchScalarGridSpec(
        num_scalar_prefetch=2,
        grid=(n // tt,),
        in_specs=[pl.BlockSpec((tt, d), lambda i, pa, pb: (i, 0)),
                  pl.BlockSpec((tt, LANES), lambda i, pa, pb: (i, 0)),
                  pl.BlockSpec(memory_space=pl.ANY),
                  pl.BlockSpec((1, d), lambda i, pa, pb: (0, 0))],
        out_specs=(pl.BlockSpec((tt, d), lambda i, pa, pb: (jnp.minimum(i, n_prompt_tiles - 1), 0)),
                   pl.BlockSpec((tt, d), lambda i, pa, pb: (jnp.maximum(i - n_prompt_tiles, 0), 0))),
        scratch_shapes=[pltpu.VMEM((2, tt * TOK_TILE_ROWS, LANES), F32),
                        pltpu.VMEM((2, tt * TOK_TILE_ROWS, LANES), F32),
                        pltpu.SemaphoreType.DMA((2,))],
    )
    return pl.pallas_call(
        functools.partial(_combine_kernel, n_prompt_tiles=n_prompt_tiles),
        out_shape=(jax.ShapeDtypeStruct((n_prompt, d), F32),
                   jax.ShapeDtypeStruct((n_sample, d), F32)),
        grid_spec=grid_spec,
        compiler_params=pltpu.CompilerParams(
            dimension_semantics=("arbitrary",), vmem_limit_bytes=VMEM_LIMIT),
        name="combine",
    )(pos_a, pos_b, hmid, info, ys, nfin)


def _moe_and_final_norm(hmid, n_prompt, rp, wg, wu, wd, nfin):
    n = hmid.shape[0]
    tm = FFN_TM
    n_tiles = (2 * n + N_EXPERTS * (tm - 1)) // tm
    xn, info, counts = _router(hmid, rp)
    tab, pos_a, pos_b = _routing_tables(info, counts, n_tiles, tm)
    ys = _expert_ffn(xn, tab, pos_a, pos_b, n_tiles, wg, wu, wd)
    return _combine(hmid, info, ys, pos_a, pos_b, nfin, n_prompt)


def kernel(x_prompt, x_sample, state_mlstm_C, state_mlstm_n, state_mlstm_m, state_mlstm_conv, state_ssm, state_ssm_conv, meta_tokens, norm_mix, w_in, conv_a_w, conv_a_b, w_q, w_k, w_v, b_i, b_f, norm_a, conv_b_w, conv_b_b, dt_bias, a_log, d_skip, norm_b, w_out, norm_ffn, w_r1, b_r1, w_r2, b_r2, w_gate, w_up, w_down, norm_final):
    bsz, seq, d = x_prompt.shape
    nb = x_sample.shape[0]
    d_a = H_A * DH_A
    conv_b = H_B * HD_B + 2 * G_B * N_STATE
    assert w_in.shape[0] == 1 and x_sample.shape[1] == 1 and seq % CHUNK == 0 and nb == CHUNK
    mp = _prep_mixer_params(norm_mix, w_in, conv_a_w, conv_a_b, w_q, w_k, w_v, b_i, b_f, norm_a,
                            conv_b_w, conv_b_b, dt_bias, a_log, d_skip, norm_b, w_out)
    rp = _prep_router_params(norm_ffn, w_r1, b_r1, w_r2, b_r2)
    xmeta = jnp.concatenate([jnp.zeros((CHUNK - N_META, d), F32), meta_tokens.astype(F32)], 0)

    hmid, p_c, p_n, p_m, p_ca, p_s, p_cb = _prompt_mixer(x_prompt.astype(F32), xmeta, mp, nb)
    m0 = jnp.pad(state_mlstm_m.reshape(nb, H_A).astype(F32), ((0, 0), (0, LANES - H_A)))
    hmid, s_c, s_n, s_m, s_ca, s_s, s_cb = _sample_mixer(
        x_sample.reshape(nb, d).astype(F32),
        state_mlstm_C.reshape(nb, H_A, DH_A, DH_A).astype(F32),
        state_mlstm_n.reshape(nb, d_a).astype(F32),
        m0,
        state_mlstm_conv.reshape(nb, (CONV_W - 1) * d_a).astype(F32),
        state_ssm.reshape(nb, H_B // 2, 2 * HD_B, N_STATE).astype(F32),
        state_ssm_conv.reshape(nb, (CONV_W - 1) * conv_b).astype(F32),
        mp, hmid, bsz * seq)

    wshape = w_gate.shape[1:]
    y_p, y_s = _moe_and_final_norm(
        hmid, bsz * seq, rp, w_gate.reshape(wshape).astype(F32), w_up.reshape(wshape).astype(F32),
        w_down.reshape(w_down.shape[1:]).astype(F32), norm_final.reshape(1, d).astype(F32))

    return (y_p.reshape(bsz, seq, d), y_s.reshape(nb, 1, d),
            p_c.reshape(1, bsz, H_A, DH_A, DH_A), p_n.reshape(1, bsz, H_A, DH_A),
            p_m[:, 0, :H_A].reshape(1, bsz, H_A), p_ca.reshape(1, bsz, CONV_W - 1, d_a),
            p_s.reshape(1, bsz, H_B, HD_B, N_STATE), p_cb.reshape(1, bsz, CONV_W - 1, conv_b),
            s_c.reshape(1, nb, H_A, DH_A, DH_A), s_n.reshape(1, nb, H_A, DH_A),
            s_m[:, :H_A].reshape(1, nb, H_A), s_ca.reshape(1, nb, CONV_W - 1, d_a),
            s_s.reshape(1, nb, H_B, HD_B, N_STATE), s_cb.reshape(1, nb, CONV_W - 1, conv_b))
```

```python
import functools
import math

import jax
import jax.numpy as jnp
from jax import lax
from jax.experimental import pallas as pl
from jax.experimental.pallas import tpu as pltpu

F32 = jnp.float32
BF16 = jnp.bfloat16

EPS = 1e-6
N_META = 16
CONV_W = 4
CHUNK = 128
H_A = 8
DH_A = 128
H_B = 16
HD_B = 64
N_STATE = 128
G_B = 2
N_EGROUPS = 4
N_EPG = 4
N_EXPERTS = 16
LANES = 128
SUBLANES = 8
CONV_HDR = SUBLANES
VMEM_LIMIT = 56 * 1024 * 1024

L_F = 0
L_DTA = 8
L_I = 24
L_DT = 32

NEG_INF = float("-inf")


def _dot(a, b):
    return jnp.dot(a, b, preferred_element_type=F32)


def _dot_nt(a, b):
    return lax.dot_general(a, b, (((1,), (1,)), ((), ())), preferred_element_type=F32)


def _dot_tn(a, b):
    return lax.dot_general(a, b, (((0,), (0,)), ((), ())), preferred_element_type=F32)


def _split3(x):
    hi = x.astype(BF16)
    r = x - hi.astype(F32)
    mid = r.astype(BF16)
    lo = (r - mid.astype(F32)).astype(BF16)
    return hi, mid, lo


def _silu(x):
    return x * jax.nn.sigmoid(x)


def _softplus_parts(x):
    t = jnp.log1p(jnp.exp(-jnp.abs(x)))
    return jnp.maximum(x, 0.0) + t, jnp.minimum(x, 0.0) - t


def _rms_scale(x):
    return lax.rsqrt(jnp.mean(x * x, axis=-1, keepdims=True) + EPS)


TOK_TILE_ROWS = SUBLANES


def _store_token_tiles(ref, x):
    n = x.shape[0]
    for j in range(TOK_TILE_ROWS):
        ref[pl.ds(j, n, stride=TOK_TILE_ROWS), :] = x[:, j * LANES:(j + 1) * LANES]


def _causal_conv(x, tail, w_ref, b_ref):
    n_tail = tail.shape[0]
    row = lax.broadcasted_iota(jnp.int32, (n_tail, 1), 0)
    acc = w_ref[CONV_W - 1:CONV_W, :] * x + b_ref[...]
    for k in range(1, CONV_W):
        rolled = pltpu.roll(x, k, axis=0)
        head = jnp.where(row < k, pltpu.roll(tail, k, axis=0), rolled[0:n_tail])
        shifted = jnp.concatenate([head, rolled[n_tail:]], axis=0)
        acc = acc + w_ref[CONV_W - 1 - k:CONV_W - k, :] * shifted
    return acc


def _load_token_tiles(ref, n):
    return jnp.concatenate(
        [ref[pl.ds(j, n, stride=TOK_TILE_ROWS), :] for j in range(TOK_TILE_ROWS)], axis=1)


PROMPT_ROWS = 1


def _prompt_rows_kernel(xmeta_ref, xp_ref, nmix_ref, wcat_ref, bsm_ref, alog_ref,
                        cwa_ref, cba_ref, cwb_ref, cbb_ref, wq_ref, wk_ref, wv_ref,
                        na_ref, nb_ref, dsk_ref, wout_ref,
                        hmid_hbm, c_ref, n_ref, m_ref, conva_ref, s_ref, convb_ref,
                        xa_buf, xbc_buf, y_buf, merged, hout, sem, *, seq, n_prompt_rows):
    p = pl.program_id(0)
    c = pl.program_id(1)
    last_p = pl.num_programs(0) - 1
    last_c = pl.num_programs(1) - 1
    T = CHUNK
    RB = xp_ref.shape[0]
    d_a = H_A * DH_A
    d_b = H_B * HD_B
    conv_b = d_b + 2 * G_B * N_STATE

    def out_copy(r, row0):
        return pltpu.make_async_copy(hout.at[r], hmid_hbm.at[pl.ds(row0, T), :], sem.at[r])

    @pl.when(c == 0)
    def _init():
        c_ref[...] = jnp.zeros_like(c_ref)
        n_ref[...] = jnp.zeros_like(n_ref)
        m_ref[...] = jnp.zeros_like(m_ref)
        s_ref[...] = jnp.zeros_like(s_ref)
        xa_buf[...] = jnp.zeros_like(xa_buf)
        xbc_buf[...] = jnp.zeros_like(xbc_buf)

    @pl.when(jnp.logical_and(p == 0, c == 0))
    def _clear_sample_rows():
        hout[0] = jnp.zeros((T, hout.shape[2]), F32)
        cp = out_copy(0, n_prompt_rows)
        cp.start()
        cp.wait()

    row = lax.broadcasted_iota(jnp.int32, (T, 1), 0)
    valid = jnp.logical_or(c > 0, row >= T - N_META)
    xs_in = [jnp.where(c == 0, xmeta_ref[...], xp_ref[r]) for r in range(RB)]
    x2 = jnp.concatenate(xs_in, axis=0)
    hn = (x2 * _rms_scale(x2) * nmix_ref[...]).astype(BF16)

    lane = lax.broadcasted_iota(jnp.int32, (1, LANES), 1)
    lane_f = lane < L_DTA
    lane_dta = jnp.logical_and(lane >= L_DTA, lane < L_I)
    lane_i = jnp.logical_and(lane >= L_I, lane < L_DT)
    lane_dt = jnp.logical_and(lane >= L_DT, lane < L_DT + H_B)
    ri = lax.broadcasted_iota(jnp.int32, (T, T), 0)
    ci = lax.broadcasted_iota(jnp.int32, (T, T), 1)
    causal = ri >= ci
    tri = jnp.where(causal, 1.0, 0.0).astype(BF16)
    a_neg = jnp.where(lane_dta, -jnp.exp(alog_ref[...]), 0.0)
    left = lane < HD_B
    top = lax.broadcasted_iota(jnp.int32, (LANES, 1), 0) < HD_B

    off_small = 2 * d_a + d_b + conv_b
    pre2 = _dot(hn, wcat_ref[:, off_small:]) + bsm_ref[...]
    xa2 = _dot(hn, wcat_ref[:, 0:d_a])

    gcols, grows, xcs, xabs = [], [], [], []

    def gate_tables():
        for r in range(RB):
            pre = pre2[r * T:(r + 1) * T]
            sp, lsig = _softplus_parts(pre)
            to_cum = jnp.where(lane_f, lsig, jnp.where(lane_dta, sp * a_neg, 0.0))
            to_cum = jnp.where(valid, to_cum, 0.0)
            hi, mid, lo = _split3(to_cum)
            cum = _dot(tri, hi) + _dot(tri, mid) + _dot(tri, lo)
            extra = jnp.where(lane_i, jnp.where(valid, pre, NEG_INF),
                              jnp.where(lane_dt, jnp.where(valid, sp, 0.0), 0.0))
            gcol = cum + extra
            gcols.append(gcol)
            grows.append(gcol.T)

    gate_tables()
    for r in range(RB):
        rs = slice(r * T, (r + 1) * T)
        xa = xa2[rs]
        xc = _causal_conv(xa, xa_buf[r], cwa_ref, cba_ref)
        xa_buf[r] = xa[T - CONV_HDR:T, :]
        conva_ref[r] = xa[T - 3:T, :]
        xcs.append(_silu(xc).astype(BF16))
        xabs.append(xa.astype(BF16))

    items = [(r, h) for h in range(H_A) for r in range(RB)]
    hsl = lambda h: slice(h * DH_A, (h + 1) * DH_A)
    m_alls = [m_ref[r] for r in range(RB)]
    m_news = list(m_alls)
    qs, ks, vs, qks, st, dd = {}, {}, {}, {}, {}, {}

    def stage_qkv(it):
        r, h = it
        qs[it] = _dot(xcs[r][:, hsl(h)], wq_ref[h]).astype(BF16)
        ks[it] = _dot(xcs[r][:, hsl(h)], wk_ref[h]) * (DH_A ** -0.5)
        vs[it] = _dot(xabs[r][:, hsl(h)], wv_ref[h]).astype(BF16)

    def stage_qk(it):
        qks[it] = _dot_nt(qs[it], ks[it].astype(BF16))

    pairs_per_group = H_B // G_B // 2
    groups = [(r, g) for g in range(G_B) for r in range(RB)]
    proj, xbcs, bgs, cgs, cbs = {}, [], {}, {}, {}

    def project(name, lo, hi):
        proj[name] = _dot(hn, wcat_ref[:, lo:hi])

    def ssd_inputs():
        for r in range(RB):
            xbc = proj["xbc"][r * T:(r + 1) * T]
            xbc_c = _causal_conv(xbc, xbc_buf[r], cwb_ref, cbb_ref)
            xbc_buf[r] = xbc[T - CONV_HDR:T, :]
            convb_ref[r] = xbc[T - 3:T, :]
            xbcs.append(_silu(xbc_c))
        for r, g in groups:
            bgs[(r, g)] = xbcs[r][:, d_b + g * N_STATE:d_b + (g + 1) * N_STATE].astype(BF16)
            cgs[(r, g)] = xbcs[r][:, d_b + (G_B + g) * N_STATE:d_b + (G_B + g + 1) * N_STATE].astype(BF16)
            cbs[(r, g)] = _dot_nt(cgs[(r, g)], bgs[(r, g)])

    def stage_weights(it):
        r, h = it
        gcol, grow = gcols[r], grows[r]
        b_col = gcol[:, L_F + h:L_F + h + 1]
        i_col = gcol[:, L_I + h:L_I + h + 1]
        b_row = grow[L_F + h:L_F + h + 1, :]
        i_row = grow[L_I + h:L_I + h + 1, :]
        m0 = m_alls[r][:, h:h + 1]
        dmat = jnp.where(causal, b_col - (b_row - i_row), NEG_INF)
        m_inter = b_col + m0
        m = jnp.maximum(m_inter, jnp.max(dmat, axis=-1, keepdims=True))
        w_inter = jnp.exp(m_inter - m)
        s = qks[it] * jnp.exp(dmat - m)
        n0 = n_ref[r, h:h + 1, :]
        den = (jnp.sum(s, axis=-1, keepdims=True)
               + w_inter * jnp.sum(qs[it].astype(F32) * n0, axis=-1, keepdims=True))
        m_last = m[T - 1:T, :]
        b_last = b_col[T - 1:T, :]
        dec = jnp.exp(b_last + m0 - m_last)
        kw = ks[it] * jnp.exp(b_last - b_col + i_col - m_last)
        n_ref[r, h:h + 1, :] = dec * n0 + jnp.sum(kw, axis=0, keepdims=True)
        m_news[r] = jnp.where(lane == h, m_last, m_news[r])
        st[it] = (s.astype(BF16), kw.astype(BF16), w_inter,
                  jnp.maximum(jnp.abs(den), jnp.exp(-m)), dec)

    def stage_readout(it):
        r, h = it
        s_b, kw_b, w_inter, den, dec = st[it]
        c0 = c_ref[r, h]
        num = _dot(s_b, vs[it]) + w_inter * _dot(qs[it], c0.astype(BF16))
        c_ref[r, h] = dec * c0 + _dot_tn(kw_b, vs[it])
        dd[it] = num / den

    def stage_head_out(it):
        r, h = it
        hh = dd[it]
        hh = hh * _rms_scale(hh) * na_ref[:, hsl(h)]
        merged[r * T:(r + 1) * T, hsl(h)] = (
            hh * jax.nn.sigmoid(proj["za"][r * T:(r + 1) * T, hsl(h)])).astype(BF16)

    pairs = [(r, pi) for pi in range(H_B // 2) for r in range(RB)]
    psl = lambda pi: slice(pi * LANES, (pi + 1) * LANES)
    sw = {}

    def stage_decay(pr):
        r, pi = pr
        g = pi // pairs_per_group
        gcol, grow = gcols[r], grows[r]
        xpair = xbcs[r][:, psl(pi)]
        scs, a_cols, w_cols, a_lasts = [], [], [], []
        for j in (2 * pi, 2 * pi + 1):
            a_col = gcol[:, L_DTA + j:L_DTA + j + 1]
            a_row = grow[L_DTA + j:L_DTA + j + 1, :]
            dt_col = gcol[:, L_DT + j:L_DT + j + 1]
            dt_row = grow[L_DT + j:L_DT + j + 1, :]
            decay = jnp.exp(jnp.where(causal, a_col - a_row, NEG_INF))
            scs.append((cbs[(r, g)] * decay * dt_row).astype(BF16))
            a_last = a_col[T - 1:T, :]
            a_cols.append(a_col)
            a_lasts.append(a_last)
            w_cols.append(jnp.exp(a_last - a_col) * dt_col)
        sw[pr] = (scs, xpair.astype(BF16),
                  (xpair * jnp.where(left, w_cols[0], w_cols[1])).astype(BF16),
                  jnp.exp(jnp.where(left, a_cols[0], a_cols[1])),
                  jnp.exp(jnp.where(top, a_lasts[0], a_lasts[1])))
    def stage_pair_out(pr):
        r, pi = pr
        g = pi // pairs_per_group
        scs, xpb, xw, ea, ea_last = sw[pr]
        s0 = s_ref[r, pi]
        y = jnp.where(left, _dot(scs[0], xpb), _dot(scs[1], xpb))
        y = y + ea * _dot_nt(cgs[(r, g)], s0.astype(BF16))
        s_ref[r, pi] = ea_last * s0 + _dot_tn(xw, bgs[(r, g)])
        y = y + dsk_ref[:, psl(pi)] * xbcs[r][:, psl(pi)]
        y_buf[r, :, psl(pi)] = y * _silu(proj["zb"][r * T:(r + 1) * T, psl(pi)])

    def each(stage, seq):
        for e in seq:
            stage(e)

    each(stage_qkv, items)
    project("xbc", 2 * d_a + d_b, off_small)
    each(stage_qk, items)
    project("za", d_a, 2 * d_a)
    ssd_inputs()
    each(stage_weights, items)
    each(stage_readout, items)
    project("zb", 2 * d_a, 2 * d_a + d_b)
    for r in range(RB):
        m_ref[r] = m_news[r]
    each(stage_head_out, items)
    each(stage_decay, pairs)
    each(stage_pair_out, pairs)
    gw = d_b // G_B
    for r in range(RB):
        for g in range(G_B):
            yg = y_buf[r, :, g * gw:(g + 1) * gw]
            merged[r * T:(r + 1) * T, d_a + g * gw:d_a + (g + 1) * gw] = (
                yg * _rms_scale(yg) * nb_ref[:, g * gw:(g + 1) * gw]).astype(BF16)

    @pl.when(c > 0)
    def _out():
        out2 = x2 + _dot(merged[...], wout_ref[...])

        @pl.when(jnp.logical_or(c > 1, p > 0))
        def _wait_previous():
            for r in range(RB):
                out_copy(r, 0).wait()

        for r in range(RB):
            hout[r] = out2[r * T:(r + 1) * T]
            out_copy(r, (p * RB + r) * seq + (c - 1) * T).start()

        @pl.when(jnp.logical_and(p == last_p, c == last_c))
        def _drain():
            for r in range(RB):
                out_copy(r, 0).wait()


def _const_spec(shape):
    nd = len(shape)
    return pl.BlockSpec(shape, lambda b, c, _nd=nd: (0,) * _nd)


def _prompt_mixer(x_prompt, xmeta, p, n_extra_rows):
    bsz, seq, d = x_prompt.shape
    assert n_extra_rows == CHUNK and seq % CHUNK == 0
    n_chunks = seq // CHUNK + 1
    cps = seq // CHUNK
    d_a = H_A * DH_A
    conv_b = H_B * HD_B + 2 * G_B * N_STATE
    consts = [p["nmix"], p["wcat"], p["bsm"], p["alog"], p["cwa"], p["cba"], p["cwb"], p["cbb"],
              p["wq"], p["wk"], p["wv"], p["na"], p["nb"], p["dsk"], p["wout"]]
    rb = PROMPT_ROWS
    assert bsz % rb == 0
    in_specs = [_const_spec(xmeta.shape),
                pl.BlockSpec((rb, CHUNK, d), lambda b, c: (b, jnp.maximum(c - 1, 0), 0))]
    in_specs += [_const_spec(a.shape) for a in consts]
    out_shape = (
        jax.ShapeDtypeStruct((bsz * seq + n_extra_rows, d), F32),
        jax.ShapeDtypeStruct((bsz, H_A, DH_A, DH_A), F32),
        jax.ShapeDtypeStruct((bsz, H_A, DH_A), F32),
        jax.ShapeDtypeStruct((bsz, 1, LANES), F32),
        jax.ShapeDtypeStruct((bsz, CONV_W - 1, d_a), F32),
        jax.ShapeDtypeStruct((bsz, H_B // 2, 2 * HD_B, N_STATE), F32),
        jax.ShapeDtypeStruct((bsz, CONV_W - 1, conv_b), F32),
    )
    out_specs = (
        pl.BlockSpec(memory_space=pl.ANY),
        pl.BlockSpec((rb, H_A, DH_A, DH_A), lambda b, c: (b, 0, 0, 0)),
        pl.BlockSpec((rb, H_A, DH_A), lambda b, c: (b, 0, 0)),
        pl.BlockSpec((rb, 1, LANES), lambda b, c: (b, 0, 0)),
        pl.BlockSpec((rb, CONV_W - 1, d_a), lambda b, c: (b, 0, 0)),
        pl.BlockSpec((rb, H_B // 2, 2 * HD_B, N_STATE), lambda b, c: (b, 0, 0, 0)),
        pl.BlockSpec((rb, CONV_W - 1, conv_b), lambda b, c: (b, 0, 0)),
    )
    return pl.pallas_call(
        functools.partial(_prompt_rows_kernel, seq=seq, n_prompt_rows=bsz * seq),
        out_shape=out_shape,
        grid=(bsz // rb, n_chunks),
        in_specs=in_specs,
        out_specs=out_specs,
        scratch_shapes=[
            pltpu.VMEM((rb, CONV_HDR, d_a), F32),
            pltpu.VMEM((rb, CONV_HDR, conv_b), F32),
            pltpu.VMEM((rb, CHUNK, H_B * HD_B), F32),
            pltpu.VMEM((rb * CHUNK, d_a + H_B * HD_B), BF16),
            pltpu.VMEM((rb, CHUNK, d), F32),
            pltpu.SemaphoreType.DMA((rb,)),
        ],
        compiler_params=pltpu.CompilerParams(
            dimension_semantics=("arbitrary", "arbitrary"), vmem_limit_bytes=VMEM_LIMIT),
        name="prompt_mixer",
    )(xmeta, x_prompt, *consts)


def _regroup_w_in_kernel(w_ref, o_ref):
    d_a = H_A * DH_A
    d_b = H_B * HD_B
    conv_b = d_b + 2 * G_B * N_STATE
    o_i = 2 * d_a
    o_f = o_i + H_A
    o_zb = o_f + H_A
    o_xbc = o_zb + d_b
    o_dt = o_xbc + conv_b
    rows = w_ref.shape[0]
    o_ref[:, 0:2 * d_a] = w_ref[:, 0:2 * d_a].astype(BF16)
    o_ref[:, 2 * d_a:2 * d_a + d_b] = w_ref[:, o_zb:o_zb + d_b].astype(BF16)
    o_ref[:, 2 * d_a + d_b:2 * d_a + d_b + conv_b] = w_ref[:, o_xbc:o_xbc + conv_b].astype(BF16)
    small = jnp.concatenate(
        [w_ref[:, o_f:o_f + H_A], w_ref[:, o_dt:o_dt + H_B], w_ref[:, o_i:o_i + H_A],
         w_ref[:, o_dt:o_dt + H_B], jnp.zeros((rows, LANES - (L_DT + H_B)), F32)], axis=1)
    o_ref[:, 2 * d_a + d_b + conv_b:] = small.astype(BF16)


def _prep_mixer_params(norm_mix, w_in, conv_a_w, conv_a_b, w_q, w_k, w_v, b_i, b_f, norm_a,
                       conv_b_w, conv_b_b, dt_bias, a_log, d_skip, norm_b, w_out):
    d_a = H_A * DH_A
    d_b = H_B * HD_B
    conv_b = d_b + 2 * G_B * N_STATE
    d_model, d_in = w_in.shape[1], w_in.shape[2]
    n_cols = 2 * d_a + d_b + conv_b + LANES
    rows = 256
    assert w_in.shape[0] == 1 and d_model % rows == 0
    wcat = pl.pallas_call(
        _regroup_w_in_kernel,
        out_shape=jax.ShapeDtypeStruct((d_model, n_cols), BF16),
        grid=(d_model // rows,),
        in_specs=[pl.BlockSpec((None, rows, d_in), lambda i: (0, i, 0))],
        out_specs=pl.BlockSpec((rows, n_cols), lambda i: (i, 0)),
        compiler_params=pltpu.CompilerParams(
            dimension_semantics=("arbitrary",), vmem_limit_bytes=VMEM_LIMIT),
        name="regroup_w_in",
    )(w_in.astype(F32))

    def lanes(parts):
        pieces, at = [], 0
        for off, a in parts:
            pieces += [jnp.zeros((1, off - at), F32), a.astype(F32)]
            at = off + a.shape[1]
        return jnp.concatenate(pieces + [jnp.zeros((1, LANES - at), F32)], axis=1)

    return dict(
        nmix=norm_mix.reshape(1, -1).astype(F32),
        wcat=wcat,
        bsm=lanes([(L_F, b_f), (L_DTA, dt_bias), (L_I, b_i), (L_DT, dt_bias)]),
        alog=lanes([(L_DTA, a_log)]),
        cwa=conv_a_w.reshape(CONV_W, d_a).astype(F32), cba=conv_a_b.reshape(1, d_a).astype(F32),
        cwb=conv_b_w.reshape(CONV_W, conv_b).astype(F32), cbb=conv_b_b.reshape(1, conv_b).astype(F32),
        wq=w_q.reshape(H_A, DH_A, DH_A).astype(BF16), wk=w_k.reshape(H_A, DH_A, DH_A).astype(BF16),
        wv=w_v.reshape(H_A, DH_A, DH_A).astype(BF16),
        na=norm_a.reshape(1, d_a).astype(F32), nb=norm_b.reshape(1, d_b).astype(F32),
        dsk=jnp.repeat(d_skip.reshape(H_B).astype(F32), HD_B)[None, :],
        wout=w_out.reshape(d_a + d_b, -1).astype(BF16),
    )


SAMPLE_BLOCK = 8


def _expand_lanes(vals, first_lane, n_heads, width):
    r = lax.broadcasted_iota(jnp.int32, (LANES, n_heads * width), 0) - first_lane
    c = lax.broadcasted_iota(jnp.int32, (LANES, n_heads * width), 1)
    sel = jnp.logical_and(c >= r * width, c < (r + 1) * width)
    e = jnp.where(sel, 1.0, 0.0).astype(BF16)
    hi, mid, lo = _split3(vals)
    return (_dot(hi, e) + _dot(mid, e)) + _dot(lo, e)


def _sample_pre_kernel(x_ref, nmix_ref, wcat_ref, bsm_ref, alog_ref, cwa_ref, cba_ref, cwb_ref, cbb_ref,
                       wq_ref, wk_ref, wv_ref, dsk_ref, conva_ref, convb_ref, n0_ref, m0_ref,
                       conva_out, convb_out, n1_out, m1_out, g_out, qt_out, kwt_out, xwt_out,
                       v_out, bc_out, a1_out, w1_out, den_out, y1_out, ea_out, zbs_out, zas_out):
    d_a = H_A * DH_A
    d_b = H_B * HD_B
    conv_b = d_b + 2 * G_B * N_STATE
    shift_i = LANES - (L_I - L_F)
    x = x_ref[...]
    hn = (x * _rms_scale(x) * nmix_ref[...]).astype(BF16)
    lane = lax.broadcasted_iota(jnp.int32, (1, LANES), 1)
    lane_f = lane < L_DTA
    lane_dta = jnp.logical_and(lane >= L_DTA, lane < L_I)
    pre = _dot(hn, wcat_ref[:, 2 * d_a + d_b + conv_b:]) + bsm_ref[...]
    sp, lsig = _softplus_parts(pre)
    a_neg = jnp.where(lane_dta, -jnp.exp(alog_ref[...]), 0.0)
    pre_al = pltpu.roll(pre, shift_i, axis=1)
    sp_al = pltpu.roll(sp, shift_i, axis=1)
    m_inter = lsig + m0_ref[...]
    m = jnp.maximum(m_inter, pre_al)
    w_inter = jnp.exp(m_inter - m)
    sfac = jnp.exp(pre_al - m)
    ea = jnp.exp(sp * a_neg)
    dt = sp_al

    xa = _dot(hn, wcat_ref[:, 0:d_a])
    xc = (cwa_ref[0:1, :] * conva_ref[:, 0:d_a] + cwa_ref[1:2, :] * conva_ref[:, d_a:2 * d_a]
          + cwa_ref[2:3, :] * conva_ref[:, 2 * d_a:3 * d_a] + cwa_ref[3:4, :] * xa + cba_ref[...])
    conva_out[:, 0:2 * d_a] = conva_ref[:, d_a:3 * d_a]
    conva_out[:, 2 * d_a:3 * d_a] = xa
    xc = _silu(xc).astype(BF16)
    xab = xa.astype(BF16)
    sf_e = _expand_lanes(sfac, L_F, H_A, DH_A)
    w_e = _expand_lanes(w_inter, L_F, H_A, DH_A)
    qk8 = jnp.zeros((x.shape[0], LANES), F32)
    qn8 = jnp.zeros((x.shape[0], LANES), F32)
    for h in range(H_A):
        sl = slice(h * DH_A, (h + 1) * DH_A)
        q = _dot(xc[:, sl], wq_ref[h])
        k = _dot(xc[:, sl], wk_ref[h]) * (DH_A ** -0.5)
        v = _dot(xab[:, sl], wv_ref[h])
        kw = k * sf_e[:, sl]
        qk8 = jnp.where(lane == h, jnp.sum(q * k, axis=-1, keepdims=True), qk8)
        qn8 = jnp.where(lane == h, jnp.sum(q * n0_ref[:, sl], axis=-1, keepdims=True), qn8)
        n1_out[:, sl] = w_e[:, sl] * n0_ref[:, sl] + kw
        v_out[:, sl] = v
        qt_out[h] = q.T
        kwt_out[h] = kw.T
    s8 = qk8 * sfac
    a1_out[...] = _expand_lanes(s8, L_F, H_A, DH_A) * v_out[...]
    w1_out[...] = w_e
    den_out[...] = jnp.maximum(jnp.abs(_expand_lanes(s8 + w_inter * qn8, L_F, H_A, DH_A)),
                               jnp.exp(-_expand_lanes(m, L_F, H_A, DH_A)))
    m1_out[...] = m
    g_out[...] = jnp.where(lane_f, w_inter, jnp.where(lane_dta, ea, 0.0))
    zas_out[...] = jax.nn.sigmoid(_dot(hn, wcat_ref[:, d_a:2 * d_a]))

    off_xbc = 2 * d_a + d_b
    xbc = _dot(hn, wcat_ref[:, off_xbc:off_xbc + conv_b])
    xbc_c = (cwb_ref[0:1, :] * convb_ref[:, 0:conv_b] + cwb_ref[1:2, :] * convb_ref[:, conv_b:2 * conv_b]
             + cwb_ref[2:3, :] * convb_ref[:, 2 * conv_b:3 * conv_b] + cwb_ref[3:4, :] * xbc + cbb_ref[...])
    convb_out[:, 0:2 * conv_b] = convb_ref[:, conv_b:3 * conv_b]
    convb_out[:, 2 * conv_b:3 * conv_b] = xbc
    xbc_c = _silu(xbc_c)
    xs = xbc_c[:, 0:d_b]
    bc = xbc_c[:, d_b:conv_b]
    bc_out[...] = bc
    heads_per_group = H_B // G_B
    cbl = jnp.zeros((x.shape[0], LANES), F32)
    for g in range(G_B):
        cb_g = jnp.sum(bc[:, g * N_STATE:(g + 1) * N_STATE]
                       * bc[:, (G_B + g) * N_STATE:(G_B + g + 1) * N_STATE], axis=-1, keepdims=True)
        in_g = jnp.logical_and(lane >= L_DTA + g * heads_per_group,
                               lane < L_DTA + (g + 1) * heads_per_group)
        cbl = jnp.where(in_g, cb_g, cbl)
    dt_e = _expand_lanes(dt, L_DTA, H_B, HD_B)
    y1_out[...] = _expand_lanes(cbl * dt, L_DTA, H_B, HD_B) * xs + dsk_ref[...] * xs
    ea_out[...] = _expand_lanes(ea, L_DTA, H_B, HD_B)
    zbs_out[...] = _silu(_dot(hn, wcat_ref[:, 2 * d_a:2 * d_a + d_b]))
    xw = xs * dt_e
    for pi in range(H_B // 2):
        xwt_out[pi] = xw[:, pi * LANES:(pi + 1) * LANES].T


def _sample_state_kernel(g_ref, c0_ref, s0_ref, qt_ref, kwt_ref, xwt_ref, v_ref, bc_ref,
                         c1_ref, s1_ref, qc_ref, ysi_ref):
    i = pl.program_id(0)
    bb = c0_ref.shape[0]
    shift = lax.rem(LANES - lax.rem(i * bb, LANES), LANES)
    lane = lax.broadcasted_iota(jnp.int32, (1, LANES), 1)
    top = lax.broadcasted_iota(jnp.int32, (LANES, 1), 0) < HD_B
    heads_per_group = H_B // G_B
    for h in range(H_A):
        sl = slice(h * DH_A, (h + 1) * DH_A)
        qt = pltpu.roll(qt_ref[h], shift, axis=1)
        kwt = pltpu.roll(kwt_ref[h], shift, axis=1)
        for r in range(bb):
            b = i * bb + r
            c0 = c0_ref[r, h]
            dec = g_ref[b, L_F + h]
            v_row = v_ref[r:r + 1, sl]
            qc_ref[r:r + 1, sl] = jnp.sum(c0 * qt[:, r:r + 1], axis=0, keepdims=True)
            c1_ref[r, h] = dec * c0 + kwt[:, r:r + 1] * v_row
    for pi in range(H_B // 2):
        g = (2 * pi) // heads_per_group
        sl = slice(pi * LANES, (pi + 1) * LANES)
        xwt = pltpu.roll(xwt_ref[pi], shift, axis=1)
        acc = jnp.zeros((LANES, LANES), F32)
        for r in range(bb):
            b = i * bb + r
            s0 = s0_ref[r, pi]
            b_row = bc_ref[r:r + 1, g * N_STATE:(g + 1) * N_STATE]
            c_row = bc_ref[r:r + 1, (G_B + g) * N_STATE:(G_B + g + 1) * N_STATE]
            col = jnp.sum(s0 * c_row, axis=-1, keepdims=True)
            acc = jnp.where(lane == r, col, acc)
            ea_rows = jnp.where(top, g_ref[b, L_DTA + 2 * pi], g_ref[b, L_DTA + 2 * pi + 1])
            s1_ref[r, pi] = ea_rows * s0 + xwt[:, r:r + 1] * b_row
        ysi_ref[:, sl] = acc.T[0:bb, :]


def _sample_post_kernel(a1_ref, w1_ref, den_ref, y1_ref, ea_ref, zbs_ref, zas_ref, x_ref, qc_ref, ysi_ref,
                        na_ref, nb_ref, wout_ref, hall_ref, hmid_ref, merged):
    del hall_ref
    d_a = H_A * DH_A
    d_b = H_B * HD_B
    hh = (a1_ref[...] + w1_ref[...] * qc_ref[...]) / den_ref[...]
    for h in range(H_A):
        sl = slice(h * DH_A, (h + 1) * DH_A)
        hs = hh[:, sl]
        merged[:, sl] = (hs * _rms_scale(hs) * na_ref[:, sl] * zas_ref[:, sl]).astype(BF16)
    y = (y1_ref[...] + ea_ref[...] * ysi_ref[...]) * zbs_ref[...]
    gw = d_b // G_B
    for g in range(G_B):
        yg = y[:, g * gw:(g + 1) * gw]
        merged[:, d_a + g * gw:d_a + (g + 1) * gw] = (
            yg * _rms_scale(yg) * nb_ref[:, g * gw:(g + 1) * gw]).astype(BF16)
    hmid_ref[...] = x_ref[...] + _dot(merged[...], wout_ref[...])


def _vmem_specs(arrays):
    return [pl.BlockSpec(a.shape, lambda *_, _nd=a.ndim: (0,) * _nd) for a in arrays]


def _sample_mixer(x, c0, n0, m0, conva, s0, convb, p, hmid_all, row_offset):
    nb, d = x.shape
    d_a = H_A * DH_A
    d_b = H_B * HD_B
    conv_b = d_b + 2 * G_B * N_STATE
    row = lambda w: jax.ShapeDtypeStruct((nb, w), F32)
    tile = lambda k: jax.ShapeDtypeStruct((k, LANES, nb), F32)
    pre_in = [x, p["nmix"], p["wcat"], p["bsm"], p["alog"], p["cwa"], p["cba"], p["cwb"], p["cbb"],
              p["wq"], p["wk"], p["wv"], p["dsk"], conva, convb, n0, m0]
    pre_out_shape = (row(3 * d_a), row(3 * conv_b), row(d_a), row(LANES), row(LANES),
                     tile(H_A), tile(H_A), tile(H_B // 2), row(d_a), row(2 * G_B * N_STATE),
                     row(d_a), row(d_a), row(d_a), row(d_b), row(d_b), row(d_b), row(d_a))
    (conva1, convb1, n1, m1, g8, qt, kwt, xwt, v, bc, a1, w1, den, y1, ea_e, zbs, zas) = pl.pallas_call(
        _sample_pre_kernel,
        out_shape=pre_out_shape,
        grid=(1,),
        in_specs=_vmem_specs(pre_in),
        out_specs=tuple(pl.BlockSpec(s.shape, lambda i, _nd=len(s.shape): (0,) * _nd) for s in pre_out_shape),
        compiler_params=pltpu.CompilerParams(
            dimension_semantics=("arbitrary",), vmem_limit_bytes=VMEM_LIMIT),
        name="sample_pre",
    )(*pre_in)

    bb = SAMPLE_BLOCK
    const3 = lambda k: pl.BlockSpec((k, LANES, nb), lambda i, g: (0, 0, 0))
    state_grid = pltpu.PrefetchScalarGridSpec(
        num_scalar_prefetch=1,
        grid=(nb // bb,),
        in_specs=[pl.BlockSpec((bb, H_A, DH_A, DH_A), lambda i, g: (i, 0, 0, 0)),
                  pl.BlockSpec((bb, H_B // 2, 2 * HD_B, N_STATE), lambda i, g: (i, 0, 0, 0)),
                  const3(H_A), const3(H_A), const3(H_B // 2),
                  pl.BlockSpec((bb, d_a), lambda i, g: (i, 0)),
                  pl.BlockSpec((bb, 2 * G_B * N_STATE), lambda i, g: (i, 0))],
        out_specs=(pl.BlockSpec((bb, H_A, DH_A, DH_A), lambda i, g: (i, 0, 0, 0)),
                   pl.BlockSpec((bb, H_B // 2, 2 * HD_B, N_STATE), lambda i, g: (i, 0, 0, 0)),
                   pl.BlockSpec((bb, d_a), lambda i, g: (i, 0)),
                   pl.BlockSpec((bb, d_b), lambda i, g: (i, 0))),
    )
    c1, s1, qc, ysi = pl.pallas_call(
        _sample_state_kernel,
        out_shape=(jax.ShapeDtypeStruct(c0.shape, F32), jax.ShapeDtypeStruct(s0.shape, F32),
                   row(d_a), row(d_b)),
        grid_spec=state_grid,
        compiler_params=pltpu.CompilerParams(
            dimension_semantics=("arbitrary",), vmem_limit_bytes=VMEM_LIMIT),
        name="sample_state",
    )(g8, c0, s0, qt, kwt, xwt, v, bc)

    post_in = [a1, w1, den, y1, ea_e, zbs, zas, x, qc, ysi, p["na"], p["nb"], p["wout"]]
    hmid_all = pl.pallas_call(
        _sample_post_kernel,
        out_shape=jax.ShapeDtypeStruct(hmid_all.shape, F32),
        grid=(1,),
        in_specs=_vmem_specs(post_in) + [pl.BlockSpec(memory_space=pl.ANY)],
        out_specs=pl.BlockSpec((nb, d), lambda i: (row_offset // nb, 0)),
        scratch_shapes=[pltpu.VMEM((nb, d_a + d_b), BF16)],
        input_output_aliases={len(post_in): 0},
        compiler_params=pltpu.CompilerParams(
            dimension_semantics=("arbitrary",), vmem_limit_bytes=VMEM_LIMIT),
        name="sample_post",
    )(*post_in, hmid_all)
    return hmid_all, c1, n1, m1, conva1, s1, convb1


R_EA, R_EB, R_RA, R_RB, R_GA, R_GB = 0, 1, 2, 3, 4, 5
RL_E = N_EGROUPS


def _router_kernel(h_ref, nf_ref, whi_ref, wmid_ref, br_ref, xn_ref, info_ref, cnt_ref, carry):
    i = pl.program_id(0)
    tr = h_ref.shape[0]

    @pl.when(i == 0)
    def _init():
        carry[...] = jnp.zeros_like(carry)

    h = h_ref[...]
    xn = h * _rms_scale(h) * nf_ref[...]
    _store_token_tiles(xn_ref, xn)
    x_hi, x_mid, _ = _split3(xn)
    logits = (_dot(x_hi, whi_ref[...]) + _dot(x_hi, wmid_ref[...]) + _dot(x_mid, whi_ref[...])
              + br_ref[...])
    lane_i = lax.broadcasted_iota(jnp.int32, (1, LANES), 1)
    lane = lane_i.astype(F32)
    big = float(LANES)

    def first_lane_of(cond):
        return jnp.min(jnp.where(cond, lane, big), axis=-1, keepdims=True)

    l1 = jnp.where(lane_i < N_EGROUPS, logits, NEG_INF)
    e1 = jnp.exp(l1 - jnp.max(l1, axis=-1, keepdims=True))
    p1 = e1 / jnp.sum(e1, axis=-1, keepdims=True)
    gp = jnp.max(p1, axis=-1, keepdims=True)
    gidx = first_lane_of(p1 == gp)
    lo = RL_E + N_EPG * gidx
    l2 = jnp.where(jnp.logical_and(lane >= lo, lane < lo + N_EPG), logits, NEG_INF)
    va = jnp.max(l2, axis=-1, keepdims=True)
    ia = first_lane_of(l2 == va)
    l2b = jnp.where(lane == ia, NEG_INF, l2)
    vb = jnp.max(l2b, axis=-1, keepdims=True)
    ib = first_lane_of(l2b == vb)
    eb = jnp.exp(vb - va)
    wa = 1.0 / (1.0 + eb)
    wb = eb / (1.0 + eb)

    is_a = lane == ia
    is_b = lane == ib
    onehot = jnp.where(jnp.logical_or(is_a, is_b), 1.0, 0.0)
    ri = lax.broadcasted_iota(jnp.int32, (tr, tr), 0)
    ci = lax.broadcasted_iota(jnp.int32, (tr, tr), 1)
    tri = jnp.where(ri >= ci, 1.0, 0.0).astype(BF16)
    incl = _dot(tri, onehot.astype(BF16))
    excl = incl - onehot + carry[...]
    rank_a = jnp.sum(jnp.where(is_a, excl, 0.0), axis=-1, keepdims=True)
    rank_b = jnp.sum(jnp.where(is_b, excl, 0.0), axis=-1, keepdims=True)
    carry[...] = carry[...] + incl[tr - 1:tr, :]
    cnt_ref[...] = carry[...]

    info = jnp.where(lane_i == R_EA, ia - RL_E, 0.0)
    info = jnp.where(lane_i == R_EB, ib - RL_E, info)
    info = jnp.where(lane_i == R_RA, rank_a, info)
    info = jnp.where(lane_i == R_RB, rank_b, info)
    info = jnp.where(lane_i == R_GA, gp * wa, info)
    info = jnp.where(lane_i == R_GB, gp * wb, info)
    info_ref[...] = info


def _row_tile(n, candidates):
    for t in candidates:
        if n % t == 0:
            return t
    raise ValueError(f"no row tile for {n} rows among {candidates}")


def _router(hmid, rp):
    n, d = hmid.shape
    assert d == TOK_TILE_ROWS * LANES
    tr = _row_tile(n, (512, 384, 256, 128))
    return pl.pallas_call(
        _router_kernel,
        out_shape=(jax.ShapeDtypeStruct((n * TOK_TILE_ROWS, LANES), F32),
                   jax.ShapeDtypeStruct((n, LANES), F32),
                   jax.ShapeDtypeStruct((1, LANES), F32)),
        grid=(n // tr,),
        in_specs=[pl.BlockSpec((tr, d), lambda i: (i, 0)),
                  pl.BlockSpec((1, d), lambda i: (0, 0)),
                  pl.BlockSpec((d, LANES), lambda i: (0, 0)),
                  pl.BlockSpec((d, LANES), lambda i: (0, 0)),
                  pl.BlockSpec((1, LANES), lambda i: (0, 0))],
        out_specs=(pl.BlockSpec((tr * TOK_TILE_ROWS, LANES), lambda i: (i, 0)),
                   pl.BlockSpec((tr, LANES), lambda i: (i, 0)),
                   pl.BlockSpec((1, LANES), lambda i: (0, 0))),
        scratch_shapes=[pltpu.VMEM((1, LANES), F32)],
        compiler_params=pltpu.CompilerParams(
            dimension_semantics=("arbitrary",), vmem_limit_bytes=VMEM_LIMIT),
        name="router",
    )(hmid, rp["nf"], rp["whi"], rp["wmid"], rp["br"])


def _prep_router_params(norm_ffn, w_r1, b_r1, w_r2, b_r2):
    d = w_r1.shape[1]
    w = jnp.concatenate([w_r1.reshape(d, N_EGROUPS).astype(F32), w_r2.reshape(d, N_EXPERTS).astype(F32),
                         jnp.zeros((d, LANES - RL_E - N_EXPERTS), F32)], axis=1)
    whi = w.astype(BF16)
    wmid = (w - whi.astype(F32)).astype(BF16)
    br = jnp.concatenate([b_r1.reshape(1, N_EGROUPS).astype(F32), b_r2.reshape(1, N_EXPERTS).astype(F32),
                          jnp.zeros((1, LANES - RL_E - N_EXPERTS), F32)], axis=1)
    return dict(nf=norm_ffn.reshape(1, d).astype(F32), whi=whi, wmid=wmid, br=br)


FFN_TM = 256


def _start_tile_gather(first_row_of, n_rows, src_hbm, dst, sem, priority_of):
    for r in range(n_rows):
        start = pl.multiple_of(first_row_of(r), TOK_TILE_ROWS)
        pltpu.make_async_copy(src_hbm.at[pl.ds(start, TOK_TILE_ROWS), :],
                              dst.at[pl.ds(r * TOK_TILE_ROWS, TOK_TILE_ROWS), :],
                              sem).start(priority=priority_of(r))


def _wait_tile_gather(n_rows, src_hbm, dst, sem):
    pltpu.make_async_copy(src_hbm.at[pl.ds(0, n_rows * TOK_TILE_ROWS), :], dst, sem).wait()


TAB_OFF, TAB_CNT, TAB_TILE_EXPERT, TAB_NVALID = 0, 1, 2, 3
TAB_LANES = 2 * LANES


def _ffn_kernel(tab_ref, pa_ref, pb_ref, xn_hbm, wg_ref, wu_ref, wd_ref, ys_ref,
                src, xbuf, wbf, sem, *, n_tokens, tm):
    i = pl.program_id(0)
    n_valid = tab_ref[TAB_NVALID, 0]
    slot = lax.rem(i, 2)
    gather_priority = lambda r: 1

    @pl.when(i == 0)
    def _build_source_rows():
        for e in range(N_EXPERTS):
            cnt_e = tab_ref[TAB_CNT, RL_E + e]
            first = tab_ref[TAB_OFF, RL_E + e] + cnt_e
            n_pad = lax.rem(tm - lax.rem(cnt_e, tm), tm)

            def pad_body(r, carry, first=first):
                src[first + r] = 0
                return carry
            lax.fori_loop(0, n_pad, pad_body, 0)

        def body(t, carry):
            first_row = t * TOK_TILE_ROWS
            src[pa_ref[t]] = first_row
            src[pb_ref[t]] = first_row
            return carry
        lax.fori_loop(0, n_tokens, body, 0, unroll=8)
        _start_tile_gather(lambda r: src[r], tm, xn_hbm, xbuf.at[0], sem.at[0], gather_priority)

    changed = jnp.logical_or(i == 0, tab_ref[TAB_TILE_EXPERT, i]
                             != tab_ref[TAB_TILE_EXPERT, jnp.maximum(i - 1, 0)])

    @pl.when(jnp.logical_and(changed, i < n_valid))
    def _cast_weights():
        wbf[0] = wg_ref[...].astype(BF16)
        wbf[1] = wu_ref[...].astype(BF16)
        wbf[2] = wd_ref[...].astype(BF16)

    @pl.when(i < n_valid)
    def _compute():
        _wait_tile_gather(tm, xn_hbm, xbuf.at[slot], sem.at[slot])
        x = _load_token_tiles(xbuf.at[slot], tm).astype(BF16)
        hg = _dot(x, wbf[0])
        hu = _dot(x, wbf[1])
        y = _dot((_silu(hg) * hu).astype(BF16), wbf[2])
        base = jnp.where(i + 1 < n_valid, i + 1, 0) * tm
        _start_tile_gather(lambda r: src[base + r], tm, xn_hbm, xbuf.at[1 - slot], sem.at[1 - slot],
                           gather_priority)
        _store_token_tiles(ys_ref, y)

    @pl.when(i == n_valid - 1)
    def _drain():
        _wait_tile_gather(tm, xn_hbm, xbuf.at[1 - slot], sem.at[1 - slot])

    @pl.when(i >= n_valid)
    def _pad():
        ys_ref[...] = jnp.zeros_like(ys_ref)


def _expert_ffn(xn_tiles, tab, pos_a, pos_b, n_tiles, wg, wu, wd):
    n = pos_a.shape[0]
    tm = FFN_TM
    d, dff = wg.shape[1], wg.shape[2]
    rows = tm * TOK_TILE_ROWS
    idx = lambda i, tab, pa, pb: (tab[TAB_TILE_EXPERT, i], 0, 0)
    grid_spec = pltpu.PrefetchScalarGridSpec(
        num_scalar_prefetch=3,
        grid=(n_tiles,),
        in_specs=[pl.BlockSpec(memory_space=pl.ANY),
                  pl.BlockSpec((None, d, dff), idx),
                  pl.BlockSpec((None, d, dff), idx),
                  pl.BlockSpec((None, dff, d), idx)],
        out_specs=pl.BlockSpec((rows, LANES), lambda i, tab, pa, pb: (i, 0)),
        scratch_shapes=[pltpu.SMEM((n_tiles * tm,), jnp.int32),
                        pltpu.VMEM((2, rows, LANES), F32),
                        pltpu.VMEM((3, d, dff), BF16),
                        pltpu.SemaphoreType.DMA((2,))],
    )
    return pl.pallas_call(
        functools.partial(_ffn_kernel, n_tokens=n, tm=tm),
        out_shape=jax.ShapeDtypeStruct((n_tiles * rows, LANES), F32),
        grid_spec=grid_spec,
        compiler_params=pltpu.CompilerParams(
            dimension_semantics=("arbitrary",), vmem_limit_bytes=VMEM_LIMIT),
        name="expert_ffn",
    )(tab, pos_a, pos_b, xn_tiles, wg, wu, wd)


def _positions_kernel(info_ref, cnt_ref, pos_ref, tab_ref, *, tm, chunk):
    lane_i = lax.broadcasted_iota(jnp.int32, (1, LANES), 1)
    lane = lane_i.astype(F32)
    is_expert = jnp.logical_and(lane_i >= RL_E, lane_i < RL_E + N_EXPERTS)
    cnt = jnp.where(is_expert, cnt_ref[...], 0.0)
    padded = jnp.floor((cnt + (tm - 1)) / tm) * tm
    ri = lax.broadcasted_iota(jnp.int32, (LANES, LANES), 0)
    ci = lax.broadcasted_iota(jnp.int32, (LANES, LANES), 1)
    before = jnp.where(ri < ci, 1.0, 0.0).astype(BF16)
    hi, mid, lo = _split3(jnp.broadcast_to(padded, (SUBLANES, LANES)))
    off = ((_dot(hi, before) + _dot(mid, before)) + _dot(lo, before))[0:1, :]
    pick = jnp.where(lax.broadcasted_iota(jnp.int32, (SUBLANES, LANES), 0) == lane_i, 1.0, 0.0).astype(BF16)

    total = jnp.sum(padded, axis=-1, keepdims=True)
    tile_row = lax.broadcasted_iota(jnp.int32, (TAB_LANES, 1), 0).astype(F32) * tm
    ends = off + padded
    done = jnp.logical_and(is_expert, ends <= jnp.minimum(tile_row, total - 1.0))
    te_col = jnp.sum(jnp.where(done, 1.0, 0.0), axis=-1, keepdims=True)
    te_rows = _dot_nt(pick, jnp.where(lane_i == 0, te_col, 0.0).astype(BF16))
    tab_ref[...] = jnp.zeros_like(tab_ref)
    tab_ref[TAB_OFF:TAB_OFF + 1, 0:LANES] = off.astype(jnp.int32)
    tab_ref[TAB_CNT:TAB_CNT + 1, 0:LANES] = cnt.astype(jnp.int32)
    tab_ref[TAB_TILE_EXPERT:TAB_TILE_EXPERT + 1, :] = te_rows[0:1, :].astype(jnp.int32)
    tab_ref[TAB_NVALID:TAB_NVALID + 1, 0:LANES] = jnp.broadcast_to(total / tm, (1, LANES)).astype(jnp.int32)

    n = info_ref.shape[0]
    for c0 in range(0, n, chunk):
        blk = info_ref[c0:c0 + chunk, :]
        lane_a = blk[:, R_EA:R_EA + 1] + RL_E
        lane_b = blk[:, R_EB:R_EB + 1] + RL_E
        pos_a = blk[:, R_RA:R_RA + 1] + jnp.sum(jnp.where(lane == lane_a, off, 0.0), axis=-1, keepdims=True)
        pos_b = blk[:, R_RB:R_RB + 1] + jnp.sum(jnp.where(lane == lane_b, off, 0.0), axis=-1, keepdims=True)
        z_hi, z_mid, z_lo = _split3(jnp.where(lane_i == 0, pos_a, jnp.where(lane_i == 1, pos_b, 0.0)))
        rows = (_dot_nt(pick, z_hi) + _dot_nt(pick, z_mid)) + _dot_nt(pick, z_lo)
        pos_ref[:, c0:c0 + chunk] = rows.astype(jnp.int32)


def _routing_tables(info, counts, n_tiles, tm):
    n = info.shape[0]
    groups = n // LANES
    assert n_tiles <= TAB_LANES
    chunk = LANES * max(g for g in range(1, 65) if groups % g == 0)
    pos, tab = pl.pallas_call(
        functools.partial(_positions_kernel, tm=tm, chunk=chunk),
        out_shape=(jax.ShapeDtypeStruct((SUBLANES, n), jnp.int32),
                   jax.ShapeDtypeStruct((SUBLANES, TAB_LANES), jnp.int32)),
        grid=(1,),
        in_specs=[pl.BlockSpec((n, LANES), lambda i: (0, 0)), pl.BlockSpec((1, LANES), lambda i: (0, 0))],
        out_specs=(pl.BlockSpec((SUBLANES, n), lambda i: (0, 0)),
                   pl.BlockSpec((SUBLANES, TAB_LANES), lambda i: (0, 0))),
        compiler_params=pltpu.CompilerParams(
            dimension_semantics=("arbitrary",), vmem_limit_bytes=VMEM_LIMIT),
        name="positions",
    )(info, counts)
    return tab, pos[0], pos[1]


def _combine_kernel(pa_ref, pb_ref, h_ref, info_ref, ys_hbm, nfin_ref, yp_ref, ysm_ref,
                    buf_a, buf_b, sem, *, n_prompt_tiles):
    i = pl.program_id(0)
    n_steps = pl.num_programs(0)
    tt = h_ref.shape[0]
    slot = lax.rem(i, 2)

    def start(tile, s):
        base = tile * tt
        _start_tile_gather(lambda r: pa_ref[base + r] * TOK_TILE_ROWS, tt, ys_hbm, buf_a.at[s],
                           sem.at[s], lambda r: 0)
        _start_tile_gather(lambda r: pb_ref[base + r] * TOK_TILE_ROWS, tt, ys_hbm, buf_b.at[s],
                           sem.at[s], lambda r: 1)

    @pl.when(i == 0)
    def _first():
        start(0, 0)

    _wait_tile_gather(tt, ys_hbm, buf_a.at[slot], sem.at[slot])
    _wait_tile_gather(tt, ys_hbm, buf_b.at[slot], sem.at[slot])
    info = info_ref[...]
    rows_a = _load_token_tiles(buf_a.at[slot], tt)
    rows_b = _load_token_tiles(buf_b.at[slot], tt)
    x = h_ref[...]
    start(lax.rem(i + 1, n_steps), 1 - slot)
    h = x + info[:, R_GA:R_GA + 1] * rows_a + info[:, R_GB:R_GB + 1] * rows_b
    y = h * _rms_scale(h) * nfin_ref[...]

    @pl.when(i < n_prompt_tiles)
    def _prompt():
        yp_ref[...] = y

    @pl.when(i >= n_prompt_tiles)
    def _sample():
        ysm_ref[...] = y

    @pl.when(i == n_steps - 1)
    def _drain():
        _wait_tile_gather(tt, ys_hbm, buf_a.at[1 - slot], sem.at[1 - slot])
        _wait_tile_gather(tt, ys_hbm, buf_b.at[1 - slot], sem.at[1 - slot])


def _combine(hmid, info, ys, pos_a, pos_b, nfin, n_prompt):
    n, d = hmid.shape
    tt = CHUNK
    n_prompt_tiles = n_prompt // tt
    n_sample = n - n_prompt
    grid_spec = pltpu.PrefetchScalarGridSpec(
        num_scalar_prefetch=2,
        grid=(n // tt,),
        in_specs=[pl.BlockSpec((tt, d), lambda i, pa, pb: (i, 0)),
                  pl.BlockSpec((tt, LANES), lambda i, pa, pb: (i, 0)),
                  pl.BlockSpec(memory_space=pl.ANY),
                  pl.BlockSpec((1, d), lambda i, pa, pb: (0, 0))],
        out_specs=(pl.BlockSpec((tt, d), lambda i, pa, pb: (jnp.minimum(i, n_prompt_tiles - 1), 0)),
                   pl.BlockSpec((tt, d), lambda i, pa, pb: (jnp.maximum(i - n_prompt_tiles, 0), 0))),
        scratch_shapes=[pltpu.VMEM((2, tt * TOK_TILE_ROWS, LANES), F32),
                        pltpu.VMEM((2, tt * TOK_TILE_ROWS, LANES), F32),
                        pltpu.SemaphoreType.DMA((2,))],
    )
    return pl.pallas_call(
        functools.partial(_combine_kernel, n_prompt_tiles=n_prompt_tiles),
        out_shape=(jax.ShapeDtypeStruct((n_prompt, d), F32),
                   jax.ShapeDtypeStruct((n_sample, d), F32)),
        grid_spec=grid_spec,
        compiler_params=pltpu.CompilerParams(
            dimension_semantics=("arbitrary",), vmem_limit_bytes=VMEM_LIMIT),
        name="combine",
    )(pos_a, pos_b, hmid, info, ys, nfin)


def _moe_and_final_norm(hmid, n_prompt, rp, wg, wu, wd, nfin):
    n = hmid.shape[0]
    tm = FFN_TM
    n_tiles = (2 * n + N_EXPERTS * (tm - 1)) // tm
    xn, info, counts = _router(hmid, rp)
    tab, pos_a, pos_b = _routing_tables(info, counts, n_tiles, tm)
    ys = _expert_ffn(xn, tab, pos_a, pos_b, n_tiles, wg, wu, wd)
    return _combine(hmid, info, ys, pos_a, pos_b, nfin, n_prompt)


def kernel(x_prompt, x_sample, state_mlstm_C, state_mlstm_n, state_mlstm_m, state_mlstm_conv, state_ssm, state_ssm_conv, meta_tokens, norm_mix, w_in, conv_a_w, conv_a_b, w_q, w_k, w_v, b_i, b_f, norm_a, conv_b_w, conv_b_b, dt_bias, a_log, d_skip, norm_b, w_out, norm_ffn, w_r1, b_r1, w_r2, b_r2, w_gate, w_up, w_down, norm_final):
    bsz, seq, d = x_prompt.shape
    nb = x_sample.shape[0]
    d_a = H_A * DH_A
    conv_b = H_B * HD_B + 2 * G_B * N_STATE
    assert w_in.shape[0] == 1 and x_sample.shape[1] == 1 and seq % CHUNK == 0 and nb == CHUNK
    mp = _prep_mixer_params(norm_mix, w_in, conv_a_w, conv_a_b, w_q, w_k, w_v, b_i, b_f, norm_a,
                            conv_b_w, conv_b_b, dt_bias, a_log, d_skip, norm_b, w_out)
    rp = _prep_router_params(norm_ffn, w_r1, b_r1, w_r2, b_r2)
    xmeta = jnp.concatenate([jnp.zeros((CHUNK - N_META, d), F32), meta_tokens.astype(F32)], 0)

    hmid, p_c, p_n, p_m, p_ca, p_s, p_cb = _prompt_mixer(x_prompt.astype(F32), xmeta, mp, nb)
    m0 = jnp.pad(state_mlstm_m.reshape(nb, H_A).astype(F32), ((0, 0), (0, LANES - H_A)))
    hmid, s_c, s_n, s_m, s_ca, s_s, s_cb = _sample_mixer(
        x_sample.reshape(nb, d).astype(F32),
        state_mlstm_C.reshape(nb, H_A, DH_A, DH_A).astype(F32),
        state_mlstm_n.reshape(nb, d_a).astype(F32),
        m0,
        state_mlstm_conv.reshape(nb, (CONV_W - 1) * d_a).astype(F32),
        state_ssm.reshape(nb, H_B // 2, 2 * HD_B, N_STATE).astype(F32),
        state_ssm_conv.reshape(nb, (CONV_W - 1) * conv_b).astype(F32),
        mp, hmid, bsz * seq)

    wshape = w_gate.shape[1:]
    y_p, y_s = _moe_and_final_norm(
        hmid, bsz * seq, rp, w_gate.reshape(wshape).astype(F32), w_up.reshape(wshape).astype(F32),
        w_down.reshape(w_down.shape[1:]).astype(F32), norm_final.reshape(1, d).astype(F32))

    return (y_p.reshape(bsz, seq, d), y_s.reshape(nb, 1, d),
            p_c.reshape(1, bsz, H_A, DH_A, DH_A), p_n.reshape(1, bsz, H_A, DH_A),
            p_m[:, 0, :H_A].reshape(1, bsz, H_A), p_ca.reshape(1, bsz, CONV_W - 1, d_a),
            p_s.reshape(1, bsz, H_B, HD_B, N_STATE), p_cb.reshape(1, bsz, CONV_W - 1, conv_b),
            s_c.reshape(1, nb, H_A, DH_A, DH_A), s_n.reshape(1, nb, H_A, DH_A),
            s_m[:, :H_A].reshape(1, nb, H_A), s_ca.reshape(1, nb, CONV_W - 1, d_a),
            s_s.reshape(1, nb, H_B, HD_B, N_STATE), s_cb.reshape(1, nb, CONV_W - 1, conv_b))
```

```python
import functools
import math

import jax
import jax.numpy as jnp
from jax import lax
from jax.experimental import pallas as pl
from jax.experimental.pallas import tpu as pltpu

F32 = jnp.float32
BF16 = jnp.bfloat16

EPS = 1e-6
N_META = 16
CONV_W = 4
CHUNK = 128
H_A = 8
DH_A = 128
H_B = 16
HD_B = 64
N_STATE = 128
G_B = 2
N_EGROUPS = 4
N_EPG = 4
N_EXPERTS = 16
LANES = 128
SUBLANES = 8
CONV_HDR = SUBLANES
VMEM_LIMIT = 56 * 1024 * 1024

L_F = 0
L_DTA = 8
L_I = 24
L_DT = 32

NEG_INF = float("-inf")


def _dot(a, b):
    return jnp.dot(a, b, preferred_element_type=F32)


def _dot_nt(a, b):
    return lax.dot_general(a, b, (((1,), (1,)), ((), ())), preferred_element_type=F32)


def _dot_tn(a, b):
    return lax.dot_general(a, b, (((0,), (0,)), ((), ())), preferred_element_type=F32)


def _split3(x):
    hi = x.astype(BF16)
    r = x - hi.astype(F32)
    mid = r.astype(BF16)
    lo = (r - mid.astype(F32)).astype(BF16)
    return hi, mid, lo


def _silu(x):
    return x * jax.nn.sigmoid(x)


def _softplus_parts(x):
    t = jnp.log1p(jnp.exp(-jnp.abs(x)))
    return jnp.maximum(x, 0.0) + t, jnp.minimum(x, 0.0) - t


def _rms_scale(x):
    return lax.rsqrt(jnp.mean(x * x, axis=-1, keepdims=True) + EPS)


TOK_TILE_ROWS = SUBLANES


def _store_token_tiles(ref, x):
    n = x.shape[0]
    for j in range(TOK_TILE_ROWS):
        ref[pl.ds(j, n, stride=TOK_TILE_ROWS), :] = x[:, j * LANES:(j + 1) * LANES]


def _causal_conv(x, tail, w_ref, b_ref):
    n_tail = tail.shape[0]
    row = lax.broadcasted_iota(jnp.int32, (n_tail, 1), 0)
    acc = w_ref[CONV_W - 1:CONV_W, :] * x + b_ref[...]
    for k in range(1, CONV_W):
        rolled = pltpu.roll(x, k, axis=0)
        head = jnp.where(row < k, pltpu.roll(tail, k, axis=0), rolled[0:n_tail])
        shifted = jnp.concatenate([head, rolled[n_tail:]], axis=0)
        acc = acc + w_ref[CONV_W - 1 - k:CONV_W - k, :] * shifted
    return acc


def _load_token_tiles(ref, n):
    return jnp.concatenate(
        [ref[pl.ds(j, n, stride=TOK_TILE_ROWS), :] for j in range(TOK_TILE_ROWS)], axis=1)


PROMPT_ROWS = 1


def _prompt_rows_kernel(xmeta_ref, xp_ref, nmix_ref, wcat_ref, bsm_ref, alog_ref,
                        cwa_ref, cba_ref, cwb_ref, cbb_ref, wq_ref, wk_ref, wv_ref,
                        na_ref, nb_ref, dsk_ref, wout_ref,
                        hmid_hbm, c_ref, n_ref, m_ref, conva_ref, s_ref, convb_ref,
                        xa_buf, xbc_buf, y_buf, merged, hout, sem, *, seq, n_prompt_rows):
    p = pl.program_id(0)
    c = pl.program_id(1)
    last_p = pl.num_programs(0) - 1
    last_c = pl.num_programs(1) - 1
    T = CHUNK
    RB = xp_ref.shape[0]
    d_a = H_A * DH_A
    d_b = H_B * HD_B
    conv_b = d_b + 2 * G_B * N_STATE

    def out_copy(r, row0):
        return pltpu.make_async_copy(hout.at[r], hmid_hbm.at[pl.ds(row0, T), :], sem.at[r])

    @pl.when(c == 0)
    def _init():
        c_ref[...] = jnp.zeros_like(c_ref)
        n_ref[...] = jnp.zeros_like(n_ref)
        m_ref[...] = jnp.zeros_like(m_ref)
        s_ref[...] = jnp.zeros_like(s_ref)
        xa_buf[...] = jnp.zeros_like(xa_buf)
        xbc_buf[...] = jnp.zeros_like(xbc_buf)

    @pl.when(jnp.logical_and(p == 0, c == 0))
    def _clear_sample_rows():
        hout[0] = jnp.zeros((T, hout.shape[2]), F32)
        cp = out_copy(0, n_prompt_rows)
        cp.start()
        cp.wait()

    row = lax.broadcasted_iota(jnp.int32, (T, 1), 0)
    valid = jnp.logical_or(c > 0, row >= T - N_META)
    xs_in = [jnp.where(c == 0, xmeta_ref[...], xp_ref[r]) for r in range(RB)]
    x2 = jnp.concatenate(xs_in, axis=0)
    hn = (x2 * _rms_scale(x2) * nmix_ref[...]).astype(BF16)

    lane = lax.broadcasted_iota(jnp.int32, (1, LANES), 1)
    lane_f = lane < L_DTA
    lane_dta = jnp.logical_and(lane >= L_DTA, lane < L_I)
    lane_i = jnp.logical_and(lane >= L_I, lane < L_DT)
    lane_dt = jnp.logical_and(lane >= L_DT, lane < L_DT + H_B)
    ri = lax.broadcasted_iota(jnp.int32, (T, T), 0)
    ci = lax.broadcasted_iota(jnp.int32, (T, T), 1)
    causal = ri >= ci
    tri = jnp.where(causal, 1.0, 0.0).astype(BF16)
    a_neg = jnp.where(lane_dta, -jnp.exp(alog_ref[...]), 0.0)
    left = lane < HD_B
    top = lax.broadcasted_iota(jnp.int32, (LANES, 1), 0) < HD_B

    off_small = 2 * d_a + d_b + conv_b
    pre2 = _dot(hn, wcat_ref[:, off_small:]) + bsm_ref[...]
    xa2 = _dot(hn, wcat_ref[:, 0:d_a])

    gcols, grows, xcs, xabs = [], [], [], []

    def gate_tables():
        for r in range(RB):
            pre = pre2[r * T:(r + 1) * T]
            sp, lsig = _softplus_parts(pre)
            to_cum = jnp.where(lane_f, lsig, jnp.where(lane_dta, sp * a_neg, 0.0))
            to_cum = jnp.where(valid, to_cum, 0.0)
            hi, mid, lo = _split3(to_cum)
            cum = _dot(tri, hi) + _dot(tri, mid) + _dot(tri, lo)
            extra = jnp.where(lane_i, jnp.where(valid, pre, NEG_INF),
                              jnp.where(lane_dt, jnp.where(valid, sp, 0.0), 0.0))
            gcol = cum + extra
            gcols.append(gcol)
            grows.append(gcol.T)

    gate_tables()
    for r in range(RB):
        rs = slice(r * T, (r + 1) * T)
        xa = xa2[rs]
        xc = _causal_conv(xa, xa_buf[r], cwa_ref, cba_ref)
        xa_buf[r] = xa[T - CONV_HDR:T, :]
        conva_ref[r] = xa[T - 3:T, :]
        xcs.append(_silu(xc).astype(BF16))
        xabs.append(xa.astype(BF16))

    items = [(r, h) for h in range(H_A) for r in range(RB)]
    hsl = lambda h: slice(h * DH_A, (h + 1) * DH_A)
    m_alls = [m_ref[r] for r in range(RB)]
    m_news = list(m_alls)
    qs, ks, vs, qks, st, dd = {}, {}, {}, {}, {}, {}

    def stage_qkv(it):
        r, h = it
        qs[it] = _dot(xcs[r][:, hsl(h)], wq_ref[h]).astype(BF16)
        ks[it] = _dot(xcs[r][:, hsl(h)], wk_ref[h]) * (DH_A ** -0.5)
        vs[it] = _dot(xabs[r][:, hsl(h)], wv_ref[h]).astype(BF16)

    def stage_qk(it):
        qks[it] = _dot_nt(qs[it], ks[it].astype(BF16))

    pairs_per_group = H_B // G_B // 2
    groups = [(r, g) for g in range(G_B) for r in range(RB)]
    proj, xbcs, bgs, cgs, cbs = {}, [], {}, {}, {}

    def project(name, lo, hi):
        proj[name] = _dot(hn, wcat_ref[:, lo:hi])

    def ssd_inputs():
        for r in range(RB):
            xbc = proj["xbc"][r * T:(r + 1) * T]
            xbc_c = _causal_conv(xbc, xbc_buf[r], cwb_ref, cbb_ref)
            xbc_buf[r] = xbc[T - CONV_HDR:T, :]
            convb_ref[r] = xbc[T - 3:T, :]
            xbcs.append(_silu(xbc_c))
        for r, g in groups:
            bgs[(r, g)] = xbcs[r][:, d_b + g * N_STATE:d_b + (g + 1) * N_STATE].astype(BF16)
            cgs[(r, g)] = xbcs[r][:, d_b + (G_B + g) * N_STATE:d_b + (G_B + g + 1) * N_STATE].astype(BF16)
            cbs[(r, g)] = _dot_nt(cgs[(r, g)], bgs[(r, g)])

    def stage_weights(it):
        r, h = it
        gcol, grow = gcols[r], grows[r]
        b_col = gcol[:, L_F + h:L_F + h + 1]
        i_col = gcol[:, L_I + h:L_I + h + 1]
        b_row = grow[L_F + h:L_F + h + 1, :]
        i_row = grow[L_I + h:L_I + h + 1, :]
        m0 = m_alls[r][:, h:h + 1]
        dmat = jnp.where(causal, b_col - (b_row - i_row), NEG_INF)
        m_inter = b_col + m0
        m = jnp.maximum(m_inter, jnp.max(dmat, axis=-1, keepdims=True))
        w_inter = jnp.exp(m_inter - m)
        s = qks[it] * jnp.exp(dmat - m)
        n0 = n_ref[r, h:h + 1, :]
        den = (jnp.sum(s, axis=-1, keepdims=True)
               + w_inter * jnp.sum(qs[it].astype(F32) * n0, axis=-1, keepdims=True))
        m_last = m[T - 1:T, :]
        b_last = b_col[T - 1:T, :]
        dec = jnp.exp(b_last + m0 - m_last)
        kw = ks[it] * jnp.exp(b_last - b_col + i_col - m_last)
        n_ref[r, h:h + 1, :] = dec * n0 + jnp.sum(kw, axis=0, keepdims=True)
        m_news[r] = jnp.where(lane == h, m_last, m_news[r])
        st[it] = (s.astype(BF16), kw.astype(BF16), w_inter,
                  jnp.maximum(jnp.abs(den), jnp.exp(-m)), dec)

    def stage_readout(it):
        r, h = it
        s_b, kw_b, w_inter, den, dec = st[it]
        c0 = c_ref[r, h]
        num = _dot(s_b, vs[it]) + w_inter * _dot(qs[it], c0.astype(BF16))
        c_ref[r, h] = dec * c0 + _dot_tn(kw_b, vs[it])
        dd[it] = num / den

    def stage_head_out(it):
        r, h = it
        hh = dd[it]
        hh = hh * _rms_scale(hh) * na_ref[:, hsl(h)]
        merged[r * T:(r + 1) * T, hsl(h)] = (
            hh * jax.nn.sigmoid(proj["za"][r * T:(r + 1) * T, hsl(h)])).astype(BF16)

    pairs = [(r, pi) for pi in range(H_B // 2) for r in range(RB)]
    psl = lambda pi: slice(pi * LANES, (pi + 1) * LANES)
    sw = {}

    def stage_decay(pr):
        r, pi = pr
        g = pi // pairs_per_group
        gcol, grow = gcols[r], grows[r]
        xpair = xbcs[r][:, psl(pi)]
        scs, a_cols, w_cols, a_lasts = [], [], [], []
        for j in (2 * pi, 2 * pi + 1):
            a_col = gcol[:, L_DTA + j:L_DTA + j + 1]
            a_row = grow[L_DTA + j:L_DTA + j + 1, :]
            dt_col = gcol[:, L_DT + j:L_DT + j + 1]
            dt_row = grow[L_DT + j:L_DT + j + 1, :]
            decay = jnp.exp(jnp.where(causal, a_col - a_row, NEG_INF))
            scs.append((cbs[(r, g)] * decay * dt_row).astype(BF16))
            a_last = a_col[T - 1:T, :]
            a_cols.append(a_col)
            a_lasts.append(a_last)
            w_cols.append(jnp.exp(a_last - a_col) * dt_col)
        sw[pr] = (scs, xpair.astype(BF16),
                  (xpair * jnp.where(left, w_cols[0], w_cols[1])).astype(BF16),
                  jnp.exp(jnp.where(left, a_cols[0], a_cols[1])),
                  jnp.exp(jnp.where(top, a_lasts[0], a_lasts[1])))
    def stage_pair_out(pr):
        r, pi = pr
        g = pi // pairs_per_group
        scs, xpb, xw, ea, ea_last = sw[pr]
        s0 = s_ref[r, pi]
        y = jnp.where(left, _dot(scs[0], xpb), _dot(scs[1], xpb))
        y = y + ea * _dot_nt(cgs[(r, g)], s0.astype(BF16))
        s_ref[r, pi] = ea_last * s0 + _dot_tn(xw, bgs[(r, g)])
        y = y + dsk_ref[:, psl(pi)] * xbcs[r][:, psl(pi)]
        y_buf[r, :, psl(pi)] = y * _silu(proj["zb"][r * T:(r + 1) * T, psl(pi)])

    def each(stage, seq):
        for e in seq:
            stage(e)

    each(stage_qkv, items)
    project("xbc", 2 * d_a + d_b, off_small)
    each(stage_qk, items)
    project("za", d_a, 2 * d_a)
    ssd_inputs()
    each(stage_weights, items)
    each(stage_readout, items)
    project("zb", 2 * d_a, 2 * d_a + d_b)
    for r in range(RB):
        m_ref[r] = m_news[r]
    each(stage_head_out, items)
    each(stage_decay, pairs)
    each(stage_pair_out, pairs)
    gw = d_b // G_B
    for r in range(RB):
        for g in range(G_B):
            yg = y_buf[r, :, g * gw:(g + 1) * gw]
            merged[r * T:(r + 1) * T, d_a + g * gw:d_a + (g + 1) * gw] = (
                yg * _rms_scale(yg) * nb_ref[:, g * gw:(g + 1) * gw]).astype(BF16)

    @pl.when(c > 0)
    def _out():
        out2 = x2 + _dot(merged[...], wout_ref[...])

        @pl.when(jnp.logical_or(c > 1, p > 0))
        def _wait_previous():
            for r in range(RB):
                out_copy(r, 0).wait()

        for r in range(RB):
            hout[r] = out2[r * T:(r + 1) * T]
            out_copy(r, (p * RB + r) * seq + (c - 1) * T).start()

        @pl.when(jnp.logical_and(p == last_p, c == last_c))
        def _drain():
            for r in range(RB):
                out_copy(r, 0).wait()


def _const_spec(shape):
    nd = len(shape)
    return pl.BlockSpec(shape, lambda b, c, _nd=nd: (0,) * _nd)


def _prompt_mixer(x_prompt, xmeta, p, n_extra_rows):
    bsz, seq, d = x_prompt.shape
    assert n_extra_rows == CHUNK and seq % CHUNK == 0
    n_chunks = seq // CHUNK + 1
    cps = seq // CHUNK
    d_a = H_A * DH_A
    conv_b = H_B * HD_B + 2 * G_B * N_STATE
    consts = [p["nmix"], p["wcat"], p["bsm"], p["alog"], p["cwa"], p["cba"], p["cwb"], p["cbb"],
              p["wq"], p["wk"], p["wv"], p["na"], p["nb"], p["dsk"], p["wout"]]
    rb = PROMPT_ROWS
    assert bsz % rb == 0
    in_specs = [_const_spec(xmeta.shape),
                pl.BlockSpec((rb, CHUNK, d), lambda b, c: (b, jnp.maximum(c - 1, 0), 0))]
    in_specs += [_const_spec(a.shape) for a in consts]
    out_shape = (
        jax.ShapeDtypeStruct((bsz * seq + n_extra_rows, d), F32),
        jax.ShapeDtypeStruct((bsz, H_A, DH_A, DH_A), F32),
        jax.ShapeDtypeStruct((bsz, H_A, DH_A), F32),
        jax.ShapeDtypeStruct((bsz, 1, LANES), F32),
        jax.ShapeDtypeStruct((bsz, CONV_W - 1, d_a), F32),
        jax.ShapeDtypeStruct((bsz, H_B // 2, 2 * HD_B, N_STATE), F32),
        jax.ShapeDtypeStruct((bsz, CONV_W - 1, conv_b), F32),
    )
    out_specs = (
        pl.BlockSpec(memory_space=pl.ANY),
        pl.BlockSpec((rb, H_A, DH_A, DH_A), lambda b, c: (b, 0, 0, 0)),
        pl.BlockSpec((rb, H_A, DH_A), lambda b, c: (b, 0, 0)),
        pl.BlockSpec((rb, 1, LANES), lambda b, c: (b, 0, 0)),
        pl.BlockSpec((rb, CONV_W - 1, d_a), lambda b, c: (b, 0, 0)),
        pl.BlockSpec((rb, H_B // 2, 2 * HD_B, N_STATE), lambda b, c: (b, 0, 0, 0)),
        pl.BlockSpec((rb, CONV_W - 1, conv_b), lambda b, c: (b, 0, 0)),
    )
    return pl.pallas_call(
        functools.partial(_prompt_rows_kernel, seq=seq, n_prompt_rows=bsz * seq),
        out_shape=out_shape,
        grid=(bsz // rb, n_chunks),
        in_specs=in_specs,
        out_specs=out_specs,
        scratch_shapes=[
            pltpu.VMEM((rb, CONV_HDR, d_a), F32),
            pltpu.VMEM((rb, CONV_HDR, conv_b), F32),
            pltpu.VMEM((rb, CHUNK, H_B * HD_B), F32),
            pltpu.VMEM((rb * CHUNK, d_a + H_B * HD_B), BF16),
            pltpu.VMEM((rb, CHUNK, d), F32),
            pltpu.SemaphoreType.DMA((rb,)),
        ],
        compiler_params=pltpu.CompilerParams(
            dimension_semantics=("arbitrary", "arbitrary"), vmem_limit_bytes=VMEM_LIMIT),
        name="prompt_mixer",
    )(xmeta, x_prompt, *consts)


def _regroup_w_in_kernel(w_ref, o_ref):
    d_a = H_A * DH_A
    d_b = H_B * HD_B
    conv_b = d_b + 2 * G_B * N_STATE
    o_i = 2 * d_a
    o_f = o_i + H_A
    o_zb = o_f + H_A
    o_xbc = o_zb + d_b
    o_dt = o_xbc + conv_b
    rows = w_ref.shape[0]
    o_ref[:, 0:2 * d_a] = w_ref[:, 0:2 * d_a].astype(BF16)
    o_ref[:, 2 * d_a:2 * d_a + d_b] = w_ref[:, o_zb:o_zb + d_b].astype(BF16)
    o_ref[:, 2 * d_a + d_b:2 * d_a + d_b + conv_b] = w_ref[:, o_xbc:o_xbc + conv_b].astype(BF16)
    small = jnp.concatenate(
        [w_ref[:, o_f:o_f + H_A], w_ref[:, o_dt:o_dt + H_B], w_ref[:, o_i:o_i + H_A],
         w_ref[:, o_dt:o_dt + H_B], jnp.zeros((rows, LANES - (L_DT + H_B)), F32)], axis=1)
    o_ref[:, 2 * d_a + d_b + conv_b:] = small.astype(BF16)


def _prep_mixer_params(norm_mix, w_in, conv_a_w, conv_a_b, w_q, w_k, w_v, b_i, b_f, norm_a,
                       conv_b_w, conv_b_b, dt_bias, a_log, d_skip, norm_b, w_out):
    d_a = H_A * DH_A
    d_b = H_B * HD_B
    conv_b = d_b + 2 * G_B * N_STATE
    d_model, d_in = w_in.shape[1], w_in.shape[2]
    n_cols = 2 * d_a + d_b + conv_b + LANES
    rows = 256
    assert w_in.shape[0] == 1 and d_model % rows == 0
    wcat = pl.pallas_call(
        _regroup_w_in_kernel,
        out_shape=jax.ShapeDtypeStruct((d_model, n_cols), BF16),
        grid=(d_model // rows,),
        in_specs=[pl.BlockSpec((None, rows, d_in), lambda i: (0, i, 0))],
        out_specs=pl.BlockSpec((rows, n_cols), lambda i: (i, 0)),
        compiler_params=pltpu.CompilerParams(
            dimension_semantics=("arbitrary",), vmem_limit_bytes=VMEM_LIMIT),
        name="regroup_w_in",
    )(w_in.astype(F32))

    def lanes(parts):
        pieces, at = [], 0
        for off, a in parts:
            pieces += [jnp.zeros((1, off - at), F32), a.astype(F32)]
            at = off + a.shape[1]
        return jnp.concatenate(pieces + [jnp.zeros((1, LANES - at), F32)], axis=1)

    return dict(
        nmix=norm_mix.reshape(1, -1).astype(F32),
        wcat=wcat,
        bsm=lanes([(L_F, b_f), (L_DTA, dt_bias), (L_I, b_i), (L_DT, dt_bias)]),
        alog=lanes([(L_DTA, a_log)]),
        cwa=conv_a_w.reshape(CONV_W, d_a).astype(F32), cba=conv_a_b.reshape(1, d_a).astype(F32),
        cwb=conv_b_w.reshape(CONV_W, conv_b).astype(F32), cbb=conv_b_b.reshape(1, conv_b).astype(F32),
        wq=w_q.reshape(H_A, DH_A, DH_A).astype(BF16), wk=w_k.reshape(H_A, DH_A, DH_A).astype(BF16),
        wv=w_v.reshape(H_A, DH_A, DH_A).astype(BF16),
        na=norm_a.reshape(1, d_a).astype(F32), nb=norm_b.reshape(1, d_b).astype(F32),
        dsk=jnp.repeat(d_skip.reshape(H_B).astype(F32), HD_B)[None, :],
        wout=w_out.reshape(d_a + d_b, -1).astype(BF16),
    )


SAMPLE_BLOCK = 8


def _expand_lanes(vals, first_lane, n_heads, width):
    r = lax.broadcasted_iota(jnp.int32, (LANES, n_heads * width), 0) - first_lane
    c = lax.broadcasted_iota(jnp.int32, (LANES, n_heads * width), 1)
    sel = jnp.logical_and(c >= r * width, c < (r + 1) * width)
    e = jnp.where(sel, 1.0, 0.0).astype(BF16)
    hi, mid, lo = _split3(vals)
    return (_dot(hi, e) + _dot(mid, e)) + _dot(lo, e)


def _sample_pre_kernel(x_ref, nmix_ref, wcat_ref, bsm_ref, alog_ref, cwa_ref, cba_ref, cwb_ref, cbb_ref,
                       wq_ref, wk_ref, wv_ref, dsk_ref, conva_ref, convb_ref, n0_ref, m0_ref,
                       conva_out, convb_out, n1_out, m1_out, g_out, qt_out, kwt_out, xwt_out,
                       v_out, bc_out, a1_out, w1_out, den_out, y1_out, ea_out, zbs_out, zas_out):
    d_a = H_A * DH_A
    d_b = H_B * HD_B
    conv_b = d_b + 2 * G_B * N_STATE
    shift_i = LANES - (L_I - L_F)
    x = x_ref[...]
    hn = (x * _rms_scale(x) * nmix_ref[...]).astype(BF16)
    lane = lax.broadcasted_iota(jnp.int32, (1, LANES), 1)
    lane_f = lane < L_DTA
    lane_dta = jnp.logical_and(lane >= L_DTA, lane < L_I)
    pre = _dot(hn, wcat_ref[:, 2 * d_a + d_b + conv_b:]) + bsm_ref[...]
    sp, lsig = _softplus_parts(pre)
    a_neg = jnp.where(lane_dta, -jnp.exp(alog_ref[...]), 0.0)
    pre_al = pltpu.roll(pre, shift_i, axis=1)
    sp_al = pltpu.roll(sp, shift_i, axis=1)
    m_inter = lsig + m0_ref[...]
    m = jnp.maximum(m_inter, pre_al)
    w_inter = jnp.exp(m_inter - m)
    sfac = jnp.exp(pre_al - m)
    ea = jnp.exp(sp * a_neg)
    dt = sp_al

    xa = _dot(hn, wcat_ref[:, 0:d_a])
    xc = (cwa_ref[0:1, :] * conva_ref[:, 0:d_a] + cwa_ref[1:2, :] * conva_ref[:, d_a:2 * d_a]
          + cwa_ref[2:3, :] * conva_ref[:, 2 * d_a:3 * d_a] + cwa_ref[3:4, :] * xa + cba_ref[...])
    conva_out[:, 0:2 * d_a] = conva_ref[:, d_a:3 * d_a]
    conva_out[:, 2 * d_a:3 * d_a] = xa
    xc = _silu(xc).astype(BF16)
    xab = xa.astype(BF16)
    sf_e = _expand_lanes(sfac, L_F, H_A, DH_A)
    w_e = _expand_lanes(w_inter, L_F, H_A, DH_A)
    qk8 = jnp.zeros((x.shape[0], LANES), F32)
    qn8 = jnp.zeros((x.shape[0], LANES), F32)
    for h in range(H_A):
        sl = slice(h * DH_A, (h + 1) * DH_A)
        q = _dot(xc[:, sl], wq_ref[h])
        k = _dot(xc[:, sl], wk_ref[h]) * (DH_A ** -0.5)
        v = _dot(xab[:, sl], wv_ref[h])
        kw = k * sf_e[:, sl]
        qk8 = jnp.where(lane == h, jnp.sum(q * k, axis=-1, keepdims=True), qk8)
        qn8 = jnp.where(lane == h, jnp.sum(q * n0_ref[:, sl], axis=-1, keepdims=True), qn8)
        n1_out[:, sl] = w_e[:, sl] * n0_ref[:, sl] + kw
        v_out[:, sl] = v
        qt_out[h] = q.T
        kwt_out[h] = kw.T
    s8 = qk8 * sfac
    a1_out[...] = _expand_lanes(s8, L_F, H_A, DH_A) * v_out[...]
    w1_out[...] = w_e
    den_out[...] = jnp.maximum(jnp.abs(_expand_lanes(s8 + w_inter * qn8, L_F, H_A, DH_A)),
                               jnp.exp(-_expand_lanes(m, L_F, H_A, DH_A)))
    m1_out[...] = m
    g_out[...] = jnp.where(lane_f, w_inter, jnp.where(lane_dta, ea, 0.0))
    zas_out[...] = jax.nn.sigmoid(_dot(hn, wcat_ref[:, d_a:2 * d_a]))

    off_xbc = 2 * d_a + d_b
    xbc = _dot(hn, wcat_ref[:, off_xbc:off_xbc + conv_b])
    xbc_c = (cwb_ref[0:1, :] * convb_ref[:, 0:conv_b] + cwb_ref[1:2, :] * convb_ref[:, conv_b:2 * conv_b]
             + cwb_ref[2:3, :] * convb_ref[:, 2 * conv_b:3 * conv_b] + cwb_ref[3:4, :] * xbc + cbb_ref[...])
    convb_out[:, 0:2 * conv_b] = convb_ref[:, conv_b:3 * conv_b]
    convb_out[:, 2 * conv_b:3 * conv_b] = xbc
    xbc_c = _silu(xbc_c)
    xs = xbc_c[:, 0:d_b]
    bc = xbc_c[:, d_b:conv_b]
    bc_out[...] = bc
    heads_per_group = H_B // G_B
    cbl = jnp.zeros((x.shape[0], LANES), F32)
    for g in range(G_B):
        cb_g = jnp.sum(bc[:, g * N_STATE:(g + 1) * N_STATE]
                       * bc[:, (G_B + g) * N_STATE:(G_B + g + 1) * N_STATE], axis=-1, keepdims=True)
        in_g = jnp.logical_and(lane >= L_DTA + g * heads_per_group,
                               lane < L_DTA + (g + 1) * heads_per_group)
        cbl = jnp.where(in_g, cb_g, cbl)
    dt_e = _expand_lanes(dt, L_DTA, H_B, HD_B)
    y1_out[...] = _expand_lanes(cbl * dt, L_DTA, H_B, HD_B) * xs + dsk_ref[...] * xs
    ea_out[...] = _expand_lanes(ea, L_DTA, H_B, HD_B)
    zbs_out[...] = _silu(_dot(hn, wcat_ref[:, 2 * d_a:2 * d_a + d_b]))
    xw = xs * dt_e
    for pi in range(H_B // 2):
        xwt_out[pi] = xw[:, pi * LANES:(pi + 1) * LANES].T


def _sample_state_kernel(g_ref, c0_ref, s0_ref, qt_ref, kwt_ref, xwt_ref, v_ref, bc_ref,
                         c1_ref, s1_ref, qc_ref, ysi_ref):
    i = pl.program_id(0)
    bb = c0_ref.shape[0]
    shift = lax.rem(LANES - lax.rem(i * bb, LANES), LANES)
    lane = lax.broadcasted_iota(jnp.int32, (1, LANES), 1)
    top = lax.broadcasted_iota(jnp.int32, (LANES, 1), 0) < HD_B
    heads_per_group = H_B // G_B
    for h in range(H_A):
        sl = slice(h * DH_A, (h + 1) * DH_A)
        qt = pltpu.roll(qt_ref[h], shift, axis=1)
        kwt = pltpu.roll(kwt_ref[h], shift, axis=1)
        for r in range(bb):
            b = i * bb + r
            c0 = c0_ref[r, h]
            dec = g_ref[b, L_F + h]
            v_row = v_ref[r:r + 1, sl]
            qc_ref[r:r + 1, sl] = jnp.sum(c0 * qt[:, r:r + 1], axis=0, keepdims=True)
            c1_ref[r, h] = dec * c0 + kwt[:, r:r + 1] * v_row
    for pi in range(H_B // 2):
        g = (2 * pi) // heads_per_group
        sl = slice(pi * LANES, (pi + 1) * LANES)
        xwt = pltpu.roll(xwt_ref[pi], shift, axis=1)
        acc = jnp.zeros((LANES, LANES), F32)
        for r in range(bb):
            b = i * bb + r
            s0 = s0_ref[r, pi]
            b_row = bc_ref[r:r + 1, g * N_STATE:(g + 1) * N_STATE]
            c_row = bc_ref[r:r + 1, (G_B + g) * N_STATE:(G_B + g + 1) * N_STATE]
            col = jnp.sum(s0 * c_row, axis=-1, keepdims=True)
            acc = jnp.where(lane == r, col, acc)
            ea_rows = jnp.where(top, g_ref[b, L_DTA + 2 * pi], g_ref[b, L_DTA + 2 * pi + 1])
            s1_ref[r, pi] = ea_rows * s0 + xwt[:, r:r + 1] * b_row
        ysi_ref[:, sl] = acc.T[0:bb, :]


def _sample_post_kernel(a1_ref, w1_ref, den_ref, y1_ref, ea_ref, zbs_ref, zas_ref, x_ref, qc_ref, ysi_ref,
                        na_ref, nb_ref, wout_ref, hall_ref, hmid_ref, merged):
    del hall_ref
    d_a = H_A * DH_A
    d_b = H_B * HD_B
    hh = (a1_ref[...] + w1_ref[...] * qc_ref[...]) / den_ref[...]
    for h in range(H_A):
        sl = slice(h * DH_A, (h + 1) * DH_A)
        hs = hh[:, sl]
        merged[:, sl] = (hs * _rms_scale(hs) * na_ref[:, sl] * zas_ref[:, sl]).astype(BF16)
    y = (y1_ref[...] + ea_ref[...] * ysi_ref[...]) * zbs_ref[...]
    gw = d_b // G_B
    for g in range(G_B):
        yg = y[:, g * gw:(g + 1) * gw]
        merged[:, d_a + g * gw:d_a + (g + 1) * gw] = (
            yg * _rms_scale(yg) * nb_ref[:, g * gw:(g + 1) * gw]).astype(BF16)
    hmid_ref[...] = x_ref[...] + _dot(merged[...], wout_ref[...])


def _vmem_specs(arrays):
    return [pl.BlockSpec(a.shape, lambda *_, _nd=a.ndim: (0,) * _nd) for a in arrays]


def _sample_mixer(x, c0, n0, m0, conva, s0, convb, p, hmid_all, row_offset):
    nb, d = x.shape
    d_a = H_A * DH_A
    d_b = H_B * HD_B
    conv_b = d_b + 2 * G_B * N_STATE
    row = lambda w: jax.ShapeDtypeStruct((nb, w), F32)
    tile = lambda k: jax.ShapeDtypeStruct((k, LANES, nb), F32)
    pre_in = [x, p["nmix"], p["wcat"], p["bsm"], p["alog"], p["cwa"], p["cba"], p["cwb"], p["cbb"],
              p["wq"], p["wk"], p["wv"], p["dsk"], conva, convb, n0, m0]
    pre_out_shape = (row(3 * d_a), row(3 * conv_b), row(d_a), row(LANES), row(LANES),
                     tile(H_A), tile(H_A), tile(H_B // 2), row(d_a), row(2 * G_B * N_STATE),
                     row(d_a), row(d_a), row(d_a), row(d_b), row(d_b), row(d_b), row(d_a))
    (conva1, convb1, n1, m1, g8, qt, kwt, xwt, v, bc, a1, w1, den, y1, ea_e, zbs, zas) = pl.pallas_call(
        _sample_pre_kernel,
        out_shape=pre_out_shape,
        grid=(1,),
        in_specs=_vmem_specs(pre_in),
        out_specs=tuple(pl.BlockSpec(s.shape, lambda i, _nd=len(s.shape): (0,) * _nd) for s in pre_out_shape),
        compiler_params=pltpu.CompilerParams(
            dimension_semantics=("arbitrary",), vmem_limit_bytes=VMEM_LIMIT),
        name="sample_pre",
    )(*pre_in)

    bb = SAMPLE_BLOCK
    const3 = lambda k: pl.BlockSpec((k, LANES, nb), lambda i, g: (0, 0, 0))
    state_grid = pltpu.PrefetchScalarGridSpec(
        num_scalar_prefetch=1,
        grid=(nb // bb,),
        in_specs=[pl.BlockSpec((bb, H_A, DH_A, DH_A), lambda i, g: (i, 0, 0, 0)),
                  pl.BlockSpec((bb, H_B // 2, 2 * HD_B, N_STATE), lambda i, g: (i, 0, 0, 0)),
                  const3(H_A), const3(H_A), const3(H_B // 2),
                  pl.BlockSpec((bb, d_a), lambda i, g: (i, 0)),
                  pl.BlockSpec((bb, 2 * G_B * N_STATE), lambda i, g: (i, 0))],
        out_specs=(pl.BlockSpec((bb, H_A, DH_A, DH_A), lambda i, g: (i, 0, 0, 0)),
                   pl.BlockSpec((bb, H_B // 2, 2 * HD_B, N_STATE), lambda i, g: (i, 0, 0, 0)),
                   pl.BlockSpec((bb, d_a), lambda i, g: (i, 0)),
                   pl.BlockSpec((bb, d_b), lambda i, g: (i, 0))),
    )
    c1, s1, qc, ysi = pl.pallas_call(
        _sample_state_kernel,
        out_shape=(jax.ShapeDtypeStruct(c0.shape, F32), jax.ShapeDtypeStruct(s0.shape, F32),
                   row(d_a), row(d_b)),
        grid_spec=state_grid,
        compiler_params=pltpu.CompilerParams(
            dimension_semantics=("arbitrary",), vmem_limit_bytes=VMEM_LIMIT),
        name="sample_state",
    )(g8, c0, s0, qt, kwt, xwt, v, bc)

    post_in = [a1, w1, den, y1, ea_e, zbs, zas, x, qc, ysi, p["na"], p["nb"], p["wout"]]
    hmid_all = pl.pallas_call(
        _sample_post_kernel,
        out_shape=jax.ShapeDtypeStruct(hmid_all.shape, F32),
        grid=(1,),
        in_specs=_vmem_specs(post_in) + [pl.BlockSpec(memory_space=pl.ANY)],
        out_specs=pl.BlockSpec((nb, d), lambda i: (row_offset // nb, 0)),
        scratch_shapes=[pltpu.VMEM((nb, d_a + d_b), BF16)],
        input_output_aliases={len(post_in): 0},
        compiler_params=pltpu.CompilerParams(
            dimension_semantics=("arbitrary",), vmem_limit_bytes=VMEM_LIMIT),
        name="sample_post",
    )(*post_in, hmid_all)
    return hmid_all, c1, n1, m1, conva1, s1, convb1


R_EA, R_EB, R_RA, R_RB, R_GA, R_GB = 0, 1, 2, 3, 4, 5
RL_E = N_EGROUPS


def _router_kernel(h_ref, nf_ref, whi_ref, wmid_ref, br_ref, xn_ref, info_ref, cnt_ref, carry):
    i = pl.program_id(0)
    tr = h_ref.shape[0]

    @pl.when(i == 0)
    def _init():
        carry[...] = jnp.zeros_like(carry)

    h = h_ref[...]
    xn = h * _rms_scale(h) * nf_ref[...]
    _store_token_tiles(xn_ref, xn)
    x_hi, x_mid, _ = _split3(xn)
    logits = (_dot(x_hi, whi_ref[...]) + _dot(x_hi, wmid_ref[...]) + _dot(x_mid, whi_ref[...])
              + br_ref[...])
    lane_i = lax.broadcasted_iota(jnp.int32, (1, LANES), 1)
    lane = lane_i.astype(F32)
    big = float(LANES)

    def first_lane_of(cond):
        return jnp.min(jnp.where(cond, lane, big), axis=-1, keepdims=True)

    l1 = jnp.where(lane_i < N_EGROUPS, logits, NEG_INF)
    e1 = jnp.exp(l1 - jnp.max(l1, axis=-1, keepdims=True))
    p1 = e1 / jnp.sum(e1, axis=-1, keepdims=True)
    gp = jnp.max(p1, axis=-1, keepdims=True)
    gidx = first_lane_of(p1 == gp)
    lo = RL_E + N_EPG * gidx
    l2 = jnp.where(jnp.logical_and(lane >= lo, lane < lo + N_EPG), logits, NEG_INF)
    va = jnp.max(l2, axis=-1, keepdims=True)
    ia = first_lane_of(l2 == va)
    l2b = jnp.where(lane == ia, NEG_INF, l2)
    vb = jnp.max(l2b, axis=-1, keepdims=True)
    ib = first_lane_of(l2b == vb)
    eb = jnp.exp(vb - va)
    wa = 1.0 / (1.0 + eb)
    wb = eb / (1.0 + eb)

    is_a = lane == ia
    is_b = lane == ib
    onehot = jnp.where(jnp.logical_or(is_a, is_b), 1.0, 0.0)
    ri = lax.broadcasted_iota(jnp.int32, (tr, tr), 0)
    ci = lax.broadcasted_iota(jnp.int32, (tr, tr), 1)
    tri = jnp.where(ri >= ci, 1.0, 0.0).astype(BF16)
    incl = _dot(tri, onehot.astype(BF16))
    excl = incl - onehot + carry[...]
    rank_a = jnp.sum(jnp.where(is_a, excl, 0.0), axis=-1, keepdims=True)
    rank_b = jnp.sum(jnp.where(is_b, excl, 0.0), axis=-1, keepdims=True)
    carry[...] = carry[...] + incl[tr - 1:tr, :]
    cnt_ref[...] = carry[...]

    info = jnp.where(lane_i == R_EA, ia - RL_E, 0.0)
    info = jnp.where(lane_i == R_EB, ib - RL_E, info)
    info = jnp.where(lane_i == R_RA, rank_a, info)
    info = jnp.where(lane_i == R_RB, rank_b, info)
    info = jnp.where(lane_i == R_GA, gp * wa, info)
    info = jnp.where(lane_i == R_GB, gp * wb, info)
    info_ref[...] = info


def _row_tile(n, candidates):
    for t in candidates:
        if n % t == 0:
            return t
    raise ValueError(f"no row tile for {n} rows among {candidates}")


def _router(hmid, rp):
    n, d = hmid.shape
    assert d == TOK_TILE_ROWS * LANES
    tr = _row_tile(n, (512, 384, 256, 128))
    return pl.pallas_call(
        _router_kernel,
        out_shape=(jax.ShapeDtypeStruct((n * TOK_TILE_ROWS, LANES), F32),
                   jax.ShapeDtypeStruct((n, LANES), F32),
                   jax.ShapeDtypeStruct((1, LANES), F32)),
        grid=(n // tr,),
        in_specs=[pl.BlockSpec((tr, d), lambda i: (i, 0)),
                  pl.BlockSpec((1, d), lambda i: (0, 0)),
                  pl.BlockSpec((d, LANES), lambda i: (0, 0)),
                  pl.BlockSpec((d, LANES), lambda i: (0, 0)),
                  pl.BlockSpec((1, LANES), lambda i: (0, 0))],
        out_specs=(pl.BlockSpec((tr * TOK_TILE_ROWS, LANES), lambda i: (i, 0)),
                   pl.BlockSpec((tr, LANES), lambda i: (i, 0)),
                   pl.BlockSpec((1, LANES), lambda i: (0, 0))),
        scratch_shapes=[pltpu.VMEM((1, LANES), F32)],
        compiler_params=pltpu.CompilerParams(
            dimension_semantics=("arbitrary",), vmem_limit_bytes=VMEM_LIMIT),
        name="router",
    )(hmid, rp["nf"], rp["whi"], rp["wmid"], rp["br"])


def _prep_router_params(norm_ffn, w_r1, b_r1, w_r2, b_r2):
    d = w_r1.shape[1]
    w = jnp.concatenate([w_r1.reshape(d, N_EGROUPS).astype(F32), w_r2.reshape(d, N_EXPERTS).astype(F32),
                         jnp.zeros((d, LANES - RL_E - N_EXPERTS), F32)], axis=1)
    whi = w.astype(BF16)
    wmid = (w - whi.astype(F32)).astype(BF16)
    br = jnp.concatenate([b_r1.reshape(1, N_EGROUPS).astype(F32), b_r2.reshape(1, N_EXPERTS).astype(F32),
                          jnp.zeros((1, LANES - RL_E - N_EXPERTS), F32)], axis=1)
    return dict(nf=norm_ffn.reshape(1, d).astype(F32), whi=whi, wmid=wmid, br=br)


FFN_TM = 256
N_GATHER_SLOTS = 3


def _start_tile_gather(first_row_of, n_rows, src_hbm, dst, sem, priority_of):
    for r in range(n_rows):
        start = pl.multiple_of(first_row_of(r), TOK_TILE_ROWS)
        pltpu.make_async_copy(src_hbm.at[pl.ds(start, TOK_TILE_ROWS), :],
                              dst.at[pl.ds(r * TOK_TILE_ROWS, TOK_TILE_ROWS), :],
                              sem).start(priority=priority_of(r))


def _wait_tile_gather(n_rows, src_hbm, dst, sem):
    pltpu.make_async_copy(src_hbm.at[pl.ds(0, n_rows * TOK_TILE_ROWS), :], dst, sem).wait()


TAB_OFF, TAB_CNT, TAB_TILE_EXPERT, TAB_NVALID = 0, 1, 2, 3
TAB_LANES = 2 * LANES


def _ffn_kernel(tab_ref, pa_ref, pb_ref, xn_hbm, wg_ref, wu_ref, wd_ref, ys_ref,
                src, xbuf, wbf, sem, *, n_tokens, tm):
    i = pl.program_id(0)
    n_valid = tab_ref[TAB_NVALID, 0]
    slot = lax.rem(i, N_GATHER_SLOTS)
    gather_priority = lambda r: 1

    @pl.when(i == 0)
    def _build_source_rows():
        for e in range(N_EXPERTS):
            cnt_e = tab_ref[TAB_CNT, RL_E + e]
            first = tab_ref[TAB_OFF, RL_E + e] + cnt_e
            n_pad = lax.rem(tm - lax.rem(cnt_e, tm), tm)

            def pad_body(r, carry, first=first):
                src[first + r] = 0
                return carry
            lax.fori_loop(0, n_pad, pad_body, 0)

        def body(t, carry):
            first_row = t * TOK_TILE_ROWS
            src[pa_ref[t]] = first_row
            src[pb_ref[t]] = first_row
            return carry
        lax.fori_loop(0, n_tokens, body, 0, unroll=8)
        _start_tile_gather(lambda r: src[r], tm, xn_hbm, xbuf.at[0], sem.at[0], gather_priority)
        second = jnp.where(n_valid > 1, tm, 0)
        _start_tile_gather(lambda r: src[second + r], tm, xn_hbm, xbuf.at[1], sem.at[1], gather_priority)

    changed = jnp.logical_or(i == 0, tab_ref[TAB_TILE_EXPERT, i]
                             != tab_ref[TAB_TILE_EXPERT, jnp.maximum(i - 1, 0)])

    @pl.when(jnp.logical_and(changed, i < n_valid))
    def _cast_weights():
        wbf[0] = wg_ref[...].astype(BF16)
        wbf[1] = wu_ref[...].astype(BF16)
        wbf[2] = wd_ref[...].astype(BF16)

    @pl.when(i < n_valid)
    def _compute():
        _wait_tile_gather(tm, xn_hbm, xbuf.at[slot], sem.at[slot])
        x = _load_token_tiles(xbuf.at[slot], tm).astype(BF16)
        hg = _dot(x, wbf[0])
        hu = _dot(x, wbf[1])
        y = _dot((_silu(hg) * hu).astype(BF16), wbf[2])
        base = jnp.where(i + 2 < n_valid, i + 2, 0) * tm
        ahead = lax.rem(i + 2, N_GATHER_SLOTS)
        _start_tile_gather(lambda r: src[base + r], tm, xn_hbm, xbuf.at[ahead], sem.at[ahead],
                           gather_priority)
        _store_token_tiles(ys_ref, y)

    @pl.when(i == n_valid - 1)
    def _drain():
        for k in (1, 2):
            s = lax.rem(i + k, N_GATHER_SLOTS)
            _wait_tile_gather(tm, xn_hbm, xbuf.at[s], sem.at[s])

    @pl.when(i >= n_valid)
    def _pad():
        ys_ref[...] = jnp.zeros_like(ys_ref)


def _expert_ffn(xn_tiles, tab, pos_a, pos_b, n_tiles, wg, wu, wd):
    n = pos_a.shape[0]
    tm = FFN_TM
    d, dff = wg.shape[1], wg.shape[2]
    rows = tm * TOK_TILE_ROWS
    idx = lambda i, tab, pa, pb: (tab[TAB_TILE_EXPERT, i], 0, 0)
    grid_spec = pltpu.PrefetchScalarGridSpec(
        num_scalar_prefetch=3,
        grid=(n_tiles,),
        in_specs=[pl.BlockSpec(memory_space=pl.ANY),
                  pl.BlockSpec((None, d, dff), idx),
                  pl.BlockSpec((None, d, dff), idx),
                  pl.BlockSpec((None, dff, d), idx)],
        out_specs=pl.BlockSpec((rows, LANES), lambda i, tab, pa, pb: (i, 0)),
        scratch_shapes=[pltpu.SMEM((n_tiles * tm,), jnp.int32),
                        pltpu.VMEM((N_GATHER_SLOTS, rows, LANES), F32),
                        pltpu.VMEM((3, d, dff), BF16),
                        pltpu.SemaphoreType.DMA((N_GATHER_SLOTS,))],
    )
    return pl.pallas_call(
        functools.partial(_ffn_kernel, n_tokens=n, tm=tm),
        out_shape=jax.ShapeDtypeStruct((n_tiles * rows, LANES), F32),
        grid_spec=grid_spec,
        compiler_params=pltpu.CompilerParams(
            dimension_semantics=("arbitrary",), vmem_limit_bytes=VMEM_LIMIT),
        name="expert_ffn",
    )(tab, pos_a, pos_b, xn_tiles, wg, wu, wd)


def _positions_kernel(info_ref, cnt_ref, pos_ref, tab_ref, *, tm, chunk):
    lane_i = lax.broadcasted_iota(jnp.int32, (1, LANES), 1)
    lane = lane_i.astype(F32)
    is_expert = jnp.logical_and(lane_i >= RL_E, lane_i < RL_E + N_EXPERTS)
    cnt = jnp.where(is_expert, cnt_ref[...], 0.0)
    padded = jnp.floor((cnt + (tm - 1)) / tm) * tm
    ri = lax.broadcasted_iota(jnp.int32, (LANES, LANES), 0)
    ci = lax.broadcasted_iota(jnp.int32, (LANES, LANES), 1)
    before = jnp.where(ri < ci, 1.0, 0.0).astype(BF16)
    hi, mid, lo = _split3(jnp.broadcast_to(padded, (SUBLANES, LANES)))
    off = ((_dot(hi, before) + _dot(mid, before)) + _dot(lo, before))[0:1, :]
    pick = jnp.where(lax.broadcasted_iota(jnp.int32, (SUBLANES, LANES), 0) == lane_i, 1.0, 0.0).astype(BF16)

    total = jnp.sum(padded, axis=-1, keepdims=True)
    tile_row = lax.broadcasted_iota(jnp.int32, (TAB_LANES, 1), 0).astype(F32) * tm
    ends = off + padded
    done = jnp.logical_and(is_expert, ends <= jnp.minimum(tile_row, total - 1.0))
    te_col = jnp.sum(jnp.where(done, 1.0, 0.0), axis=-1, keepdims=True)
    te_rows = _dot_nt(pick, jnp.where(lane_i == 0, te_col, 0.0).astype(BF16))
    tab_ref[...] = jnp.zeros_like(tab_ref)
    tab_ref[TAB_OFF:TAB_OFF + 1, 0:LANES] = off.astype(jnp.int32)
    tab_ref[TAB_CNT:TAB_CNT + 1, 0:LANES] = cnt.astype(jnp.int32)
    tab_ref[TAB_TILE_EXPERT:TAB_TILE_EXPERT + 1, :] = te_rows[0:1, :].astype(jnp.int32)
    tab_ref[TAB_NVALID:TAB_NVALID + 1, 0:LANES] = jnp.broadcast_to(total / tm, (1, LANES)).astype(jnp.int32)

    n = info_ref.shape[0]
    for c0 in range(0, n, chunk):
        blk = info_ref[c0:c0 + chunk, :]
        lane_a = blk[:, R_EA:R_EA + 1] + RL_E
        lane_b = blk[:, R_EB:R_EB + 1] + RL_E
        pos_a = blk[:, R_RA:R_RA + 1] + jnp.sum(jnp.where(lane == lane_a, off, 0.0), axis=-1, keepdims=True)
        pos_b = blk[:, R_RB:R_RB + 1] + jnp.sum(jnp.where(lane == lane_b, off, 0.0), axis=-1, keepdims=True)
        z_hi, z_mid, z_lo = _split3(jnp.where(lane_i == 0, pos_a, jnp.where(lane_i == 1, pos_b, 0.0)))
        rows = (_dot_nt(pick, z_hi) + _dot_nt(pick, z_mid)) + _dot_nt(pick, z_lo)
        pos_ref[:, c0:c0 + chunk] = rows.astype(jnp.int32)


def _routing_tables(info, counts, n_tiles, tm):
    n = info.shape[0]
    groups = n // LANES
    assert n_tiles <= TAB_LANES
    chunk = LANES * max(g for g in range(1, 65) if groups % g == 0)
    pos, tab = pl.pallas_call(
        functools.partial(_positions_kernel, tm=tm, chunk=chunk),
        out_shape=(jax.ShapeDtypeStruct((SUBLANES, n), jnp.int32),
                   jax.ShapeDtypeStruct((SUBLANES, TAB_LANES), jnp.int32)),
        grid=(1,),
        in_specs=[pl.BlockSpec((n, LANES), lambda i: (0, 0)), pl.BlockSpec((1, LANES), lambda i: (0, 0))],
        out_specs=(pl.BlockSpec((SUBLANES, n), lambda i: (0, 0)),
                   pl.BlockSpec((SUBLANES, TAB_LANES), lambda i: (0, 0))),
        compiler_params=pltpu.CompilerParams(
            dimension_semantics=("arbitrary",), vmem_limit_bytes=VMEM_LIMIT),
        name="positions",
    )(info, counts)
    return tab, pos[0], pos[1]


def _combine_kernel(pa_ref, pb_ref, h_ref, info_ref, ys_hbm, nfin_ref, yp_ref, ysm_ref,
                    buf_a, buf_b, sem, *, n_prompt_tiles):
    i = pl.program_id(0)
    n_steps = pl.num_programs(0)
    tt = h_ref.shape[0]
    slot = lax.rem(i, N_GATHER_SLOTS)

    def start(tile, s):
        base = tile * tt
        _start_tile_gather(lambda r: pa_ref[base + r] * TOK_TILE_ROWS, tt, ys_hbm, buf_a.at[s],
                           sem.at[s], lambda r: 0)
        _start_tile_gather(lambda r: pb_ref[base + r] * TOK_TILE_ROWS, tt, ys_hbm, buf_b.at[s],
                           sem.at[s], lambda r: 1)

    @pl.when(i == 0)
    def _first():
        start(0, 0)
        start(lax.rem(1, n_steps), 1)

    _wait_tile_gather(tt, ys_hbm, buf_a.at[slot], sem.at[slot])
    _wait_tile_gather(tt, ys_hbm, buf_b.at[slot], sem.at[slot])
    info = info_ref[...]
    rows_a = _load_token_tiles(buf_a.at[slot], tt)
    rows_b = _load_token_tiles(buf_b.at[slot], tt)
    x = h_ref[...]
    start(lax.rem(i + 2, n_steps), lax.rem(i + 2, N_GATHER_SLOTS))
    h = x + info[:, R_GA:R_GA + 1] * rows_a + info[:, R_GB:R_GB + 1] * rows_b
    y = h * _rms_scale(h) * nfin_ref[...]

    @pl.when(i < n_prompt_tiles)
    def _prompt():
        yp_ref[...] = y

    @pl.when(i >= n_prompt_tiles)
    def _sample():
        ysm_ref[...] = y

    @pl.when(i == n_steps - 1)
    def _drain():
        for k in (1, 2):
            s = lax.rem(i + k, N_GATHER_SLOTS)
            _wait_tile_gather(tt, ys_hbm, buf_a.at[s], sem.at[s])
            _wait_tile_gather(tt, ys_hbm, buf_b.at[s], sem.at[s])


def _combine(hmid, info, ys, pos_a, pos_b, nfin, n_prompt):
    n, d = hmid.shape
    tt = CHUNK
    n_prompt_tiles = n_prompt // tt
    n_sample = n - n_prompt
    grid_spec = pltpu.PrefetchScalarGridSpec(
        num_scalar_prefetch=2,
        grid=(n // tt,),
        in_specs=[pl.BlockSpec((tt, d), lambda i, pa, pb: (i, 0)),
                  pl.BlockSpec((tt, LANES), lambda i, pa, pb: (i, 0)),
                  pl.BlockSpec(memory_space=pl.ANY),
                  pl.BlockSpec((1, d), lambda i, pa, pb: (0, 0))],
        out_specs=(pl.BlockSpec((tt, d), lambda i, pa, pb: (jnp.minimum(i, n_prompt_tiles - 1), 0)),
                   pl.BlockSpec((tt, d), lambda i, pa, pb: (jnp.maximum(i - n_prompt_tiles, 0), 0))),
        scratch_shapes=[pltpu.VMEM((N_GATHER_SLOTS, tt * TOK_TILE_ROWS, LANES), F32),
                        pltpu.VMEM((N_GATHER_SLOTS, tt * TOK_TILE_ROWS, LANES), F32),
                        pltpu.SemaphoreType.DMA((N_GATHER_SLOTS,))],
    )
    return pl.pallas_call(
        functools.partial(_combine_kernel, n_prompt_tiles=n_prompt_tiles),
        out_shape=(jax.ShapeDtypeStruct((n_prompt, d), F32),
                   jax.ShapeDtypeStruct((n_sample, d), F32)),
        grid_spec=grid_spec,
        compiler_params=pltpu.CompilerParams(
            dimension_semantics=("arbitrary",), vmem_limit_bytes=VMEM_LIMIT),
        name="combine",
    )(pos_a, pos_b, hmid, info, ys, nfin)


def _moe_and_final_norm(hmid, n_prompt, rp, wg, wu, wd, nfin):
    n = hmid.shape[0]
    tm = FFN_TM
    n_tiles = (2 * n + N_EXPERTS * (tm - 1)) // tm
    xn, info, counts = _router(hmid, rp)
    tab, pos_a, pos_b = _routing_tables(info, counts, n_tiles, tm)
    ys = _expert_ffn(xn, tab, pos_a, pos_b, n_tiles, wg, wu, wd)
    return _combine(hmid, info, ys, pos_a, pos_b, nfin, n_prompt)


def kernel(x_prompt, x_sample, state_mlstm_C, state_mlstm_n, state_mlstm_m, state_mlstm_conv, state_ssm, state_ssm_conv, meta_tokens, norm_mix, w_in, conv_a_w, conv_a_b, w_q, w_k, w_v, b_i, b_f, norm_a, conv_b_w, conv_b_b, dt_bias, a_log, d_skip, norm_b, w_out, norm_ffn, w_r1, b_r1, w_r2, b_r2, w_gate, w_up, w_down, norm_final):
    bsz, seq, d = x_prompt.shape
    nb = x_sample.shape[0]
    d_a = H_A * DH_A
    conv_b = H_B * HD_B + 2 * G_B * N_STATE
    assert w_in.shape[0] == 1 and x_sample.shape[1] == 1 and seq % CHUNK == 0 and nb == CHUNK
    mp = _prep_mixer_params(norm_mix, w_in, conv_a_w, conv_a_b, w_q, w_k, w_v, b_i, b_f, norm_a,
                            conv_b_w, conv_b_b, dt_bias, a_log, d_skip, norm_b, w_out)
    rp = _prep_router_params(norm_ffn, w_r1, b_r1, w_r2, b_r2)
    xmeta = jnp.concatenate([jnp.zeros((CHUNK - N_META, d), F32), meta_tokens.astype(F32)], 0)

    hmid, p_c, p_n, p_m, p_ca, p_s, p_cb = _prompt_mixer(x_prompt.astype(F32), xmeta, mp, nb)
    m0 = jnp.pad(state_mlstm_m.reshape(nb, H_A).astype(F32), ((0, 0), (0, LANES - H_A)))
    hmid, s_c, s_n, s_m, s_ca, s_s, s_cb = _sample_mixer(
        x_sample.reshape(nb, d).astype(F32),
        state_mlstm_C.reshape(nb, H_A, DH_A, DH_A).astype(F32),
        state_mlstm_n.reshape(nb, d_a).astype(F32),
        m0,
        state_mlstm_conv.reshape(nb, (CONV_W - 1) * d_a).astype(F32),
        state_ssm.reshape(nb, H_B // 2, 2 * HD_B, N_STATE).astype(F32),
        state_ssm_conv.reshape(nb, (CONV_W - 1) * conv_b).astype(F32),
        mp, hmid, bsz * seq)

    wshape = w_gate.shape[1:]
    y_p, y_s = _moe_and_final_norm(
        hmid, bsz * seq, rp, w_gate.reshape(wshape).astype(F32), w_up.reshape(wshape).astype(F32),
        w_down.reshape(w_down.shape[1:]).astype(F32), norm_final.reshape(1, d).astype(F32))

    return (y_p.reshape(bsz, seq, d), y_s.reshape(nb, 1, d),
            p_c.reshape(1, bsz, H_A, DH_A, DH_A), p_n.reshape(1, bsz, H_A, DH_A),
            p_m[:, 0, :H_A].reshape(1, bsz, H_A), p_ca.reshape(1, bsz, CONV_W - 1, d_a),
            p_s.reshape(1, bsz, H_B, HD_B, N_STATE), p_cb.reshape(1, bsz, CONV_W - 1, conv_b),
            s_c.reshape(1, nb, H_A, DH_A, DH_A), s_n.reshape(1, nb, H_A, DH_A),
            s_m[:, :H_A].reshape(1, nb, H_A), s_ca.reshape(1, nb, CONV_W - 1, d_a),
            s_s.reshape(1, nb, H_B, HD_B, N_STATE), s_cb.reshape(1, nb, CONV_W - 1, conv_b))
```

```python
import functools
import math

import jax
import jax.numpy as jnp
from jax import lax
from jax.experimental import pallas as pl
from jax.experimental.pallas import tpu as pltpu

F32 = jnp.float32
BF16 = jnp.bfloat16

EPS = 1e-6
N_META = 16
CONV_W = 4
CHUNK = 128
H_A = 8
DH_A = 128
H_B = 16
HD_B = 64
N_STATE = 128
G_B = 2
N_EGROUPS = 4
N_EPG = 4
N_EXPERTS = 16
LANES = 128
SUBLANES = 8
CONV_HDR = SUBLANES
VMEM_LIMIT = 56 * 1024 * 1024

L_F = 0
L_DTA = 8
L_I = 24
L_DT = 32

NEG_INF = float("-inf")


def _dot(a, b):
    return jnp.dot(a, b, preferred_element_type=F32)


def _dot_nt(a, b):
    return lax.dot_general(a, b, (((1,), (1,)), ((), ())), preferred_element_type=F32)


def _dot_tn(a, b):
    return lax.dot_general(a, b, (((0,), (0,)), ((), ())), preferred_element_type=F32)


def _split3(x):
    hi = x.astype(BF16)
    r = x - hi.astype(F32)
    mid = r.astype(BF16)
    lo = (r - mid.astype(F32)).astype(BF16)
    return hi, mid, lo


def _silu(x):
    return x * jax.nn.sigmoid(x)


def _softplus_parts(x):
    t = jnp.log1p(jnp.exp(-jnp.abs(x)))
    return jnp.maximum(x, 0.0) + t, jnp.minimum(x, 0.0) - t


def _rms_scale(x):
    return lax.rsqrt(jnp.mean(x * x, axis=-1, keepdims=True) + EPS)


TOK_TILE_ROWS = SUBLANES


def _store_token_tiles(ref, x):
    n = x.shape[0]
    for j in range(TOK_TILE_ROWS):
        ref[pl.ds(j, n, stride=TOK_TILE_ROWS), :] = x[:, j * LANES:(j + 1) * LANES]


def _causal_conv(x, tail, w_ref, b_ref):
    n_tail = tail.shape[0]
    row = lax.broadcasted_iota(jnp.int32, (n_tail, 1), 0)
    acc = w_ref[CONV_W - 1:CONV_W, :] * x + b_ref[...]
    for k in range(1, CONV_W):
        rolled = pltpu.roll(x, k, axis=0)
        head = jnp.where(row < k, pltpu.roll(tail, k, axis=0), rolled[0:n_tail])
        shifted = jnp.concatenate([head, rolled[n_tail:]], axis=0)
        acc = acc + w_ref[CONV_W - 1 - k:CONV_W - k, :] * shifted
    return acc


def _load_token_tiles(ref, n):
    return jnp.concatenate(
        [ref[pl.ds(j, n, stride=TOK_TILE_ROWS), :] for j in range(TOK_TILE_ROWS)], axis=1)


PROMPT_ROWS = 1


def _prompt_rows_kernel(xmeta_ref, xp_ref, nmix_ref, wcat_ref, bsm_ref, alog_ref,
                        cwa_ref, cba_ref, cwb_ref, cbb_ref, wq_ref, wk_ref, wv_ref,
                        na_ref, nb_ref, dsk_ref, wout_ref,
                        hmid_hbm, c_ref, n_ref, m_ref, conva_ref, s_ref, convb_ref,
                        xa_buf, xbc_buf, y_buf, merged, hout, sem, *, seq, n_prompt_rows):
    p = pl.program_id(0)
    c = pl.program_id(1)
    last_p = pl.num_programs(0) - 1
    last_c = pl.num_programs(1) - 1
    T = CHUNK
    RB = xp_ref.shape[0]
    d_a = H_A * DH_A
    d_b = H_B * HD_B
    conv_b = d_b + 2 * G_B * N_STATE

    def out_copy(r, row0):
        return pltpu.make_async_copy(hout.at[r], hmid_hbm.at[pl.ds(row0, T), :], sem.at[r])

    @pl.when(c == 0)
    def _init():
        c_ref[...] = jnp.zeros_like(c_ref)
        n_ref[...] = jnp.zeros_like(n_ref)
        m_ref[...] = jnp.zeros_like(m_ref)
        s_ref[...] = jnp.zeros_like(s_ref)
        xa_buf[...] = jnp.zeros_like(xa_buf)
        xbc_buf[...] = jnp.zeros_like(xbc_buf)

    @pl.when(jnp.logical_and(p == 0, c == 0))
    def _clear_sample_rows():
        hout[0] = jnp.zeros((T, hout.shape[2]), F32)
        cp = out_copy(0, n_prompt_rows)
        cp.start()
        cp.wait()

    row = lax.broadcasted_iota(jnp.int32, (T, 1), 0)
    valid = jnp.logical_or(c > 0, row >= T - N_META)
    xs_in = [jnp.where(c == 0, xmeta_ref[...], xp_ref[r]) for r in range(RB)]
    x2 = jnp.concatenate(xs_in, axis=0)
    hn = (x2 * _rms_scale(x2) * nmix_ref[...]).astype(BF16)

    lane = lax.broadcasted_iota(jnp.int32, (1, LANES), 1)
    lane_f = lane < L_DTA
    lane_dta = jnp.logical_and(lane >= L_DTA, lane < L_I)
    lane_i = jnp.logical_and(lane >= L_I, lane < L_DT)
    lane_dt = jnp.logical_and(lane >= L_DT, lane < L_DT + H_B)
    ri = lax.broadcasted_iota(jnp.int32, (T, T), 0)
    ci = lax.broadcasted_iota(jnp.int32, (T, T), 1)
    causal = ri >= ci
    tri = jnp.where(causal, 1.0, 0.0).astype(BF16)
    a_neg = jnp.where(lane_dta, -jnp.exp(alog_ref[...]), 0.0)
    left = lane < HD_B
    top = lax.broadcasted_iota(jnp.int32, (LANES, 1), 0) < HD_B

    off_small = 2 * d_a + d_b + conv_b
    pre2 = _dot(hn, wcat_ref[:, off_small:]) + bsm_ref[...]
    xa2 = _dot(hn, wcat_ref[:, 0:d_a])

    gcols, grows, xcs, xabs = [], [], [], []

    def gate_tables():
        for r in range(RB):
            pre = pre2[r * T:(r + 1) * T]
            sp, lsig = _softplus_parts(pre)
            to_cum = jnp.where(lane_f, lsig, jnp.where(lane_dta, sp * a_neg, 0.0))
            to_cum = jnp.where(valid, to_cum, 0.0)
            hi, mid, lo = _split3(to_cum)
            cum = _dot(tri, hi) + _dot(tri, mid) + _dot(tri, lo)
            extra = jnp.where(lane_i, jnp.where(valid, pre, NEG_INF),
                              jnp.where(lane_dt, jnp.where(valid, sp, 0.0), 0.0))
            gcol = cum + extra
            gcols.append(gcol)
            grows.append(gcol.T)

    gate_tables()
    for r in range(RB):
        rs = slice(r * T, (r + 1) * T)
        xa = xa2[rs]
        xc = _causal_conv(xa, xa_buf[r], cwa_ref, cba_ref)
        xa_buf[r] = xa[T - CONV_HDR:T, :]
        conva_ref[r] = xa[T - 3:T, :]
        xcs.append(_silu(xc).astype(BF16))
        xabs.append(xa.astype(BF16))

    items = [(r, h) for h in range(H_A) for r in range(RB)]
    hsl = lambda h: slice(h * DH_A, (h + 1) * DH_A)
    m_alls = [m_ref[r] for r in range(RB)]
    m_news = list(m_alls)
    qs, ks, vs, qks, st, dd = {}, {}, {}, {}, {}, {}

    def stage_qkv(it):
        r, h = it
        qs[it] = _dot(xcs[r][:, hsl(h)], wq_ref[h]).astype(BF16)
        ks[it] = _dot(xcs[r][:, hsl(h)], wk_ref[h]) * (DH_A ** -0.5)
        vs[it] = _dot(xabs[r][:, hsl(h)], wv_ref[h]).astype(BF16)

    def stage_qk(it):
        qks[it] = _dot_nt(qs[it], ks[it].astype(BF16))

    pairs_per_group = H_B // G_B // 2
    groups = [(r, g) for g in range(G_B) for r in range(RB)]
    proj, xbcs, bgs, cgs, cbs = {}, [], {}, {}, {}

    def project(name, lo, hi):
        proj[name] = _dot(hn, wcat_ref[:, lo:hi])

    def ssd_inputs():
        for r in range(RB):
            xbc = proj["xbc"][r * T:(r + 1) * T]
            xbc_c = _causal_conv(xbc, xbc_buf[r], cwb_ref, cbb_ref)
            xbc_buf[r] = xbc[T - CONV_HDR:T, :]
            convb_ref[r] = xbc[T - 3:T, :]
            xbcs.append(_silu(xbc_c))
        for r, g in groups:
            bgs[(r, g)] = xbcs[r][:, d_b + g * N_STATE:d_b + (g + 1) * N_STATE].astype(BF16)
            cgs[(r, g)] = xbcs[r][:, d_b + (G_B + g) * N_STATE:d_b + (G_B + g + 1) * N_STATE].astype(BF16)
            cbs[(r, g)] = _dot_nt(cgs[(r, g)], bgs[(r, g)])

    def stage_weights(it):
        r, h = it
        gcol, grow = gcols[r], grows[r]
        b_col = gcol[:, L_F + h:L_F + h + 1]
        i_col = gcol[:, L_I + h:L_I + h + 1]
        b_row = grow[L_F + h:L_F + h + 1, :]
        i_row = grow[L_I + h:L_I + h + 1, :]
        m0 = m_alls[r][:, h:h + 1]
        dmat = jnp.where(causal, b_col - (b_row - i_row), NEG_INF)
        m_inter = b_col + m0
        m = jnp.maximum(m_inter, jnp.max(dmat, axis=-1, keepdims=True))
        w_inter = jnp.exp(m_inter - m)
        s = qks[it] * jnp.exp(dmat - m)
        n0 = n_ref[r, h:h + 1, :]
        den = (jnp.sum(s, axis=-1, keepdims=True)
               + w_inter * jnp.sum(qs[it].astype(F32) * n0, axis=-1, keepdims=True))
        m_last = m[T - 1:T, :]
        b_last = b_col[T - 1:T, :]
        dec = jnp.exp(b_last + m0 - m_last)
        kw = ks[it] * jnp.exp(b_last - b_col + i_col - m_last)
        n_ref[r, h:h + 1, :] = dec * n0 + jnp.sum(kw, axis=0, keepdims=True)
        m_news[r] = jnp.where(lane == h, m_last, m_news[r])
        st[it] = (s.astype(BF16), kw.astype(BF16), w_inter,
                  jnp.maximum(jnp.abs(den), jnp.exp(-m)), dec)

    def stage_readout(it):
        r, h = it
        s_b, kw_b, w_inter, den, dec = st[it]
        c0 = c_ref[r, h]
        num = _dot(s_b, vs[it]) + w_inter * _dot(qs[it], c0.astype(BF16))
        c_ref[r, h] = dec * c0 + _dot_tn(kw_b, vs[it])
        dd[it] = num / den

    def stage_head_out(it):
        r, h = it
        hh = dd[it]
        hh = hh * _rms_scale(hh) * na_ref[:, hsl(h)]
        merged[r * T:(r + 1) * T, hsl(h)] = (
            hh * jax.nn.sigmoid(proj["za"][r * T:(r + 1) * T, hsl(h)])).astype(BF16)

    pairs = [(r, pi) for pi in range(H_B // 2) for r in range(RB)]
    psl = lambda pi: slice(pi * LANES, (pi + 1) * LANES)
    sw = {}

    def stage_decay(pr):
        r, pi = pr
        g = pi // pairs_per_group
        gcol, grow = gcols[r], grows[r]
        xpair = xbcs[r][:, psl(pi)]
        scs, a_cols, w_cols, a_lasts = [], [], [], []
        for j in (2 * pi, 2 * pi + 1):
            a_col = gcol[:, L_DTA + j:L_DTA + j + 1]
            a_row = grow[L_DTA + j:L_DTA + j + 1, :]
            dt_col = gcol[:, L_DT + j:L_DT + j + 1]
            dt_row = grow[L_DT + j:L_DT + j + 1, :]
            decay = jnp.exp(jnp.where(causal, a_col - a_row, NEG_INF))
            scs.append((cbs[(r, g)] * decay * dt_row).astype(BF16))
            a_last = a_col[T - 1:T, :]
            a_cols.append(a_col)
            a_lasts.append(a_last)
            w_cols.append(jnp.exp(a_last - a_col) * dt_col)
        sw[pr] = (scs, xpair.astype(BF16),
                  (xpair * jnp.where(left, w_cols[0], w_cols[1])).astype(BF16),
                  jnp.exp(jnp.where(left, a_cols[0], a_cols[1])),
                  jnp.exp(jnp.where(top, a_lasts[0], a_lasts[1])))
    def stage_pair_out(pr):
        r, pi = pr
        g = pi // pairs_per_group
        scs, xpb, xw, ea, ea_last = sw[pr]
        s0 = s_ref[r, pi]
        y = jnp.where(left, _dot(scs[0], xpb), _dot(scs[1], xpb))
        y = y + ea * _dot_nt(cgs[(r, g)], s0.astype(BF16))
        s_ref[r, pi] = ea_last * s0 + _dot_tn(xw, bgs[(r, g)])
        y = y + dsk_ref[:, psl(pi)] * xbcs[r][:, psl(pi)]
        y_buf[r, :, psl(pi)] = y * _silu(proj["zb"][r * T:(r + 1) * T, psl(pi)])

    def each(stage, seq):
        for e in seq:
            stage(e)

    each(stage_qkv, items)
    project("xbc", 2 * d_a + d_b, off_small)
    each(stage_qk, items)
    project("za", d_a, 2 * d_a)
    ssd_inputs()
    each(stage_weights, items)
    each(stage_readout, items)
    project("zb", 2 * d_a, 2 * d_a + d_b)
    for r in range(RB):
        m_ref[r] = m_news[r]
    each(stage_head_out, items)
    each(stage_decay, pairs)
    each(stage_pair_out, pairs)
    gw = d_b // G_B
    for r in range(RB):
        for g in range(G_B):
            yg = y_buf[r, :, g * gw:(g + 1) * gw]
            merged[r * T:(r + 1) * T, d_a + g * gw:d_a + (g + 1) * gw] = (
                yg * _rms_scale(yg) * nb_ref[:, g * gw:(g + 1) * gw]).astype(BF16)

    @pl.when(c > 0)
    def _out():
        out2 = x2 + _dot(merged[...], wout_ref[...])

        @pl.when(jnp.logical_or(c > 1, p > 0))
        def _wait_previous():
            for r in range(RB):
                out_copy(r, 0).wait()

        for r in range(RB):
            hout[r] = out2[r * T:(r + 1) * T]
            out_copy(r, (p * RB + r) * seq + (c - 1) * T).start()

        @pl.when(jnp.logical_and(p == last_p, c == last_c))
        def _drain():
            for r in range(RB):
                out_copy(r, 0).wait()


def _const_spec(shape):
    nd = len(shape)
    return pl.BlockSpec(shape, lambda b, c, _nd=nd: (0,) * _nd)


def _prompt_mixer(x_prompt, xmeta, p, n_extra_rows):
    bsz, seq, d = x_prompt.shape
    assert n_extra_rows == CHUNK and seq % CHUNK == 0
    n_chunks = seq // CHUNK + 1
    cps = seq // CHUNK
    d_a = H_A * DH_A
    conv_b = H_B * HD_B + 2 * G_B * N_STATE
    consts = [p["nmix"], p["wcat"], p["bsm"], p["alog"], p["cwa"], p["cba"], p["cwb"], p["cbb"],
              p["wq"], p["wk"], p["wv"], p["na"], p["nb"], p["dsk"], p["wout"]]
    rb = PROMPT_ROWS
    assert bsz % rb == 0
    in_specs = [_const_spec(xmeta.shape),
                pl.BlockSpec((rb, CHUNK, d), lambda b, c: (b, jnp.maximum(c - 1, 0), 0))]
    in_specs += [_const_spec(a.shape) for a in consts]
    out_shape = (
        jax.ShapeDtypeStruct((bsz * seq + n_extra_rows, d), F32),
        jax.ShapeDtypeStruct((bsz, H_A, DH_A, DH_A), F32),
        jax.ShapeDtypeStruct((bsz, H_A, DH_A), F32),
        jax.ShapeDtypeStruct((bsz, 1, LANES), F32),
        jax.ShapeDtypeStruct((bsz, CONV_W - 1, d_a), F32),
        jax.ShapeDtypeStruct((bsz, H_B // 2, 2 * HD_B, N_STATE), F32),
        jax.ShapeDtypeStruct((bsz, CONV_W - 1, conv_b), F32),
    )
    out_specs = (
        pl.BlockSpec(memory_space=pl.ANY),
        pl.BlockSpec((rb, H_A, DH_A, DH_A), lambda b, c: (b, 0, 0, 0)),
        pl.BlockSpec((rb, H_A, DH_A), lambda b, c: (b, 0, 0)),
        pl.BlockSpec((rb, 1, LANES), lambda b, c: (b, 0, 0)),
        pl.BlockSpec((rb, CONV_W - 1, d_a), lambda b, c: (b, 0, 0)),
        pl.BlockSpec((rb, H_B // 2, 2 * HD_B, N_STATE), lambda b, c: (b, 0, 0, 0)),
        pl.BlockSpec((rb, CONV_W - 1, conv_b), lambda b, c: (b, 0, 0)),
    )
    return pl.pallas_call(
        functools.partial(_prompt_rows_kernel, seq=seq, n_prompt_rows=bsz * seq),
        out_shape=out_shape,
        grid=(bsz // rb, n_chunks),
        in_specs=in_specs,
        out_specs=out_specs,
        scratch_shapes=[
            pltpu.VMEM((rb, CONV_HDR, d_a), F32),
            pltpu.VMEM((rb, CONV_HDR, conv_b), F32),
            pltpu.VMEM((rb, CHUNK, H_B * HD_B), F32),
            pltpu.VMEM((rb * CHUNK, d_a + H_B * HD_B), BF16),
            pltpu.VMEM((rb, CHUNK, d), F32),
            pltpu.SemaphoreType.DMA((rb,)),
        ],
        compiler_params=pltpu.CompilerParams(
            dimension_semantics=("arbitrary", "arbitrary"), vmem_limit_bytes=VMEM_LIMIT),
        name="prompt_mixer",
    )(xmeta, x_prompt, *consts)


def _regroup_w_in_kernel(w_ref, o_ref):
    d_a = H_A * DH_A
    d_b = H_B * HD_B
    conv_b = d_b + 2 * G_B * N_STATE
    o_i = 2 * d_a
    o_f = o_i + H_A
    o_zb = o_f + H_A
    o_xbc = o_zb + d_b
    o_dt = o_xbc + conv_b
    rows = w_ref.shape[0]
    o_ref[:, 0:2 * d_a] = w_ref[:, 0:2 * d_a].astype(BF16)
    o_ref[:, 2 * d_a:2 * d_a + d_b] = w_ref[:, o_zb:o_zb + d_b].astype(BF16)
    o_ref[:, 2 * d_a + d_b:2 * d_a + d_b + conv_b] = w_ref[:, o_xbc:o_xbc + conv_b].astype(BF16)
    small = jnp.concatenate(
        [w_ref[:, o_f:o_f + H_A], w_ref[:, o_dt:o_dt + H_B], w_ref[:, o_i:o_i + H_A],
         w_ref[:, o_dt:o_dt + H_B], jnp.zeros((rows, LANES - (L_DT + H_B)), F32)], axis=1)
    o_ref[:, 2 * d_a + d_b + conv_b:] = small.astype(BF16)


def _prep_mixer_params(norm_mix, w_in, conv_a_w, conv_a_b, w_q, w_k, w_v, b_i, b_f, norm_a,
                       conv_b_w, conv_b_b, dt_bias, a_log, d_skip, norm_b, w_out):
    d_a = H_A * DH_A
    d_b = H_B * HD_B
    conv_b = d_b + 2 * G_B * N_STATE
    d_model, d_in = w_in.shape[1], w_in.shape[2]
    n_cols = 2 * d_a + d_b + conv_b + LANES
    rows = 256
    assert w_in.shape[0] == 1 and d_model % rows == 0
    wcat = pl.pallas_call(
        _regroup_w_in_kernel,
        out_shape=jax.ShapeDtypeStruct((d_model, n_cols), BF16),
        grid=(d_model // rows,),
        in_specs=[pl.BlockSpec((None, rows, d_in), lambda i: (0, i, 0))],
        out_specs=pl.BlockSpec((rows, n_cols), lambda i: (i, 0)),
        compiler_params=pltpu.CompilerParams(
            dimension_semantics=("arbitrary",), vmem_limit_bytes=VMEM_LIMIT),
        name="regroup_w_in",
    )(w_in.astype(F32))

    def lanes(parts):
        pieces, at = [], 0
        for off, a in parts:
            pieces += [jnp.zeros((1, off - at), F32), a.astype(F32)]
            at = off + a.shape[1]
        return jnp.concatenate(pieces + [jnp.zeros((1, LANES - at), F32)], axis=1)

    return dict(
        nmix=norm_mix.reshape(1, -1).astype(F32),
        wcat=wcat,
        bsm=lanes([(L_F, b_f), (L_DTA, dt_bias), (L_I, b_i), (L_DT, dt_bias)]),
        alog=lanes([(L_DTA, a_log)]),
        cwa=conv_a_w.reshape(CONV_W, d_a).astype(F32), cba=conv_a_b.reshape(1, d_a).astype(F32),
        cwb=conv_b_w.reshape(CONV_W, conv_b).astype(F32), cbb=conv_b_b.reshape(1, conv_b).astype(F32),
        wq=w_q.reshape(H_A, DH_A, DH_A).astype(BF16), wk=w_k.reshape(H_A, DH_A, DH_A).astype(BF16),
        wv=w_v.reshape(H_A, DH_A, DH_A).astype(BF16),
        na=norm_a.reshape(1, d_a).astype(F32), nb=norm_b.reshape(1, d_b).astype(F32),
        dsk=jnp.repeat(d_skip.reshape(H_B).astype(F32), HD_B)[None, :],
        wout=w_out.reshape(d_a + d_b, -1).astype(BF16),
    )


SAMPLE_BLOCK = 8


def _expand_lanes(vals, first_lane, n_heads, width):
    r = lax.broadcasted_iota(jnp.int32, (LANES, n_heads * width), 0) - first_lane
    c = lax.broadcasted_iota(jnp.int32, (LANES, n_heads * width), 1)
    sel = jnp.logical_and(c >= r * width, c < (r + 1) * width)
    e = jnp.where(sel, 1.0, 0.0).astype(BF16)
    hi, mid, lo = _split3(vals)
    return (_dot(hi, e) + _dot(mid, e)) + _dot(lo, e)


def _sample_pre_kernel(x_ref, nmix_ref, wcat_ref, bsm_ref, alog_ref, cwa_ref, cba_ref, cwb_ref, cbb_ref,
                       wq_ref, wk_ref, wv_ref, dsk_ref, conva_ref, convb_ref, n0_ref, m0_ref,
                       conva_out, convb_out, n1_out, m1_out, g_out, qt_out, kwt_out, xwt_out,
                       v_out, bc_out, a1_out, w1_out, den_out, y1_out, ea_out, zbs_out, zas_out):
    d_a = H_A * DH_A
    d_b = H_B * HD_B
    conv_b = d_b + 2 * G_B * N_STATE
    shift_i = LANES - (L_I - L_F)
    x = x_ref[...]
    hn = (x * _rms_scale(x) * nmix_ref[...]).astype(BF16)
    lane = lax.broadcasted_iota(jnp.int32, (1, LANES), 1)
    lane_f = lane < L_DTA
    lane_dta = jnp.logical_and(lane >= L_DTA, lane < L_I)
    pre = _dot(hn, wcat_ref[:, 2 * d_a + d_b + conv_b:]) + bsm_ref[...]
    sp, lsig = _softplus_parts(pre)
    a_neg = jnp.where(lane_dta, -jnp.exp(alog_ref[...]), 0.0)
    pre_al = pltpu.roll(pre, shift_i, axis=1)
    sp_al = pltpu.roll(sp, shift_i, axis=1)
    m_inter = lsig + m0_ref[...]
    m = jnp.maximum(m_inter, pre_al)
    w_inter = jnp.exp(m_inter - m)
    sfac = jnp.exp(pre_al - m)
    ea = jnp.exp(sp * a_neg)
    dt = sp_al

    xa = _dot(hn, wcat_ref[:, 0:d_a])
    xc = (cwa_ref[0:1, :] * conva_ref[:, 0:d_a] + cwa_ref[1:2, :] * conva_ref[:, d_a:2 * d_a]
          + cwa_ref[2:3, :] * conva_ref[:, 2 * d_a:3 * d_a] + cwa_ref[3:4, :] * xa + cba_ref[...])
    conva_out[:, 0:2 * d_a] = conva_ref[:, d_a:3 * d_a]
    conva_out[:, 2 * d_a:3 * d_a] = xa
    xc = _silu(xc).astype(BF16)
    xab = xa.astype(BF16)
    sf_e = _expand_lanes(sfac, L_F, H_A, DH_A)
    w_e = _expand_lanes(w_inter, L_F, H_A, DH_A)
    qk8 = jnp.zeros((x.shape[0], LANES), F32)
    qn8 = jnp.zeros((x.shape[0], LANES), F32)
    for h in range(H_A):
        sl = slice(h * DH_A, (h + 1) * DH_A)
        q = _dot(xc[:, sl], wq_ref[h])
        k = _dot(xc[:, sl], wk_ref[h]) * (DH_A ** -0.5)
        v = _dot(xab[:, sl], wv_ref[h])
        kw = k * sf_e[:, sl]
        qk8 = jnp.where(lane == h, jnp.sum(q * k, axis=-1, keepdims=True), qk8)
        qn8 = jnp.where(lane == h, jnp.sum(q * n0_ref[:, sl], axis=-1, keepdims=True), qn8)
        n1_out[:, sl] = w_e[:, sl] * n0_ref[:, sl] + kw
        v_out[:, sl] = v
        qt_out[h] = q.T
        kwt_out[h] = kw.T
    s8 = qk8 * sfac
    a1_out[...] = _expand_lanes(s8, L_F, H_A, DH_A) * v_out[...]
    w1_out[...] = w_e
    den_out[...] = jnp.maximum(jnp.abs(_expand_lanes(s8 + w_inter * qn8, L_F, H_A, DH_A)),
                               jnp.exp(-_expand_lanes(m, L_F, H_A, DH_A)))
    m1_out[...] = m
    g_out[...] = jnp.where(lane_f, w_inter, jnp.where(lane_dta, ea, 0.0))
    zas_out[...] = jax.nn.sigmoid(_dot(hn, wcat_ref[:, d_a:2 * d_a]))

    off_xbc = 2 * d_a + d_b
    xbc = _dot(hn, wcat_ref[:, off_xbc:off_xbc + conv_b])
    xbc_c = (cwb_ref[0:1, :] * convb_ref[:, 0:conv_b] + cwb_ref[1:2, :] * convb_ref[:, conv_b:2 * conv_b]
             + cwb_ref[2:3, :] * convb_ref[:, 2 * conv_b:3 * conv_b] + cwb_ref[3:4, :] * xbc + cbb_ref[...])
    convb_out[:, 0:2 * conv_b] = convb_ref[:, conv_b:3 * conv_b]
    convb_out[:, 2 * conv_b:3 * conv_b] = xbc
    xbc_c = _silu(xbc_c)
    xs = xbc_c[:, 0:d_b]
    bc = xbc_c[:, d_b:conv_b]
    bc_out[...] = bc
    heads_per_group = H_B // G_B
    cbl = jnp.zeros((x.shape[0], LANES), F32)
    for g in range(G_B):
        cb_g = jnp.sum(bc[:, g * N_STATE:(g + 1) * N_STATE]
                       * bc[:, (G_B + g) * N_STATE:(G_B + g + 1) * N_STATE], axis=-1, keepdims=True)
        in_g = jnp.logical_and(lane >= L_DTA + g * heads_per_group,
                               lane < L_DTA + (g + 1) * heads_per_group)
        cbl = jnp.where(in_g, cb_g, cbl)
    dt_e = _expand_lanes(dt, L_DTA, H_B, HD_B)
    y1_out[...] = _expand_lanes(cbl * dt, L_DTA, H_B, HD_B) * xs + dsk_ref[...] * xs
    ea_out[...] = _expand_lanes(ea, L_DTA, H_B, HD_B)
    zbs_out[...] = _silu(_dot(hn, wcat_ref[:, 2 * d_a:2 * d_a + d_b]))
    xw = xs * dt_e
    for pi in range(H_B // 2):
        xwt_out[pi] = xw[:, pi * LANES:(pi + 1) * LANES].T


def _sample_state_kernel(g_ref, c0_ref, s0_ref, qt_ref, kwt_ref, xwt_ref, v_ref, bc_ref,
                         c1_ref, s1_ref, qc_ref, ysi_ref):
    i = pl.program_id(0)
    bb = c0_ref.shape[0]
    shift = lax.rem(LANES - lax.rem(i * bb, LANES), LANES)
    lane = lax.broadcasted_iota(jnp.int32, (1, LANES), 1)
    top = lax.broadcasted_iota(jnp.int32, (LANES, 1), 0) < HD_B
    heads_per_group = H_B // G_B
    for h in range(H_A):
        sl = slice(h * DH_A, (h + 1) * DH_A)
        qt = pltpu.roll(qt_ref[h], shift, axis=1)
        kwt = pltpu.roll(kwt_ref[h], shift, axis=1)
        for r in range(bb):
            b = i * bb + r
            c0 = c0_ref[r, h]
            dec = g_ref[b, L_F + h]
            v_row = v_ref[r:r + 1, sl]
            qc_ref[r:r + 1, sl] = jnp.sum(c0 * qt[:, r:r + 1], axis=0, keepdims=True)
            c1_ref[r, h] = dec * c0 + kwt[:, r:r + 1] * v_row
    for pi in range(H_B // 2):
        g = (2 * pi) // heads_per_group
        sl = slice(pi * LANES, (pi + 1) * LANES)
        xwt = pltpu.roll(xwt_ref[pi], shift, axis=1)
        acc = jnp.zeros((LANES, LANES), F32)
        for r in range(bb):
            b = i * bb + r
            s0 = s0_ref[r, pi]
            b_row = bc_ref[r:r + 1, g * N_STATE:(g + 1) * N_STATE]
            c_row = bc_ref[r:r + 1, (G_B + g) * N_STATE:(G_B + g + 1) * N_STATE]
            col = jnp.sum(s0 * c_row, axis=-1, keepdims=True)
            acc = jnp.where(lane == r, col, acc)
            ea_rows = jnp.where(top, g_ref[b, L_DTA + 2 * pi], g_ref[b, L_DTA + 2 * pi + 1])
            s1_ref[r, pi] = ea_rows * s0 + xwt[:, r:r + 1] * b_row
        ysi_ref[:, sl] = acc.T[0:bb, :]


def _sample_post_kernel(a1_ref, w1_ref, den_ref, y1_ref, ea_ref, zbs_ref, zas_ref, x_ref, qc_ref, ysi_ref,
                        na_ref, nb_ref, wout_ref, hall_ref, hmid_ref, merged):
    del hall_ref
    d_a = H_A * DH_A
    d_b = H_B * HD_B
    hh = (a1_ref[...] + w1_ref[...] * qc_ref[...]) / den_ref[...]
    for h in range(H_A):
        sl = slice(h * DH_A, (h + 1) * DH_A)
        hs = hh[:, sl]
        merged[:, sl] = (hs * _rms_scale(hs) * na_ref[:, sl] * zas_ref[:, sl]).astype(BF16)
    y = (y1_ref[...] + ea_ref[...] * ysi_ref[...]) * zbs_ref[...]
    gw = d_b // G_B
    for g in range(G_B):
        yg = y[:, g * gw:(g + 1) * gw]
        merged[:, d_a + g * gw:d_a + (g + 1) * gw] = (
            yg * _rms_scale(yg) * nb_ref[:, g * gw:(g + 1) * gw]).astype(BF16)
    hmid_ref[...] = x_ref[...] + _dot(merged[...], wout_ref[...])


def _vmem_specs(arrays):
    return [pl.BlockSpec(a.shape, lambda *_, _nd=a.ndim: (0,) * _nd) for a in arrays]


def _sample_mixer(x, c0, n0, m0, conva, s0, convb, p, hmid_all, row_offset):
    nb, d = x.shape
    d_a = H_A * DH_A
    d_b = H_B * HD_B
    conv_b = d_b + 2 * G_B * N_STATE
    row = lambda w: jax.ShapeDtypeStruct((nb, w), F32)
    tile = lambda k: jax.ShapeDtypeStruct((k, LANES, nb), F32)
    pre_in = [x, p["nmix"], p["wcat"], p["bsm"], p["alog"], p["cwa"], p["cba"], p["cwb"], p["cbb"],
              p["wq"], p["wk"], p["wv"], p["dsk"], conva, convb, n0, m0]
    pre_out_shape = (row(3 * d_a), row(3 * conv_b), row(d_a), row(LANES), row(LANES),
                     tile(H_A), tile(H_A), tile(H_B // 2), row(d_a), row(2 * G_B * N_STATE),
                     row(d_a), row(d_a), row(d_a), row(d_b), row(d_b), row(d_b), row(d_a))
    (conva1, convb1, n1, m1, g8, qt, kwt, xwt, v, bc, a1, w1, den, y1, ea_e, zbs, zas) = pl.pallas_call(
        _sample_pre_kernel,
        out_shape=pre_out_shape,
        grid=(1,),
        in_specs=_vmem_specs(pre_in),
        out_specs=tuple(pl.BlockSpec(s.shape, lambda i, _nd=len(s.shape): (0,) * _nd) for s in pre_out_shape),
        compiler_params=pltpu.CompilerParams(
            dimension_semantics=("arbitrary",), vmem_limit_bytes=VMEM_LIMIT),
        name="sample_pre",
    )(*pre_in)

    bb = SAMPLE_BLOCK
    const3 = lambda k: pl.BlockSpec((k, LANES, nb), lambda i, g: (0, 0, 0))
    state_grid = pltpu.PrefetchScalarGridSpec(
        num_scalar_prefetch=1,
        grid=(nb // bb,),
        in_specs=[pl.BlockSpec((bb, H_A, DH_A, DH_A), lambda i, g: (i, 0, 0, 0)),
                  pl.BlockSpec((bb, H_B // 2, 2 * HD_B, N_STATE), lambda i, g: (i, 0, 0, 0)),
                  const3(H_A), const3(H_A), const3(H_B // 2),
                  pl.BlockSpec((bb, d_a), lambda i, g: (i, 0)),
                  pl.BlockSpec((bb, 2 * G_B * N_STATE), lambda i, g: (i, 0))],
        out_specs=(pl.BlockSpec((bb, H_A, DH_A, DH_A), lambda i, g: (i, 0, 0, 0)),
                   pl.BlockSpec((bb, H_B // 2, 2 * HD_B, N_STATE), lambda i, g: (i, 0, 0, 0)),
                   pl.BlockSpec((bb, d_a), lambda i, g: (i, 0)),
                   pl.BlockSpec((bb, d_b), lambda i, g: (i, 0))),
    )
    c1, s1, qc, ysi = pl.pallas_call(
        _sample_state_kernel,
        out_shape=(jax.ShapeDtypeStruct(c0.shape, F32), jax.ShapeDtypeStruct(s0.shape, F32),
                   row(d_a), row(d_b)),
        grid_spec=state_grid,
        compiler_params=pltpu.CompilerParams(
            dimension_semantics=("arbitrary",), vmem_limit_bytes=VMEM_LIMIT),
        name="sample_state",
    )(g8, c0, s0, qt, kwt, xwt, v, bc)

    post_in = [a1, w1, den, y1, ea_e, zbs, zas, x, qc, ysi, p["na"], p["nb"], p["wout"]]
    hmid_all = pl.pallas_call(
        _sample_post_kernel,
        out_shape=jax.ShapeDtypeStruct(hmid_all.shape, F32),
        grid=(1,),
        in_specs=_vmem_specs(post_in) + [pl.BlockSpec(memory_space=pl.ANY)],
        out_specs=pl.BlockSpec((nb, d), lambda i: (row_offset // nb, 0)),
        scratch_shapes=[pltpu.VMEM((nb, d_a + d_b), BF16)],
        input_output_aliases={len(post_in): 0},
        compiler_params=pltpu.CompilerParams(
            dimension_semantics=("arbitrary",), vmem_limit_bytes=VMEM_LIMIT),
        name="sample_post",
    )(*post_in, hmid_all)
    return hmid_all, c1, n1, m1, conva1, s1, convb1


R_EA, R_EB, R_RA, R_RB, R_GA, R_GB = 0, 1, 2, 3, 4, 5
RL_E = N_EGROUPS


def _router_kernel(h_ref, nf_ref, whi_ref, wmid_ref, br_ref, xn_ref, info_ref, cnt_ref, carry):
    i = pl.program_id(0)
    tr = h_ref.shape[0]

    @pl.when(i == 0)
    def _init():
        carry[...] = jnp.zeros_like(carry)

    h = h_ref[...]
    xn = h * _rms_scale(h) * nf_ref[...]
    _store_token_tiles(xn_ref, xn)
    x_hi, x_mid, _ = _split3(xn)
    logits = (_dot(x_hi, whi_ref[...]) + _dot(x_hi, wmid_ref[...]) + _dot(x_mid, whi_ref[...])
              + br_ref[...])
    lane_i = lax.broadcasted_iota(jnp.int32, (1, LANES), 1)
    lane = lane_i.astype(F32)
    big = float(LANES)

    def first_lane_of(cond):
        return jnp.min(jnp.where(cond, lane, big), axis=-1, keepdims=True)

    l1 = jnp.where(lane_i < N_EGROUPS, logits, NEG_INF)
    e1 = jnp.exp(l1 - jnp.max(l1, axis=-1, keepdims=True))
    p1 = e1 / jnp.sum(e1, axis=-1, keepdims=True)
    gp = jnp.max(p1, axis=-1, keepdims=True)
    gidx = first_lane_of(p1 == gp)
    lo = RL_E + N_EPG * gidx
    l2 = jnp.where(jnp.logical_and(lane >= lo, lane < lo + N_EPG), logits, NEG_INF)
    va = jnp.max(l2, axis=-1, keepdims=True)
    ia = first_lane_of(l2 == va)
    l2b = jnp.where(lane == ia, NEG_INF, l2)
    vb = jnp.max(l2b, axis=-1, keepdims=True)
    ib = first_lane_of(l2b == vb)
    eb = jnp.exp(vb - va)
    wa = 1.0 / (1.0 + eb)
    wb = eb / (1.0 + eb)

    is_a = lane == ia
    is_b = lane == ib
    onehot = jnp.where(jnp.logical_or(is_a, is_b), 1.0, 0.0)
    ri = lax.broadcasted_iota(jnp.int32, (tr, tr), 0)
    ci = lax.broadcasted_iota(jnp.int32, (tr, tr), 1)
    tri = jnp.where(ri >= ci, 1.0, 0.0).astype(BF16)
    incl = _dot(tri, onehot.astype(BF16))
    excl = incl - onehot + carry[...]
    rank_a = jnp.sum(jnp.where(is_a, excl, 0.0), axis=-1, keepdims=True)
    rank_b = jnp.sum(jnp.where(is_b, excl, 0.0), axis=-1, keepdims=True)
    carry[...] = carry[...] + incl[tr - 1:tr, :]
    cnt_ref[...] = carry[...]

    info = jnp.where(lane_i == R_EA, ia - RL_E, 0.0)
    info = jnp.where(lane_i == R_EB, ib - RL_E, info)
    info = jnp.where(lane_i == R_RA, rank_a, info)
    info = jnp.where(lane_i == R_RB, rank_b, info)
    info = jnp.where(lane_i == R_GA, gp * wa, info)
    info = jnp.where(lane_i == R_GB, gp * wb, info)
    info_ref[...] = info


def _row_tile(n, candidates):
    for t in candidates:
        if n % t == 0:
            return t
    raise ValueError(f"no row tile for {n} rows among {candidates}")


def _router(hmid, rp):
    n, d = hmid.shape
    assert d == TOK_TILE_ROWS * LANES
    tr = _row_tile(n, (512, 384, 256, 128))
    return pl.pallas_call(
        _router_kernel,
        out_shape=(jax.ShapeDtypeStruct((n * TOK_TILE_ROWS, LANES), F32),
                   jax.ShapeDtypeStruct((n, LANES), F32),
                   jax.ShapeDtypeStruct((1, LANES), F32)),
        grid=(n // tr,),
        in_specs=[pl.BlockSpec((tr, d), lambda i: (i, 0)),
                  pl.BlockSpec((1, d), lambda i: (0, 0)),
                  pl.BlockSpec((d, LANES), lambda i: (0, 0)),
                  pl.BlockSpec((d, LANES), lambda i: (0, 0)),
                  pl.BlockSpec((1, LANES), lambda i: (0, 0))],
        out_specs=(pl.BlockSpec((tr * TOK_TILE_ROWS, LANES), lambda i: (i, 0)),
                   pl.BlockSpec((tr, LANES), lambda i: (i, 0)),
                   pl.BlockSpec((1, LANES), lambda i: (0, 0))),
        scratch_shapes=[pltpu.VMEM((1, LANES), F32)],
        compiler_params=pltpu.CompilerParams(
            dimension_semantics=("arbitrary",), vmem_limit_bytes=VMEM_LIMIT),
        name="router",
    )(hmid, rp["nf"], rp["whi"], rp["wmid"], rp["br"])


def _prep_router_params(norm_ffn, w_r1, b_r1, w_r2, b_r2):
    d = w_r1.shape[1]
    w = jnp.concatenate([w_r1.reshape(d, N_EGROUPS).astype(F32), w_r2.reshape(d, N_EXPERTS).astype(F32),
                         jnp.zeros((d, LANES - RL_E - N_EXPERTS), F32)], axis=1)
    whi = w.astype(BF16)
    wmid = (w - whi.astype(F32)).astype(BF16)
    br = jnp.concatenate([b_r1.reshape(1, N_EGROUPS).astype(F32), b_r2.reshape(1, N_EXPERTS).astype(F32),
                          jnp.zeros((1, LANES - RL_E - N_EXPERTS), F32)], axis=1)
    return dict(nf=norm_ffn.reshape(1, d).astype(F32), whi=whi, wmid=wmid, br=br)


FFN_TM = 256
N_GATHER_SLOTS = 3


def _start_tile_gather(first_row_of, n_rows, src_hbm, dst, sem, priority_of, row0=0):
    for r in range(row0, n_rows):
        start = pl.multiple_of(first_row_of(r), TOK_TILE_ROWS)
        pltpu.make_async_copy(src_hbm.at[pl.ds(start, TOK_TILE_ROWS), :],
                              dst.at[pl.ds(r * TOK_TILE_ROWS, TOK_TILE_ROWS), :],
                              sem).start(priority=priority_of(r))


def _wait_tile_gather(n_rows, src_hbm, dst, sem):
    pltpu.make_async_copy(src_hbm.at[pl.ds(0, n_rows * TOK_TILE_ROWS), :], dst, sem).wait()


TAB_OFF, TAB_CNT, TAB_TILE_EXPERT, TAB_NVALID = 0, 1, 2, 3
TAB_LANES = 2 * LANES


def _ffn_kernel(tab_ref, pa_ref, pb_ref, xn_hbm, wg_ref, wu_ref, wd_ref, ys_ref,
                src, xbuf, wbf, sem, *, n_tokens, tm):
    i = pl.program_id(0)
    n_valid = tab_ref[TAB_NVALID, 0]
    slot = lax.rem(i, N_GATHER_SLOTS)
    gather_priority = lambda r: 1

    @pl.when(i == 0)
    def _build_source_rows():
        for e in range(N_EXPERTS):
            cnt_e = tab_ref[TAB_CNT, RL_E + e]
            first = tab_ref[TAB_OFF, RL_E + e] + cnt_e
            n_pad = lax.rem(tm - lax.rem(cnt_e, tm), tm)

            def pad_body(r, carry, first=first):
                src[first + r] = 0
                return carry
            lax.fori_loop(0, n_pad, pad_body, 0)

        def body(t, carry):
            first_row = t * TOK_TILE_ROWS
            src[pa_ref[t]] = first_row
            src[pb_ref[t]] = first_row
            return carry
        lax.fori_loop(0, n_tokens, body, 0, unroll=8)
        _start_tile_gather(lambda r: src[r], tm, xn_hbm, xbuf.at[0], sem.at[0], gather_priority)
        second = jnp.where(n_valid > 1, tm, 0)
        _start_tile_gather(lambda r: src[second + r], tm, xn_hbm, xbuf.at[1], sem.at[1], gather_priority)

    changed = jnp.logical_or(i == 0, tab_ref[TAB_TILE_EXPERT, i]
                             != tab_ref[TAB_TILE_EXPERT, jnp.maximum(i - 1, 0)])

    @pl.when(jnp.logical_and(changed, i < n_valid))
    def _cast_weights():
        wbf[0] = wg_ref[...].astype(BF16)
        wbf[1] = wu_ref[...].astype(BF16)
        wbf[2] = wd_ref[...].astype(BF16)

    @pl.when(i < n_valid)
    def _compute():
        _wait_tile_gather(tm, xn_hbm, xbuf.at[slot], sem.at[slot])
        x = _load_token_tiles(xbuf.at[slot], tm).astype(BF16)
        base = jnp.where(i + 2 < n_valid, i + 2, 0) * tm
        ahead = lax.rem(i + 2, N_GATHER_SLOTS)

        def gather_rows(lo, hi):
            _start_tile_gather(lambda r: src[base + r], hi, xn_hbm, xbuf.at[ahead], sem.at[ahead],
                               gather_priority, row0=lo)

        hg = _dot(x, wbf[0])
        gather_rows(0, tm // 3)
        hu = _dot(x, wbf[1])
        gather_rows(tm // 3, 2 * tm // 3)
        y = _dot((_silu(hg) * hu).astype(BF16), wbf[2])
        gather_rows(2 * tm // 3, tm)
        _store_token_tiles(ys_ref, y)

    @pl.when(i == n_valid - 1)
    def _drain():
        for k in (1, 2):
            s = lax.rem(i + k, N_GATHER_SLOTS)
            _wait_tile_gather(tm, xn_hbm, xbuf.at[s], sem.at[s])

    @pl.when(i >= n_valid)
    def _pad():
        ys_ref[...] = jnp.zeros_like(ys_ref)


def _expert_ffn(xn_tiles, tab, pos_a, pos_b, n_tiles, wg, wu, wd):
    n = pos_a.shape[0]
    tm = FFN_TM
    d, dff = wg.shape[1], wg.shape[2]
    rows = tm * TOK_TILE_ROWS
    idx = lambda i, tab, pa, pb: (tab[TAB_TILE_EXPERT, i], 0, 0)
    grid_spec = pltpu.PrefetchScalarGridSpec(
        num_scalar_prefetch=3,
        grid=(n_tiles,),
        in_specs=[pl.BlockSpec(memory_space=pl.ANY),
                  pl.BlockSpec((None, d, dff), idx),
                  pl.BlockSpec((None, d, dff), idx),
                  pl.BlockSpec((None, dff, d), idx)],
        out_specs=pl.BlockSpec((rows, LANES), lambda i, tab, pa, pb: (i, 0)),
        scratch_shapes=[pltpu.SMEM((n_tiles * tm,), jnp.int32),
                        pltpu.VMEM((N_GATHER_SLOTS, rows, LANES), F32),
                        pltpu.VMEM((3, d, dff), BF16),
                        pltpu.SemaphoreType.DMA((N_GATHER_SLOTS,))],
    )
    return pl.pallas_call(
        functools.partial(_ffn_kernel, n_tokens=n, tm=tm),
        out_shape=jax.ShapeDtypeStruct((n_tiles * rows, LANES), F32),
        grid_spec=grid_spec,
        compiler_params=pltpu.CompilerParams(
            dimension_semantics=("arbitrary",), vmem_limit_bytes=VMEM_LIMIT),
        name="expert_ffn",
    )(tab, pos_a, pos_b, xn_tiles, wg, wu, wd)


def _positions_kernel(info_ref, cnt_ref, pos_ref, tab_ref, *, tm, chunk):
    lane_i = lax.broadcasted_iota(jnp.int32, (1, LANES), 1)
    lane = lane_i.astype(F32)
    is_expert = jnp.logical_and(lane_i >= RL_E, lane_i < RL_E + N_EXPERTS)
    cnt = jnp.where(is_expert, cnt_ref[...], 0.0)
    padded = jnp.floor((cnt + (tm - 1)) / tm) * tm
    ri = lax.broadcasted_iota(jnp.int32, (LANES, LANES), 0)
    ci = lax.broadcasted_iota(jnp.int32, (LANES, LANES), 1)
    before = jnp.where(ri < ci, 1.0, 0.0).astype(BF16)
    hi, mid, lo = _split3(jnp.broadcast_to(padded, (SUBLANES, LANES)))
    off = ((_dot(hi, before) + _dot(mid, before)) + _dot(lo, before))[0:1, :]
    pick = jnp.where(lax.broadcasted_iota(jnp.int32, (SUBLANES, LANES), 0) == lane_i, 1.0, 0.0).astype(BF16)

    total = jnp.sum(padded, axis=-1, keepdims=True)
    tile_row = lax.broadcasted_iota(jnp.int32, (TAB_LANES, 1), 0).astype(F32) * tm
    ends = off + padded
    done = jnp.logical_and(is_expert, ends <= jnp.minimum(tile_row, total - 1.0))
    te_col = jnp.sum(jnp.where(done, 1.0, 0.0), axis=-1, keepdims=True)
    te_rows = _dot_nt(pick, jnp.where(lane_i == 0, te_col, 0.0).astype(BF16))
    tab_ref[...] = jnp.zeros_like(tab_ref)
    tab_ref[TAB_OFF:TAB_OFF + 1, 0:LANES] = off.astype(jnp.int32)
    tab_ref[TAB_CNT:TAB_CNT + 1, 0:LANES] = cnt.astype(jnp.int32)
    tab_ref[TAB_TILE_EXPERT:TAB_TILE_EXPERT + 1, :] = te_rows[0:1, :].astype(jnp.int32)
    tab_ref[TAB_NVALID:TAB_NVALID + 1, 0:LANES] = jnp.broadcast_to(total / tm, (1, LANES)).astype(jnp.int32)

    n = info_ref.shape[0]
    for c0 in range(0, n, chunk):
        blk = info_ref[c0:c0 + chunk, :]
        lane_a = blk[:, R_EA:R_EA + 1] + RL_E
        lane_b = blk[:, R_EB:R_EB + 1] + RL_E
        pos_a = blk[:, R_RA:R_RA + 1] + jnp.sum(jnp.where(lane == lane_a, off, 0.0), axis=-1, keepdims=True)
        pos_b = blk[:, R_RB:R_RB + 1] + jnp.sum(jnp.where(lane == lane_b, off, 0.0), axis=-1, keepdims=True)
        z_hi, z_mid, z_lo = _split3(jnp.where(lane_i == 0, pos_a, jnp.where(lane_i == 1, pos_b, 0.0)))
        rows = (_dot_nt(pick, z_hi) + _dot_nt(pick, z_mid)) + _dot_nt(pick, z_lo)
        pos_ref[:, c0:c0 + chunk] = rows.astype(jnp.int32)


def _routing_tables(info, counts, n_tiles, tm):
    n = info.shape[0]
    groups = n // LANES
    assert n_tiles <= TAB_LANES
    chunk = LANES * max(g for g in range(1, 65) if groups % g == 0)
    pos, tab = pl.pallas_call(
        functools.partial(_positions_kernel, tm=tm, chunk=chunk),
        out_shape=(jax.ShapeDtypeStruct((SUBLANES, n), jnp.int32),
                   jax.ShapeDtypeStruct((SUBLANES, TAB_LANES), jnp.int32)),
        grid=(1,),
        in_specs=[pl.BlockSpec((n, LANES), lambda i: (0, 0)), pl.BlockSpec((1, LANES), lambda i: (0, 0))],
        out_specs=(pl.BlockSpec((SUBLANES, n), lambda i: (0, 0)),
                   pl.BlockSpec((SUBLANES, TAB_LANES), lambda i: (0, 0))),
        compiler_params=pltpu.CompilerParams(
            dimension_semantics=("arbitrary",), vmem_limit_bytes=VMEM_LIMIT),
        name="positions",
    )(info, counts)
    return tab, pos[0], pos[1]


def _combine_kernel(pa_ref, pb_ref, h_ref, info_ref, ys_hbm, nfin_ref, yp_ref, ysm_ref,
                    buf_a, buf_b, sem, *, n_prompt_tiles):
    i = pl.program_id(0)
    n_steps = pl.num_programs(0)
    tt = h_ref.shape[0]
    slot = lax.rem(i, N_GATHER_SLOTS)

    def start(tile, s):
        base = tile * tt
        _start_tile_gather(lambda r: pa_ref[base + r] * TOK_TILE_ROWS, tt, ys_hbm, buf_a.at[s],
                           sem.at[s], lambda r: 0)
        _start_tile_gather(lambda r: pb_ref[base + r] * TOK_TILE_ROWS, tt, ys_hbm, buf_b.at[s],
                           sem.at[s], lambda r: 1)

    @pl.when(i == 0)
    def _first():
        start(0, 0)
        start(lax.rem(1, n_steps), 1)

    _wait_tile_gather(tt, ys_hbm, buf_a.at[slot], sem.at[slot])
    _wait_tile_gather(tt, ys_hbm, buf_b.at[slot], sem.at[slot])
    info = info_ref[...]
    rows_a = _load_token_tiles(buf_a.at[slot], tt)
    rows_b = _load_token_tiles(buf_b.at[slot], tt)
    x = h_ref[...]
    start(lax.rem(i + 2, n_steps), lax.rem(i + 2, N_GATHER_SLOTS))
    h = x + info[:, R_GA:R_GA + 1] * rows_a + info[:, R_GB:R_GB + 1] * rows_b
    y = h * _rms_scale(h) * nfin_ref[...]

    @pl.when(i < n_prompt_tiles)
    def _prompt():
        yp_ref[...] = y

    @pl.when(i >= n_prompt_tiles)
    def _sample():
        ysm_ref[...] = y

    @pl.when(i == n_steps - 1)
    def _drain():
        for k in (1, 2):
            s = lax.rem(i + k, N_GATHER_SLOTS)
            _wait_tile_gather(tt, ys_hbm, buf_a.at[s], sem.at[s])
            _wait_tile_gather(tt, ys_hbm, buf_b.at[s], sem.at[s])


def _combine(hmid, info, ys, pos_a, pos_b, nfin, n_prompt):
    n, d = hmid.shape
    tt = CHUNK
    n_prompt_tiles = n_prompt // tt
    n_sample = n - n_prompt
    grid_spec = pltpu.PrefetchScalarGridSpec(
        num_scalar_prefetch=2,
        grid=(n // tt,),
        in_specs=[pl.BlockSpec((tt, d), lambda i, pa, pb: (i, 0)),
                  pl.BlockSpec((tt, LANES), lambda i, pa, pb: (i, 0)),
                  pl.BlockSpec(memory_space=pl.ANY),
                  pl.BlockSpec((1, d), lambda i, pa, pb: (0, 0))],
        out_specs=(pl.BlockSpec((tt, d), lambda i, pa, pb: (jnp.minimum(i, n_prompt_tiles - 1), 0)),
                   pl.BlockSpec((tt, d), lambda i, pa, pb: (jnp.maximum(i - n_prompt_tiles, 0), 0))),
        scratch_shapes=[pltpu.VMEM((N_GATHER_SLOTS, tt * TOK_TILE_ROWS, LANES), F32),
                        pltpu.VMEM((N_GATHER_SLOTS, tt * TOK_TILE_ROWS, LANES), F32),
                        pltpu.SemaphoreType.DMA((N_GATHER_SLOTS,))],
    )
    return pl.pallas_call(
        functools.partial(_combine_kernel, n_prompt_tiles=n_prompt_tiles),
        out_shape=(jax.ShapeDtypeStruct((n_prompt, d), F32),
                   jax.ShapeDtypeStruct((n_sample, d), F32)),
        grid_spec=grid_spec,
        compiler_params=pltpu.CompilerParams(
            dimension_semantics=("arbitrary",), vmem_limit_bytes=VMEM_LIMIT),
        name="combine",
    )(pos_a, pos_b, hmid, info, ys, nfin)


def _moe_and_final_norm(hmid, n_prompt, rp, wg, wu, wd, nfin):
    n = hmid.shape[0]
    tm = FFN_TM
    n_tiles = (2 * n + N_EXPERTS * (tm - 1)) // tm
    xn, info, counts = _router(hmid, rp)
    tab, pos_a, pos_b = _routing_tables(info, counts, n_tiles, tm)
    ys = _expert_ffn(xn, tab, pos_a, pos_b, n_tiles, wg, wu, wd)
    return _combine(hmid, info, ys, pos_a, pos_b, nfin, n_prompt)


def kernel(x_prompt, x_sample, state_mlstm_C, state_mlstm_n, state_mlstm_m, state_mlstm_conv, state_ssm, state_ssm_conv, meta_tokens, norm_mix, w_in, conv_a_w, conv_a_b, w_q, w_k, w_v, b_i, b_f, norm_a, conv_b_w, conv_b_b, dt_bias, a_log, d_skip, norm_b, w_out, norm_ffn, w_r1, b_r1, w_r2, b_r2, w_gate, w_up, w_down, norm_final):
    bsz, seq, d = x_prompt.shape
    nb = x_sample.shape[0]
    d_a = H_A * DH_A
    conv_b = H_B * HD_B + 2 * G_B * N_STATE
    assert w_in.shape[0] == 1 and x_sample.shape[1] == 1 and seq % CHUNK == 0 and nb == CHUNK
    mp = _prep_mixer_params(norm_mix, w_in, conv_a_w, conv_a_b, w_q, w_k, w_v, b_i, b_f, norm_a,
                            conv_b_w, conv_b_b, dt_bias, a_log, d_skip, norm_b, w_out)
    rp = _prep_router_params(norm_ffn, w_r1, b_r1, w_r2, b_r2)
    xmeta = jnp.concatenate([jnp.zeros((CHUNK - N_META, d), F32), meta_tokens.astype(F32)], 0)

    hmid, p_c, p_n, p_m, p_ca, p_s, p_cb = _prompt_mixer(x_prompt.astype(F32), xmeta, mp, nb)
    m0 = jnp.pad(state_mlstm_m.reshape(nb, H_A).astype(F32), ((0, 0), (0, LANES - H_A)))
    hmid, s_c, s_n, s_m, s_ca, s_s, s_cb = _sample_mixer(
        x_sample.reshape(nb, d).astype(F32),
        state_mlstm_C.reshape(nb, H_A, DH_A, DH_A).astype(F32),
        state_mlstm_n.reshape(nb, d_a).astype(F32),
        m0,
        state_mlstm_conv.reshape(nb, (CONV_W - 1) * d_a).astype(F32),
        state_ssm.reshape(nb, H_B // 2, 2 * HD_B, N_STATE).astype(F32),
        state_ssm_conv.reshape(nb, (CONV_W - 1) * conv_b).astype(F32),
        mp, hmid, bsz * seq)

    wshape = w_gate.shape[1:]
    y_p, y_s = _moe_and_final_norm(
        hmid, bsz * seq, rp, w_gate.reshape(wshape).astype(F32), w_up.reshape(wshape).astype(F32),
        w_down.reshape(w_down.shape[1:]).astype(F32), norm_final.reshape(1, d).astype(F32))

    return (y_p.reshape(bsz, seq, d), y_s.reshape(nb, 1, d),
            p_c.reshape(1, bsz, H_A, DH_A, DH_A), p_n.reshape(1, bsz, H_A, DH_A),
            p_m[:, 0, :H_A].reshape(1, bsz, H_A), p_ca.reshape(1, bsz, CONV_W - 1, d_a),
            p_s.reshape(1, bsz, H_B, HD_B, N_STATE), p_cb.reshape(1, bsz, CONV_W - 1, conv_b),
            s_c.reshape(1, nb, H_A, DH_A, DH_A), s_n.reshape(1, nb, H_A, DH_A),
            s_m[:, :H_A].reshape(1, nb, H_A), s_ca.reshape(1, nb, CONV_W - 1, d_a),
            s_s.reshape(1, nb, H_B, HD_B, N_STATE), s_cb.reshape(1, nb, CONV_W - 1, conv_b))
```

```python
import functools
import math

import jax
import jax.numpy as jnp
from jax import lax
from jax.experimental import pallas as pl
from jax.experimental.pallas import tpu as pltpu

F32 = jnp.float32
BF16 = jnp.bfloat16

EPS = 1e-6
N_META = 16
CONV_W = 4
CHUNK = 128
H_A = 8
DH_A = 128
H_B = 16
HD_B = 64
N_STATE = 128
G_B = 2
N_EGROUPS = 4
N_EPG = 4
N_EXPERTS = 16
LANES = 128
SUBLANES = 8
CONV_HDR = SUBLANES
VMEM_LIMIT = 56 * 1024 * 1024

L_F = 0
L_DTA = 8
L_I = 24
L_DT = 32

NEG_INF = float("-inf")


def _dot(a, b):
    return jnp.dot(a, b, preferred_element_type=F32)


def _dot_nt(a, b):
    return lax.dot_general(a, b, (((1,), (1,)), ((), ())), preferred_element_type=F32)


def _dot_tn(a, b):
    return lax.dot_general(a, b, (((0,), (0,)), ((), ())), preferred_element_type=F32)


def _split3(x):
    hi = x.astype(BF16)
    r = x - hi.astype(F32)
    mid = r.astype(BF16)
    lo = (r - mid.astype(F32)).astype(BF16)
    return hi, mid, lo


def _silu(x):
    return x * jax.nn.sigmoid(x)


def _softplus_parts(x):
    t = jnp.log1p(jnp.exp(-jnp.abs(x)))
    return jnp.maximum(x, 0.0) + t, jnp.minimum(x, 0.0) - t


def _rms_scale(x):
    return lax.rsqrt(jnp.mean(x * x, axis=-1, keepdims=True) + EPS)


TOK_TILE_ROWS = SUBLANES


def _store_token_tiles(ref, x):
    n = x.shape[0]
    for j in range(TOK_TILE_ROWS):
        ref[pl.ds(j, n, stride=TOK_TILE_ROWS), :] = x[:, j * LANES:(j + 1) * LANES]


def _causal_conv(x, tail, w_ref, b_ref):
    n_tail = tail.shape[0]
    row = lax.broadcasted_iota(jnp.int32, (n_tail, 1), 0)
    acc = w_ref[CONV_W - 1:CONV_W, :] * x + b_ref[...]
    for k in range(1, CONV_W):
        rolled = pltpu.roll(x, k, axis=0)
        head = jnp.where(row < k, pltpu.roll(tail, k, axis=0), rolled[0:n_tail])
        shifted = jnp.concatenate([head, rolled[n_tail:]], axis=0)
        acc = acc + w_ref[CONV_W - 1 - k:CONV_W - k, :] * shifted
    return acc


def _load_token_tiles(ref, n):
    return jnp.concatenate(
        [ref[pl.ds(j, n, stride=TOK_TILE_ROWS), :] for j in range(TOK_TILE_ROWS)], axis=1)


PROMPT_ROWS = 2


def _prompt_rows_kernel(xmeta_ref, xp_ref, nmix_ref, wcat_ref, bsm_ref, alog_ref,
                        cwa_ref, cba_ref, cwb_ref, cbb_ref, wq_ref, wk_ref, wv_ref,
                        na_ref, nb_ref, dsk_ref, wout_ref,
                        hmid_hbm, c_ref, n_ref, m_ref, conva_ref, s_ref, convb_ref,
                        xa_buf, xbc_buf, y_buf, merged, hout, sem, *, seq, n_prompt_rows):
    p = pl.program_id(0)
    c = pl.program_id(1)
    last_p = pl.num_programs(0) - 1
    last_c = pl.num_programs(1) - 1
    T = CHUNK
    RB = xp_ref.shape[0]
    d_a = H_A * DH_A
    d_b = H_B * HD_B
    conv_b = d_b + 2 * G_B * N_STATE

    def out_copy(r, row0):
        return pltpu.make_async_copy(hout.at[r], hmid_hbm.at[pl.ds(row0, T), :], sem.at[r])

    @pl.when(c == 0)
    def _init():
        c_ref[...] = jnp.zeros_like(c_ref)
        n_ref[...] = jnp.zeros_like(n_ref)
        m_ref[...] = jnp.zeros_like(m_ref)
        s_ref[...] = jnp.zeros_like(s_ref)
        xa_buf[...] = jnp.zeros_like(xa_buf)
        xbc_buf[...] = jnp.zeros_like(xbc_buf)

    @pl.when(jnp.logical_and(p == 0, c == 0))
    def _clear_sample_rows():
        hout[0] = jnp.zeros((T, hout.shape[2]), F32)
        cp = out_copy(0, n_prompt_rows)
        cp.start()
        cp.wait()

    row = lax.broadcasted_iota(jnp.int32, (T, 1), 0)
    valid = jnp.logical_or(c > 0, row >= T - N_META)
    xs_in = [jnp.where(c == 0, xmeta_ref[...], xp_ref[r]) for r in range(RB)]
    x2 = jnp.concatenate(xs_in, axis=0)
    hn = (x2 * _rms_scale(x2) * nmix_ref[...]).astype(BF16)

    lane = lax.broadcasted_iota(jnp.int32, (1, LANES), 1)
    lane_f = lane < L_DTA
    lane_dta = jnp.logical_and(lane >= L_DTA, lane < L_I)
    lane_i = jnp.logical_and(lane >= L_I, lane < L_DT)
    lane_dt = jnp.logical_and(lane >= L_DT, lane < L_DT + H_B)
    ri = lax.broadcasted_iota(jnp.int32, (T, T), 0)
    ci = lax.broadcasted_iota(jnp.int32, (T, T), 1)
    causal = ri >= ci
    tri = jnp.where(causal, 1.0, 0.0).astype(BF16)
    a_neg = jnp.where(lane_dta, -jnp.exp(alog_ref[...]), 0.0)
    left = lane < HD_B
    top = lax.broadcasted_iota(jnp.int32, (LANES, 1), 0) < HD_B

    off_small = 2 * d_a + d_b + conv_b
    pre2 = _dot(hn, wcat_ref[:, off_small:]) + bsm_ref[...]
    xa2 = _dot(hn, wcat_ref[:, 0:d_a])

    gcols, grows, xcs, xabs = [], [], [], []

    def gate_tables():
        for r in range(RB):
            pre = pre2[r * T:(r + 1) * T]
            sp, lsig = _softplus_parts(pre)
            to_cum = jnp.where(lane_f, lsig, jnp.where(lane_dta, sp * a_neg, 0.0))
            to_cum = jnp.where(valid, to_cum, 0.0)
            hi, mid, lo = _split3(to_cum)
            cum = _dot(tri, hi) + _dot(tri, mid) + _dot(tri, lo)
            extra = jnp.where(lane_i, jnp.where(valid, pre, NEG_INF),
                              jnp.where(lane_dt, jnp.where(valid, sp, 0.0), 0.0))
            gcol = cum + extra
            gcols.append(gcol)
            grows.append(gcol.T)

    gate_tables()
    for r in range(RB):
        rs = slice(r * T, (r + 1) * T)
        xa = xa2[rs]
        xc = _causal_conv(xa, xa_buf[r], cwa_ref, cba_ref)
        xa_buf[r] = xa[T - CONV_HDR:T, :]
        conva_ref[r] = xa[T - 3:T, :]
        xcs.append(_silu(xc).astype(BF16))
        xabs.append(xa.astype(BF16))

    items = [(r, h) for h in range(H_A) for r in range(RB)]
    hsl = lambda h: slice(h * DH_A, (h + 1) * DH_A)
    m_alls = [m_ref[r] for r in range(RB)]
    m_news = list(m_alls)
    qs, ks, vs, qks, st, dd = {}, {}, {}, {}, {}, {}

    def stage_qkv(it):
        r, h = it
        qs[it] = _dot(xcs[r][:, hsl(h)], wq_ref[h]).astype(BF16)
        ks[it] = _dot(xcs[r][:, hsl(h)], wk_ref[h]) * (DH_A ** -0.5)
        vs[it] = _dot(xabs[r][:, hsl(h)], wv_ref[h]).astype(BF16)

    def stage_qk(it):
        qks[it] = _dot_nt(qs[it], ks[it].astype(BF16))

    pairs_per_group = H_B // G_B // 2
    groups = [(r, g) for g in range(G_B) for r in range(RB)]
    proj, xbcs, bgs, cgs, cbs = {}, [], {}, {}, {}

    def project(name, lo, hi):
        proj[name] = _dot(hn, wcat_ref[:, lo:hi])

    def ssd_inputs():
        for r in range(RB):
            xbc = proj["xbc"][r * T:(r + 1) * T]
            xbc_c = _causal_conv(xbc, xbc_buf[r], cwb_ref, cbb_ref)
            xbc_buf[r] = xbc[T - CONV_HDR:T, :]
            convb_ref[r] = xbc[T - 3:T, :]
            xbcs.append(_silu(xbc_c))
        for r, g in groups:
            bgs[(r, g)] = xbcs[r][:, d_b + g * N_STATE:d_b + (g + 1) * N_STATE].astype(BF16)
            cgs[(r, g)] = xbcs[r][:, d_b + (G_B + g) * N_STATE:d_b + (G_B + g + 1) * N_STATE].astype(BF16)
            cbs[(r, g)] = _dot_nt(cgs[(r, g)], bgs[(r, g)])

    def stage_weights(it):
        r, h = it
        gcol, grow = gcols[r], grows[r]
        b_col = gcol[:, L_F + h:L_F + h + 1]
        i_col = gcol[:, L_I + h:L_I + h + 1]
        b_row = grow[L_F + h:L_F + h + 1, :]
        i_row = grow[L_I + h:L_I + h + 1, :]
        m0 = m_alls[r][:, h:h + 1]
        dmat = jnp.where(causal, b_col - (b_row - i_row), NEG_INF)
        m_inter = b_col + m0
        m = jnp.maximum(m_inter, jnp.max(dmat, axis=-1, keepdims=True))
        w_inter = jnp.exp(m_inter - m)
        s = qks[it] * jnp.exp(dmat - m)
        n0 = n_ref[r, h:h + 1, :]
        den = (jnp.sum(s, axis=-1, keepdims=True)
               + w_inter * jnp.sum(qs[it].astype(F32) * n0, axis=-1, keepdims=True))
        m_last = m[T - 1:T, :]
        b_last = b_col[T - 1:T, :]
        dec = jnp.exp(b_last + m0 - m_last)
        kw = ks[it] * jnp.exp(b_last - b_col + i_col - m_last)
        n_ref[r, h:h + 1, :] = dec * n0 + jnp.sum(kw, axis=0, keepdims=True)
        m_news[r] = jnp.where(lane == h, m_last, m_news[r])
        st[it] = (s.astype(BF16), kw.astype(BF16), w_inter,
                  jnp.maximum(jnp.abs(den), jnp.exp(-m)), dec)

    def stage_readout(it):
        r, h = it
        s_b, kw_b, w_inter, den, dec = st[it]
        c0 = c_ref[r, h]
        num = _dot(s_b, vs[it]) + w_inter * _dot(qs[it], c0.astype(BF16))
        c_ref[r, h] = dec * c0 + _dot_tn(kw_b, vs[it])
        dd[it] = num / den

    def stage_head_out(it):
        r, h = it
        hh = dd[it]
        hh = hh * _rms_scale(hh) * na_ref[:, hsl(h)]
        merged[r * T:(r + 1) * T, hsl(h)] = (
            hh * jax.nn.sigmoid(proj["za"][r * T:(r + 1) * T, hsl(h)])).astype(BF16)

    pairs = [(r, pi) for pi in range(H_B // 2) for r in range(RB)]
    psl = lambda pi: slice(pi * LANES, (pi + 1) * LANES)
    sw = {}

    def stage_decay(pr):
        r, pi = pr
        g = pi // pairs_per_group
        gcol, grow = gcols[r], grows[r]
        xpair = xbcs[r][:, psl(pi)]
        scs, a_cols, w_cols, a_lasts = [], [], [], []
        for j in (2 * pi, 2 * pi + 1):
            a_col = gcol[:, L_DTA + j:L_DTA + j + 1]
            a_row = grow[L_DTA + j:L_DTA + j + 1, :]
            dt_col = gcol[:, L_DT + j:L_DT + j + 1]
            dt_row = grow[L_DT + j:L_DT + j + 1, :]
            decay = jnp.exp(jnp.where(causal, a_col - a_row, NEG_INF))
            scs.append((cbs[(r, g)] * decay * dt_row).astype(BF16))
            a_last = a_col[T - 1:T, :]
            a_cols.append(a_col)
            a_lasts.append(a_last)
            w_cols.append(jnp.exp(a_last - a_col) * dt_col)
        sw[pr] = (scs, xpair.astype(BF16),
                  (xpair * jnp.where(left, w_cols[0], w_cols[1])).astype(BF16),
                  jnp.exp(jnp.where(left, a_cols[0], a_cols[1])),
                  jnp.exp(jnp.where(top, a_lasts[0], a_lasts[1])))
    def stage_pair_out(pr):
        r, pi = pr
        g = pi // pairs_per_group
        scs, xpb, xw, ea, ea_last = sw[pr]
        s0 = s_ref[r, pi]
        y = jnp.where(left, _dot(scs[0], xpb), _dot(scs[1], xpb))
        y = y + ea * _dot_nt(cgs[(r, g)], s0.astype(BF16))
        s_ref[r, pi] = ea_last * s0 + _dot_tn(xw, bgs[(r, g)])
        y = y + dsk_ref[:, psl(pi)] * xbcs[r][:, psl(pi)]
        y_buf[r, :, psl(pi)] = y * _silu(proj["zb"][r * T:(r + 1) * T, psl(pi)])

    def each(stage, seq):
        for e in seq:
            stage(e)

    each(stage_qkv, items)
    project("xbc", 2 * d_a + d_b, off_small)
    each(stage_qk, items)
    project("za", d_a, 2 * d_a)
    ssd_inputs()
    each(stage_weights, items)
    each(stage_readout, items)
    project("zb", 2 * d_a, 2 * d_a + d_b)
    for r in range(RB):
        m_ref[r] = m_news[r]
    each(stage_head_out, items)
    each(stage_decay, pairs)
    each(stage_pair_out, pairs)
    gw = d_b // G_B
    for r in range(RB):
        for g in range(G_B):
            yg = y_buf[r, :, g * gw:(g + 1) * gw]
            merged[r * T:(r + 1) * T, d_a + g * gw:d_a + (g + 1) * gw] = (
                yg * _rms_scale(yg) * nb_ref[:, g * gw:(g + 1) * gw]).astype(BF16)

    @pl.when(c > 0)
    def _out():
        out2 = x2 + _dot(merged[...], wout_ref[...])

        @pl.when(jnp.logical_or(c > 1, p > 0))
        def _wait_previous():
            for r in range(RB):
                out_copy(r, 0).wait()

        for r in range(RB):
            hout[r] = out2[r * T:(r + 1) * T]
            out_copy(r, (p * RB + r) * seq + (c - 1) * T).start()

        @pl.when(jnp.logical_and(p == last_p, c == last_c))
        def _drain():
            for r in range(RB):
                out_copy(r, 0).wait()


def _const_spec(shape):
    nd = len(shape)
    return pl.BlockSpec(shape, lambda b, c, _nd=nd: (0,) * _nd)


def _prompt_mixer(x_prompt, xmeta, p, n_extra_rows):
    bsz, seq, d = x_prompt.shape
    assert n_extra_rows == CHUNK and seq % CHUNK == 0
    n_chunks = seq // CHUNK + 1
    cps = seq // CHUNK
    d_a = H_A * DH_A
    conv_b = H_B * HD_B + 2 * G_B * N_STATE
    consts = [p["nmix"], p["wcat"], p["bsm"], p["alog"], p["cwa"], p["cba"], p["cwb"], p["cbb"],
              p["wq"], p["wk"], p["wv"], p["na"], p["nb"], p["dsk"], p["wout"]]
    rb = PROMPT_ROWS
    assert bsz % rb == 0
    in_specs = [_const_spec(xmeta.shape),
                pl.BlockSpec((rb, CHUNK, d), lambda b, c: (b, jnp.maximum(c - 1, 0), 0))]
    in_specs += [_const_spec(a.shape) for a in consts]
    out_shape = (
        jax.ShapeDtypeStruct((bsz * seq + n_extra_rows, d), F32),
        jax.ShapeDtypeStruct((bsz, H_A, DH_A, DH_A), F32),
        jax.ShapeDtypeStruct((bsz, H_A, DH_A), F32),
        jax.ShapeDtypeStruct((bsz, 1, LANES), F32),
        jax.ShapeDtypeStruct((bsz, CONV_W - 1, d_a), F32),
        jax.ShapeDtypeStruct((bsz, H_B // 2, 2 * HD_B, N_STATE), F32),
        jax.ShapeDtypeStruct((bsz, CONV_W - 1, conv_b), F32),
    )
    out_specs = (
        pl.BlockSpec(memory_space=pl.ANY),
        pl.BlockSpec((rb, H_A, DH_A, DH_A), lambda b, c: (b, 0, 0, 0)),
        pl.BlockSpec((rb, H_A, DH_A), lambda b, c: (b, 0, 0)),
        pl.BlockSpec((rb, 1, LANES), lambda b, c: (b, 0, 0)),
        pl.BlockSpec((rb, CONV_W - 1, d_a), lambda b, c: (b, 0, 0)),
        pl.BlockSpec((rb, H_B // 2, 2 * HD_B, N_STATE), lambda b, c: (b, 0, 0, 0)),
        pl.BlockSpec((rb, CONV_W - 1, conv_b), lambda b, c: (b, 0, 0)),
    )
    return pl.pallas_call(
        functools.partial(_prompt_rows_kernel, seq=seq, n_prompt_rows=bsz * seq),
        out_shape=out_shape,
        grid=(bsz // rb, n_chunks),
        in_specs=in_specs,
        out_specs=out_specs,
        scratch_shapes=[
            pltpu.VMEM((rb, CONV_HDR, d_a), F32),
            pltpu.VMEM((rb, CONV_HDR, conv_b), F32),
            pltpu.VMEM((rb, CHUNK, H_B * HD_B), F32),
            pltpu.VMEM((rb * CHUNK, d_a + H_B * HD_B), BF16),
            pltpu.VMEM((rb, CHUNK, d), F32),
            pltpu.SemaphoreType.DMA((rb,)),
        ],
        compiler_params=pltpu.CompilerParams(
            dimension_semantics=("arbitrary", "arbitrary"), vmem_limit_bytes=VMEM_LIMIT),
        name="prompt_mixer",
    )(xmeta, x_prompt, *consts)


def _regroup_w_in_kernel(w_ref, o_ref):
    d_a = H_A * DH_A
    d_b = H_B * HD_B
    conv_b = d_b + 2 * G_B * N_STATE
    o_i = 2 * d_a
    o_f = o_i + H_A
    o_zb = o_f + H_A
    o_xbc = o_zb + d_b
    o_dt = o_xbc + conv_b
    rows = w_ref.shape[0]
    o_ref[:, 0:2 * d_a] = w_ref[:, 0:2 * d_a].astype(BF16)
    o_ref[:, 2 * d_a:2 * d_a + d_b] = w_ref[:, o_zb:o_zb + d_b].astype(BF16)
    o_ref[:, 2 * d_a + d_b:2 * d_a + d_b + conv_b] = w_ref[:, o_xbc:o_xbc + conv_b].astype(BF16)
    small = jnp.concatenate(
        [w_ref[:, o_f:o_f + H_A], w_ref[:, o_dt:o_dt + H_B], w_ref[:, o_i:o_i + H_A],
         w_ref[:, o_dt:o_dt + H_B], jnp.zeros((rows, LANES - (L_DT + H_B)), F32)], axis=1)
    o_ref[:, 2 * d_a + d_b + conv_b:] = small.astype(BF16)


def _prep_mixer_params(norm_mix, w_in, conv_a_w, conv_a_b, w_q, w_k, w_v, b_i, b_f, norm_a,
                       conv_b_w, conv_b_b, dt_bias, a_log, d_skip, norm_b, w_out):
    d_a = H_A * DH_A
    d_b = H_B * HD_B
    conv_b = d_b + 2 * G_B * N_STATE
    d_model, d_in = w_in.shape[1], w_in.shape[2]
    n_cols = 2 * d_a + d_b + conv_b + LANES
    rows = 256
    assert w_in.shape[0] == 1 and d_model % rows == 0
    wcat = pl.pallas_call(
        _regroup_w_in_kernel,
        out_shape=jax.ShapeDtypeStruct((d_model, n_cols), BF16),
        grid=(d_model // rows,),
        in_specs=[pl.BlockSpec((None, rows, d_in), lambda i: (0, i, 0))],
        out_specs=pl.BlockSpec((rows, n_cols), lambda i: (i, 0)),
        compiler_params=pltpu.CompilerParams(
            dimension_semantics=("arbitrary",), vmem_limit_bytes=VMEM_LIMIT),
        name="regroup_w_in",
    )(w_in.astype(F32))

    def lanes(parts):
        pieces, at = [], 0
        for off, a in parts:
            pieces += [jnp.zeros((1, off - at), F32), a.astype(F32)]
            at = off + a.shape[1]
        return jnp.concatenate(pieces + [jnp.zeros((1, LANES - at), F32)], axis=1)

    return dict(
        nmix=norm_mix.reshape(1, -1).astype(F32),
        wcat=wcat,
        bsm=lanes([(L_F, b_f), (L_DTA, dt_bias), (L_I, b_i), (L_DT, dt_bias)]),
        alog=lanes([(L_DTA, a_log)]),
        cwa=conv_a_w.reshape(CONV_W, d_a).astype(F32), cba=conv_a_b.reshape(1, d_a).astype(F32),
        cwb=conv_b_w.reshape(CONV_W, conv_b).astype(F32), cbb=conv_b_b.reshape(1, conv_b).astype(F32),
        wq=w_q.reshape(H_A, DH_A, DH_A).astype(BF16), wk=w_k.reshape(H_A, DH_A, DH_A).astype(BF16),
        wv=w_v.reshape(H_A, DH_A, DH_A).astype(BF16),
        na=norm_a.reshape(1, d_a).astype(F32), nb=norm_b.reshape(1, d_b).astype(F32),
        dsk=jnp.repeat(d_skip.reshape(H_B).astype(F32), HD_B)[None, :],
        wout=w_out.reshape(d_a + d_b, -1).astype(BF16),
    )


SAMPLE_BLOCK = 8


def _expand_lanes(vals, first_lane, n_heads, width):
    r = lax.broadcasted_iota(jnp.int32, (LANES, n_heads * width), 0) - first_lane
    c = lax.broadcasted_iota(jnp.int32, (LANES, n_heads * width), 1)
    sel = jnp.logical_and(c >= r * width, c < (r + 1) * width)
    e = jnp.where(sel, 1.0, 0.0).astype(BF16)
    hi, mid, lo = _split3(vals)
    return (_dot(hi, e) + _dot(mid, e)) + _dot(lo, e)


def _sample_pre_kernel(x_ref, nmix_ref, wcat_ref, bsm_ref, alog_ref, cwa_ref, cba_ref, cwb_ref, cbb_ref,
                       wq_ref, wk_ref, wv_ref, dsk_ref, conva_ref, convb_ref, n0_ref, m0_ref,
                       conva_out, convb_out, n1_out, m1_out, g_out, qt_out, kwt_out, xwt_out,
                       v_out, bc_out, a1_out, w1_out, den_out, y1_out, ea_out, zbs_out, zas_out):
    d_a = H_A * DH_A
    d_b = H_B * HD_B
    conv_b = d_b + 2 * G_B * N_STATE
    shift_i = LANES - (L_I - L_F)
    x = x_ref[...]
    hn = (x * _rms_scale(x) * nmix_ref[...]).astype(BF16)
    lane = lax.broadcasted_iota(jnp.int32, (1, LANES), 1)
    lane_f = lane < L_DTA
    lane_dta = jnp.logical_and(lane >= L_DTA, lane < L_I)
    pre = _dot(hn, wcat_ref[:, 2 * d_a + d_b + conv_b:]) + bsm_ref[...]
    sp, lsig = _softplus_parts(pre)
    a_neg = jnp.where(lane_dta, -jnp.exp(alog_ref[...]), 0.0)
    pre_al = pltpu.roll(pre, shift_i, axis=1)
    sp_al = pltpu.roll(sp, shift_i, axis=1)
    m_inter = lsig + m0_ref[...]
    m = jnp.maximum(m_inter, pre_al)
    w_inter = jnp.exp(m_inter - m)
    sfac = jnp.exp(pre_al - m)
    ea = jnp.exp(sp * a_neg)
    dt = sp_al

    xa = _dot(hn, wcat_ref[:, 0:d_a])
    xc = (cwa_ref[0:1, :] * conva_ref[:, 0:d_a] + cwa_ref[1:2, :] * conva_ref[:, d_a:2 * d_a]
          + cwa_ref[2:3, :] * conva_ref[:, 2 * d_a:3 * d_a] + cwa_ref[3:4, :] * xa + cba_ref[...])
    conva_out[:, 0:2 * d_a] = conva_ref[:, d_a:3 * d_a]
    conva_out[:, 2 * d_a:3 * d_a] = xa
    xc = _silu(xc).astype(BF16)
    xab = xa.astype(BF16)
    sf_e = _expand_lanes(sfac, L_F, H_A, DH_A)
    w_e = _expand_lanes(w_inter, L_F, H_A, DH_A)
    qk8 = jnp.zeros((x.shape[0], LANES), F32)
    qn8 = jnp.zeros((x.shape[0], LANES), F32)
    for h in range(H_A):
        sl = slice(h * DH_A, (h + 1) * DH_A)
        q = _dot(xc[:, sl], wq_ref[h])
        k = _dot(xc[:, sl], wk_ref[h]) * (DH_A ** -0.5)
        v = _dot(xab[:, sl], wv_ref[h])
        kw = k * sf_e[:, sl]
        qk8 = jnp.where(lane == h, jnp.sum(q * k, axis=-1, keepdims=True), qk8)
        qn8 = jnp.where(lane == h, jnp.sum(q * n0_ref[:, sl], axis=-1, keepdims=True), qn8)
        n1_out[:, sl] = w_e[:, sl] * n0_ref[:, sl] + kw
        v_out[:, sl] = v
        qt_out[h] = q.T
        kwt_out[h] = kw.T
    s8 = qk8 * sfac
    a1_out[...] = _expand_lanes(s8, L_F, H_A, DH_A) * v_out[...]
    w1_out[...] = w_e
    den_out[...] = jnp.maximum(jnp.abs(_expand_lanes(s8 + w_inter * qn8, L_F, H_A, DH_A)),
                               jnp.exp(-_expand_lanes(m, L_F, H_A, DH_A)))
    m1_out[...] = m
    g_out[...] = jnp.where(lane_f, w_inter, jnp.where(lane_dta, ea, 0.0))
    zas_out[...] = jax.nn.sigmoid(_dot(hn, wcat_ref[:, d_a:2 * d_a]))

    off_xbc = 2 * d_a + d_b
    xbc = _dot(hn, wcat_ref[:, off_xbc:off_xbc + conv_b])
    xbc_c = (cwb_ref[0:1, :] * convb_ref[:, 0:conv_b] + cwb_ref[1:2, :] * convb_ref[:, conv_b:2 * conv_b]
             + cwb_ref[2:3, :] * convb_ref[:, 2 * conv_b:3 * conv_b] + cwb_ref[3:4, :] * xbc + cbb_ref[...])
    convb_out[:, 0:2 * conv_b] = convb_ref[:, conv_b:3 * conv_b]
    convb_out[:, 2 * conv_b:3 * conv_b] = xbc
    xbc_c = _silu(xbc_c)
    xs = xbc_c[:, 0:d_b]
    bc = xbc_c[:, d_b:conv_b]
    bc_out[...] = bc
    heads_per_group = H_B // G_B
    cbl = jnp.zeros((x.shape[0], LANES), F32)
    for g in range(G_B):
        cb_g = jnp.sum(bc[:, g * N_STATE:(g + 1) * N_STATE]
                       * bc[:, (G_B + g) * N_STATE:(G_B + g + 1) * N_STATE], axis=-1, keepdims=True)
        in_g = jnp.logical_and(lane >= L_DTA + g * heads_per_group,
                               lane < L_DTA + (g + 1) * heads_per_group)
        cbl = jnp.where(in_g, cb_g, cbl)
    dt_e = _expand_lanes(dt, L_DTA, H_B, HD_B)
    y1_out[...] = _expand_lanes(cbl * dt, L_DTA, H_B, HD_B) * xs + dsk_ref[...] * xs
    ea_out[...] = _expand_lanes(ea, L_DTA, H_B, HD_B)
    zbs_out[...] = _silu(_dot(hn, wcat_ref[:, 2 * d_a:2 * d_a + d_b]))
    xw = xs * dt_e
    for pi in range(H_B // 2):
        xwt_out[pi] = xw[:, pi * LANES:(pi + 1) * LANES].T


def _sample_state_kernel(g_ref, c0_ref, s0_ref, qt_ref, kwt_ref, xwt_ref, v_ref, bc_ref,
                         c1_ref, s1_ref, qc_ref, ysi_ref):
    i = pl.program_id(0)
    bb = c0_ref.shape[0]
    shift = lax.rem(LANES - lax.rem(i * bb, LANES), LANES)
    lane = lax.broadcasted_iota(jnp.int32, (1, LANES), 1)
    top = lax.broadcasted_iota(jnp.int32, (LANES, 1), 0) < HD_B
    heads_per_group = H_B // G_B
    for h in range(H_A):
        sl = slice(h * DH_A, (h + 1) * DH_A)
        qt = pltpu.roll(qt_ref[h], shift, axis=1)
        kwt = pltpu.roll(kwt_ref[h], shift, axis=1)
        for r in range(bb):
            b = i * bb + r
            c0 = c0_ref[r, h]
            dec = g_ref[b, L_F + h]
            v_row = v_ref[r:r + 1, sl]
            qc_ref[r:r + 1, sl] = jnp.sum(c0 * qt[:, r:r + 1], axis=0, keepdims=True)
            c1_ref[r, h] = dec * c0 + kwt[:, r:r + 1] * v_row
    for pi in range(H_B // 2):
        g = (2 * pi) // heads_per_group
        sl = slice(pi * LANES, (pi + 1) * LANES)
        xwt = pltpu.roll(xwt_ref[pi], shift, axis=1)
        acc = jnp.zeros((LANES, LANES), F32)
        for r in range(bb):
            b = i * bb + r
            s0 = s0_ref[r, pi]
            b_row = bc_ref[r:r + 1, g * N_STATE:(g + 1) * N_STATE]
            c_row = bc_ref[r:r + 1, (G_B + g) * N_STATE:(G_B + g + 1) * N_STATE]
            col = jnp.sum(s0 * c_row, axis=-1, keepdims=True)
            acc = jnp.where(lane == r, col, acc)
            ea_rows = jnp.where(top, g_ref[b, L_DTA + 2 * pi], g_ref[b, L_DTA + 2 * pi + 1])
            s1_ref[r, pi] = ea_rows * s0 + xwt[:, r:r + 1] * b_row
        ysi_ref[:, sl] = acc.T[0:bb, :]


def _sample_post_kernel(a1_ref, w1_ref, den_ref, y1_ref, ea_ref, zbs_ref, zas_ref, x_ref, qc_ref, ysi_ref,
                        na_ref, nb_ref, wout_ref, hall_ref, hmid_ref, merged):
    del hall_ref
    d_a = H_A * DH_A
    d_b = H_B * HD_B
    hh = (a1_ref[...] + w1_ref[...] * qc_ref[...]) / den_ref[...]
    for h in range(H_A):
        sl = slice(h * DH_A, (h + 1) * DH_A)
        hs = hh[:, sl]
        merged[:, sl] = (hs * _rms_scale(hs) * na_ref[:, sl] * zas_ref[:, sl]).astype(BF16)
    y = (y1_ref[...] + ea_ref[...] * ysi_ref[...]) * zbs_ref[...]
    gw = d_b // G_B
    for g in range(G_B):
        yg = y[:, g * gw:(g + 1) * gw]
        merged[:, d_a + g * gw:d_a + (g + 1) * gw] = (
            yg * _rms_scale(yg) * nb_ref[:, g * gw:(g + 1) * gw]).astype(BF16)
    hmid_ref[...] = x_ref[...] + _dot(merged[...], wout_ref[...])


def _vmem_specs(arrays):
    return [pl.BlockSpec(a.shape, lambda *_, _nd=a.ndim: (0,) * _nd) for a in arrays]


def _sample_mixer(x, c0, n0, m0, conva, s0, convb, p, hmid_all, row_offset):
    nb, d = x.shape
    d_a = H_A * DH_A
    d_b = H_B * HD_B
    conv_b = d_b + 2 * G_B * N_STATE
    row = lambda w: jax.ShapeDtypeStruct((nb, w), F32)
    tile = lambda k: jax.ShapeDtypeStruct((k, LANES, nb), F32)
    pre_in = [x, p["nmix"], p["wcat"], p["bsm"], p["alog"], p["cwa"], p["cba"], p["cwb"], p["cbb"],
              p["wq"], p["wk"], p["wv"], p["dsk"], conva, convb, n0, m0]
    pre_out_shape = (row(3 * d_a), row(3 * conv_b), row(d_a), row(LANES), row(LANES),
                     tile(H_A), tile(H_A), tile(H_B // 2), row(d_a), row(2 * G_B * N_STATE),
                     row(d_a), row(d_a), row(d_a), row(d_b), row(d_b), row(d_b), row(d_a))
    (conva1, convb1, n1, m1, g8, qt, kwt, xwt, v, bc, a1, w1, den, y1, ea_e, zbs, zas) = pl.pallas_call(
        _sample_pre_kernel,
        out_shape=pre_out_shape,
        grid=(1,),
        in_specs=_vmem_specs(pre_in),
        out_specs=tuple(pl.BlockSpec(s.shape, lambda i, _nd=len(s.shape): (0,) * _nd) for s in pre_out_shape),
        compiler_params=pltpu.CompilerParams(
            dimension_semantics=("arbitrary",), vmem_limit_bytes=VMEM_LIMIT),
        name="sample_pre",
    )(*pre_in)

    bb = SAMPLE_BLOCK
    const3 = lambda k: pl.BlockSpec((k, LANES, nb), lambda i, g: (0, 0, 0))
    state_grid = pltpu.PrefetchScalarGridSpec(
        num_scalar_prefetch=1,
        grid=(nb // bb,),
        in_specs=[pl.BlockSpec((bb, H_A, DH_A, DH_A), lambda i, g: (i, 0, 0, 0)),
                  pl.BlockSpec((bb, H_B // 2, 2 * HD_B, N_STATE), lambda i, g: (i, 0, 0, 0)),
                  const3(H_A), const3(H_A), const3(H_B // 2),
                  pl.BlockSpec((bb, d_a), lambda i, g: (i, 0)),
                  pl.BlockSpec((bb, 2 * G_B * N_STATE), lambda i, g: (i, 0))],
        out_specs=(pl.BlockSpec((bb, H_A, DH_A, DH_A), lambda i, g: (i, 0, 0, 0)),
                   pl.BlockSpec((bb, H_B // 2, 2 * HD_B, N_STATE), lambda i, g: (i, 0, 0, 0)),
                   pl.BlockSpec((bb, d_a), lambda i, g: (i, 0)),
                   pl.BlockSpec((bb, d_b), lambda i, g: (i, 0))),
    )
    c1, s1, qc, ysi = pl.pallas_call(
        _sample_state_kernel,
        out_shape=(jax.ShapeDtypeStruct(c0.shape, F32), jax.ShapeDtypeStruct(s0.shape, F32),
                   row(d_a), row(d_b)),
        grid_spec=state_grid,
        compiler_params=pltpu.CompilerParams(
            dimension_semantics=("arbitrary",), vmem_limit_bytes=VMEM_LIMIT),
        name="sample_state",
    )(g8, c0, s0, qt, kwt, xwt, v, bc)

    post_in = [a1, w1, den, y1, ea_e, zbs, zas, x, qc, ysi, p["na"], p["nb"], p["wout"]]
    hmid_all = pl.pallas_call(
        _sample_post_kernel,
        out_shape=jax.ShapeDtypeStruct(hmid_all.shape, F32),
        grid=(1,),
        in_specs=_vmem_specs(post_in) + [pl.BlockSpec(memory_space=pl.ANY)],
        out_specs=pl.BlockSpec((nb, d), lambda i: (row_offset // nb, 0)),
        scratch_shapes=[pltpu.VMEM((nb, d_a + d_b), BF16)],
        input_output_aliases={len(post_in): 0},
        compiler_params=pltpu.CompilerParams(
            dimension_semantics=("arbitrary",), vmem_limit_bytes=VMEM_LIMIT),
        name="sample_post",
    )(*post_in, hmid_all)
    return hmid_all, c1, n1, m1, conva1, s1, convb1


R_EA, R_EB, R_RA, R_RB, R_GA, R_GB = 0, 1, 2, 3, 4, 5
RL_E = N_EGROUPS


def _router_kernel(h_ref, nf_ref, whi_ref, wmid_ref, br_ref, xn_ref, info_ref, cnt_ref, carry):
    i = pl.program_id(0)
    tr = h_ref.shape[0]

    @pl.when(i == 0)
    def _init():
        carry[...] = jnp.zeros_like(carry)

    h = h_ref[...]
    xn = h * _rms_scale(h) * nf_ref[...]
    _store_token_tiles(xn_ref, xn)
    x_hi, x_mid, _ = _split3(xn)
    logits = (_dot(x_hi, whi_ref[...]) + _dot(x_hi, wmid_ref[...]) + _dot(x_mid, whi_ref[...])
              + br_ref[...])
    lane_i = lax.broadcasted_iota(jnp.int32, (1, LANES), 1)
    lane = lane_i.astype(F32)
    big = float(LANES)

    def first_lane_of(cond):
        return jnp.min(jnp.where(cond, lane, big), axis=-1, keepdims=True)

    l1 = jnp.where(lane_i < N_EGROUPS, logits, NEG_INF)
    e1 = jnp.exp(l1 - jnp.max(l1, axis=-1, keepdims=True))
    p1 = e1 / jnp.sum(e1, axis=-1, keepdims=True)
    gp = jnp.max(p1, axis=-1, keepdims=True)
    gidx = first_lane_of(p1 == gp)
    lo = RL_E + N_EPG * gidx
    l2 = jnp.where(jnp.logical_and(lane >= lo, lane < lo + N_EPG), logits, NEG_INF)
    va = jnp.max(l2, axis=-1, keepdims=True)
    ia = first_lane_of(l2 == va)
    l2b = jnp.where(lane == ia, NEG_INF, l2)
    vb = jnp.max(l2b, axis=-1, keepdims=True)
    ib = first_lane_of(l2b == vb)
    eb = jnp.exp(vb - va)
    wa = 1.0 / (1.0 + eb)
    wb = eb / (1.0 + eb)

    is_a = lane == ia
    is_b = lane == ib
    onehot = jnp.where(jnp.logical_or(is_a, is_b), 1.0, 0.0)
    ri = lax.broadcasted_iota(jnp.int32, (tr, tr), 0)
    ci = lax.broadcasted_iota(jnp.int32, (tr, tr), 1)
    tri = jnp.where(ri >= ci, 1.0, 0.0).astype(BF16)
    incl = _dot(tri, onehot.astype(BF16))
    excl = incl - onehot + carry[...]
    rank_a = jnp.sum(jnp.where(is_a, excl, 0.0), axis=-1, keepdims=True)
    rank_b = jnp.sum(jnp.where(is_b, excl, 0.0), axis=-1, keepdims=True)
    carry[...] = carry[...] + incl[tr - 1:tr, :]
    cnt_ref[...] = carry[...]

    info = jnp.where(lane_i == R_EA, ia - RL_E, 0.0)
    info = jnp.where(lane_i == R_EB, ib - RL_E, info)
    info = jnp.where(lane_i == R_RA, rank_a, info)
    info = jnp.where(lane_i == R_RB, rank_b, info)
    info = jnp.where(lane_i == R_GA, gp * wa, info)
    info = jnp.where(lane_i == R_GB, gp * wb, info)
    info_ref[...] = info


def _row_tile(n, candidates):
    for t in candidates:
        if n % t == 0:
            return t
    raise ValueError(f"no row tile for {n} rows among {candidates}")


def _router(hmid, rp):
    n, d = hmid.shape
    assert d == TOK_TILE_ROWS * LANES
    tr = _row_tile(n, (512, 384, 256, 128))
    return pl.pallas_call(
        _router_kernel,
        out_shape=(jax.ShapeDtypeStruct((n * TOK_TILE_ROWS, LANES), F32),
                   jax.ShapeDtypeStruct((n, LANES), F32),
                   jax.ShapeDtypeStruct((1, LANES), F32)),
        grid=(n // tr,),
        in_specs=[pl.BlockSpec((tr, d), lambda i: (i, 0)),
                  pl.BlockSpec((1, d), lambda i: (0, 0)),
                  pl.BlockSpec((d, LANES), lambda i: (0, 0)),
                  pl.BlockSpec((d, LANES), lambda i: (0, 0)),
                  pl.BlockSpec((1, LANES), lambda i: (0, 0))],
        out_specs=(pl.BlockSpec((tr * TOK_TILE_ROWS, LANES), lambda i: (i, 0)),
                   pl.BlockSpec((tr, LANES), lambda i: (i, 0)),
                   pl.BlockSpec((1, LANES), lambda i: (0, 0))),
        scratch_shapes=[pltpu.VMEM((1, LANES), F32)],
        compiler_params=pltpu.CompilerParams(
            dimension_semantics=("arbitrary",), vmem_limit_bytes=VMEM_LIMIT),
        name="router",
    )(hmid, rp["nf"], rp["whi"], rp["wmid"], rp["br"])


def _prep_router_params(norm_ffn, w_r1, b_r1, w_r2, b_r2):
    d = w_r1.shape[1]
    w = jnp.concatenate([w_r1.reshape(d, N_EGROUPS).astype(F32), w_r2.reshape(d, N_EXPERTS).astype(F32),
                         jnp.zeros((d, LANES - RL_E - N_EXPERTS), F32)], axis=1)
    whi = w.astype(BF16)
    wmid = (w - whi.astype(F32)).astype(BF16)
    br = jnp.concatenate([b_r1.reshape(1, N_EGROUPS).astype(F32), b_r2.reshape(1, N_EXPERTS).astype(F32),
                          jnp.zeros((1, LANES - RL_E - N_EXPERTS), F32)], axis=1)
    return dict(nf=norm_ffn.reshape(1, d).astype(F32), whi=whi, wmid=wmid, br=br)


FFN_TM = 256
N_GATHER_SLOTS = 3


def _start_tile_gather(first_row_of, n_rows, src_hbm, dst, sem, priority_of):
    for r in range(n_rows):
        start = pl.multiple_of(first_row_of(r), TOK_TILE_ROWS)
        pltpu.make_async_copy(src_hbm.at[pl.ds(start, TOK_TILE_ROWS), :],
                              dst.at[pl.ds(r * TOK_TILE_ROWS, TOK_TILE_ROWS), :],
                              sem).start(priority=priority_of(r))


def _wait_tile_gather(n_rows, src_hbm, dst, sem):
    pltpu.make_async_copy(src_hbm.at[pl.ds(0, n_rows * TOK_TILE_ROWS), :], dst, sem).wait()


TAB_OFF, TAB_CNT, TAB_TILE_EXPERT, TAB_NVALID = 0, 1, 2, 3
TAB_LANES = 2 * LANES


def _ffn_kernel(tab_ref, pa_ref, pb_ref, xn_hbm, wg_ref, wu_ref, wd_ref, ys_ref,
                src, xbuf, wbf, sem, *, n_tokens, tm):
    i = pl.program_id(0)
    n_valid = tab_ref[TAB_NVALID, 0]
    slot = lax.rem(i, N_GATHER_SLOTS)
    gather_priority = lambda r: 1

    @pl.when(i == 0)
    def _build_source_rows():
        for e in range(N_EXPERTS):
            cnt_e = tab_ref[TAB_CNT, RL_E + e]
            first = tab_ref[TAB_OFF, RL_E + e] + cnt_e
            n_pad = lax.rem(tm - lax.rem(cnt_e, tm), tm)

            def pad_body(r, carry, first=first):
                src[first + r] = 0
                return carry
            lax.fori_loop(0, n_pad, pad_body, 0)

        def body(t, carry):
            first_row = t * TOK_TILE_ROWS
            src[pa_ref[t]] = first_row
            src[pb_ref[t]] = first_row
            return carry
        lax.fori_loop(0, n_tokens, body, 0, unroll=8)
        _start_tile_gather(lambda r: src[r], tm, xn_hbm, xbuf.at[0], sem.at[0], gather_priority)
        second = jnp.where(n_valid > 1, tm, 0)
        _start_tile_gather(lambda r: src[second + r], tm, xn_hbm, xbuf.at[1], sem.at[1], gather_priority)

    changed = jnp.logical_or(i == 0, tab_ref[TAB_TILE_EXPERT, i]
                             != tab_ref[TAB_TILE_EXPERT, jnp.maximum(i - 1, 0)])

    @pl.when(jnp.logical_and(changed, i < n_valid))
    def _cast_weights():
        wbf[0] = wg_ref[...].astype(BF16)
        wbf[1] = wu_ref[...].astype(BF16)
        wbf[2] = wd_ref[...].astype(BF16)

    @pl.when(i < n_valid)
    def _compute():
        _wait_tile_gather(tm, xn_hbm, xbuf.at[slot], sem.at[slot])
        x = _load_token_tiles(xbuf.at[slot], tm).astype(BF16)
        hg = _dot(x, wbf[0])
        hu = _dot(x, wbf[1])
        y = _dot((_silu(hg) * hu).astype(BF16), wbf[2])
        base = jnp.where(i + 2 < n_valid, i + 2, 0) * tm
        ahead = lax.rem(i + 2, N_GATHER_SLOTS)
        _start_tile_gather(lambda r: src[base + r], tm, xn_hbm, xbuf.at[ahead], sem.at[ahead],
                           gather_priority)
        _store_token_tiles(ys_ref, y)

    @pl.when(i == n_valid - 1)
    def _drain():
        for k in (1, 2):
            s = lax.rem(i + k, N_GATHER_SLOTS)
            _wait_tile_gather(tm, xn_hbm, xbuf.at[s], sem.at[s])

    @pl.when(i >= n_valid)
    def _pad():
        ys_ref[...] = jnp.zeros_like(ys_ref)


def _expert_ffn(xn_tiles, tab, pos_a, pos_b, n_tiles, wg, wu, wd):
    n = pos_a.shape[0]
    tm = FFN_TM
    d, dff = wg.shape[1], wg.shape[2]
    rows = tm * TOK_TILE_ROWS
    idx = lambda i, tab, pa, pb: (tab[TAB_TILE_EXPERT, i], 0, 0)
    grid_spec = pltpu.PrefetchScalarGridSpec(
        num_scalar_prefetch=3,
        grid=(n_tiles,),
        in_specs=[pl.BlockSpec(memory_space=pl.ANY),
                  pl.BlockSpec((None, d, dff), idx),
                  pl.BlockSpec((None, d, dff), idx),
                  pl.BlockSpec((None, dff, d), idx)],
        out_specs=pl.BlockSpec((rows, LANES), lambda i, tab, pa, pb: (i, 0)),
        scratch_shapes=[pltpu.SMEM((n_tiles * tm,), jnp.int32),
                        pltpu.VMEM((N_GATHER_SLOTS, rows, LANES), F32),
                        pltpu.VMEM((3, d, dff), BF16),
                        pltpu.SemaphoreType.DMA((N_GATHER_SLOTS,))],
    )
    return pl.pallas_call(
        functools.partial(_ffn_kernel, n_tokens=n, tm=tm),
        out_shape=jax.ShapeDtypeStruct((n_tiles * rows, LANES), F32),
        grid_spec=grid_spec,
        compiler_params=pltpu.CompilerParams(
            dimension_semantics=("arbitrary",), vmem_limit_bytes=VMEM_LIMIT),
        name="expert_ffn",
    )(tab, pos_a, pos_b, xn_tiles, wg, wu, wd)


def _positions_kernel(info_ref, cnt_ref, pos_ref, tab_ref, *, tm, chunk):
    lane_i = lax.broadcasted_iota(jnp.int32, (1, LANES), 1)
    lane = lane_i.astype(F32)
    is_expert = jnp.logical_and(lane_i >= RL_E, lane_i < RL_E + N_EXPERTS)
    cnt = jnp.where(is_expert, cnt_ref[...], 0.0)
    padded = jnp.floor((cnt + (tm - 1)) / tm) * tm
    ri = lax.broadcasted_iota(jnp.int32, (LANES, LANES), 0)
    ci = lax.broadcasted_iota(jnp.int32, (LANES, LANES), 1)
    before = jnp.where(ri < ci, 1.0, 0.0).astype(BF16)
    hi, mid, lo = _split3(jnp.broadcast_to(padded, (SUBLANES, LANES)))
    off = ((_dot(hi, before) + _dot(mid, before)) + _dot(lo, before))[0:1, :]
    pick = jnp.where(lax.broadcasted_iota(jnp.int32, (SUBLANES, LANES), 0) == lane_i, 1.0, 0.0).astype(BF16)

    total = jnp.sum(padded, axis=-1, keepdims=True)
    tile_row = lax.broadcasted_iota(jnp.int32, (TAB_LANES, 1), 0).astype(F32) * tm
    ends = off + padded
    done = jnp.logical_and(is_expert, ends <= jnp.minimum(tile_row, total - 1.0))
    te_col = jnp.sum(jnp.where(done, 1.0, 0.0), axis=-1, keepdims=True)
    te_rows = _dot_nt(pick, jnp.where(lane_i == 0, te_col, 0.0).astype(BF16))
    tab_ref[...] = jnp.zeros_like(tab_ref)
    tab_ref[TAB_OFF:TAB_OFF + 1, 0:LANES] = off.astype(jnp.int32)
    tab_ref[TAB_CNT:TAB_CNT + 1, 0:LANES] = cnt.astype(jnp.int32)
    tab_ref[TAB_TILE_EXPERT:TAB_TILE_EXPERT + 1, :] = te_rows[0:1, :].astype(jnp.int32)
    tab_ref[TAB_NVALID:TAB_NVALID + 1, 0:LANES] = jnp.broadcast_to(total / tm, (1, LANES)).astype(jnp.int32)

    n = info_ref.shape[0]
    for c0 in range(0, n, chunk):
        blk = info_ref[c0:c0 + chunk, :]
        lane_a = blk[:, R_EA:R_EA + 1] + RL_E
        lane_b = blk[:, R_EB:R_EB + 1] + RL_E
        pos_a = blk[:, R_RA:R_RA + 1] + jnp.sum(jnp.where(lane == lane_a, off, 0.0), axis=-1, keepdims=True)
        pos_b = blk[:, R_RB:R_RB + 1] + jnp.sum(jnp.where(lane == lane_b, off, 0.0), axis=-1, keepdims=True)
        z_hi, z_mid, z_lo = _split3(jnp.where(lane_i == 0, pos_a, jnp.where(lane_i == 1, pos_b, 0.0)))
        rows = (_dot_nt(pick, z_hi) + _dot_nt(pick, z_mid)) + _dot_nt(pick, z_lo)
        pos_ref[:, c0:c0 + chunk] = rows.astype(jnp.int32)


def _routing_tables(info, counts, n_tiles, tm):
    n = info.shape[0]
    groups = n // LANES
    assert n_tiles <= TAB_LANES
    chunk = LANES * max(g for g in range(1, 65) if groups % g == 0)
    pos, tab = pl.pallas_call(
        functools.partial(_positions_kernel, tm=tm, chunk=chunk),
        out_shape=(jax.ShapeDtypeStruct((SUBLANES, n), jnp.int32),
                   jax.ShapeDtypeStruct((SUBLANES, TAB_LANES), jnp.int32)),
        grid=(1,),
        in_specs=[pl.BlockSpec((n, LANES), lambda i: (0, 0)), pl.BlockSpec((1, LANES), lambda i: (0, 0))],
        out_specs=(pl.BlockSpec((SUBLANES, n), lambda i: (0, 0)),
                   pl.BlockSpec((SUBLANES, TAB_LANES), lambda i: (0, 0))),
        compiler_params=pltpu.CompilerParams(
            dimension_semantics=("arbitrary",), vmem_limit_bytes=VMEM_LIMIT),
        name="positions",
    )(info, counts)
    return tab, pos[0], pos[1]


def _combine_kernel(pa_ref, pb_ref, h_ref, info_ref, ys_hbm, nfin_ref, yp_ref, ysm_ref,
                    buf_a, buf_b, sem, *, n_prompt_tiles):
    i = pl.program_id(0)
    n_steps = pl.num_programs(0)
    tt = h_ref.shape[0]
    slot = lax.rem(i, N_GATHER_SLOTS)

    def start(tile, s):
        base = tile * tt
        _start_tile_gather(lambda r: pa_ref[base + r] * TOK_TILE_ROWS, tt, ys_hbm, buf_a.at[s],
                           sem.at[s], lambda r: 0)
        _start_tile_gather(lambda r: pb_ref[base + r] * TOK_TILE_ROWS, tt, ys_hbm, buf_b.at[s],
                           sem.at[s], lambda r: 1)

    @pl.when(i == 0)
    def _first():
        start(0, 0)
        start(lax.rem(1, n_steps), 1)

    _wait_tile_gather(tt, ys_hbm, buf_a.at[slot], sem.at[slot])
    _wait_tile_gather(tt, ys_hbm, buf_b.at[slot], sem.at[slot])
    info = info_ref[...]
    rows_a = _load_token_tiles(buf_a.at[slot], tt)
    rows_b = _load_token_tiles(buf_b.at[slot], tt)
    x = h_ref[...]
    start(lax.rem(i + 2, n_steps), lax.rem(i + 2, N_GATHER_SLOTS))
    h = x + info[:, R_GA:R_GA + 1] * rows_a + info[:, R_GB:R_GB + 1] * rows_b
    y = h * _rms_scale(h) * nfin_ref[...]

    @pl.when(i < n_prompt_tiles)
    def _prompt():
        yp_ref[...] = y

    @pl.when(i >= n_prompt_tiles)
    def _sample():
        ysm_ref[...] = y

    @pl.when(i == n_steps - 1)
    def _drain():
        for k in (1, 2):
            s = lax.rem(i + k, N_GATHER_SLOTS)
            _wait_tile_gather(tt, ys_hbm, buf_a.at[s], sem.at[s])
            _wait_tile_gather(tt, ys_hbm, buf_b.at[s], sem.at[s])


def _combine(hmid, info, ys, pos_a, pos_b, nfin, n_prompt):
    n, d = hmid.shape
    tt = CHUNK
    n_prompt_tiles = n_prompt // tt
    n_sample = n - n_prompt
    grid_spec = pltpu.PrefetchScalarGridSpec(
        num_scalar_prefetch=2,
        grid=(n // tt,),
        in_specs=[pl.BlockSpec((tt, d), lambda i, pa, pb: (i, 0)),
                  pl.BlockSpec((tt, LANES), lambda i, pa, pb: (i, 0)),
                  pl.BlockSpec(memory_space=pl.ANY),
                  pl.BlockSpec((1, d), lambda i, pa, pb: (0, 0))],
        out_specs=(pl.BlockSpec((tt, d), lambda i, pa, pb: (jnp.minimum(i, n_prompt_tiles - 1), 0)),
                   pl.BlockSpec((tt, d), lambda i, pa, pb: (jnp.maximum(i - n_prompt_tiles, 0), 0))),
        scratch_shapes=[pltpu.VMEM((N_GATHER_SLOTS, tt * TOK_TILE_ROWS, LANES), F32),
                        pltpu.VMEM((N_GATHER_SLOTS, tt * TOK_TILE_ROWS, LANES), F32),
                        pltpu.SemaphoreType.DMA((N_GATHER_SLOTS,))],
    )
    return pl.pallas_call(
        functools.partial(_combine_kernel, n_prompt_tiles=n_prompt_tiles),
        out_shape=(jax.ShapeDtypeStruct((n_prompt, d), F32),
                   jax.ShapeDtypeStruct((n_sample, d), F32)),
        grid_spec=grid_spec,
        compiler_params=pltpu.CompilerParams(
            dimension_semantics=("arbitrary",), vmem_limit_bytes=VMEM_LIMIT),
        name="combine",
    )(pos_a, pos_b, hmid, info, ys, nfin)


def _moe_and_final_norm(hmid, n_prompt, rp, wg, wu, wd, nfin):
    n = hmid.shape[0]
    tm = FFN_TM
    n_tiles = (2 * n + N_EXPERTS * (tm - 1)) // tm
    xn, info, counts = _router(hmid, rp)
    tab, pos_a, pos_b = _routing_tables(info, counts, n_tiles, tm)
    ys = _expert_ffn(xn, tab, pos_a, pos_b, n_tiles, wg, wu, wd)
    return _combine(hmid, info, ys, pos_a, pos_b, nfin, n_prompt)


def kernel(x_prompt, x_sample, state_mlstm_C, state_mlstm_n, state_mlstm_m, state_mlstm_conv, state_ssm, state_ssm_conv, meta_tokens, norm_mix, w_in, conv_a_w, conv_a_b, w_q, w_k, w_v, b_i, b_f, norm_a, conv_b_w, conv_b_b, dt_bias, a_log, d_skip, norm_b, w_out, norm_ffn, w_r1, b_r1, w_r2, b_r2, w_gate, w_up, w_down, norm_final):
    bsz, seq, d = x_prompt.shape
    nb = x_sample.shape[0]
    d_a = H_A * DH_A
    conv_b = H_B * HD_B + 2 * G_B * N_STATE
    assert w_in.shape[0] == 1 and x_sample.shape[1] == 1 and seq % CHUNK == 0 and nb == CHUNK
    mp = _prep_mixer_params(norm_mix, w_in, conv_a_w, conv_a_b, w_q, w_k, w_v, b_i, b_f, norm_a,
                            conv_b_w, conv_b_b, dt_bias, a_log, d_skip, norm_b, w_out)
    rp = _prep_router_params(norm_ffn, w_r1, b_r1, w_r2, b_r2)
    xmeta = jnp.concatenate([jnp.zeros((CHUNK - N_META, d), F32), meta_tokens.astype(F32)], 0)

    hmid, p_c, p_n, p_m, p_ca, p_s, p_cb = _prompt_mixer(x_prompt.astype(F32), xmeta, mp, nb)
    m0 = jnp.pad(state_mlstm_m.reshape(nb, H_A).astype(F32), ((0, 0), (0, LANES - H_A)))
    hmid, s_c, s_n, s_m, s_ca, s_s, s_cb = _sample_mixer(
        x_sample.reshape(nb, d).astype(F32),
        state_mlstm_C.reshape(nb, H_A, DH_A, DH_A).astype(F32),
        state_mlstm_n.reshape(nb, d_a).astype(F32),
        m0,
        state_mlstm_conv.reshape(nb, (CONV_W - 1) * d_a).astype(F32),
        state_ssm.reshape(nb, H_B // 2, 2 * HD_B, N_STATE).astype(F32),
        state_ssm_conv.reshape(nb, (CONV_W - 1) * conv_b).astype(F32),
        mp, hmid, bsz * seq)

    wshape = w_gate.shape[1:]
    y_p, y_s = _moe_and_final_norm(
        hmid, bsz * seq, rp, w_gate.reshape(wshape).astype(F32), w_up.reshape(wshape).astype(F32),
        w_down.reshape(w_down.shape[1:]).astype(F32), norm_final.reshape(1, d).astype(F32))

    return (y_p.reshape(bsz, seq, d), y_s.reshape(nb, 1, d),
            p_c.reshape(1, bsz, H_A, DH_A, DH_A), p_n.reshape(1, bsz, H_A, DH_A),
            p_m[:, 0, :H_A].reshape(1, bsz, H_A), p_ca.reshape(1, bsz, CONV_W - 1, d_a),
            p_s.reshape(1, bsz, H_B, HD_B, N_STATE), p_cb.reshape(1, bsz, CONV_W - 1, conv_b),
            s_c.reshape(1, nb, H_A, DH_A, DH_A), s_n.reshape(1, nb, H_A, DH_A),
            s_m[:, :H_A].reshape(1, nb, H_A), s_ca.reshape(1, nb, CONV_W - 1, d_a),
            s_s.reshape(1, nb, H_B, HD_B, N_STATE), s_cb.reshape(1, nb, CONV_W - 1, conv_b))
```

```python
import functools
import math

import jax
import jax.numpy as jnp
from jax import lax
from jax.experimental import pallas as pl
from jax.experimental.pallas import tpu as pltpu

F32 = jnp.float32
BF16 = jnp.bfloat16

EPS = 1e-6
N_META = 16
CONV_W = 4
CHUNK = 128
H_A = 8
DH_A = 128
H_B = 16
HD_B = 64
N_STATE = 128
G_B = 2
N_EGROUPS = 4
N_EPG = 4
N_EXPERTS = 16
LANES = 128
SUBLANES = 8
CONV_HDR = SUBLANES
VMEM_LIMIT = 60 * 1024 * 1024

L_F = 0
L_DTA = 8
L_I = 24
L_DT = 32

NEG_INF = float("-inf")


def _dot(a, b):
    return jnp.dot(a, b, preferred_element_type=F32)


def _dot_nt(a, b):
    return lax.dot_general(a, b, (((1,), (1,)), ((), ())), preferred_element_type=F32)


def _dot_tn(a, b):
    return lax.dot_general(a, b, (((0,), (0,)), ((), ())), preferred_element_type=F32)


def _split3(x):
    hi = x.astype(BF16)
    r = x - hi.astype(F32)
    mid = r.astype(BF16)
    lo = (r - mid.astype(F32)).astype(BF16)
    return hi, mid, lo


def _silu(x):
    return x * jax.nn.sigmoid(x)


def _softplus_parts(x):
    t = jnp.log1p(jnp.exp(-jnp.abs(x)))
    return jnp.maximum(x, 0.0) + t, jnp.minimum(x, 0.0) - t


def _rms_scale(x):
    return lax.rsqrt(jnp.mean(x * x, axis=-1, keepdims=True) + EPS)


TOK_TILE_ROWS = SUBLANES


def _store_token_tiles(ref, x):
    n = x.shape[0]
    for j in range(TOK_TILE_ROWS):
        ref[pl.ds(j, n, stride=TOK_TILE_ROWS), :] = x[:, j * LANES:(j + 1) * LANES]


def _causal_conv(x, tail, w_ref, b_ref):
    n_tail = tail.shape[0]
    row = lax.broadcasted_iota(jnp.int32, (n_tail, 1), 0)
    acc = w_ref[CONV_W - 1:CONV_W, :] * x + b_ref[...]
    for k in range(1, CONV_W):
        rolled = pltpu.roll(x, k, axis=0)
        head = jnp.where(row < k, pltpu.roll(tail, k, axis=0), rolled[0:n_tail])
        shifted = jnp.concatenate([head, rolled[n_tail:]], axis=0)
        acc = acc + w_ref[CONV_W - 1 - k:CONV_W - k, :] * shifted
    return acc


def _load_token_tiles(ref, n):
    return jnp.concatenate(
        [ref[pl.ds(j, n, stride=TOK_TILE_ROWS), :] for j in range(TOK_TILE_ROWS)], axis=1)


PROMPT_ROWS = 4


def _prompt_rows_kernel(xmeta_ref, xp_ref, nmix_ref, wcat_ref, bsm_ref, alog_ref,
                        cwa_ref, cba_ref, cwb_ref, cbb_ref, wq_ref, wk_ref, wv_ref,
                        na_ref, nb_ref, dsk_ref, wout_ref,
                        hmid_hbm, c_ref, n_ref, m_ref, conva_ref, s_ref, convb_ref,
                        xa_buf, xbc_buf, y_buf, merged, hout, sem, *, seq, n_prompt_rows):
    p = pl.program_id(0)
    c = pl.program_id(1)
    last_p = pl.num_programs(0) - 1
    last_c = pl.num_programs(1) - 1
    T = CHUNK
    RB = xp_ref.shape[0]
    d_a = H_A * DH_A
    d_b = H_B * HD_B
    conv_b = d_b + 2 * G_B * N_STATE

    def out_copy(r, row0):
        return pltpu.make_async_copy(hout.at[r], hmid_hbm.at[pl.ds(row0, T), :], sem.at[r])

    @pl.when(c == 0)
    def _init():
        c_ref[...] = jnp.zeros_like(c_ref)
        n_ref[...] = jnp.zeros_like(n_ref)
        m_ref[...] = jnp.zeros_like(m_ref)
        s_ref[...] = jnp.zeros_like(s_ref)
        xa_buf[...] = jnp.zeros_like(xa_buf)
        xbc_buf[...] = jnp.zeros_like(xbc_buf)

    @pl.when(jnp.logical_and(p == 0, c == 0))
    def _clear_sample_rows():
        hout[0] = jnp.zeros((T, hout.shape[2]), F32)
        cp = out_copy(0, n_prompt_rows)
        cp.start()
        cp.wait()

    row = lax.broadcasted_iota(jnp.int32, (T, 1), 0)
    valid = jnp.logical_or(c > 0, row >= T - N_META)
    xs_in = [jnp.where(c == 0, xmeta_ref[...], xp_ref[r]) for r in range(RB)]
    x2 = jnp.concatenate(xs_in, axis=0)
    hn = (x2 * _rms_scale(x2) * nmix_ref[...]).astype(BF16)

    lane = lax.broadcasted_iota(jnp.int32, (1, LANES), 1)
    lane_f = lane < L_DTA
    lane_dta = jnp.logical_and(lane >= L_DTA, lane < L_I)
    lane_i = jnp.logical_and(lane >= L_I, lane < L_DT)
    lane_dt = jnp.logical_and(lane >= L_DT, lane < L_DT + H_B)
    ri = lax.broadcasted_iota(jnp.int32, (T, T), 0)
    ci = lax.broadcasted_iota(jnp.int32, (T, T), 1)
    causal = ri >= ci
    tri = jnp.where(causal, 1.0, 0.0).astype(BF16)
    a_neg = jnp.where(lane_dta, -jnp.exp(alog_ref[...]), 0.0)
    left = lane < HD_B
    top = lax.broadcasted_iota(jnp.int32, (LANES, 1), 0) < HD_B

    off_small = 2 * d_a + d_b + conv_b
    pre2 = _dot(hn, wcat_ref[:, off_small:]) + bsm_ref[...]
    xa2 = _dot(hn, wcat_ref[:, 0:d_a])

    gcols, grows, xcs, xabs = [], [], [], []

    def gate_tables():
        for r in range(RB):
            pre = pre2[r * T:(r + 1) * T]
            sp, lsig = _softplus_parts(pre)
            to_cum = jnp.where(lane_f, lsig, jnp.where(lane_dta, sp * a_neg, 0.0))
            to_cum = jnp.where(valid, to_cum, 0.0)
            hi, mid, lo = _split3(to_cum)
            cum = _dot(tri, hi) + _dot(tri, mid) + _dot(tri, lo)
            extra = jnp.where(lane_i, jnp.where(valid, pre, NEG_INF),
                              jnp.where(lane_dt, jnp.where(valid, sp, 0.0), 0.0))
            gcol = cum + extra
            gcols.append(gcol)
            grows.append(gcol.T)

    gate_tables()
    for r in range(RB):
        rs = slice(r * T, (r + 1) * T)
        xa = xa2[rs]
        xc = _causal_conv(xa, xa_buf[r], cwa_ref, cba_ref)
        xa_buf[r] = xa[T - CONV_HDR:T, :]
        conva_ref[r] = xa[T - 3:T, :]
        xcs.append(_silu(xc).astype(BF16))
        xabs.append(xa.astype(BF16))

    items = [(r, h) for h in range(H_A) for r in range(RB)]
    hsl = lambda h: slice(h * DH_A, (h + 1) * DH_A)
    m_alls = [m_ref[r] for r in range(RB)]
    m_news = list(m_alls)
    qs, ks, vs, qks, st, dd = {}, {}, {}, {}, {}, {}

    def stage_qkv(it):
        r, h = it
        qs[it] = _dot(xcs[r][:, hsl(h)], wq_ref[h]).astype(BF16)
        ks[it] = _dot(xcs[r][:, hsl(h)], wk_ref[h]) * (DH_A ** -0.5)
        vs[it] = _dot(xabs[r][:, hsl(h)], wv_ref[h]).astype(BF16)

    def stage_qk(it):
        qks[it] = _dot_nt(qs[it], ks[it].astype(BF16))

    pairs_per_group = H_B // G_B // 2
    groups = [(r, g) for g in range(G_B) for r in range(RB)]
    proj, xbcs, bgs, cgs, cbs = {}, [], {}, {}, {}

    def project(name, lo, hi):
        proj[name] = _dot(hn, wcat_ref[:, lo:hi])

    def ssd_inputs():
        for r in range(RB):
            xbc = proj["xbc"][r * T:(r + 1) * T]
            xbc_c = _causal_conv(xbc, xbc_buf[r], cwb_ref, cbb_ref)
            xbc_buf[r] = xbc[T - CONV_HDR:T, :]
            convb_ref[r] = xbc[T - 3:T, :]
            xbcs.append(_silu(xbc_c))
        for r, g in groups:
            bgs[(r, g)] = xbcs[r][:, d_b + g * N_STATE:d_b + (g + 1) * N_STATE].astype(BF16)
            cgs[(r, g)] = xbcs[r][:, d_b + (G_B + g) * N_STATE:d_b + (G_B + g + 1) * N_STATE].astype(BF16)
            cbs[(r, g)] = _dot_nt(cgs[(r, g)], bgs[(r, g)])

    def stage_weights(it):
        r, h = it
        gcol, grow = gcols[r], grows[r]
        b_col = gcol[:, L_F + h:L_F + h + 1]
        i_col = gcol[:, L_I + h:L_I + h + 1]
        b_row = grow[L_F + h:L_F + h + 1, :]
        i_row = grow[L_I + h:L_I + h + 1, :]
        m0 = m_alls[r][:, h:h + 1]
        dmat = jnp.where(causal, b_col - (b_row - i_row), NEG_INF)
        m_inter = b_col + m0
        m = jnp.maximum(m_inter, jnp.max(dmat, axis=-1, keepdims=True))
        w_inter = jnp.exp(m_inter - m)
        s = qks[it] * jnp.exp(dmat - m)
        n0 = n_ref[r, h:h + 1, :]
        den = (jnp.sum(s, axis=-1, keepdims=True)
               + w_inter * jnp.sum(qs[it].astype(F32) * n0, axis=-1, keepdims=True))
        m_last = m[T - 1:T, :]
        b_last = b_col[T - 1:T, :]
        dec = jnp.exp(b_last + m0 - m_last)
        kw = ks[it] * jnp.exp(b_last - b_col + i_col - m_last)
        n_ref[r, h:h + 1, :] = dec * n0 + jnp.sum(kw, axis=0, keepdims=True)
        m_news[r] = jnp.where(lane == h, m_last, m_news[r])
        st[it] = (s.astype(BF16), kw.astype(BF16), w_inter,
                  jnp.maximum(jnp.abs(den), jnp.exp(-m)), dec)

    def stage_readout(it):
        r, h = it
        s_b, kw_b, w_inter, den, dec = st[it]
        c0 = c_ref[r, h]
        num = _dot(s_b, vs[it]) + w_inter * _dot(qs[it], c0.astype(BF16))
        c_ref[r, h] = dec * c0 + _dot_tn(kw_b, vs[it])
        dd[it] = num / den

    def stage_head_out(it):
        r, h = it
        hh = dd[it]
        hh = hh * _rms_scale(hh) * na_ref[:, hsl(h)]
        merged[r * T:(r + 1) * T, hsl(h)] = (
            hh * jax.nn.sigmoid(proj["za"][r * T:(r + 1) * T, hsl(h)])).astype(BF16)

    pairs = [(r, pi) for pi in range(H_B // 2) for r in range(RB)]
    psl = lambda pi: slice(pi * LANES, (pi + 1) * LANES)
    sw = {}

    def stage_decay(pr):
        r, pi = pr
        g = pi // pairs_per_group
        gcol, grow = gcols[r], grows[r]
        xpair = xbcs[r][:, psl(pi)]
        scs, a_cols, w_cols, a_lasts = [], [], [], []
        for j in (2 * pi, 2 * pi + 1):
            a_col = gcol[:, L_DTA + j:L_DTA + j + 1]
            a_row = grow[L_DTA + j:L_DTA + j + 1, :]
            dt_col = gcol[:, L_DT + j:L_DT + j + 1]
            dt_row = grow[L_DT + j:L_DT + j + 1, :]
            decay = jnp.exp(jnp.where(causal, a_col - a_row, NEG_INF))
            scs.append((cbs[(r, g)] * decay * dt_row).astype(BF16))
            a_last = a_col[T - 1:T, :]
            a_cols.append(a_col)
            a_lasts.append(a_last)
            w_cols.append(jnp.exp(a_last - a_col) * dt_col)
        sw[pr] = (scs, xpair.astype(BF16),
                  (xpair * jnp.where(left, w_cols[0], w_cols[1])).astype(BF16),
                  jnp.exp(jnp.where(left, a_cols[0], a_cols[1])),
                  jnp.exp(jnp.where(top, a_lasts[0], a_lasts[1])))
    def stage_pair_out(pr):
        r, pi = pr
        g = pi // pairs_per_group
        scs, xpb, xw, ea, ea_last = sw[pr]
        s0 = s_ref[r, pi]
        y = jnp.where(left, _dot(scs[0], xpb), _dot(scs[1], xpb))
        y = y + ea * _dot_nt(cgs[(r, g)], s0.astype(BF16))
        s_ref[r, pi] = ea_last * s0 + _dot_tn(xw, bgs[(r, g)])
        y = y + dsk_ref[:, psl(pi)] * xbcs[r][:, psl(pi)]
        y_buf[r, :, psl(pi)] = y * _silu(proj["zb"][r * T:(r + 1) * T, psl(pi)])

    def each(stage, seq):
        for e in seq:
            stage(e)

    each(stage_qkv, items)
    project("xbc", 2 * d_a + d_b, off_small)
    each(stage_qk, items)
    project("za", d_a, 2 * d_a)
    ssd_inputs()
    each(stage_weights, items)
    each(stage_readout, items)
    project("zb", 2 * d_a, 2 * d_a + d_b)
    for r in range(RB):
        m_ref[r] = m_news[r]
    each(stage_head_out, items)
    each(stage_decay, pairs)
    each(stage_pair_out, pairs)
    gw = d_b // G_B
    for r in range(RB):
        for g in range(G_B):
            yg = y_buf[r, :, g * gw:(g + 1) * gw]
            merged[r * T:(r + 1) * T, d_a + g * gw:d_a + (g + 1) * gw] = (
                yg * _rms_scale(yg) * nb_ref[:, g * gw:(g + 1) * gw]).astype(BF16)

    @pl.when(c > 0)
    def _out():
        out2 = x2 + _dot(merged[...], wout_ref[...])

        @pl.when(jnp.logical_or(c > 1, p > 0))
        def _wait_previous():
            for r in range(RB):
                out_copy(r, 0).wait()

        for r in range(RB):
            hout[r] = out2[r * T:(r + 1) * T]
            out_copy(r, (p * RB + r) * seq + (c - 1) * T).start()

        @pl.when(jnp.logical_and(p == last_p, c == last_c))
        def _drain():
            for r in range(RB):
                out_copy(r, 0).wait()


def _const_spec(shape):
    nd = len(shape)
    return pl.BlockSpec(shape, lambda b, c, _nd=nd: (0,) * _nd)


def _prompt_mixer(x_prompt, xmeta, p, n_extra_rows):
    bsz, seq, d = x_prompt.shape
    assert n_extra_rows == CHUNK and seq % CHUNK == 0
    n_chunks = seq // CHUNK + 1
    cps = seq // CHUNK
    d_a = H_A * DH_A
    conv_b = H_B * HD_B + 2 * G_B * N_STATE
    consts = [p["nmix"], p["wcat"], p["bsm"], p["alog"], p["cwa"], p["cba"], p["cwb"], p["cbb"],
              p["wq"], p["wk"], p["wv"], p["na"], p["nb"], p["dsk"], p["wout"]]
    rb = PROMPT_ROWS
    assert bsz % rb == 0
    in_specs = [_const_spec(xmeta.shape),
                pl.BlockSpec((rb, CHUNK, d), lambda b, c: (b, jnp.maximum(c - 1, 0), 0))]
    in_specs += [_const_spec(a.shape) for a in consts]
    out_shape = (
        jax.ShapeDtypeStruct((bsz * seq + n_extra_rows, d), F32),
        jax.ShapeDtypeStruct((bsz, H_A, DH_A, DH_A), F32),
        jax.ShapeDtypeStruct((bsz, H_A, DH_A), F32),
        jax.ShapeDtypeStruct((bsz, 1, LANES), F32),
        jax.ShapeDtypeStruct((bsz, CONV_W - 1, d_a), F32),
        jax.ShapeDtypeStruct((bsz, H_B // 2, 2 * HD_B, N_STATE), F32),
        jax.ShapeDtypeStruct((bsz, CONV_W - 1, conv_b), F32),
    )
    out_specs = (
        pl.BlockSpec(memory_space=pl.ANY),
        pl.BlockSpec((rb, H_A, DH_A, DH_A), lambda b, c: (b, 0, 0, 0)),
        pl.BlockSpec((rb, H_A, DH_A), lambda b, c: (b, 0, 0)),
        pl.BlockSpec((rb, 1, LANES), lambda b, c: (b, 0, 0)),
        pl.BlockSpec((rb, CONV_W - 1, d_a), lambda b, c: (b, 0, 0)),
        pl.BlockSpec((rb, H_B // 2, 2 * HD_B, N_STATE), lambda b, c: (b, 0, 0, 0)),
        pl.BlockSpec((rb, CONV_W - 1, conv_b), lambda b, c: (b, 0, 0)),
    )
    return pl.pallas_call(
        functools.partial(_prompt_rows_kernel, seq=seq, n_prompt_rows=bsz * seq),
        out_shape=out_shape,
        grid=(bsz // rb, n_chunks),
        in_specs=in_specs,
        out_specs=out_specs,
        scratch_shapes=[
            pltpu.VMEM((rb, CONV_HDR, d_a), F32),
            pltpu.VMEM((rb, CONV_HDR, conv_b), F32),
            pltpu.VMEM((rb, CHUNK, H_B * HD_B), F32),
            pltpu.VMEM((rb * CHUNK, d_a + H_B * HD_B), BF16),
            pltpu.VMEM((rb, CHUNK, d), F32),
            pltpu.SemaphoreType.DMA((rb,)),
        ],
        compiler_params=pltpu.CompilerParams(
            dimension_semantics=("arbitrary", "arbitrary"), vmem_limit_bytes=VMEM_LIMIT),
        name="prompt_mixer",
    )(xmeta, x_prompt, *consts)


def _regroup_w_in_kernel(w_ref, o_ref):
    d_a = H_A * DH_A
    d_b = H_B * HD_B
    conv_b = d_b + 2 * G_B * N_STATE
    o_i = 2 * d_a
    o_f = o_i + H_A
    o_zb = o_f + H_A
    o_xbc = o_zb + d_b
    o_dt = o_xbc + conv_b
    rows = w_ref.shape[0]
    o_ref[:, 0:2 * d_a] = w_ref[:, 0:2 * d_a].astype(BF16)
    o_ref[:, 2 * d_a:2 * d_a + d_b] = w_ref[:, o_zb:o_zb + d_b].astype(BF16)
    o_ref[:, 2 * d_a + d_b:2 * d_a + d_b + conv_b] = w_ref[:, o_xbc:o_xbc + conv_b].astype(BF16)
    small = jnp.concatenate(
        [w_ref[:, o_f:o_f + H_A], w_ref[:, o_dt:o_dt + H_B], w_ref[:, o_i:o_i + H_A],
         w_ref[:, o_dt:o_dt + H_B], jnp.zeros((rows, LANES - (L_DT + H_B)), F32)], axis=1)
    o_ref[:, 2 * d_a + d_b + conv_b:] = small.astype(BF16)


def _prep_mixer_params(norm_mix, w_in, conv_a_w, conv_a_b, w_q, w_k, w_v, b_i, b_f, norm_a,
                       conv_b_w, conv_b_b, dt_bias, a_log, d_skip, norm_b, w_out):
    d_a = H_A * DH_A
    d_b = H_B * HD_B
    conv_b = d_b + 2 * G_B * N_STATE
    d_model, d_in = w_in.shape[1], w_in.shape[2]
    n_cols = 2 * d_a + d_b + conv_b + LANES
    rows = 256
    assert w_in.shape[0] == 1 and d_model % rows == 0
    wcat = pl.pallas_call(
        _regroup_w_in_kernel,
        out_shape=jax.ShapeDtypeStruct((d_model, n_cols), BF16),
        grid=(d_model // rows,),
        in_specs=[pl.BlockSpec((None, rows, d_in), lambda i: (0, i, 0))],
        out_specs=pl.BlockSpec((rows, n_cols), lambda i: (i, 0)),
        compiler_params=pltpu.CompilerParams(
            dimension_semantics=("arbitrary",), vmem_limit_bytes=VMEM_LIMIT),
        name="regroup_w_in",
    )(w_in.astype(F32))

    def lanes(parts):
        pieces, at = [], 0
        for off, a in parts:
            pieces += [jnp.zeros((1, off - at), F32), a.astype(F32)]
            at = off + a.shape[1]
        return jnp.concatenate(pieces + [jnp.zeros((1, LANES - at), F32)], axis=1)

    return dict(
        nmix=norm_mix.reshape(1, -1).astype(F32),
        wcat=wcat,
        bsm=lanes([(L_F, b_f), (L_DTA, dt_bias), (L_I, b_i), (L_DT, dt_bias)]),
        alog=lanes([(L_DTA, a_log)]),
        cwa=conv_a_w.reshape(CONV_W, d_a).astype(F32), cba=conv_a_b.reshape(1, d_a).astype(F32),
        cwb=conv_b_w.reshape(CONV_W, conv_b).astype(F32), cbb=conv_b_b.reshape(1, conv_b).astype(F32),
        wq=w_q.reshape(H_A, DH_A, DH_A).astype(BF16), wk=w_k.reshape(H_A, DH_A, DH_A).astype(BF16),
        wv=w_v.reshape(H_A, DH_A, DH_A).astype(BF16),
        na=norm_a.reshape(1, d_a).astype(F32), nb=norm_b.reshape(1, d_b).astype(F32),
        dsk=jnp.repeat(d_skip.reshape(H_B).astype(F32), HD_B)[None, :],
        wout=w_out.reshape(d_a + d_b, -1).astype(BF16),
    )


SAMPLE_BLOCK = 8


def _expand_lanes(vals, first_lane, n_heads, width):
    r = lax.broadcasted_iota(jnp.int32, (LANES, n_heads * width), 0) - first_lane
    c = lax.broadcasted_iota(jnp.int32, (LANES, n_heads * width), 1)
    sel = jnp.logical_and(c >= r * width, c < (r + 1) * width)
    e = jnp.where(sel, 1.0, 0.0).astype(BF16)
    hi, mid, lo = _split3(vals)
    return (_dot(hi, e) + _dot(mid, e)) + _dot(lo, e)


def _sample_pre_kernel(x_ref, nmix_ref, wcat_ref, bsm_ref, alog_ref, cwa_ref, cba_ref, cwb_ref, cbb_ref,
                       wq_ref, wk_ref, wv_ref, dsk_ref, conva_ref, convb_ref, n0_ref, m0_ref,
                       conva_out, convb_out, n1_out, m1_out, g_out, qt_out, kwt_out, xwt_out,
                       v_out, bc_out, a1_out, w1_out, den_out, y1_out, ea_out, zbs_out, zas_out):
    d_a = H_A * DH_A
    d_b = H_B * HD_B
    conv_b = d_b + 2 * G_B * N_STATE
    shift_i = LANES - (L_I - L_F)
    x = x_ref[...]
    hn = (x * _rms_scale(x) * nmix_ref[...]).astype(BF16)
    lane = lax.broadcasted_iota(jnp.int32, (1, LANES), 1)
    lane_f = lane < L_DTA
    lane_dta = jnp.logical_and(lane >= L_DTA, lane < L_I)
    pre = _dot(hn, wcat_ref[:, 2 * d_a + d_b + conv_b:]) + bsm_ref[...]
    sp, lsig = _softplus_parts(pre)
    a_neg = jnp.where(lane_dta, -jnp.exp(alog_ref[...]), 0.0)
    pre_al = pltpu.roll(pre, shift_i, axis=1)
    sp_al = pltpu.roll(sp, shift_i, axis=1)
    m_inter = lsig + m0_ref[...]
    m = jnp.maximum(m_inter, pre_al)
    w_inter = jnp.exp(m_inter - m)
    sfac = jnp.exp(pre_al - m)
    ea = jnp.exp(sp * a_neg)
    dt = sp_al

    xa = _dot(hn, wcat_ref[:, 0:d_a])
    xc = (cwa_ref[0:1, :] * conva_ref[:, 0:d_a] + cwa_ref[1:2, :] * conva_ref[:, d_a:2 * d_a]
          + cwa_ref[2:3, :] * conva_ref[:, 2 * d_a:3 * d_a] + cwa_ref[3:4, :] * xa + cba_ref[...])
    conva_out[:, 0:2 * d_a] = conva_ref[:, d_a:3 * d_a]
    conva_out[:, 2 * d_a:3 * d_a] = xa
    xc = _silu(xc).astype(BF16)
    xab = xa.astype(BF16)
    sf_e = _expand_lanes(sfac, L_F, H_A, DH_A)
    w_e = _expand_lanes(w_inter, L_F, H_A, DH_A)
    qk8 = jnp.zeros((x.shape[0], LANES), F32)
    qn8 = jnp.zeros((x.shape[0], LANES), F32)
    for h in range(H_A):
        sl = slice(h * DH_A, (h + 1) * DH_A)
        q = _dot(xc[:, sl], wq_ref[h])
        k = _dot(xc[:, sl], wk_ref[h]) * (DH_A ** -0.5)
        v = _dot(xab[:, sl], wv_ref[h])
        kw = k * sf_e[:, sl]
        qk8 = jnp.where(lane == h, jnp.sum(q * k, axis=-1, keepdims=True), qk8)
        qn8 = jnp.where(lane == h, jnp.sum(q * n0_ref[:, sl], axis=-1, keepdims=True), qn8)
        n1_out[:, sl] = w_e[:, sl] * n0_ref[:, sl] + kw
        v_out[:, sl] = v
        qt_out[h] = q.T
        kwt_out[h] = kw.T
    s8 = qk8 * sfac
    a1_out[...] = _expand_lanes(s8, L_F, H_A, DH_A) * v_out[...]
    w1_out[...] = w_e
    den_out[...] = jnp.maximum(jnp.abs(_expand_lanes(s8 + w_inter * qn8, L_F, H_A, DH_A)),
                               jnp.exp(-_expand_lanes(m, L_F, H_A, DH_A)))
    m1_out[...] = m
    g_out[...] = jnp.where(lane_f, w_inter, jnp.where(lane_dta, ea, 0.0))
    zas_out[...] = jax.nn.sigmoid(_dot(hn, wcat_ref[:, d_a:2 * d_a]))

    off_xbc = 2 * d_a + d_b
    xbc = _dot(hn, wcat_ref[:, off_xbc:off_xbc + conv_b])
    xbc_c = (cwb_ref[0:1, :] * convb_ref[:, 0:conv_b] + cwb_ref[1:2, :] * convb_ref[:, conv_b:2 * conv_b]
             + cwb_ref[2:3, :] * convb_ref[:, 2 * conv_b:3 * conv_b] + cwb_ref[3:4, :] * xbc + cbb_ref[...])
    convb_out[:, 0:2 * conv_b] = convb_ref[:, conv_b:3 * conv_b]
    convb_out[:, 2 * conv_b:3 * conv_b] = xbc
    xbc_c = _silu(xbc_c)
    xs = xbc_c[:, 0:d_b]
    bc = xbc_c[:, d_b:conv_b]
    bc_out[...] = bc
    heads_per_group = H_B // G_B
    cbl = jnp.zeros((x.shape[0], LANES), F32)
    for g in range(G_B):
        cb_g = jnp.sum(bc[:, g * N_STATE:(g + 1) * N_STATE]
                       * bc[:, (G_B + g) * N_STATE:(G_B + g + 1) * N_STATE], axis=-1, keepdims=True)
        in_g = jnp.logical_and(lane >= L_DTA + g * heads_per_group,
                               lane < L_DTA + (g + 1) * heads_per_group)
        cbl = jnp.where(in_g, cb_g, cbl)
    dt_e = _expand_lanes(dt, L_DTA, H_B, HD_B)
    y1_out[...] = _expand_lanes(cbl * dt, L_DTA, H_B, HD_B) * xs + dsk_ref[...] * xs
    ea_out[...] = _expand_lanes(ea, L_DTA, H_B, HD_B)
    zbs_out[...] = _silu(_dot(hn, wcat_ref[:, 2 * d_a:2 * d_a + d_b]))
    xw = xs * dt_e
    for pi in range(H_B // 2):
        xwt_out[pi] = xw[:, pi * LANES:(pi + 1) * LANES].T


def _sample_state_kernel(g_ref, c0_ref, s0_ref, qt_ref, kwt_ref, xwt_ref, v_ref, bc_ref,
                         c1_ref, s1_ref, qc_ref, ysi_ref):
    i = pl.program_id(0)
    bb = c0_ref.shape[0]
    shift = lax.rem(LANES - lax.rem(i * bb, LANES), LANES)
    lane = lax.broadcasted_iota(jnp.int32, (1, LANES), 1)
    top = lax.broadcasted_iota(jnp.int32, (LANES, 1), 0) < HD_B
    heads_per_group = H_B // G_B
    for h in range(H_A):
        sl = slice(h * DH_A, (h + 1) * DH_A)
        qt = pltpu.roll(qt_ref[h], shift, axis=1)
        kwt = pltpu.roll(kwt_ref[h], shift, axis=1)
        for r in range(bb):
            b = i * bb + r
            c0 = c0_ref[r, h]
            dec = g_ref[b, L_F + h]
            v_row = v_ref[r:r + 1, sl]
            qc_ref[r:r + 1, sl] = jnp.sum(c0 * qt[:, r:r + 1], axis=0, keepdims=True)
            c1_ref[r, h] = dec * c0 + kwt[:, r:r + 1] * v_row
    for pi in range(H_B // 2):
        g = (2 * pi) // heads_per_group
        sl = slice(pi * LANES, (pi + 1) * LANES)
        xwt = pltpu.roll(xwt_ref[pi], shift, axis=1)
        acc = jnp.zeros((LANES, LANES), F32)
        for r in range(bb):
            b = i * bb + r
            s0 = s0_ref[r, pi]
            b_row = bc_ref[r:r + 1, g * N_STATE:(g + 1) * N_STATE]
            c_row = bc_ref[r:r + 1, (G_B + g) * N_STATE:(G_B + g + 1) * N_STATE]
            col = jnp.sum(s0 * c_row, axis=-1, keepdims=True)
            acc = jnp.where(lane == r, col, acc)
            ea_rows = jnp.where(top, g_ref[b, L_DTA + 2 * pi], g_ref[b, L_DTA + 2 * pi + 1])
            s1_ref[r, pi] = ea_rows * s0 + xwt[:, r:r + 1] * b_row
        ysi_ref[:, sl] = acc.T[0:bb, :]


def _sample_post_kernel(a1_ref, w1_ref, den_ref, y1_ref, ea_ref, zbs_ref, zas_ref, x_ref, qc_ref, ysi_ref,
                        na_ref, nb_ref, wout_ref, hall_ref, hmid_ref, merged):
    del hall_ref
    d_a = H_A * DH_A
    d_b = H_B * HD_B
    hh = (a1_ref[...] + w1_ref[...] * qc_ref[...]) / den_ref[...]
    for h in range(H_A):
        sl = slice(h * DH_A, (h + 1) * DH_A)
        hs = hh[:, sl]
        merged[:, sl] = (hs * _rms_scale(hs) * na_ref[:, sl] * zas_ref[:, sl]).astype(BF16)
    y = (y1_ref[...] + ea_ref[...] * ysi_ref[...]) * zbs_ref[...]
    gw = d_b // G_B
    for g in range(G_B):
        yg = y[:, g * gw:(g + 1) * gw]
        merged[:, d_a + g * gw:d_a + (g + 1) * gw] = (
            yg * _rms_scale(yg) * nb_ref[:, g * gw:(g + 1) * gw]).astype(BF16)
    hmid_ref[...] = x_ref[...] + _dot(merged[...], wout_ref[...])


def _vmem_specs(arrays):
    return [pl.BlockSpec(a.shape, lambda *_, _nd=a.ndim: (0,) * _nd) for a in arrays]


def _sample_mixer(x, c0, n0, m0, conva, s0, convb, p, hmid_all, row_offset):
    nb, d = x.shape
    d_a = H_A * DH_A
    d_b = H_B * HD_B
    conv_b = d_b + 2 * G_B * N_STATE
    row = lambda w: jax.ShapeDtypeStruct((nb, w), F32)
    tile = lambda k: jax.ShapeDtypeStruct((k, LANES, nb), F32)
    pre_in = [x, p["nmix"], p["wcat"], p["bsm"], p["alog"], p["cwa"], p["cba"], p["cwb"], p["cbb"],
              p["wq"], p["wk"], p["wv"], p["dsk"], conva, convb, n0, m0]
    pre_out_shape = (row(3 * d_a), row(3 * conv_b), row(d_a), row(LANES), row(LANES),
                     tile(H_A), tile(H_A), tile(H_B // 2), row(d_a), row(2 * G_B * N_STATE),
                     row(d_a), row(d_a), row(d_a), row(d_b), row(d_b), row(d_b), row(d_a))
    (conva1, convb1, n1, m1, g8, qt, kwt, xwt, v, bc, a1, w1, den, y1, ea_e, zbs, zas) = pl.pallas_call(
        _sample_pre_kernel,
        out_shape=pre_out_shape,
        grid=(1,),
        in_specs=_vmem_specs(pre_in),
        out_specs=tuple(pl.BlockSpec(s.shape, lambda i, _nd=len(s.shape): (0,) * _nd) for s in pre_out_shape),
        compiler_params=pltpu.CompilerParams(
            dimension_semantics=("arbitrary",), vmem_limit_bytes=VMEM_LIMIT),
        name="sample_pre",
    )(*pre_in)

    bb = SAMPLE_BLOCK
    const3 = lambda k: pl.BlockSpec((k, LANES, nb), lambda i, g: (0, 0, 0))
    state_grid = pltpu.PrefetchScalarGridSpec(
        num_scalar_prefetch=1,
        grid=(nb // bb,),
        in_specs=[pl.BlockSpec((bb, H_A, DH_A, DH_A), lambda i, g: (i, 0, 0, 0)),
                  pl.BlockSpec((bb, H_B // 2, 2 * HD_B, N_STATE), lambda i, g: (i, 0, 0, 0)),
                  const3(H_A), const3(H_A), const3(H_B // 2),
                  pl.BlockSpec((bb, d_a), lambda i, g: (i, 0)),
                  pl.BlockSpec((bb, 2 * G_B * N_STATE), lambda i, g: (i, 0))],
        out_specs=(pl.BlockSpec((bb, H_A, DH_A, DH_A), lambda i, g: (i, 0, 0, 0)),
                   pl.BlockSpec((bb, H_B // 2, 2 * HD_B, N_STATE), lambda i, g: (i, 0, 0, 0)),
                   pl.BlockSpec((bb, d_a), lambda i, g: (i, 0)),
                   pl.BlockSpec((bb, d_b), lambda i, g: (i, 0))),
    )
    c1, s1, qc, ysi = pl.pallas_call(
        _sample_state_kernel,
        out_shape=(jax.ShapeDtypeStruct(c0.shape, F32), jax.ShapeDtypeStruct(s0.shape, F32),
                   row(d_a), row(d_b)),
        grid_spec=state_grid,
        compiler_params=pltpu.CompilerParams(
            dimension_semantics=("arbitrary",), vmem_limit_bytes=VMEM_LIMIT),
        name="sample_state",
    )(g8, c0, s0, qt, kwt, xwt, v, bc)

    post_in = [a1, w1, den, y1, ea_e, zbs, zas, x, qc, ysi, p["na"], p["nb"], p["wout"]]
    hmid_all = pl.pallas_call(
        _sample_post_kernel,
        out_shape=jax.ShapeDtypeStruct(hmid_all.shape, F32),
        grid=(1,),
        in_specs=_vmem_specs(post_in) + [pl.BlockSpec(memory_space=pl.ANY)],
        out_specs=pl.BlockSpec((nb, d), lambda i: (row_offset // nb, 0)),
        scratch_shapes=[pltpu.VMEM((nb, d_a + d_b), BF16)],
        input_output_aliases={len(post_in): 0},
        compiler_params=pltpu.CompilerParams(
            dimension_semantics=("arbitrary",), vmem_limit_bytes=VMEM_LIMIT),
        name="sample_post",
    )(*post_in, hmid_all)
    return hmid_all, c1, n1, m1, conva1, s1, convb1


R_EA, R_EB, R_RA, R_RB, R_GA, R_GB = 0, 1, 2, 3, 4, 5
RL_E = N_EGROUPS


def _router_kernel(h_ref, nf_ref, whi_ref, wmid_ref, br_ref, xn_ref, info_ref, cnt_ref, carry):
    i = pl.program_id(0)
    tr = h_ref.shape[0]

    @pl.when(i == 0)
    def _init():
        carry[...] = jnp.zeros_like(carry)

    h = h_ref[...]
    xn = h * _rms_scale(h) * nf_ref[...]
    _store_token_tiles(xn_ref, xn)
    x_hi, x_mid, _ = _split3(xn)
    logits = (_dot(x_hi, whi_ref[...]) + _dot(x_hi, wmid_ref[...]) + _dot(x_mid, whi_ref[...])
              + br_ref[...])
    lane_i = lax.broadcasted_iota(jnp.int32, (1, LANES), 1)
    lane = lane_i.astype(F32)
    big = float(LANES)

    def first_lane_of(cond):
        return jnp.min(jnp.where(cond, lane, big), axis=-1, keepdims=True)

    l1 = jnp.where(lane_i < N_EGROUPS, logits, NEG_INF)
    e1 = jnp.exp(l1 - jnp.max(l1, axis=-1, keepdims=True))
    p1 = e1 / jnp.sum(e1, axis=-1, keepdims=True)
    gp = jnp.max(p1, axis=-1, keepdims=True)
    gidx = first_lane_of(p1 == gp)
    lo = RL_E + N_EPG * gidx
    l2 = jnp.where(jnp.logical_and(lane >= lo, lane < lo + N_EPG), logits, NEG_INF)
    va = jnp.max(l2, axis=-1, keepdims=True)
    ia = first_lane_of(l2 == va)
    l2b = jnp.where(lane == ia, NEG_INF, l2)
    vb = jnp.max(l2b, axis=-1, keepdims=True)
    ib = first_lane_of(l2b == vb)
    eb = jnp.exp(vb - va)
    wa = 1.0 / (1.0 + eb)
    wb = eb / (1.0 + eb)

    is_a = lane == ia
    is_b = lane == ib
    onehot = jnp.where(jnp.logical_or(is_a, is_b), 1.0, 0.0)
    ri = lax.broadcasted_iota(jnp.int32, (tr, tr), 0)
    ci = lax.broadcasted_iota(jnp.int32, (tr, tr), 1)
    tri = jnp.where(ri >= ci, 1.0, 0.0).astype(BF16)
    incl = _dot(tri, onehot.astype(BF16))
    excl = incl - onehot + carry[...]
    rank_a = jnp.sum(jnp.where(is_a, excl, 0.0), axis=-1, keepdims=True)
    rank_b = jnp.sum(jnp.where(is_b, excl, 0.0), axis=-1, keepdims=True)
    carry[...] = carry[...] + incl[tr - 1:tr, :]
    cnt_ref[...] = carry[...]

    info = jnp.where(lane_i == R_EA, ia - RL_E, 0.0)
    info = jnp.where(lane_i == R_EB, ib - RL_E, info)
    info = jnp.where(lane_i == R_RA, rank_a, info)
    info = jnp.where(lane_i == R_RB, rank_b, info)
    info = jnp.where(lane_i == R_GA, gp * wa, info)
    info = jnp.where(lane_i == R_GB, gp * wb, info)
    info_ref[...] = info


def _row_tile(n, candidates):
    for t in candidates:
        if n % t == 0:
            return t
    raise ValueError(f"no row tile for {n} rows among {candidates}")


def _router(hmid, rp):
    n, d = hmid.shape
    assert d == TOK_TILE_ROWS * LANES
    tr = _row_tile(n, (512, 384, 256, 128))
    return pl.pallas_call(
        _router_kernel,
        out_shape=(jax.ShapeDtypeStruct((n * TOK_TILE_ROWS, LANES), F32),
                   jax.ShapeDtypeStruct((n, LANES), F32),
                   jax.ShapeDtypeStruct((1, LANES), F32)),
        grid=(n // tr,),
        in_specs=[pl.BlockSpec((tr, d), lambda i: (i, 0)),
                  pl.BlockSpec((1, d), lambda i: (0, 0)),
                  pl.BlockSpec((d, LANES), lambda i: (0, 0)),
                  pl.BlockSpec((d, LANES), lambda i: (0, 0)),
                  pl.BlockSpec((1, LANES), lambda i: (0, 0))],
        out_specs=(pl.BlockSpec((tr * TOK_TILE_ROWS, LANES), lambda i: (i, 0)),
                   pl.BlockSpec((tr, LANES), lambda i: (i, 0)),
                   pl.BlockSpec((1, LANES), lambda i: (0, 0))),
        scratch_shapes=[pltpu.VMEM((1, LANES), F32)],
        compiler_params=pltpu.CompilerParams(
            dimension_semantics=("arbitrary",), vmem_limit_bytes=VMEM_LIMIT),
        name="router",
    )(hmid, rp["nf"], rp["whi"], rp["wmid"], rp["br"])


def _prep_router_params(norm_ffn, w_r1, b_r1, w_r2, b_r2):
    d = w_r1.shape[1]
    w = jnp.concatenate([w_r1.reshape(d, N_EGROUPS).astype(F32), w_r2.reshape(d, N_EXPERTS).astype(F32),
                         jnp.zeros((d, LANES - RL_E - N_EXPERTS), F32)], axis=1)
    whi = w.astype(BF16)
    wmid = (w - whi.astype(F32)).astype(BF16)
    br = jnp.concatenate([b_r1.reshape(1, N_EGROUPS).astype(F32), b_r2.reshape(1, N_EXPERTS).astype(F32),
                          jnp.zeros((1, LANES - RL_E - N_EXPERTS), F32)], axis=1)
    return dict(nf=norm_ffn.reshape(1, d).astype(F32), whi=whi, wmid=wmid, br=br)


FFN_TM = 256
N_GATHER_SLOTS = 3


def _start_tile_gather(first_row_of, n_rows, src_hbm, dst, sem, priority_of):
    for r in range(n_rows):
        start = pl.multiple_of(first_row_of(r), TOK_TILE_ROWS)
        pltpu.make_async_copy(src_hbm.at[pl.ds(start, TOK_TILE_ROWS), :],
                              dst.at[pl.ds(r * TOK_TILE_ROWS, TOK_TILE_ROWS), :],
                              sem).start(priority=priority_of(r))


def _wait_tile_gather(n_rows, src_hbm, dst, sem):
    pltpu.make_async_copy(src_hbm.at[pl.ds(0, n_rows * TOK_TILE_ROWS), :], dst, sem).wait()


TAB_OFF, TAB_CNT, TAB_TILE_EXPERT, TAB_NVALID = 0, 1, 2, 3
TAB_LANES = 2 * LANES


def _ffn_kernel(tab_ref, pa_ref, pb_ref, xn_hbm, wg_ref, wu_ref, wd_ref, ys_ref,
                src, xbuf, wbf, sem, *, n_tokens, tm):
    i = pl.program_id(0)
    n_valid = tab_ref[TAB_NVALID, 0]
    slot = lax.rem(i, N_GATHER_SLOTS)
    gather_priority = lambda r: 1

    @pl.when(i == 0)
    def _build_source_rows():
        for e in range(N_EXPERTS):
            cnt_e = tab_ref[TAB_CNT, RL_E + e]
            first = tab_ref[TAB_OFF, RL_E + e] + cnt_e
            n_pad = lax.rem(tm - lax.rem(cnt_e, tm), tm)

            def pad_body(r, carry, first=first):
                src[first + r] = 0
                return carry
            lax.fori_loop(0, n_pad, pad_body, 0)

        def body(t, carry):
            first_row = t * TOK_TILE_ROWS
            src[pa_ref[t]] = first_row
            src[pb_ref[t]] = first_row
            return carry
        lax.fori_loop(0, n_tokens, body, 0, unroll=8)
        _start_tile_gather(lambda r: src[r], tm, xn_hbm, xbuf.at[0], sem.at[0], gather_priority)
        second = jnp.where(n_valid > 1, tm, 0)
        _start_tile_gather(lambda r: src[second + r], tm, xn_hbm, xbuf.at[1], sem.at[1], gather_priority)

    changed = jnp.logical_or(i == 0, tab_ref[TAB_TILE_EXPERT, i]
                             != tab_ref[TAB_TILE_EXPERT, jnp.maximum(i - 1, 0)])

    @pl.when(jnp.logical_and(changed, i < n_valid))
    def _cast_weights():
        wbf[0] = wg_ref[...].astype(BF16)
        wbf[1] = wu_ref[...].astype(BF16)
        wbf[2] = wd_ref[...].astype(BF16)

    @pl.when(i < n_valid)
    def _compute():
        _wait_tile_gather(tm, xn_hbm, xbuf.at[slot], sem.at[slot])
        x = _load_token_tiles(xbuf.at[slot], tm).astype(BF16)
        hg = _dot(x, wbf[0])
        hu = _dot(x, wbf[1])
        y = _dot((_silu(hg) * hu).astype(BF16), wbf[2])
        base = jnp.where(i + 2 < n_valid, i + 2, 0) * tm
        ahead = lax.rem(i + 2, N_GATHER_SLOTS)
        _start_tile_gather(lambda r: src[base + r], tm, xn_hbm, xbuf.at[ahead], sem.at[ahead],
                           gather_priority)
        _store_token_tiles(ys_ref, y)

    @pl.when(i == n_valid - 1)
    def _drain():
        for k in (1, 2):
            s = lax.rem(i + k, N_GATHER_SLOTS)
            _wait_tile_gather(tm, xn_hbm, xbuf.at[s], sem.at[s])

    @pl.when(i >= n_valid)
    def _pad():
        ys_ref[...] = jnp.zeros_like(ys_ref)


def _expert_ffn(xn_tiles, tab, pos_a, pos_b, n_tiles, wg, wu, wd):
    n = pos_a.shape[0]
    tm = FFN_TM
    d, dff = wg.shape[1], wg.shape[2]
    rows = tm * TOK_TILE_ROWS
    idx = lambda i, tab, pa, pb: (tab[TAB_TILE_EXPERT, i], 0, 0)
    grid_spec = pltpu.PrefetchScalarGridSpec(
        num_scalar_prefetch=3,
        grid=(n_tiles,),
        in_specs=[pl.BlockSpec(memory_space=pl.ANY),
                  pl.BlockSpec((None, d, dff), idx),
                  pl.BlockSpec((None, d, dff), idx),
                  pl.BlockSpec((None, dff, d), idx)],
        out_specs=pl.BlockSpec((rows, LANES), lambda i, tab, pa, pb: (i, 0)),
        scratch_shapes=[pltpu.SMEM((n_tiles * tm,), jnp.int32),
                        pltpu.VMEM((N_GATHER_SLOTS, rows, LANES), F32),
                        pltpu.VMEM((3, d, dff), BF16),
                        pltpu.SemaphoreType.DMA((N_GATHER_SLOTS,))],
    )
    return pl.pallas_call(
        functools.partial(_ffn_kernel, n_tokens=n, tm=tm),
        out_shape=jax.ShapeDtypeStruct((n_tiles * rows, LANES), F32),
        grid_spec=grid_spec,
        compiler_params=pltpu.CompilerParams(
            dimension_semantics=("arbitrary",), vmem_limit_bytes=VMEM_LIMIT),
        name="expert_ffn",
    )(tab, pos_a, pos_b, xn_tiles, wg, wu, wd)


def _positions_kernel(info_ref, cnt_ref, pos_ref, tab_ref, *, tm, chunk):
    lane_i = lax.broadcasted_iota(jnp.int32, (1, LANES), 1)
    lane = lane_i.astype(F32)
    is_expert = jnp.logical_and(lane_i >= RL_E, lane_i < RL_E + N_EXPERTS)
    cnt = jnp.where(is_expert, cnt_ref[...], 0.0)
    padded = jnp.floor((cnt + (tm - 1)) / tm) * tm
    ri = lax.broadcasted_iota(jnp.int32, (LANES, LANES), 0)
    ci = lax.broadcasted_iota(jnp.int32, (LANES, LANES), 1)
    before = jnp.where(ri < ci, 1.0, 0.0).astype(BF16)
    hi, mid, lo = _split3(jnp.broadcast_to(padded, (SUBLANES, LANES)))
    off = ((_dot(hi, before) + _dot(mid, before)) + _dot(lo, before))[0:1, :]
    pick = jnp.where(lax.broadcasted_iota(jnp.int32, (SUBLANES, LANES), 0) == lane_i, 1.0, 0.0).astype(BF16)

    total = jnp.sum(padded, axis=-1, keepdims=True)
    tile_row = lax.broadcasted_iota(jnp.int32, (TAB_LANES, 1), 0).astype(F32) * tm
    ends = off + padded
    done = jnp.logical_and(is_expert, ends <= jnp.minimum(tile_row, total - 1.0))
    te_col = jnp.sum(jnp.where(done, 1.0, 0.0), axis=-1, keepdims=True)
    te_rows = _dot_nt(pick, jnp.where(lane_i == 0, te_col, 0.0).astype(BF16))
    tab_ref[...] = jnp.zeros_like(tab_ref)
    tab_ref[TAB_OFF:TAB_OFF + 1, 0:LANES] = off.astype(jnp.int32)
    tab_ref[TAB_CNT:TAB_CNT + 1, 0:LANES] = cnt.astype(jnp.int32)
    tab_ref[TAB_TILE_EXPERT:TAB_TILE_EXPERT + 1, :] = te_rows[0:1, :].astype(jnp.int32)
    tab_ref[TAB_NVALID:TAB_NVALID + 1, 0:LANES] = jnp.broadcast_to(total / tm, (1, LANES)).astype(jnp.int32)

    n = info_ref.shape[0]
    for c0 in range(0, n, chunk):
        blk = info_ref[c0:c0 + chunk, :]
        lane_a = blk[:, R_EA:R_EA + 1] + RL_E
        lane_b = blk[:, R_EB:R_EB + 1] + RL_E
        pos_a = blk[:, R_RA:R_RA + 1] + jnp.sum(jnp.where(lane == lane_a, off, 0.0), axis=-1, keepdims=True)
        pos_b = blk[:, R_RB:R_RB + 1] + jnp.sum(jnp.where(lane == lane_b, off, 0.0), axis=-1, keepdims=True)
        z_hi, z_mid, z_lo = _split3(jnp.where(lane_i == 0, pos_a, jnp.where(lane_i == 1, pos_b, 0.0)))
        rows = (_dot_nt(pick, z_hi) + _dot_nt(pick, z_mid)) + _dot_nt(pick, z_lo)
        pos_ref[:, c0:c0 + chunk] = rows.astype(jnp.int32)


def _routing_tables(info, counts, n_tiles, tm):
    n = info.shape[0]
    groups = n // LANES
    assert n_tiles <= TAB_LANES
    chunk = LANES * max(g for g in range(1, 65) if groups % g == 0)
    pos, tab = pl.pallas_call(
        functools.partial(_positions_kernel, tm=tm, chunk=chunk),
        out_shape=(jax.ShapeDtypeStruct((SUBLANES, n), jnp.int32),
                   jax.ShapeDtypeStruct((SUBLANES, TAB_LANES), jnp.int32)),
        grid=(1,),
        in_specs=[pl.BlockSpec((n, LANES), lambda i: (0, 0)), pl.BlockSpec((1, LANES), lambda i: (0, 0))],
        out_specs=(pl.BlockSpec((SUBLANES, n), lambda i: (0, 0)),
                   pl.BlockSpec((SUBLANES, TAB_LANES), lambda i: (0, 0))),
        compiler_params=pltpu.CompilerParams(
            dimension_semantics=("arbitrary",), vmem_limit_bytes=VMEM_LIMIT),
        name="positions",
    )(info, counts)
    return tab, pos[0], pos[1]


def _combine_kernel(pa_ref, pb_ref, h_ref, info_ref, ys_hbm, nfin_ref, yp_ref, ysm_ref,
                    buf_a, buf_b, sem, *, n_prompt_tiles):
    i = pl.program_id(0)
    n_steps = pl.num_programs(0)
    tt = h_ref.shape[0]
    slot = lax.rem(i, N_GATHER_SLOTS)

    def start(tile, s):
        base = tile * tt
        _start_tile_gather(lambda r: pa_ref[base + r] * TOK_TILE_ROWS, tt, ys_hbm, buf_a.at[s],
                           sem.at[s], lambda r: 0)
        _start_tile_gather(lambda r: pb_ref[base + r] * TOK_TILE_ROWS, tt, ys_hbm, buf_b.at[s],
                           sem.at[s], lambda r: 1)

    @pl.when(i == 0)
    def _first():
        start(0, 0)
        start(lax.rem(1, n_steps), 1)

    _wait_tile_gather(tt, ys_hbm, buf_a.at[slot], sem.at[slot])
    _wait_tile_gather(tt, ys_hbm, buf_b.at[slot], sem.at[slot])
    info = info_ref[...]
    rows_a = _load_token_tiles(buf_a.at[slot], tt)
    rows_b = _load_token_tiles(buf_b.at[slot], tt)
    x = h_ref[...]
    start(lax.rem(i + 2, n_steps), lax.rem(i + 2, N_GATHER_SLOTS))
    h = x + info[:, R_GA:R_GA + 1] * rows_a + info[:, R_GB:R_GB + 1] * rows_b
    y = h * _rms_scale(h) * nfin_ref[...]

    @pl.when(i < n_prompt_tiles)
    def _prompt():
        yp_ref[...] = y

    @pl.when(i >= n_prompt_tiles)
    def _sample():
        ysm_ref[...] = y

    @pl.when(i == n_steps - 1)
    def _drain():
        for k in (1, 2):
            s = lax.rem(i + k, N_GATHER_SLOTS)
            _wait_tile_gather(tt, ys_hbm, buf_a.at[s], sem.at[s])
            _wait_tile_gather(tt, ys_hbm, buf_b.at[s], sem.at[s])


def _combine(hmid, info, ys, pos_a, pos_b, nfin, n_prompt):
    n, d = hmid.shape
    tt = CHUNK
    n_prompt_tiles = n_prompt // tt
    n_sample = n - n_prompt
    grid_spec = pltpu.PrefetchScalarGridSpec(
        num_scalar_prefetch=2,
        grid=(n // tt,),
        in_specs=[pl.BlockSpec((tt, d), lambda i, pa, pb: (i, 0)),
                  pl.BlockSpec((tt, LANES), lambda i, pa, pb: (i, 0)),
                  pl.BlockSpec(memory_space=pl.ANY),
                  pl.BlockSpec((1, d), lambda i, pa, pb: (0, 0))],
        out_specs=(pl.BlockSpec((tt, d), lambda i, pa, pb: (jnp.minimum(i, n_prompt_tiles - 1), 0)),
                   pl.BlockSpec((tt, d), lambda i, pa, pb: (jnp.maximum(i - n_prompt_tiles, 0), 0))),
        scratch_shapes=[pltpu.VMEM((N_GATHER_SLOTS, tt * TOK_TILE_ROWS, LANES), F32),
                        pltpu.VMEM((N_GATHER_SLOTS, tt * TOK_TILE_ROWS, LANES), F32),
                        pltpu.SemaphoreType.DMA((N_GATHER_SLOTS,))],
    )
    return pl.pallas_call(
        functools.partial(_combine_kernel, n_prompt_tiles=n_prompt_tiles),
        out_shape=(jax.ShapeDtypeStruct((n_prompt, d), F32),
                   jax.ShapeDtypeStruct((n_sample, d), F32)),
        grid_spec=grid_spec,
        compiler_params=pltpu.CompilerParams(
            dimension_semantics=("arbitrary",), vmem_limit_bytes=VMEM_LIMIT),
        name="combine",
    )(pos_a, pos_b, hmid, info, ys, nfin)


def _moe_and_final_norm(hmid, n_prompt, rp, wg, wu, wd, nfin):
    n = hmid.shape[0]
    tm = FFN_TM
    n_tiles = (2 * n + N_EXPERTS * (tm - 1)) // tm
    xn, info, counts = _router(hmid, rp)
    tab, pos_a, pos_b = _routing_tables(info, counts, n_tiles, tm)
    ys = _expert_ffn(xn, tab, pos_a, pos_b, n_tiles, wg, wu, wd)
    return _combine(hmid, info, ys, pos_a, pos_b, nfin, n_prompt)


def kernel(x_prompt, x_sample, state_mlstm_C, state_mlstm_n, state_mlstm_m, state_mlstm_conv, state_ssm, state_ssm_conv, meta_tokens, norm_mix, w_in, conv_a_w, conv_a_b, w_q, w_k, w_v, b_i, b_f, norm_a, conv_b_w, conv_b_b, dt_bias, a_log, d_skip, norm_b, w_out, norm_ffn, w_r1, b_r1, w_r2, b_r2, w_gate, w_up, w_down, norm_final):
    bsz, seq, d = x_prompt.shape
    nb = x_sample.shape[0]
    d_a = H_A * DH_A
    conv_b = H_B * HD_B + 2 * G_B * N_STATE
    assert w_in.shape[0] == 1 and x_sample.shape[1] == 1 and seq % CHUNK == 0 and nb == CHUNK
    mp = _prep_mixer_params(norm_mix, w_in, conv_a_w, conv_a_b, w_q, w_k, w_v, b_i, b_f, norm_a,
                            conv_b_w, conv_b_b, dt_bias, a_log, d_skip, norm_b, w_out)
    rp = _prep_router_params(norm_ffn, w_r1, b_r1, w_r2, b_r2)
    xmeta = jnp.concatenate([jnp.zeros((CHUNK - N_META, d), F32), meta_tokens.astype(F32)], 0)

    hmid, p_c, p_n, p_m, p_ca, p_s, p_cb = _prompt_mixer(x_prompt.astype(F32), xmeta, mp, nb)
    m0 = jnp.pad(state_mlstm_m.reshape(nb, H_A).astype(F32), ((0, 0), (0, LANES - H_A)))
    hmid, s_c, s_n, s_m, s_ca, s_s, s_cb = _sample_mixer(
        x_sample.reshape(nb, d).astype(F32),
        state_mlstm_C.reshape(nb, H_A, DH_A, DH_A).astype(F32),
        state_mlstm_n.reshape(nb, d_a).astype(F32),
        m0,
        state_mlstm_conv.reshape(nb, (CONV_W - 1) * d_a).astype(F32),
        state_ssm.reshape(nb, H_B // 2, 2 * HD_B, N_STATE).astype(F32),
        state_ssm_conv.reshape(nb, (CONV_W - 1) * conv_b).astype(F32),
        mp, hmid, bsz * seq)

    wshape = w_gate.shape[1:]
    y_p, y_s = _moe_and_final_norm(
        hmid, bsz * seq, rp, w_gate.reshape(wshape).astype(F32), w_up.reshape(wshape).astype(F32),
        w_down.reshape(w_down.shape[1:]).astype(F32), norm_final.reshape(1, d).astype(F32))

    return (y_p.reshape(bsz, seq, d), y_s.reshape(nb, 1, d),
            p_c.reshape(1, bsz, H_A, DH_A, DH_A), p_n.reshape(1, bsz, H_A, DH_A),
            p_m[:, 0, :H_A].reshape(1, bsz, H_A), p_ca.reshape(1, bsz, CONV_W - 1, d_a),
            p_s.reshape(1, bsz, H_B, HD_B, N_STATE), p_cb.reshape(1, bsz, CONV_W - 1, conv_b),
            s_c.reshape(1, nb, H_A, DH_A, DH_A), s_n.reshape(1, nb, H_A, DH_A),
            s_m[:, :H_A].reshape(1, nb, H_A), s_ca.reshape(1, nb, CONV_W - 1, d_a),
            s_s.reshape(1, nb, H_B, HD_B, N_STATE), s_cb.reshape(1, nb, CONV_W - 1, conv_b))
```

```python
import functools
import math

import jax
import jax.numpy as jnp
from jax import lax
from jax.experimental import pallas as pl
from jax.experimental.pallas import tpu as pltpu

F32 = jnp.float32
BF16 = jnp.bfloat16

EPS = 1e-6
N_META = 16
CONV_W = 4
CHUNK = 128
H_A = 8
DH_A = 128
H_B = 16
HD_B = 64
N_STATE = 128
G_B = 2
N_EGROUPS = 4
N_EPG = 4
N_EXPERTS = 16
LANES = 128
SUBLANES = 8
CONV_HDR = SUBLANES
VMEM_LIMIT = 56 * 1024 * 1024

L_F = 0
L_DTA = 8
L_I = 24
L_DT = 32

NEG_INF = float("-inf")


def _dot(a, b):
    return jnp.dot(a, b, preferred_element_type=F32)


def _dot_nt(a, b):
    return lax.dot_general(a, b, (((1,), (1,)), ((), ())), preferred_element_type=F32)


def _dot_tn(a, b):
    return lax.dot_general(a, b, (((0,), (0,)), ((), ())), preferred_element_type=F32)


def _split3(x):
    hi = x.astype(BF16)
    r = x - hi.astype(F32)
    mid = r.astype(BF16)
    lo = (r - mid.astype(F32)).astype(BF16)
    return hi, mid, lo


def _silu(x):
    return x * jax.nn.sigmoid(x)


def _softplus_parts(x):
    t = jnp.log1p(jnp.exp(-jnp.abs(x)))
    return jnp.maximum(x, 0.0) + t, jnp.minimum(x, 0.0) - t


def _rms_scale(x):
    return lax.rsqrt(jnp.mean(x * x, axis=-1, keepdims=True) + EPS)


TOK_TILE_ROWS = SUBLANES


def _store_token_tiles(ref, x):
    n = x.shape[0]
    for j in range(TOK_TILE_ROWS):
        ref[pl.ds(j, n, stride=TOK_TILE_ROWS), :] = x[:, j * LANES:(j + 1) * LANES]


def _causal_conv(x, tail, w_ref, b_ref):
    n_tail = tail.shape[0]
    row = lax.broadcasted_iota(jnp.int32, (n_tail, 1), 0)
    acc = w_ref[CONV_W - 1:CONV_W, :] * x + b_ref[...]
    for k in range(1, CONV_W):
        rolled = pltpu.roll(x, k, axis=0)
        head = jnp.where(row < k, pltpu.roll(tail, k, axis=0), rolled[0:n_tail])
        shifted = jnp.concatenate([head, rolled[n_tail:]], axis=0)
        acc = acc + w_ref[CONV_W - 1 - k:CONV_W - k, :] * shifted
    return acc


def _load_token_tiles(ref, n):
    return jnp.concatenate(
        [ref[pl.ds(j, n, stride=TOK_TILE_ROWS), :] for j in range(TOK_TILE_ROWS)], axis=1)


PROMPT_ROWS = 2


def _prompt_rows_kernel(xmeta_ref, xp_ref, nmix_ref, wcat_ref, bsm_ref, alog_ref,
                        cwa_ref, cba_ref, cwb_ref, cbb_ref, wq_ref, wk_ref, wv_ref,
                        na_ref, nb_ref, dsk_ref, wout_ref,
                        hmid_hbm, c_ref, n_ref, m_ref, conva_ref, s_ref, convb_ref,
                        xa_buf, xbc_buf, y_buf, merged, hout, sem, *, seq, n_prompt_rows):
    p = pl.program_id(0)
    c = pl.program_id(1)
    last_p = pl.num_programs(0) - 1
    last_c = pl.num_programs(1) - 1
    T = CHUNK
    RB = xp_ref.shape[0]
    d_a = H_A * DH_A
    d_b = H_B * HD_B
    conv_b = d_b + 2 * G_B * N_STATE

    def out_copy(r, row0):
        return pltpu.make_async_copy(hout.at[r], hmid_hbm.at[pl.ds(row0, T), :], sem.at[r])

    @pl.when(c == 0)
    def _init():
        c_ref[...] = jnp.zeros_like(c_ref)
        n_ref[...] = jnp.zeros_like(n_ref)
        m_ref[...] = jnp.zeros_like(m_ref)
        s_ref[...] = jnp.zeros_like(s_ref)
        xa_buf[...] = jnp.zeros_like(xa_buf)
        xbc_buf[...] = jnp.zeros_like(xbc_buf)

    @pl.when(jnp.logical_and(p == 0, c == 0))
    def _clear_sample_rows():
        hout[0] = jnp.zeros((T, hout.shape[2]), F32)
        cp = out_copy(0, n_prompt_rows)
        cp.start()
        cp.wait()

    row = lax.broadcasted_iota(jnp.int32, (T, 1), 0)
    valid = jnp.logical_or(c > 0, row >= T - N_META)
    xs_in = [jnp.where(c == 0, xmeta_ref[...], xp_ref[r]) for r in range(RB)]
    x2 = jnp.concatenate(xs_in, axis=0)
    hn = (x2 * _rms_scale(x2) * nmix_ref[...]).astype(BF16)

    lane = lax.broadcasted_iota(jnp.int32, (1, LANES), 1)
    lane_f = lane < L_DTA
    lane_dta = jnp.logical_and(lane >= L_DTA, lane < L_I)
    lane_i = jnp.logical_and(lane >= L_I, lane < L_DT)
    lane_dt = jnp.logical_and(lane >= L_DT, lane < L_DT + H_B)
    ri = lax.broadcasted_iota(jnp.int32, (T, T), 0)
    ci = lax.broadcasted_iota(jnp.int32, (T, T), 1)
    causal = ri >= ci
    tri = jnp.where(causal, 1.0, 0.0).astype(BF16)
    a_neg = jnp.where(lane_dta, -jnp.exp(alog_ref[...]), 0.0)
    left = lane < HD_B
    top = lax.broadcasted_iota(jnp.int32, (LANES, 1), 0) < HD_B

    off_small = 2 * d_a + d_b + conv_b
    pre2 = _dot(hn, wcat_ref[:, off_small:]) + bsm_ref[...]
    xa2 = _dot(hn, wcat_ref[:, 0:d_a])

    gcols, grows, xcs, xabs = [], [], [], []

    def gate_tables():
        for r in range(RB):
            pre = pre2[r * T:(r + 1) * T]
            sp, lsig = _softplus_parts(pre)
            to_cum = jnp.where(lane_f, lsig, jnp.where(lane_dta, sp * a_neg, 0.0))
            to_cum = jnp.where(valid, to_cum, 0.0)
            hi, mid, lo = _split3(to_cum)
            cum = _dot(tri, hi) + _dot(tri, mid) + _dot(tri, lo)
            extra = jnp.where(lane_i, jnp.where(valid, pre, NEG_INF),
                              jnp.where(lane_dt, jnp.where(valid, sp, 0.0), 0.0))
            gcol = cum + extra
            gcols.append(gcol)
            grows.append(gcol.T)

    gate_tables()
    for r in range(RB):
        rs = slice(r * T, (r + 1) * T)
        xa = xa2[rs]
        xc = _causal_conv(xa, xa_buf[r], cwa_ref, cba_ref)
        xa_buf[r] = xa[T - CONV_HDR:T, :]
        conva_ref[r] = xa[T - 3:T, :]
        xcs.append(_silu(xc).astype(BF16))
        xabs.append(xa.astype(BF16))

    items = [(r, h) for h in range(H_A) for r in range(RB)]
    hsl = lambda h: slice(h * DH_A, (h + 1) * DH_A)
    m_alls = [m_ref[r] for r in range(RB)]
    m_news = list(m_alls)
    qs, ks, vs, qks, st, dd = {}, {}, {}, {}, {}, {}

    def stage_qkv(it):
        r, h = it
        qs[it] = _dot(xcs[r][:, hsl(h)], wq_ref[h]).astype(BF16)
        ks[it] = _dot(xcs[r][:, hsl(h)], wk_ref[h]) * (DH_A ** -0.5)
        vs[it] = _dot(xabs[r][:, hsl(h)], wv_ref[h]).astype(BF16)

    def stage_qk(it):
        qks[it] = _dot_nt(qs[it], ks[it].astype(BF16))

    pairs_per_group = H_B // G_B // 2
    groups = [(r, g) for g in range(G_B) for r in range(RB)]
    proj, xbcs, bgs, cgs, cbs = {}, [], {}, {}, {}

    def project(name, lo, hi):
        proj[name] = _dot(hn, wcat_ref[:, lo:hi])

    def ssd_inputs():
        for r in range(RB):
            xbc = proj["xbc"][r * T:(r + 1) * T]
            xbc_c = _causal_conv(xbc, xbc_buf[r], cwb_ref, cbb_ref)
            xbc_buf[r] = xbc[T - CONV_HDR:T, :]
            convb_ref[r] = xbc[T - 3:T, :]
            xbcs.append(_silu(xbc_c))
        for r, g in groups:
            bgs[(r, g)] = xbcs[r][:, d_b + g * N_STATE:d_b + (g + 1) * N_STATE].astype(BF16)
            cgs[(r, g)] = xbcs[r][:, d_b + (G_B + g) * N_STATE:d_b + (G_B + g + 1) * N_STATE].astype(BF16)
            cbs[(r, g)] = _dot_nt(cgs[(r, g)], bgs[(r, g)])

    def stage_weights(it):
        r, h = it
        gcol, grow = gcols[r], grows[r]
        b_col = gcol[:, L_F + h:L_F + h + 1]
        i_col = gcol[:, L_I + h:L_I + h + 1]
        b_row = grow[L_F + h:L_F + h + 1, :]
        i_row = grow[L_I + h:L_I + h + 1, :]
        m0 = m_alls[r][:, h:h + 1]
        dmat = jnp.where(causal, b_col - (b_row - i_row), NEG_INF)
        m_inter = b_col + m0
        m = jnp.maximum(m_inter, jnp.max(dmat, axis=-1, keepdims=True))
        w_inter = jnp.exp(m_inter - m)
        s = qks[it] * jnp.exp(dmat - m)
        n0 = n_ref[r, h:h + 1, :]
        den = (jnp.sum(s, axis=-1, keepdims=True)
               + w_inter * jnp.sum(qs[it].astype(F32) * n0, axis=-1, keepdims=True))
        m_last = m[T - 1:T, :]
        b_last = b_col[T - 1:T, :]
        dec = jnp.exp(b_last + m0 - m_last)
        kw = ks[it] * jnp.exp(b_last - b_col + i_col - m_last)
        n_ref[r, h:h + 1, :] = dec * n0 + jnp.sum(kw, axis=0, keepdims=True)
        m_news[r] = jnp.where(lane == h, m_last, m_news[r])
        st[it] = (s.astype(BF16), kw.astype(BF16), w_inter,
                  jnp.maximum(jnp.abs(den), jnp.exp(-m)), dec)

    def stage_readout(it):
        r, h = it
        s_b, kw_b, w_inter, den, dec = st[it]
        c0 = c_ref[r, h]
        num = _dot(s_b, vs[it]) + w_inter * _dot(qs[it], c0.astype(BF16))
        c_ref[r, h] = dec * c0 + _dot_tn(kw_b, vs[it])
        dd[it] = num / den

    def stage_head_out(it):
        r, h = it
        hh = dd[it]
        hh = hh * _rms_scale(hh) * na_ref[:, hsl(h)]
        merged[r * T:(r + 1) * T, hsl(h)] = (
            hh * jax.nn.sigmoid(proj["za"][r * T:(r + 1) * T, hsl(h)])).astype(BF16)

    pairs = [(r, pi) for pi in range(H_B // 2) for r in range(RB)]
    psl = lambda pi: slice(pi * LANES, (pi + 1) * LANES)
    sw = {}

    def stage_decay(pr):
        r, pi = pr
        g = pi // pairs_per_group
        gcol, grow = gcols[r], grows[r]
        xpair = xbcs[r][:, psl(pi)]
        scs, a_cols, w_cols, a_lasts = [], [], [], []
        for j in (2 * pi, 2 * pi + 1):
            a_col = gcol[:, L_DTA + j:L_DTA + j + 1]
            a_row = grow[L_DTA + j:L_DTA + j + 1, :]
            dt_col = gcol[:, L_DT + j:L_DT + j + 1]
            dt_row = grow[L_DT + j:L_DT + j + 1, :]
            decay = jnp.exp(jnp.where(causal, a_col - a_row, NEG_INF))
            scs.append((cbs[(r, g)] * decay * dt_row).astype(BF16))
            a_last = a_col[T - 1:T, :]
            a_cols.append(a_col)
            a_lasts.append(a_last)
            w_cols.append(jnp.exp(a_last - a_col) * dt_col)
        sw[pr] = (scs, xpair.astype(BF16),
                  (xpair * jnp.where(left, w_cols[0], w_cols[1])).astype(BF16),
                  jnp.exp(jnp.where(left, a_cols[0], a_cols[1])),
                  jnp.exp(jnp.where(top, a_lasts[0], a_lasts[1])))
    def stage_pair_out(pr):
        r, pi = pr
        g = pi // pairs_per_group
        scs, xpb, xw, ea, ea_last = sw[pr]
        s0 = s_ref[r, pi]
        y = jnp.where(left, _dot(scs[0], xpb), _dot(scs[1], xpb))
        y = y + ea * _dot_nt(cgs[(r, g)], s0.astype(BF16))
        s_ref[r, pi] = ea_last * s0 + _dot_tn(xw, bgs[(r, g)])
        y = y + dsk_ref[:, psl(pi)] * xbcs[r][:, psl(pi)]
        y_buf[r, :, psl(pi)] = y * _silu(proj["zb"][r * T:(r + 1) * T, psl(pi)])

    def each(stage, seq):
        for e in seq:
            stage(e)

    each(stage_qkv, items)
    project("xbc", 2 * d_a + d_b, off_small)
    each(stage_qk, items)
    project("za", d_a, 2 * d_a)
    ssd_inputs()
    each(stage_weights, items)
    each(stage_readout, items)
    project("zb", 2 * d_a, 2 * d_a + d_b)
    for r in range(RB):
        m_ref[r] = m_news[r]
    each(stage_head_out, items)
    each(stage_decay, pairs)
    each(stage_pair_out, pairs)
    gw = d_b // G_B
    for r in range(RB):
        for g in range(G_B):
            yg = y_buf[r, :, g * gw:(g + 1) * gw]
            merged[r * T:(r + 1) * T, d_a + g * gw:d_a + (g + 1) * gw] = (
                yg * _rms_scale(yg) * nb_ref[:, g * gw:(g + 1) * gw]).astype(BF16)

    @pl.when(c > 0)
    def _out():
        out2 = x2 + _dot(merged[...], wout_ref[...])

        @pl.when(jnp.logical_or(c > 1, p > 0))
        def _wait_previous():
            for r in range(RB):
                out_copy(r, 0).wait()

        for r in range(RB):
            hout[r] = out2[r * T:(r + 1) * T]
            out_copy(r, (p * RB + r) * seq + (c - 1) * T).start()

        @pl.when(jnp.logical_and(p == last_p, c == last_c))
        def _drain():
            for r in range(RB):
                out_copy(r, 0).wait()


def _const_spec(shape):
    nd = len(shape)
    return pl.BlockSpec(shape, lambda b, c, _nd=nd: (0,) * _nd)


def _prompt_mixer(x_prompt, xmeta, p, n_extra_rows):
    bsz, seq, d = x_prompt.shape
    assert n_extra_rows == CHUNK and seq % CHUNK == 0
    n_chunks = seq // CHUNK + 1
    cps = seq // CHUNK
    d_a = H_A * DH_A
    conv_b = H_B * HD_B + 2 * G_B * N_STATE
    consts = [p["nmix"], p["wcat"], p["bsm"], p["alog"], p["cwa"], p["cba"], p["cwb"], p["cbb"],
              p["wq"], p["wk"], p["wv"], p["na"], p["nb"], p["dsk"], p["wout"]]
    rb = PROMPT_ROWS
    assert bsz % rb == 0
    in_specs = [_const_spec(xmeta.shape),
                pl.BlockSpec((rb, CHUNK, d), lambda b, c: (b, jnp.maximum(c - 1, 0), 0))]
    in_specs += [_const_spec(a.shape) for a in consts]
    out_shape = (
        jax.ShapeDtypeStruct((bsz * seq + n_extra_rows, d), F32),
        jax.ShapeDtypeStruct((bsz, H_A, DH_A, DH_A), F32),
        jax.ShapeDtypeStruct((bsz, H_A, DH_A), F32),
        jax.ShapeDtypeStruct((bsz, 1, LANES), F32),
        jax.ShapeDtypeStruct((bsz, CONV_W - 1, d_a), F32),
        jax.ShapeDtypeStruct((bsz, H_B // 2, 2 * HD_B, N_STATE), F32),
        jax.ShapeDtypeStruct((bsz, CONV_W - 1, conv_b), F32),
    )
    out_specs = (
        pl.BlockSpec(memory_space=pl.ANY),
        pl.BlockSpec((rb, H_A, DH_A, DH_A), lambda b, c: (b, 0, 0, 0)),
        pl.BlockSpec((rb, H_A, DH_A), lambda b, c: (b, 0, 0)),
        pl.BlockSpec((rb, 1, LANES), lambda b, c: (b, 0, 0)),
        pl.BlockSpec((rb, CONV_W - 1, d_a), lambda b, c: (b, 0, 0)),
        pl.BlockSpec((rb, H_B // 2, 2 * HD_B, N_STATE), lambda b, c: (b, 0, 0, 0)),
        pl.BlockSpec((rb, CONV_W - 1, conv_b), lambda b, c: (b, 0, 0)),
    )
    return pl.pallas_call(
        functools.partial(_prompt_rows_kernel, seq=seq, n_prompt_rows=bsz * seq),
        out_shape=out_shape,
        grid=(bsz // rb, n_chunks),
        in_specs=in_specs,
        out_specs=out_specs,
        scratch_shapes=[
            pltpu.VMEM((rb, CONV_HDR, d_a), F32),
            pltpu.VMEM((rb, CONV_HDR, conv_b), F32),
            pltpu.VMEM((rb, CHUNK, H_B * HD_B), F32),
            pltpu.VMEM((rb * CHUNK, d_a + H_B * HD_B), BF16),
            pltpu.VMEM((rb, CHUNK, d), F32),
            pltpu.SemaphoreType.DMA((rb,)),
        ],
        compiler_params=pltpu.CompilerParams(
            dimension_semantics=("arbitrary", "arbitrary"), vmem_limit_bytes=VMEM_LIMIT),
        name="prompt_mixer",
    )(xmeta, x_prompt, *consts)


def _regroup_w_in_kernel(w_ref, o_ref):
    d_a = H_A * DH_A
    d_b = H_B * HD_B
    conv_b = d_b + 2 * G_B * N_STATE
    o_i = 2 * d_a
    o_f = o_i + H_A
    o_zb = o_f + H_A
    o_xbc = o_zb + d_b
    o_dt = o_xbc + conv_b
    rows = w_ref.shape[0]
    o_ref[:, 0:2 * d_a] = w_ref[:, 0:2 * d_a].astype(BF16)
    o_ref[:, 2 * d_a:2 * d_a + d_b] = w_ref[:, o_zb:o_zb + d_b].astype(BF16)
    o_ref[:, 2 * d_a + d_b:2 * d_a + d_b + conv_b] = w_ref[:, o_xbc:o_xbc + conv_b].astype(BF16)
    small = jnp.concatenate(
        [w_ref[:, o_f:o_f + H_A], w_ref[:, o_dt:o_dt + H_B], w_ref[:, o_i:o_i + H_A],
         w_ref[:, o_dt:o_dt + H_B], jnp.zeros((rows, LANES - (L_DT + H_B)), F32)], axis=1)
    o_ref[:, 2 * d_a + d_b + conv_b:] = small.astype(BF16)


def _prep_mixer_params(norm_mix, w_in, conv_a_w, conv_a_b, w_q, w_k, w_v, b_i, b_f, norm_a,
                       conv_b_w, conv_b_b, dt_bias, a_log, d_skip, norm_b, w_out):
    d_a = H_A * DH_A
    d_b = H_B * HD_B
    conv_b = d_b + 2 * G_B * N_STATE
    d_model, d_in = w_in.shape[1], w_in.shape[2]
    n_cols = 2 * d_a + d_b + conv_b + LANES
    rows = 256
    assert w_in.shape[0] == 1 and d_model % rows == 0
    wcat = pl.pallas_call(
        _regroup_w_in_kernel,
        out_shape=jax.ShapeDtypeStruct((d_model, n_cols), BF16),
        grid=(d_model // rows,),
        in_specs=[pl.BlockSpec((None, rows, d_in), lambda i: (0, i, 0))],
        out_specs=pl.BlockSpec((rows, n_cols), lambda i: (i, 0)),
        compiler_params=pltpu.CompilerParams(
            dimension_semantics=("arbitrary",), vmem_limit_bytes=VMEM_LIMIT),
        name="regroup_w_in",
    )(w_in.astype(F32))

    def lanes(parts):
        pieces, at = [], 0
        for off, a in parts:
            pieces += [jnp.zeros((1, off - at), F32), a.astype(F32)]
            at = off + a.shape[1]
        return jnp.concatenate(pieces + [jnp.zeros((1, LANES - at), F32)], axis=1)

    return dict(
        nmix=norm_mix.reshape(1, -1).astype(F32),
        wcat=wcat,
        bsm=lanes([(L_F, b_f), (L_DTA, dt_bias), (L_I, b_i), (L_DT, dt_bias)]),
        alog=lanes([(L_DTA, a_log)]),
        cwa=conv_a_w.reshape(CONV_W, d_a).astype(F32), cba=conv_a_b.reshape(1, d_a).astype(F32),
        cwb=conv_b_w.reshape(CONV_W, conv_b).astype(F32), cbb=conv_b_b.reshape(1, conv_b).astype(F32),
        wq=w_q.reshape(H_A, DH_A, DH_A).astype(BF16), wk=w_k.reshape(H_A, DH_A, DH_A).astype(BF16),
        wv=w_v.reshape(H_A, DH_A, DH_A).astype(BF16),
        na=norm_a.reshape(1, d_a).astype(F32), nb=norm_b.reshape(1, d_b).astype(F32),
        dsk=jnp.repeat(d_skip.reshape(H_B).astype(F32), HD_B)[None, :],
        wout=w_out.reshape(d_a + d_b, -1).astype(BF16),
    )


SAMPLE_BLOCK = 8


def _expand_lanes(vals, first_lane, n_heads, width):
    r = lax.broadcasted_iota(jnp.int32, (LANES, n_heads * width), 0) - first_lane
    c = lax.broadcasted_iota(jnp.int32, (LANES, n_heads * width), 1)
    sel = jnp.logical_and(c >= r * width, c < (r + 1) * width)
    e = jnp.where(sel, 1.0, 0.0).astype(BF16)
    hi, mid, lo = _split3(vals)
    return (_dot(hi, e) + _dot(mid, e)) + _dot(lo, e)


def _sample_pre_kernel(x_ref, nmix_ref, wcat_ref, bsm_ref, alog_ref, cwa_ref, cba_ref, cwb_ref, cbb_ref,
                       wq_ref, wk_ref, wv_ref, dsk_ref, conva_ref, convb_ref, n0_ref, m0_ref,
                       conva_out, convb_out, n1_out, m1_out, g_out, qt_out, kwt_out, xwt_out,
                       v_out, bc_out, a1_out, w1_out, den_out, y1_out, ea_out, zbs_out, zas_out):
    d_a = H_A * DH_A
    d_b = H_B * HD_B
    conv_b = d_b + 2 * G_B * N_STATE
    shift_i = LANES - (L_I - L_F)
    x = x_ref[...]
    hn = (x * _rms_scale(x) * nmix_ref[...]).astype(BF16)
    lane = lax.broadcasted_iota(jnp.int32, (1, LANES), 1)
    lane_f = lane < L_DTA
    lane_dta = jnp.logical_and(lane >= L_DTA, lane < L_I)
    pre = _dot(hn, wcat_ref[:, 2 * d_a + d_b + conv_b:]) + bsm_ref[...]
    sp, lsig = _softplus_parts(pre)
    a_neg = jnp.where(lane_dta, -jnp.exp(alog_ref[...]), 0.0)
    pre_al = pltpu.roll(pre, shift_i, axis=1)
    sp_al = pltpu.roll(sp, shift_i, axis=1)
    m_inter = lsig + m0_ref[...]
    m = jnp.maximum(m_inter, pre_al)
    w_inter = jnp.exp(m_inter - m)
    sfac = jnp.exp(pre_al - m)
    ea = jnp.exp(sp * a_neg)
    dt = sp_al

    xa = _dot(hn, wcat_ref[:, 0:d_a])
    xc = (cwa_ref[0:1, :] * conva_ref[:, 0:d_a] + cwa_ref[1:2, :] * conva_ref[:, d_a:2 * d_a]
          + cwa_ref[2:3, :] * conva_ref[:, 2 * d_a:3 * d_a] + cwa_ref[3:4, :] * xa + cba_ref[...])
    conva_out[:, 0:2 * d_a] = conva_ref[:, d_a:3 * d_a]
    conva_out[:, 2 * d_a:3 * d_a] = xa
    xc = _silu(xc).astype(BF16)
    xab = xa.astype(BF16)
    sf_e = _expand_lanes(sfac, L_F, H_A, DH_A)
    w_e = _expand_lanes(w_inter, L_F, H_A, DH_A)
    qk8 = jnp.zeros((x.shape[0], LANES), F32)
    qn8 = jnp.zeros((x.shape[0], LANES), F32)
    for h in range(H_A):
        sl = slice(h * DH_A, (h + 1) * DH_A)
        q = _dot(xc[:, sl], wq_ref[h])
        k = _dot(xc[:, sl], wk_ref[h]) * (DH_A ** -0.5)
        v = _dot(xab[:, sl], wv_ref[h])
        kw = k * sf_e[:, sl]
        qk8 = jnp.where(lane == h, jnp.sum(q * k, axis=-1, keepdims=True), qk8)
        qn8 = jnp.where(lane == h, jnp.sum(q * n0_ref[:, sl], axis=-1, keepdims=True), qn8)
        n1_out[:, sl] = w_e[:, sl] * n0_ref[:, sl] + kw
        v_out[:, sl] = v
        qt_out[h] = q.T
        kwt_out[h] = kw.T
    s8 = qk8 * sfac
    a1_out[...] = _expand_lanes(s8, L_F, H_A, DH_A) * v_out[...]
    w1_out[...] = w_e
    den_out[...] = jnp.maximum(jnp.abs(_expand_lanes(s8 + w_inter * qn8, L_F, H_A, DH_A)),
                               jnp.exp(-_expand_lanes(m, L_F, H_A, DH_A)))
    m1_out[...] = m
    g_out[...] = jnp.where(lane_f, w_inter, jnp.where(lane_dta, ea, 0.0))
    zas_out[...] = jax.nn.sigmoid(_dot(hn, wcat_ref[:, d_a:2 * d_a]))

    off_xbc = 2 * d_a + d_b
    xbc = _dot(hn, wcat_ref[:, off_xbc:off_xbc + conv_b])
    xbc_c = (cwb_ref[0:1, :] * convb_ref[:, 0:conv_b] + cwb_ref[1:2, :] * convb_ref[:, conv_b:2 * conv_b]
             + cwb_ref[2:3, :] * convb_ref[:, 2 * conv_b:3 * conv_b] + cwb_ref[3:4, :] * xbc + cbb_ref[...])
    convb_out[:, 0:2 * conv_b] = convb_ref[:, conv_b:3 * conv_b]
    convb_out[:, 2 * conv_b:3 * conv_b] = xbc
    xbc_c = _silu(xbc_c)
    xs = xbc_c[:, 0:d_b]
    bc = xbc_c[:, d_b:conv_b]
    bc_out[...] = bc
    heads_per_group = H_B // G_B
    cbl = jnp.zeros((x.shape[0], LANES), F32)
    for g in range(G_B):
        cb_g = jnp.sum(bc[:, g * N_STATE:(g + 1) * N_STATE]
                       * bc[:, (G_B + g) * N_STATE:(G_B + g + 1) * N_STATE], axis=-1, keepdims=True)
        in_g = jnp.logical_and(lane >= L_DTA + g * heads_per_group,
                               lane < L_DTA + (g + 1) * heads_per_group)
        cbl = jnp.where(in_g, cb_g, cbl)
    dt_e = _expand_lanes(dt, L_DTA, H_B, HD_B)
    y1_out[...] = _expand_lanes(cbl * dt, L_DTA, H_B, HD_B) * xs + dsk_ref[...] * xs
    ea_out[...] = _expand_lanes(ea, L_DTA, H_B, HD_B)
    zbs_out[...] = _silu(_dot(hn, wcat_ref[:, 2 * d_a:2 * d_a + d_b]))
    xw = xs * dt_e
    for pi in range(H_B // 2):
        xwt_out[pi] = xw[:, pi * LANES:(pi + 1) * LANES].T


def _sample_state_kernel(g_ref, c0_ref, s0_ref, qt_ref, kwt_ref, xwt_ref, v_ref, bc_ref,
                         c1_ref, s1_ref, qc_ref, ysi_ref):
    i = pl.program_id(0)
    bb = c0_ref.shape[0]
    shift = lax.rem(LANES - lax.rem(i * bb, LANES), LANES)
    lane = lax.broadcasted_iota(jnp.int32, (1, LANES), 1)
    top = lax.broadcasted_iota(jnp.int32, (LANES, 1), 0) < HD_B
    heads_per_group = H_B // G_B
    for h in range(H_A):
        sl = slice(h * DH_A, (h + 1) * DH_A)
        qt = pltpu.roll(qt_ref[h], shift, axis=1)
        kwt = pltpu.roll(kwt_ref[h], shift, axis=1)
        for r in range(bb):
            b = i * bb + r
            c0 = c0_ref[r, h]
            dec = g_ref[b, L_F + h]
            v_row = v_ref[r:r + 1, sl]
            qc_ref[r:r + 1, sl] = jnp.sum(c0 * qt[:, r:r + 1], axis=0, keepdims=True)
            c1_ref[r, h] = dec * c0 + kwt[:, r:r + 1] * v_row
    for pi in range(H_B // 2):
        g = (2 * pi) // heads_per_group
        sl = slice(pi * LANES, (pi + 1) * LANES)
        xwt = pltpu.roll(xwt_ref[pi], shift, axis=1)
        acc = jnp.zeros((LANES, LANES), F32)
        for r in range(bb):
            b = i * bb + r
            s0 = s0_ref[r, pi]
            b_row = bc_ref[r:r + 1, g * N_STATE:(g + 1) * N_STATE]
            c_row = bc_ref[r:r + 1, (G_B + g) * N_STATE:(G_B + g + 1) * N_STATE]
            col = jnp.sum(s0 * c_row, axis=-1, keepdims=True)
            acc = jnp.where(lane == r, col, acc)
            ea_rows = jnp.where(top, g_ref[b, L_DTA + 2 * pi], g_ref[b, L_DTA + 2 * pi + 1])
            s1_ref[r, pi] = ea_rows * s0 + xwt[:, r:r + 1] * b_row
        ysi_ref[:, sl] = acc.T[0:bb, :]


def _sample_post_kernel(a1_ref, w1_ref, den_ref, y1_ref, ea_ref, zbs_ref, zas_ref, x_ref, qc_ref, ysi_ref,
                        na_ref, nb_ref, wout_ref, hall_ref, hmid_ref, merged):
    del hall_ref
    d_a = H_A * DH_A
    d_b = H_B * HD_B
    hh = (a1_ref[...] + w1_ref[...] * qc_ref[...]) / den_ref[...]
    for h in range(H_A):
        sl = slice(h * DH_A, (h + 1) * DH_A)
        hs = hh[:, sl]
        merged[:, sl] = (hs * _rms_scale(hs) * na_ref[:, sl] * zas_ref[:, sl]).astype(BF16)
    y = (y1_ref[...] + ea_ref[...] * ysi_ref[...]) * zbs_ref[...]
    gw = d_b // G_B
    for g in range(G_B):
        yg = y[:, g * gw:(g + 1) * gw]
        merged[:, d_a + g * gw:d_a + (g + 1) * gw] = (
            yg * _rms_scale(yg) * nb_ref[:, g * gw:(g + 1) * gw]).astype(BF16)
    hmid_ref[...] = x_ref[...] + _dot(merged[...], wout_ref[...])


def _vmem_specs(arrays):
    return [pl.BlockSpec(a.shape, lambda *_, _nd=a.ndim: (0,) * _nd) for a in arrays]


def _sample_mixer(x, c0, n0, m0, conva, s0, convb, p, hmid_all, row_offset):
    nb, d = x.shape
    d_a = H_A * DH_A
    d_b = H_B * HD_B
    conv_b = d_b + 2 * G_B * N_STATE
    row = lambda w: jax.ShapeDtypeStruct((nb, w), F32)
    tile = lambda k: jax.ShapeDtypeStruct((k, LANES, nb), F32)
    pre_in = [x, p["nmix"], p["wcat"], p["bsm"], p["alog"], p["cwa"], p["cba"], p["cwb"], p["cbb"],
              p["wq"], p["wk"], p["wv"], p["dsk"], conva, convb, n0, m0]
    pre_out_shape = (row(3 * d_a), row(3 * conv_b), row(d_a), row(LANES), row(LANES),
                     tile(H_A), tile(H_A), tile(H_B // 2), row(d_a), row(2 * G_B * N_STATE),
                     row(d_a), row(d_a), row(d_a), row(d_b), row(d_b), row(d_b), row(d_a))
    (conva1, convb1, n1, m1, g8, qt, kwt, xwt, v, bc, a1, w1, den, y1, ea_e, zbs, zas) = pl.pallas_call(
        _sample_pre_kernel,
        out_shape=pre_out_shape,
        grid=(1,),
        in_specs=_vmem_specs(pre_in),
        out_specs=tuple(pl.BlockSpec(s.shape, lambda i, _nd=len(s.shape): (0,) * _nd) for s in pre_out_shape),
        compiler_params=pltpu.CompilerParams(
            dimension_semantics=("arbitrary",), vmem_limit_bytes=VMEM_LIMIT),
        name="sample_pre",
    )(*pre_in)

    bb = SAMPLE_BLOCK
    const3 = lambda k: pl.BlockSpec((k, LANES, nb), lambda i, g: (0, 0, 0))
    state_grid = pltpu.PrefetchScalarGridSpec(
        num_scalar_prefetch=1,
        grid=(nb // bb,),
        in_specs=[pl.BlockSpec((bb, H_A, DH_A, DH_A), lambda i, g: (i, 0, 0, 0)),
                  pl.BlockSpec((bb, H_B // 2, 2 * HD_B, N_STATE), lambda i, g: (i, 0, 0, 0)),
                  const3(H_A), const3(H_A), const3(H_B // 2),
                  pl.BlockSpec((bb, d_a), lambda i, g: (i, 0)),
                  pl.BlockSpec((bb, 2 * G_B * N_STATE), lambda i, g: (i, 0))],
        out_specs=(pl.BlockSpec((bb, H_A, DH_A, DH_A), lambda i, g: (i, 0, 0, 0)),
                   pl.BlockSpec((bb, H_B // 2, 2 * HD_B, N_STATE), lambda i, g: (i, 0, 0, 0)),
                   pl.BlockSpec((bb, d_a), lambda i, g: (i, 0)),
                   pl.BlockSpec((bb, d_b), lambda i, g: (i, 0))),
    )
    c1, s1, qc, ysi = pl.pallas_call(
        _sample_state_kernel,
        out_shape=(jax.ShapeDtypeStruct(c0.shape, F32), jax.ShapeDtypeStruct(s0.shape, F32),
                   row(d_a), row(d_b)),
        grid_spec=state_grid,
        compiler_params=pltpu.CompilerParams(
            dimension_semantics=("arbitrary",), vmem_limit_bytes=VMEM_LIMIT),
        name="sample_state",
    )(g8, c0, s0, qt, kwt, xwt, v, bc)

    post_in = [a1, w1, den, y1, ea_e, zbs, zas, x, qc, ysi, p["na"], p["nb"], p["wout"]]
    hmid_all = pl.pallas_call(
        _sample_post_kernel,
        out_shape=jax.ShapeDtypeStruct(hmid_all.shape, F32),
        grid=(1,),
        in_specs=_vmem_specs(post_in) + [pl.BlockSpec(memory_space=pl.ANY)],
        out_specs=pl.BlockSpec((nb, d), lambda i: (row_offset // nb, 0)),
        scratch_shapes=[pltpu.VMEM((nb, d_a + d_b), BF16)],
        input_output_aliases={len(post_in): 0},
        compiler_params=pltpu.CompilerParams(
            dimension_semantics=("arbitrary",), vmem_limit_bytes=VMEM_LIMIT),
        name="sample_post",
    )(*post_in, hmid_all)
    return hmid_all, c1, n1, m1, conva1, s1, convb1


R_EA, R_EB, R_RA, R_RB, R_GA, R_GB = 0, 1, 2, 3, 4, 5
RL_E = N_EGROUPS


def _router_kernel(h_ref, nf_ref, whi_ref, wmid_ref, br_ref, xn_ref, info_ref, cnt_ref, carry):
    i = pl.program_id(0)
    tr = h_ref.shape[0]

    @pl.when(i == 0)
    def _init():
        carry[...] = jnp.zeros_like(carry)

    h = h_ref[...]
    xn = h * _rms_scale(h) * nf_ref[...]
    _store_token_tiles(xn_ref, xn)
    x_hi, x_mid, _ = _split3(xn)
    logits = (_dot(x_hi, whi_ref[...]) + _dot(x_hi, wmid_ref[...]) + _dot(x_mid, whi_ref[...])
              + br_ref[...])
    lane_i = lax.broadcasted_iota(jnp.int32, (1, LANES), 1)
    lane = lane_i.astype(F32)
    big = float(LANES)

    def first_lane_of(cond):
        return jnp.min(jnp.where(cond, lane, big), axis=-1, keepdims=True)

    l1 = jnp.where(lane_i < N_EGROUPS, logits, NEG_INF)
    e1 = jnp.exp(l1 - jnp.max(l1, axis=-1, keepdims=True))
    p1 = e1 / jnp.sum(e1, axis=-1, keepdims=True)
    gp = jnp.max(p1, axis=-1, keepdims=True)
    gidx = first_lane_of(p1 == gp)
    lo = RL_E + N_EPG * gidx
    l2 = jnp.where(jnp.logical_and(lane >= lo, lane < lo + N_EPG), logits, NEG_INF)
    va = jnp.max(l2, axis=-1, keepdims=True)
    ia = first_lane_of(l2 == va)
    l2b = jnp.where(lane == ia, NEG_INF, l2)
    vb = jnp.max(l2b, axis=-1, keepdims=True)
    ib = first_lane_of(l2b == vb)
    eb = jnp.exp(vb - va)
    wa = 1.0 / (1.0 + eb)
    wb = eb / (1.0 + eb)

    is_a = lane == ia
    is_b = lane == ib
    onehot = jnp.where(jnp.logical_or(is_a, is_b), 1.0, 0.0)
    ri = lax.broadcasted_iota(jnp.int32, (tr, tr), 0)
    ci = lax.broadcasted_iota(jnp.int32, (tr, tr), 1)
    tri = jnp.where(ri >= ci, 1.0, 0.0).astype(BF16)
    incl = _dot(tri, onehot.astype(BF16))
    excl = incl - onehot + carry[...]
    rank_a = jnp.sum(jnp.where(is_a, excl, 0.0), axis=-1, keepdims=True)
    rank_b = jnp.sum(jnp.where(is_b, excl, 0.0), axis=-1, keepdims=True)
    carry[...] = carry[...] + incl[tr - 1:tr, :]
    cnt_ref[...] = carry[...]

    info = jnp.where(lane_i == R_EA, ia - RL_E, 0.0)
    info = jnp.where(lane_i == R_EB, ib - RL_E, info)
    info = jnp.where(lane_i == R_RA, rank_a, info)
    info = jnp.where(lane_i == R_RB, rank_b, info)
    info = jnp.where(lane_i == R_GA, gp * wa, info)
    info = jnp.where(lane_i == R_GB, gp * wb, info)
    info_ref[...] = info


def _row_tile(n, candidates):
    for t in candidates:
        if n % t == 0:
            return t
    raise ValueError(f"no row tile for {n} rows among {candidates}")


def _router(hmid, rp):
    n, d = hmid.shape
    assert d == TOK_TILE_ROWS * LANES
    tr = _row_tile(n, (1376, 768, 512, 384, 256, 128))
    return pl.pallas_call(
        _router_kernel,
        out_shape=(jax.ShapeDtypeStruct((n * TOK_TILE_ROWS, LANES), F32),
                   jax.ShapeDtypeStruct((n, LANES), F32),
                   jax.ShapeDtypeStruct((1, LANES), F32)),
        grid=(n // tr,),
        in_specs=[pl.BlockSpec((tr, d), lambda i: (i, 0)),
                  pl.BlockSpec((1, d), lambda i: (0, 0)),
                  pl.BlockSpec((d, LANES), lambda i: (0, 0)),
                  pl.BlockSpec((d, LANES), lambda i: (0, 0)),
                  pl.BlockSpec((1, LANES), lambda i: (0, 0))],
        out_specs=(pl.BlockSpec((tr * TOK_TILE_ROWS, LANES), lambda i: (i, 0)),
                   pl.BlockSpec((tr, LANES), lambda i: (i, 0)),
                   pl.BlockSpec((1, LANES), lambda i: (0, 0))),
        scratch_shapes=[pltpu.VMEM((1, LANES), F32)],
        compiler_params=pltpu.CompilerParams(
            dimension_semantics=("arbitrary",), vmem_limit_bytes=VMEM_LIMIT),
        name="router",
    )(hmid, rp["nf"], rp["whi"], rp["wmid"], rp["br"])


def _prep_router_params(norm_ffn, w_r1, b_r1, w_r2, b_r2):
    d = w_r1.shape[1]
    w = jnp.concatenate([w_r1.reshape(d, N_EGROUPS).astype(F32), w_r2.reshape(d, N_EXPERTS).astype(F32),
                         jnp.zeros((d, LANES - RL_E - N_EXPERTS), F32)], axis=1)
    whi = w.astype(BF16)
    wmid = (w - whi.astype(F32)).astype(BF16)
    br = jnp.concatenate([b_r1.reshape(1, N_EGROUPS).astype(F32), b_r2.reshape(1, N_EXPERTS).astype(F32),
                          jnp.zeros((1, LANES - RL_E - N_EXPERTS), F32)], axis=1)
    return dict(nf=norm_ffn.reshape(1, d).astype(F32), whi=whi, wmid=wmid, br=br)


FFN_TM = 256
N_GATHER_SLOTS = 3


def _start_tile_gather(first_row_of, n_rows, src_hbm, dst, sem, priority_of):
    for r in range(n_rows):
        start = pl.multiple_of(first_row_of(r), TOK_TILE_ROWS)
        pltpu.make_async_copy(src_hbm.at[pl.ds(start, TOK_TILE_ROWS), :],
                              dst.at[pl.ds(r * TOK_TILE_ROWS, TOK_TILE_ROWS), :],
                              sem).start(priority=priority_of(r))


def _wait_tile_gather(n_rows, src_hbm, dst, sem):
    pltpu.make_async_copy(src_hbm.at[pl.ds(0, n_rows * TOK_TILE_ROWS), :], dst, sem).wait()


TAB_OFF, TAB_CNT, TAB_TILE_EXPERT, TAB_NVALID = 0, 1, 2, 3
TAB_LANES = 2 * LANES


def _ffn_kernel(tab_ref, pa_ref, pb_ref, xn_hbm, wg_ref, wu_ref, wd_ref, ys_ref,
                src, xbuf, wbf, sem, *, n_tokens, tm):
    i = pl.program_id(0)
    n_valid = tab_ref[TAB_NVALID, 0]
    slot = lax.rem(i, N_GATHER_SLOTS)
    gather_priority = lambda r: 1

    @pl.when(i == 0)
    def _build_source_rows():
        for e in range(N_EXPERTS):
            cnt_e = tab_ref[TAB_CNT, RL_E + e]
            first = tab_ref[TAB_OFF, RL_E + e] + cnt_e
            n_pad = lax.rem(tm - lax.rem(cnt_e, tm), tm)

            def pad_body(r, carry, first=first):
                src[first + r] = 0
                return carry
            lax.fori_loop(0, n_pad, pad_body, 0)

        def body(t, carry):
            first_row = t * TOK_TILE_ROWS
            src[pa_ref[t]] = first_row
            src[pb_ref[t]] = first_row
            return carry
        lax.fori_loop(0, n_tokens, body, 0, unroll=8)
        _start_tile_gather(lambda r: src[r], tm, xn_hbm, xbuf.at[0], sem.at[0], gather_priority)
        second = jnp.where(n_valid > 1, tm, 0)
        _start_tile_gather(lambda r: src[second + r], tm, xn_hbm, xbuf.at[1], sem.at[1], gather_priority)

    changed = jnp.logical_or(i == 0, tab_ref[TAB_TILE_EXPERT, i]
                             != tab_ref[TAB_TILE_EXPERT, jnp.maximum(i - 1, 0)])

    @pl.when(jnp.logical_and(changed, i < n_valid))
    def _cast_weights():
        wbf[0] = wg_ref[...].astype(BF16)
        wbf[1] = wu_ref[...].astype(BF16)
        wbf[2] = wd_ref[...].astype(BF16)

    @pl.when(i < n_valid)
    def _compute():
        _wait_tile_gather(tm, xn_hbm, xbuf.at[slot], sem.at[slot])
        x = _load_token_tiles(xbuf.at[slot], tm).astype(BF16)
        hg = _dot(x, wbf[0])
        hu = _dot(x, wbf[1])
        y = _dot((_silu(hg) * hu).astype(BF16), wbf[2])
        base = jnp.where(i + 2 < n_valid, i + 2, 0) * tm
        ahead = lax.rem(i + 2, N_GATHER_SLOTS)
        _start_tile_gather(lambda r: src[base + r], tm, xn_hbm, xbuf.at[ahead], sem.at[ahead],
                           gather_priority)
        _store_token_tiles(ys_ref, y)

    @pl.when(i == n_valid - 1)
    def _drain():
        for k in (1, 2):
            s = lax.rem(i + k, N_GATHER_SLOTS)
            _wait_tile_gather(tm, xn_hbm, xbuf.at[s], sem.at[s])

    @pl.when(i >= n_valid)
    def _pad():
        ys_ref[...] = jnp.zeros_like(ys_ref)


def _expert_ffn(xn_tiles, tab, pos_a, pos_b, n_tiles, wg, wu, wd):
    n = pos_a.shape[0]
    tm = FFN_TM
    d, dff = wg.shape[1], wg.shape[2]
    rows = tm * TOK_TILE_ROWS
    idx = lambda i, tab, pa, pb: (tab[TAB_TILE_EXPERT, i], 0, 0)
    grid_spec = pltpu.PrefetchScalarGridSpec(
        num_scalar_prefetch=3,
        grid=(n_tiles,),
        in_specs=[pl.BlockSpec(memory_space=pl.ANY),
                  pl.BlockSpec((None, d, dff), idx),
                  pl.BlockSpec((None, d, dff), idx),
                  pl.BlockSpec((None, dff, d), idx)],
        out_specs=pl.BlockSpec((rows, LANES), lambda i, tab, pa, pb: (i, 0)),
        scratch_shapes=[pltpu.SMEM((n_tiles * tm,), jnp.int32),
                        pltpu.VMEM((N_GATHER_SLOTS, rows, LANES), F32),
                        pltpu.VMEM((3, d, dff), BF16),
                        pltpu.SemaphoreType.DMA((N_GATHER_SLOTS,))],
    )
    return pl.pallas_call(
        functools.partial(_ffn_kernel, n_tokens=n, tm=tm),
        out_shape=jax.ShapeDtypeStruct((n_tiles * rows, LANES), F32),
        grid_spec=grid_spec,
        compiler_params=pltpu.CompilerParams(
            dimension_semantics=("arbitrary",), vmem_limit_bytes=VMEM_LIMIT),
        name="expert_ffn",
    )(tab, pos_a, pos_b, xn_tiles, wg, wu, wd)


def _positions_kernel(info_ref, cnt_ref, pos_ref, tab_ref, *, tm, chunk):
    lane_i = lax.broadcasted_iota(jnp.int32, (1, LANES), 1)
    lane = lane_i.astype(F32)
    is_expert = jnp.logical_and(lane_i >= RL_E, lane_i < RL_E + N_EXPERTS)
    cnt = jnp.where(is_expert, cnt_ref[...], 0.0)
    padded = jnp.floor((cnt + (tm - 1)) / tm) * tm
    ri = lax.broadcasted_iota(jnp.int32, (LANES, LANES), 0)
    ci = lax.broadcasted_iota(jnp.int32, (LANES, LANES), 1)
    before = jnp.where(ri < ci, 1.0, 0.0).astype(BF16)
    hi, mid, lo = _split3(jnp.broadcast_to(padded, (SUBLANES, LANES)))
    off = ((_dot(hi, before) + _dot(mid, before)) + _dot(lo, before))[0:1, :]
    pick = jnp.where(lax.broadcasted_iota(jnp.int32, (SUBLANES, LANES), 0) == lane_i, 1.0, 0.0).astype(BF16)

    total = jnp.sum(padded, axis=-1, keepdims=True)
    tile_row = lax.broadcasted_iota(jnp.int32, (TAB_LANES, 1), 0).astype(F32) * tm
    ends = off + padded
    done = jnp.logical_and(is_expert, ends <= jnp.minimum(tile_row, total - 1.0))
    te_col = jnp.sum(jnp.where(done, 1.0, 0.0), axis=-1, keepdims=True)
    te_rows = _dot_nt(pick, jnp.where(lane_i == 0, te_col, 0.0).astype(BF16))
    tab_ref[...] = jnp.zeros_like(tab_ref)
    tab_ref[TAB_OFF:TAB_OFF + 1, 0:LANES] = off.astype(jnp.int32)
    tab_ref[TAB_CNT:TAB_CNT + 1, 0:LANES] = cnt.astype(jnp.int32)
    tab_ref[TAB_TILE_EXPERT:TAB_TILE_EXPERT + 1, :] = te_rows[0:1, :].astype(jnp.int32)
    tab_ref[TAB_NVALID:TAB_NVALID + 1, 0:LANES] = jnp.broadcast_to(total / tm, (1, LANES)).astype(jnp.int32)

    n = info_ref.shape[0]
    for c0 in range(0, n, chunk):
        blk = info_ref[c0:c0 + chunk, :]
        lane_a = blk[:, R_EA:R_EA + 1] + RL_E
        lane_b = blk[:, R_EB:R_EB + 1] + RL_E
        pos_a = blk[:, R_RA:R_RA + 1] + jnp.sum(jnp.where(lane == lane_a, off, 0.0), axis=-1, keepdims=True)
        pos_b = blk[:, R_RB:R_RB + 1] + jnp.sum(jnp.where(lane == lane_b, off, 0.0), axis=-1, keepdims=True)
        z_hi, z_mid, z_lo = _split3(jnp.where(lane_i == 0, pos_a, jnp.where(lane_i == 1, pos_b, 0.0)))
        rows = (_dot_nt(pick, z_hi) + _dot_nt(pick, z_mid)) + _dot_nt(pick, z_lo)
        pos_ref[:, c0:c0 + chunk] = rows.astype(jnp.int32)


def _routing_tables(info, counts, n_tiles, tm):
    n = info.shape[0]
    groups = n // LANES
    assert n_tiles <= TAB_LANES
    chunk = LANES * max(g for g in range(1, 65) if groups % g == 0)
    pos, tab = pl.pallas_call(
        functools.partial(_positions_kernel, tm=tm, chunk=chunk),
        out_shape=(jax.ShapeDtypeStruct((SUBLANES, n), jnp.int32),
                   jax.ShapeDtypeStruct((SUBLANES, TAB_LANES), jnp.int32)),
        grid=(1,),
        in_specs=[pl.BlockSpec((n, LANES), lambda i: (0, 0)), pl.BlockSpec((1, LANES), lambda i: (0, 0))],
        out_specs=(pl.BlockSpec((SUBLANES, n), lambda i: (0, 0)),
                   pl.BlockSpec((SUBLANES, TAB_LANES), lambda i: (0, 0))),
        compiler_params=pltpu.CompilerParams(
            dimension_semantics=("arbitrary",), vmem_limit_bytes=VMEM_LIMIT),
        name="positions",
    )(info, counts)
    return tab, pos[0], pos[1]


def _combine_kernel(pa_ref, pb_ref, h_ref, info_ref, ys_hbm, nfin_ref, yp_ref, ysm_ref,
                    buf_a, buf_b, sem, *, n_prompt_tiles):
    i = pl.program_id(0)
    n_steps = pl.num_programs(0)
    tt = h_ref.shape[0]
    slot = lax.rem(i, N_GATHER_SLOTS)

    def start(tile, s):
        base = tile * tt
        _start_tile_gather(lambda r: pa_ref[base + r] * TOK_TILE_ROWS, tt, ys_hbm, buf_a.at[s],
                           sem.at[s], lambda r: 0)
        _start_tile_gather(lambda r: pb_ref[base + r] * TOK_TILE_ROWS, tt, ys_hbm, buf_b.at[s],
                           sem.at[s], lambda r: 1)

    @pl.when(i == 0)
    def _first():
        start(0, 0)
        start(lax.rem(1, n_steps), 1)

    _wait_tile_gather(tt, ys_hbm, buf_a.at[slot], sem.at[slot])
    _wait_tile_gather(tt, ys_hbm, buf_b.at[slot], sem.at[slot])
    info = info_ref[...]
    rows_a = _load_token_tiles(buf_a.at[slot], tt)
    rows_b = _load_token_tiles(buf_b.at[slot], tt)
    x = h_ref[...]
    start(lax.rem(i + 2, n_steps), lax.rem(i + 2, N_GATHER_SLOTS))
    h = x + info[:, R_GA:R_GA + 1] * rows_a + info[:, R_GB:R_GB + 1] * rows_b
    y = h * _rms_scale(h) * nfin_ref[...]

    @pl.when(i < n_prompt_tiles)
    def _prompt():
        yp_ref[...] = y

    @pl.when(i >= n_prompt_tiles)
    def _sample():
        ysm_ref[...] = y

    @pl.when(i == n_steps - 1)
    def _drain():
        for k in (1, 2):
            s = lax.rem(i + k, N_GATHER_SLOTS)
            _wait_tile_gather(tt, ys_hbm, buf_a.at[s], sem.at[s])
            _wait_tile_gather(tt, ys_hbm, buf_b.at[s], sem.at[s])


def _combine(hmid, info, ys, pos_a, pos_b, nfin, n_prompt):
    n, d = hmid.shape
    tt = CHUNK
    n_prompt_tiles = n_prompt // tt
    n_sample = n - n_prompt
    grid_spec = pltpu.PrefetchScalarGridSpec(
        num_scalar_prefetch=2,
        grid=(n // tt,),
        in_specs=[pl.BlockSpec((tt, d), lambda i, pa, pb: (i, 0)),
                  pl.BlockSpec((tt, LANES), lambda i, pa, pb: (i, 0)),
                  pl.BlockSpec(memory_space=pl.ANY),
                  pl.BlockSpec((1, d), lambda i, pa, pb: (0, 0))],
        out_specs=(pl.BlockSpec((tt, d), lambda i, pa, pb: (jnp.minimum(i, n_prompt_tiles - 1), 0)),
                   pl.BlockSpec((tt, d), lambda i, pa, pb: (jnp.maximum(i - n_prompt_tiles, 0), 0))),
        scratch_shapes=[pltpu.VMEM((N_GATHER_SLOTS, tt * TOK_TILE_ROWS, LANES), F32),
                        pltpu.VMEM((N_GATHER_SLOTS, tt * TOK_TILE_ROWS, LANES), F32),
                        pltpu.SemaphoreType.DMA((N_GATHER_SLOTS,))],
    )
    return pl.pallas_call(
        functools.partial(_combine_kernel, n_prompt_tiles=n_prompt_tiles),
        out_shape=(jax.ShapeDtypeStruct((n_prompt, d), F32),
                   jax.ShapeDtypeStruct((n_sample, d), F32)),
        grid_spec=grid_spec,
        compiler_params=pltpu.CompilerParams(
            dimension_semantics=("arbitrary",), vmem_limit_bytes=VMEM_LIMIT),
        name="combine",
    )(pos_a, pos_b, hmid, info, ys, nfin)


def _moe_and_final_norm(hmid, n_prompt, rp, wg, wu, wd, nfin):
    n = hmid.shape[0]
    tm = FFN_TM
    n_tiles = (2 * n + N_EXPERTS * (tm - 1)) // tm
    xn, info, counts = _router(hmid, rp)
    tab, pos_a, pos_b = _routing_tables(info, counts, n_tiles, tm)
    ys = _expert_ffn(xn, tab, pos_a, pos_b, n_tiles, wg, wu, wd)
    return _combine(hmid, info, ys, pos_a, pos_b, nfin, n_prompt)


def kernel(x_prompt, x_sample, state_mlstm_C, state_mlstm_n, state_mlstm_m, state_mlstm_conv, state_ssm, state_ssm_conv, meta_tokens, norm_mix, w_in, conv_a_w, conv_a_b, w_q, w_k, w_v, b_i, b_f, norm_a, conv_b_w, conv_b_b, dt_bias, a_log, d_skip, norm_b, w_out, norm_ffn, w_r1, b_r1, w_r2, b_r2, w_gate, w_up, w_down, norm_final):
    bsz, seq, d = x_prompt.shape
    nb = x_sample.shape[0]
    d_a = H_A * DH_A
    conv_b = H_B * HD_B + 2 * G_B * N_STATE
    assert w_in.shape[0] == 1 and x_sample.shape[1] == 1 and seq % CHUNK == 0 and nb == CHUNK
    mp = _prep_mixer_params(norm_mix, w_in, conv_a_w, conv_a_b, w_q, w_k, w_v, b_i, b_f, norm_a,
                            conv_b_w, conv_b_b, dt_bias, a_log, d_skip, norm_b, w_out)
    rp = _prep_router_params(norm_ffn, w_r1, b_r1, w_r2, b_r2)
    xmeta = jnp.concatenate([jnp.zeros((CHUNK - N_META, d), F32), meta_tokens.astype(F32)], 0)

    hmid, p_c, p_n, p_m, p_ca, p_s, p_cb = _prompt_mixer(x_prompt.astype(F32), xmeta, mp, nb)
    m0 = jnp.pad(state_mlstm_m.reshape(nb, H_A).astype(F32), ((0, 0), (0, LANES - H_A)))
    hmid, s_c, s_n, s_m, s_ca, s_s, s_cb = _sample_mixer(
        x_sample.reshape(nb, d).astype(F32),
        state_mlstm_C.reshape(nb, H_A, DH_A, DH_A).astype(F32),
        state_mlstm_n.reshape(nb, d_a).astype(F32),
        m0,
        state_mlstm_conv.reshape(nb, (CONV_W - 1) * d_a).astype(F32),
        state_ssm.reshape(nb, H_B // 2, 2 * HD_B, N_STATE).astype(F32),
        state_ssm_conv.reshape(nb, (CONV_W - 1) * conv_b).astype(F32),
        mp, hmid, bsz * seq)

    wshape = w_gate.shape[1:]
    y_p, y_s = _moe_and_final_norm(
        hmid, bsz * seq, rp, w_gate.reshape(wshape).astype(F32), w_up.reshape(wshape).astype(F32),
        w_down.reshape(w_down.shape[1:]).astype(F32), norm_final.reshape(1, d).astype(F32))

    return (y_p.reshape(bsz, seq, d), y_s.reshape(nb, 1, d),
            p_c.reshape(1, bsz, H_A, DH_A, DH_A), p_n.reshape(1, bsz, H_A, DH_A),
            p_m[:, 0, :H_A].reshape(1, bsz, H_A), p_ca.reshape(1, bsz, CONV_W - 1, d_a),
            p_s.reshape(1, bsz, H_B, HD_B, N_STATE), p_cb.reshape(1, bsz, CONV_W - 1, conv_b),
            s_c.reshape(1, nb, H_A, DH_A, DH_A), s_n.reshape(1, nb, H_A, DH_A),
            s_m[:, :H_A].reshape(1, nb, H_A), s_ca.reshape(1, nb, CONV_W - 1, d_a),
            s_s.reshape(1, nb, H_B, HD_B, N_STATE), s_cb.reshape(1, nb, CONV_W - 1, conv_b))
```

```python
import functools
import math

import jax
import jax.numpy as jnp
from jax import lax
from jax.experimental import pallas as pl
from jax.experimental.pallas import tpu as pltpu

F32 = jnp.float32
BF16 = jnp.bfloat16

EPS = 1e-6
N_META = 16
CONV_W = 4
CHUNK = 128
H_A = 8
DH_A = 128
H_B = 16
HD_B = 64
N_STATE = 128
G_B = 2
N_EGROUPS = 4
N_EPG = 4
N_EXPERTS = 16
LANES = 128
SUBLANES = 8
CONV_HDR = SUBLANES
VMEM_LIMIT = 56 * 1024 * 1024

L_F = 0
L_DTA = 8
L_I = 24
L_DT = 32

NEG_INF = float("-inf")


def _dot(a, b):
    return jnp.dot(a, b, preferred_element_type=F32)


def _dot_nt(a, b):
    return lax.dot_general(a, b, (((1,), (1,)), ((), ())), preferred_element_type=F32)


def _dot_tn(a, b):
    return lax.dot_general(a, b, (((0,), (0,)), ((), ())), preferred_element_type=F32)


def _split3(x):
    hi = x.astype(BF16)
    r = x - hi.astype(F32)
    mid = r.astype(BF16)
    lo = (r - mid.astype(F32)).astype(BF16)
    return hi, mid, lo


def _silu(x):
    return x * jax.nn.sigmoid(x)


def _softplus_parts(x):
    t = jnp.log1p(jnp.exp(-jnp.abs(x)))
    return jnp.maximum(x, 0.0) + t, jnp.minimum(x, 0.0) - t


def _rms_scale(x):
    return lax.rsqrt(jnp.mean(x * x, axis=-1, keepdims=True) + EPS)


TOK_TILE_ROWS = SUBLANES


def _store_token_tiles(ref, x):
    n = x.shape[0]
    for j in range(TOK_TILE_ROWS):
        ref[pl.ds(j, n, stride=TOK_TILE_ROWS), :] = x[:, j * LANES:(j + 1) * LANES]


def _causal_conv(x, tail, w_ref, b_ref):
    n_tail = tail.shape[0]
    row = lax.broadcasted_iota(jnp.int32, (n_tail, 1), 0)
    acc = w_ref[CONV_W - 1:CONV_W, :] * x + b_ref[...]
    for k in range(1, CONV_W):
        rolled = pltpu.roll(x, k, axis=0)
        head = jnp.where(row < k, pltpu.roll(tail, k, axis=0), rolled[0:n_tail])
        shifted = jnp.concatenate([head, rolled[n_tail:]], axis=0)
        acc = acc + w_ref[CONV_W - 1 - k:CONV_W - k, :] * shifted
    return acc


def _load_token_tiles(ref, n):
    return jnp.concatenate(
        [ref[pl.ds(j, n, stride=TOK_TILE_ROWS), :] for j in range(TOK_TILE_ROWS)], axis=1)


PROMPT_ROWS = 2


def _prompt_rows_kernel(xmeta_ref, xp_ref, nmix_ref, wcat_ref, bsm_ref, alog_ref,
                        cwa_ref, cba_ref, cwb_ref, cbb_ref, wq_ref, wk_ref, wv_ref,
                        na_ref, nb_ref, dsk_ref, wout_ref,
                        hmid_hbm, c_ref, n_ref, m_ref, conva_ref, s_ref, convb_ref,
                        xa_buf, xbc_buf, y_buf, merged, hout, sem, *, seq, n_prompt_rows):
    p = pl.program_id(0)
    c = pl.program_id(1)
    last_p = pl.num_programs(0) - 1
    last_c = pl.num_programs(1) - 1
    T = CHUNK
    RB = xp_ref.shape[0]
    d_a = H_A * DH_A
    d_b = H_B * HD_B
    conv_b = d_b + 2 * G_B * N_STATE

    def out_copy(r, row0):
        return pltpu.make_async_copy(hout.at[r], hmid_hbm.at[pl.ds(row0, T), :], sem.at[r])

    @pl.when(c == 0)
    def _init():
        c_ref[...] = jnp.zeros_like(c_ref)
        n_ref[...] = jnp.zeros_like(n_ref)
        m_ref[...] = jnp.zeros_like(m_ref)
        s_ref[...] = jnp.zeros_like(s_ref)
        xa_buf[...] = jnp.zeros_like(xa_buf)
        xbc_buf[...] = jnp.zeros_like(xbc_buf)

    @pl.when(jnp.logical_and(p == 0, c == 0))
    def _clear_sample_rows():
        hout[0] = jnp.zeros((T, hout.shape[2]), F32)
        cp = out_copy(0, n_prompt_rows)
        cp.start()
        cp.wait()

    row = lax.broadcasted_iota(jnp.int32, (T, 1), 0)
    valid = jnp.logical_or(c > 0, row >= T - N_META)
    xs_in = [jnp.where(c == 0, xmeta_ref[...], xp_ref[r]) for r in range(RB)]
    x2 = jnp.concatenate(xs_in, axis=0)
    hn = (x2 * _rms_scale(x2) * nmix_ref[...]).astype(BF16)

    lane = lax.broadcasted_iota(jnp.int32, (1, LANES), 1)
    lane_f = lane < L_DTA
    lane_dta = jnp.logical_and(lane >= L_DTA, lane < L_I)
    lane_i = jnp.logical_and(lane >= L_I, lane < L_DT)
    lane_dt = jnp.logical_and(lane >= L_DT, lane < L_DT + H_B)
    ri = lax.broadcasted_iota(jnp.int32, (T, T), 0)
    ci = lax.broadcasted_iota(jnp.int32, (T, T), 1)
    causal = ri >= ci
    tri = jnp.where(causal, 1.0, 0.0).astype(BF16)
    a_neg = jnp.where(lane_dta, -jnp.exp(alog_ref[...]), 0.0)
    left = lane < HD_B
    top = lax.broadcasted_iota(jnp.int32, (LANES, 1), 0) < HD_B

    off_small = 2 * d_a + d_b + conv_b
    pre2 = _dot(hn, wcat_ref[:, off_small:]) + bsm_ref[...]
    xa2 = _dot(hn, wcat_ref[:, 0:d_a])

    gcols, grows, xcs, xabs = [], [], [], []

    def gate_tables():
        for r in range(RB):
            pre = pre2[r * T:(r + 1) * T]
            sp, lsig = _softplus_parts(pre)
            to_cum = jnp.where(lane_f, lsig, jnp.where(lane_dta, sp * a_neg, 0.0))
            to_cum = jnp.where(valid, to_cum, 0.0)
            hi, mid, lo = _split3(to_cum)
            cum = _dot(tri, hi) + _dot(tri, mid) + _dot(tri, lo)
            extra = jnp.where(lane_i, jnp.where(valid, pre, NEG_INF),
                              jnp.where(lane_dt, jnp.where(valid, sp, 0.0), 0.0))
            gcol = cum + extra
            gcols.append(gcol)
            grows.append(gcol.T)

    gate_tables()
    for r in range(RB):
        rs = slice(r * T, (r + 1) * T)
        xa = xa2[rs]
        xc = _causal_conv(xa, xa_buf[r], cwa_ref, cba_ref)
        xa_buf[r] = xa[T - CONV_HDR:T, :]
        conva_ref[r] = xa[T - 3:T, :]
        xcs.append(_silu(xc).astype(BF16))
        xabs.append(xa.astype(BF16))

    items = [(r, h) for h in range(H_A) for r in range(RB)]
    hsl = lambda h: slice(h * DH_A, (h + 1) * DH_A)
    m_alls = [m_ref[r] for r in range(RB)]
    m_news = list(m_alls)
    qs, ks, vs, qks, st, dd = {}, {}, {}, {}, {}, {}

    def stage_qkv(it):
        r, h = it
        qs[it] = _dot(xcs[r][:, hsl(h)], wq_ref[h]).astype(BF16)
        ks[it] = _dot(xcs[r][:, hsl(h)], wk_ref[h]) * (DH_A ** -0.5)
        vs[it] = _dot(xabs[r][:, hsl(h)], wv_ref[h]).astype(BF16)

    def stage_qk(it):
        qks[it] = _dot_nt(qs[it], ks[it].astype(BF16))

    pairs_per_group = H_B // G_B // 2
    groups = [(r, g) for g in range(G_B) for r in range(RB)]
    proj, xbcs, bgs, cgs, cbs = {}, [], {}, {}, {}

    def project(name, lo, hi):
        proj[name] = _dot(hn, wcat_ref[:, lo:hi])

    def ssd_inputs():
        for r in range(RB):
            xbc = proj["xbc"][r * T:(r + 1) * T]
            xbc_c = _causal_conv(xbc, xbc_buf[r], cwb_ref, cbb_ref)
            xbc_buf[r] = xbc[T - CONV_HDR:T, :]
            convb_ref[r] = xbc[T - 3:T, :]
            xbcs.append(_silu(xbc_c))
        for r, g in groups:
            bgs[(r, g)] = xbcs[r][:, d_b + g * N_STATE:d_b + (g + 1) * N_STATE].astype(BF16)
            cgs[(r, g)] = xbcs[r][:, d_b + (G_B + g) * N_STATE:d_b + (G_B + g + 1) * N_STATE].astype(BF16)
            cbs[(r, g)] = _dot_nt(cgs[(r, g)], bgs[(r, g)])

    def stage_weights(it):
        r, h = it
        gcol, grow = gcols[r], grows[r]
        b_col = gcol[:, L_F + h:L_F + h + 1]
        i_col = gcol[:, L_I + h:L_I + h + 1]
        b_row = grow[L_F + h:L_F + h + 1, :]
        i_row = grow[L_I + h:L_I + h + 1, :]
        m0 = m_alls[r][:, h:h + 1]
        dmat = jnp.where(causal, b_col - (b_row - i_row), NEG_INF)
        m_inter = b_col + m0
        m = jnp.maximum(m_inter, jnp.max(dmat, axis=-1, keepdims=True))
        w_inter = jnp.exp(m_inter - m)
        s = qks[it] * jnp.exp(dmat - m)
        n0 = n_ref[r, h:h + 1, :]
        den = (jnp.sum(s, axis=-1, keepdims=True)
               + w_inter * jnp.sum(qs[it].astype(F32) * n0, axis=-1, keepdims=True))
        m_last = m[T - 1:T, :]
        b_last = b_col[T - 1:T, :]
        dec = jnp.exp(b_last + m0 - m_last)
        kw = ks[it] * jnp.exp(b_last - b_col + i_col - m_last)
        n_ref[r, h:h + 1, :] = dec * n0 + jnp.sum(kw, axis=0, keepdims=True)
        m_news[r] = jnp.where(lane == h, m_last, m_news[r])
        st[it] = (s.astype(BF16), kw.astype(BF16), w_inter,
                  jnp.maximum(jnp.abs(den), jnp.exp(-m)), dec)

    def stage_readout(it):
        r, h = it
        s_b, kw_b, w_inter, den, dec = st[it]
        c0 = c_ref[r, h]
        num = _dot(s_b, vs[it]) + w_inter * _dot(qs[it], c0.astype(BF16))
        c_ref[r, h] = dec * c0 + _dot_tn(kw_b, vs[it])
        dd[it] = num / den

    def stage_head_out(it):
        r, h = it
        hh = dd[it]
        hh = hh * _rms_scale(hh) * na_ref[:, hsl(h)]
        merged[r * T:(r + 1) * T, hsl(h)] = (
            hh * jax.nn.sigmoid(proj["za"][r * T:(r + 1) * T, hsl(h)])).astype(BF16)

    pairs = [(r, pi) for pi in range(H_B // 2) for r in range(RB)]
    psl = lambda pi: slice(pi * LANES, (pi + 1) * LANES)
    sw = {}

    def stage_decay(pr):
        r, pi = pr
        g = pi // pairs_per_group
        gcol, grow = gcols[r], grows[r]
        xpair = xbcs[r][:, psl(pi)]
        scs, a_cols, w_cols, a_lasts = [], [], [], []
        for j in (2 * pi, 2 * pi + 1):
            a_col = gcol[:, L_DTA + j:L_DTA + j + 1]
            a_row = grow[L_DTA + j:L_DTA + j + 1, :]
            dt_col = gcol[:, L_DT + j:L_DT + j + 1]
            dt_row = grow[L_DT + j:L_DT + j + 1, :]
            decay = jnp.exp(jnp.where(causal, a_col - a_row, NEG_INF))
            scs.append((cbs[(r, g)] * decay * dt_row).astype(BF16))
            a_last = a_col[T - 1:T, :]
            a_cols.append(a_col)
            a_lasts.append(a_last)
            w_cols.append(jnp.exp(a_last - a_col) * dt_col)
        sw[pr] = (scs, xpair.astype(BF16),
                  (xpair * jnp.where(left, w_cols[0], w_cols[1])).astype(BF16),
                  jnp.exp(jnp.where(left, a_cols[0], a_cols[1])),
                  jnp.exp(jnp.where(top, a_lasts[0], a_lasts[1])))
    def stage_pair_out(pr):
        r, pi = pr
        g = pi // pairs_per_group
        scs, xpb, xw, ea, ea_last = sw[pr]
        s0 = s_ref[r, pi]
        y = jnp.where(left, _dot(scs[0], xpb), _dot(scs[1], xpb))
        y = y + ea * _dot_nt(cgs[(r, g)], s0.astype(BF16))
        s_ref[r, pi] = ea_last * s0 + _dot_tn(xw, bgs[(r, g)])
        y = y + dsk_ref[:, psl(pi)] * xbcs[r][:, psl(pi)]
        y_buf[r, :, psl(pi)] = y * _silu(proj["zb"][r * T:(r + 1) * T, psl(pi)])

    def each(stage, seq):
        for e in seq:
            stage(e)

    each(stage_qkv, items)
    project("xbc", 2 * d_a + d_b, off_small)
    each(stage_qk, items)
    project("za", d_a, 2 * d_a)
    ssd_inputs()
    each(stage_weights, items)
    each(stage_readout, items)
    project("zb", 2 * d_a, 2 * d_a + d_b)
    for r in range(RB):
        m_ref[r] = m_news[r]
    each(stage_head_out, items)
    each(stage_decay, pairs)
    each(stage_pair_out, pairs)
    gw = d_b // G_B
    for r in range(RB):
        for g in range(G_B):
            yg = y_buf[r, :, g * gw:(g + 1) * gw]
            merged[r * T:(r + 1) * T, d_a + g * gw:d_a + (g + 1) * gw] = (
                yg * _rms_scale(yg) * nb_ref[:, g * gw:(g + 1) * gw]).astype(BF16)

    @pl.when(c > 0)
    def _out():
        out2 = x2 + _dot(merged[...], wout_ref[...])

        @pl.when(jnp.logical_or(c > 1, p > 0))
        def _wait_previous():
            for r in range(RB):
                out_copy(r, 0).wait()

        for r in range(RB):
            hout[r] = out2[r * T:(r + 1) * T]
            out_copy(r, (p * RB + r) * seq + (c - 1) * T).start()

        @pl.when(jnp.logical_and(p == last_p, c == last_c))
        def _drain():
            for r in range(RB):
                out_copy(r, 0).wait()


def _const_spec(shape):
    nd = len(shape)
    return pl.BlockSpec(shape, lambda b, c, _nd=nd: (0,) * _nd)


def _prompt_mixer(x_prompt, xmeta, p, n_extra_rows):
    bsz, seq, d = x_prompt.shape
    assert n_extra_rows == CHUNK and seq % CHUNK == 0
    n_chunks = seq // CHUNK + 1
    cps = seq // CHUNK
    d_a = H_A * DH_A
    conv_b = H_B * HD_B + 2 * G_B * N_STATE
    consts = [p["nmix"], p["wcat"], p["bsm"], p["alog"], p["cwa"], p["cba"], p["cwb"], p["cbb"],
              p["wq"], p["wk"], p["wv"], p["na"], p["nb"], p["dsk"], p["wout"]]
    rb = PROMPT_ROWS
    assert bsz % rb == 0
    in_specs = [_const_spec(xmeta.shape),
                pl.BlockSpec((rb, CHUNK, d), lambda b, c: (b, jnp.maximum(c - 1, 0), 0))]
    in_specs += [_const_spec(a.shape) for a in consts]
    out_shape = (
        jax.ShapeDtypeStruct((bsz * seq + n_extra_rows, d), F32),
        jax.ShapeDtypeStruct((bsz, H_A, DH_A, DH_A), F32),
        jax.ShapeDtypeStruct((bsz, H_A, DH_A), F32),
        jax.ShapeDtypeStruct((bsz, 1, LANES), F32),
        jax.ShapeDtypeStruct((bsz, CONV_W - 1, d_a), F32),
        jax.ShapeDtypeStruct((bsz, H_B // 2, 2 * HD_B, N_STATE), F32),
        jax.ShapeDtypeStruct((bsz, CONV_W - 1, conv_b), F32),
    )
    out_specs = (
        pl.BlockSpec(memory_space=pl.ANY),
        pl.BlockSpec((rb, H_A, DH_A, DH_A), lambda b, c: (b, 0, 0, 0)),
        pl.BlockSpec((rb, H_A, DH_A), lambda b, c: (b, 0, 0)),
        pl.BlockSpec((rb, 1, LANES), lambda b, c: (b, 0, 0)),
        pl.BlockSpec((rb, CONV_W - 1, d_a), lambda b, c: (b, 0, 0)),
        pl.BlockSpec((rb, H_B // 2, 2 * HD_B, N_STATE), lambda b, c: (b, 0, 0, 0)),
        pl.BlockSpec((rb, CONV_W - 1, conv_b), lambda b, c: (b, 0, 0)),
    )
    return pl.pallas_call(
        functools.partial(_prompt_rows_kernel, seq=seq, n_prompt_rows=bsz * seq),
        out_shape=out_shape,
        grid=(bsz // rb, n_chunks),
        in_specs=in_specs,
        out_specs=out_specs,
        scratch_shapes=[
            pltpu.VMEM((rb, CONV_HDR, d_a), F32),
            pltpu.VMEM((rb, CONV_HDR, conv_b), F32),
            pltpu.VMEM((rb, CHUNK, H_B * HD_B), F32),
            pltpu.VMEM((rb * CHUNK, d_a + H_B * HD_B), BF16),
            pltpu.VMEM((rb, CHUNK, d), F32),
            pltpu.SemaphoreType.DMA((rb,)),
        ],
        compiler_params=pltpu.CompilerParams(
            dimension_semantics=("arbitrary", "arbitrary"), vmem_limit_bytes=VMEM_LIMIT),
        name="prompt_mixer",
    )(xmeta, x_prompt, *consts)


def _regroup_w_in_kernel(w_ref, o_ref):
    d_a = H_A * DH_A
    d_b = H_B * HD_B
    conv_b = d_b + 2 * G_B * N_STATE
    o_i = 2 * d_a
    o_f = o_i + H_A
    o_zb = o_f + H_A
    o_xbc = o_zb + d_b
    o_dt = o_xbc + conv_b
    rows = w_ref.shape[0]
    o_ref[:, 0:2 * d_a] = w_ref[:, 0:2 * d_a].astype(BF16)
    o_ref[:, 2 * d_a:2 * d_a + d_b] = w_ref[:, o_zb:o_zb + d_b].astype(BF16)
    o_ref[:, 2 * d_a + d_b:2 * d_a + d_b + conv_b] = w_ref[:, o_xbc:o_xbc + conv_b].astype(BF16)
    small = jnp.concatenate(
        [w_ref[:, o_f:o_f + H_A], w_ref[:, o_dt:o_dt + H_B], w_ref[:, o_i:o_i + H_A],
         w_ref[:, o_dt:o_dt + H_B], jnp.zeros((rows, LANES - (L_DT + H_B)), F32)], axis=1)
    o_ref[:, 2 * d_a + d_b + conv_b:] = small.astype(BF16)


def _prep_mixer_params(norm_mix, w_in, conv_a_w, conv_a_b, w_q, w_k, w_v, b_i, b_f, norm_a,
                       conv_b_w, conv_b_b, dt_bias, a_log, d_skip, norm_b, w_out):
    d_a = H_A * DH_A
    d_b = H_B * HD_B
    conv_b = d_b + 2 * G_B * N_STATE
    d_model, d_in = w_in.shape[1], w_in.shape[2]
    n_cols = 2 * d_a + d_b + conv_b + LANES
    rows = 256
    assert w_in.shape[0] == 1 and d_model % rows == 0
    wcat = pl.pallas_call(
        _regroup_w_in_kernel,
        out_shape=jax.ShapeDtypeStruct((d_model, n_cols), BF16),
        grid=(d_model // rows,),
        in_specs=[pl.BlockSpec((None, rows, d_in), lambda i: (0, i, 0))],
        out_specs=pl.BlockSpec((rows, n_cols), lambda i: (i, 0)),
        compiler_params=pltpu.CompilerParams(
            dimension_semantics=("arbitrary",), vmem_limit_bytes=VMEM_LIMIT),
        name="regroup_w_in",
    )(w_in.astype(F32))

    def lanes(parts):
        pieces, at = [], 0
        for off, a in parts:
            pieces += [jnp.zeros((1, off - at), F32), a.astype(F32)]
            at = off + a.shape[1]
        return jnp.concatenate(pieces + [jnp.zeros((1, LANES - at), F32)], axis=1)

    return dict(
        nmix=norm_mix.reshape(1, -1).astype(F32),
        wcat=wcat,
        bsm=lanes([(L_F, b_f), (L_DTA, dt_bias), (L_I, b_i), (L_DT, dt_bias)]),
        alog=lanes([(L_DTA, a_log)]),
        cwa=conv_a_w.reshape(CONV_W, d_a).astype(F32), cba=conv_a_b.reshape(1, d_a).astype(F32),
        cwb=conv_b_w.reshape(CONV_W, conv_b).astype(F32), cbb=conv_b_b.reshape(1, conv_b).astype(F32),
        wq=w_q.reshape(H_A, DH_A, DH_A).astype(BF16), wk=w_k.reshape(H_A, DH_A, DH_A).astype(BF16),
        wv=w_v.reshape(H_A, DH_A, DH_A).astype(BF16),
        na=norm_a.reshape(1, d_a).astype(F32), nb=norm_b.reshape(1, d_b).astype(F32),
        dsk=jnp.repeat(d_skip.reshape(H_B).astype(F32), HD_B)[None, :],
        wout=w_out.reshape(d_a + d_b, -1).astype(BF16),
    )


SAMPLE_BLOCK = 8


def _expand_lanes(vals, first_lane, n_heads, width):
    r = lax.broadcasted_iota(jnp.int32, (LANES, n_heads * width), 0) - first_lane
    c = lax.broadcasted_iota(jnp.int32, (LANES, n_heads * width), 1)
    sel = jnp.logical_and(c >= r * width, c < (r + 1) * width)
    e = jnp.where(sel, 1.0, 0.0).astype(BF16)
    hi, mid, lo = _split3(vals)
    return (_dot(hi, e) + _dot(mid, e)) + _dot(lo, e)


def _sample_pre_kernel(x_ref, nmix_ref, wcat_ref, bsm_ref, alog_ref, cwa_ref, cba_ref, cwb_ref, cbb_ref,
                       wq_ref, wk_ref, wv_ref, dsk_ref, conva_ref, convb_ref, n0_ref, m0_ref,
                       conva_out, convb_out, n1_out, m1_out, g_out, qt_out, kwt_out, xwt_out,
                       v_out, bc_out, a1_out, w1_out, den_out, y1_out, ea_out, zbs_out, zas_out):
    d_a = H_A * DH_A
    d_b = H_B * HD_B
    conv_b = d_b + 2 * G_B * N_STATE
    shift_i = LANES - (L_I - L_F)
    x = x_ref[...]
    hn = (x * _rms_scale(x) * nmix_ref[...]).astype(BF16)
    lane = lax.broadcasted_iota(jnp.int32, (1, LANES), 1)
    lane_f = lane < L_DTA
    lane_dta = jnp.logical_and(lane >= L_DTA, lane < L_I)
    pre = _dot(hn, wcat_ref[:, 2 * d_a + d_b + conv_b:]) + bsm_ref[...]
    sp, lsig = _softplus_parts(pre)
    a_neg = jnp.where(lane_dta, -jnp.exp(alog_ref[...]), 0.0)
    pre_al = pltpu.roll(pre, shift_i, axis=1)
    sp_al = pltpu.roll(sp, shift_i, axis=1)
    m_inter = lsig + m0_ref[...]
    m = jnp.maximum(m_inter, pre_al)
    w_inter = jnp.exp(m_inter - m)
    sfac = jnp.exp(pre_al - m)
    ea = jnp.exp(sp * a_neg)
    dt = sp_al

    xa = _dot(hn, wcat_ref[:, 0:d_a])
    xc = (cwa_ref[0:1, :] * conva_ref[:, 0:d_a] + cwa_ref[1:2, :] * conva_ref[:, d_a:2 * d_a]
          + cwa_ref[2:3, :] * conva_ref[:, 2 * d_a:3 * d_a] + cwa_ref[3:4, :] * xa + cba_ref[...])
    conva_out[:, 0:2 * d_a] = conva_ref[:, d_a:3 * d_a]
    conva_out[:, 2 * d_a:3 * d_a] = xa
    xc = _silu(xc).astype(BF16)
    xab = xa.astype(BF16)
    sf_e = _expand_lanes(sfac, L_F, H_A, DH_A)
    w_e = _expand_lanes(w_inter, L_F, H_A, DH_A)
    qk8 = jnp.zeros((x.shape[0], LANES), F32)
    qn8 = jnp.zeros((x.shape[0], LANES), F32)
    for h in range(H_A):
        sl = slice(h * DH_A, (h + 1) * DH_A)
        q = _dot(xc[:, sl], wq_ref[h])
        k = _dot(xc[:, sl], wk_ref[h]) * (DH_A ** -0.5)
        v = _dot(xab[:, sl], wv_ref[h])
        kw = k * sf_e[:, sl]
        qk8 = jnp.where(lane == h, jnp.sum(q * k, axis=-1, keepdims=True), qk8)
        qn8 = jnp.where(lane == h, jnp.sum(q * n0_ref[:, sl], axis=-1, keepdims=True), qn8)
        n1_out[:, sl] = w_e[:, sl] * n0_ref[:, sl] + kw
        v_out[:, sl] = v
        qt_out[h] = q.T
        kwt_out[h] = kw.T
    s8 = qk8 * sfac
    a1_out[...] = _expand_lanes(s8, L_F, H_A, DH_A) * v_out[...]
    w1_out[...] = w_e
    den_out[...] = jnp.maximum(jnp.abs(_expand_lanes(s8 + w_inter * qn8, L_F, H_A, DH_A)),
                               jnp.exp(-_expand_lanes(m, L_F, H_A, DH_A)))
    m1_out[...] = m
    g_out[...] = jnp.where(lane_f, w_inter, jnp.where(lane_dta, ea, 0.0))
    zas_out[...] = jax.nn.sigmoid(_dot(hn, wcat_ref[:, d_a:2 * d_a]))

    off_xbc = 2 * d_a + d_b
    xbc = _dot(hn, wcat_ref[:, off_xbc:off_xbc + conv_b])
    xbc_c = (cwb_ref[0:1, :] * convb_ref[:, 0:conv_b] + cwb_ref[1:2, :] * convb_ref[:, conv_b:2 * conv_b]
             + cwb_ref[2:3, :] * convb_ref[:, 2 * conv_b:3 * conv_b] + cwb_ref[3:4, :] * xbc + cbb_ref[...])
    convb_out[:, 0:2 * conv_b] = convb_ref[:, conv_b:3 * conv_b]
    convb_out[:, 2 * conv_b:3 * conv_b] = xbc
    xbc_c = _silu(xbc_c)
    xs = xbc_c[:, 0:d_b]
    bc = xbc_c[:, d_b:conv_b]
    bc_out[...] = bc
    heads_per_group = H_B // G_B
    cbl = jnp.zeros((x.shape[0], LANES), F32)
    for g in range(G_B):
        cb_g = jnp.sum(bc[:, g * N_STATE:(g + 1) * N_STATE]
                       * bc[:, (G_B + g) * N_STATE:(G_B + g + 1) * N_STATE], axis=-1, keepdims=True)
        in_g = jnp.logical_and(lane >= L_DTA + g * heads_per_group,
                               lane < L_DTA + (g + 1) * heads_per_group)
        cbl = jnp.where(in_g, cb_g, cbl)
    dt_e = _expand_lanes(dt, L_DTA, H_B, HD_B)
    y1_out[...] = _expand_lanes(cbl * dt, L_DTA, H_B, HD_B) * xs + dsk_ref[...] * xs
    ea_out[...] = _expand_lanes(ea, L_DTA, H_B, HD_B)
    zbs_out[...] = _silu(_dot(hn, wcat_ref[:, 2 * d_a:2 * d_a + d_b]))
    xw = xs * dt_e
    for pi in range(H_B // 2):
        xwt_out[pi] = xw[:, pi * LANES:(pi + 1) * LANES].T


def _sample_state_kernel(g_ref, c0_ref, s0_ref, qt_ref, kwt_ref, xwt_ref, v_ref, bc_ref,
                         c1_ref, s1_ref, qc_ref, ysi_ref):
    i = pl.program_id(0)
    bb = c0_ref.shape[0]
    shift = lax.rem(LANES - lax.rem(i * bb, LANES), LANES)
    lane = lax.broadcasted_iota(jnp.int32, (1, LANES), 1)
    top = lax.broadcasted_iota(jnp.int32, (LANES, 1), 0) < HD_B
    heads_per_group = H_B // G_B
    for h in range(H_A):
        sl = slice(h * DH_A, (h + 1) * DH_A)
        qt = pltpu.roll(qt_ref[h], shift, axis=1)
        kwt = pltpu.roll(kwt_ref[h], shift, axis=1)
        for r in range(bb):
            b = i * bb + r
            c0 = c0_ref[r, h]
            dec = g_ref[b, L_F + h]
            v_row = v_ref[r:r + 1, sl]
            qc_ref[r:r + 1, sl] = jnp.sum(c0 * qt[:, r:r + 1], axis=0, keepdims=True)
            c1_ref[r, h] = dec * c0 + kwt[:, r:r + 1] * v_row
    for pi in range(H_B // 2):
        g = (2 * pi) // heads_per_group
        sl = slice(pi * LANES, (pi + 1) * LANES)
        xwt = pltpu.roll(xwt_ref[pi], shift, axis=1)
        acc = jnp.zeros((LANES, LANES), F32)
        for r in range(bb):
            b = i * bb + r
            s0 = s0_ref[r, pi]
            b_row = bc_ref[r:r + 1, g * N_STATE:(g + 1) * N_STATE]
            c_row = bc_ref[r:r + 1, (G_B + g) * N_STATE:(G_B + g + 1) * N_STATE]
            col = jnp.sum(s0 * c_row, axis=-1, keepdims=True)
            acc = jnp.where(lane == r, col, acc)
            ea_rows = jnp.where(top, g_ref[b, L_DTA + 2 * pi], g_ref[b, L_DTA + 2 * pi + 1])
            s1_ref[r, pi] = ea_rows * s0 + xwt[:, r:r + 1] * b_row
        ysi_ref[:, sl] = acc.T[0:bb, :]


def _sample_post_kernel(a1_ref, w1_ref, den_ref, y1_ref, ea_ref, zbs_ref, zas_ref, x_ref, qc_ref, ysi_ref,
                        na_ref, nb_ref, wout_ref, hall_ref, hmid_ref, merged):
    del hall_ref
    d_a = H_A * DH_A
    d_b = H_B * HD_B
    hh = (a1_ref[...] + w1_ref[...] * qc_ref[...]) / den_ref[...]
    for h in range(H_A):
        sl = slice(h * DH_A, (h + 1) * DH_A)
        hs = hh[:, sl]
        merged[:, sl] = (hs * _rms_scale(hs) * na_ref[:, sl] * zas_ref[:, sl]).astype(BF16)
    y = (y1_ref[...] + ea_ref[...] * ysi_ref[...]) * zbs_ref[...]
    gw = d_b // G_B
    for g in range(G_B):
        yg = y[:, g * gw:(g + 1) * gw]
        merged[:, d_a + g * gw:d_a + (g + 1) * gw] = (
            yg * _rms_scale(yg) * nb_ref[:, g * gw:(g + 1) * gw]).astype(BF16)
    hmid_ref[...] = x_ref[...] + _dot(merged[...], wout_ref[...])


def _vmem_specs(arrays):
    return [pl.BlockSpec(a.shape, lambda *_, _nd=a.ndim: (0,) * _nd) for a in arrays]


def _sample_mixer(x, c0, n0, m0, conva, s0, convb, p, hmid_all, row_offset):
    nb, d = x.shape
    d_a = H_A * DH_A
    d_b = H_B * HD_B
    conv_b = d_b + 2 * G_B * N_STATE
    row = lambda w: jax.ShapeDtypeStruct((nb, w), F32)
    tile = lambda k: jax.ShapeDtypeStruct((k, LANES, nb), F32)
    pre_in = [x, p["nmix"], p["wcat"], p["bsm"], p["alog"], p["cwa"], p["cba"], p["cwb"], p["cbb"],
              p["wq"], p["wk"], p["wv"], p["dsk"], conva, convb, n0, m0]
    pre_out_shape = (row(3 * d_a), row(3 * conv_b), row(d_a), row(LANES), row(LANES),
                     tile(H_A), tile(H_A), tile(H_B // 2), row(d_a), row(2 * G_B * N_STATE),
                     row(d_a), row(d_a), row(d_a), row(d_b), row(d_b), row(d_b), row(d_a))
    (conva1, convb1, n1, m1, g8, qt, kwt, xwt, v, bc, a1, w1, den, y1, ea_e, zbs, zas) = pl.pallas_call(
        _sample_pre_kernel,
        out_shape=pre_out_shape,
        grid=(1,),
        in_specs=_vmem_specs(pre_in),
        out_specs=tuple(pl.BlockSpec(s.shape, lambda i, _nd=len(s.shape): (0,) * _nd) for s in pre_out_shape),
        compiler_params=pltpu.CompilerParams(
            dimension_semantics=("arbitrary",), vmem_limit_bytes=VMEM_LIMIT),
        name="sample_pre",
    )(*pre_in)

    bb = SAMPLE_BLOCK
    const3 = lambda k: pl.BlockSpec((k, LANES, nb), lambda i, g: (0, 0, 0))
    state_grid = pltpu.PrefetchScalarGridSpec(
        num_scalar_prefetch=1,
        grid=(nb // bb,),
        in_specs=[pl.BlockSpec((bb, H_A, DH_A, DH_A), lambda i, g: (i, 0, 0, 0)),
                  pl.BlockSpec((bb, H_B // 2, 2 * HD_B, N_STATE), lambda i, g: (i, 0, 0, 0)),
                  const3(H_A), const3(H_A), const3(H_B // 2),
                  pl.BlockSpec((bb, d_a), lambda i, g: (i, 0)),
                  pl.BlockSpec((bb, 2 * G_B * N_STATE), lambda i, g: (i, 0))],
        out_specs=(pl.BlockSpec((bb, H_A, DH_A, DH_A), lambda i, g: (i, 0, 0, 0)),
                   pl.BlockSpec((bb, H_B // 2, 2 * HD_B, N_STATE), lambda i, g: (i, 0, 0, 0)),
                   pl.BlockSpec((bb, d_a), lambda i, g: (i, 0)),
                   pl.BlockSpec((bb, d_b), lambda i, g: (i, 0))),
    )
    c1, s1, qc, ysi = pl.pallas_call(
        _sample_state_kernel,
        out_shape=(jax.ShapeDtypeStruct(c0.shape, F32), jax.ShapeDtypeStruct(s0.shape, F32),
                   row(d_a), row(d_b)),
        grid_spec=state_grid,
        compiler_params=pltpu.CompilerParams(
            dimension_semantics=("arbitrary",), vmem_limit_bytes=VMEM_LIMIT),
        name="sample_state",
    )(g8, c0, s0, qt, kwt, xwt, v, bc)

    post_in = [a1, w1, den, y1, ea_e, zbs, zas, x, qc, ysi, p["na"], p["nb"], p["wout"]]
    hmid_all = pl.pallas_call(
        _sample_post_kernel,
        out_shape=jax.ShapeDtypeStruct(hmid_all.shape, F32),
        grid=(1,),
        in_specs=_vmem_specs(post_in) + [pl.BlockSpec(memory_space=pl.ANY)],
        out_specs=pl.BlockSpec((nb, d), lambda i: (row_offset // nb, 0)),
        scratch_shapes=[pltpu.VMEM((nb, d_a + d_b), BF16)],
        input_output_aliases={len(post_in): 0},
        compiler_params=pltpu.CompilerParams(
            dimension_semantics=("arbitrary",), vmem_limit_bytes=VMEM_LIMIT),
        name="sample_post",
    )(*post_in, hmid_all)
    return hmid_all, c1, n1, m1, conva1, s1, convb1


R_EA, R_EB, R_RA, R_RB, R_GA, R_GB = 0, 1, 2, 3, 4, 5
RL_E = N_EGROUPS


def _router_kernel(h_ref, nf_ref, whi_ref, wmid_ref, br_ref, xn_ref, info_ref, cnt_ref, carry):
    i = pl.program_id(0)
    tr = h_ref.shape[0]

    @pl.when(i == 0)
    def _init():
        carry[...] = jnp.zeros_like(carry)

    h = h_ref[...]
    xn = h * _rms_scale(h) * nf_ref[...]
    _store_token_tiles(xn_ref, xn)
    x_hi, x_mid, _ = _split3(xn)
    logits = (_dot(x_hi, whi_ref[...]) + _dot(x_hi, wmid_ref[...]) + _dot(x_mid, whi_ref[...])
              + br_ref[...])
    lane_i = lax.broadcasted_iota(jnp.int32, (1, LANES), 1)
    lane = lane_i.astype(F32)
    big = float(LANES)

    def first_lane_of(cond):
        return jnp.min(jnp.where(cond, lane, big), axis=-1, keepdims=True)

    l1 = jnp.where(lane_i < N_EGROUPS, logits, NEG_INF)
    e1 = jnp.exp(l1 - jnp.max(l1, axis=-1, keepdims=True))
    p1 = e1 / jnp.sum(e1, axis=-1, keepdims=True)
    gp = jnp.max(p1, axis=-1, keepdims=True)
    gidx = first_lane_of(p1 == gp)
    lo = RL_E + N_EPG * gidx
    l2 = jnp.where(jnp.logical_and(lane >= lo, lane < lo + N_EPG), logits, NEG_INF)
    va = jnp.max(l2, axis=-1, keepdims=True)
    ia = first_lane_of(l2 == va)
    l2b = jnp.where(lane == ia, NEG_INF, l2)
    vb = jnp.max(l2b, axis=-1, keepdims=True)
    ib = first_lane_of(l2b == vb)
    eb = jnp.exp(vb - va)
    wa = 1.0 / (1.0 + eb)
    wb = eb / (1.0 + eb)

    is_a = lane == ia
    is_b = lane == ib
    onehot = jnp.where(jnp.logical_or(is_a, is_b), 1.0, 0.0)
    ri = lax.broadcasted_iota(jnp.int32, (tr, tr), 0)
    ci = lax.broadcasted_iota(jnp.int32, (tr, tr), 1)
    tri = jnp.where(ri >= ci, 1.0, 0.0).astype(BF16)
    incl = _dot(tri, onehot.astype(BF16))
    excl = incl - onehot + carry[...]
    rank_a = jnp.sum(jnp.where(is_a, excl, 0.0), axis=-1, keepdims=True)
    rank_b = jnp.sum(jnp.where(is_b, excl, 0.0), axis=-1, keepdims=True)
    carry[...] = carry[...] + incl[tr - 1:tr, :]
    cnt_ref[...] = carry[...]

    info = jnp.where(lane_i == R_EA, ia - RL_E, 0.0)
    info = jnp.where(lane_i == R_EB, ib - RL_E, info)
    info = jnp.where(lane_i == R_RA, rank_a, info)
    info = jnp.where(lane_i == R_RB, rank_b, info)
    info = jnp.where(lane_i == R_GA, gp * wa, info)
    info = jnp.where(lane_i == R_GB, gp * wb, info)
    info_ref[...] = info


def _row_tile(n, candidates):
    for t in candidates:
        if n % t == 0:
            return t
    raise ValueError(f"no row tile for {n} rows among {candidates}")


def _router(hmid, rp):
    n, d = hmid.shape
    assert d == TOK_TILE_ROWS * LANES
    tr = _row_tile(n, (512, 384, 256, 128))
    return pl.pallas_call(
        _router_kernel,
        out_shape=(jax.ShapeDtypeStruct((n * TOK_TILE_ROWS, LANES), F32),
                   jax.ShapeDtypeStruct((n, LANES), F32),
                   jax.ShapeDtypeStruct((1, LANES), F32)),
        grid=(n // tr,),
        in_specs=[pl.BlockSpec((tr, d), lambda i: (i, 0)),
                  pl.BlockSpec((1, d), lambda i: (0, 0)),
                  pl.BlockSpec((d, LANES), lambda i: (0, 0)),
                  pl.BlockSpec((d, LANES), lambda i: (0, 0)),
                  pl.BlockSpec((1, LANES), lambda i: (0, 0))],
        out_specs=(pl.BlockSpec((tr * TOK_TILE_ROWS, LANES), lambda i: (i, 0)),
                   pl.BlockSpec((tr, LANES), lambda i: (i, 0)),
                   pl.BlockSpec((1, LANES), lambda i: (0, 0))),
        scratch_shapes=[pltpu.VMEM((1, LANES), F32)],
        compiler_params=pltpu.CompilerParams(
            dimension_semantics=("arbitrary",), vmem_limit_bytes=VMEM_LIMIT),
        name="router",
    )(hmid, rp["nf"], rp["whi"], rp["wmid"], rp["br"])


def _prep_router_params(norm_ffn, w_r1, b_r1, w_r2, b_r2):
    d = w_r1.shape[1]
    w = jnp.concatenate([w_r1.reshape(d, N_EGROUPS).astype(F32), w_r2.reshape(d, N_EXPERTS).astype(F32),
                         jnp.zeros((d, LANES - RL_E - N_EXPERTS), F32)], axis=1)
    whi = w.astype(BF16)
    wmid = (w - whi.astype(F32)).astype(BF16)
    br = jnp.concatenate([b_r1.reshape(1, N_EGROUPS).astype(F32), b_r2.reshape(1, N_EXPERTS).astype(F32),
                          jnp.zeros((1, LANES - RL_E - N_EXPERTS), F32)], axis=1)
    return dict(nf=norm_ffn.reshape(1, d).astype(F32), whi=whi, wmid=wmid, br=br)


FFN_TM = 512
N_GATHER_SLOTS = 3


def _start_tile_gather(first_row_of, n_rows, src_hbm, dst, sem, priority_of):
    for r in range(n_rows):
        start = pl.multiple_of(first_row_of(r), TOK_TILE_ROWS)
        pltpu.make_async_copy(src_hbm.at[pl.ds(start, TOK_TILE_ROWS), :],
                              dst.at[pl.ds(r * TOK_TILE_ROWS, TOK_TILE_ROWS), :],
                              sem).start(priority=priority_of(r))


def _wait_tile_gather(n_rows, src_hbm, dst, sem):
    pltpu.make_async_copy(src_hbm.at[pl.ds(0, n_rows * TOK_TILE_ROWS), :], dst, sem).wait()


TAB_OFF, TAB_CNT, TAB_TILE_EXPERT, TAB_NVALID = 0, 1, 2, 3
TAB_LANES = 2 * LANES


def _ffn_kernel(tab_ref, pa_ref, pb_ref, xn_hbm, wg_ref, wu_ref, wd_ref, ys_ref,
                src, xbuf, wbf, sem, *, n_tokens, tm):
    i = pl.program_id(0)
    n_valid = tab_ref[TAB_NVALID, 0]
    slot = lax.rem(i, N_GATHER_SLOTS)
    gather_priority = lambda r: 1

    @pl.when(i == 0)
    def _build_source_rows():
        for e in range(N_EXPERTS):
            cnt_e = tab_ref[TAB_CNT, RL_E + e]
            first = tab_ref[TAB_OFF, RL_E + e] + cnt_e
            n_pad = lax.rem(tm - lax.rem(cnt_e, tm), tm)

            def pad_body(r, carry, first=first):
                src[first + r] = 0
                return carry
            lax.fori_loop(0, n_pad, pad_body, 0)

        def body(t, carry):
            first_row = t * TOK_TILE_ROWS
            src[pa_ref[t]] = first_row
            src[pb_ref[t]] = first_row
            return carry
        lax.fori_loop(0, n_tokens, body, 0, unroll=8)
        _start_tile_gather(lambda r: src[r], tm, xn_hbm, xbuf.at[0], sem.at[0], gather_priority)
        second = jnp.where(n_valid > 1, tm, 0)
        _start_tile_gather(lambda r: src[second + r], tm, xn_hbm, xbuf.at[1], sem.at[1], gather_priority)

    changed = jnp.logical_or(i == 0, tab_ref[TAB_TILE_EXPERT, i]
                             != tab_ref[TAB_TILE_EXPERT, jnp.maximum(i - 1, 0)])

    @pl.when(jnp.logical_and(changed, i < n_valid))
    def _cast_weights():
        wbf[0] = wg_ref[...].astype(BF16)
        wbf[1] = wu_ref[...].astype(BF16)
        wbf[2] = wd_ref[...].astype(BF16)

    @pl.when(i < n_valid)
    def _compute():
        _wait_tile_gather(tm, xn_hbm, xbuf.at[slot], sem.at[slot])
        x = _load_token_tiles(xbuf.at[slot], tm).astype(BF16)
        hg = _dot(x, wbf[0])
        hu = _dot(x, wbf[1])
        y = _dot((_silu(hg) * hu).astype(BF16), wbf[2])
        base = jnp.where(i + 2 < n_valid, i + 2, 0) * tm
        ahead = lax.rem(i + 2, N_GATHER_SLOTS)
        _start_tile_gather(lambda r: src[base + r], tm, xn_hbm, xbuf.at[ahead], sem.at[ahead],
                           gather_priority)
        _store_token_tiles(ys_ref, y)

    @pl.when(i == n_valid - 1)
    def _drain():
        for k in (1, 2):
            s = lax.rem(i + k, N_GATHER_SLOTS)
            _wait_tile_gather(tm, xn_hbm, xbuf.at[s], sem.at[s])

    @pl.when(i >= n_valid)
    def _pad():
        ys_ref[...] = jnp.zeros_like(ys_ref)


def _expert_ffn(xn_tiles, tab, pos_a, pos_b, n_tiles, wg, wu, wd):
    n = pos_a.shape[0]
    tm = FFN_TM
    d, dff = wg.shape[1], wg.shape[2]
    rows = tm * TOK_TILE_ROWS
    idx = lambda i, tab, pa, pb: (tab[TAB_TILE_EXPERT, i], 0, 0)
    grid_spec = pltpu.PrefetchScalarGridSpec(
        num_scalar_prefetch=3,
        grid=(n_tiles,),
        in_specs=[pl.BlockSpec(memory_space=pl.ANY),
                  pl.BlockSpec((None, d, dff), idx),
                  pl.BlockSpec((None, d, dff), idx),
                  pl.BlockSpec((None, dff, d), idx)],
        out_specs=pl.BlockSpec((rows, LANES), lambda i, tab, pa, pb: (i, 0)),
        scratch_shapes=[pltpu.SMEM((n_tiles * tm,), jnp.int32),
                        pltpu.VMEM((N_GATHER_SLOTS, rows, LANES), F32),
                        pltpu.VMEM((3, d, dff), BF16),
                        pltpu.SemaphoreType.DMA((N_GATHER_SLOTS,))],
    )
    return pl.pallas_call(
        functools.partial(_ffn_kernel, n_tokens=n, tm=tm),
        out_shape=jax.ShapeDtypeStruct((n_tiles * rows, LANES), F32),
        grid_spec=grid_spec,
        compiler_params=pltpu.CompilerParams(
            dimension_semantics=("arbitrary",), vmem_limit_bytes=VMEM_LIMIT),
        name="expert_ffn",
    )(tab, pos_a, pos_b, xn_tiles, wg, wu, wd)


def _positions_kernel(info_ref, cnt_ref, pos_ref, tab_ref, *, tm, chunk):
    lane_i = lax.broadcasted_iota(jnp.int32, (1, LANES), 1)
    lane = lane_i.astype(F32)
    is_expert = jnp.logical_and(lane_i >= RL_E, lane_i < RL_E + N_EXPERTS)
    cnt = jnp.where(is_expert, cnt_ref[...], 0.0)
    padded = jnp.floor((cnt + (tm - 1)) / tm) * tm
    ri = lax.broadcasted_iota(jnp.int32, (LANES, LANES), 0)
    ci = lax.broadcasted_iota(jnp.int32, (LANES, LANES), 1)
    before = jnp.where(ri < ci, 1.0, 0.0).astype(BF16)
    hi, mid, lo = _split3(jnp.broadcast_to(padded, (SUBLANES, LANES)))
    off = ((_dot(hi, before) + _dot(mid, before)) + _dot(lo, before))[0:1, :]
    pick = jnp.where(lax.broadcasted_iota(jnp.int32, (SUBLANES, LANES), 0) == lane_i, 1.0, 0.0).astype(BF16)

    total = jnp.sum(padded, axis=-1, keepdims=True)
    tile_row = lax.broadcasted_iota(jnp.int32, (TAB_LANES, 1), 0).astype(F32) * tm
    ends = off + padded
    done = jnp.logical_and(is_expert, ends <= jnp.minimum(tile_row, total - 1.0))
    te_col = jnp.sum(jnp.where(done, 1.0, 0.0), axis=-1, keepdims=True)
    te_rows = _dot_nt(pick, jnp.where(lane_i == 0, te_col, 0.0).astype(BF16))
    tab_ref[...] = jnp.zeros_like(tab_ref)
    tab_ref[TAB_OFF:TAB_OFF + 1, 0:LANES] = off.astype(jnp.int32)
    tab_ref[TAB_CNT:TAB_CNT + 1, 0:LANES] = cnt.astype(jnp.int32)
    tab_ref[TAB_TILE_EXPERT:TAB_TILE_EXPERT + 1, :] = te_rows[0:1, :].astype(jnp.int32)
    tab_ref[TAB_NVALID:TAB_NVALID + 1, 0:LANES] = jnp.broadcast_to(total / tm, (1, LANES)).astype(jnp.int32)

    n = info_ref.shape[0]
    for c0 in range(0, n, chunk):
        blk = info_ref[c0:c0 + chunk, :]
        lane_a = blk[:, R_EA:R_EA + 1] + RL_E
        lane_b = blk[:, R_EB:R_EB + 1] + RL_E
        pos_a = blk[:, R_RA:R_RA + 1] + jnp.sum(jnp.where(lane == lane_a, off, 0.0), axis=-1, keepdims=True)
        pos_b = blk[:, R_RB:R_RB + 1] + jnp.sum(jnp.where(lane == lane_b, off, 0.0), axis=-1, keepdims=True)
        z_hi, z_mid, z_lo = _split3(jnp.where(lane_i == 0, pos_a, jnp.where(lane_i == 1, pos_b, 0.0)))
        rows = (_dot_nt(pick, z_hi) + _dot_nt(pick, z_mid)) + _dot_nt(pick, z_lo)
        pos_ref[:, c0:c0 + chunk] = rows.astype(jnp.int32)


def _routing_tables(info, counts, n_tiles, tm):
    n = info.shape[0]
    groups = n // LANES
    assert n_tiles <= TAB_LANES
    chunk = LANES * max(g for g in range(1, 65) if groups % g == 0)
    pos, tab = pl.pallas_call(
        functools.partial(_positions_kernel, tm=tm, chunk=chunk),
        out_shape=(jax.ShapeDtypeStruct((SUBLANES, n), jnp.int32),
                   jax.ShapeDtypeStruct((SUBLANES, TAB_LANES), jnp.int32)),
        grid=(1,),
        in_specs=[pl.BlockSpec((n, LANES), lambda i: (0, 0)), pl.BlockSpec((1, LANES), lambda i: (0, 0))],
        out_specs=(pl.BlockSpec((SUBLANES, n), lambda i: (0, 0)),
                   pl.BlockSpec((SUBLANES, TAB_LANES), lambda i: (0, 0))),
        compiler_params=pltpu.CompilerParams(
            dimension_semantics=("arbitrary",), vmem_limit_bytes=VMEM_LIMIT),
        name="positions",
    )(info, counts)
    return tab, pos[0], pos[1]


def _combine_kernel(pa_ref, pb_ref, h_ref, info_ref, ys_hbm, nfin_ref, yp_ref, ysm_ref,
                    buf_a, buf_b, sem, *, n_prompt_tiles):
    i = pl.program_id(0)
    n_steps = pl.num_programs(0)
    tt = h_ref.shape[0]
    slot = lax.rem(i, N_GATHER_SLOTS)

    def start(tile, s):
        base = tile * tt
        _start_tile_gather(lambda r: pa_ref[base + r] * TOK_TILE_ROWS, tt, ys_hbm, buf_a.at[s],
                           sem.at[s], lambda r: 0)
        _start_tile_gather(lambda r: pb_ref[base + r] * TOK_TILE_ROWS, tt, ys_hbm, buf_b.at[s],
                           sem.at[s], lambda r: 1)

    @pl.when(i == 0)
    def _first():
        start(0, 0)
        start(lax.rem(1, n_steps), 1)

    _wait_tile_gather(tt, ys_hbm, buf_a.at[slot], sem.at[slot])
    _wait_tile_gather(tt, ys_hbm, buf_b.at[slot], sem.at[slot])
    info = info_ref[...]
    rows_a = _load_token_tiles(buf_a.at[slot], tt)
    rows_b = _load_token_tiles(buf_b.at[slot], tt)
    x = h_ref[...]
    start(lax.rem(i + 2, n_steps), lax.rem(i + 2, N_GATHER_SLOTS))
    h = x + info[:, R_GA:R_GA + 1] * rows_a + info[:, R_GB:R_GB + 1] * rows_b
    y = h * _rms_scale(h) * nfin_ref[...]

    @pl.when(i < n_prompt_tiles)
    def _prompt():
        yp_ref[...] = y

    @pl.when(i >= n_prompt_tiles)
    def _sample():
        ysm_ref[...] = y

    @pl.when(i == n_steps - 1)
    def _drain():
        for k in (1, 2):
            s = lax.rem(i + k, N_GATHER_SLOTS)
            _wait_tile_gather(tt, ys_hbm, buf_a.at[s], sem.at[s])
            _wait_tile_gather(tt, ys_hbm, buf_b.at[s], sem.at[s])


def _combine(hmid, info, ys, pos_a, pos_b, nfin, n_prompt):
    n, d = hmid.shape
    tt = CHUNK
    n_prompt_tiles = n_prompt // tt
    n_sample = n - n_prompt
    grid_spec = pltpu.PrefetchScalarGridSpec(
        num_scalar_prefetch=2,
        grid=(n // tt,),
        in_specs=[pl.BlockSpec((tt, d), lambda i, pa, pb: (i, 0)),
                  pl.BlockSpec((tt, LANES), lambda i, pa, pb: (i, 0)),
                  pl.BlockSpec(memory_space=pl.ANY),
                  pl.BlockSpec((1, d), lambda i, pa, pb: (0, 0))],
        out_specs=(pl.BlockSpec((tt, d), lambda i, pa, pb: (jnp.minimum(i, n_prompt_tiles - 1), 0)),
                   pl.BlockSpec((tt, d), lambda i, pa, pb: (jnp.maximum(i - n_prompt_tiles, 0), 0))),
        scratch_shapes=[pltpu.VMEM((N_GATHER_SLOTS, tt * TOK_TILE_ROWS, LANES), F32),
                        pltpu.VMEM((N_GATHER_SLOTS, tt * TOK_TILE_ROWS, LANES), F32),
                        pltpu.SemaphoreType.DMA((N_GATHER_SLOTS,))],
    )
    return pl.pallas_call(
        functools.partial(_combine_kernel, n_prompt_tiles=n_prompt_tiles),
        out_shape=(jax.ShapeDtypeStruct((n_prompt, d), F32),
                   jax.ShapeDtypeStruct((n_sample, d), F32)),
        grid_spec=grid_spec,
        compiler_params=pltpu.CompilerParams(
            dimension_semantics=("arbitrary",), vmem_limit_bytes=VMEM_LIMIT),
        name="combine",
    )(pos_a, pos_b, hmid, info, ys, nfin)


def _moe_and_final_norm(hmid, n_prompt, rp, wg, wu, wd, nfin):
    n = hmid.shape[0]
    tm = FFN_TM
    n_tiles = (2 * n + N_EXPERTS * (tm - 1)) // tm
    xn, info, counts = _router(hmid, rp)
    tab, pos_a, pos_b = _routing_tables(info, counts, n_tiles, tm)
    ys = _expert_ffn(xn, tab, pos_a, pos_b, n_tiles, wg, wu, wd)
    return _combine(hmid, info, ys, pos_a, pos_b, nfin, n_prompt)


def kernel(x_prompt, x_sample, state_mlstm_C, state_mlstm_n, state_mlstm_m, state_mlstm_conv, state_ssm, state_ssm_conv, meta_tokens, norm_mix, w_in, conv_a_w, conv_a_b, w_q, w_k, w_v, b_i, b_f, norm_a, conv_b_w, conv_b_b, dt_bias, a_log, d_skip, norm_b, w_out, norm_ffn, w_r1, b_r1, w_r2, b_r2, w_gate, w_up, w_down, norm_final):
    bsz, seq, d = x_prompt.shape
    nb = x_sample.shape[0]
    d_a = H_A * DH_A
    conv_b = H_B * HD_B + 2 * G_B * N_STATE
    assert w_in.shape[0] == 1 and x_sample.shape[1] == 1 and seq % CHUNK == 0 and nb == CHUNK
    mp = _prep_mixer_params(norm_mix, w_in, conv_a_w, conv_a_b, w_q, w_k, w_v, b_i, b_f, norm_a,
                            conv_b_w, conv_b_b, dt_bias, a_log, d_skip, norm_b, w_out)
    rp = _prep_router_params(norm_ffn, w_r1, b_r1, w_r2, b_r2)
    xmeta = jnp.concatenate([jnp.zeros((CHUNK - N_META, d), F32), meta_tokens.astype(F32)], 0)

    hmid, p_c, p_n, p_m, p_ca, p_s, p_cb = _prompt_mixer(x_prompt.astype(F32), xmeta, mp, nb)
    m0 = jnp.pad(state_mlstm_m.reshape(nb, H_A).astype(F32), ((0, 0), (0, LANES - H_A)))
    hmid, s_c, s_n, s_m, s_ca, s_s, s_cb = _sample_mixer(
        x_sample.reshape(nb, d).astype(F32),
        state_mlstm_C.reshape(nb, H_A, DH_A, DH_A).astype(F32),
        state_mlstm_n.reshape(nb, d_a).astype(F32),
        m0,
        state_mlstm_conv.reshape(nb, (CONV_W - 1) * d_a).astype(F32),
        state_ssm.reshape(nb, H_B // 2, 2 * HD_B, N_STATE).astype(F32),
        state_ssm_conv.reshape(nb, (CONV_W - 1) * conv_b).astype(F32),
        mp, hmid, bsz * seq)

    wshape = w_gate.shape[1:]
    y_p, y_s = _moe_and_final_norm(
        hmid, bsz * seq, rp, w_gate.reshape(wshape).astype(F32), w_up.reshape(wshape).astype(F32),
        w_down.reshape(w_down.shape[1:]).astype(F32), norm_final.reshape(1, d).astype(F32))

    return (y_p.reshape(bsz, seq, d), y_s.reshape(nb, 1, d),
            p_c.reshape(1, bsz, H_A, DH_A, DH_A), p_n.reshape(1, bsz, H_A, DH_A),
            p_m[:, 0, :H_A].reshape(1, bsz, H_A), p_ca.reshape(1, bsz, CONV_W - 1, d_a),
            p_s.reshape(1, bsz, H_B, HD_B, N_STATE), p_cb.reshape(1, bsz, CONV_W - 1, conv_b),
            s_c.reshape(1, nb, H_A, DH_A, DH_A), s_n.reshape(1, nb, H_A, DH_A),
            s_m[:, :H_A].reshape(1, nb, H_A), s_ca.reshape(1, nb, CONV_W - 1, d_a),
            s_s.reshape(1, nb, H_B, HD_B, N_STATE), s_cb.reshape(1, nb, CONV_W - 1, conv_b))
```

```python
import functools

import jax
import jax.numpy as jnp
from jax import lax
from jax.experimental import pallas as pl
from jax.experimental.pallas import tpu as pltpu

F32 = jnp.float32
BF16 = jnp.bfloat16

EPS = 1e-6
N_META = 16
CONV_W = 4
CHUNK = 128
H_A = 8
DH_A = 128
H_B = 16
HD_B = 64
N_STATE = 128
G_B = 2
N_EGROUPS = 4
N_EPG = 4
N_EXPERTS = 16
LANES = 128
SUBLANES = 8
CONV_HDR = SUBLANES
VMEM_LIMIT = 56 * 1024 * 1024

L_F = 0
L_DTA = 8
L_I = 24
L_DT = 32

NEG_INF = float("-inf")


def _dot(a, b):
    return jnp.dot(a, b, preferred_element_type=F32)


def _dot_nt(a, b):
    return lax.dot_general(a, b, (((1,), (1,)), ((), ())), preferred_element_type=F32)


def _dot_tn(a, b):
    return lax.dot_general(a, b, (((0,), (0,)), ((), ())), preferred_element_type=F32)


def _split3(x):
    hi = x.astype(BF16)
    r = x - hi.astype(F32)
    mid = r.astype(BF16)
    lo = (r - mid.astype(F32)).astype(BF16)
    return hi, mid, lo


def _silu(x):
    return x * jax.nn.sigmoid(x)


def _softplus_parts(x):
    t = jnp.log1p(jnp.exp(-jnp.abs(x)))
    return jnp.maximum(x, 0.0) + t, jnp.minimum(x, 0.0) - t


def _rms_scale(x):
    return lax.rsqrt(jnp.mean(x * x, axis=-1, keepdims=True) + EPS)


TOK_TILE_ROWS = SUBLANES


def _store_token_tiles(ref, x):
    n = x.shape[0]
    for j in range(TOK_TILE_ROWS):
        ref[pl.ds(j, n, stride=TOK_TILE_ROWS), :] = x[:, j * LANES:(j + 1) * LANES]


def _causal_conv(x, tail, w_ref, b_ref):
    n_tail = tail.shape[0]
    row = lax.broadcasted_iota(jnp.int32, (n_tail, 1), 0)
    acc = w_ref[CONV_W - 1:CONV_W, :] * x + b_ref[...]
    for k in range(1, CONV_W):
        rolled = pltpu.roll(x, k, axis=0)
        head = jnp.where(row < k, pltpu.roll(tail, k, axis=0), rolled[0:n_tail])
        shifted = jnp.concatenate([head, rolled[n_tail:]], axis=0)
        acc = acc + w_ref[CONV_W - 1 - k:CONV_W - k, :] * shifted
    return acc


def _load_token_tiles(ref, n):
    return jnp.concatenate(
        [ref[pl.ds(j, n, stride=TOK_TILE_ROWS), :] for j in range(TOK_TILE_ROWS)], axis=1)


PROMPT_ROWS = 2


def _prompt_rows_kernel(xmeta_ref, xp_ref, nmix_ref, wcat_ref, bsm_ref, alog_ref,
                        cwa_ref, cba_ref, cwb_ref, cbb_ref, wq_ref, wk_ref, wv_ref,
                        na_ref, nb_ref, dsk_ref, wout_ref,
                        hmid_hbm, c_ref, n_ref, m_ref, conva_ref, s_ref, convb_ref,
                        xa_buf, xbc_buf, y_buf, merged, hout, sem, *, seq, n_prompt_rows):
    p = pl.program_id(0)
    c = pl.program_id(1)
    last_p = pl.num_programs(0) - 1
    last_c = pl.num_programs(1) - 1
    T = CHUNK
    RB = xp_ref.shape[0]
    d_a = H_A * DH_A
    d_b = H_B * HD_B
    conv_b = d_b + 2 * G_B * N_STATE

    def out_copy(r, row0):
        return pltpu.make_async_copy(hout.at[r], hmid_hbm.at[pl.ds(row0, T), :], sem.at[r])

    @pl.when(c == 0)
    def _init():
        c_ref[...] = jnp.zeros_like(c_ref)
        n_ref[...] = jnp.zeros_like(n_ref)
        m_ref[...] = jnp.zeros_like(m_ref)
        s_ref[...] = jnp.zeros_like(s_ref)
        xa_buf[...] = jnp.zeros_like(xa_buf)
        xbc_buf[...] = jnp.zeros_like(xbc_buf)

    @pl.when(jnp.logical_and(p == 0, c == 0))
    def _clear_sample_rows():
        hout[0] = jnp.zeros((T, hout.shape[2]), F32)
        cp = out_copy(0, n_prompt_rows)
        cp.start()
        cp.wait()

    row = lax.broadcasted_iota(jnp.int32, (T, 1), 0)
    valid = jnp.logical_or(c > 0, row >= T - N_META)
    xs_in = [jnp.where(c == 0, xmeta_ref[...], xp_ref[r]) for r in range(RB)]
    x2 = jnp.concatenate(xs_in, axis=0)
    hn = (x2 * _rms_scale(x2) * nmix_ref[...]).astype(BF16)

    lane = lax.broadcasted_iota(jnp.int32, (1, LANES), 1)
    lane_f = lane < L_DTA
    lane_dta = jnp.logical_and(lane >= L_DTA, lane < L_I)
    lane_i = jnp.logical_and(lane >= L_I, lane < L_DT)
    lane_dt = jnp.logical_and(lane >= L_DT, lane < L_DT + H_B)
    ri = lax.broadcasted_iota(jnp.int32, (T, T), 0)
    ci = lax.broadcasted_iota(jnp.int32, (T, T), 1)
    causal = ri >= ci
    tri = jnp.where(causal, 1.0, 0.0).astype(BF16)
    a_neg = jnp.where(lane_dta, -jnp.exp(alog_ref[...]), 0.0)
    left = lane < HD_B
    top = lax.broadcasted_iota(jnp.int32, (LANES, 1), 0) < HD_B

    off_small = 2 * d_a + d_b + conv_b
    pre2 = _dot(hn, wcat_ref[:, off_small:]) + bsm_ref[...]
    xa2 = _dot(hn, wcat_ref[:, 0:d_a])

    gcols, grows, xcs, xabs = [], [], [], []

    def gate_tables():
        for r in range(RB):
            pre = pre2[r * T:(r + 1) * T]
            sp, lsig = _softplus_parts(pre)
            to_cum = jnp.where(lane_f, lsig, jnp.where(lane_dta, sp * a_neg, 0.0))
            to_cum = jnp.where(valid, to_cum, 0.0)
            hi, mid, lo = _split3(to_cum)
            cum = _dot(tri, hi) + _dot(tri, mid) + _dot(tri, lo)
            extra = jnp.where(lane_i, jnp.where(valid, pre, NEG_INF),
                              jnp.where(lane_dt, jnp.where(valid, sp, 0.0), 0.0))
            gcol = cum + extra
            gcols.append(gcol)
            grows.append(gcol.T)

    gate_tables()
    for r in range(RB):
        rs = slice(r * T, (r + 1) * T)
        xa = xa2[rs]
        xc = _causal_conv(xa, xa_buf[r], cwa_ref, cba_ref)
        xa_buf[r] = xa[T - CONV_HDR:T, :]
        conva_ref[r] = xa[T - 3:T, :]
        xcs.append(_silu(xc).astype(BF16))
        xabs.append(xa.astype(BF16))

    items = [(r, h) for h in range(H_A) for r in range(RB)]
    hsl = lambda h: slice(h * DH_A, (h + 1) * DH_A)
    m_alls = [m_ref[r] for r in range(RB)]
    m_news = list(m_alls)
    qs, ks, vs, qks, st, dd = {}, {}, {}, {}, {}, {}

    def stage_qkv(it):
        r, h = it
        qs[it] = _dot(xcs[r][:, hsl(h)], wq_ref[h]).astype(BF16)
        ks[it] = _dot(xcs[r][:, hsl(h)], wk_ref[h]) * (DH_A ** -0.5)
        vs[it] = _dot(xabs[r][:, hsl(h)], wv_ref[h]).astype(BF16)

    def stage_qk(it):
        qks[it] = _dot_nt(qs[it], ks[it].astype(BF16))

    pairs_per_group = H_B // G_B // 2
    groups = [(r, g) for g in range(G_B) for r in range(RB)]
    proj, xbcs, bgs, cgs, cbs = {}, [], {}, {}, {}

    def project(name, lo, hi):
        proj[name] = _dot(hn, wcat_ref[:, lo:hi])

    def ssd_inputs():
        for r in range(RB):
            xbc = proj["xbc"][r * T:(r + 1) * T]
            xbc_c = _causal_conv(xbc, xbc_buf[r], cwb_ref, cbb_ref)
            xbc_buf[r] = xbc[T - CONV_HDR:T, :]
            convb_ref[r] = xbc[T - 3:T, :]
            xbcs.append(_silu(xbc_c))
        for r, g in groups:
            bgs[(r, g)] = xbcs[r][:, d_b + g * N_STATE:d_b + (g + 1) * N_STATE].astype(BF16)
            cgs[(r, g)] = xbcs[r][:, d_b + (G_B + g) * N_STATE:d_b + (G_B + g + 1) * N_STATE].astype(BF16)
            cbs[(r, g)] = _dot_nt(cgs[(r, g)], bgs[(r, g)])

    def stage_weights(it):
        r, h = it
        gcol, grow = gcols[r], grows[r]
        b_col = gcol[:, L_F + h:L_F + h + 1]
        i_col = gcol[:, L_I + h:L_I + h + 1]
        b_row = grow[L_F + h:L_F + h + 1, :]
        i_row = grow[L_I + h:L_I + h + 1, :]
        m0 = m_alls[r][:, h:h + 1]
        dmat = jnp.where(causal, b_col - (b_row - i_row), NEG_INF)
        m_inter = b_col + m0
        m = jnp.maximum(m_inter, jnp.max(dmat, axis=-1, keepdims=True))
        w_inter = jnp.exp(m_inter - m)
        s = qks[it] * jnp.exp(dmat - m)
        n0 = n_ref[r, h:h + 1, :]
        den = (jnp.sum(s, axis=-1, keepdims=True)
               + w_inter * jnp.sum(qs[it].astype(F32) * n0, axis=-1, keepdims=True))
        m_last = m[T - 1:T, :]
        b_last = b_col[T - 1:T, :]
        dec = jnp.exp(b_last + m0 - m_last)
        kw = ks[it] * jnp.exp(b_last - b_col + i_col - m_last)
        n_ref[r, h:h + 1, :] = dec * n0 + jnp.sum(kw, axis=0, keepdims=True)
        m_news[r] = jnp.where(lane == h, m_last, m_news[r])
        st[it] = (s.astype(BF16), kw.astype(BF16), w_inter,
                  jnp.maximum(jnp.abs(den), jnp.exp(-m)), dec)

    def stage_readout(it):
        r, h = it
        s_b, kw_b, w_inter, den, dec = st[it]
        c0 = c_ref[r, h]
        num = _dot(s_b, vs[it]) + w_inter * _dot(qs[it], c0.astype(BF16))
        c_ref[r, h] = dec * c0 + _dot_tn(kw_b, vs[it])
        dd[it] = num / den

    def stage_head_out(it):
        r, h = it
        hh = dd[it]
        hh = hh * _rms_scale(hh) * na_ref[:, hsl(h)]
        merged[r * T:(r + 1) * T, hsl(h)] = (
            hh * jax.nn.sigmoid(proj["za"][r * T:(r + 1) * T, hsl(h)])).astype(BF16)

    pairs = [(r, pi) for pi in range(H_B // 2) for r in range(RB)]
    psl = lambda pi: slice(pi * LANES, (pi + 1) * LANES)
    sw = {}

    def stage_decay(pr):
        r, pi = pr
        g = pi // pairs_per_group
        gcol, grow = gcols[r], grows[r]
        xpair = xbcs[r][:, psl(pi)]
        scs, a_cols, w_cols, a_lasts = [], [], [], []
        for j in (2 * pi, 2 * pi + 1):
            a_col = gcol[:, L_DTA + j:L_DTA + j + 1]
            a_row = grow[L_DTA + j:L_DTA + j + 1, :]
            dt_col = gcol[:, L_DT + j:L_DT + j + 1]
            dt_row = grow[L_DT + j:L_DT + j + 1, :]
            decay = jnp.exp(jnp.where(causal, a_col - a_row, NEG_INF))
            scs.append((cbs[(r, g)] * decay * dt_row).astype(BF16))
            a_last = a_col[T - 1:T, :]
            a_cols.append(a_col)
            a_lasts.append(a_last)
            w_cols.append(jnp.exp(a_last - a_col) * dt_col)
        sw[pr] = (scs, xpair.astype(BF16),
                  (xpair * jnp.where(left, w_cols[0], w_cols[1])).astype(BF16),
                  jnp.exp(jnp.where(left, a_cols[0], a_cols[1])),
                  jnp.exp(jnp.where(top, a_lasts[0], a_lasts[1])))
    def stage_pair_out(pr):
        r, pi = pr
        g = pi // pairs_per_group
        scs, xpb, xw, ea, ea_last = sw[pr]
        s0 = s_ref[r, pi]
        y = jnp.where(left, _dot(scs[0], xpb), _dot(scs[1], xpb))
        y = y + ea * _dot_nt(cgs[(r, g)], s0.astype(BF16))
        s_ref[r, pi] = ea_last * s0 + _dot_tn(xw, bgs[(r, g)])
        y = y + dsk_ref[:, psl(pi)] * xbcs[r][:, psl(pi)]
        y_buf[r, :, psl(pi)] = y * _silu(proj["zb"][r * T:(r + 1) * T, psl(pi)])

    def each(stage, seq):
        for e in seq:
            stage(e)

    each(stage_qkv, items)
    project("xbc", 2 * d_a + d_b, off_small)
    each(stage_qk, items)
    project("za", d_a, 2 * d_a)
    ssd_inputs()
    each(stage_weights, items)
    each(stage_readout, items)
    project("zb", 2 * d_a, 2 * d_a + d_b)
    for r in range(RB):
        m_ref[r] = m_news[r]
    each(stage_head_out, items)
    each(stage_decay, pairs)
    each(stage_pair_out, pairs)
    gw = d_b // G_B
    for r in range(RB):
        for g in range(G_B):
            yg = y_buf[r, :, g * gw:(g + 1) * gw]
            merged[r * T:(r + 1) * T, d_a + g * gw:d_a + (g + 1) * gw] = (
                yg * _rms_scale(yg) * nb_ref[:, g * gw:(g + 1) * gw]).astype(BF16)

    @pl.when(c > 0)
    def _out():
        out2 = x2 + _dot(merged[...], wout_ref[...])

        @pl.when(jnp.logical_or(c > 1, p > 0))
        def _wait_previous():
            for r in range(RB):
                out_copy(r, 0).wait()

        for r in range(RB):
            hout[r] = out2[r * T:(r + 1) * T]
            out_copy(r, (p * RB + r) * seq + (c - 1) * T).start()

        @pl.when(jnp.logical_and(p == last_p, c == last_c))
        def _drain():
            for r in range(RB):
                out_copy(r, 0).wait()


def _const_spec(shape):
    nd = len(shape)
    return pl.BlockSpec(shape, lambda b, c, _nd=nd: (0,) * _nd)


def _prompt_mixer(x_prompt, xmeta, p, n_extra_rows):
    bsz, seq, d = x_prompt.shape
    assert n_extra_rows == CHUNK and seq % CHUNK == 0
    n_chunks = seq // CHUNK + 1
    cps = seq // CHUNK
    d_a = H_A * DH_A
    conv_b = H_B * HD_B + 2 * G_B * N_STATE
    consts = [p["nmix"], p["wcat"], p["bsm"], p["alog"], p["cwa"], p["cba"], p["cwb"], p["cbb"],
              p["wq"], p["wk"], p["wv"], p["na"], p["nb"], p["dsk"], p["wout"]]
    rb = PROMPT_ROWS
    assert bsz % rb == 0
    in_specs = [_const_spec(xmeta.shape),
                pl.BlockSpec((rb, CHUNK, d), lambda b, c: (b, jnp.maximum(c - 1, 0), 0))]
    in_specs += [_const_spec(a.shape) for a in consts]
    out_shape = (
        jax.ShapeDtypeStruct((bsz * seq + n_extra_rows, d), F32),
        jax.ShapeDtypeStruct((bsz, H_A, DH_A, DH_A), F32),
        jax.ShapeDtypeStruct((bsz, H_A, DH_A), F32),
        jax.ShapeDtypeStruct((bsz, 1, LANES), F32),
        jax.ShapeDtypeStruct((bsz, CONV_W - 1, d_a), F32),
        jax.ShapeDtypeStruct((bsz, H_B // 2, 2 * HD_B, N_STATE), F32),
        jax.ShapeDtypeStruct((bsz, CONV_W - 1, conv_b), F32),
    )
    out_specs = (
        pl.BlockSpec(memory_space=pl.ANY),
        pl.BlockSpec((rb, H_A, DH_A, DH_A), lambda b, c: (b, 0, 0, 0)),
        pl.BlockSpec((rb, H_A, DH_A), lambda b, c: (b, 0, 0)),
        pl.BlockSpec((rb, 1, LANES), lambda b, c: (b, 0, 0)),
        pl.BlockSpec((rb, CONV_W - 1, d_a), lambda b, c: (b, 0, 0)),
        pl.BlockSpec((rb, H_B // 2, 2 * HD_B, N_STATE), lambda b, c: (b, 0, 0, 0)),
        pl.BlockSpec((rb, CONV_W - 1, conv_b), lambda b, c: (b, 0, 0)),
    )
    return pl.pallas_call(
        functools.partial(_prompt_rows_kernel, seq=seq, n_prompt_rows=bsz * seq),
        out_shape=out_shape,
        grid=(bsz // rb, n_chunks),
        in_specs=in_specs,
        out_specs=out_specs,
        scratch_shapes=[
            pltpu.VMEM((rb, CONV_HDR, d_a), F32),
            pltpu.VMEM((rb, CONV_HDR, conv_b), F32),
            pltpu.VMEM((rb, CHUNK, H_B * HD_B), F32),
            pltpu.VMEM((rb * CHUNK, d_a + H_B * HD_B), BF16),
            pltpu.VMEM((rb, CHUNK, d), F32),
            pltpu.SemaphoreType.DMA((rb,)),
        ],
        compiler_params=pltpu.CompilerParams(
            dimension_semantics=("arbitrary", "arbitrary"), vmem_limit_bytes=VMEM_LIMIT),
        name="prompt_mixer",
    )(xmeta, x_prompt, *consts)


def _regroup_w_in_kernel(w_ref, o_ref):
    d_a = H_A * DH_A
    d_b = H_B * HD_B
    conv_b = d_b + 2 * G_B * N_STATE
    o_i = 2 * d_a
    o_f = o_i + H_A
    o_zb = o_f + H_A
    o_xbc = o_zb + d_b
    o_dt = o_xbc + conv_b
    rows = w_ref.shape[0]
    o_ref[:, 0:2 * d_a] = w_ref[:, 0:2 * d_a].astype(BF16)
    o_ref[:, 2 * d_a:2 * d_a + d_b] = w_ref[:, o_zb:o_zb + d_b].astype(BF16)
    o_ref[:, 2 * d_a + d_b:2 * d_a + d_b + conv_b] = w_ref[:, o_xbc:o_xbc + conv_b].astype(BF16)
    small = jnp.concatenate(
        [w_ref[:, o_f:o_f + H_A], w_ref[:, o_dt:o_dt + H_B], w_ref[:, o_i:o_i + H_A],
         w_ref[:, o_dt:o_dt + H_B], jnp.zeros((rows, LANES - (L_DT + H_B)), F32)], axis=1)
    o_ref[:, 2 * d_a + d_b + conv_b:] = small.astype(BF16)


def _prep_mixer_params(norm_mix, w_in, conv_a_w, conv_a_b, w_q, w_k, w_v, b_i, b_f, norm_a,
                       conv_b_w, conv_b_b, dt_bias, a_log, d_skip, norm_b, w_out):
    d_a = H_A * DH_A
    d_b = H_B * HD_B
    conv_b = d_b + 2 * G_B * N_STATE
    d_model, d_in = w_in.shape[1], w_in.shape[2]
    n_cols = 2 * d_a + d_b + conv_b + LANES
    rows = 256
    assert w_in.shape[0] == 1 and d_model % rows == 0
    wcat = pl.pallas_call(
        _regroup_w_in_kernel,
        out_shape=jax.ShapeDtypeStruct((d_model, n_cols), BF16),
        grid=(d_model // rows,),
        in_specs=[pl.BlockSpec((None, rows, d_in), lambda i: (0, i, 0))],
        out_specs=pl.BlockSpec((rows, n_cols), lambda i: (i, 0)),
        compiler_params=pltpu.CompilerParams(
            dimension_semantics=("arbitrary",), vmem_limit_bytes=VMEM_LIMIT),
        name="regroup_w_in",
    )(w_in.astype(F32))

    def lanes(parts):
        pieces, at = [], 0
        for off, a in parts:
            pieces += [jnp.zeros((1, off - at), F32), a.astype(F32)]
            at = off + a.shape[1]
        return jnp.concatenate(pieces + [jnp.zeros((1, LANES - at), F32)], axis=1)

    return dict(
        nmix=norm_mix.reshape(1, -1).astype(F32),
        wcat=wcat,
        bsm=lanes([(L_F, b_f), (L_DTA, dt_bias), (L_I, b_i), (L_DT, dt_bias)]),
        alog=lanes([(L_DTA, a_log)]),
        cwa=conv_a_w.reshape(CONV_W, d_a).astype(F32), cba=conv_a_b.reshape(1, d_a).astype(F32),
        cwb=conv_b_w.reshape(CONV_W, conv_b).astype(F32), cbb=conv_b_b.reshape(1, conv_b).astype(F32),
        wq=w_q.reshape(H_A, DH_A, DH_A).astype(BF16), wk=w_k.reshape(H_A, DH_A, DH_A).astype(BF16),
        wv=w_v.reshape(H_A, DH_A, DH_A).astype(BF16),
        na=norm_a.reshape(1, d_a).astype(F32), nb=norm_b.reshape(1, d_b).astype(F32),
        dsk=jnp.repeat(d_skip.reshape(H_B).astype(F32), HD_B)[None, :],
        wout=w_out.reshape(d_a + d_b, -1).astype(BF16),
    )


SAMPLE_BLOCK = 8


def _expand_lanes(vals, first_lane, n_heads, width):
    r = lax.broadcasted_iota(jnp.int32, (LANES, n_heads * width), 0) - first_lane
    c = lax.broadcasted_iota(jnp.int32, (LANES, n_heads * width), 1)
    sel = jnp.logical_and(c >= r * width, c < (r + 1) * width)
    e = jnp.where(sel, 1.0, 0.0).astype(BF16)
    hi, mid, lo = _split3(vals)
    return (_dot(hi, e) + _dot(mid, e)) + _dot(lo, e)


def _sample_pre_kernel(x_ref, nmix_ref, wcat_ref, bsm_ref, alog_ref, cwa_ref, cba_ref, cwb_ref, cbb_ref,
                       wq_ref, wk_ref, wv_ref, dsk_ref, conva_ref, convb_ref, n0_ref, m0_ref,
                       conva_out, convb_out, n1_out, m1_out, g_out, qt_out, kwt_out, xwt_out,
                       v_out, bc_out, a1_out, w1_out, den_out, y1_out, ea_out, zbs_out, zas_out):
    d_a = H_A * DH_A
    d_b = H_B * HD_B
    conv_b = d_b + 2 * G_B * N_STATE
    shift_i = LANES - (L_I - L_F)
    x = x_ref[...]
    hn = (x * _rms_scale(x) * nmix_ref[...]).astype(BF16)
    lane = lax.broadcasted_iota(jnp.int32, (1, LANES), 1)
    lane_f = lane < L_DTA
    lane_dta = jnp.logical_and(lane >= L_DTA, lane < L_I)
    pre = _dot(hn, wcat_ref[:, 2 * d_a + d_b + conv_b:]) + bsm_ref[...]
    sp, lsig = _softplus_parts(pre)
    a_neg = jnp.where(lane_dta, -jnp.exp(alog_ref[...]), 0.0)
    pre_al = pltpu.roll(pre, shift_i, axis=1)
    sp_al = pltpu.roll(sp, shift_i, axis=1)
    m_inter = lsig + m0_ref[...]
    m = jnp.maximum(m_inter, pre_al)
    w_inter = jnp.exp(m_inter - m)
    sfac = jnp.exp(pre_al - m)
    ea = jnp.exp(sp * a_neg)
    dt = sp_al

    xa = _dot(hn, wcat_ref[:, 0:d_a])
    xc = (cwa_ref[0:1, :] * conva_ref[:, 0:d_a] + cwa_ref[1:2, :] * conva_ref[:, d_a:2 * d_a]
          + cwa_ref[2:3, :] * conva_ref[:, 2 * d_a:3 * d_a] + cwa_ref[3:4, :] * xa + cba_ref[...])
    conva_out[:, 0:2 * d_a] = conva_ref[:, d_a:3 * d_a]
    conva_out[:, 2 * d_a:3 * d_a] = xa
    xc = _silu(xc).astype(BF16)
    xab = xa.astype(BF16)
    sf_e = _expand_lanes(sfac, L_F, H_A, DH_A)
    w_e = _expand_lanes(w_inter, L_F, H_A, DH_A)
    qk8 = jnp.zeros((x.shape[0], LANES), F32)
    qn8 = jnp.zeros((x.shape[0], LANES), F32)
    for h in range(H_A):
        sl = slice(h * DH_A, (h + 1) * DH_A)
        q = _dot(xc[:, sl], wq_ref[h])
        k = _dot(xc[:, sl], wk_ref[h]) * (DH_A ** -0.5)
        v = _dot(xab[:, sl], wv_ref[h])
        kw = k * sf_e[:, sl]
        qk8 = jnp.where(lane == h, jnp.sum(q * k, axis=-1, keepdims=True), qk8)
        qn8 = jnp.where(lane == h, jnp.sum(q * n0_ref[:, sl], axis=-1, keepdims=True), qn8)
        n1_out[:, sl] = w_e[:, sl] * n0_ref[:, sl] + kw
        v_out[:, sl] = v
        qt_out[h] = q.T
        kwt_out[h] = kw.T
    s8 = qk8 * sfac
    a1_out[...] = _expand_lanes(s8, L_F, H_A, DH_A) * v_out[...]
    w1_out[...] = w_e
    den_out[...] = jnp.maximum(jnp.abs(_expand_lanes(s8 + w_inter * qn8, L_F, H_A, DH_A)),
                               jnp.exp(-_expand_lanes(m, L_F, H_A, DH_A)))
    m1_out[...] = m
    g_out[...] = jnp.where(lane_f, w_inter, jnp.where(lane_dta, ea, 0.0))
    zas_out[...] = jax.nn.sigmoid(_dot(hn, wcat_ref[:, d_a:2 * d_a]))

    off_xbc = 2 * d_a + d_b
    xbc = _dot(hn, wcat_ref[:, off_xbc:off_xbc + conv_b])
    xbc_c = (cwb_ref[0:1, :] * convb_ref[:, 0:conv_b] + cwb_ref[1:2, :] * convb_ref[:, conv_b:2 * conv_b]
             + cwb_ref[2:3, :] * convb_ref[:, 2 * conv_b:3 * conv_b] + cwb_ref[3:4, :] * xbc + cbb_ref[...])
    convb_out[:, 0:2 * conv_b] = convb_ref[:, conv_b:3 * conv_b]
    convb_out[:, 2 * conv_b:3 * conv_b] = xbc
    xbc_c = _silu(xbc_c)
    xs = xbc_c[:, 0:d_b]
    bc = xbc_c[:, d_b:conv_b]
    bc_out[...] = bc
    heads_per_group = H_B // G_B
    cbl = jnp.zeros((x.shape[0], LANES), F32)
    for g in range(G_B):
        cb_g = jnp.sum(bc[:, g * N_STATE:(g + 1) * N_STATE]
                       * bc[:, (G_B + g) * N_STATE:(G_B + g + 1) * N_STATE], axis=-1, keepdims=True)
        in_g = jnp.logical_and(lane >= L_DTA + g * heads_per_group,
                               lane < L_DTA + (g + 1) * heads_per_group)
        cbl = jnp.where(in_g, cb_g, cbl)
    dt_e = _expand_lanes(dt, L_DTA, H_B, HD_B)
    y1_out[...] = _expand_lanes(cbl * dt, L_DTA, H_B, HD_B) * xs + dsk_ref[...] * xs
    ea_out[...] = _expand_lanes(ea, L_DTA, H_B, HD_B)
    zbs_out[...] = _silu(_dot(hn, wcat_ref[:, 2 * d_a:2 * d_a + d_b]))
    xw = xs * dt_e
    for pi in range(H_B // 2):
        xwt_out[pi] = xw[:, pi * LANES:(pi + 1) * LANES].T


def _sample_state_kernel(g_ref, c0_ref, s0_ref, qt_ref, kwt_ref, xwt_ref, v_ref, bc_ref,
                         c1_ref, s1_ref, qc_ref, ysi_ref):
    i = pl.program_id(0)
    bb = c0_ref.shape[0]
    shift = lax.rem(LANES - lax.rem(i * bb, LANES), LANES)
    lane = lax.broadcasted_iota(jnp.int32, (1, LANES), 1)
    top = lax.broadcasted_iota(jnp.int32, (LANES, 1), 0) < HD_B
    heads_per_group = H_B // G_B
    for h in range(H_A):
        sl = slice(h * DH_A, (h + 1) * DH_A)
        qt = pltpu.roll(qt_ref[h], shift, axis=1)
        kwt = pltpu.roll(kwt_ref[h], shift, axis=1)
        for r in range(bb):
            b = i * bb + r
            c0 = c0_ref[r, h]
            dec = g_ref[b, L_F + h]
            v_row = v_ref[r:r + 1, sl]
            qc_ref[r:r + 1, sl] = jnp.sum(c0 * qt[:, r:r + 1], axis=0, keepdims=True)
            c1_ref[r, h] = dec * c0 + kwt[:, r:r + 1] * v_row
    for pi in range(H_B // 2):
        g = (2 * pi) // heads_per_group
        sl = slice(pi * LANES, (pi + 1) * LANES)
        xwt = pltpu.roll(xwt_ref[pi], shift, axis=1)
        acc = jnp.zeros((LANES, LANES), F32)
        for r in range(bb):
            b = i * bb + r
            s0 = s0_ref[r, pi]
            b_row = bc_ref[r:r + 1, g * N_STATE:(g + 1) * N_STATE]
            c_row = bc_ref[r:r + 1, (G_B + g) * N_STATE:(G_B + g + 1) * N_STATE]
            col = jnp.sum(s0 * c_row, axis=-1, keepdims=True)
            acc = jnp.where(lane == r, col, acc)
            ea_rows = jnp.where(top, g_ref[b, L_DTA + 2 * pi], g_ref[b, L_DTA + 2 * pi + 1])
            s1_ref[r, pi] = ea_rows * s0 + xwt[:, r:r + 1] * b_row
        ysi_ref[:, sl] = acc.T[0:bb, :]


def _sample_post_kernel(a1_ref, w1_ref, den_ref, y1_ref, ea_ref, zbs_ref, zas_ref, x_ref, qc_ref, ysi_ref,
                        na_ref, nb_ref, wout_ref, hall_ref, hmid_ref, merged):
    del hall_ref
    d_a = H_A * DH_A
    d_b = H_B * HD_B
    hh = (a1_ref[...] + w1_ref[...] * qc_ref[...]) / den_ref[...]
    for h in range(H_A):
        sl = slice(h * DH_A, (h + 1) * DH_A)
        hs = hh[:, sl]
        merged[:, sl] = (hs * _rms_scale(hs) * na_ref[:, sl] * zas_ref[:, sl]).astype(BF16)
    y = (y1_ref[...] + ea_ref[...] * ysi_ref[...]) * zbs_ref[...]
    gw = d_b // G_B
    for g in range(G_B):
        yg = y[:, g * gw:(g + 1) * gw]
        merged[:, d_a + g * gw:d_a + (g + 1) * gw] = (
            yg * _rms_scale(yg) * nb_ref[:, g * gw:(g + 1) * gw]).astype(BF16)
    hmid_ref[...] = x_ref[...] + _dot(merged[...], wout_ref[...])


def _vmem_specs(arrays):
    return [pl.BlockSpec(a.shape, lambda *_, _nd=a.ndim: (0,) * _nd) for a in arrays]


def _sample_mixer(x, c0, n0, m0, conva, s0, convb, p, hmid_all, row_offset):
    nb, d = x.shape
    d_a = H_A * DH_A
    d_b = H_B * HD_B
    conv_b = d_b + 2 * G_B * N_STATE
    row = lambda w: jax.ShapeDtypeStruct((nb, w), F32)
    tile = lambda k: jax.ShapeDtypeStruct((k, LANES, nb), F32)
    pre_in = [x, p["nmix"], p["wcat"], p["bsm"], p["alog"], p["cwa"], p["cba"], p["cwb"], p["cbb"],
              p["wq"], p["wk"], p["wv"], p["dsk"], conva, convb, n0, m0]
    pre_out_shape = (row(3 * d_a), row(3 * conv_b), row(d_a), row(LANES), row(LANES),
                     tile(H_A), tile(H_A), tile(H_B // 2), row(d_a), row(2 * G_B * N_STATE),
                     row(d_a), row(d_a), row(d_a), row(d_b), row(d_b), row(d_b), row(d_a))
    (conva1, convb1, n1, m1, g8, qt, kwt, xwt, v, bc, a1, w1, den, y1, ea_e, zbs, zas) = pl.pallas_call(
        _sample_pre_kernel,
        out_shape=pre_out_shape,
        grid=(1,),
        in_specs=_vmem_specs(pre_in),
        out_specs=tuple(pl.BlockSpec(s.shape, lambda i, _nd=len(s.shape): (0,) * _nd) for s in pre_out_shape),
        compiler_params=pltpu.CompilerParams(
            dimension_semantics=("arbitrary",), vmem_limit_bytes=VMEM_LIMIT),
        name="sample_pre",
    )(*pre_in)

    bb = SAMPLE_BLOCK
    const3 = lambda k: pl.BlockSpec((k, LANES, nb), lambda i, g: (0, 0, 0))
    state_grid = pltpu.PrefetchScalarGridSpec(
        num_scalar_prefetch=1,
        grid=(nb // bb,),
        in_specs=[pl.BlockSpec((bb, H_A, DH_A, DH_A), lambda i, g: (i, 0, 0, 0)),
                  pl.BlockSpec((bb, H_B // 2, 2 * HD_B, N_STATE), lambda i, g: (i, 0, 0, 0)),
                  const3(H_A), const3(H_A), const3(H_B // 2),
                  pl.BlockSpec((bb, d_a), lambda i, g: (i, 0)),
                  pl.BlockSpec((bb, 2 * G_B * N_STATE), lambda i, g: (i, 0))],
        out_specs=(pl.BlockSpec((bb, H_A, DH_A, DH_A), lambda i, g: (i, 0, 0, 0)),
                   pl.BlockSpec((bb, H_B // 2, 2 * HD_B, N_STATE), lambda i, g: (i, 0, 0, 0)),
                   pl.BlockSpec((bb, d_a), lambda i, g: (i, 0)),
                   pl.BlockSpec((bb, d_b), lambda i, g: (i, 0))),
    )
    c1, s1, qc, ysi = pl.pallas_call(
        _sample_state_kernel,
        out_shape=(jax.ShapeDtypeStruct(c0.shape, F32), jax.ShapeDtypeStruct(s0.shape, F32),
                   row(d_a), row(d_b)),
        grid_spec=state_grid,
        compiler_params=pltpu.CompilerParams(
            dimension_semantics=("arbitrary",), vmem_limit_bytes=VMEM_LIMIT),
        name="sample_state",
    )(g8, c0, s0, qt, kwt, xwt, v, bc)

    post_in = [a1, w1, den, y1, ea_e, zbs, zas, x, qc, ysi, p["na"], p["nb"], p["wout"]]
    hmid_all = pl.pallas_call(
        _sample_post_kernel,
        out_shape=jax.ShapeDtypeStruct(hmid_all.shape, F32),
        grid=(1,),
        in_specs=_vmem_specs(post_in) + [pl.BlockSpec(memory_space=pl.ANY)],
        out_specs=pl.BlockSpec((nb, d), lambda i: (row_offset // nb, 0)),
        scratch_shapes=[pltpu.VMEM((nb, d_a + d_b), BF16)],
        input_output_aliases={len(post_in): 0},
        compiler_params=pltpu.CompilerParams(
            dimension_semantics=("arbitrary",), vmem_limit_bytes=VMEM_LIMIT),
        name="sample_post",
    )(*post_in, hmid_all)
    return hmid_all, c1, n1, m1, conva1, s1, convb1


R_EA, R_EB, R_RA, R_RB, R_GA, R_GB = 0, 1, 2, 3, 4, 5
RL_E = N_EGROUPS


def _router_kernel(h_ref, nf_ref, whi_ref, wmid_ref, br_ref, xn_ref, info_ref, cnt_ref, carry):
    i = pl.program_id(0)
    tr = h_ref.shape[0]

    @pl.when(i == 0)
    def _init():
        carry[...] = jnp.zeros_like(carry)

    h = h_ref[...]
    xn = h * _rms_scale(h) * nf_ref[...]
    _store_token_tiles(xn_ref, xn)
    x_hi, x_mid, _ = _split3(xn)
    logits = (_dot(x_hi, whi_ref[...]) + _dot(x_hi, wmid_ref[...]) + _dot(x_mid, whi_ref[...])
              + br_ref[...])
    lane_i = lax.broadcasted_iota(jnp.int32, (1, LANES), 1)
    lane = lane_i.astype(F32)
    big = float(LANES)

    def first_lane_of(cond):
        return jnp.min(jnp.where(cond, lane, big), axis=-1, keepdims=True)

    l1 = jnp.where(lane_i < N_EGROUPS, logits, NEG_INF)
    e1 = jnp.exp(l1 - jnp.max(l1, axis=-1, keepdims=True))
    p1 = e1 / jnp.sum(e1, axis=-1, keepdims=True)
    gp = jnp.max(p1, axis=-1, keepdims=True)
    gidx = first_lane_of(p1 == gp)
    lo = RL_E + N_EPG * gidx
    l2 = jnp.where(jnp.logical_and(lane >= lo, lane < lo + N_EPG), logits, NEG_INF)
    va = jnp.max(l2, axis=-1, keepdims=True)
    ia = first_lane_of(l2 == va)
    l2b = jnp.where(lane == ia, NEG_INF, l2)
    vb = jnp.max(l2b, axis=-1, keepdims=True)
    ib = first_lane_of(l2b == vb)
    eb = jnp.exp(vb - va)
    wa = 1.0 / (1.0 + eb)
    wb = eb / (1.0 + eb)

    is_a = lane == ia
    is_b = lane == ib
    onehot = jnp.where(jnp.logical_or(is_a, is_b), 1.0, 0.0)
    ri = lax.broadcasted_iota(jnp.int32, (tr, tr), 0)
    ci = lax.broadcasted_iota(jnp.int32, (tr, tr), 1)
    tri = jnp.where(ri >= ci, 1.0, 0.0).astype(BF16)
    incl = _dot(tri, onehot.astype(BF16))
    excl = incl - onehot + carry[...]
    rank_a = jnp.sum(jnp.where(is_a, excl, 0.0), axis=-1, keepdims=True)
    rank_b = jnp.sum(jnp.where(is_b, excl, 0.0), axis=-1, keepdims=True)
    carry[...] = carry[...] + incl[tr - 1:tr, :]
    cnt_ref[...] = carry[...]

    info = jnp.where(lane_i == R_EA, ia - RL_E, 0.0)
    info = jnp.where(lane_i == R_EB, ib - RL_E, info)
    info = jnp.where(lane_i == R_RA, rank_a, info)
    info = jnp.where(lane_i == R_RB, rank_b, info)
    info = jnp.where(lane_i == R_GA, gp * wa, info)
    info = jnp.where(lane_i == R_GB, gp * wb, info)
    info_ref[...] = info


def _row_tile(n, candidates):
    for t in candidates:
        if n % t == 0:
            return t
    raise ValueError(f"no row tile for {n} rows among {candidates}")


def _router(hmid, rp):
    n, d = hmid.shape
    assert d == TOK_TILE_ROWS * LANES
    tr = _row_tile(n, (512, 384, 256, 128))
    return pl.pallas_call(
        _router_kernel,
        out_shape=(jax.ShapeDtypeStruct((n * TOK_TILE_ROWS, LANES), F32),
                   jax.ShapeDtypeStruct((n, LANES), F32),
                   jax.ShapeDtypeStruct((1, LANES), F32)),
        grid=(n // tr,),
        in_specs=[pl.BlockSpec((tr, d), lambda i: (i, 0)),
                  pl.BlockSpec((1, d), lambda i: (0, 0)),
                  pl.BlockSpec((d, LANES), lambda i: (0, 0)),
                  pl.BlockSpec((d, LANES), lambda i: (0, 0)),
                  pl.BlockSpec((1, LANES), lambda i: (0, 0))],
        out_specs=(pl.BlockSpec((tr * TOK_TILE_ROWS, LANES), lambda i: (i, 0)),
                   pl.BlockSpec((tr, LANES), lambda i: (i, 0)),
                   pl.BlockSpec((1, LANES), lambda i: (0, 0))),
        scratch_shapes=[pltpu.VMEM((1, LANES), F32)],
        compiler_params=pltpu.CompilerParams(
            dimension_semantics=("arbitrary",), vmem_limit_bytes=VMEM_LIMIT),
        name="router",
    )(hmid, rp["nf"], rp["whi"], rp["wmid"], rp["br"])


def _prep_router_params(norm_ffn, w_r1, b_r1, w_r2, b_r2):
    d = w_r1.shape[1]
    w = jnp.concatenate([w_r1.reshape(d, N_EGROUPS).astype(F32), w_r2.reshape(d, N_EXPERTS).astype(F32),
                         jnp.zeros((d, LANES - RL_E - N_EXPERTS), F32)], axis=1)
    whi = w.astype(BF16)
    wmid = (w - whi.astype(F32)).astype(BF16)
    br = jnp.concatenate([b_r1.reshape(1, N_EGROUPS).astype(F32), b_r2.reshape(1, N_EXPERTS).astype(F32),
                          jnp.zeros((1, LANES - RL_E - N_EXPERTS), F32)], axis=1)
    return dict(nf=norm_ffn.reshape(1, d).astype(F32), whi=whi, wmid=wmid, br=br)


FFN_TM = 256
N_GATHER_SLOTS = 3


def _start_tile_gather(first_row_of, n_rows, src_hbm, dst, sem, priority_of):
    for r in range(n_rows):
        start = pl.multiple_of(first_row_of(r), TOK_TILE_ROWS)
        pltpu.make_async_copy(src_hbm.at[pl.ds(start, TOK_TILE_ROWS), :],
                              dst.at[pl.ds(r * TOK_TILE_ROWS, TOK_TILE_ROWS), :],
                              sem).start(priority=priority_of(r))


def _wait_tile_gather(n_rows, src_hbm, dst, sem):
    pltpu.make_async_copy(src_hbm.at[pl.ds(0, n_rows * TOK_TILE_ROWS), :], dst, sem).wait()


TAB_OFF, TAB_CNT, TAB_TILE_EXPERT, TAB_NVALID = 0, 1, 2, 3
TAB_LANES = 2 * LANES


def _ffn_kernel(tab_ref, pa_ref, pb_ref, xn_hbm, wg_ref, wu_ref, wd_ref, ys_ref,
                src, xbuf, wbf, sem, *, n_tokens, tm):
    i = pl.program_id(0)
    n_valid = tab_ref[TAB_NVALID, 0]
    slot = lax.rem(i, N_GATHER_SLOTS)
    gather_priority = lambda r: 1

    @pl.when(i == 0)
    def _build_source_rows():
        for e in range(N_EXPERTS):
            cnt_e = tab_ref[TAB_CNT, RL_E + e]
            first = tab_ref[TAB_OFF, RL_E + e] + cnt_e
            n_pad = lax.rem(tm - lax.rem(cnt_e, tm), tm)

            def pad_body(r, carry, first=first):
                src[first + r] = 0
                return carry
            lax.fori_loop(0, n_pad, pad_body, 0)

        def body(t, carry):
            first_row = t * TOK_TILE_ROWS
            src[pa_ref[t]] = first_row
            src[pb_ref[t]] = first_row
            return carry
        lax.fori_loop(0, n_tokens, body, 0, unroll=8)
        _start_tile_gather(lambda r: src[r], tm, xn_hbm, xbuf.at[0], sem.at[0], gather_priority)
        second = jnp.where(n_valid > 1, tm, 0)
        _start_tile_gather(lambda r: src[second + r], tm, xn_hbm, xbuf.at[1], sem.at[1], gather_priority)

    changed = jnp.logical_or(i == 0, tab_ref[TAB_TILE_EXPERT, i]
                             != tab_ref[TAB_TILE_EXPERT, jnp.maximum(i - 1, 0)])

    @pl.when(jnp.logical_and(changed, i < n_valid))
    def _cast_weights():
        wbf[0] = wg_ref[...].astype(BF16)
        wbf[1] = wu_ref[...].astype(BF16)
        wbf[2] = wd_ref[...].astype(BF16)

    @pl.when(i < n_valid)
    def _compute():
        _wait_tile_gather(tm, xn_hbm, xbuf.at[slot], sem.at[slot])
        x = _load_token_tiles(xbuf.at[slot], tm).astype(BF16)
        hg = _dot(x, wbf[0])
        hu = _dot(x, wbf[1])
        y = _dot((_silu(hg) * hu).astype(BF16), wbf[2])
        base = jnp.where(i + 2 < n_valid, i + 2, 0) * tm
        ahead = lax.rem(i + 2, N_GATHER_SLOTS)
        _start_tile_gather(lambda r: src[base + r], tm, xn_hbm, xbuf.at[ahead], sem.at[ahead],
                           gather_priority)
        _store_token_tiles(ys_ref, y)

    @pl.when(i == n_valid - 1)
    def _drain():
        for k in (1, 2):
            s = lax.rem(i + k, N_GATHER_SLOTS)
            _wait_tile_gather(tm, xn_hbm, xbuf.at[s], sem.at[s])

    @pl.when(i >= n_valid)
    def _pad():
        ys_ref[...] = jnp.zeros_like(ys_ref)


def _expert_ffn(xn_tiles, tab, pos_a, pos_b, n_tiles, wg, wu, wd):
    n = pos_a.shape[0]
    tm = FFN_TM
    d, dff = wg.shape[1], wg.shape[2]
    rows = tm * TOK_TILE_ROWS
    idx = lambda i, tab, pa, pb: (tab[TAB_TILE_EXPERT, i], 0, 0)
    grid_spec = pltpu.PrefetchScalarGridSpec(
        num_scalar_prefetch=3,
        grid=(n_tiles,),
        in_specs=[pl.BlockSpec(memory_space=pl.ANY),
                  pl.BlockSpec((None, d, dff), idx),
                  pl.BlockSpec((None, d, dff), idx),
                  pl.BlockSpec((None, dff, d), idx)],
        out_specs=pl.BlockSpec((rows, LANES), lambda i, tab, pa, pb: (i, 0)),
        scratch_shapes=[pltpu.SMEM((n_tiles * tm,), jnp.int32),
                        pltpu.VMEM((N_GATHER_SLOTS, rows, LANES), F32),
                        pltpu.VMEM((3, d, dff), BF16),
                        pltpu.SemaphoreType.DMA((N_GATHER_SLOTS,))],
    )
    return pl.pallas_call(
        functools.partial(_ffn_kernel, n_tokens=n, tm=tm),
        out_shape=jax.ShapeDtypeStruct((n_tiles * rows, LANES), F32),
        grid_spec=grid_spec,
        compiler_params=pltpu.CompilerParams(
            dimension_semantics=("arbitrary",), vmem_limit_bytes=VMEM_LIMIT),
        name="expert_ffn",
    )(tab, pos_a, pos_b, xn_tiles, wg, wu, wd)


def _positions_kernel(info_ref, cnt_ref, pos_ref, tab_ref, *, tm, chunk):
    lane_i = lax.broadcasted_iota(jnp.int32, (1, LANES), 1)
    lane = lane_i.astype(F32)
    is_expert = jnp.logical_and(lane_i >= RL_E, lane_i < RL_E + N_EXPERTS)
    cnt = jnp.where(is_expert, cnt_ref[...], 0.0)
    padded = jnp.floor((cnt + (tm - 1)) / tm) * tm
    ri = lax.broadcasted_iota(jnp.int32, (LANES, LANES), 0)
    ci = lax.broadcasted_iota(jnp.int32, (LANES, LANES), 1)
    before = jnp.where(ri < ci, 1.0, 0.0).astype(BF16)
    hi, mid, lo = _split3(jnp.broadcast_to(padded, (SUBLANES, LANES)))
    off = ((_dot(hi, before) + _dot(mid, before)) + _dot(lo, before))[0:1, :]
    pick = jnp.where(lax.broadcasted_iota(jnp.int32, (SUBLANES, LANES), 0) == lane_i, 1.0, 0.0).astype(BF16)

    total = jnp.sum(padded, axis=-1, keepdims=True)
    tile_row = lax.broadcasted_iota(jnp.int32, (TAB_LANES, 1), 0).astype(F32) * tm
    ends = off + padded
    done = jnp.logical_and(is_expert, ends <= jnp.minimum(tile_row, total - 1.0))
    te_col = jnp.sum(jnp.where(done, 1.0, 0.0), axis=-1, keepdims=True)
    te_rows = _dot_nt(pick, jnp.where(lane_i == 0, te_col, 0.0).astype(BF16))
    tab_ref[...] = jnp.zeros_like(tab_ref)
    tab_ref[TAB_OFF:TAB_OFF + 1, 0:LANES] = off.astype(jnp.int32)
    tab_ref[TAB_CNT:TAB_CNT + 1, 0:LANES] = cnt.astype(jnp.int32)
    tab_ref[TAB_TILE_EXPERT:TAB_TILE_EXPERT + 1, :] = te_rows[0:1, :].astype(jnp.int32)
    tab_ref[TAB_NVALID:TAB_NVALID + 1, 0:LANES] = jnp.broadcast_to(total / tm, (1, LANES)).astype(jnp.int32)

    n = info_ref.shape[0]
    for c0 in range(0, n, chunk):
        blk = info_ref[c0:c0 + chunk, :]
        lane_a = blk[:, R_EA:R_EA + 1] + RL_E
        lane_b = blk[:, R_EB:R_EB + 1] + RL_E
        pos_a = blk[:, R_RA:R_RA + 1] + jnp.sum(jnp.where(lane == lane_a, off, 0.0), axis=-1, keepdims=True)
        pos_b = blk[:, R_RB:R_RB + 1] + jnp.sum(jnp.where(lane == lane_b, off, 0.0), axis=-1, keepdims=True)
        z_hi, z_mid, z_lo = _split3(jnp.where(lane_i == 0, pos_a, jnp.where(lane_i == 1, pos_b, 0.0)))
        rows = (_dot_nt(pick, z_hi) + _dot_nt(pick, z_mid)) + _dot_nt(pick, z_lo)
        pos_ref[:, c0:c0 + chunk] = rows.astype(jnp.int32)


def _routing_tables(info, counts, n_tiles, tm):
    n = info.shape[0]
    groups = n // LANES
    assert n_tiles <= TAB_LANES
    chunk = LANES * max(g for g in range(1, 65) if groups % g == 0)
    pos, tab = pl.pallas_call(
        functools.partial(_positions_kernel, tm=tm, chunk=chunk),
        out_shape=(jax.ShapeDtypeStruct((SUBLANES, n), jnp.int32),
                   jax.ShapeDtypeStruct((SUBLANES, TAB_LANES), jnp.int32)),
        grid=(1,),
        in_specs=[pl.BlockSpec((n, LANES), lambda i: (0, 0)), pl.BlockSpec((1, LANES), lambda i: (0, 0))],
        out_specs=(pl.BlockSpec((SUBLANES, n), lambda i: (0, 0)),
                   pl.BlockSpec((SUBLANES, TAB_LANES), lambda i: (0, 0))),
        compiler_params=pltpu.CompilerParams(
            dimension_semantics=("arbitrary",), vmem_limit_bytes=VMEM_LIMIT),
        name="positions",
    )(info, counts)
    return tab, pos[0], pos[1]


def _combine_kernel(pa_ref, pb_ref, h_ref, info_ref, ys_hbm, nfin_ref, yp_ref, ysm_ref,
                    buf_a, buf_b, sem, *, n_prompt_tiles):
    i = pl.program_id(0)
    n_steps = pl.num_programs(0)
    tt = h_ref.shape[0]
    slot = lax.rem(i, N_GATHER_SLOTS)

    def start(tile, s):
        base = tile * tt
        _start_tile_gather(lambda r: pa_ref[base + r] * TOK_TILE_ROWS, tt, ys_hbm, buf_a.at[s],
                           sem.at[s], lambda r: 0)
        _start_tile_gather(lambda r: pb_ref[base + r] * TOK_TILE_ROWS, tt, ys_hbm, buf_b.at[s],
                           sem.at[s], lambda r: 1)

    @pl.when(i == 0)
    def _first():
        start(0, 0)
        start(lax.rem(1, n_steps), 1)

    _wait_tile_gather(tt, ys_hbm, buf_a.at[slot], sem.at[slot])
    _wait_tile_gather(tt, ys_hbm, buf_b.at[slot], sem.at[slot])
    info = info_ref[...]
    rows_a = _load_token_tiles(buf_a.at[slot], tt)
    rows_b = _load_token_tiles(buf_b.at[slot], tt)
    x = h_ref[...]
    start(lax.rem(i + 2, n_steps), lax.rem(i + 2, N_GATHER_SLOTS))
    h = x + info[:, R_GA:R_GA + 1] * rows_a + info[:, R_GB:R_GB + 1] * rows_b
    y = h * _rms_scale(h) * nfin_ref[...]

    @pl.when(i < n_prompt_tiles)
    def _prompt():
        yp_ref[...] = y

    @pl.when(i >= n_prompt_tiles)
    def _sample():
        ysm_ref[...] = y

    @pl.when(i == n_steps - 1)
    def _drain():
        for k in (1, 2):
            s = lax.rem(i + k, N_GATHER_SLOTS)
            _wait_tile_gather(tt, ys_hbm, buf_a.at[s], sem.at[s])
            _wait_tile_gather(tt, ys_hbm, buf_b.at[s], sem.at[s])


def _combine(hmid, info, ys, pos_a, pos_b, nfin, n_prompt):
    n, d = hmid.shape
    tt = CHUNK
    n_prompt_tiles = n_prompt // tt
    n_sample = n - n_prompt
    grid_spec = pltpu.PrefetchScalarGridSpec(
        num_scalar_prefetch=2,
        grid=(n // tt,),
        in_specs=[pl.BlockSpec((tt, d), lambda i, pa, pb: (i, 0)),
                  pl.BlockSpec((tt, LANES), lambda i, pa, pb: (i, 0)),
                  pl.BlockSpec(memory_space=pl.ANY),
                  pl.BlockSpec((1, d), lambda i, pa, pb: (0, 0))],
        out_specs=(pl.BlockSpec((tt, d), lambda i, pa, pb: (jnp.minimum(i, n_prompt_tiles - 1), 0)),
                   pl.BlockSpec((tt, d), lambda i, pa, pb: (jnp.maximum(i - n_prompt_tiles, 0), 0))),
        scratch_shapes=[pltpu.VMEM((N_GATHER_SLOTS, tt * TOK_TILE_ROWS, LANES), F32),
                        pltpu.VMEM((N_GATHER_SLOTS, tt * TOK_TILE_ROWS, LANES), F32),
                        pltpu.SemaphoreType.DMA((N_GATHER_SLOTS,))],
    )
    return pl.pallas_call(
        functools.partial(_combine_kernel, n_prompt_tiles=n_prompt_tiles),
        out_shape=(jax.ShapeDtypeStruct((n_prompt, d), F32),
                   jax.ShapeDtypeStruct((n_sample, d), F32)),
        grid_spec=grid_spec,
        compiler_params=pltpu.CompilerParams(
            dimension_semantics=("arbitrary",), vmem_limit_bytes=VMEM_LIMIT),
        name="combine",
    )(pos_a, pos_b, hmid, info, ys, nfin)


def _moe_and_final_norm(hmid, n_prompt, rp, wg, wu, wd, nfin):
    n = hmid.shape[0]
    tm = FFN_TM
    n_tiles = (2 * n + N_EXPERTS * (tm - 1)) // tm
    xn, info, counts = _router(hmid, rp)
    tab, pos_a, pos_b = _routing_tables(info, counts, n_tiles, tm)
    ys = _expert_ffn(xn, tab, pos_a, pos_b, n_tiles, wg, wu, wd)
    return _combine(hmid, info, ys, pos_a, pos_b, nfin, n_prompt)


def kernel(x_prompt, x_sample, state_mlstm_C, state_mlstm_n, state_mlstm_m, state_mlstm_conv, state_ssm, state_ssm_conv, meta_tokens, norm_mix, w_in, conv_a_w, conv_a_b, w_q, w_k, w_v, b_i, b_f, norm_a, conv_b_w, conv_b_b, dt_bias, a_log, d_skip, norm_b, w_out, norm_ffn, w_r1, b_r1, w_r2, b_r2, w_gate, w_up, w_down, norm_final):
    bsz, seq, d = x_prompt.shape
    nb = x_sample.shape[0]
    d_a = H_A * DH_A
    conv_b = H_B * HD_B + 2 * G_B * N_STATE
    assert w_in.shape[0] == 1 and x_sample.shape[1] == 1 and seq % CHUNK == 0 and nb == CHUNK
    mp = _prep_mixer_params(norm_mix, w_in, conv_a_w, conv_a_b, w_q, w_k, w_v, b_i, b_f, norm_a,
                            conv_b_w, conv_b_b, dt_bias, a_log, d_skip, norm_b, w_out)
    rp = _prep_router_params(norm_ffn, w_r1, b_r1, w_r2, b_r2)
    xmeta = jnp.concatenate([jnp.zeros((CHUNK - N_META, d), F32), meta_tokens.astype(F32)], 0)

    hmid, p_c, p_n, p_m, p_ca, p_s, p_cb = _prompt_mixer(x_prompt.astype(F32), xmeta, mp, nb)
    m0 = jnp.pad(state_mlstm_m.reshape(nb, H_A).astype(F32), ((0, 0), (0, LANES - H_A)))
    hmid, s_c, s_n, s_m, s_ca, s_s, s_cb = _sample_mixer(
        x_sample.reshape(nb, d).astype(F32),
        state_mlstm_C.reshape(nb, H_A, DH_A, DH_A).astype(F32),
        state_mlstm_n.reshape(nb, d_a).astype(F32),
        m0,
        state_mlstm_conv.reshape(nb, (CONV_W - 1) * d_a).astype(F32),
        state_ssm.reshape(nb, H_B // 2, 2 * HD_B, N_STATE).astype(F32),
        state_ssm_conv.reshape(nb, (CONV_W - 1) * conv_b).astype(F32),
        mp, hmid, bsz * seq)

    wshape = w_gate.shape[1:]
    y_p, y_s = _moe_and_final_norm(
        hmid, bsz * seq, rp, w_gate.reshape(wshape).astype(F32), w_up.reshape(wshape).astype(F32),
        w_down.reshape(w_down.shape[1:]).astype(F32), norm_final.reshape(1, d).astype(F32))

    return (y_p.reshape(bsz, seq, d), y_s.reshape(nb, 1, d),
            p_c.reshape(1, bsz, H_A, DH_A, DH_A), p_n.reshape(1, bsz, H_A, DH_A),
            p_m[:, 0, :H_A].reshape(1, bsz, H_A), p_ca.reshape(1, bsz, CONV_W - 1, d_a),
            p_s.reshape(1, bsz, H_B, HD_B, N_STATE), p_cb.reshape(1, bsz, CONV_W - 1, conv_b),
            s_c.reshape(1, nb, H_A, DH_A, DH_A), s_n.reshape(1, nb, H_A, DH_A),
            s_m[:, :H_A].reshape(1, nb, H_A), s_ca.reshape(1, nb, CONV_W - 1, d_a),
            s_s.reshape(1, nb, H_B, HD_B, N_STATE), s_cb.reshape(1, nb, CONV_W - 1, conv_b))
```

```python
import functools

import jax
import jax.numpy as jnp
from jax import lax
from jax.experimental import pallas as pl
from jax.experimental.pallas import tpu as pltpu

F32 = jnp.float32
BF16 = jnp.bfloat16

EPS = 1e-6
N_META = 16
CONV_W = 4
CHUNK = 128
H_A = 8
DH_A = 128
H_B = 16
HD_B = 64
N_STATE = 128
G_B = 2
N_EGROUPS = 4
N_EPG = 4
N_EXPERTS = 16
LANES = 128
SUBLANES = 8
CONV_HDR = SUBLANES
VMEM_LIMIT = 56 * 1024 * 1024

L_F = 0
L_DTA = 8
L_I = 24
L_DT = 32

NEG_INF = float("-inf")


def _dot(a, b):
    return jnp.dot(a, b, preferred_element_type=F32)


def _dot_nt(a, b):
    return lax.dot_general(a, b, (((1,), (1,)), ((), ())), preferred_element_type=F32)


def _dot_tn(a, b):
    return lax.dot_general(a, b, (((0,), (0,)), ((), ())), preferred_element_type=F32)


def _split3(x):
    hi = x.astype(BF16)
    r = x - hi.astype(F32)
    mid = r.astype(BF16)
    lo = (r - mid.astype(F32)).astype(BF16)
    return hi, mid, lo


def _silu(x):
    return x * jax.nn.sigmoid(x)


def _softplus_parts(x):
    t = jnp.log1p(jnp.exp(-jnp.abs(x)))
    return jnp.maximum(x, 0.0) + t, jnp.minimum(x, 0.0) - t


def _rms_scale(x):
    return lax.rsqrt(jnp.mean(x * x, axis=-1, keepdims=True) + EPS)


TOK_TILE_ROWS = SUBLANES


def _store_token_tiles(ref, x):
    n = x.shape[0]
    for j in range(TOK_TILE_ROWS):
        ref[pl.ds(j, n, stride=TOK_TILE_ROWS), :] = x[:, j * LANES:(j + 1) * LANES]


def _causal_conv(x, tail, w_ref, b_ref):
    n_tail = tail.shape[0]
    row = lax.broadcasted_iota(jnp.int32, (n_tail, 1), 0)
    acc = w_ref[CONV_W - 1:CONV_W, :] * x + b_ref[...]
    for k in range(1, CONV_W):
        rolled = pltpu.roll(x, k, axis=0)
        head = jnp.where(row < k, pltpu.roll(tail, k, axis=0), rolled[0:n_tail])
        shifted = jnp.concatenate([head, rolled[n_tail:]], axis=0)
        acc = acc + w_ref[CONV_W - 1 - k:CONV_W - k, :] * shifted
    return acc


def _load_token_tiles(ref, n):
    return jnp.concatenate(
        [ref[pl.ds(j, n, stride=TOK_TILE_ROWS), :] for j in range(TOK_TILE_ROWS)], axis=1)


PROMPT_ROWS = 2


def _prompt_rows_kernel(xmeta_ref, xp_ref, nmix_ref, wcat_ref, bsm_ref, alog_ref,
                        cwa_ref, cba_ref, cwb_ref, cbb_ref, wq_ref, wk_ref, wv_ref,
                        na_ref, nb_ref, dsk_ref, wout_ref,
                        hmid_hbm, c_ref, n_ref, m_ref, conva_ref, s_ref, convb_ref,
                        xa_buf, xbc_buf, y_buf, merged, hout, sem, *, seq, n_prompt_rows):
    p = pl.program_id(0)
    c = pl.program_id(1)
    last_p = pl.num_programs(0) - 1
    last_c = pl.num_programs(1) - 1
    T = CHUNK
    RB = xp_ref.shape[0]
    d_a = H_A * DH_A
    d_b = H_B * HD_B
    conv_b = d_b + 2 * G_B * N_STATE

    def out_copy(r, row0):
        return pltpu.make_async_copy(hout.at[r], hmid_hbm.at[pl.ds(row0, T), :], sem.at[r])

    @pl.when(c == 0)
    def _init():
        c_ref[...] = jnp.zeros_like(c_ref)
        n_ref[...] = jnp.zeros_like(n_ref)
        m_ref[...] = jnp.zeros_like(m_ref)
        s_ref[...] = jnp.zeros_like(s_ref)
        xa_buf[...] = jnp.zeros_like(xa_buf)
        xbc_buf[...] = jnp.zeros_like(xbc_buf)

    @pl.when(jnp.logical_and(p == 0, c == 0))
    def _clear_sample_rows():
        hout[0] = jnp.zeros((T, hout.shape[2]), F32)
        cp = out_copy(0, n_prompt_rows)
        cp.start()
        cp.wait()

    row = lax.broadcasted_iota(jnp.int32, (T, 1), 0)
    valid = jnp.logical_or(c > 0, row >= T - N_META)
    xs_in = [jnp.where(c == 0, xmeta_ref[...], xp_ref[r]) for r in range(RB)]
    x2 = jnp.concatenate(xs_in, axis=0)
    hn = (x2 * _rms_scale(x2) * nmix_ref[...]).astype(BF16)

    lane = lax.broadcasted_iota(jnp.int32, (1, LANES), 1)
    lane_f = lane < L_DTA
    lane_dta = jnp.logical_and(lane >= L_DTA, lane < L_I)
    lane_i = jnp.logical_and(lane >= L_I, lane < L_DT)
    lane_dt = jnp.logical_and(lane >= L_DT, lane < L_DT + H_B)
    ri = lax.broadcasted_iota(jnp.int32, (T, T), 0)
    ci = lax.broadcasted_iota(jnp.int32, (T, T), 1)
    causal = ri >= ci
    tri = jnp.where(causal, 1.0, 0.0).astype(BF16)
    a_neg = jnp.where(lane_dta, -jnp.exp(alog_ref[...]), 0.0)
    left = lane < HD_B
    top = lax.broadcasted_iota(jnp.int32, (LANES, 1), 0) < HD_B

    off_small = 2 * d_a + d_b + conv_b
    pre2 = _dot(hn, wcat_ref[:, off_small:]) + bsm_ref[...]
    xa2 = _dot(hn, wcat_ref[:, 0:d_a])

    gcols, grows, xcs, xabs = [], [], [], []

    def gate_tables():
        for r in range(RB):
            pre = pre2[r * T:(r + 1) * T]
            sp, lsig = _softplus_parts(pre)
            to_cum = jnp.where(lane_f, lsig, jnp.where(lane_dta, sp * a_neg, 0.0))
            to_cum = jnp.where(valid, to_cum, 0.0)
            hi, mid, lo = _split3(to_cum)
            cum = _dot(tri, hi) + _dot(tri, mid) + _dot(tri, lo)
            extra = jnp.where(lane_i, jnp.where(valid, pre, NEG_INF),
                              jnp.where(lane_dt, jnp.where(valid, sp, 0.0), 0.0))
            gcol = cum + extra
            gcols.append(gcol)
            grows.append(gcol.T)

    gate_tables()
    for r in range(RB):
        rs = slice(r * T, (r + 1) * T)
        xa = xa2[rs]
        xc = _causal_conv(xa, xa_buf[r], cwa_ref, cba_ref)
        xa_buf[r] = xa[T - CONV_HDR:T, :]
        conva_ref[r] = xa[T - 3:T, :]
        xcs.append(_silu(xc).astype(BF16))
        xabs.append(xa.astype(BF16))

    items = [(r, h) for h in range(H_A) for r in range(RB)]
    hsl = lambda h: slice(h * DH_A, (h + 1) * DH_A)
    m_alls = [m_ref[r] for r in range(RB)]
    m_news = list(m_alls)
    qs, ks, vs, qks, st, dd = {}, {}, {}, {}, {}, {}

    def stage_qkv(it):
        r, h = it
        qs[it] = _dot(xcs[r][:, hsl(h)], wq_ref[h]).astype(BF16)
        ks[it] = _dot(xcs[r][:, hsl(h)], wk_ref[h]) * (DH_A ** -0.5)
        vs[it] = _dot(xabs[r][:, hsl(h)], wv_ref[h]).astype(BF16)

    def stage_qk(it):
        qks[it] = _dot_nt(qs[it], ks[it].astype(BF16))

    pairs_per_group = H_B // G_B // 2
    groups = [(r, g) for g in range(G_B) for r in range(RB)]
    proj, xbcs, bgs, cgs, cbs = {}, [], {}, {}, {}

    def project(name, lo, hi):
        proj[name] = _dot(hn, wcat_ref[:, lo:hi])

    def ssd_inputs():
        for r in range(RB):
            xbc = proj["xbc"][r * T:(r + 1) * T]
            xbc_c = _causal_conv(xbc, xbc_buf[r], cwb_ref, cbb_ref)
            xbc_buf[r] = xbc[T - CONV_HDR:T, :]
            convb_ref[r] = xbc[T - 3:T, :]
            xbcs.append(_silu(xbc_c))
        for r, g in groups:
            bgs[(r, g)] = xbcs[r][:, d_b + g * N_STATE:d_b + (g + 1) * N_STATE].astype(BF16)
            cgs[(r, g)] = xbcs[r][:, d_b + (G_B + g) * N_STATE:d_b + (G_B + g + 1) * N_STATE].astype(BF16)
            cbs[(r, g)] = _dot_nt(cgs[(r, g)], bgs[(r, g)])

    def stage_weights(it):
        r, h = it
        gcol, grow = gcols[r], grows[r]
        b_col = gcol[:, L_F + h:L_F + h + 1]
        i_col = gcol[:, L_I + h:L_I + h + 1]
        b_row = grow[L_F + h:L_F + h + 1, :]
        i_row = grow[L_I + h:L_I + h + 1, :]
        m0 = m_alls[r][:, h:h + 1]
        dmat = jnp.where(causal, b_col - (b_row - i_row), NEG_INF)
        m_inter = b_col + m0
        m = jnp.maximum(m_inter, jnp.max(dmat, axis=-1, keepdims=True))
        w_inter = jnp.exp(m_inter - m)
        s = qks[it] * jnp.exp(dmat - m)
        n0 = n_ref[r, h:h + 1, :]
        den = (jnp.sum(s, axis=-1, keepdims=True)
               + w_inter * jnp.sum(qs[it].astype(F32) * n0, axis=-1, keepdims=True))
        m_last = m[T - 1:T, :]
        b_last = b_col[T - 1:T, :]
        dec = jnp.exp(b_last + m0 - m_last)
        kw = ks[it] * jnp.exp(b_last - b_col + i_col - m_last)
        n_ref[r, h:h + 1, :] = dec * n0 + jnp.sum(kw, axis=0, keepdims=True)
        m_news[r] = jnp.where(lane == h, m_last, m_news[r])
        st[it] = (s.astype(BF16), kw.astype(BF16), w_inter,
                  jnp.maximum(jnp.abs(den), jnp.exp(-m)), dec)

    def stage_readout(it):
        r, h = it
        s_b, kw_b, w_inter, den, dec = st[it]
        c0 = c_ref[r, h]
        num = _dot(s_b, vs[it]) + w_inter * _dot(qs[it], c0.astype(BF16))
        c_ref[r, h] = dec * c0 + _dot_tn(kw_b, vs[it])
        dd[it] = num / den

    def stage_head_out(it):
        r, h = it
        hh = dd[it]
        hh = hh * _rms_scale(hh) * na_ref[:, hsl(h)]
        merged[r * T:(r + 1) * T, hsl(h)] = (
            hh * jax.nn.sigmoid(proj["za"][r * T:(r + 1) * T, hsl(h)])).astype(BF16)

    pairs = [(r, pi) for pi in range(H_B // 2) for r in range(RB)]
    psl = lambda pi: slice(pi * LANES, (pi + 1) * LANES)
    sw = {}

    def stage_decay(pr):
        r, pi = pr
        g = pi // pairs_per_group
        gcol, grow = gcols[r], grows[r]
        xpair = xbcs[r][:, psl(pi)]
        scs, a_cols, w_cols, a_lasts = [], [], [], []
        for j in (2 * pi, 2 * pi + 1):
            a_col = gcol[:, L_DTA + j:L_DTA + j + 1]
            a_row = grow[L_DTA + j:L_DTA + j + 1, :]
            dt_col = gcol[:, L_DT + j:L_DT + j + 1]
            dt_row = grow[L_DT + j:L_DT + j + 1, :]
            decay = jnp.exp(jnp.where(causal, a_col - a_row, NEG_INF))
            scs.append((cbs[(r, g)] * decay * dt_row).astype(BF16))
            a_last = a_col[T - 1:T, :]
            a_cols.append(a_col)
            a_lasts.append(a_last)
            w_cols.append(jnp.exp(a_last - a_col) * dt_col)
        sw[pr] = (scs, xpair.astype(BF16),
                  (xpair * jnp.where(left, w_cols[0], w_cols[1])).astype(BF16),
                  jnp.exp(jnp.where(left, a_cols[0], a_cols[1])),
                  jnp.exp(jnp.where(top, a_lasts[0], a_lasts[1])))
    def stage_pair_out(pr):
        r, pi = pr
        g = pi // pairs_per_group
        scs, xpb, xw, ea, ea_last = sw[pr]
        s0 = s_ref[r, pi]
        y = jnp.where(left, _dot(scs[0], xpb), _dot(scs[1], xpb))
        y = y + ea * _dot_nt(cgs[(r, g)], s0.astype(BF16))
        s_ref[r, pi] = ea_last * s0 + _dot_tn(xw, bgs[(r, g)])
        y = y + dsk_ref[:, psl(pi)] * xbcs[r][:, psl(pi)]
        y_buf[r, :, psl(pi)] = y * _silu(proj["zb"][r * T:(r + 1) * T, psl(pi)])

    def each(stage, seq):
        for e in seq:
            stage(e)

    each(stage_qkv, items)
    project("xbc", 2 * d_a + d_b, off_small)
    each(stage_qk, items)
    project("za", d_a, 2 * d_a)
    ssd_inputs()
    each(stage_weights, items)
    each(stage_readout, items)
    project("zb", 2 * d_a, 2 * d_a + d_b)
    for r in range(RB):
        m_ref[r] = m_news[r]
    each(stage_head_out, items)
    each(stage_decay, pairs)
    each(stage_pair_out, pairs)
    gw = d_b // G_B
    for r in range(RB):
        for g in range(G_B):
            yg = y_buf[r, :, g * gw:(g + 1) * gw]
            merged[r * T:(r + 1) * T, d_a + g * gw:d_a + (g + 1) * gw] = (
                yg * _rms_scale(yg) * nb_ref[:, g * gw:(g + 1) * gw]).astype(BF16)

    @pl.when(c > 0)
    def _out():
        out2 = x2 + _dot(merged[...], wout_ref[...])

        @pl.when(jnp.logical_or(c > 1, p > 0))
        def _wait_previous():
            for r in range(RB):
                out_copy(r, 0).wait()

        for r in range(RB):
            hout[r] = out2[r * T:(r + 1) * T]
            out_copy(r, (p * RB + r) * seq + (c - 1) * T).start()

        @pl.when(jnp.logical_and(p == last_p, c == last_c))
        def _drain():
            for r in range(RB):
                out_copy(r, 0).wait()


def _const_spec(shape):
    nd = len(shape)
    return pl.BlockSpec(shape, lambda b, c, _nd=nd: (0,) * _nd)


def _prompt_mixer(x_prompt, xmeta, p, n_extra_rows):
    bsz, seq, d = x_prompt.shape
    assert n_extra_rows == CHUNK and seq % CHUNK == 0
    n_chunks = seq // CHUNK + 1
    cps = seq // CHUNK
    d_a = H_A * DH_A
    conv_b = H_B * HD_B + 2 * G_B * N_STATE
    consts = [p["nmix"], p["wcat"], p["bsm"], p["alog"], p["cwa"], p["cba"], p["cwb"], p["cbb"],
              p["wq"], p["wk"], p["wv"], p["na"], p["nb"], p["dsk"], p["wout"]]
    rb = PROMPT_ROWS
    assert bsz % rb == 0
    in_specs = [_const_spec(xmeta.shape),
                pl.BlockSpec((rb, CHUNK, d), lambda b, c: (b, jnp.maximum(c - 1, 0), 0))]
    in_specs += [_const_spec(a.shape) for a in consts]
    out_shape = (
        jax.ShapeDtypeStruct((bsz * seq + n_extra_rows, d), F32),
        jax.ShapeDtypeStruct((bsz, H_A, DH_A, DH_A), F32),
        jax.ShapeDtypeStruct((bsz, H_A, DH_A), F32),
        jax.ShapeDtypeStruct((bsz, 1, LANES), F32),
        jax.ShapeDtypeStruct((bsz, CONV_W - 1, d_a), F32),
        jax.ShapeDtypeStruct((bsz, H_B // 2, 2 * HD_B, N_STATE), F32),
        jax.ShapeDtypeStruct((bsz, CONV_W - 1, conv_b), F32),
    )
    out_specs = (
        pl.BlockSpec(memory_space=pl.ANY),
        pl.BlockSpec((rb, H_A, DH_A, DH_A), lambda b, c: (b, 0, 0, 0)),
        pl.BlockSpec((rb, H_A, DH_A), lambda b, c: (b, 0, 0)),
        pl.BlockSpec((rb, 1, LANES), lambda b, c: (b, 0, 0)),
        pl.BlockSpec((rb, CONV_W - 1, d_a), lambda b, c: (b, 0, 0)),
        pl.BlockSpec((rb, H_B // 2, 2 * HD_B, N_STATE), lambda b, c: (b, 0, 0, 0)),
        pl.BlockSpec((rb, CONV_W - 1, conv_b), lambda b, c: (b, 0, 0)),
    )
    return pl.pallas_call(
        functools.partial(_prompt_rows_kernel, seq=seq, n_prompt_rows=bsz * seq),
        out_shape=out_shape,
        grid=(bsz // rb, n_chunks),
        in_specs=in_specs,
        out_specs=out_specs,
        scratch_shapes=[
            pltpu.VMEM((rb, CONV_HDR, d_a), F32),
            pltpu.VMEM((rb, CONV_HDR, conv_b), F32),
            pltpu.VMEM((rb, CHUNK, H_B * HD_B), F32),
            pltpu.VMEM((rb * CHUNK, d_a + H_B * HD_B), BF16),
            pltpu.VMEM((rb, CHUNK, d), F32),
            pltpu.SemaphoreType.DMA((rb,)),
        ],
        compiler_params=pltpu.CompilerParams(
            dimension_semantics=("arbitrary", "arbitrary"), vmem_limit_bytes=VMEM_LIMIT),
        name="prompt_mixer",
    )(xmeta, x_prompt, *consts)


def _regroup_w_in_kernel(w_ref, o_ref):
    d_a = H_A * DH_A
    d_b = H_B * HD_B
    conv_b = d_b + 2 * G_B * N_STATE
    o_i = 2 * d_a
    o_f = o_i + H_A
    o_zb = o_f + H_A
    o_xbc = o_zb + d_b
    o_dt = o_xbc + conv_b
    rows = w_ref.shape[0]
    o_ref[:, 0:2 * d_a] = w_ref[:, 0:2 * d_a].astype(BF16)
    o_ref[:, 2 * d_a:2 * d_a + d_b] = w_ref[:, o_zb:o_zb + d_b].astype(BF16)
    o_ref[:, 2 * d_a + d_b:2 * d_a + d_b + conv_b] = w_ref[:, o_xbc:o_xbc + conv_b].astype(BF16)
    small = jnp.concatenate(
        [w_ref[:, o_f:o_f + H_A], w_ref[:, o_dt:o_dt + H_B], w_ref[:, o_i:o_i + H_A],
         w_ref[:, o_dt:o_dt + H_B], jnp.zeros((rows, LANES - (L_DT + H_B)), F32)], axis=1)
    o_ref[:, 2 * d_a + d_b + conv_b:] = small.astype(BF16)


def _prep_mixer_params(norm_mix, w_in, conv_a_w, conv_a_b, w_q, w_k, w_v, b_i, b_f, norm_a,
                       conv_b_w, conv_b_b, dt_bias, a_log, d_skip, norm_b, w_out):
    d_a = H_A * DH_A
    d_b = H_B * HD_B
    conv_b = d_b + 2 * G_B * N_STATE
    d_model, d_in = w_in.shape[1], w_in.shape[2]
    n_cols = 2 * d_a + d_b + conv_b + LANES
    rows = 256
    assert w_in.shape[0] == 1 and d_model % rows == 0
    wcat = pl.pallas_call(
        _regroup_w_in_kernel,
        out_shape=jax.ShapeDtypeStruct((d_model, n_cols), BF16),
        grid=(d_model // rows,),
        in_specs=[pl.BlockSpec((None, rows, d_in), lambda i: (0, i, 0))],
        out_specs=pl.BlockSpec((rows, n_cols), lambda i: (i, 0)),
        compiler_params=pltpu.CompilerParams(
            dimension_semantics=("arbitrary",), vmem_limit_bytes=VMEM_LIMIT),
        name="regroup_w_in",
    )(w_in.astype(F32))

    def lanes(parts):
        pieces, at = [], 0
        for off, a in parts:
            pieces += [jnp.zeros((1, off - at), F32), a.astype(F32)]
            at = off + a.shape[1]
        return jnp.concatenate(pieces + [jnp.zeros((1, LANES - at), F32)], axis=1)

    return dict(
        nmix=norm_mix.reshape(1, -1).astype(F32),
        wcat=wcat,
        bsm=lanes([(L_F, b_f), (L_DTA, dt_bias), (L_I, b_i), (L_DT, dt_bias)]),
        alog=lanes([(L_DTA, a_log)]),
        cwa=conv_a_w.reshape(CONV_W, d_a).astype(F32), cba=conv_a_b.reshape(1, d_a).astype(F32),
        cwb=conv_b_w.reshape(CONV_W, conv_b).astype(F32), cbb=conv_b_b.reshape(1, conv_b).astype(F32),
        wq=w_q.reshape(H_A, DH_A, DH_A).astype(BF16), wk=w_k.reshape(H_A, DH_A, DH_A).astype(BF16),
        wv=w_v.reshape(H_A, DH_A, DH_A).astype(BF16),
        na=norm_a.reshape(1, d_a).astype(F32), nb=norm_b.reshape(1, d_b).astype(F32),
        dsk=jnp.repeat(d_skip.reshape(H_B).astype(F32), HD_B)[None, :],
        wout=w_out.reshape(d_a + d_b, -1).astype(BF16),
    )


SAMPLE_BLOCK = 8


def _expand_lanes(vals, first_lane, n_heads, width):
    r = lax.broadcasted_iota(jnp.int32, (LANES, n_heads * width), 0) - first_lane
    c = lax.broadcasted_iota(jnp.int32, (LANES, n_heads * width), 1)
    sel = jnp.logical_and(c >= r * width, c < (r + 1) * width)
    e = jnp.where(sel, 1.0, 0.0).astype(BF16)
    hi, mid, lo = _split3(vals)
    return (_dot(hi, e) + _dot(mid, e)) + _dot(lo, e)


def _sample_pre_kernel(x_ref, nmix_ref, wcat_ref, bsm_ref, alog_ref, cwa_ref, cba_ref, cwb_ref, cbb_ref,
                       wq_ref, wk_ref, wv_ref, dsk_ref, conva_ref, convb_ref, n0_ref, m0_ref,
                       conva_out, convb_out, n1_out, m1_out, g_out, qt_out, kwt_out, xwt_out,
                       v_out, bc_out, a1_out, w1_out, den_out, y1_out, ea_out, zbs_out, zas_out):
    d_a = H_A * DH_A
    d_b = H_B * HD_B
    conv_b = d_b + 2 * G_B * N_STATE
    shift_i = LANES - (L_I - L_F)
    x = x_ref[...]
    hn = (x * _rms_scale(x) * nmix_ref[...]).astype(BF16)
    lane = lax.broadcasted_iota(jnp.int32, (1, LANES), 1)
    lane_f = lane < L_DTA
    lane_dta = jnp.logical_and(lane >= L_DTA, lane < L_I)
    pre = _dot(hn, wcat_ref[:, 2 * d_a + d_b + conv_b:]) + bsm_ref[...]
    sp, lsig = _softplus_parts(pre)
    a_neg = jnp.where(lane_dta, -jnp.exp(alog_ref[...]), 0.0)
    pre_al = pltpu.roll(pre, shift_i, axis=1)
    sp_al = pltpu.roll(sp, shift_i, axis=1)
    m_inter = lsig + m0_ref[...]
    m = jnp.maximum(m_inter, pre_al)
    w_inter = jnp.exp(m_inter - m)
    sfac = jnp.exp(pre_al - m)
    ea = jnp.exp(sp * a_neg)
    dt = sp_al

    xa = _dot(hn, wcat_ref[:, 0:d_a])
    xc = (cwa_ref[0:1, :] * conva_ref[:, 0:d_a] + cwa_ref[1:2, :] * conva_ref[:, d_a:2 * d_a]
          + cwa_ref[2:3, :] * conva_ref[:, 2 * d_a:3 * d_a] + cwa_ref[3:4, :] * xa + cba_ref[...])
    conva_out[:, 0:2 * d_a] = conva_ref[:, d_a:3 * d_a]
    conva_out[:, 2 * d_a:3 * d_a] = xa
    xc = _silu(xc).astype(BF16)
    xab = xa.astype(BF16)
    sf_e = _expand_lanes(sfac, L_F, H_A, DH_A)
    w_e = _expand_lanes(w_inter, L_F, H_A, DH_A)
    qk8 = jnp.zeros((x.shape[0], LANES), F32)
    qn8 = jnp.zeros((x.shape[0], LANES), F32)
    for h in range(H_A):
        sl = slice(h * DH_A, (h + 1) * DH_A)
        q = _dot(xc[:, sl], wq_ref[h])
        k = _dot(xc[:, sl], wk_ref[h]) * (DH_A ** -0.5)
        v = _dot(xab[:, sl], wv_ref[h])
        kw = k * sf_e[:, sl]
        qk8 = jnp.where(lane == h, jnp.sum(q * k, axis=-1, keepdims=True), qk8)
        qn8 = jnp.where(lane == h, jnp.sum(q * n0_ref[:, sl], axis=-1, keepdims=True), qn8)
        n1_out[:, sl] = w_e[:, sl] * n0_ref[:, sl] + kw
        v_out[:, sl] = v
        qt_out[h] = q.T
        kwt_out[h] = kw.T
    s8 = qk8 * sfac
    a1_out[...] = _expand_lanes(s8, L_F, H_A, DH_A) * v_out[...]
    w1_out[...] = w_e
    den_out[...] = jnp.maximum(jnp.abs(_expand_lanes(s8 + w_inter * qn8, L_F, H_A, DH_A)),
                               jnp.exp(-_expand_lanes(m, L_F, H_A, DH_A)))
    m1_out[...] = m
    g_out[...] = jnp.where(lane_f, w_inter, jnp.where(lane_dta, ea, 0.0))
    zas_out[...] = jax.nn.sigmoid(_dot(hn, wcat_ref[:, d_a:2 * d_a]))

    off_xbc = 2 * d_a + d_b
    xbc = _dot(hn, wcat_ref[:, off_xbc:off_xbc + conv_b])
    xbc_c = (cwb_ref[0:1, :] * convb_ref[:, 0:conv_b] + cwb_ref[1:2, :] * convb_ref[:, conv_b:2 * conv_b]
             + cwb_ref[2:3, :] * convb_ref[:, 2 * conv_b:3 * conv_b] + cwb_ref[3:4, :] * xbc + cbb_ref[...])
    convb_out[:, 0:2 * conv_b] = convb_ref[:, conv_b:3 * conv_b]
    convb_out[:, 2 * conv_b:3 * conv_b] = xbc
    xbc_c = _silu(xbc_c)
    xs = xbc_c[:, 0:d_b]
    bc = xbc_c[:, d_b:conv_b]
    bc_out[...] = bc
    heads_per_group = H_B // G_B
    cbl = jnp.zeros((x.shape[0], LANES), F32)
    for g in range(G_B):
        cb_g = jnp.sum(bc[:, g * N_STATE:(g + 1) * N_STATE]
                       * bc[:, (G_B + g) * N_STATE:(G_B + g + 1) * N_STATE], axis=-1, keepdims=True)
        in_g = jnp.logical_and(lane >= L_DTA + g * heads_per_group,
                               lane < L_DTA + (g + 1) * heads_per_group)
        cbl = jnp.where(in_g, cb_g, cbl)
    dt_e = _expand_lanes(dt, L_DTA, H_B, HD_B)
    y1_out[...] = _expand_lanes(cbl * dt, L_DTA, H_B, HD_B) * xs + dsk_ref[...] * xs
    ea_out[...] = _expand_lanes(ea, L_DTA, H_B, HD_B)
    zbs_out[...] = _silu(_dot(hn, wcat_ref[:, 2 * d_a:2 * d_a + d_b]))
    xw = xs * dt_e
    for pi in range(H_B // 2):
        xwt_out[pi] = xw[:, pi * LANES:(pi + 1) * LANES].T


def _sample_state_kernel(g_ref, c0_ref, s0_ref, qt_ref, kwt_ref, xwt_ref, v_ref, bc_ref,
                         c1_ref, s1_ref, qc_ref, ysi_ref):
    i = pl.program_id(0)
    bb = c0_ref.shape[0]
    shift = lax.rem(LANES - lax.rem(i * bb, LANES), LANES)
    lane = lax.broadcasted_iota(jnp.int32, (1, LANES), 1)
    top = lax.broadcasted_iota(jnp.int32, (LANES, 1), 0) < HD_B
    heads_per_group = H_B // G_B
    for h in range(H_A):
        sl = slice(h * DH_A, (h + 1) * DH_A)
        qt = pltpu.roll(qt_ref[h], shift, axis=1)
        kwt = pltpu.roll(kwt_ref[h], shift, axis=1)
        for r in range(bb):
            b = i * bb + r
            c0 = c0_ref[r, h]
            dec = g_ref[b, L_F + h]
            v_row = v_ref[r:r + 1, sl]
            qc_ref[r:r + 1, sl] = jnp.sum(c0 * qt[:, r:r + 1], axis=0, keepdims=True)
            c1_ref[r, h] = dec * c0 + kwt[:, r:r + 1] * v_row
    for pi in range(H_B // 2):
        g = (2 * pi) // heads_per_group
        sl = slice(pi * LANES, (pi + 1) * LANES)
        xwt = pltpu.roll(xwt_ref[pi], shift, axis=1)
        acc = jnp.zeros((LANES, LANES), F32)
        for r in range(bb):
            b = i * bb + r
            s0 = s0_ref[r, pi]
            b_row = bc_ref[r:r + 1, g * N_STATE:(g + 1) * N_STATE]
            c_row = bc_ref[r:r + 1, (G_B + g) * N_STATE:(G_B + g + 1) * N_STATE]
            col = jnp.sum(s0 * c_row, axis=-1, keepdims=True)
            acc = jnp.where(lane == r, col, acc)
            ea_rows = jnp.where(top, g_ref[b, L_DTA + 2 * pi], g_ref[b, L_DTA + 2 * pi + 1])
            s1_ref[r, pi] = ea_rows * s0 + xwt[:, r:r + 1] * b_row
        ysi_ref[:, sl] = acc.T[0:bb, :]


def _sample_post_kernel(a1_ref, w1_ref, den_ref, y1_ref, ea_ref, zbs_ref, zas_ref, x_ref, qc_ref, ysi_ref,
                        na_ref, nb_ref, wout_ref, hall_ref, hmid_ref, merged):
    del hall_ref
    d_a = H_A * DH_A
    d_b = H_B * HD_B
    hh = (a1_ref[...] + w1_ref[...] * qc_ref[...]) / den_ref[...]
    for h in range(H_A):
        sl = slice(h * DH_A, (h + 1) * DH_A)
        hs = hh[:, sl]
        merged[:, sl] = (hs * _rms_scale(hs) * na_ref[:, sl] * zas_ref[:, sl]).astype(BF16)
    y = (y1_ref[...] + ea_ref[...] * ysi_ref[...]) * zbs_ref[...]
    gw = d_b // G_B
    for g in range(G_B):
        yg = y[:, g * gw:(g + 1) * gw]
        merged[:, d_a + g * gw:d_a + (g + 1) * gw] = (
            yg * _rms_scale(yg) * nb_ref[:, g * gw:(g + 1) * gw]).astype(BF16)
    hmid_ref[...] = x_ref[...] + _dot(merged[...], wout_ref[...])


def _vmem_specs(arrays):
    return [pl.BlockSpec(a.shape, lambda *_, _nd=a.ndim: (0,) * _nd) for a in arrays]


def _sample_mixer(x, c0, n0, m0, conva, s0, convb, p, hmid_all, row_offset):
    nb, d = x.shape
    d_a = H_A * DH_A
    d_b = H_B * HD_B
    conv_b = d_b + 2 * G_B * N_STATE
    row = lambda w: jax.ShapeDtypeStruct((nb, w), F32)
    tile = lambda k: jax.ShapeDtypeStruct((k, LANES, nb), F32)
    pre_in = [x, p["nmix"], p["wcat"], p["bsm"], p["alog"], p["cwa"], p["cba"], p["cwb"], p["cbb"],
              p["wq"], p["wk"], p["wv"], p["dsk"], conva, convb, n0, m0]
    pre_out_shape = (row(3 * d_a), row(3 * conv_b), row(d_a), row(LANES), row(LANES),
                     tile(H_A), tile(H_A), tile(H_B // 2), row(d_a), row(2 * G_B * N_STATE),
                     row(d_a), row(d_a), row(d_a), row(d_b), row(d_b), row(d_b), row(d_a))
    (conva1, convb1, n1, m1, g8, qt, kwt, xwt, v, bc, a1, w1, den, y1, ea_e, zbs, zas) = pl.pallas_call(
        _sample_pre_kernel,
        out_shape=pre_out_shape,
        grid=(1,),
        in_specs=_vmem_specs(pre_in),
        out_specs=tuple(pl.BlockSpec(s.shape, lambda i, _nd=len(s.shape): (0,) * _nd) for s in pre_out_shape),
        compiler_params=pltpu.CompilerParams(
            dimension_semantics=("arbitrary",), vmem_limit_bytes=VMEM_LIMIT),
        name="sample_pre",
    )(*pre_in)

    bb = SAMPLE_BLOCK
    const3 = lambda k: pl.BlockSpec((k, LANES, nb), lambda i, g: (0, 0, 0))
    state_grid = pltpu.PrefetchScalarGridSpec(
        num_scalar_prefetch=1,
        grid=(nb // bb,),
        in_specs=[pl.BlockSpec((bb, H_A, DH_A, DH_A), lambda i, g: (i, 0, 0, 0)),
                  pl.BlockSpec((bb, H_B // 2, 2 * HD_B, N_STATE), lambda i, g: (i, 0, 0, 0)),
                  const3(H_A), const3(H_A), const3(H_B // 2),
                  pl.BlockSpec((bb, d_a), lambda i, g: (i, 0)),
                  pl.BlockSpec((bb, 2 * G_B * N_STATE), lambda i, g: (i, 0))],
        out_specs=(pl.BlockSpec((bb, H_A, DH_A, DH_A), lambda i, g: (i, 0, 0, 0)),
                   pl.BlockSpec((bb, H_B // 2, 2 * HD_B, N_STATE), lambda i, g: (i, 0, 0, 0)),
                   pl.BlockSpec((bb, d_a), lambda i, g: (i, 0)),
                   pl.BlockSpec((bb, d_b), lambda i, g: (i, 0))),
    )
    c1, s1, qc, ysi = pl.pallas_call(
        _sample_state_kernel,
        out_shape=(jax.ShapeDtypeStruct(c0.shape, F32), jax.ShapeDtypeStruct(s0.shape, F32),
                   row(d_a), row(d_b)),
        grid_spec=state_grid,
        compiler_params=pltpu.CompilerParams(
            dimension_semantics=("arbitrary",), vmem_limit_bytes=VMEM_LIMIT),
        name="sample_state",
    )(g8, c0, s0, qt, kwt, xwt, v, bc)

    post_in = [a1, w1, den, y1, ea_e, zbs, zas, x, qc, ysi, p["na"], p["nb"], p["wout"]]
    hmid_all = pl.pallas_call(
        _sample_post_kernel,
        out_shape=jax.ShapeDtypeStruct(hmid_all.shape, F32),
        grid=(1,),
        in_specs=_vmem_specs(post_in) + [pl.BlockSpec(memory_space=pl.ANY)],
        out_specs=pl.BlockSpec((nb, d), lambda i: (row_offset // nb, 0)),
        scratch_shapes=[pltpu.VMEM((nb, d_a + d_b), BF16)],
        input_output_aliases={len(post_in): 0},
        compiler_params=pltpu.CompilerParams(
            dimension_semantics=("arbitrary",), vmem_limit_bytes=VMEM_LIMIT),
        name="sample_post",
    )(*post_in, hmid_all)
    return hmid_all, c1, n1, m1, conva1, s1, convb1


R_EA, R_EB, R_RA, R_RB, R_GA, R_GB = 0, 1, 2, 3, 4, 5
RL_E = N_EGROUPS


def _router_kernel(h_ref, nf_ref, whi_ref, wmid_ref, br_ref, xn_ref, info_ref, cnt_ref, carry):
    i = pl.program_id(0)
    tr = h_ref.shape[0]

    @pl.when(i == 0)
    def _init():
        carry[...] = jnp.zeros_like(carry)

    h = h_ref[...]
    xn = h * _rms_scale(h) * nf_ref[...]
    _store_token_tiles(xn_ref, xn)
    x_hi, x_mid, _ = _split3(xn)
    logits = (_dot(x_hi, whi_ref[...]) + _dot(x_hi, wmid_ref[...]) + _dot(x_mid, whi_ref[...])
              + br_ref[...])
    lane_i = lax.broadcasted_iota(jnp.int32, (1, LANES), 1)
    lane = lane_i.astype(F32)
    big = float(LANES)

    def first_lane_of(cond):
        return jnp.min(jnp.where(cond, lane, big), axis=-1, keepdims=True)

    l1 = jnp.where(lane_i < N_EGROUPS, logits, NEG_INF)
    e1 = jnp.exp(l1 - jnp.max(l1, axis=-1, keepdims=True))
    p1 = e1 / jnp.sum(e1, axis=-1, keepdims=True)
    gp = jnp.max(p1, axis=-1, keepdims=True)
    gidx = first_lane_of(p1 == gp)
    lo = RL_E + N_EPG * gidx
    l2 = jnp.where(jnp.logical_and(lane >= lo, lane < lo + N_EPG), logits, NEG_INF)
    va = jnp.max(l2, axis=-1, keepdims=True)
    ia = first_lane_of(l2 == va)
    l2b = jnp.where(lane == ia, NEG_INF, l2)
    vb = jnp.max(l2b, axis=-1, keepdims=True)
    ib = first_lane_of(l2b == vb)
    eb = jnp.exp(vb - va)
    wa = 1.0 / (1.0 + eb)
    wb = eb / (1.0 + eb)

    is_a = lane == ia
    is_b = lane == ib
    onehot = jnp.where(jnp.logical_or(is_a, is_b), 1.0, 0.0)
    ri = lax.broadcasted_iota(jnp.int32, (tr, tr), 0)
    ci = lax.broadcasted_iota(jnp.int32, (tr, tr), 1)
    tri = jnp.where(ri >= ci, 1.0, 0.0).astype(BF16)
    incl = _dot(tri, onehot.astype(BF16))
    excl = incl - onehot + carry[...]
    rank_a = jnp.sum(jnp.where(is_a, excl, 0.0), axis=-1, keepdims=True)
    rank_b = jnp.sum(jnp.where(is_b, excl, 0.0), axis=-1, keepdims=True)
    carry[...] = carry[...] + incl[tr - 1:tr, :]
    cnt_ref[...] = carry[...]

    info = jnp.where(lane_i == R_EA, ia - RL_E, 0.0)
    info = jnp.where(lane_i == R_EB, ib - RL_E, info)
    info = jnp.where(lane_i == R_RA, rank_a, info)
    info = jnp.where(lane_i == R_RB, rank_b, info)
    info = jnp.where(lane_i == R_GA, gp * wa, info)
    info = jnp.where(lane_i == R_GB, gp * wb, info)
    info_ref[...] = info


def _row_tile(n, candidates):
    for t in candidates:
        if n % t == 0:
            return t
    raise ValueError(f"no row tile for {n} rows among {candidates}")


def _router(hmid, rp):
    n, d = hmid.shape
    assert d == TOK_TILE_ROWS * LANES
    tr = _row_tile(n, (512, 384, 256, 128))
    return pl.pallas_call(
        _router_kernel,
        out_shape=(jax.ShapeDtypeStruct((n * TOK_TILE_ROWS, LANES), F32),
                   jax.ShapeDtypeStruct((n, LANES), F32),
                   jax.ShapeDtypeStruct((1, LANES), F32)),
        grid=(n // tr,),
        in_specs=[pl.BlockSpec((tr, d), lambda i: (i, 0)),
                  pl.BlockSpec((1, d), lambda i: (0, 0)),
                  pl.BlockSpec((d, LANES), lambda i: (0, 0)),
                  pl.BlockSpec((d, LANES), lambda i: (0, 0)),
                  pl.BlockSpec((1, LANES), lambda i: (0, 0))],
        out_specs=(pl.BlockSpec((tr * TOK_TILE_ROWS, LANES), lambda i: (i, 0)),
                   pl.BlockSpec((tr, LANES), lambda i: (i, 0)),
                   pl.BlockSpec((1, LANES), lambda i: (0, 0))),
        scratch_shapes=[pltpu.VMEM((1, LANES), F32)],
        compiler_params=pltpu.CompilerParams(
            dimension_semantics=("arbitrary",), vmem_limit_bytes=VMEM_LIMIT),
        name="router",
    )(hmid, rp["nf"], rp["whi"], rp["wmid"], rp["br"])


def _prep_router_params(norm_ffn, w_r1, b_r1, w_r2, b_r2):
    d = w_r1.shape[1]
    w = jnp.concatenate([w_r1.reshape(d, N_EGROUPS).astype(F32), w_r2.reshape(d, N_EXPERTS).astype(F32),
                         jnp.zeros((d, LANES - RL_E - N_EXPERTS), F32)], axis=1)
    whi = w.astype(BF16)
    wmid = (w - whi.astype(F32)).astype(BF16)
    br = jnp.concatenate([b_r1.reshape(1, N_EGROUPS).astype(F32), b_r2.reshape(1, N_EXPERTS).astype(F32),
                          jnp.zeros((1, LANES - RL_E - N_EXPERTS), F32)], axis=1)
    return dict(nf=norm_ffn.reshape(1, d).astype(F32), whi=whi, wmid=wmid, br=br)


FFN_TM = 256
N_GATHER_SLOTS = 3


def _start_tile_gather(first_row_of, n_rows, src_hbm, dst, sem, priority_of):
    for r in range(n_rows):
        start = pl.multiple_of(first_row_of(r), TOK_TILE_ROWS)
        pltpu.make_async_copy(src_hbm.at[pl.ds(start, TOK_TILE_ROWS), :],
                              dst.at[pl.ds(r * TOK_TILE_ROWS, TOK_TILE_ROWS), :],
                              sem).start(priority=priority_of(r))


def _wait_tile_gather(n_rows, src_hbm, dst, sem):
    pltpu.make_async_copy(src_hbm.at[pl.ds(0, n_rows * TOK_TILE_ROWS), :], dst, sem).wait()


TAB_OFF, TAB_CNT, TAB_TILE_EXPERT, TAB_NVALID = 0, 1, 2, 3
TAB_LANES = 2 * LANES


def _ffn_kernel(tab_ref, pa_ref, pb_ref, xn_hbm, wg_ref, wu_ref, wd_ref, ys_ref,
                src, xbuf, wbf, sem, *, n_tokens, tm):
    i = pl.program_id(0)
    n_valid = tab_ref[TAB_NVALID, 0]
    slot = lax.rem(i, N_GATHER_SLOTS)
    gather_priority = lambda r: r % 2

    @pl.when(i == 0)
    def _build_source_rows():
        for e in range(N_EXPERTS):
            cnt_e = tab_ref[TAB_CNT, RL_E + e]
            first = tab_ref[TAB_OFF, RL_E + e] + cnt_e
            n_pad = lax.rem(tm - lax.rem(cnt_e, tm), tm)

            def pad_body(r, carry, first=first):
                src[first + r] = 0
                return carry
            lax.fori_loop(0, n_pad, pad_body, 0)

        def body(t, carry):
            first_row = t * TOK_TILE_ROWS
            src[pa_ref[t]] = first_row
            src[pb_ref[t]] = first_row
            return carry
        lax.fori_loop(0, n_tokens, body, 0, unroll=8)
        _start_tile_gather(lambda r: src[r], tm, xn_hbm, xbuf.at[0], sem.at[0], gather_priority)
        second = jnp.where(n_valid > 1, tm, 0)
        _start_tile_gather(lambda r: src[second + r], tm, xn_hbm, xbuf.at[1], sem.at[1], gather_priority)

    changed = jnp.logical_or(i == 0, tab_ref[TAB_TILE_EXPERT, i]
                             != tab_ref[TAB_TILE_EXPERT, jnp.maximum(i - 1, 0)])

    @pl.when(jnp.logical_and(changed, i < n_valid))
    def _cast_weights():
        wbf[0] = wg_ref[...].astype(BF16)
        wbf[1] = wu_ref[...].astype(BF16)
        wbf[2] = wd_ref[...].astype(BF16)

    @pl.when(i < n_valid)
    def _compute():
        _wait_tile_gather(tm, xn_hbm, xbuf.at[slot], sem.at[slot])
        x = _load_token_tiles(xbuf.at[slot], tm).astype(BF16)
        hg = _dot(x, wbf[0])
        hu = _dot(x, wbf[1])
        y = _dot((_silu(hg) * hu).astype(BF16), wbf[2])
        base = jnp.where(i + 2 < n_valid, i + 2, 0) * tm
        ahead = lax.rem(i + 2, N_GATHER_SLOTS)
        _start_tile_gather(lambda r: src[base + r], tm, xn_hbm, xbuf.at[ahead], sem.at[ahead],
                           gather_priority)
        _store_token_tiles(ys_ref, y)

    @pl.when(i == n_valid - 1)
    def _drain():
        for k in (1, 2):
            s = lax.rem(i + k, N_GATHER_SLOTS)
            _wait_tile_gather(tm, xn_hbm, xbuf.at[s], sem.at[s])

    @pl.when(i >= n_valid)
    def _pad():
        ys_ref[...] = jnp.zeros_like(ys_ref)


def _expert_ffn(xn_tiles, tab, pos_a, pos_b, n_tiles, wg, wu, wd):
    n = pos_a.shape[0]
    tm = FFN_TM
    d, dff = wg.shape[1], wg.shape[2]
    rows = tm * TOK_TILE_ROWS
    idx = lambda i, tab, pa, pb: (tab[TAB_TILE_EXPERT, i], 0, 0)
    grid_spec = pltpu.PrefetchScalarGridSpec(
        num_scalar_prefetch=3,
        grid=(n_tiles,),
        in_specs=[pl.BlockSpec(memory_space=pl.ANY),
                  pl.BlockSpec((None, d, dff), idx),
                  pl.BlockSpec((None, d, dff), idx),
                  pl.BlockSpec((None, dff, d), idx)],
        out_specs=pl.BlockSpec((rows, LANES), lambda i, tab, pa, pb: (i, 0)),
        scratch_shapes=[pltpu.SMEM((n_tiles * tm,), jnp.int32),
                        pltpu.VMEM((N_GATHER_SLOTS, rows, LANES), F32),
                        pltpu.VMEM((3, d, dff), BF16),
                        pltpu.SemaphoreType.DMA((N_GATHER_SLOTS,))],
    )
    return pl.pallas_call(
        functools.partial(_ffn_kernel, n_tokens=n, tm=tm),
        out_shape=jax.ShapeDtypeStruct((n_tiles * rows, LANES), F32),
        grid_spec=grid_spec,
        compiler_params=pltpu.CompilerParams(
            dimension_semantics=("arbitrary",), vmem_limit_bytes=VMEM_LIMIT),
        name="expert_ffn",
    )(tab, pos_a, pos_b, xn_tiles, wg, wu, wd)


def _positions_kernel(info_ref, cnt_ref, pos_ref, tab_ref, *, tm, chunk):
    lane_i = lax.broadcasted_iota(jnp.int32, (1, LANES), 1)
    lane = lane_i.astype(F32)
    is_expert = jnp.logical_and(lane_i >= RL_E, lane_i < RL_E + N_EXPERTS)
    cnt = jnp.where(is_expert, cnt_ref[...], 0.0)
    padded = jnp.floor((cnt + (tm - 1)) / tm) * tm
    ri = lax.broadcasted_iota(jnp.int32, (LANES, LANES), 0)
    ci = lax.broadcasted_iota(jnp.int32, (LANES, LANES), 1)
    before = jnp.where(ri < ci, 1.0, 0.0).astype(BF16)
    hi, mid, lo = _split3(jnp.broadcast_to(padded, (SUBLANES, LANES)))
    off = ((_dot(hi, before) + _dot(mid, before)) + _dot(lo, before))[0:1, :]
    pick = jnp.where(lax.broadcasted_iota(jnp.int32, (SUBLANES, LANES), 0) == lane_i, 1.0, 0.0).astype(BF16)

    total = jnp.sum(padded, axis=-1, keepdims=True)
    tile_row = lax.broadcasted_iota(jnp.int32, (TAB_LANES, 1), 0).astype(F32) * tm
    ends = off + padded
    done = jnp.logical_and(is_expert, ends <= jnp.minimum(tile_row, total - 1.0))
    te_col = jnp.sum(jnp.where(done, 1.0, 0.0), axis=-1, keepdims=True)
    te_rows = _dot_nt(pick, jnp.where(lane_i == 0, te_col, 0.0).astype(BF16))
    tab_ref[...] = jnp.zeros_like(tab_ref)
    tab_ref[TAB_OFF:TAB_OFF + 1, 0:LANES] = off.astype(jnp.int32)
    tab_ref[TAB_CNT:TAB_CNT + 1, 0:LANES] = cnt.astype(jnp.int32)
    tab_ref[TAB_TILE_EXPERT:TAB_TILE_EXPERT + 1, :] = te_rows[0:1, :].astype(jnp.int32)
    tab_ref[TAB_NVALID:TAB_NVALID + 1, 0:LANES] = jnp.broadcast_to(total / tm, (1, LANES)).astype(jnp.int32)

    n = info_ref.shape[0]
    for c0 in range(0, n, chunk):
        blk = info_ref[c0:c0 + chunk, :]
        lane_a = blk[:, R_EA:R_EA + 1] + RL_E
        lane_b = blk[:, R_EB:R_EB + 1] + RL_E
        pos_a = blk[:, R_RA:R_RA + 1] + jnp.sum(jnp.where(lane == lane_a, off, 0.0), axis=-1, keepdims=True)
        pos_b = blk[:, R_RB:R_RB + 1] + jnp.sum(jnp.where(lane == lane_b, off, 0.0), axis=-1, keepdims=True)
        z_hi, z_mid, z_lo = _split3(jnp.where(lane_i == 0, pos_a, jnp.where(lane_i == 1, pos_b, 0.0)))
        rows = (_dot_nt(pick, z_hi) + _dot_nt(pick, z_mid)) + _dot_nt(pick, z_lo)
        pos_ref[:, c0:c0 + chunk] = rows.astype(jnp.int32)


def _routing_tables(info, counts, n_tiles, tm):
    n = info.shape[0]
    groups = n // LANES
    assert n_tiles <= TAB_LANES
    chunk = LANES * max(g for g in range(1, 65) if groups % g == 0)
    pos, tab = pl.pallas_call(
        functools.partial(_positions_kernel, tm=tm, chunk=chunk),
        out_shape=(jax.ShapeDtypeStruct((SUBLANES, n), jnp.int32),
                   jax.ShapeDtypeStruct((SUBLANES, TAB_LANES), jnp.int32)),
        grid=(1,),
        in_specs=[pl.BlockSpec((n, LANES), lambda i: (0, 0)), pl.BlockSpec((1, LANES), lambda i: (0, 0))],
        out_specs=(pl.BlockSpec((SUBLANES, n), lambda i: (0, 0)),
                   pl.BlockSpec((SUBLANES, TAB_LANES), lambda i: (0, 0))),
        compiler_params=pltpu.CompilerParams(
            dimension_semantics=("arbitrary",), vmem_limit_bytes=VMEM_LIMIT),
        name="positions",
    )(info, counts)
    return tab, pos[0], pos[1]


def _combine_kernel(pa_ref, pb_ref, h_ref, info_ref, ys_hbm, nfin_ref, yp_ref, ysm_ref,
                    buf_a, buf_b, sem, *, n_prompt_tiles):
    i = pl.program_id(0)
    n_steps = pl.num_programs(0)
    tt = h_ref.shape[0]
    slot = lax.rem(i, N_GATHER_SLOTS)

    def start(tile, s):
        base = tile * tt
        _start_tile_gather(lambda r: pa_ref[base + r] * TOK_TILE_ROWS, tt, ys_hbm, buf_a.at[s],
                           sem.at[s], lambda r: 0)
        _start_tile_gather(lambda r: pb_ref[base + r] * TOK_TILE_ROWS, tt, ys_hbm, buf_b.at[s],
                           sem.at[s], lambda r: 1)

    @pl.when(i == 0)
    def _first():
        start(0, 0)
        start(lax.rem(1, n_steps), 1)

    _wait_tile_gather(tt, ys_hbm, buf_a.at[slot], sem.at[slot])
    _wait_tile_gather(tt, ys_hbm, buf_b.at[slot], sem.at[slot])
    info = info_ref[...]
    rows_a = _load_token_tiles(buf_a.at[slot], tt)
    rows_b = _load_token_tiles(buf_b.at[slot], tt)
    x = h_ref[...]
    start(lax.rem(i + 2, n_steps), lax.rem(i + 2, N_GATHER_SLOTS))
    h = x + info[:, R_GA:R_GA + 1] * rows_a + info[:, R_GB:R_GB + 1] * rows_b
    y = h * _rms_scale(h) * nfin_ref[...]

    @pl.when(i < n_prompt_tiles)
    def _prompt():
        yp_ref[...] = y

    @pl.when(i >= n_prompt_tiles)
    def _sample():
        ysm_ref[...] = y

    @pl.when(i == n_steps - 1)
    def _drain():
        for k in (1, 2):
            s = lax.rem(i + k, N_GATHER_SLOTS)
            _wait_tile_gather(tt, ys_hbm, buf_a.at[s], sem.at[s])
            _wait_tile_gather(tt, ys_hbm, buf_b.at[s], sem.at[s])


def _combine(hmid, info, ys, pos_a, pos_b, nfin, n_prompt):
    n, d = hmid.shape
    tt = CHUNK
    n_prompt_tiles = n_prompt // tt
    n_sample = n - n_prompt
    grid_spec = pltpu.PrefetchScalarGridSpec(
        num_scalar_prefetch=2,
        grid=(n // tt,),
        in_specs=[pl.BlockSpec((tt, d), lambda i, pa, pb: (i, 0)),
                  pl.BlockSpec((tt, LANES), lambda i, pa, pb: (i, 0)),
                  pl.BlockSpec(memory_space=pl.ANY),
                  pl.BlockSpec((1, d), lambda i, pa, pb: (0, 0))],
        out_specs=(pl.BlockSpec((tt, d), lambda i, pa, pb: (jnp.minimum(i, n_prompt_tiles - 1), 0)),
                   pl.BlockSpec((tt, d), lambda i, pa, pb: (jnp.maximum(i - n_prompt_tiles, 0), 0))),
        scratch_shapes=[pltpu.VMEM((N_GATHER_SLOTS, tt * TOK_TILE_ROWS, LANES), F32),
                        pltpu.VMEM((N_GATHER_SLOTS, tt * TOK_TILE_ROWS, LANES), F32),
                        pltpu.SemaphoreType.DMA((N_GATHER_SLOTS,))],
    )
    return pl.pallas_call(
        functools.partial(_combine_kernel, n_prompt_tiles=n_prompt_tiles),
        out_shape=(jax.ShapeDtypeStruct((n_prompt, d), F32),
                   jax.ShapeDtypeStruct((n_sample, d), F32)),
        grid_spec=grid_spec,
        compiler_params=pltpu.CompilerParams(
            dimension_semantics=("arbitrary",), vmem_limit_bytes=VMEM_LIMIT),
        name="combine",
    )(pos_a, pos_b, hmid, info, ys, nfin)


def _moe_and_final_norm(hmid, n_prompt, rp, wg, wu, wd, nfin):
    n = hmid.shape[0]
    tm = FFN_TM
    n_tiles = (2 * n + N_EXPERTS * (tm - 1)) // tm
    xn, info, counts = _router(hmid, rp)
    tab, pos_a, pos_b = _routing_tables(info, counts, n_tiles, tm)
    ys = _expert_ffn(xn, tab, pos_a, pos_b, n_tiles, wg, wu, wd)
    return _combine(hmid, info, ys, pos_a, pos_b, nfin, n_prompt)


def kernel(x_prompt, x_sample, state_mlstm_C, state_mlstm_n, state_mlstm_m, state_mlstm_conv, state_ssm, state_ssm_conv, meta_tokens, norm_mix, w_in, conv_a_w, conv_a_b, w_q, w_k, w_v, b_i, b_f, norm_a, conv_b_w, conv_b_b, dt_bias, a_log, d_skip, norm_b, w_out, norm_ffn, w_r1, b_r1, w_r2, b_r2, w_gate, w_up, w_down, norm_final):
    bsz, seq, d = x_prompt.shape
    nb = x_sample.shape[0]
    d_a = H_A * DH_A
    conv_b = H_B * HD_B + 2 * G_B * N_STATE
    assert w_in.shape[0] == 1 and x_sample.shape[1] == 1 and seq % CHUNK == 0 and nb == CHUNK
    mp = _prep_mixer_params(norm_mix, w_in, conv_a_w, conv_a_b, w_q, w_k, w_v, b_i, b_f, norm_a,
                            conv_b_w, conv_b_b, dt_bias, a_log, d_skip, norm_b, w_out)
    rp = _prep_router_params(norm_ffn, w_r1, b_r1, w_r2, b_r2)
    xmeta = jnp.concatenate([jnp.zeros((CHUNK - N_META, d), F32), meta_tokens.astype(F32)], 0)

    hmid, p_c, p_n, p_m, p_ca, p_s, p_cb = _prompt_mixer(x_prompt.astype(F32), xmeta, mp, nb)
    m0 = jnp.pad(state_mlstm_m.reshape(nb, H_A).astype(F32), ((0, 0), (0, LANES - H_A)))
    hmid, s_c, s_n, s_m, s_ca, s_s, s_cb = _sample_mixer(
        x_sample.reshape(nb, d).astype(F32),
        state_mlstm_C.reshape(nb, H_A, DH_A, DH_A).astype(F32),
        state_mlstm_n.reshape(nb, d_a).astype(F32),
        m0,
        state_mlstm_conv.reshape(nb, (CONV_W - 1) * d_a).astype(F32),
        state_ssm.reshape(nb, H_B // 2, 2 * HD_B, N_STATE).astype(F32),
        state_ssm_conv.reshape(nb, (CONV_W - 1) * conv_b).astype(F32),
        mp, hmid, bsz * seq)

    wshape = w_gate.shape[1:]
    y_p, y_s = _moe_and_final_norm(
        hmid, bsz * seq, rp, w_gate.reshape(wshape).astype(F32), w_up.reshape(wshape).astype(F32),
        w_down.reshape(w_down.shape[1:]).astype(F32), norm_final.reshape(1, d).astype(F32))

    return (y_p.reshape(bsz, seq, d), y_s.reshape(nb, 1, d),
            p_c.reshape(1, bsz, H_A, DH_A, DH_A), p_n.reshape(1, bsz, H_A, DH_A),
            p_m[:, 0, :H_A].reshape(1, bsz, H_A), p_ca.reshape(1, bsz, CONV_W - 1, d_a),
            p_s.reshape(1, bsz, H_B, HD_B, N_STATE), p_cb.reshape(1, bsz, CONV_W - 1, conv_b),
            s_c.reshape(1, nb, H_A, DH_A, DH_A), s_n.reshape(1, nb, H_A, DH_A),
            s_m[:, :H_A].reshape(1, nb, H_A), s_ca.reshape(1, nb, CONV_W - 1, d_a),
            s_s.reshape(1, nb, H_B, HD_B, N_STATE), s_cb.reshape(1, nb, CONV_W - 1, conv_b))
```

```python
import functools

import jax
import jax.numpy as jnp
from jax import lax
from jax.experimental import pallas as pl
from jax.experimental.pallas import tpu as pltpu

F32 = jnp.float32
BF16 = jnp.bfloat16

EPS = 1e-6
N_META = 16
CONV_W = 4
CHUNK = 128
H_A = 8
DH_A = 128
H_B = 16
HD_B = 64
N_STATE = 128
G_B = 2
N_EGROUPS = 4
N_EPG = 4
N_EXPERTS = 16
LANES = 128
SUBLANES = 8
CONV_HDR = SUBLANES
VMEM_LIMIT = 56 * 1024 * 1024

L_F = 0
L_DTA = 8
L_I = 24
L_DT = 32

NEG_INF = float("-inf")


def _dot(a, b):
    return jnp.dot(a, b, preferred_element_type=F32)


def _dot_nt(a, b):
    return lax.dot_general(a, b, (((1,), (1,)), ((), ())), preferred_element_type=F32)


def _dot_tn(a, b):
    return lax.dot_general(a, b, (((0,), (0,)), ((), ())), preferred_element_type=F32)


def _split3(x):
    hi = x.astype(BF16)
    r = x - hi.astype(F32)
    mid = r.astype(BF16)
    lo = (r - mid.astype(F32)).astype(BF16)
    return hi, mid, lo


def _silu(x):
    return x * jax.nn.sigmoid(x)


def _softplus_parts(x):
    t = jnp.log1p(jnp.exp(-jnp.abs(x)))
    return jnp.maximum(x, 0.0) + t, jnp.minimum(x, 0.0) - t


def _rms_scale(x):
    return lax.rsqrt(jnp.mean(x * x, axis=-1, keepdims=True) + EPS)


TOK_TILE_ROWS = SUBLANES


def _store_token_tiles(ref, x):
    n = x.shape[0]
    for j in range(TOK_TILE_ROWS):
        ref[pl.ds(j, n, stride=TOK_TILE_ROWS), :] = x[:, j * LANES:(j + 1) * LANES]


def _causal_conv(x, tail, w_ref, b_ref):
    n_tail = tail.shape[0]
    row = lax.broadcasted_iota(jnp.int32, (n_tail, 1), 0)
    acc = w_ref[CONV_W - 1:CONV_W, :] * x + b_ref[...]
    for k in range(1, CONV_W):
        rolled = pltpu.roll(x, k, axis=0)
        head = jnp.where(row < k, pltpu.roll(tail, k, axis=0), rolled[0:n_tail])
        shifted = jnp.concatenate([head, rolled[n_tail:]], axis=0)
        acc = acc + w_ref[CONV_W - 1 - k:CONV_W - k, :] * shifted
    return acc


def _load_token_tiles(ref, n):
    return jnp.concatenate(
        [ref[pl.ds(j, n, stride=TOK_TILE_ROWS), :] for j in range(TOK_TILE_ROWS)], axis=1)


PROMPT_ROWS = 2


def _prompt_rows_kernel(xmeta_ref, xp_ref, nmix_ref, wcat_ref, bsm_ref, alog_ref,
                        cwa_ref, cba_ref, cwb_ref, cbb_ref, wq_ref, wk_ref, wv_ref,
                        na_ref, nb_ref, dsk_ref, wout_ref,
                        hmid_hbm, c_ref, n_ref, m_ref, conva_ref, s_ref, convb_ref,
                        xa_buf, xbc_buf, y_buf, merged, hout, sem, *, seq, n_prompt_rows):
    p = pl.program_id(0)
    c = pl.program_id(1)
    last_p = pl.num_programs(0) - 1
    last_c = pl.num_programs(1) - 1
    T = CHUNK
    RB = xp_ref.shape[0]
    d_a = H_A * DH_A
    d_b = H_B * HD_B
    conv_b = d_b + 2 * G_B * N_STATE

    def out_copy(r, row0):
        return pltpu.make_async_copy(hout.at[r], hmid_hbm.at[pl.ds(row0, T), :], sem.at[r])

    @pl.when(c == 0)
    def _init():
        c_ref[...] = jnp.zeros_like(c_ref)
        n_ref[...] = jnp.zeros_like(n_ref)
        m_ref[...] = jnp.zeros_like(m_ref)
        s_ref[...] = jnp.zeros_like(s_ref)
        xa_buf[...] = jnp.zeros_like(xa_buf)
        xbc_buf[...] = jnp.zeros_like(xbc_buf)

    @pl.when(jnp.logical_and(p == 0, c == 0))
    def _clear_sample_rows():
        hout[0] = jnp.zeros((T, hout.shape[2]), F32)
        cp = out_copy(0, n_prompt_rows)
        cp.start()
        cp.wait()

    row = lax.broadcasted_iota(jnp.int32, (T, 1), 0)
    valid = jnp.logical_or(c > 0, row >= T - N_META)
    xs_in = [jnp.where(c == 0, xmeta_ref[...], xp_ref[r]) for r in range(RB)]
    x2 = jnp.concatenate(xs_in, axis=0)
    hn = (x2 * _rms_scale(x2) * nmix_ref[...]).astype(BF16)

    lane = lax.broadcasted_iota(jnp.int32, (1, LANES), 1)
    lane_f = lane < L_DTA
    lane_dta = jnp.logical_and(lane >= L_DTA, lane < L_I)
    lane_i = jnp.logical_and(lane >= L_I, lane < L_DT)
    lane_dt = jnp.logical_and(lane >= L_DT, lane < L_DT + H_B)
    ri = lax.broadcasted_iota(jnp.int32, (T, T), 0)
    ci = lax.broadcasted_iota(jnp.int32, (T, T), 1)
    causal = ri >= ci
    tri = jnp.where(causal, 1.0, 0.0).astype(BF16)
    a_neg = jnp.where(lane_dta, -jnp.exp(alog_ref[...]), 0.0)
    left = lane < HD_B
    top = lax.broadcasted_iota(jnp.int32, (LANES, 1), 0) < HD_B

    off_small = 2 * d_a + d_b + conv_b
    pre2 = _dot(hn, wcat_ref[:, off_small:]) + bsm_ref[...]
    xa2 = _dot(hn, wcat_ref[:, 0:d_a])

    gcols, grows, xcs, xabs = [], [], [], []

    def gate_tables():
        for r in range(RB):
            pre = pre2[r * T:(r + 1) * T]
            sp, lsig = _softplus_parts(pre)
            to_cum = jnp.where(lane_f, lsig, jnp.where(lane_dta, sp * a_neg, 0.0))
            to_cum = jnp.where(valid, to_cum, 0.0)
            hi, mid, lo = _split3(to_cum)
            cum = _dot(tri, hi) + _dot(tri, mid) + _dot(tri, lo)
            extra = jnp.where(lane_i, jnp.where(valid, pre, NEG_INF),
                              jnp.where(lane_dt, jnp.where(valid, sp, 0.0), 0.0))
            gcol = cum + extra
            gcols.append(gcol)
            grows.append(gcol.T)

    gate_tables()
    for r in range(RB):
        rs = slice(r * T, (r + 1) * T)
        xa = xa2[rs]
        xc = _causal_conv(xa, xa_buf[r], cwa_ref, cba_ref)
        xa_buf[r] = xa[T - CONV_HDR:T, :]
        conva_ref[r] = xa[T - 3:T, :]
        xcs.append(_silu(xc).astype(BF16))
        xabs.append(xa.astype(BF16))

    items = [(r, h) for h in range(H_A) for r in range(RB)]
    hsl = lambda h: slice(h * DH_A, (h + 1) * DH_A)
    m_alls = [m_ref[r] for r in range(RB)]
    m_news = list(m_alls)
    qs, ks, vs, qks, st, dd = {}, {}, {}, {}, {}, {}

    def stage_qkv(it):
        r, h = it
        qs[it] = _dot(xcs[r][:, hsl(h)], wq_ref[h]).astype(BF16)
        ks[it] = _dot(xcs[r][:, hsl(h)], wk_ref[h]) * (DH_A ** -0.5)
        vs[it] = _dot(xabs[r][:, hsl(h)], wv_ref[h]).astype(BF16)

    def stage_qk(it):
        qks[it] = _dot_nt(qs[it], ks[it].astype(BF16))

    pairs_per_group = H_B // G_B // 2
    groups = [(r, g) for g in range(G_B) for r in range(RB)]
    proj, xbcs, bgs, cgs, cbs = {}, [], {}, {}, {}

    def project(name, lo, hi):
        proj[name] = _dot(hn, wcat_ref[:, lo:hi])

    def ssd_inputs():
        for r in range(RB):
            xbc = proj["xbc"][r * T:(r + 1) * T]
            xbc_c = _causal_conv(xbc, xbc_buf[r], cwb_ref, cbb_ref)
            xbc_buf[r] = xbc[T - CONV_HDR:T, :]
            convb_ref[r] = xbc[T - 3:T, :]
            xbcs.append(_silu(xbc_c))
        for r, g in groups:
            bgs[(r, g)] = xbcs[r][:, d_b + g * N_STATE:d_b + (g + 1) * N_STATE].astype(BF16)
            cgs[(r, g)] = xbcs[r][:, d_b + (G_B + g) * N_STATE:d_b + (G_B + g + 1) * N_STATE].astype(BF16)
            cbs[(r, g)] = _dot_nt(cgs[(r, g)], bgs[(r, g)])

    def stage_weights(it):
        r, h = it
        gcol, grow = gcols[r], grows[r]
        b_col = gcol[:, L_F + h:L_F + h + 1]
        i_col = gcol[:, L_I + h:L_I + h + 1]
        b_row = grow[L_F + h:L_F + h + 1, :]
        i_row = grow[L_I + h:L_I + h + 1, :]
        m0 = m_alls[r][:, h:h + 1]
        dmat = jnp.where(causal, b_col - (b_row - i_row), NEG_INF)
        m_inter = b_col + m0
        m = jnp.maximum(m_inter, jnp.max(dmat, axis=-1, keepdims=True))
        w_inter = jnp.exp(m_inter - m)
        s = qks[it] * jnp.exp(dmat - m)
        n0 = n_ref[r, h:h + 1, :]
        den = (jnp.sum(s, axis=-1, keepdims=True)
               + w_inter * jnp.sum(qs[it].astype(F32) * n0, axis=-1, keepdims=True))
        m_last = m[T - 1:T, :]
        b_last = b_col[T - 1:T, :]
        dec = jnp.exp(b_last + m0 - m_last)
        kw = ks[it] * jnp.exp(b_last - b_col + i_col - m_last)
        n_ref[r, h:h + 1, :] = dec * n0 + jnp.sum(kw, axis=0, keepdims=True)
        m_news[r] = jnp.where(lane == h, m_last, m_news[r])
        st[it] = (s.astype(BF16), kw.astype(BF16), w_inter,
                  jnp.maximum(jnp.abs(den), jnp.exp(-m)), dec)

    def stage_readout(it):
        r, h = it
        s_b, kw_b, w_inter, den, dec = st[it]
        c0 = c_ref[r, h]
        num = _dot(s_b, vs[it]) + w_inter * _dot(qs[it], c0.astype(BF16))
        c_ref[r, h] = dec * c0 + _dot_tn(kw_b, vs[it])
        dd[it] = num / den

    def stage_head_out(it):
        r, h = it
        hh = dd[it]
        hh = hh * _rms_scale(hh) * na_ref[:, hsl(h)]
        merged[r * T:(r + 1) * T, hsl(h)] = (
            hh * jax.nn.sigmoid(proj["za"][r * T:(r + 1) * T, hsl(h)])).astype(BF16)

    pairs = [(r, pi) for pi in range(H_B // 2) for r in range(RB)]
    psl = lambda pi: slice(pi * LANES, (pi + 1) * LANES)
    sw = {}

    def stage_decay(pr):
        r, pi = pr
        g = pi // pairs_per_group
        gcol, grow = gcols[r], grows[r]
        xpair = xbcs[r][:, psl(pi)]
        scs, a_cols, w_cols, a_lasts = [], [], [], []
        for j in (2 * pi, 2 * pi + 1):
            a_col = gcol[:, L_DTA + j:L_DTA + j + 1]
            a_row = grow[L_DTA + j:L_DTA + j + 1, :]
            dt_col = gcol[:, L_DT + j:L_DT + j + 1]
            dt_row = grow[L_DT + j:L_DT + j + 1, :]
            decay = jnp.exp(jnp.where(causal, a_col - a_row, NEG_INF))
            scs.append((cbs[(r, g)] * decay * dt_row).astype(BF16))
            a_last = a_col[T - 1:T, :]
            a_cols.append(a_col)
            a_lasts.append(a_last)
            w_cols.append(jnp.exp(a_last - a_col) * dt_col)
        sw[pr] = (scs, xpair.astype(BF16),
                  (xpair * jnp.where(left, w_cols[0], w_cols[1])).astype(BF16),
                  jnp.exp(jnp.where(left, a_cols[0], a_cols[1])),
                  jnp.exp(jnp.where(top, a_lasts[0], a_lasts[1])))
    def stage_pair_out(pr):
        r, pi = pr
        g = pi // pairs_per_group
        scs, xpb, xw, ea, ea_last = sw[pr]
        s0 = s_ref[r, pi]
        y = jnp.where(left, _dot(scs[0], xpb), _dot(scs[1], xpb))
        y = y + ea * _dot_nt(cgs[(r, g)], s0.astype(BF16))
        s_ref[r, pi] = ea_last * s0 + _dot_tn(xw, bgs[(r, g)])
        y = y + dsk_ref[:, psl(pi)] * xbcs[r][:, psl(pi)]
        y_buf[r, :, psl(pi)] = y * _silu(proj["zb"][r * T:(r + 1) * T, psl(pi)])

    def each(stage, seq):
        for e in seq:
            stage(e)

    each(stage_qkv, items)
    project("xbc", 2 * d_a + d_b, off_small)
    each(stage_qk, items)
    project("za", d_a, 2 * d_a)
    ssd_inputs()
    each(stage_weights, items)
    each(stage_readout, items)
    project("zb", 2 * d_a, 2 * d_a + d_b)
    for r in range(RB):
        m_ref[r] = m_news[r]
    each(stage_head_out, items)
    each(stage_decay, pairs)
    each(stage_pair_out, pairs)
    gw = d_b // G_B
    for r in range(RB):
        for g in range(G_B):
            yg = y_buf[r, :, g * gw:(g + 1) * gw]
            merged[r * T:(r + 1) * T, d_a + g * gw:d_a + (g + 1) * gw] = (
                yg * _rms_scale(yg) * nb_ref[:, g * gw:(g + 1) * gw]).astype(BF16)

    @pl.when(c > 0)
    def _out():
        out2 = x2 + _dot(merged[...], wout_ref[...])

        @pl.when(jnp.logical_or(c > 1, p > 0))
        def _wait_previous():
            for r in range(RB):
                out_copy(r, 0).wait()

        for r in range(RB):
            hout[r] = out2[r * T:(r + 1) * T]
            out_copy(r, (p * RB + r) * seq + (c - 1) * T).start()

        @pl.when(jnp.logical_and(p == last_p, c == last_c))
        def _drain():
            for r in range(RB):
                out_copy(r, 0).wait()


def _const_spec(shape):
    nd = len(shape)
    return pl.BlockSpec(shape, lambda b, c, _nd=nd: (0,) * _nd)


def _prompt_mixer(x_prompt, xmeta, p, n_extra_rows):
    bsz, seq, d = x_prompt.shape
    assert n_extra_rows == CHUNK and seq % CHUNK == 0
    n_chunks = seq // CHUNK + 1
    cps = seq // CHUNK
    d_a = H_A * DH_A
    conv_b = H_B * HD_B + 2 * G_B * N_STATE
    consts = [p["nmix"], p["wcat"], p["bsm"], p["alog"], p["cwa"], p["cba"], p["cwb"], p["cbb"],
              p["wq"], p["wk"], p["wv"], p["na"], p["nb"], p["dsk"], p["wout"]]
    rb = PROMPT_ROWS
    assert bsz % rb == 0
    in_specs = [_const_spec(xmeta.shape),
                pl.BlockSpec((rb, CHUNK, d), lambda b, c: (b, jnp.maximum(c - 1, 0), 0))]
    in_specs += [_const_spec(a.shape) for a in consts]
    out_shape = (
        jax.ShapeDtypeStruct((bsz * seq + n_extra_rows, d), F32),
        jax.ShapeDtypeStruct((bsz, H_A, DH_A, DH_A), F32),
        jax.ShapeDtypeStruct((bsz, H_A, DH_A), F32),
        jax.ShapeDtypeStruct((bsz, 1, LANES), F32),
        jax.ShapeDtypeStruct((bsz, CONV_W - 1, d_a), F32),
        jax.ShapeDtypeStruct((bsz, H_B // 2, 2 * HD_B, N_STATE), F32),
        jax.ShapeDtypeStruct((bsz, CONV_W - 1, conv_b), F32),
    )
    out_specs = (
        pl.BlockSpec(memory_space=pl.ANY),
        pl.BlockSpec((rb, H_A, DH_A, DH_A), lambda b, c: (b, 0, 0, 0)),
        pl.BlockSpec((rb, H_A, DH_A), lambda b, c: (b, 0, 0)),
        pl.BlockSpec((rb, 1, LANES), lambda b, c: (b, 0, 0)),
        pl.BlockSpec((rb, CONV_W - 1, d_a), lambda b, c: (b, 0, 0)),
        pl.BlockSpec((rb, H_B // 2, 2 * HD_B, N_STATE), lambda b, c: (b, 0, 0, 0)),
        pl.BlockSpec((rb, CONV_W - 1, conv_b), lambda b, c: (b, 0, 0)),
    )
    return pl.pallas_call(
        functools.partial(_prompt_rows_kernel, seq=seq, n_prompt_rows=bsz * seq),
        out_shape=out_shape,
        grid=(bsz // rb, n_chunks),
        in_specs=in_specs,
        out_specs=out_specs,
        scratch_shapes=[
            pltpu.VMEM((rb, CONV_HDR, d_a), F32),
            pltpu.VMEM((rb, CONV_HDR, conv_b), F32),
            pltpu.VMEM((rb, CHUNK, H_B * HD_B), F32),
            pltpu.VMEM((rb * CHUNK, d_a + H_B * HD_B), BF16),
            pltpu.VMEM((rb, CHUNK, d), F32),
            pltpu.SemaphoreType.DMA((rb,)),
        ],
        compiler_params=pltpu.CompilerParams(
            dimension_semantics=("arbitrary", "arbitrary"), vmem_limit_bytes=VMEM_LIMIT),
        name="prompt_mixer",
    )(xmeta, x_prompt, *consts)


def _regroup_w_in_kernel(w_ref, o_ref):
    d_a = H_A * DH_A
    d_b = H_B * HD_B
    conv_b = d_b + 2 * G_B * N_STATE
    o_i = 2 * d_a
    o_f = o_i + H_A
    o_zb = o_f + H_A
    o_xbc = o_zb + d_b
    o_dt = o_xbc + conv_b
    rows = w_ref.shape[0]
    o_ref[:, 0:2 * d_a] = w_ref[:, 0:2 * d_a].astype(BF16)
    o_ref[:, 2 * d_a:2 * d_a + d_b] = w_ref[:, o_zb:o_zb + d_b].astype(BF16)
    o_ref[:, 2 * d_a + d_b:2 * d_a + d_b + conv_b] = w_ref[:, o_xbc:o_xbc + conv_b].astype(BF16)
    small = jnp.concatenate(
        [w_ref[:, o_f:o_f + H_A], w_ref[:, o_dt:o_dt + H_B], w_ref[:, o_i:o_i + H_A],
         w_ref[:, o_dt:o_dt + H_B], jnp.zeros((rows, LANES - (L_DT + H_B)), F32)], axis=1)
    o_ref[:, 2 * d_a + d_b + conv_b:] = small.astype(BF16)


def _prep_mixer_params(norm_mix, w_in, conv_a_w, conv_a_b, w_q, w_k, w_v, b_i, b_f, norm_a,
                       conv_b_w, conv_b_b, dt_bias, a_log, d_skip, norm_b, w_out):
    d_a = H_A * DH_A
    d_b = H_B * HD_B
    conv_b = d_b + 2 * G_B * N_STATE
    d_model, d_in = w_in.shape[1], w_in.shape[2]
    n_cols = 2 * d_a + d_b + conv_b + LANES
    rows = 256
    assert w_in.shape[0] == 1 and d_model % rows == 0
    wcat = pl.pallas_call(
        _regroup_w_in_kernel,
        out_shape=jax.ShapeDtypeStruct((d_model, n_cols), BF16),
        grid=(d_model // rows,),
        in_specs=[pl.BlockSpec((None, rows, d_in), lambda i: (0, i, 0))],
        out_specs=pl.BlockSpec((rows, n_cols), lambda i: (i, 0)),
        compiler_params=pltpu.CompilerParams(
            dimension_semantics=("arbitrary",), vmem_limit_bytes=VMEM_LIMIT),
        name="regroup_w_in",
    )(w_in.astype(F32))

    def lanes(parts):
        pieces, at = [], 0
        for off, a in parts:
            pieces += [jnp.zeros((1, off - at), F32), a.astype(F32)]
            at = off + a.shape[1]
        return jnp.concatenate(pieces + [jnp.zeros((1, LANES - at), F32)], axis=1)

    return dict(
        nmix=norm_mix.reshape(1, -1).astype(F32),
        wcat=wcat,
        bsm=lanes([(L_F, b_f), (L_DTA, dt_bias), (L_I, b_i), (L_DT, dt_bias)]),
        alog=lanes([(L_DTA, a_log)]),
        cwa=conv_a_w.reshape(CONV_W, d_a).astype(F32), cba=conv_a_b.reshape(1, d_a).astype(F32),
        cwb=conv_b_w.reshape(CONV_W, conv_b).astype(F32), cbb=conv_b_b.reshape(1, conv_b).astype(F32),
        wq=w_q.reshape(H_A, DH_A, DH_A).astype(BF16), wk=w_k.reshape(H_A, DH_A, DH_A).astype(BF16),
        wv=w_v.reshape(H_A, DH_A, DH_A).astype(BF16),
        na=norm_a.reshape(1, d_a).astype(F32), nb=norm_b.reshape(1, d_b).astype(F32),
        dsk=jnp.repeat(d_skip.reshape(H_B).astype(F32), HD_B)[None, :],
        wout=w_out.reshape(d_a + d_b, -1).astype(BF16),
    )


SAMPLE_BLOCK = 8


def _expand_lanes(vals, first_lane, n_heads, width):
    r = lax.broadcasted_iota(jnp.int32, (LANES, n_heads * width), 0) - first_lane
    c = lax.broadcasted_iota(jnp.int32, (LANES, n_heads * width), 1)
    sel = jnp.logical_and(c >= r * width, c < (r + 1) * width)
    e = jnp.where(sel, 1.0, 0.0).astype(BF16)
    hi, mid, lo = _split3(vals)
    return (_dot(hi, e) + _dot(mid, e)) + _dot(lo, e)


def _sample_pre_kernel(x_ref, nmix_ref, wcat_ref, bsm_ref, alog_ref, cwa_ref, cba_ref, cwb_ref, cbb_ref,
                       wq_ref, wk_ref, wv_ref, dsk_ref, conva_ref, convb_ref, n0_ref, m0_ref,
                       conva_out, convb_out, n1_out, m1_out, g_out, qt_out, kwt_out, xwt_out,
                       v_out, bc_out, a1_out, w1_out, den_out, y1_out, ea_out, zbs_out, zas_out):
    d_a = H_A * DH_A
    d_b = H_B * HD_B
    conv_b = d_b + 2 * G_B * N_STATE
    shift_i = LANES - (L_I - L_F)
    x = x_ref[...]
    hn = (x * _rms_scale(x) * nmix_ref[...]).astype(BF16)
    lane = lax.broadcasted_iota(jnp.int32, (1, LANES), 1)
    lane_f = lane < L_DTA
    lane_dta = jnp.logical_and(lane >= L_DTA, lane < L_I)
    pre = _dot(hn, wcat_ref[:, 2 * d_a + d_b + conv_b:]) + bsm_ref[...]
    sp, lsig = _softplus_parts(pre)
    a_neg = jnp.where(lane_dta, -jnp.exp(alog_ref[...]), 0.0)
    pre_al = pltpu.roll(pre, shift_i, axis=1)
    sp_al = pltpu.roll(sp, shift_i, axis=1)
    m_inter = lsig + m0_ref[...]
    m = jnp.maximum(m_inter, pre_al)
    w_inter = jnp.exp(m_inter - m)
    sfac = jnp.exp(pre_al - m)
    ea = jnp.exp(sp * a_neg)
    dt = sp_al

    xa = _dot(hn, wcat_ref[:, 0:d_a])
    xc = (cwa_ref[0:1, :] * conva_ref[:, 0:d_a] + cwa_ref[1:2, :] * conva_ref[:, d_a:2 * d_a]
          + cwa_ref[2:3, :] * conva_ref[:, 2 * d_a:3 * d_a] + cwa_ref[3:4, :] * xa + cba_ref[...])
    conva_out[:, 0:2 * d_a] = conva_ref[:, d_a:3 * d_a]
    conva_out[:, 2 * d_a:3 * d_a] = xa
    xc = _silu(xc).astype(BF16)
    xab = xa.astype(BF16)
    sf_e = _expand_lanes(sfac, L_F, H_A, DH_A)
    w_e = _expand_lanes(w_inter, L_F, H_A, DH_A)
    qk8 = jnp.zeros((x.shape[0], LANES), F32)
    qn8 = jnp.zeros((x.shape[0], LANES), F32)
    for h in range(H_A):
        sl = slice(h * DH_A, (h + 1) * DH_A)
        q = _dot(xc[:, sl], wq_ref[h])
        k = _dot(xc[:, sl], wk_ref[h]) * (DH_A ** -0.5)
        v = _dot(xab[:, sl], wv_ref[h])
        kw = k * sf_e[:, sl]
        qk8 = jnp.where(lane == h, jnp.sum(q * k, axis=-1, keepdims=True), qk8)
        qn8 = jnp.where(lane == h, jnp.sum(q * n0_ref[:, sl], axis=-1, keepdims=True), qn8)
        n1_out[:, sl] = w_e[:, sl] * n0_ref[:, sl] + kw
        v_out[:, sl] = v
        qt_out[h] = q.T
        kwt_out[h] = kw.T
    s8 = qk8 * sfac
    a1_out[...] = _expand_lanes(s8, L_F, H_A, DH_A) * v_out[...]
    w1_out[...] = w_e
    den_out[...] = jnp.maximum(jnp.abs(_expand_lanes(s8 + w_inter * qn8, L_F, H_A, DH_A)),
                               jnp.exp(-_expand_lanes(m, L_F, H_A, DH_A)))
    m1_out[...] = m
    g_out[...] = jnp.where(lane_f, w_inter, jnp.where(lane_dta, ea, 0.0))
    zas_out[...] = jax.nn.sigmoid(_dot(hn, wcat_ref[:, d_a:2 * d_a]))

    off_xbc = 2 * d_a + d_b
    xbc = _dot(hn, wcat_ref[:, off_xbc:off_xbc + conv_b])
    xbc_c = (cwb_ref[0:1, :] * convb_ref[:, 0:conv_b] + cwb_ref[1:2, :] * convb_ref[:, conv_b:2 * conv_b]
             + cwb_ref[2:3, :] * convb_ref[:, 2 * conv_b:3 * conv_b] + cwb_ref[3:4, :] * xbc + cbb_ref[...])
    convb_out[:, 0:2 * conv_b] = convb_ref[:, conv_b:3 * conv_b]
    convb_out[:, 2 * conv_b:3 * conv_b] = xbc
    xbc_c = _silu(xbc_c)
    xs = xbc_c[:, 0:d_b]
    bc = xbc_c[:, d_b:conv_b]
    bc_out[...] = bc
    heads_per_group = H_B // G_B
    cbl = jnp.zeros((x.shape[0], LANES), F32)
    for g in range(G_B):
        cb_g = jnp.sum(bc[:, g * N_STATE:(g + 1) * N_STATE]
                       * bc[:, (G_B + g) * N_STATE:(G_B + g + 1) * N_STATE], axis=-1, keepdims=True)
        in_g = jnp.logical_and(lane >= L_DTA + g * heads_per_group,
                               lane < L_DTA + (g + 1) * heads_per_group)
        cbl = jnp.where(in_g, cb_g, cbl)
    dt_e = _expand_lanes(dt, L_DTA, H_B, HD_B)
    y1_out[...] = _expand_lanes(cbl * dt, L_DTA, H_B, HD_B) * xs + dsk_ref[...] * xs
    ea_out[...] = _expand_lanes(ea, L_DTA, H_B, HD_B)
    zbs_out[...] = _silu(_dot(hn, wcat_ref[:, 2 * d_a:2 * d_a + d_b]))
    xw = xs * dt_e
    for pi in range(H_B // 2):
        xwt_out[pi] = xw[:, pi * LANES:(pi + 1) * LANES].T


def _sample_state_kernel(g_ref, c0_ref, s0_ref, qt_ref, kwt_ref, xwt_ref, v_ref, bc_ref,
                         c1_ref, s1_ref, qc_ref, ysi_ref):
    i = pl.program_id(0)
    bb = c0_ref.shape[0]
    shift = lax.rem(LANES - lax.rem(i * bb, LANES), LANES)
    lane = lax.broadcasted_iota(jnp.int32, (1, LANES), 1)
    top = lax.broadcasted_iota(jnp.int32, (LANES, 1), 0) < HD_B
    heads_per_group = H_B // G_B
    for h in range(H_A):
        sl = slice(h * DH_A, (h + 1) * DH_A)
        qt = pltpu.roll(qt_ref[h], shift, axis=1)
        kwt = pltpu.roll(kwt_ref[h], shift, axis=1)
        for r in range(bb):
            b = i * bb + r
            c0 = c0_ref[r, h]
            dec = g_ref[b, L_F + h]
            v_row = v_ref[r:r + 1, sl]
            qc_ref[r:r + 1, sl] = jnp.sum(c0 * qt[:, r:r + 1], axis=0, keepdims=True)
            c1_ref[r, h] = dec * c0 + kwt[:, r:r + 1] * v_row
    for pi in range(H_B // 2):
        g = (2 * pi) // heads_per_group
        sl = slice(pi * LANES, (pi + 1) * LANES)
        xwt = pltpu.roll(xwt_ref[pi], shift, axis=1)
        acc = jnp.zeros((LANES, LANES), F32)
        for r in range(bb):
            b = i * bb + r
            s0 = s0_ref[r, pi]
            b_row = bc_ref[r:r + 1, g * N_STATE:(g + 1) * N_STATE]
            c_row = bc_ref[r:r + 1, (G_B + g) * N_STATE:(G_B + g + 1) * N_STATE]
            col = jnp.sum(s0 * c_row, axis=-1, keepdims=True)
            acc = jnp.where(lane == r, col, acc)
            ea_rows = jnp.where(top, g_ref[b, L_DTA + 2 * pi], g_ref[b, L_DTA + 2 * pi + 1])
            s1_ref[r, pi] = ea_rows * s0 + xwt[:, r:r + 1] * b_row
        ysi_ref[:, sl] = acc.T[0:bb, :]


def _sample_post_kernel(a1_ref, w1_ref, den_ref, y1_ref, ea_ref, zbs_ref, zas_ref, x_ref, qc_ref, ysi_ref,
                        na_ref, nb_ref, wout_ref, hall_ref, hmid_ref, merged):
    del hall_ref
    d_a = H_A * DH_A
    d_b = H_B * HD_B
    hh = (a1_ref[...] + w1_ref[...] * qc_ref[...]) / den_ref[...]
    for h in range(H_A):
        sl = slice(h * DH_A, (h + 1) * DH_A)
        hs = hh[:, sl]
        merged[:, sl] = (hs * _rms_scale(hs) * na_ref[:, sl] * zas_ref[:, sl]).astype(BF16)
    y = (y1_ref[...] + ea_ref[...] * ysi_ref[...]) * zbs_ref[...]
    gw = d_b // G_B
    for g in range(G_B):
        yg = y[:, g * gw:(g + 1) * gw]
        merged[:, d_a + g * gw:d_a + (g + 1) * gw] = (
            yg * _rms_scale(yg) * nb_ref[:, g * gw:(g + 1) * gw]).astype(BF16)
    hmid_ref[...] = x_ref[...] + _dot(merged[...], wout_ref[...])


def _vmem_specs(arrays):
    return [pl.BlockSpec(a.shape, lambda *_, _nd=a.ndim: (0,) * _nd) for a in arrays]


def _sample_mixer(x, c0, n0, m0, conva, s0, convb, p, hmid_all, row_offset):
    nb, d = x.shape
    d_a = H_A * DH_A
    d_b = H_B * HD_B
    conv_b = d_b + 2 * G_B * N_STATE
    row = lambda w: jax.ShapeDtypeStruct((nb, w), F32)
    tile = lambda k: jax.ShapeDtypeStruct((k, LANES, nb), F32)
    pre_in = [x, p["nmix"], p["wcat"], p["bsm"], p["alog"], p["cwa"], p["cba"], p["cwb"], p["cbb"],
              p["wq"], p["wk"], p["wv"], p["dsk"], conva, convb, n0, m0]
    pre_out_shape = (row(3 * d_a), row(3 * conv_b), row(d_a), row(LANES), row(LANES),
                     tile(H_A), tile(H_A), tile(H_B // 2), row(d_a), row(2 * G_B * N_STATE),
                     row(d_a), row(d_a), row(d_a), row(d_b), row(d_b), row(d_b), row(d_a))
    (conva1, convb1, n1, m1, g8, qt, kwt, xwt, v, bc, a1, w1, den, y1, ea_e, zbs, zas) = pl.pallas_call(
        _sample_pre_kernel,
        out_shape=pre_out_shape,
        grid=(1,),
        in_specs=_vmem_specs(pre_in),
        out_specs=tuple(pl.BlockSpec(s.shape, lambda i, _nd=len(s.shape): (0,) * _nd) for s in pre_out_shape),
        compiler_params=pltpu.CompilerParams(
            dimension_semantics=("arbitrary",), vmem_limit_bytes=VMEM_LIMIT),
        name="sample_pre",
    )(*pre_in)

    bb = SAMPLE_BLOCK
    const3 = lambda k: pl.BlockSpec((k, LANES, nb), lambda i, g: (0, 0, 0))
    state_grid = pltpu.PrefetchScalarGridSpec(
        num_scalar_prefetch=1,
        grid=(nb // bb,),
        in_specs=[pl.BlockSpec((bb, H_A, DH_A, DH_A), lambda i, g: (i, 0, 0, 0)),
                  pl.BlockSpec((bb, H_B // 2, 2 * HD_B, N_STATE), lambda i, g: (i, 0, 0, 0)),
                  const3(H_A), const3(H_A), const3(H_B // 2),
                  pl.BlockSpec((bb, d_a), lambda i, g: (i, 0)),
                  pl.BlockSpec((bb, 2 * G_B * N_STATE), lambda i, g: (i, 0))],
        out_specs=(pl.BlockSpec((bb, H_A, DH_A, DH_A), lambda i, g: (i, 0, 0, 0)),
                   pl.BlockSpec((bb, H_B // 2, 2 * HD_B, N_STATE), lambda i, g: (i, 0, 0, 0)),
                   pl.BlockSpec((bb, d_a), lambda i, g: (i, 0)),
                   pl.BlockSpec((bb, d_b), lambda i, g: (i, 0))),
    )
    c1, s1, qc, ysi = pl.pallas_call(
        _sample_state_kernel,
        out_shape=(jax.ShapeDtypeStruct(c0.shape, F32), jax.ShapeDtypeStruct(s0.shape, F32),
                   row(d_a), row(d_b)),
        grid_spec=state_grid,
        compiler_params=pltpu.CompilerParams(
            dimension_semantics=("arbitrary",), vmem_limit_bytes=VMEM_LIMIT),
        name="sample_state",
    )(g8, c0, s0, qt, kwt, xwt, v, bc)

    post_in = [a1, w1, den, y1, ea_e, zbs, zas, x, qc, ysi, p["na"], p["nb"], p["wout"]]
    hmid_all = pl.pallas_call(
        _sample_post_kernel,
        out_shape=jax.ShapeDtypeStruct(hmid_all.shape, F32),
        grid=(1,),
        in_specs=_vmem_specs(post_in) + [pl.BlockSpec(memory_space=pl.ANY)],
        out_specs=pl.BlockSpec((nb, d), lambda i: (row_offset // nb, 0)),
        scratch_shapes=[pltpu.VMEM((nb, d_a + d_b), BF16)],
        input_output_aliases={len(post_in): 0},
        compiler_params=pltpu.CompilerParams(
            dimension_semantics=("arbitrary",), vmem_limit_bytes=VMEM_LIMIT),
        name="sample_post",
    )(*post_in, hmid_all)
    return hmid_all, c1, n1, m1, conva1, s1, convb1


R_EA, R_EB, R_RA, R_RB, R_GA, R_GB = 0, 1, 2, 3, 4, 5
RL_E = N_EGROUPS


def _router_kernel(h_ref, nf_ref, whi_ref, wmid_ref, br_ref, xn_ref, info_ref, cnt_ref, carry):
    i = pl.program_id(0)
    tr = h_ref.shape[0]

    @pl.when(i == 0)
    def _init():
        carry[...] = jnp.zeros_like(carry)

    h = h_ref[...]
    xn = h * _rms_scale(h) * nf_ref[...]
    _store_token_tiles(xn_ref, xn)
    x_hi, x_mid, _ = _split3(xn)
    logits = (_dot(x_hi, whi_ref[...]) + _dot(x_hi, wmid_ref[...]) + _dot(x_mid, whi_ref[...])
              + br_ref[...])
    lane_i = lax.broadcasted_iota(jnp.int32, (1, LANES), 1)
    lane = lane_i.astype(F32)
    big = float(LANES)

    def first_lane_of(cond):
        return jnp.min(jnp.where(cond, lane, big), axis=-1, keepdims=True)

    l1 = jnp.where(lane_i < N_EGROUPS, logits, NEG_INF)
    e1 = jnp.exp(l1 - jnp.max(l1, axis=-1, keepdims=True))
    p1 = e1 / jnp.sum(e1, axis=-1, keepdims=True)
    gp = jnp.max(p1, axis=-1, keepdims=True)
    gidx = first_lane_of(p1 == gp)
    lo = RL_E + N_EPG * gidx
    l2 = jnp.where(jnp.logical_and(lane >= lo, lane < lo + N_EPG), logits, NEG_INF)
    va = jnp.max(l2, axis=-1, keepdims=True)
    ia = first_lane_of(l2 == va)
    l2b = jnp.where(lane == ia, NEG_INF, l2)
    vb = jnp.max(l2b, axis=-1, keepdims=True)
    ib = first_lane_of(l2b == vb)
    eb = jnp.exp(vb - va)
    wa = 1.0 / (1.0 + eb)
    wb = eb / (1.0 + eb)

    is_a = lane == ia
    is_b = lane == ib
    onehot = jnp.where(jnp.logical_or(is_a, is_b), 1.0, 0.0)
    ri = lax.broadcasted_iota(jnp.int32, (tr, tr), 0)
    ci = lax.broadcasted_iota(jnp.int32, (tr, tr), 1)
    tri = jnp.where(ri >= ci, 1.0, 0.0).astype(BF16)
    incl = _dot(tri, onehot.astype(BF16))
    excl = incl - onehot + carry[...]
    rank_a = jnp.sum(jnp.where(is_a, excl, 0.0), axis=-1, keepdims=True)
    rank_b = jnp.sum(jnp.where(is_b, excl, 0.0), axis=-1, keepdims=True)
    carry[...] = carry[...] + incl[tr - 1:tr, :]
    cnt_ref[...] = carry[...]

    info = jnp.where(lane_i == R_EA, ia - RL_E, 0.0)
    info = jnp.where(lane_i == R_EB, ib - RL_E, info)
    info = jnp.where(lane_i == R_RA, rank_a, info)
    info = jnp.where(lane_i == R_RB, rank_b, info)
    info = jnp.where(lane_i == R_GA, gp * wa, info)
    info = jnp.where(lane_i == R_GB, gp * wb, info)
    info_ref[...] = info


def _row_tile(n, candidates):
    for t in candidates:
        if n % t == 0:
            return t
    raise ValueError(f"no row tile for {n} rows among {candidates}")


def _router(hmid, rp):
    n, d = hmid.shape
    assert d == TOK_TILE_ROWS * LANES
    tr = _row_tile(n, (512, 384, 256, 128))
    return pl.pallas_call(
        _router_kernel,
        out_shape=(jax.ShapeDtypeStruct((n * TOK_TILE_ROWS, LANES), F32),
                   jax.ShapeDtypeStruct((n, LANES), F32),
                   jax.ShapeDtypeStruct((1, LANES), F32)),
        grid=(n // tr,),
        in_specs=[pl.BlockSpec((tr, d), lambda i: (i, 0)),
                  pl.BlockSpec((1, d), lambda i: (0, 0)),
                  pl.BlockSpec((d, LANES), lambda i: (0, 0)),
                  pl.BlockSpec((d, LANES), lambda i: (0, 0)),
                  pl.BlockSpec((1, LANES), lambda i: (0, 0))],
        out_specs=(pl.BlockSpec((tr * TOK_TILE_ROWS, LANES), lambda i: (i, 0)),
                   pl.BlockSpec((tr, LANES), lambda i: (i, 0)),
                   pl.BlockSpec((1, LANES), lambda i: (0, 0))),
        scratch_shapes=[pltpu.VMEM((1, LANES), F32)],
        compiler_params=pltpu.CompilerParams(
            dimension_semantics=("arbitrary",), vmem_limit_bytes=VMEM_LIMIT),
        name="router",
    )(hmid, rp["nf"], rp["whi"], rp["wmid"], rp["br"])


def _prep_router_params(norm_ffn, w_r1, b_r1, w_r2, b_r2):
    d = w_r1.shape[1]
    w = jnp.concatenate([w_r1.reshape(d, N_EGROUPS).astype(F32), w_r2.reshape(d, N_EXPERTS).astype(F32),
                         jnp.zeros((d, LANES - RL_E - N_EXPERTS), F32)], axis=1)
    whi = w.astype(BF16)
    wmid = (w - whi.astype(F32)).astype(BF16)
    br = jnp.concatenate([b_r1.reshape(1, N_EGROUPS).astype(F32), b_r2.reshape(1, N_EXPERTS).astype(F32),
                          jnp.zeros((1, LANES - RL_E - N_EXPERTS), F32)], axis=1)
    return dict(nf=norm_ffn.reshape(1, d).astype(F32), whi=whi, wmid=wmid, br=br)


FFN_TM = 256
N_GATHER_SLOTS = 3


def _start_tile_gather(first_row_of, n_rows, src_hbm, dst, sem, priority_of):
    for r in range(n_rows):
        start = pl.multiple_of(first_row_of(r), TOK_TILE_ROWS)
        pltpu.make_async_copy(src_hbm.at[pl.ds(start, TOK_TILE_ROWS), :],
                              dst.at[pl.ds(r * TOK_TILE_ROWS, TOK_TILE_ROWS), :],
                              sem).start(priority=priority_of(r))


def _wait_tile_gather(n_rows, src_hbm, dst, sem):
    pltpu.make_async_copy(src_hbm.at[pl.ds(0, n_rows * TOK_TILE_ROWS), :], dst, sem).wait()


TAB_OFF, TAB_CNT, TAB_TILE_EXPERT, TAB_NVALID = 0, 1, 2, 3
TAB_LANES = 2 * LANES


def _ffn_kernel(tab_ref, pa_ref, pb_ref, xn_hbm, wg_hbm, wu_hbm, wd_hbm, ys_ref,
                src, xbuf, wbf, wst_in, wst_out, sem, wsem, *, n_tokens, tm):
    i = pl.program_id(0)
    n_valid = tab_ref[TAB_NVALID, 0]
    slot = lax.rem(i, N_GATHER_SLOTS)
    gather_priority = lambda r: r % 2

    def weight_copies(e):
        return (pltpu.make_async_copy(wg_hbm.at[e], wst_in.at[0], wsem.at[0]),
                pltpu.make_async_copy(wu_hbm.at[e], wst_in.at[1], wsem.at[1]),
                pltpu.make_async_copy(wd_hbm.at[e], wst_out, wsem.at[2]))

    @pl.when(i == 0)
    def _build_source_rows():
        for cp in weight_copies(tab_ref[TAB_TILE_EXPERT, 0]):
            cp.start()
        for e in range(N_EXPERTS):
            cnt_e = tab_ref[TAB_CNT, RL_E + e]
            first = tab_ref[TAB_OFF, RL_E + e] + cnt_e
            n_pad = lax.rem(tm - lax.rem(cnt_e, tm), tm)

            def pad_body(r, carry, first=first):
                src[first + r] = 0
                return carry
            lax.fori_loop(0, n_pad, pad_body, 0)

        def body(t, carry):
            first_row = t * TOK_TILE_ROWS
            src[pa_ref[t]] = first_row
            src[pb_ref[t]] = first_row
            return carry
        lax.fori_loop(0, n_tokens, body, 0, unroll=8)
        _start_tile_gather(lambda r: src[r], tm, xn_hbm, xbuf.at[0], sem.at[0], gather_priority)
        second = jnp.where(n_valid > 1, tm, 0)
        _start_tile_gather(lambda r: src[second + r], tm, xn_hbm, xbuf.at[1], sem.at[1], gather_priority)

    changed = jnp.logical_or(i == 0, tab_ref[TAB_TILE_EXPERT, i]
                             != tab_ref[TAB_TILE_EXPERT, jnp.maximum(i - 1, 0)])

    @pl.when(jnp.logical_and(changed, i < n_valid))
    def _cast_weights():
        e = tab_ref[TAB_TILE_EXPERT, i]
        for cp in weight_copies(e):
            cp.wait()
        wbf[0] = wst_in[0].astype(BF16)
        wbf[1] = wst_in[1].astype(BF16)
        wbf[2] = wst_out[...].astype(BF16)
        nxt = i + lax.div(tab_ref[TAB_CNT, RL_E + e] + (tm - 1), tm)

        @pl.when(nxt < n_valid)
        def _prefetch():
            for cp in weight_copies(tab_ref[TAB_TILE_EXPERT, jnp.minimum(nxt, pl.num_programs(0) - 1)]):
                cp.start()

    @pl.when(i < n_valid)
    def _compute():
        _wait_tile_gather(tm, xn_hbm, xbuf.at[slot], sem.at[slot])
        x = _load_token_tiles(xbuf.at[slot], tm).astype(BF16)
        hg = _dot(x, wbf[0])
        hu = _dot(x, wbf[1])
        y = _dot((_silu(hg) * hu).astype(BF16), wbf[2])
        base = jnp.where(i + 2 < n_valid, i + 2, 0) * tm
        ahead = lax.rem(i + 2, N_GATHER_SLOTS)
        _start_tile_gather(lambda r: src[base + r], tm, xn_hbm, xbuf.at[ahead], sem.at[ahead],
                           gather_priority)
        _store_token_tiles(ys_ref, y)

    @pl.when(i == n_valid - 1)
    def _drain():
        for k in (1, 2):
            s = lax.rem(i + k, N_GATHER_SLOTS)
            _wait_tile_gather(tm, xn_hbm, xbuf.at[s], sem.at[s])

    @pl.when(i >= n_valid)
    def _pad():
        ys_ref[...] = jnp.zeros_like(ys_ref)


def _expert_ffn(xn_tiles, tab, pos_a, pos_b, n_tiles, wg, wu, wd):
    n = pos_a.shape[0]
    tm = FFN_TM
    d, dff = wg.shape[1], wg.shape[2]
    rows = tm * TOK_TILE_ROWS
    assert d == dff
    grid_spec = pltpu.PrefetchScalarGridSpec(
        num_scalar_prefetch=3,
        grid=(n_tiles,),
        in_specs=[pl.BlockSpec(memory_space=pl.ANY)] * 4,
        out_specs=pl.BlockSpec((rows, LANES), lambda i, tab, pa, pb: (i, 0)),
        scratch_shapes=[pltpu.SMEM((n_tiles * tm,), jnp.int32),
                        pltpu.VMEM((N_GATHER_SLOTS, rows, LANES), F32),
                        pltpu.VMEM((3, d, dff), BF16),
                        pltpu.VMEM((2, d, dff), F32),
                        pltpu.VMEM((dff, d), F32),
                        pltpu.SemaphoreType.DMA((N_GATHER_SLOTS,)),
                        pltpu.SemaphoreType.DMA((3,))],
    )
    return pl.pallas_call(
        functools.partial(_ffn_kernel, n_tokens=n, tm=tm),
        out_shape=jax.ShapeDtypeStruct((n_tiles * rows, LANES), F32),
        grid_spec=grid_spec,
        compiler_params=pltpu.CompilerParams(
            dimension_semantics=("arbitrary",), vmem_limit_bytes=VMEM_LIMIT),
        name="expert_ffn",
    )(tab, pos_a, pos_b, xn_tiles, wg, wu, wd)


def _positions_kernel(info_ref, cnt_ref, pos_ref, tab_ref, *, tm, chunk):
    lane_i = lax.broadcasted_iota(jnp.int32, (1, LANES), 1)
    lane = lane_i.astype(F32)
    is_expert = jnp.logical_and(lane_i >= RL_E, lane_i < RL_E + N_EXPERTS)
    cnt = jnp.where(is_expert, cnt_ref[...], 0.0)
    padded = jnp.floor((cnt + (tm - 1)) / tm) * tm
    ri = lax.broadcasted_iota(jnp.int32, (LANES, LANES), 0)
    ci = lax.broadcasted_iota(jnp.int32, (LANES, LANES), 1)
    before = jnp.where(ri < ci, 1.0, 0.0).astype(BF16)
    hi, mid, lo = _split3(jnp.broadcast_to(padded, (SUBLANES, LANES)))
    off = ((_dot(hi, before) + _dot(mid, before)) + _dot(lo, before))[0:1, :]
    pick = jnp.where(lax.broadcasted_iota(jnp.int32, (SUBLANES, LANES), 0) == lane_i, 1.0, 0.0).astype(BF16)

    total = jnp.sum(padded, axis=-1, keepdims=True)
    tile_row = lax.broadcasted_iota(jnp.int32, (TAB_LANES, 1), 0).astype(F32) * tm
    ends = off + padded
    done = jnp.logical_and(is_expert, ends <= jnp.minimum(tile_row, total - 1.0))
    te_col = jnp.sum(jnp.where(done, 1.0, 0.0), axis=-1, keepdims=True)
    te_rows = _dot_nt(pick, jnp.where(lane_i == 0, te_col, 0.0).astype(BF16))
    tab_ref[...] = jnp.zeros_like(tab_ref)
    tab_ref[TAB_OFF:TAB_OFF + 1, 0:LANES] = off.astype(jnp.int32)
    tab_ref[TAB_CNT:TAB_CNT + 1, 0:LANES] = cnt.astype(jnp.int32)
    tab_ref[TAB_TILE_EXPERT:TAB_TILE_EXPERT + 1, :] = te_rows[0:1, :].astype(jnp.int32)
    tab_ref[TAB_NVALID:TAB_NVALID + 1, 0:LANES] = jnp.broadcast_to(total / tm, (1, LANES)).astype(jnp.int32)

    n = info_ref.shape[0]
    for c0 in range(0, n, chunk):
        blk = info_ref[c0:c0 + chunk, :]
        lane_a = blk[:, R_EA:R_EA + 1] + RL_E
        lane_b = blk[:, R_EB:R_EB + 1] + RL_E
        pos_a = blk[:, R_RA:R_RA + 1] + jnp.sum(jnp.where(lane == lane_a, off, 0.0), axis=-1, keepdims=True)
        pos_b = blk[:, R_RB:R_RB + 1] + jnp.sum(jnp.where(lane == lane_b, off, 0.0), axis=-1, keepdims=True)
        z_hi, z_mid, z_lo = _split3(jnp.where(lane_i == 0, pos_a, jnp.where(lane_i == 1, pos_b, 0.0)))
        rows = (_dot_nt(pick, z_hi) + _dot_nt(pick, z_mid)) + _dot_nt(pick, z_lo)
        pos_ref[:, c0:c0 + chunk] = rows.astype(jnp.int32)


def _routing_tables(info, counts, n_tiles, tm):
    n = info.shape[0]
    groups = n // LANES
    assert n_tiles <= TAB_LANES
    chunk = LANES * max(g for g in range(1, 65) if groups % g == 0)
    pos, tab = pl.pallas_call(
        functools.partial(_positions_kernel, tm=tm, chunk=chunk),
        out_shape=(jax.ShapeDtypeStruct((SUBLANES, n), jnp.int32),
                   jax.ShapeDtypeStruct((SUBLANES, TAB_LANES), jnp.int32)),
        grid=(1,),
        in_specs=[pl.BlockSpec((n, LANES), lambda i: (0, 0)), pl.BlockSpec((1, LANES), lambda i: (0, 0))],
        out_specs=(pl.BlockSpec((SUBLANES, n), lambda i: (0, 0)),
                   pl.BlockSpec((SUBLANES, TAB_LANES), lambda i: (0, 0))),
        compiler_params=pltpu.CompilerParams(
            dimension_semantics=("arbitrary",), vmem_limit_bytes=VMEM_LIMIT),
        name="positions",
    )(info, counts)
    return tab, pos[0], pos[1]


def _combine_kernel(pa_ref, pb_ref, h_ref, info_ref, ys_hbm, nfin_ref, yp_ref, ysm_ref,
                    buf_a, buf_b, sem, *, n_prompt_tiles):
    i = pl.program_id(0)
    n_steps = pl.num_programs(0)
    tt = h_ref.shape[0]
    slot = lax.rem(i, N_GATHER_SLOTS)

    def start(tile, s):
        base = tile * tt
        _start_tile_gather(lambda r: pa_ref[base + r] * TOK_TILE_ROWS, tt, ys_hbm, buf_a.at[s],
                           sem.at[s], lambda r: 0)
        _start_tile_gather(lambda r: pb_ref[base + r] * TOK_TILE_ROWS, tt, ys_hbm, buf_b.at[s],
                           sem.at[s], lambda r: 1)

    @pl.when(i == 0)
    def _first():
        start(0, 0)
        start(lax.rem(1, n_steps), 1)

    _wait_tile_gather(tt, ys_hbm, buf_a.at[slot], sem.at[slot])
    _wait_tile_gather(tt, ys_hbm, buf_b.at[slot], sem.at[slot])
    info = info_ref[...]
    rows_a = _load_token_tiles(buf_a.at[slot], tt)
    rows_b = _load_token_tiles(buf_b.at[slot], tt)
    x = h_ref[...]
    start(lax.rem(i + 2, n_steps), lax.rem(i + 2, N_GATHER_SLOTS))
    h = x + info[:, R_GA:R_GA + 1] * rows_a + info[:, R_GB:R_GB + 1] * rows_b
    y = h * _rms_scale(h) * nfin_ref[...]

    @pl.when(i < n_prompt_tiles)
    def _prompt():
        yp_ref[...] = y

    @pl.when(i >= n_prompt_tiles)
    def _sample():
        ysm_ref[...] = y

    @pl.when(i == n_steps - 1)
    def _drain():
        for k in (1, 2):
            s = lax.rem(i + k, N_GATHER_SLOTS)
            _wait_tile_gather(tt, ys_hbm, buf_a.at[s], sem.at[s])
            _wait_tile_gather(tt, ys_hbm, buf_b.at[s], sem.at[s])


def _combine(hmid, info, ys, pos_a, pos_b, nfin, n_prompt):
    n, d = hmid.shape
    tt = CHUNK
    n_prompt_tiles = n_prompt // tt
    n_sample = n - n_prompt
    grid_spec = pltpu.PrefetchScalarGridSpec(
        num_scalar_prefetch=2,
        grid=(n // tt,),
        in_specs=[pl.BlockSpec((tt, d), lambda i, pa, pb: (i, 0)),
                  pl.BlockSpec((tt, LANES), lambda i, pa, pb: (i, 0)),
                  pl.BlockSpec(memory_space=pl.ANY),
                  pl.BlockSpec((1, d), lambda i, pa, pb: (0, 0))],
        out_specs=(pl.BlockSpec((tt, d), lambda i, pa, pb: (jnp.minimum(i, n_prompt_tiles - 1), 0)),
                   pl.BlockSpec((tt, d), lambda i, pa, pb: (jnp.maximum(i - n_prompt_tiles, 0), 0))),
        scratch_shapes=[pltpu.VMEM((N_GATHER_SLOTS, tt * TOK_TILE_ROWS, LANES), F32),
                        pltpu.VMEM((N_GATHER_SLOTS, tt * TOK_TILE_ROWS, LANES), F32),
                        pltpu.SemaphoreType.DMA((N_GATHER_SLOTS,))],
    )
    return pl.pallas_call(
        functools.partial(_combine_kernel, n_prompt_tiles=n_prompt_tiles),
        out_shape=(jax.ShapeDtypeStruct((n_prompt, d), F32),
                   jax.ShapeDtypeStruct((n_sample, d), F32)),
        grid_spec=grid_spec,
        compiler_params=pltpu.CompilerParams(
            dimension_semantics=("arbitrary",), vmem_limit_bytes=VMEM_LIMIT),
        name="combine",
    )(pos_a, pos_b, hmid, info, ys, nfin)


def _moe_and_final_norm(hmid, n_prompt, rp, wg, wu, wd, nfin):
    n = hmid.shape[0]
    tm = FFN_TM
    n_tiles = (2 * n + N_EXPERTS * (tm - 1)) // tm
    xn, info, counts = _router(hmid, rp)
    tab, pos_a, pos_b = _routing_tables(info, counts, n_tiles, tm)
    ys = _expert_ffn(xn, tab, pos_a, pos_b, n_tiles, wg, wu, wd)
    return _combine(hmid, info, ys, pos_a, pos_b, nfin, n_prompt)


def kernel(x_prompt, x_sample, state_mlstm_C, state_mlstm_n, state_mlstm_m, state_mlstm_conv, state_ssm, state_ssm_conv, meta_tokens, norm_mix, w_in, conv_a_w, conv_a_b, w_q, w_k, w_v, b_i, b_f, norm_a, conv_b_w, conv_b_b, dt_bias, a_log, d_skip, norm_b, w_out, norm_ffn, w_r1, b_r1, w_r2, b_r2, w_gate, w_up, w_down, norm_final):
    bsz, seq, d = x_prompt.shape
    nb = x_sample.shape[0]
    d_a = H_A * DH_A
    conv_b = H_B * HD_B + 2 * G_B * N_STATE
    assert w_in.shape[0] == 1 and x_sample.shape[1] == 1 and seq % CHUNK == 0 and nb == CHUNK
    mp = _prep_mixer_params(norm_mix, w_in, conv_a_w, conv_a_b, w_q, w_k, w_v, b_i, b_f, norm_a,
                            conv_b_w, conv_b_b, dt_bias, a_log, d_skip, norm_b, w_out)
    rp = _prep_router_params(norm_ffn, w_r1, b_r1, w_r2, b_r2)
    xmeta = jnp.concatenate([jnp.zeros((CHUNK - N_META, d), F32), meta_tokens.astype(F32)], 0)

    hmid, p_c, p_n, p_m, p_ca, p_s, p_cb = _prompt_mixer(x_prompt.astype(F32), xmeta, mp, nb)
    m0 = jnp.pad(state_mlstm_m.reshape(nb, H_A).astype(F32), ((0, 0), (0, LANES - H_A)))
    hmid, s_c, s_n, s_m, s_ca, s_s, s_cb = _sample_mixer(
        x_sample.reshape(nb, d).astype(F32),
        state_mlstm_C.reshape(nb, H_A, DH_A, DH_A).astype(F32),
        state_mlstm_n.reshape(nb, d_a).astype(F32),
        m0,
        state_mlstm_conv.reshape(nb, (CONV_W - 1) * d_a).astype(F32),
        state_ssm.reshape(nb, H_B // 2, 2 * HD_B, N_STATE).astype(F32),
        state_ssm_conv.reshape(nb, (CONV_W - 1) * conv_b).astype(F32),
        mp, hmid, bsz * seq)

    wshape = w_gate.shape[1:]
    y_p, y_s = _moe_and_final_norm(
        hmid, bsz * seq, rp, w_gate.reshape(wshape).astype(F32), w_up.reshape(wshape).astype(F32),
        w_down.reshape(w_down.shape[1:]).astype(F32), norm_final.reshape(1, d).astype(F32))

    return (y_p.reshape(bsz, seq, d), y_s.reshape(nb, 1, d),
            p_c.reshape(1, bsz, H_A, DH_A, DH_A), p_n.reshape(1, bsz, H_A, DH_A),
            p_m[:, 0, :H_A].reshape(1, bsz, H_A), p_ca.reshape(1, bsz, CONV_W - 1, d_a),
            p_s.reshape(1, bsz, H_B, HD_B, N_STATE), p_cb.reshape(1, bsz, CONV_W - 1, conv_b),
            s_c.reshape(1, nb, H_A, DH_A, DH_A), s_n.reshape(1, nb, H_A, DH_A),
            s_m[:, :H_A].reshape(1, nb, H_A), s_ca.reshape(1, nb, CONV_W - 1, d_a),
            s_s.reshape(1, nb, H_B, HD_B, N_STATE), s_cb.reshape(1, nb, CONV_W - 1, conv_b))
```

```python
import functools

import jax
import jax.numpy as jnp
from jax import lax
from jax.experimental import pallas as pl
from jax.experimental.pallas import tpu as pltpu

F32 = jnp.float32
BF16 = jnp.bfloat16

EPS = 1e-6
N_META = 16
CONV_W = 4
CHUNK = 128
H_A = 8
DH_A = 128
H_B = 16
HD_B = 64
N_STATE = 128
G_B = 2
N_EGROUPS = 4
N_EPG = 4
N_EXPERTS = 16
LANES = 128
SUBLANES = 8
CONV_HDR = SUBLANES
VMEM_LIMIT = 56 * 1024 * 1024

L_F = 0
L_DTA = 8
L_I = 24
L_DT = 32

NEG_INF = float("-inf")


def _dot(a, b):
    return jnp.dot(a, b, preferred_element_type=F32)


def _dot_nt(a, b):
    return lax.dot_general(a, b, (((1,), (1,)), ((), ())), preferred_element_type=F32)


def _dot_tn(a, b):
    return lax.dot_general(a, b, (((0,), (0,)), ((), ())), preferred_element_type=F32)


def _split3(x):
    hi = x.astype(BF16)
    r = x - hi.astype(F32)
    mid = r.astype(BF16)
    lo = (r - mid.astype(F32)).astype(BF16)
    return hi, mid, lo


def _silu(x):
    return x * jax.nn.sigmoid(x)


def _softplus_parts(x):
    t = jnp.log1p(jnp.exp(-jnp.abs(x)))
    return jnp.maximum(x, 0.0) + t, jnp.minimum(x, 0.0) - t


def _rms_scale(x):
    return lax.rsqrt(jnp.mean(x * x, axis=-1, keepdims=True) + EPS)


TOK_TILE_ROWS = SUBLANES


def _store_token_tiles(ref, x):
    n = x.shape[0]
    for j in range(TOK_TILE_ROWS):
        ref[pl.ds(j, n, stride=TOK_TILE_ROWS), :] = x[:, j * LANES:(j + 1) * LANES]


def _causal_conv(x, tail, w_ref, b_ref):
    n_tail = tail.shape[0]
    row = lax.broadcasted_iota(jnp.int32, (n_tail, 1), 0)
    acc = w_ref[CONV_W - 1:CONV_W, :] * x + b_ref[...]
    for k in range(1, CONV_W):
        rolled = pltpu.roll(x, k, axis=0)
        head = jnp.where(row < k, pltpu.roll(tail, k, axis=0), rolled[0:n_tail])
        shifted = jnp.concatenate([head, rolled[n_tail:]], axis=0)
        acc = acc + w_ref[CONV_W - 1 - k:CONV_W - k, :] * shifted
    return acc


def _load_token_tiles(ref, n):
    return jnp.concatenate(
        [ref[pl.ds(j, n, stride=TOK_TILE_ROWS), :] for j in range(TOK_TILE_ROWS)], axis=1)


PROMPT_ROWS = 2


def _prompt_rows_kernel(xmeta_ref, xp_ref, nmix_ref, wcat_ref, bsm_ref, alog_ref,
                        cwa_ref, cba_ref, cwb_ref, cbb_ref, wq_ref, wk_ref, wv_ref,
                        na_ref, nb_ref, dsk_ref, wout_ref,
                        hmid_hbm, c_ref, n_ref, m_ref, conva_ref, s_ref, convb_ref,
                        xa_buf, xbc_buf, y_buf, merged, hout, sem, *, seq, n_prompt_rows):
    p = pl.program_id(0)
    c = pl.program_id(1)
    last_p = pl.num_programs(0) - 1
    last_c = pl.num_programs(1) - 1
    T = CHUNK
    RB = xp_ref.shape[0]
    d_a = H_A * DH_A
    d_b = H_B * HD_B
    conv_b = d_b + 2 * G_B * N_STATE

    def out_copy(r, row0):
        return pltpu.make_async_copy(hout.at[r], hmid_hbm.at[pl.ds(row0, T), :], sem.at[r])

    @pl.when(c == 0)
    def _init():
        c_ref[...] = jnp.zeros_like(c_ref)
        n_ref[...] = jnp.zeros_like(n_ref)
        m_ref[...] = jnp.zeros_like(m_ref)
        s_ref[...] = jnp.zeros_like(s_ref)
        xa_buf[...] = jnp.zeros_like(xa_buf)
        xbc_buf[...] = jnp.zeros_like(xbc_buf)

    @pl.when(jnp.logical_and(p == 0, c == 0))
    def _clear_sample_rows():
        hout[0] = jnp.zeros((T, hout.shape[2]), F32)
        cp = out_copy(0, n_prompt_rows)
        cp.start()
        cp.wait()

    row = lax.broadcasted_iota(jnp.int32, (T, 1), 0)
    valid = jnp.logical_or(c > 0, row >= T - N_META)
    xs_in = [jnp.where(c == 0, xmeta_ref[...], xp_ref[r]) for r in range(RB)]
    x2 = jnp.concatenate(xs_in, axis=0)
    hn = (x2 * _rms_scale(x2) * nmix_ref[...]).astype(BF16)

    lane = lax.broadcasted_iota(jnp.int32, (1, LANES), 1)
    lane_f = lane < L_DTA
    lane_dta = jnp.logical_and(lane >= L_DTA, lane < L_I)
    lane_i = jnp.logical_and(lane >= L_I, lane < L_DT)
    lane_dt = jnp.logical_and(lane >= L_DT, lane < L_DT + H_B)
    ri = lax.broadcasted_iota(jnp.int32, (T, T), 0)
    ci = lax.broadcasted_iota(jnp.int32, (T, T), 1)
    causal = ri >= ci
    tri = jnp.where(causal, 1.0, 0.0).astype(BF16)
    a_neg = jnp.where(lane_dta, -jnp.exp(alog_ref[...]), 0.0)
    left = lane < HD_B
    top = lax.broadcasted_iota(jnp.int32, (LANES, 1), 0) < HD_B

    off_small = 2 * d_a + d_b + conv_b
    pre2 = _dot(hn, wcat_ref[:, off_small:]) + bsm_ref[...]
    xa2 = _dot(hn, wcat_ref[:, 0:d_a])

    gcols, grows, xcs, xabs = [], [], [], []

    def gate_tables():
        for r in range(RB):
            pre = pre2[r * T:(r + 1) * T]
            sp, lsig = _softplus_parts(pre)
            to_cum = jnp.where(lane_f, lsig, jnp.where(lane_dta, sp * a_neg, 0.0))
            to_cum = jnp.where(valid, to_cum, 0.0)
            hi, mid, lo = _split3(to_cum)
            cum = _dot(tri, hi) + _dot(tri, mid) + _dot(tri, lo)
            extra = jnp.where(lane_i, jnp.where(valid, pre, NEG_INF),
                              jnp.where(lane_dt, jnp.where(valid, sp, 0.0), 0.0))
            gcol = cum + extra
            gcols.append(gcol)
            grows.append(gcol.T)

    gate_tables()
    for r in range(RB):
        rs = slice(r * T, (r + 1) * T)
        xa = xa2[rs]
        xc = _causal_conv(xa, xa_buf[r], cwa_ref, cba_ref)
        xa_buf[r] = xa[T - CONV_HDR:T, :]
        conva_ref[r] = xa[T - 3:T, :]
        xcs.append(_silu(xc).astype(BF16))
        xabs.append(xa.astype(BF16))

    items = [(r, h) for h in range(H_A) for r in range(RB)]
    hsl = lambda h: slice(h * DH_A, (h + 1) * DH_A)
    m_alls = [m_ref[r] for r in range(RB)]
    m_news = list(m_alls)
    qs, ks, vs, qks, st, dd = {}, {}, {}, {}, {}, {}

    def stage_qkv(it):
        r, h = it
        qs[it] = _dot(xcs[r][:, hsl(h)], wq_ref[h]).astype(BF16)
        ks[it] = _dot(xcs[r][:, hsl(h)], wk_ref[h]) * (DH_A ** -0.5)
        vs[it] = _dot(xabs[r][:, hsl(h)], wv_ref[h]).astype(BF16)

    def stage_qk(it):
        qks[it] = _dot_nt(qs[it], ks[it].astype(BF16))

    pairs_per_group = H_B // G_B // 2
    groups = [(r, g) for g in range(G_B) for r in range(RB)]
    proj, xbcs, bgs, cgs, cbs = {}, [], {}, {}, {}

    def project(name, lo, hi):
        proj[name] = _dot(hn, wcat_ref[:, lo:hi])

    def ssd_inputs():
        for r in range(RB):
            xbc = proj["xbc"][r * T:(r + 1) * T]
            xbc_c = _causal_conv(xbc, xbc_buf[r], cwb_ref, cbb_ref)
            xbc_buf[r] = xbc[T - CONV_HDR:T, :]
            convb_ref[r] = xbc[T - 3:T, :]
            xbcs.append(_silu(xbc_c))
        for r, g in groups:
            bgs[(r, g)] = xbcs[r][:, d_b + g * N_STATE:d_b + (g + 1) * N_STATE].astype(BF16)
            cgs[(r, g)] = xbcs[r][:, d_b + (G_B + g) * N_STATE:d_b + (G_B + g + 1) * N_STATE].astype(BF16)
            cbs[(r, g)] = _dot_nt(cgs[(r, g)], bgs[(r, g)])

    def stage_weights(it):
        r, h = it
        gcol, grow = gcols[r], grows[r]
        b_col = gcol[:, L_F + h:L_F + h + 1]
        i_col = gcol[:, L_I + h:L_I + h + 1]
        b_row = grow[L_F + h:L_F + h + 1, :]
        i_row = grow[L_I + h:L_I + h + 1, :]
        m0 = m_alls[r][:, h:h + 1]
        dmat = jnp.where(causal, b_col - (b_row - i_row), NEG_INF)
        m_inter = b_col + m0
        m = jnp.maximum(m_inter, jnp.max(dmat, axis=-1, keepdims=True))
        w_inter = jnp.exp(m_inter - m)
        s = qks[it] * jnp.exp(dmat - m)
        n0 = n_ref[r, h:h + 1, :]
        den = (jnp.sum(s, axis=-1, keepdims=True)
               + w_inter * jnp.sum(qs[it].astype(F32) * n0, axis=-1, keepdims=True))
        m_last = m[T - 1:T, :]
        b_last = b_col[T - 1:T, :]
        dec = jnp.exp(b_last + m0 - m_last)
        kw = ks[it] * jnp.exp(b_last - b_col + i_col - m_last)
        n_ref[r, h:h + 1, :] = dec * n0 + jnp.sum(kw, axis=0, keepdims=True)
        m_news[r] = jnp.where(lane == h, m_last, m_news[r])
        st[it] = (s.astype(BF16), kw.astype(BF16), w_inter,
                  jnp.maximum(jnp.abs(den), jnp.exp(-m)), dec)

    def stage_readout(it):
        r, h = it
        s_b, kw_b, w_inter, den, dec = st[it]
        c0 = c_ref[r, h]
        num = _dot(s_b, vs[it]) + w_inter * _dot(qs[it], c0.astype(BF16))
        c_ref[r, h] = dec * c0 + _dot_tn(kw_b, vs[it])
        dd[it] = num / den

    def stage_head_out(it):
        r, h = it
        hh = dd[it]
        hh = hh * _rms_scale(hh) * na_ref[:, hsl(h)]
        merged[r * T:(r + 1) * T, hsl(h)] = (
            hh * jax.nn.sigmoid(proj["za"][r * T:(r + 1) * T, hsl(h)])).astype(BF16)

    pairs = [(r, pi) for pi in range(H_B // 2) for r in range(RB)]
    psl = lambda pi: slice(pi * LANES, (pi + 1) * LANES)
    sw = {}

    def stage_decay(pr):
        r, pi = pr
        g = pi // pairs_per_group
        gcol, grow = gcols[r], grows[r]
        xpair = xbcs[r][:, psl(pi)]
        scs, a_cols, w_cols, a_lasts = [], [], [], []
        for j in (2 * pi, 2 * pi + 1):
            a_col = gcol[:, L_DTA + j:L_DTA + j + 1]
            a_row = grow[L_DTA + j:L_DTA + j + 1, :]
            dt_col = gcol[:, L_DT + j:L_DT + j + 1]
            dt_row = grow[L_DT + j:L_DT + j + 1, :]
            decay = jnp.exp(jnp.where(causal, a_col - a_row, NEG_INF))
            scs.append((cbs[(r, g)] * decay * dt_row).astype(BF16))
            a_last = a_col[T - 1:T, :]
            a_cols.append(a_col)
            a_lasts.append(a_last)
            w_cols.append(jnp.exp(a_last - a_col) * dt_col)
        sw[pr] = (scs, xpair.astype(BF16),
                  (xpair * jnp.where(left, w_cols[0], w_cols[1])).astype(BF16),
                  jnp.exp(jnp.where(left, a_cols[0], a_cols[1])),
                  jnp.exp(jnp.where(top, a_lasts[0], a_lasts[1])))
    def stage_pair_out(pr):
        r, pi = pr
        g = pi // pairs_per_group
        scs, xpb, xw, ea, ea_last = sw[pr]
        s0 = s_ref[r, pi]
        y = jnp.where(left, _dot(scs[0], xpb), _dot(scs[1], xpb))
        y = y + ea * _dot_nt(cgs[(r, g)], s0.astype(BF16))
        s_ref[r, pi] = ea_last * s0 + _dot_tn(xw, bgs[(r, g)])
        y = y + dsk_ref[:, psl(pi)] * xbcs[r][:, psl(pi)]
        y_buf[r, :, psl(pi)] = y * _silu(proj["zb"][r * T:(r + 1) * T, psl(pi)])

    def each(stage, seq):
        for e in seq:
            stage(e)

    each(stage_qkv, items)
    project("xbc", 2 * d_a + d_b, off_small)
    each(stage_qk, items)
    project("za", d_a, 2 * d_a)
    ssd_inputs()
    each(stage_weights, items)
    each(stage_readout, items)
    project("zb", 2 * d_a, 2 * d_a + d_b)
    for r in range(RB):
        m_ref[r] = m_news[r]
    each(stage_head_out, items)
    each(stage_decay, pairs)
    each(stage_pair_out, pairs)
    gw = d_b // G_B
    for r in range(RB):
        for g in range(G_B):
            yg = y_buf[r, :, g * gw:(g + 1) * gw]
            merged[r * T:(r + 1) * T, d_a + g * gw:d_a + (g + 1) * gw] = (
                yg * _rms_scale(yg) * nb_ref[:, g * gw:(g + 1) * gw]).astype(BF16)

    @pl.when(c > 0)
    def _out():
        out2 = x2 + _dot(merged[...], wout_ref[...])

        @pl.when(jnp.logical_or(c > 1, p > 0))
        def _wait_previous():
            for r in range(RB):
                out_copy(r, 0).wait()

        for r in range(RB):
            hout[r] = out2[r * T:(r + 1) * T]
            out_copy(r, (p * RB + r) * seq + (c - 1) * T).start()

        @pl.when(jnp.logical_and(p == last_p, c == last_c))
        def _drain():
            for r in range(RB):
                out_copy(r, 0).wait()


def _const_spec(shape):
    nd = len(shape)
    return pl.BlockSpec(shape, lambda b, c, _nd=nd: (0,) * _nd)


def _prompt_mixer(x_prompt, xmeta, p, n_extra_rows):
    bsz, seq, d = x_prompt.shape
    assert n_extra_rows == CHUNK and seq % CHUNK == 0
    n_chunks = seq // CHUNK + 1
    cps = seq // CHUNK
    d_a = H_A * DH_A
    conv_b = H_B * HD_B + 2 * G_B * N_STATE
    consts = [p["nmix"], p["wcat"], p["bsm"], p["alog"], p["cwa"], p["cba"], p["cwb"], p["cbb"],
              p["wq"], p["wk"], p["wv"], p["na"], p["nb"], p["dsk"], p["wout"]]
    rb = PROMPT_ROWS
    assert bsz % rb == 0
    in_specs = [_const_spec(xmeta.shape),
                pl.BlockSpec((rb, CHUNK, d), lambda b, c: (b, jnp.maximum(c - 1, 0), 0))]
    in_specs += [_const_spec(a.shape) for a in consts]
    out_shape = (
        jax.ShapeDtypeStruct((bsz * seq + n_extra_rows, d), F32),
        jax.ShapeDtypeStruct((bsz, H_A, DH_A, DH_A), F32),
        jax.ShapeDtypeStruct((bsz, H_A, DH_A), F32),
        jax.ShapeDtypeStruct((bsz, 1, LANES), F32),
        jax.ShapeDtypeStruct((bsz, CONV_W - 1, d_a), F32),
        jax.ShapeDtypeStruct((bsz, H_B // 2, 2 * HD_B, N_STATE), F32),
        jax.ShapeDtypeStruct((bsz, CONV_W - 1, conv_b), F32),
    )
    out_specs = (
        pl.BlockSpec(memory_space=pl.ANY),
        pl.BlockSpec((rb, H_A, DH_A, DH_A), lambda b, c: (b, 0, 0, 0)),
        pl.BlockSpec((rb, H_A, DH_A), lambda b, c: (b, 0, 0)),
        pl.BlockSpec((rb, 1, LANES), lambda b, c: (b, 0, 0)),
        pl.BlockSpec((rb, CONV_W - 1, d_a), lambda b, c: (b, 0, 0)),
        pl.BlockSpec((rb, H_B // 2, 2 * HD_B, N_STATE), lambda b, c: (b, 0, 0, 0)),
        pl.BlockSpec((rb, CONV_W - 1, conv_b), lambda b, c: (b, 0, 0)),
    )
    return pl.pallas_call(
        functools.partial(_prompt_rows_kernel, seq=seq, n_prompt_rows=bsz * seq),
        out_shape=out_shape,
        grid=(bsz // rb, n_chunks),
        in_specs=in_specs,
        out_specs=out_specs,
        scratch_shapes=[
            pltpu.VMEM((rb, CONV_HDR, d_a), F32),
            pltpu.VMEM((rb, CONV_HDR, conv_b), F32),
            pltpu.VMEM((rb, CHUNK, H_B * HD_B), F32),
            pltpu.VMEM((rb * CHUNK, d_a + H_B * HD_B), BF16),
            pltpu.VMEM((rb, CHUNK, d), F32),
            pltpu.SemaphoreType.DMA((rb,)),
        ],
        compiler_params=pltpu.CompilerParams(
            dimension_semantics=("arbitrary", "arbitrary"), vmem_limit_bytes=VMEM_LIMIT),
        name="prompt_mixer",
    )(xmeta, x_prompt, *consts)


def _regroup_w_in_kernel(w_ref, o_ref):
    d_a = H_A * DH_A
    d_b = H_B * HD_B
    conv_b = d_b + 2 * G_B * N_STATE
    o_i = 2 * d_a
    o_f = o_i + H_A
    o_zb = o_f + H_A
    o_xbc = o_zb + d_b
    o_dt = o_xbc + conv_b
    rows = w_ref.shape[0]
    o_ref[:, 0:2 * d_a] = w_ref[:, 0:2 * d_a].astype(BF16)
    o_ref[:, 2 * d_a:2 * d_a + d_b] = w_ref[:, o_zb:o_zb + d_b].astype(BF16)
    o_ref[:, 2 * d_a + d_b:2 * d_a + d_b + conv_b] = w_ref[:, o_xbc:o_xbc + conv_b].astype(BF16)
    small = jnp.concatenate(
        [w_ref[:, o_f:o_f + H_A], w_ref[:, o_dt:o_dt + H_B], w_ref[:, o_i:o_i + H_A],
         w_ref[:, o_dt:o_dt + H_B], jnp.zeros((rows, LANES - (L_DT + H_B)), F32)], axis=1)
    o_ref[:, 2 * d_a + d_b + conv_b:] = small.astype(BF16)


def _prep_mixer_params(norm_mix, w_in, conv_a_w, conv_a_b, w_q, w_k, w_v, b_i, b_f, norm_a,
                       conv_b_w, conv_b_b, dt_bias, a_log, d_skip, norm_b, w_out):
    d_a = H_A * DH_A
    d_b = H_B * HD_B
    conv_b = d_b + 2 * G_B * N_STATE
    d_model, d_in = w_in.shape[1], w_in.shape[2]
    n_cols = 2 * d_a + d_b + conv_b + LANES
    rows = 256
    assert w_in.shape[0] == 1 and d_model % rows == 0
    wcat = pl.pallas_call(
        _regroup_w_in_kernel,
        out_shape=jax.ShapeDtypeStruct((d_model, n_cols), BF16),
        grid=(d_model // rows,),
        in_specs=[pl.BlockSpec((None, rows, d_in), lambda i: (0, i, 0))],
        out_specs=pl.BlockSpec((rows, n_cols), lambda i: (i, 0)),
        compiler_params=pltpu.CompilerParams(
            dimension_semantics=("arbitrary",), vmem_limit_bytes=VMEM_LIMIT),
        name="regroup_w_in",
    )(w_in.astype(F32))

    def lanes(parts):
        pieces, at = [], 0
        for off, a in parts:
            pieces += [jnp.zeros((1, off - at), F32), a.astype(F32)]
            at = off + a.shape[1]
        return jnp.concatenate(pieces + [jnp.zeros((1, LANES - at), F32)], axis=1)

    return dict(
        nmix=norm_mix.reshape(1, -1).astype(F32),
        wcat=wcat,
        bsm=lanes([(L_F, b_f), (L_DTA, dt_bias), (L_I, b_i), (L_DT, dt_bias)]),
        alog=lanes([(L_DTA, a_log)]),
        cwa=conv_a_w.reshape(CONV_W, d_a).astype(F32), cba=conv_a_b.reshape(1, d_a).astype(F32),
        cwb=conv_b_w.reshape(CONV_W, conv_b).astype(F32), cbb=conv_b_b.reshape(1, conv_b).astype(F32),
        wq=w_q.reshape(H_A, DH_A, DH_A).astype(BF16), wk=w_k.reshape(H_A, DH_A, DH_A).astype(BF16),
        wv=w_v.reshape(H_A, DH_A, DH_A).astype(BF16),
        na=norm_a.reshape(1, d_a).astype(F32), nb=norm_b.reshape(1, d_b).astype(F32),
        dsk=jnp.repeat(d_skip.reshape(H_B).astype(F32), HD_B)[None, :],
        wout=w_out.reshape(d_a + d_b, -1).astype(BF16),
    )


SAMPLE_BLOCK = 8


def _expand_lanes(vals, first_lane, n_heads, width):
    r = lax.broadcasted_iota(jnp.int32, (LANES, n_heads * width), 0) - first_lane
    c = lax.broadcasted_iota(jnp.int32, (LANES, n_heads * width), 1)
    sel = jnp.logical_and(c >= r * width, c < (r + 1) * width)
    e = jnp.where(sel, 1.0, 0.0).astype(BF16)
    hi, mid, lo = _split3(vals)
    return (_dot(hi, e) + _dot(mid, e)) + _dot(lo, e)


def _sample_pre_kernel(x_ref, nmix_ref, wcat_ref, bsm_ref, alog_ref, cwa_ref, cba_ref, cwb_ref, cbb_ref,
                       wq_ref, wk_ref, wv_ref, dsk_ref, conva_ref, convb_ref, n0_ref, m0_ref,
                       conva_out, convb_out, n1_out, m1_out, g_out, qt_out, kwt_out, xwt_out,
                       v_out, bc_out, a1_out, w1_out, den_out, y1_out, ea_out, zbs_out, zas_out):
    d_a = H_A * DH_A
    d_b = H_B * HD_B
    conv_b = d_b + 2 * G_B * N_STATE
    shift_i = LANES - (L_I - L_F)
    x = x_ref[...]
    hn = (x * _rms_scale(x) * nmix_ref[...]).astype(BF16)
    lane = lax.broadcasted_iota(jnp.int32, (1, LANES), 1)
    lane_f = lane < L_DTA
    lane_dta = jnp.logical_and(lane >= L_DTA, lane < L_I)
    pre = _dot(hn, wcat_ref[:, 2 * d_a + d_b + conv_b:]) + bsm_ref[...]
    sp, lsig = _softplus_parts(pre)
    a_neg = jnp.where(lane_dta, -jnp.exp(alog_ref[...]), 0.0)
    pre_al = pltpu.roll(pre, shift_i, axis=1)
    sp_al = pltpu.roll(sp, shift_i, axis=1)
    m_inter = lsig + m0_ref[...]
    m = jnp.maximum(m_inter, pre_al)
    w_inter = jnp.exp(m_inter - m)
    sfac = jnp.exp(pre_al - m)
    ea = jnp.exp(sp * a_neg)
    dt = sp_al

    xa = _dot(hn, wcat_ref[:, 0:d_a])
    xc = (cwa_ref[0:1, :] * conva_ref[:, 0:d_a] + cwa_ref[1:2, :] * conva_ref[:, d_a:2 * d_a]
          + cwa_ref[2:3, :] * conva_ref[:, 2 * d_a:3 * d_a] + cwa_ref[3:4, :] * xa + cba_ref[...])
    conva_out[:, 0:2 * d_a] = conva_ref[:, d_a:3 * d_a]
    conva_out[:, 2 * d_a:3 * d_a] = xa
    xc = _silu(xc).astype(BF16)
    xab = xa.astype(BF16)
    sf_e = _expand_lanes(sfac, L_F, H_A, DH_A)
    w_e = _expand_lanes(w_inter, L_F, H_A, DH_A)
    qk8 = jnp.zeros((x.shape[0], LANES), F32)
    qn8 = jnp.zeros((x.shape[0], LANES), F32)
    for h in range(H_A):
        sl = slice(h * DH_A, (h + 1) * DH_A)
        q = _dot(xc[:, sl], wq_ref[h])
        k = _dot(xc[:, sl], wk_ref[h]) * (DH_A ** -0.5)
        v = _dot(xab[:, sl], wv_ref[h])
        kw = k * sf_e[:, sl]
        qk8 = jnp.where(lane == h, jnp.sum(q * k, axis=-1, keepdims=True), qk8)
        qn8 = jnp.where(lane == h, jnp.sum(q * n0_ref[:, sl], axis=-1, keepdims=True), qn8)
        n1_out[:, sl] = w_e[:, sl] * n0_ref[:, sl] + kw
        v_out[:, sl] = v
        qt_out[h] = q.T
        kwt_out[h] = kw.T
    s8 = qk8 * sfac
    a1_out[...] = _expand_lanes(s8, L_F, H_A, DH_A) * v_out[...]
    w1_out[...] = w_e
    den_out[...] = jnp.maximum(jnp.abs(_expand_lanes(s8 + w_inter * qn8, L_F, H_A, DH_A)),
                               jnp.exp(-_expand_lanes(m, L_F, H_A, DH_A)))
    m1_out[...] = m
    g_out[...] = jnp.where(lane_f, w_inter, jnp.where(lane_dta, ea, 0.0))
    zas_out[...] = jax.nn.sigmoid(_dot(hn, wcat_ref[:, d_a:2 * d_a]))

    off_xbc = 2 * d_a + d_b
    xbc = _dot(hn, wcat_ref[:, off_xbc:off_xbc + conv_b])
    xbc_c = (cwb_ref[0:1, :] * convb_ref[:, 0:conv_b] + cwb_ref[1:2, :] * convb_ref[:, conv_b:2 * conv_b]
             + cwb_ref[2:3, :] * convb_ref[:, 2 * conv_b:3 * conv_b] + cwb_ref[3:4, :] * xbc + cbb_ref[...])
    convb_out[:, 0:2 * conv_b] = convb_ref[:, conv_b:3 * conv_b]
    convb_out[:, 2 * conv_b:3 * conv_b] = xbc
    xbc_c = _silu(xbc_c)
    xs = xbc_c[:, 0:d_b]
    bc = xbc_c[:, d_b:conv_b]
    bc_out[...] = bc
    heads_per_group = H_B // G_B
    cbl = jnp.zeros((x.shape[0], LANES), F32)
    for g in range(G_B):
        cb_g = jnp.sum(bc[:, g * N_STATE:(g + 1) * N_STATE]
                       * bc[:, (G_B + g) * N_STATE:(G_B + g + 1) * N_STATE], axis=-1, keepdims=True)
        in_g = jnp.logical_and(lane >= L_DTA + g * heads_per_group,
                               lane < L_DTA + (g + 1) * heads_per_group)
        cbl = jnp.where(in_g, cb_g, cbl)
    dt_e = _expand_lanes(dt, L_DTA, H_B, HD_B)
    y1_out[...] = _expand_lanes(cbl * dt, L_DTA, H_B, HD_B) * xs + dsk_ref[...] * xs
    ea_out[...] = _expand_lanes(ea, L_DTA, H_B, HD_B)
    zbs_out[...] = _silu(_dot(hn, wcat_ref[:, 2 * d_a:2 * d_a + d_b]))
    xw = xs * dt_e
    for pi in range(H_B // 2):
        xwt_out[pi] = xw[:, pi * LANES:(pi + 1) * LANES].T


def _sample_state_kernel(g_ref, c0_ref, s0_ref, qt_ref, kwt_ref, xwt_ref, v_ref, bc_ref,
                         c1_ref, s1_ref, qc_ref, ysi_ref):
    i = pl.program_id(0)
    bb = c0_ref.shape[0]
    shift = lax.rem(LANES - lax.rem(i * bb, LANES), LANES)
    lane = lax.broadcasted_iota(jnp.int32, (1, LANES), 1)
    top = lax.broadcasted_iota(jnp.int32, (LANES, 1), 0) < HD_B
    heads_per_group = H_B // G_B
    qts = [pltpu.roll(qt_ref[h], shift, axis=1) for h in range(H_A)]
    kwts = [pltpu.roll(kwt_ref[h], shift, axis=1) for h in range(H_A)]
    xwts = [pltpu.roll(xwt_ref[pi], shift, axis=1) for pi in range(H_B // 2)]
    for h in range(H_A):
        sl = slice(h * DH_A, (h + 1) * DH_A)
        qt = qts[h]
        kwt = kwts[h]
        for r in range(bb):
            b = i * bb + r
            c0 = c0_ref[r, h]
            dec = g_ref[b, L_F + h]
            v_row = v_ref[r:r + 1, sl]
            qc_ref[r:r + 1, sl] = jnp.sum(c0 * qt[:, r:r + 1], axis=0, keepdims=True)
            c1_ref[r, h] = dec * c0 + kwt[:, r:r + 1] * v_row
    for pi in range(H_B // 2):
        g = (2 * pi) // heads_per_group
        sl = slice(pi * LANES, (pi + 1) * LANES)
        xwt = xwts[pi]
        acc = jnp.zeros((LANES, LANES), F32)
        for r in range(bb):
            b = i * bb + r
            s0 = s0_ref[r, pi]
            b_row = bc_ref[r:r + 1, g * N_STATE:(g + 1) * N_STATE]
            c_row = bc_ref[r:r + 1, (G_B + g) * N_STATE:(G_B + g + 1) * N_STATE]
            col = jnp.sum(s0 * c_row, axis=-1, keepdims=True)
            acc = jnp.where(lane == r, col, acc)
            ea_rows = jnp.where(top, g_ref[b, L_DTA + 2 * pi], g_ref[b, L_DTA + 2 * pi + 1])
            s1_ref[r, pi] = ea_rows * s0 + xwt[:, r:r + 1] * b_row
        ysi_ref[:, sl] = acc.T[0:bb, :]


def _sample_post_kernel(a1_ref, w1_ref, den_ref, y1_ref, ea_ref, zbs_ref, zas_ref, x_ref, qc_ref, ysi_ref,
                        na_ref, nb_ref, wout_ref, hall_ref, hmid_ref, merged):
    del hall_ref
    d_a = H_A * DH_A
    d_b = H_B * HD_B
    hh = (a1_ref[...] + w1_ref[...] * qc_ref[...]) / den_ref[...]
    for h in range(H_A):
        sl = slice(h * DH_A, (h + 1) * DH_A)
        hs = hh[:, sl]
        merged[:, sl] = (hs * _rms_scale(hs) * na_ref[:, sl] * zas_ref[:, sl]).astype(BF16)
    y = (y1_ref[...] + ea_ref[...] * ysi_ref[...]) * zbs_ref[...]
    gw = d_b // G_B
    for g in range(G_B):
        yg = y[:, g * gw:(g + 1) * gw]
        merged[:, d_a + g * gw:d_a + (g + 1) * gw] = (
            yg * _rms_scale(yg) * nb_ref[:, g * gw:(g + 1) * gw]).astype(BF16)
    hmid_ref[...] = x_ref[...] + _dot(merged[...], wout_ref[...])


def _vmem_specs(arrays):
    return [pl.BlockSpec(a.shape, lambda *_, _nd=a.ndim: (0,) * _nd) for a in arrays]


def _sample_mixer(x, c0, n0, m0, conva, s0, convb, p, hmid_all, row_offset):
    nb, d = x.shape
    d_a = H_A * DH_A
    d_b = H_B * HD_B
    conv_b = d_b + 2 * G_B * N_STATE
    row = lambda w: jax.ShapeDtypeStruct((nb, w), F32)
    tile = lambda k: jax.ShapeDtypeStruct((k, LANES, nb), F32)
    pre_in = [x, p["nmix"], p["wcat"], p["bsm"], p["alog"], p["cwa"], p["cba"], p["cwb"], p["cbb"],
              p["wq"], p["wk"], p["wv"], p["dsk"], conva, convb, n0, m0]
    pre_out_shape = (row(3 * d_a), row(3 * conv_b), row(d_a), row(LANES), row(LANES),
                     tile(H_A), tile(H_A), tile(H_B // 2), row(d_a), row(2 * G_B * N_STATE),
                     row(d_a), row(d_a), row(d_a), row(d_b), row(d_b), row(d_b), row(d_a))
    (conva1, convb1, n1, m1, g8, qt, kwt, xwt, v, bc, a1, w1, den, y1, ea_e, zbs, zas) = pl.pallas_call(
        _sample_pre_kernel,
        out_shape=pre_out_shape,
        grid=(1,),
        in_specs=_vmem_specs(pre_in),
        out_specs=tuple(pl.BlockSpec(s.shape, lambda i, _nd=len(s.shape): (0,) * _nd) for s in pre_out_shape),
        compiler_params=pltpu.CompilerParams(
            dimension_semantics=("arbitrary",), vmem_limit_bytes=VMEM_LIMIT),
        name="sample_pre",
    )(*pre_in)

    bb = SAMPLE_BLOCK
    const3 = lambda k: pl.BlockSpec((k, LANES, nb), lambda i, g: (0, 0, 0))
    state_grid = pltpu.PrefetchScalarGridSpec(
        num_scalar_prefetch=1,
        grid=(nb // bb,),
        in_specs=[pl.BlockSpec((bb, H_A, DH_A, DH_A), lambda i, g: (i, 0, 0, 0)),
                  pl.BlockSpec((bb, H_B // 2, 2 * HD_B, N_STATE), lambda i, g: (i, 0, 0, 0)),
                  const3(H_A), const3(H_A), const3(H_B // 2),
                  pl.BlockSpec((bb, d_a), lambda i, g: (i, 0)),
                  pl.BlockSpec((bb, 2 * G_B * N_STATE), lambda i, g: (i, 0))],
        out_specs=(pl.BlockSpec((bb, H_A, DH_A, DH_A), lambda i, g: (i, 0, 0, 0)),
                   pl.BlockSpec((bb, H_B // 2, 2 * HD_B, N_STATE), lambda i, g: (i, 0, 0, 0)),
                   pl.BlockSpec((bb, d_a), lambda i, g: (i, 0)),
                   pl.BlockSpec((bb, d_b), lambda i, g: (i, 0))),
    )
    c1, s1, qc, ysi = pl.pallas_call(
        _sample_state_kernel,
        out_shape=(jax.ShapeDtypeStruct(c0.shape, F32), jax.ShapeDtypeStruct(s0.shape, F32),
                   row(d_a), row(d_b)),
        grid_spec=state_grid,
        compiler_params=pltpu.CompilerParams(
            dimension_semantics=("arbitrary",), vmem_limit_bytes=VMEM_LIMIT),
        name="sample_state",
    )(g8, c0, s0, qt, kwt, xwt, v, bc)

    post_in = [a1, w1, den, y1, ea_e, zbs, zas, x, qc, ysi, p["na"], p["nb"], p["wout"]]
    hmid_all = pl.pallas_call(
        _sample_post_kernel,
        out_shape=jax.ShapeDtypeStruct(hmid_all.shape, F32),
        grid=(1,),
        in_specs=_vmem_specs(post_in) + [pl.BlockSpec(memory_space=pl.ANY)],
        out_specs=pl.BlockSpec((nb, d), lambda i: (row_offset // nb, 0)),
        scratch_shapes=[pltpu.VMEM((nb, d_a + d_b), BF16)],
        input_output_aliases={len(post_in): 0},
        compiler_params=pltpu.CompilerParams(
            dimension_semantics=("arbitrary",), vmem_limit_bytes=VMEM_LIMIT),
        name="sample_post",
    )(*post_in, hmid_all)
    return hmid_all, c1, n1, m1, conva1, s1, convb1


R_EA, R_EB, R_RA, R_RB, R_GA, R_GB = 0, 1, 2, 3, 4, 5
RL_E = N_EGROUPS


def _router_kernel(h_ref, nf_ref, whi_ref, wmid_ref, br_ref, xn_ref, info_ref, cnt_ref, carry):
    i = pl.program_id(0)
    tr = h_ref.shape[0]

    @pl.when(i == 0)
    def _init():
        carry[...] = jnp.zeros_like(carry)

    h = h_ref[...]
    xn = h * _rms_scale(h) * nf_ref[...]
    _store_token_tiles(xn_ref, xn)
    x_hi, x_mid, _ = _split3(xn)
    logits = (_dot(x_hi, whi_ref[...]) + _dot(x_hi, wmid_ref[...]) + _dot(x_mid, whi_ref[...])
              + br_ref[...])
    lane_i = lax.broadcasted_iota(jnp.int32, (1, LANES), 1)
    lane = lane_i.astype(F32)
    big = float(LANES)

    def first_lane_of(cond):
        return jnp.min(jnp.where(cond, lane, big), axis=-1, keepdims=True)

    l1 = jnp.where(lane_i < N_EGROUPS, logits, NEG_INF)
    e1 = jnp.exp(l1 - jnp.max(l1, axis=-1, keepdims=True))
    p1 = e1 / jnp.sum(e1, axis=-1, keepdims=True)
    gp = jnp.max(p1, axis=-1, keepdims=True)
    gidx = first_lane_of(p1 == gp)
    lo = RL_E + N_EPG * gidx
    l2 = jnp.where(jnp.logical_and(lane >= lo, lane < lo + N_EPG), logits, NEG_INF)
    va = jnp.max(l2, axis=-1, keepdims=True)
    ia = first_lane_of(l2 == va)
    l2b = jnp.where(lane == ia, NEG_INF, l2)
    vb = jnp.max(l2b, axis=-1, keepdims=True)
    ib = first_lane_of(l2b == vb)
    eb = jnp.exp(vb - va)
    wa = 1.0 / (1.0 + eb)
    wb = eb / (1.0 + eb)

    is_a = lane == ia
    is_b = lane == ib
    onehot = jnp.where(jnp.logical_or(is_a, is_b), 1.0, 0.0)
    ri = lax.broadcasted_iota(jnp.int32, (tr, tr), 0)
    ci = lax.broadcasted_iota(jnp.int32, (tr, tr), 1)
    tri = jnp.where(ri >= ci, 1.0, 0.0).astype(BF16)
    incl = _dot(tri, onehot.astype(BF16))
    excl = incl - onehot + carry[...]
    rank_a = jnp.sum(jnp.where(is_a, excl, 0.0), axis=-1, keepdims=True)
    rank_b = jnp.sum(jnp.where(is_b, excl, 0.0), axis=-1, keepdims=True)
    carry[...] = carry[...] + incl[tr - 1:tr, :]
    cnt_ref[...] = carry[...]

    info = jnp.where(lane_i == R_EA, ia - RL_E, 0.0)
    info = jnp.where(lane_i == R_EB, ib - RL_E, info)
    info = jnp.where(lane_i == R_RA, rank_a, info)
    info = jnp.where(lane_i == R_RB, rank_b, info)
    info = jnp.where(lane_i == R_GA, gp * wa, info)
    info = jnp.where(lane_i == R_GB, gp * wb, info)
    info_ref[...] = info


def _row_tile(n, candidates):
    for t in candidates:
        if n % t == 0:
            return t
    raise ValueError(f"no row tile for {n} rows among {candidates}")


def _router(hmid, rp):
    n, d = hmid.shape
    assert d == TOK_TILE_ROWS * LANES
    tr = _row_tile(n, (512, 384, 256, 128))
    return pl.pallas_call(
        _router_kernel,
        out_shape=(jax.ShapeDtypeStruct((n * TOK_TILE_ROWS, LANES), F32),
                   jax.ShapeDtypeStruct((n, LANES), F32),
                   jax.ShapeDtypeStruct((1, LANES), F32)),
        grid=(n // tr,),
        in_specs=[pl.BlockSpec((tr, d), lambda i: (i, 0)),
                  pl.BlockSpec((1, d), lambda i: (0, 0)),
                  pl.BlockSpec((d, LANES), lambda i: (0, 0)),
                  pl.BlockSpec((d, LANES), lambda i: (0, 0)),
                  pl.BlockSpec((1, LANES), lambda i: (0, 0))],
        out_specs=(pl.BlockSpec((tr * TOK_TILE_ROWS, LANES), lambda i: (i, 0)),
                   pl.BlockSpec((tr, LANES), lambda i: (i, 0)),
                   pl.BlockSpec((1, LANES), lambda i: (0, 0))),
        scratch_shapes=[pltpu.VMEM((1, LANES), F32)],
        compiler_params=pltpu.CompilerParams(
            dimension_semantics=("arbitrary",), vmem_limit_bytes=VMEM_LIMIT),
        name="router",
    )(hmid, rp["nf"], rp["whi"], rp["wmid"], rp["br"])


def _prep_router_params(norm_ffn, w_r1, b_r1, w_r2, b_r2):
    d = w_r1.shape[1]
    w = jnp.concatenate([w_r1.reshape(d, N_EGROUPS).astype(F32), w_r2.reshape(d, N_EXPERTS).astype(F32),
                         jnp.zeros((d, LANES - RL_E - N_EXPERTS), F32)], axis=1)
    whi = w.astype(BF16)
    wmid = (w - whi.astype(F32)).astype(BF16)
    br = jnp.concatenate([b_r1.reshape(1, N_EGROUPS).astype(F32), b_r2.reshape(1, N_EXPERTS).astype(F32),
                          jnp.zeros((1, LANES - RL_E - N_EXPERTS), F32)], axis=1)
    return dict(nf=norm_ffn.reshape(1, d).astype(F32), whi=whi, wmid=wmid, br=br)


FFN_TM = 256
N_GATHER_SLOTS = 3


def _start_tile_gather(first_row_of, n_rows, src_hbm, dst, sem, priority_of):
    for r in range(n_rows):
        start = pl.multiple_of(first_row_of(r), TOK_TILE_ROWS)
        pltpu.make_async_copy(src_hbm.at[pl.ds(start, TOK_TILE_ROWS), :],
                              dst.at[pl.ds(r * TOK_TILE_ROWS, TOK_TILE_ROWS), :],
                              sem).start(priority=priority_of(r))


def _wait_tile_gather(n_rows, src_hbm, dst, sem):
    pltpu.make_async_copy(src_hbm.at[pl.ds(0, n_rows * TOK_TILE_ROWS), :], dst, sem).wait()


TAB_OFF, TAB_CNT, TAB_TILE_EXPERT, TAB_NVALID = 0, 1, 2, 3
TAB_LANES = 2 * LANES


def _ffn_kernel(tab_ref, pa_ref, pb_ref, xn_hbm, wg_hbm, wu_hbm, wd_hbm, ys_ref,
                src, xbuf, wbf, wst_in, wst_out, sem, wsem, *, n_tokens, tm):
    i = pl.program_id(0)
    n_valid = tab_ref[TAB_NVALID, 0]
    slot = lax.rem(i, N_GATHER_SLOTS)
    gather_priority = lambda r: r % 2

    def weight_copies(e):
        return (pltpu.make_async_copy(wg_hbm.at[e], wst_in.at[0], wsem.at[0]),
                pltpu.make_async_copy(wu_hbm.at[e], wst_in.at[1], wsem.at[1]),
                pltpu.make_async_copy(wd_hbm.at[e], wst_out, wsem.at[2]))

    @pl.when(i == 0)
    def _build_source_rows():
        for cp in weight_copies(tab_ref[TAB_TILE_EXPERT, 0]):
            cp.start()
        for e in range(N_EXPERTS):
            cnt_e = tab_ref[TAB_CNT, RL_E + e]
            first = tab_ref[TAB_OFF, RL_E + e] + cnt_e
            n_pad = lax.rem(tm - lax.rem(cnt_e, tm), tm)

            def pad_body(r, carry, first=first):
                src[first + r] = 0
                return carry
            lax.fori_loop(0, n_pad, pad_body, 0)

        def body(t, carry):
            first_row = t * TOK_TILE_ROWS
            src[pa_ref[t]] = first_row
            src[pb_ref[t]] = first_row
            return carry
        lax.fori_loop(0, n_tokens, body, 0, unroll=8)
        _start_tile_gather(lambda r: src[r], tm, xn_hbm, xbuf.at[0], sem.at[0], gather_priority)
        second = jnp.where(n_valid > 1, tm, 0)
        _start_tile_gather(lambda r: src[second + r], tm, xn_hbm, xbuf.at[1], sem.at[1], gather_priority)

    changed = jnp.logical_or(i == 0, tab_ref[TAB_TILE_EXPERT, i]
                             != tab_ref[TAB_TILE_EXPERT, jnp.maximum(i - 1, 0)])

    @pl.when(jnp.logical_and(changed, i < n_valid))
    def _cast_weights():
        e = tab_ref[TAB_TILE_EXPERT, i]
        for cp in weight_copies(e):
            cp.wait()
        wbf[0] = wst_in[0].astype(BF16)
        wbf[1] = wst_in[1].astype(BF16)
        wbf[2] = wst_out[...].astype(BF16)
        nxt = i + lax.div(tab_ref[TAB_CNT, RL_E + e] + (tm - 1), tm)

        @pl.when(nxt < n_valid)
        def _prefetch():
            for cp in weight_copies(tab_ref[TAB_TILE_EXPERT, jnp.minimum(nxt, pl.num_programs(0) - 1)]):
                cp.start()

    @pl.when(i < n_valid)
    def _compute():
        _wait_tile_gather(tm, xn_hbm, xbuf.at[slot], sem.at[slot])
        x = _load_token_tiles(xbuf.at[slot], tm).astype(BF16)
        hg = _dot(x, wbf[0])
        hu = _dot(x, wbf[1])
        y = _dot((_silu(hg) * hu).astype(BF16), wbf[2])
        base = jnp.where(i + 2 < n_valid, i + 2, 0) * tm
        ahead = lax.rem(i + 2, N_GATHER_SLOTS)
        _start_tile_gather(lambda r: src[base + r], tm, xn_hbm, xbuf.at[ahead], sem.at[ahead],
                           gather_priority)
        _store_token_tiles(ys_ref, y)

    @pl.when(i == n_valid - 1)
    def _drain():
        for k in (1, 2):
            s = lax.rem(i + k, N_GATHER_SLOTS)
            _wait_tile_gather(tm, xn_hbm, xbuf.at[s], sem.at[s])

    @pl.when(i >= n_valid)
    def _pad():
        ys_ref[...] = jnp.zeros_like(ys_ref)


def _expert_ffn(xn_tiles, tab, pos_a, pos_b, n_tiles, wg, wu, wd):
    n = pos_a.shape[0]
    tm = FFN_TM
    d, dff = wg.shape[1], wg.shape[2]
    rows = tm * TOK_TILE_ROWS
    assert d == dff
    grid_spec = pltpu.PrefetchScalarGridSpec(
        num_scalar_prefetch=3,
        grid=(n_tiles,),
        in_specs=[pl.BlockSpec(memory_space=pl.ANY)] * 4,
        out_specs=pl.BlockSpec((rows, LANES), lambda i, tab, pa, pb: (i, 0)),
        scratch_shapes=[pltpu.SMEM((n_tiles * tm,), jnp.int32),
                        pltpu.VMEM((N_GATHER_SLOTS, rows, LANES), F32),
                        pltpu.VMEM((3, d, dff), BF16),
                        pltpu.VMEM((2, d, dff), F32),
                        pltpu.VMEM((dff, d), F32),
                        pltpu.SemaphoreType.DMA((N_GATHER_SLOTS,)),
                        pltpu.SemaphoreType.DMA((3,))],
    )
    return pl.pallas_call(
        functools.partial(_ffn_kernel, n_tokens=n, tm=tm),
        out_shape=jax.ShapeDtypeStruct((n_tiles * rows, LANES), F32),
        grid_spec=grid_spec,
        compiler_params=pltpu.CompilerParams(
            dimension_semantics=("arbitrary",), vmem_limit_bytes=VMEM_LIMIT),
        name="expert_ffn",
    )(tab, pos_a, pos_b, xn_tiles, wg, wu, wd)


def _positions_kernel(info_ref, cnt_ref, pos_ref, tab_ref, *, tm, chunk):
    lane_i = lax.broadcasted_iota(jnp.int32, (1, LANES), 1)
    lane = lane_i.astype(F32)
    is_expert = jnp.logical_and(lane_i >= RL_E, lane_i < RL_E + N_EXPERTS)
    cnt = jnp.where(is_expert, cnt_ref[...], 0.0)
    padded = jnp.floor((cnt + (tm - 1)) / tm) * tm
    ri = lax.broadcasted_iota(jnp.int32, (LANES, LANES), 0)
    ci = lax.broadcasted_iota(jnp.int32, (LANES, LANES), 1)
    before = jnp.where(ri < ci, 1.0, 0.0).astype(BF16)
    hi, mid, lo = _split3(jnp.broadcast_to(padded, (SUBLANES, LANES)))
    off = ((_dot(hi, before) + _dot(mid, before)) + _dot(lo, before))[0:1, :]
    pick = jnp.where(lax.broadcasted_iota(jnp.int32, (SUBLANES, LANES), 0) == lane_i, 1.0, 0.0).astype(BF16)

    total = jnp.sum(padded, axis=-1, keepdims=True)
    tile_row = lax.broadcasted_iota(jnp.int32, (TAB_LANES, 1), 0).astype(F32) * tm
    ends = off + padded
    done = jnp.logical_and(is_expert, ends <= jnp.minimum(tile_row, total - 1.0))
    te_col = jnp.sum(jnp.where(done, 1.0, 0.0), axis=-1, keepdims=True)
    te_rows = _dot_nt(pick, jnp.where(lane_i == 0, te_col, 0.0).astype(BF16))
    tab_ref[...] = jnp.zeros_like(tab_ref)
    tab_ref[TAB_OFF:TAB_OFF + 1, 0:LANES] = off.astype(jnp.int32)
    tab_ref[TAB_CNT:TAB_CNT + 1, 0:LANES] = cnt.astype(jnp.int32)
    tab_ref[TAB_TILE_EXPERT:TAB_TILE_EXPERT + 1, :] = te_rows[0:1, :].astype(jnp.int32)
    tab_ref[TAB_NVALID:TAB_NVALID + 1, 0:LANES] = jnp.broadcast_to(total / tm, (1, LANES)).astype(jnp.int32)

    n = info_ref.shape[0]
    for c0 in range(0, n, chunk):
        blk = info_ref[c0:c0 + chunk, :]
        lane_a = blk[:, R_EA:R_EA + 1] + RL_E
        lane_b = blk[:, R_EB:R_EB + 1] + RL_E
        pos_a = blk[:, R_RA:R_RA + 1] + jnp.sum(jnp.where(lane == lane_a, off, 0.0), axis=-1, keepdims=True)
        pos_b = blk[:, R_RB:R_RB + 1] + jnp.sum(jnp.where(lane == lane_b, off, 0.0), axis=-1, keepdims=True)
        z_hi, z_mid, z_lo = _split3(jnp.where(lane_i == 0, pos_a, jnp.where(lane_i == 1, pos_b, 0.0)))
        rows = (_dot_nt(pick, z_hi) + _dot_nt(pick, z_mid)) + _dot_nt(pick, z_lo)
        pos_ref[:, c0:c0 + chunk] = rows.astype(jnp.int32)


def _routing_tables(info, counts, n_tiles, tm):
    n = info.shape[0]
    groups = n // LANES
    assert n_tiles <= TAB_LANES
    chunk = LANES * max(g for g in range(1, 65) if groups % g == 0)
    pos, tab = pl.pallas_call(
        functools.partial(_positions_kernel, tm=tm, chunk=chunk),
        out_shape=(jax.ShapeDtypeStruct((SUBLANES, n), jnp.int32),
                   jax.ShapeDtypeStruct((SUBLANES, TAB_LANES), jnp.int32)),
        grid=(1,),
        in_specs=[pl.BlockSpec((n, LANES), lambda i: (0, 0)), pl.BlockSpec((1, LANES), lambda i: (0, 0))],
        out_specs=(pl.BlockSpec((SUBLANES, n), lambda i: (0, 0)),
                   pl.BlockSpec((SUBLANES, TAB_LANES), lambda i: (0, 0))),
        compiler_params=pltpu.CompilerParams(
            dimension_semantics=("arbitrary",), vmem_limit_bytes=VMEM_LIMIT),
        name="positions",
    )(info, counts)
    return tab, pos[0], pos[1]


def _combine_kernel(pa_ref, pb_ref, h_ref, info_ref, ys_hbm, nfin_ref, yp_ref, ysm_ref,
                    buf_a, buf_b, sem, *, n_prompt_tiles):
    i = pl.program_id(0)
    n_steps = pl.num_programs(0)
    tt = h_ref.shape[0]
    slot = lax.rem(i, N_GATHER_SLOTS)

    def start(tile, s):
        base = tile * tt
        _start_tile_gather(lambda r: pa_ref[base + r] * TOK_TILE_ROWS, tt, ys_hbm, buf_a.at[s],
                           sem.at[s], lambda r: 0)
        _start_tile_gather(lambda r: pb_ref[base + r] * TOK_TILE_ROWS, tt, ys_hbm, buf_b.at[s],
                           sem.at[s], lambda r: 1)

    @pl.when(i == 0)
    def _first():
        start(0, 0)
        start(lax.rem(1, n_steps), 1)

    _wait_tile_gather(tt, ys_hbm, buf_a.at[slot], sem.at[slot])
    _wait_tile_gather(tt, ys_hbm, buf_b.at[slot], sem.at[slot])
    info = info_ref[...]
    rows_a = _load_token_tiles(buf_a.at[slot], tt)
    rows_b = _load_token_tiles(buf_b.at[slot], tt)
    x = h_ref[...]
    start(lax.rem(i + 2, n_steps), lax.rem(i + 2, N_GATHER_SLOTS))
    h = x + info[:, R_GA:R_GA + 1] * rows_a + info[:, R_GB:R_GB + 1] * rows_b
    y = h * _rms_scale(h) * nfin_ref[...]

    @pl.when(i < n_prompt_tiles)
    def _prompt():
        yp_ref[...] = y

    @pl.when(i >= n_prompt_tiles)
    def _sample():
        ysm_ref[...] = y

    @pl.when(i == n_steps - 1)
    def _drain():
        for k in (1, 2):
            s = lax.rem(i + k, N_GATHER_SLOTS)
            _wait_tile_gather(tt, ys_hbm, buf_a.at[s], sem.at[s])
            _wait_tile_gather(tt, ys_hbm, buf_b.at[s], sem.at[s])


def _combine(hmid, info, ys, pos_a, pos_b, nfin, n_prompt):
    n, d = hmid.shape
    tt = CHUNK
    n_prompt_tiles = n_prompt // tt
    n_sample = n - n_prompt
    grid_spec = pltpu.PrefetchScalarGridSpec(
        num_scalar_prefetch=2,
        grid=(n // tt,),
        in_specs=[pl.BlockSpec((tt, d), lambda i, pa, pb: (i, 0)),
                  pl.BlockSpec((tt, LANES), lambda i, pa, pb: (i, 0)),
                  pl.BlockSpec(memory_space=pl.ANY),
                  pl.BlockSpec((1, d), lambda i, pa, pb: (0, 0))],
        out_specs=(pl.BlockSpec((tt, d), lambda i, pa, pb: (jnp.minimum(i, n_prompt_tiles - 1), 0)),
                   pl.BlockSpec((tt, d), lambda i, pa, pb: (jnp.maximum(i - n_prompt_tiles, 0), 0))),
        scratch_shapes=[pltpu.VMEM((N_GATHER_SLOTS, tt * TOK_TILE_ROWS, LANES), F32),
                        pltpu.VMEM((N_GATHER_SLOTS, tt * TOK_TILE_ROWS, LANES), F32),
                        pltpu.SemaphoreType.DMA((N_GATHER_SLOTS,))],
    )
    return pl.pallas_call(
        functools.partial(_combine_kernel, n_prompt_tiles=n_prompt_tiles),
        out_shape=(jax.ShapeDtypeStruct((n_prompt, d), F32),
                   jax.ShapeDtypeStruct((n_sample, d), F32)),
        grid_spec=grid_spec,
        compiler_params=pltpu.CompilerParams(
            dimension_semantics=("arbitrary",), vmem_limit_bytes=VMEM_LIMIT),
        name="combine",
    )(pos_a, pos_b, hmid, info, ys, nfin)


def _moe_and_final_norm(hmid, n_prompt, rp, wg, wu, wd, nfin):
    n = hmid.shape[0]
    tm = FFN_TM
    n_tiles = (2 * n + N_EXPERTS * (tm - 1)) // tm
    xn, info, counts = _router(hmid, rp)
    tab, pos_a, pos_b = _routing_tables(info, counts, n_tiles, tm)
    ys = _expert_ffn(xn, tab, pos_a, pos_b, n_tiles, wg, wu, wd)
    return _combine(hmid, info, ys, pos_a, pos_b, nfin, n_prompt)


def kernel(x_prompt, x_sample, state_mlstm_C, state_mlstm_n, state_mlstm_m, state_mlstm_conv, state_ssm, state_ssm_conv, meta_tokens, norm_mix, w_in, conv_a_w, conv_a_b, w_q, w_k, w_v, b_i, b_f, norm_a, conv_b_w, conv_b_b, dt_bias, a_log, d_skip, norm_b, w_out, norm_ffn, w_r1, b_r1, w_r2, b_r2, w_gate, w_up, w_down, norm_final):
    bsz, seq, d = x_prompt.shape
    nb = x_sample.shape[0]
    d_a = H_A * DH_A
    conv_b = H_B * HD_B + 2 * G_B * N_STATE
    assert w_in.shape[0] == 1 and x_sample.shape[1] == 1 and seq % CHUNK == 0 and nb == CHUNK
    mp = _prep_mixer_params(norm_mix, w_in, conv_a_w, conv_a_b, w_q, w_k, w_v, b_i, b_f, norm_a,
                            conv_b_w, conv_b_b, dt_bias, a_log, d_skip, norm_b, w_out)
    rp = _prep_router_params(norm_ffn, w_r1, b_r1, w_r2, b_r2)
    xmeta = jnp.concatenate([jnp.zeros((CHUNK - N_META, d), F32), meta_tokens.astype(F32)], 0)

    hmid, p_c, p_n, p_m, p_ca, p_s, p_cb = _prompt_mixer(x_prompt.astype(F32), xmeta, mp, nb)
    m0 = jnp.pad(state_mlstm_m.reshape(nb, H_A).astype(F32), ((0, 0), (0, LANES - H_A)))
    hmid, s_c, s_n, s_m, s_ca, s_s, s_cb = _sample_mixer(
        x_sample.reshape(nb, d).astype(F32),
        state_mlstm_C.reshape(nb, H_A, DH_A, DH_A).astype(F32),
        state_mlstm_n.reshape(nb, d_a).astype(F32),
        m0,
        state_mlstm_conv.reshape(nb, (CONV_W - 1) * d_a).astype(F32),
        state_ssm.reshape(nb, H_B // 2, 2 * HD_B, N_STATE).astype(F32),
        state_ssm_conv.reshape(nb, (CONV_W - 1) * conv_b).astype(F32),
        mp, hmid, bsz * seq)

    wshape = w_gate.shape[1:]
    y_p, y_s = _moe_and_final_norm(
        hmid, bsz * seq, rp, w_gate.reshape(wshape).astype(F32), w_up.reshape(wshape).astype(F32),
        w_down.reshape(w_down.shape[1:]).astype(F32), norm_final.reshape(1, d).astype(F32))

    return (y_p.reshape(bsz, seq, d), y_s.reshape(nb, 1, d),
            p_c.reshape(1, bsz, H_A, DH_A, DH_A), p_n.reshape(1, bsz, H_A, DH_A),
            p_m[:, 0, :H_A].reshape(1, bsz, H_A), p_ca.reshape(1, bsz, CONV_W - 1, d_a),
            p_s.reshape(1, bsz, H_B, HD_B, N_STATE), p_cb.reshape(1, bsz, CONV_W - 1, conv_b),
            s_c.reshape(1, nb, H_A, DH_A, DH_A), s_n.reshape(1, nb, H_A, DH_A),
            s_m[:, :H_A].reshape(1, nb, H_A), s_ca.reshape(1, nb, CONV_W - 1, d_a),
            s_s.reshape(1, nb, H_B, HD_B, N_STATE), s_cb.reshape(1, nb, CONV_W - 1, conv_b))
```
